```python
import math
import jax, jax.numpy as jnp
from jax import lax
import numpy as np

D_MODEL = 1024
BATCH = 8
SEQ = 8192
DEPTH = 2

MIX_WIDTH = D_MODEL
MLA_WIDTH = D_MODEL // 2
SGU_WIDTH = D_MODEL // 4
POOL_WIDTH = D_MODEL // 4

MLA_HEADS = 4
V_HEAD = MLA_WIDTH // MLA_HEADS
QK_NOPE = 64
QK_ROPE = 32
QK_HEAD = QK_NOPE + QK_ROPE
Q_LORA = D_MODEL // 4
KV_LORA = D_MODEL // 8
ROPE_THETA = 10000.0
Q_BLOCK = 128

SGU_HEADS = 4
SGU_HEAD_DIM = SGU_WIDTH // SGU_HEADS
CHUNK = 128

POOL_WINDOWS = (2, 4, 8, 16)
POOL_GROUPS = len(POOL_WINDOWS)
POOL_GROUP_DIM = POOL_WIDTH // POOL_GROUPS

IN_WIDTH = Q_LORA + KV_LORA + QK_ROPE + 2 * SGU_WIDTH + POOL_WIDTH
FFN_HIDDEN = -(-8 * D_MODEL // (3 * 256)) * 256
EPS = 1e-6

kernel_name = "hybrid_mla_sgu_pool_block"


def rms_norm(x, g):
    xf = x.astype(jnp.float32)
    y = xf * lax.rsqrt(jnp.mean(xf * xf, axis=-1, keepdims=True) + EPS)
    return (y * g.astype(jnp.float32)).astype(x.dtype)


def apply_rope(x, positions):
    half = x.shape[-1] // 2
    inv_freq = 1.0 / (ROPE_THETA ** (jnp.arange(half, dtype=jnp.float32) / half))
    ang = positions.astype(jnp.float32)[:, :, None, None] * inv_freq
    cos, sin = jnp.cos(ang), jnp.sin(ang)
    xf = x.astype(jnp.float32)
    x1, x2 = xf[..., :half], xf[..., half:]
    return jnp.concatenate([x1 * cos - x2 * sin, x2 * cos + x1 * sin], axis=-1).astype(x.dtype)


def mla_mixer(q_lat, kv_lat, k_rope, positions, g_q_lat, w_q_up, g_kv_lat, w_kv_up, g_q_head, g_k_head):
    B, S, _ = q_lat.shape
    q = (rms_norm(q_lat, g_q_lat) @ w_q_up).reshape(B, S, MLA_HEADS, QK_HEAD)
    kv = (rms_norm(kv_lat, g_kv_lat) @ w_kv_up).reshape(B, S, MLA_HEADS, QK_NOPE + V_HEAD)
    k_nope, v = kv[..., :QK_NOPE], kv[..., QK_NOPE:]
    k_pe = jnp.broadcast_to(k_rope[:, :, None, :], (B, S, MLA_HEADS, QK_ROPE))
    k = jnp.concatenate([k_nope, k_pe], axis=-1)
    q = rms_norm(q, g_q_head)
    k = rms_norm(k, g_k_head)
    q = jnp.concatenate([q[..., :QK_NOPE], apply_rope(q[..., QK_NOPE:], positions)], axis=-1)
    k = jnp.concatenate([k[..., :QK_NOPE], apply_rope(k[..., QK_NOPE:], positions)], axis=-1)

    n_blocks = S // Q_BLOCK
    scale = 1.0 / math.sqrt(QK_HEAD)
    qb = q.reshape(B, n_blocks, Q_BLOCK, MLA_HEADS, QK_HEAD).transpose(1, 0, 2, 3, 4)
    kpos = jnp.arange(S)

    def attend_block(args):
        qi, bi = args
        s = jnp.einsum('bqhd,bkhd->bhqk', qi, k).astype(jnp.float32) * scale
        qpos = bi * Q_BLOCK + jnp.arange(Q_BLOCK)
        causal = kpos[None, :] <= qpos[:, None]
        s = jnp.where(causal[None, None], s, jnp.finfo(jnp.float32).min)
        p = jax.nn.softmax(s, axis=-1)
        return jnp.einsum('bhqk,bkhd->bqhd', p.astype(v.dtype), v)

    o = lax.map(attend_block, (qb, jnp.arange(n_blocks)))
    return o.transpose(1, 0, 2, 3, 4).reshape(B, S, MLA_WIDTH)


def sgu_mixer(uv, g_v, w_spatial, b_spatial):
    B, S, _ = uv.shape
    u, v = uv[..., :SGU_WIDTH], uv[..., SGU_WIDTH:]
    v = rms_norm(v, g_v)
    vc = v.reshape(B, S // CHUNK, CHUNK, SGU_HEADS, SGU_HEAD_DIM)
    w = w_spatial * jnp.tril(jnp.ones((CHUNK, CHUNK), dtype=w_spatial.dtype))
    zc = jnp.einsum('hts,bcshd->bcthd', w, vc) + b_spatial.T[None, None, :, :, None]
    return u * zc.reshape(B, S, SGU_WIDTH)


def pool_mixer(p, w_pool, pool_scale):
    B, S, _ = p.shape
    pf = p.astype(jnp.float32).reshape(B, S, POOL_GROUPS, POOL_GROUP_DIM)
    t1 = jnp.arange(1, S + 1, dtype=jnp.float32)
    outs = []
    for g, win in enumerate(POOL_WINDOWS):
        xg = pf[:, :, g]
        cs = jnp.cumsum(xg, axis=1)
        cs_shift = jnp.pad(cs, ((0, 0), (win, 0), (0, 0)))[:, :S]
        count = jnp.minimum(t1, float(win))[None, :, None]
        outs.append((cs - cs_shift) / count - xg)
    m = jnp.stack(outs, axis=2).astype(p.dtype)
    y = jnp.einsum('bsgc,gcd->bsgd', m, w_pool).reshape(B, S, POOL_WIDTH)
    return y * pool_scale


def _fwd_setup_inputs(seed: int = 0) -> dict:
    key = jax.random.key(seed)
    ks = jax.random.split(key, 24)
    f32 = jnp.float32

    def dense(k, shape, fan_in):
        return jax.random.normal(k, shape, f32) * fan_in ** -0.5

    def gain(k, shape):
        return 1.0 + 0.01 * jax.random.normal(k, shape, f32)

    x = jax.random.normal(ks[0], (BATCH, SEQ, D_MODEL), f32)
    start = jax.random.randint(ks[1], (BATCH, 1), 0, 4096, dtype=jnp.int32)
    positions = start + jnp.arange(SEQ, dtype=jnp.int32)[None, :]
    return {
        "x": x,
        "positions": positions,
        "g_mix_norm": gain(ks[2], (DEPTH, D_MODEL)),
        "w_in": dense(ks[3], (DEPTH, D_MODEL, IN_WIDTH), D_MODEL),
        "g_q_lat": gain(ks[4], (DEPTH, Q_LORA)),
        "w_q_up": dense(ks[5], (DEPTH, Q_LORA, MLA_HEADS * QK_HEAD), Q_LORA),
        "g_kv_lat": gain(ks[6], (DEPTH, KV_LORA)),
        "w_kv_up": dense(ks[7], (DEPTH, KV_LORA, MLA_HEADS * (QK_NOPE + V_HEAD)), KV_LORA),
        "g_q_head": gain(ks[8], (DEPTH, QK_HEAD)),
        "g_k_head": gain(ks[9], (DEPTH, QK_HEAD)),
        "g_sgu_v": gain(ks[10], (DEPTH, SGU_WIDTH)),
        "w_spatial": dense(ks[11], (DEPTH, SGU_HEADS, CHUNK, CHUNK), CHUNK),
        "b_spatial": 1.0 + 0.01 * jax.random.normal(ks[12], (DEPTH, SGU_HEADS, CHUNK), f32),
        "w_pool": dense(ks[13], (DEPTH, POOL_GROUPS, POOL_GROUP_DIM, POOL_GROUP_DIM), POOL_GROUP_DIM),
        "pool_scale": 1.0 + 0.1 * jax.random.normal(ks[14], (DEPTH, POOL_WIDTH), f32),
        "g_out_mla": gain(ks[15], (DEPTH, MLA_WIDTH)),
        "g_out_sgu": gain(ks[16], (DEPTH, SGU_WIDTH)),
        "g_out_pool": gain(ks[17], (DEPTH, POOL_WIDTH)),
        "w_out": dense(ks[18], (DEPTH, MIX_WIDTH, D_MODEL), MIX_WIDTH),
        "g_ffn_norm": gain(ks[19], (DEPTH, D_MODEL)),
        "w_gate": dense(ks[20], (DEPTH, D_MODEL, FFN_HIDDEN), D_MODEL),
        "w_up": dense(ks[21], (DEPTH, D_MODEL, FFN_HIDDEN), D_MODEL),
        "w_down": dense(ks[22], (DEPTH, FFN_HIDDEN, D_MODEL), FFN_HIDDEN),
    }


def _fwd_reference(x, positions, g_mix_norm, w_in, g_q_lat, w_q_up, g_kv_lat, w_kv_up, g_q_head, g_k_head,
              g_sgu_v, w_spatial, b_spatial, w_pool, pool_scale, g_out_mla, g_out_sgu, g_out_pool,
              w_out, g_ffn_norm, w_gate, w_up, w_down):
    o1 = Q_LORA
    o2 = o1 + KV_LORA
    o3 = o2 + QK_ROPE
    o4 = o3 + 2 * SGU_WIDTH
    for l in range(DEPTH):
        h = rms_norm(x, g_mix_norm[l])
        z = h @ w_in[l]
        q_lat, kv_lat, k_rope = z[..., :o1], z[..., o1:o2], z[..., o2:o3]
        uv, pin = z[..., o3:o4], z[..., o4:]
        a = mla_mixer(q_lat, kv_lat, k_rope, positions, g_q_lat[l], w_q_up[l], g_kv_lat[l],
                      w_kv_up[l], g_q_head[l], g_k_head[l])
        gm = sgu_mixer(uv, g_sgu_v[l], w_spatial[l], b_spatial[l])
        po = pool_mixer(pin, w_pool[l], pool_scale[l])
        mix = jnp.concatenate([rms_norm(a, g_out_mla[l]), rms_norm(gm, g_out_sgu[l]),
                               rms_norm(po, g_out_pool[l])], axis=-1)
        x = x + mix @ w_out[l]
        h = rms_norm(x, g_ffn_norm[l])
        x = x + (jax.nn.silu(h @ w_gate[l]) * (h @ w_up[l])) @ w_down[l]
    return x


import jax as _jax
import jax.numpy as _jnp

TWIN_FORMAT = 'train_step'
FWD_PARAMS = ['x', 'positions', 'g_mix_norm', 'w_in', 'g_q_lat', 'w_q_up', 'g_kv_lat', 'w_kv_up', 'g_q_head', 'g_k_head', 'g_sgu_v', 'w_spatial', 'b_spatial', 'w_pool', 'pool_scale', 'g_out_mla', 'g_out_sgu', 'g_out_pool', 'w_out', 'g_ffn_norm', 'w_gate', 'w_up', 'w_down']
TWIN_WEIGHTS = ['g_mix_norm', 'w_in', 'g_q_lat', 'w_q_up', 'g_kv_lat', 'w_kv_up', 'g_q_head', 'g_k_head', 'g_sgu_v', 'w_spatial', 'b_spatial', 'w_pool', 'pool_scale', 'g_out_mla', 'g_out_sgu', 'g_out_pool', 'w_out', 'g_ffn_norm', 'w_gate', 'w_up', 'w_down']
TWIN_DIFF_INPUT = 'x'
TWIN_INPUTS = ['x', 'positions', 'g_mix_norm', 'w_in', 'g_q_lat', 'w_q_up', 'g_kv_lat', 'w_kv_up', 'g_q_head', 'g_k_head', 'g_sgu_v', 'w_spatial', 'b_spatial', 'w_pool', 'pool_scale', 'g_out_mla', 'g_out_sgu', 'g_out_pool', 'w_out', 'g_ffn_norm', 'w_gate', 'w_up', 'w_down', 'loss_target', 'm_g_mix_norm', 'm_w_in', 'm_g_q_lat', 'm_w_q_up', 'm_g_kv_lat', 'm_w_kv_up', 'm_g_q_head', 'm_g_k_head', 'm_g_sgu_v', 'm_w_spatial', 'm_b_spatial', 'm_w_pool', 'm_pool_scale', 'm_g_out_mla', 'm_g_out_sgu', 'm_g_out_pool', 'm_w_out', 'm_g_ffn_norm', 'm_w_gate', 'm_w_up', 'm_w_down', 'v_g_mix_norm', 'v_w_in', 'v_g_q_lat', 'v_w_q_up', 'v_g_kv_lat', 'v_w_kv_up', 'v_g_q_head', 'v_g_k_head', 'v_g_sgu_v', 'v_w_spatial', 'v_b_spatial', 'v_w_pool', 'v_pool_scale', 'v_g_out_mla', 'v_g_out_sgu', 'v_g_out_pool', 'v_w_out', 'v_g_ffn_norm', 'v_w_gate', 'v_w_up', 'v_w_down']
TWIN_OUTPUTS = ['loss', 'grad_x', 'grad_g_mix_norm', 'grad_w_in', 'grad_g_q_lat', 'grad_w_q_up', 'grad_g_kv_lat', 'grad_w_kv_up', 'grad_g_q_head', 'grad_g_k_head', 'grad_g_sgu_v', 'grad_w_spatial', 'grad_b_spatial', 'grad_w_pool', 'grad_pool_scale', 'grad_g_out_mla', 'grad_g_out_sgu', 'grad_g_out_pool', 'grad_w_out', 'grad_g_ffn_norm', 'grad_w_gate', 'grad_w_up', 'grad_w_down', 'delta_g_mix_norm', 'delta_w_in', 'delta_g_q_lat', 'delta_w_q_up', 'delta_g_kv_lat', 'delta_w_kv_up', 'delta_g_q_head', 'delta_g_k_head', 'delta_g_sgu_v', 'delta_w_spatial', 'delta_b_spatial', 'delta_w_pool', 'delta_pool_scale', 'delta_g_out_mla', 'delta_g_out_sgu', 'delta_g_out_pool', 'delta_w_out', 'delta_g_ffn_norm', 'delta_w_gate', 'delta_w_up', 'delta_w_down', 'new_m_g_mix_norm', 'new_m_w_in', 'new_m_g_q_lat', 'new_m_w_q_up', 'new_m_g_kv_lat', 'new_m_w_kv_up', 'new_m_g_q_head', 'new_m_g_k_head', 'new_m_g_sgu_v', 'new_m_w_spatial', 'new_m_b_spatial', 'new_m_w_pool', 'new_m_pool_scale', 'new_m_g_out_mla', 'new_m_g_out_sgu', 'new_m_g_out_pool', 'new_m_w_out', 'new_m_g_ffn_norm', 'new_m_w_gate', 'new_m_w_up', 'new_m_w_down', 'new_v_g_mix_norm', 'new_v_w_in', 'new_v_g_q_lat', 'new_v_w_q_up', 'new_v_g_kv_lat', 'new_v_w_kv_up', 'new_v_g_q_head', 'new_v_g_k_head', 'new_v_g_sgu_v', 'new_v_w_spatial', 'new_v_b_spatial', 'new_v_w_pool', 'new_v_pool_scale', 'new_v_g_out_mla', 'new_v_g_out_sgu', 'new_v_g_out_pool', 'new_v_w_out', 'new_v_g_ffn_norm', 'new_v_w_gate', 'new_v_w_up', 'new_v_w_down']
TWIN_LEAF_KINDS = {'loss': 'loss', 'grad_x': 'grad_x', 'grad_g_mix_norm': 'grad_w', 'grad_w_in': 'grad_w', 'grad_g_q_lat': 'grad_w', 'grad_w_q_up': 'grad_w', 'grad_g_kv_lat': 'grad_w', 'grad_w_kv_up': 'grad_w', 'grad_g_q_head': 'grad_w', 'grad_g_k_head': 'grad_w', 'grad_g_sgu_v': 'grad_w', 'grad_w_spatial': 'grad_w', 'grad_b_spatial': 'grad_w', 'grad_w_pool': 'grad_w', 'grad_pool_scale': 'grad_w', 'grad_g_out_mla': 'grad_w', 'grad_g_out_sgu': 'grad_w', 'grad_g_out_pool': 'grad_w', 'grad_w_out': 'grad_w', 'grad_g_ffn_norm': 'grad_w', 'grad_w_gate': 'grad_w', 'grad_w_up': 'grad_w', 'grad_w_down': 'grad_w', 'delta_g_mix_norm': 'delta_w', 'delta_w_in': 'delta_w', 'delta_g_q_lat': 'delta_w', 'delta_w_q_up': 'delta_w', 'delta_g_kv_lat': 'delta_w', 'delta_w_kv_up': 'delta_w', 'delta_g_q_head': 'delta_w', 'delta_g_k_head': 'delta_w', 'delta_g_sgu_v': 'delta_w', 'delta_w_spatial': 'delta_w', 'delta_b_spatial': 'delta_w', 'delta_w_pool': 'delta_w', 'delta_pool_scale': 'delta_w', 'delta_g_out_mla': 'delta_w', 'delta_g_out_sgu': 'delta_w', 'delta_g_out_pool': 'delta_w', 'delta_w_out': 'delta_w', 'delta_g_ffn_norm': 'delta_w', 'delta_w_gate': 'delta_w', 'delta_w_up': 'delta_w', 'delta_w_down': 'delta_w', 'new_m_g_mix_norm': 'new_m', 'new_m_w_in': 'new_m', 'new_m_g_q_lat': 'new_m', 'new_m_w_q_up': 'new_m', 'new_m_g_kv_lat': 'new_m', 'new_m_w_kv_up': 'new_m', 'new_m_g_q_head': 'new_m', 'new_m_g_k_head': 'new_m', 'new_m_g_sgu_v': 'new_m', 'new_m_w_spatial': 'new_m', 'new_m_b_spatial': 'new_m', 'new_m_w_pool': 'new_m', 'new_m_pool_scale': 'new_m', 'new_m_g_out_mla': 'new_m', 'new_m_g_out_sgu': 'new_m', 'new_m_g_out_pool': 'new_m', 'new_m_w_out': 'new_m', 'new_m_g_ffn_norm': 'new_m', 'new_m_w_gate': 'new_m', 'new_m_w_up': 'new_m', 'new_m_w_down': 'new_m', 'new_v_g_mix_norm': 'new_v', 'new_v_w_in': 'new_v', 'new_v_g_q_lat': 'new_v', 'new_v_w_q_up': 'new_v', 'new_v_g_kv_lat': 'new_v', 'new_v_w_kv_up': 'new_v', 'new_v_g_q_head': 'new_v', 'new_v_g_k_head': 'new_v', 'new_v_g_sgu_v': 'new_v', 'new_v_w_spatial': 'new_v', 'new_v_b_spatial': 'new_v', 'new_v_w_pool': 'new_v', 'new_v_pool_scale': 'new_v', 'new_v_g_out_mla': 'new_v', 'new_v_g_out_sgu': 'new_v', 'new_v_g_out_pool': 'new_v', 'new_v_w_out': 'new_v', 'new_v_g_ffn_norm': 'new_v', 'new_v_w_gate': 'new_v', 'new_v_w_up': 'new_v', 'new_v_w_down': 'new_v'}


def _forward(args):
    return _fwd_reference(*[args[k] for k in FWD_PARAMS])


def _output_shape():
    def fwd():
        inp = _fwd_setup_inputs(0)
        return _fwd_reference(*[inp[k] for k in FWD_PARAMS])
    out = _jax.eval_shape(fwd)
    return out.shape, out.dtype

N_MICROBATCH = 1
ADAM_LR = 0.001
ADAM_B1 = 0.9
ADAM_B2 = 0.999
ADAM_EPS = 1e-08
ADAM_WD = 0.01
ADAM_STEP = 10
PER_EXAMPLE_BATCH_AXIS = {'x': 0, 'positions': 0, 'loss_target': 0}
SHARED_INPUTS = []
_WEIGHT_DTYPES = {'g_mix_norm': _jnp.float32, 'w_in': _jnp.float32, 'g_q_lat': _jnp.float32, 'w_q_up': _jnp.float32, 'g_kv_lat': _jnp.float32, 'w_kv_up': _jnp.float32, 'g_q_head': _jnp.float32, 'g_k_head': _jnp.float32, 'g_sgu_v': _jnp.float32, 'w_spatial': _jnp.float32, 'b_spatial': _jnp.float32, 'w_pool': _jnp.float32, 'pool_scale': _jnp.float32, 'g_out_mla': _jnp.float32, 'g_out_sgu': _jnp.float32, 'g_out_pool': _jnp.float32, 'w_out': _jnp.float32, 'g_ffn_norm': _jnp.float32, 'w_gate': _jnp.float32, 'w_up': _jnp.float32, 'w_down': _jnp.float32}
MOMENT_SCALE = {'g_mix_norm': 8.185659e+00, 'w_in': 7.971197e+00, 'g_q_lat': 3.047113e+00, 'w_q_up': 2.477230e+00, 'g_kv_lat': 3.248025e+01, 'w_kv_up': 1.084788e+01, 'g_q_head': 8.900728e+00, 'g_k_head': 9.058494e+00, 'g_sgu_v': 8.786746e-01, 'w_spatial': 3.377089e-01, 'b_spatial': 5.290256e-01, 'w_pool': 3.410699e+00, 'pool_scale': 4.198700e+00, 'g_out_mla': 6.813053e+01, 'g_out_sgu': 6.454035e+01, 'g_out_pool': 6.733345e+01, 'w_out': 9.491567e+00, 'g_ffn_norm': 4.922655e+01, 'w_gate': 6.688937e-01, 'w_up': 9.014210e-01, 'w_down': 1.429989e+00}


def _to_microbatches(a, axis):
    t = _jnp.moveaxis(a, axis, 0)
    t = t.reshape((N_MICROBATCH, t.shape[0] // N_MICROBATCH) + t.shape[1:])
    return _jnp.moveaxis(t, 1, axis + 1)


def setup_inputs(seed: int = 0) -> dict:
    inp = _fwd_setup_inputs(seed)
    key = _jax.random.fold_in(_jax.random.key(seed), 7919)
    shape, _ = _output_shape()
    out = dict(inp)
    out["loss_target"] = _jax.random.normal(_jax.random.fold_in(key, 0), shape, _jnp.float32)
    for i, name in enumerate(TWIN_WEIGHTS):
        w = inp[name].astype(_jnp.float32)
        if MOMENT_SCALE is None:
            s = _jnp.sqrt(_jnp.mean(_jnp.square(w)) + 1e-30)
        else:
            s = MOMENT_SCALE[name]
        km, kv = _jax.random.split(_jax.random.fold_in(key, i + 1))
        out[name] = w
        out["m_" + name] = s * _jax.random.normal(km, w.shape, _jnp.float32)
        out["v_" + name] = (s * s) * _jax.random.uniform(kv, w.shape, _jnp.float32, 0.5, 1.5)
    if N_MICROBATCH > 1:
        for name, axis in PER_EXAMPLE_BATCH_AXIS.items():
            out[name] = _to_microbatches(out[name], axis)
    return {'x': out['x'], 'positions': out['positions'], 'g_mix_norm': out['g_mix_norm'], 'w_in': out['w_in'], 'g_q_lat': out['g_q_lat'], 'w_q_up': out['w_q_up'], 'g_kv_lat': out['g_kv_lat'], 'w_kv_up': out['w_kv_up'], 'g_q_head': out['g_q_head'], 'g_k_head': out['g_k_head'], 'g_sgu_v': out['g_sgu_v'], 'w_spatial': out['w_spatial'], 'b_spatial': out['b_spatial'], 'w_pool': out['w_pool'], 'pool_scale': out['pool_scale'], 'g_out_mla': out['g_out_mla'], 'g_out_sgu': out['g_out_sgu'], 'g_out_pool': out['g_out_pool'], 'w_out': out['w_out'], 'g_ffn_norm': out['g_ffn_norm'], 'w_gate': out['w_gate'], 'w_up': out['w_up'], 'w_down': out['w_down'], 'loss_target': out['loss_target'], 'm_g_mix_norm': out['m_g_mix_norm'], 'm_w_in': out['m_w_in'], 'm_g_q_lat': out['m_g_q_lat'], 'm_w_q_up': out['m_w_q_up'], 'm_g_kv_lat': out['m_g_kv_lat'], 'm_w_kv_up': out['m_w_kv_up'], 'm_g_q_head': out['m_g_q_head'], 'm_g_k_head': out['m_g_k_head'], 'm_g_sgu_v': out['m_g_sgu_v'], 'm_w_spatial': out['m_w_spatial'], 'm_b_spatial': out['m_b_spatial'], 'm_w_pool': out['m_w_pool'], 'm_pool_scale': out['m_pool_scale'], 'm_g_out_mla': out['m_g_out_mla'], 'm_g_out_sgu': out['m_g_out_sgu'], 'm_g_out_pool': out['m_g_out_pool'], 'm_w_out': out['m_w_out'], 'm_g_ffn_norm': out['m_g_ffn_norm'], 'm_w_gate': out['m_w_gate'], 'm_w_up': out['m_w_up'], 'm_w_down': out['m_w_down'], 'v_g_mix_norm': out['v_g_mix_norm'], 'v_w_in': out['v_w_in'], 'v_g_q_lat': out['v_g_q_lat'], 'v_w_q_up': out['v_w_q_up'], 'v_g_kv_lat': out['v_g_kv_lat'], 'v_w_kv_up': out['v_w_kv_up'], 'v_g_q_head': out['v_g_q_head'], 'v_g_k_head': out['v_g_k_head'], 'v_g_sgu_v': out['v_g_sgu_v'], 'v_w_spatial': out['v_w_spatial'], 'v_b_spatial': out['v_b_spatial'], 'v_w_pool': out['v_w_pool'], 'v_pool_scale': out['v_pool_scale'], 'v_g_out_mla': out['v_g_out_mla'], 'v_g_out_sgu': out['v_g_out_sgu'], 'v_g_out_pool': out['v_g_out_pool'], 'v_w_out': out['v_w_out'], 'v_g_ffn_norm': out['v_g_ffn_norm'], 'v_w_gate': out['v_w_gate'], 'v_w_up': out['v_w_up'], 'v_w_down': out['v_w_down']}


def _loss(weights, diff, rest, loss_target):
    with _jax.named_scope("forward"):
        args = {**rest, TWIN_DIFF_INPUT: diff, **{k: w.astype(_WEIGHT_DTYPES[k]) for k, w in weights.items()}}
        y = _forward(args)
    with _jax.named_scope("loss_head"):
        err = _jnp.square(y.astype(_jnp.float32) - loss_target)
        return 0.5 * _jnp.sum(_jnp.mean(err, axis=-1)) if err.ndim else 0.5 * err


def _adamw(w, g, m, v):
    m = ADAM_B1 * m + (1.0 - ADAM_B1) * g
    v = ADAM_B2 * v + (1.0 - ADAM_B2) * _jnp.square(g)
    m_hat = m / (1.0 - ADAM_B1 ** ADAM_STEP)
    v_hat = v / (1.0 - ADAM_B2 ** ADAM_STEP)
    delta = -ADAM_LR * (m_hat / (_jnp.sqrt(v_hat) + ADAM_EPS) + ADAM_WD * w)
    return delta, m, v


def reference(x, positions, g_mix_norm, w_in, g_q_lat, w_q_up, g_kv_lat, w_kv_up, g_q_head, g_k_head, g_sgu_v, w_spatial, b_spatial, w_pool, pool_scale, g_out_mla, g_out_sgu, g_out_pool, w_out, g_ffn_norm, w_gate, w_up, w_down, loss_target, m_g_mix_norm, m_w_in, m_g_q_lat, m_w_q_up, m_g_kv_lat, m_w_kv_up, m_g_q_head, m_g_k_head, m_g_sgu_v, m_w_spatial, m_b_spatial, m_w_pool, m_pool_scale, m_g_out_mla, m_g_out_sgu, m_g_out_pool, m_w_out, m_g_ffn_norm, m_w_gate, m_w_up, m_w_down, v_g_mix_norm, v_w_in, v_g_q_lat, v_w_q_up, v_g_kv_lat, v_w_kv_up, v_g_q_head, v_g_k_head, v_g_sgu_v, v_w_spatial, v_b_spatial, v_w_pool, v_pool_scale, v_g_out_mla, v_g_out_sgu, v_g_out_pool, v_w_out, v_g_ffn_norm, v_w_gate, v_w_up, v_w_down):
    given = dict(x=x, positions=positions, g_mix_norm=g_mix_norm, w_in=w_in, g_q_lat=g_q_lat, w_q_up=w_q_up, g_kv_lat=g_kv_lat, w_kv_up=w_kv_up, g_q_head=g_q_head, g_k_head=g_k_head, g_sgu_v=g_sgu_v, w_spatial=w_spatial, b_spatial=b_spatial, w_pool=w_pool, pool_scale=pool_scale, g_out_mla=g_out_mla, g_out_sgu=g_out_sgu, g_out_pool=g_out_pool, w_out=w_out, g_ffn_norm=g_ffn_norm, w_gate=w_gate, w_up=w_up, w_down=w_down, loss_target=loss_target, m_g_mix_norm=m_g_mix_norm, m_w_in=m_w_in, m_g_q_lat=m_g_q_lat, m_w_q_up=m_w_q_up, m_g_kv_lat=m_g_kv_lat, m_w_kv_up=m_w_kv_up, m_g_q_head=m_g_q_head, m_g_k_head=m_g_k_head, m_g_sgu_v=m_g_sgu_v, m_w_spatial=m_w_spatial, m_b_spatial=m_b_spatial, m_w_pool=m_w_pool, m_pool_scale=m_pool_scale, m_g_out_mla=m_g_out_mla, m_g_out_sgu=m_g_out_sgu, m_g_out_pool=m_g_out_pool, m_w_out=m_w_out, m_g_ffn_norm=m_g_ffn_norm, m_w_gate=m_w_gate, m_w_up=m_w_up, m_w_down=m_w_down, v_g_mix_norm=v_g_mix_norm, v_w_in=v_w_in, v_g_q_lat=v_g_q_lat, v_w_q_up=v_w_q_up, v_g_kv_lat=v_g_kv_lat, v_w_kv_up=v_w_kv_up, v_g_q_head=v_g_q_head, v_g_k_head=v_g_k_head, v_g_sgu_v=v_g_sgu_v, v_w_spatial=v_w_spatial, v_b_spatial=v_b_spatial, v_w_pool=v_w_pool, v_pool_scale=v_pool_scale, v_g_out_mla=v_g_out_mla, v_g_out_sgu=v_g_out_sgu, v_g_out_pool=v_g_out_pool, v_w_out=v_w_out, v_g_ffn_norm=v_g_ffn_norm, v_w_gate=v_w_gate, v_w_up=v_w_up, v_w_down=v_w_down)
    weights = {n: given[n] for n in TWIN_WEIGHTS}
    shared = {n: given[n] for n in SHARED_INPUTS}
    per_example = {n: given[n] for n in ['x', 'positions']}
    grad_fn = _jax.value_and_grad(_loss, argnums=(0, 1))

    def one_microbatch(ex, loss_target):
        ex = dict(ex)
        diff = ex.pop(TWIN_DIFF_INPUT)
        return grad_fn(weights, diff, {**shared, **ex}, loss_target)

    if N_MICROBATCH == 1:
        loss, (grad_w, grad_x) = one_microbatch(per_example, given["loss_target"])
    else:
        def body(carry, xs):
            loss_sum, grad_sum = carry
            l_k, (gw_k, gx_k) = one_microbatch(xs[0], xs[1])
            with _jax.named_scope("update"):
                return (loss_sum + l_k, _jax.tree.map(_jnp.add, grad_sum, gw_k)), gx_k

        init = (_jnp.zeros((), _jnp.float32), _jax.tree.map(_jnp.zeros_like, weights))
        (loss, grad_w), grad_x = _jax.lax.scan(body, init, (per_example, given["loss_target"]))
    with _jax.named_scope("update"):
        delta_w, new_m, new_v = {}, {}, {}
        for n in TWIN_WEIGHTS:
            delta_w[n], new_m[n], new_v[n] = _adamw(weights[n], grad_w[n], given["m_" + n], given["v_" + n])
    return (loss, grad_x, *[grad_w[n] for n in TWIN_WEIGHTS], *[delta_w[n] for n in TWIN_WEIGHTS],
            *[new_m[n] for n in TWIN_WEIGHTS], *[new_v[n] for n in TWIN_WEIGHTS])
```

```python
import functools
import math

import jax
import jax.numpy as jnp
from jax import lax
from jax.experimental import pallas as pl
from jax.experimental.pallas import tpu as pltpu

F32 = jnp.float32
BF16 = jnp.bfloat16

N_DEV = 8
DEPTH = 2
D_MODEL = 1024
HEADS = 4
HEAD_PAD = 128
QK_NOPE = 64
QK_ROPE = 32
QK_HEAD = QK_NOPE + QK_ROPE
V_HEAD = 128
Q_LORA = 256
KV_LORA = 128
SGU_WIDTH = 256
SGU_HEAD_DIM = 64
CHUNK = 128
POOL_WIDTH = 256
POOL_WINDOWS = (2, 4, 8, 16)
POOL_HALO = 16
MLA_WIDTH = 512
IN_WIDTH = 1184
Z_WIDTH = 1280
FFN_HIDDEN = 2816
FFN_CHUNK = 256
ROPE_THETA = 10000.0
EPS = 1e-6
ATTN_SCALE = 1.0 / math.sqrt(QK_HEAD)
NEG_BIG = -1e30

ADAM_LR = 0.001
ADAM_B1 = 0.9
ADAM_B2 = 0.999
ADAM_EPS = 1e-08
ADAM_WD = 0.01
ADAM_STEP = 10

VMEM_LIMIT = 56 * 1024 * 1024
ROW_TILE = 512
ATTN_TILE = 512
MESH = pl.DeviceIdType.MESH

WHOLE = pl.BlockSpec(memory_space=pltpu.VMEM)
ANY = pl.BlockSpec(memory_space=pl.ANY)


def _cparams(**kw):
    return pltpu.CompilerParams(vmem_limit_bytes=VMEM_LIMIT, **kw)


def _dot(a, b):
    return jnp.dot(a, b, preferred_element_type=F32)


def _dot_nt(a, b):
    return lax.dot_general(a, b, (((1,), (1,)), ((), ())), preferred_element_type=F32)


def _dot_tn(a, b):
    return lax.dot_general(a, b, (((0,), (0,)), ((), ())), preferred_element_type=F32)


def _rms(x, g, n):
    r = lax.rsqrt(jnp.sum(x * x, axis=-1, keepdims=True) * (1.0 / n) + EPS)
    return x * r * g, r


def _rms_bwd(x, r, g, dy, n):
    gdy = dy * g
    dx = r * gdy - x * (r * r * r) * (jnp.sum(x * gdy, axis=-1, keepdims=True) * (1.0 / n))
    dg = jnp.sum(dy * (x * r), axis=0, keepdims=True)
    return dx, dg


def _sigmoid(x):
    return 1.0 / (1.0 + jnp.exp(-x))


def _rope_tables(ang):
    lane = lax.broadcasted_iota(jnp.int32, ang.shape, 1)
    c, s = jnp.cos(ang), jnp.sin(ang)
    sa = jnp.where(lane < QK_NOPE + QK_ROPE // 2, -s, 0.0)
    sb = jnp.where(lane >= QK_NOPE + QK_ROPE // 2, s, 0.0)
    return c, sa, sb


def _rope(x, c, sa, sb):
    half = QK_ROPE // 2
    return x * c + pltpu.roll(x, HEAD_PAD - half, 1) * sa + pltpu.roll(x, half, 1) * sb


def _rope_bwd(dy, c, sa, sb):
    half = QK_ROPE // 2
    return dy * c + pltpu.roll(dy * sa, half, 1) + pltpu.roll(dy * sb, HEAD_PAD - half, 1)


def _head_masks(shape, width):
    lane = lax.broadcasted_iota(jnp.int32, shape, len(shape) - 1)
    return [(lane >= width * h) & (lane < width * (h + 1)) for h in range(HEADS)]


def _qkv_pre(z, gql, wq, gkvl, wkv):
    ql = z[:, 0:Q_LORA]
    kvl = z[:, Q_LORA:Q_LORA + KV_LORA]
    kr = z[:, Q_LORA + KV_LORA:Q_LORA + KV_LORA + HEAD_PAD]
    qn, rq = _rms(ql, gql, Q_LORA)
    kvn, rkv = _rms(kvl, gkvl, KV_LORA)
    qn = qn.astype(BF16)
    kvn = kvn.astype(BF16)
    q_up = _dot(qn, wq)
    kv_up = _dot(kvn, wkv)
    return ql, kvl, kr, qn, rq, kvn, rkv, q_up, kv_up


def mixin_fwd(x, ang, gmix, win, gql, wq, gkvl, wkv, gqh, gkh):
    T = x.shape[0]
    tm = min(ROW_TILE, T)

    def body(x_ref, ang_ref, gmix_ref, win_ref, gql_ref, wq_ref, gkvl_ref, wkv_ref, gqh_ref, gkh_ref,
             z_ref, q_ref, k_ref, v_ref):
        hn = _rms(x_ref[...], gmix_ref[...], D_MODEL)[0].astype(BF16)
        z = _dot(hn, win_ref[...])
        z_ref[...] = z
        _, _, kr, _, _, _, _, q_up, kv_up = _qkv_pre(z, gql_ref[...], wq_ref[...], gkvl_ref[...], wkv_ref[...])
        c, sa, sb = _rope_tables(ang_ref[...])
        for h in range(HEADS):
            lo = HEAD_PAD * h
            qh = _rms(q_up[:, lo:lo + HEAD_PAD], gqh_ref[...], QK_HEAD)[0]
            q_ref[h] = _rope(qh, c, sa, sb).astype(BF16)
            kh = _rms(kv_up[:, lo:lo + HEAD_PAD] + kr, gkh_ref[...], QK_HEAD)[0]
            k_ref[h] = _rope(kh, c, sa, sb).astype(BF16)
            v_ref[h] = kv_up[:, HEADS * HEAD_PAD + lo:HEADS * HEAD_PAD + lo + HEAD_PAD].astype(BF16)

    row = lambda w: pl.BlockSpec((tm, w), lambda i: (i, 0))
    head = pl.BlockSpec((HEADS, tm, HEAD_PAD), lambda i: (0, i, 0))
    return pl.pallas_call(
        body, grid=(T // tm,), name="mixin_fwd",
        in_specs=[row(D_MODEL), row(HEAD_PAD)] + [WHOLE] * 8,
        out_specs=[row(Z_WIDTH), head, head, head],
        out_shape=[jax.ShapeDtypeStruct((T, Z_WIDTH), F32)] + [jax.ShapeDtypeStruct((HEADS, T, HEAD_PAD), BF16)] * 3,
        compiler_params=_cparams(),
    )(x, ang, gmix, win, gql, wq, gkvl, wkv, gqh, gkh)


def attn_fwd(q, k, v):
    _, T, _ = q.shape
    tq = min(ATTN_TILE, T)
    nq = T // tq

    def body(q_ref, k_ref, v_ref, o_ref, lse_ref, m_s, l_s, acc_s):
        i, j = pl.program_id(1), pl.program_id(2)

        @pl.when(j == 0)
        def _():
            m_s[...] = jnp.full(m_s.shape, -jnp.inf, F32)
            l_s[...] = jnp.zeros(l_s.shape, F32)
            acc_s[...] = jnp.zeros(acc_s.shape, F32)

        def step(masked):
            s = _dot_nt(q_ref[...], k_ref[...]) * ATTN_SCALE
            if masked:
                rows = lax.broadcasted_iota(jnp.int32, s.shape, 0)
                cols = lax.broadcasted_iota(jnp.int32, s.shape, 1)
                s = jnp.where(cols <= rows, s, NEG_BIG)
            m_prev = m_s[...]
            m_new = jnp.maximum(m_prev, jnp.max(s, axis=1, keepdims=True))
            alpha = jnp.exp(m_prev - m_new)
            p = jnp.exp(s - m_new)
            l_s[...] = alpha * l_s[...] + jnp.sum(p, axis=1, keepdims=True)
            acc_s[...] = alpha * acc_s[...] + _dot(p.astype(BF16), v_ref[...])
            m_s[...] = m_new

        @pl.when(j < i)
        def _():
            step(False)

        @pl.when(j == i)
        def _():
            step(True)
            l = l_s[...]
            o_ref[...] = acc_s[...] / l
            lse_ref[...] = jnp.broadcast_to(m_s[...] + jnp.log(l), lse_ref.shape)

    qspec = pl.BlockSpec((None, tq, HEAD_PAD), lambda h, i, j: (h, i, 0))
    kspec = pl.BlockSpec((None, tq, HEAD_PAD), lambda h, i, j: (h, jnp.minimum(i, j), 0))
    return pl.pallas_call(
        body, grid=(HEADS, nq, nq), name="attn_fwd",
        in_specs=[qspec, kspec, kspec],
        out_specs=[pl.BlockSpec((tq, HEAD_PAD), lambda h, i, j: (i, h)), qspec],
        out_shape=[jax.ShapeDtypeStruct((T, MLA_WIDTH), F32), jax.ShapeDtypeStruct((HEADS, T, HEAD_PAD), F32)],
        scratch_shapes=[pltpu.VMEM((tq, 1), F32), pltpu.VMEM((tq, 1), F32), pltpu.VMEM((tq, HEAD_PAD), F32)],
        compiler_params=_cparams(),
    )(q, k, v)


def _sgu_fwd_chunk(vn_c, wcat, bz, masks):
    vstack = jnp.concatenate([jnp.where(mk, vn_c, 0.0).astype(BF16) for mk in masks], axis=0)
    return _dot(wcat, vstack) + bz


def _pool_counts(i, tm):
    pos1 = (i * tm + 1 + lax.broadcasted_iota(jnp.int32, (tm, POOL_WIDTH), 0)).astype(F32)
    lane = lax.broadcasted_iota(jnp.int32, (tm, POOL_WIDTH), 1)
    win = jnp.where(lane < 64, 2.0, jnp.where(lane < 128, 4.0, jnp.where(lane < 192, 8.0, 16.0)))
    return jnp.minimum(pos1, win), lane


def _by_group(lane, s2, s4, s8, s16):
    return jnp.where(lane < 64, s2, jnp.where(lane < 128, s4, jnp.where(lane < 192, s8, s16)))


def _pool_means(pin, halo, i, tm):
    s1 = jnp.concatenate([halo, pin], axis=0)
    s2 = s1 + pltpu.roll(s1, 1, 0)
    s4 = s2 + pltpu.roll(s2, 2, 0)
    s8 = s4 + pltpu.roll(s4, 4, 0)
    s16 = s8 + pltpu.roll(s8, 8, 0)
    cnt, lane = _pool_counts(i, tm)
    sel = _by_group(lane, s2[POOL_HALO:], s4[POOL_HALO:], s8[POOL_HALO:], s16[POOL_HALO:])
    return sel / cnt - pin


def _mixers_fwd_tile(i, tm, a, u, vs, pin, halo, gsgu, wcat, bz, wp, pscale, goa, gos, gop):
    vn, rv = _rms(vs, gsgu, SGU_WIDTH)
    masks = _head_masks((CHUNK, SGU_WIDTH), SGU_HEAD_DIM)
    zc = jnp.concatenate([_sgu_fwd_chunk(vn[CHUNK * c:CHUNK * (c + 1)], wcat, bz, masks) for c in range(tm // CHUNK)], axis=0)
    gm = u * zc
    halo = jnp.where(i > 0, halo, 0.0)
    m = _pool_means(pin, halo, i, tm).astype(BF16)
    yp_pre = _dot(m, wp)
    yp = yp_pre * pscale
    na, ra = _rms(a, goa, MLA_WIDTH)
    ng, rg = _rms(gm, gos, SGU_WIDTH)
    npo, rp = _rms(yp, gop, POOL_WIDTH)
    mix = jnp.concatenate([na, ng, npo], axis=1).astype(BF16)
    return vn, rv, zc, gm, m, yp_pre, yp, ra, rg, rp, mix


def _z_specs(tm):
    col = lambda c: pl.BlockSpec((tm, 256), lambda i: (i, c))
    halo = pl.BlockSpec((POOL_HALO, 256), lambda i: (jnp.maximum(i * (tm // POOL_HALO) - 1, 0), 4))
    return [col(2), col(3), col(4), halo]


def mixers_fwd(x, a, z, gsgu, wcat, bz, wp, pscale, goa, gos, gop, wout):
    T = x.shape[0]
    tm = min(ROW_TILE, T)

    def body(x_ref, a_ref, u_ref, vs_ref, pin_ref, halo_ref, gsgu_ref, wcat_ref, bz_ref, wp_ref, ps_ref,
             goa_ref, gos_ref, gop_ref, wout_ref, x1_ref):
        i = pl.program_id(0)
        mix = _mixers_fwd_tile(i, tm, a_ref[...], u_ref[...], vs_ref[...], pin_ref[...], halo_ref[...], gsgu_ref[...],
                               wcat_ref[...], bz_ref[...], wp_ref[...], ps_ref[...], goa_ref[...], gos_ref[...],
                               gop_ref[...])[-1]
        x1_ref[...] = x_ref[...] + _dot(mix, wout_ref[...])

    row = lambda w: pl.BlockSpec((tm, w), lambda i: (i, 0))
    return pl.pallas_call(
        body, grid=(T // tm,), name="mixers_fwd",
        in_specs=[row(D_MODEL), row(MLA_WIDTH)] + _z_specs(tm) + [WHOLE] * 9,
        out_specs=row(D_MODEL),
        out_shape=jax.ShapeDtypeStruct((T, D_MODEL), F32),
        compiler_params=_cparams(),
    )(x, a, z, z, z, z, gsgu, wcat, bz, wp, pscale, goa, gos, gop, wout)


def ffn_fwd(x1, gffn, wg, wu, wd):
    T = x1.shape[0]
    tm = min(ROW_TILE, T)

    def body(x1_ref, gffn_ref, wg_ref, wu_ref, wd_ref, x2_ref, gs_ref, us_ref):
        x1v = x1_ref[...]
        h2 = _rms(x1v, gffn_ref[...], D_MODEL)[0].astype(BF16)
        acc = x1v
        for c in range(FFN_HIDDEN // FFN_CHUNK):
            sl = slice(FFN_CHUNK * c, FFN_CHUNK * (c + 1))
            g = _dot(h2, wg_ref[:, sl])
            u = _dot(h2, wu_ref[:, sl])
            gs_ref[:, sl] = g.astype(BF16)
            us_ref[:, sl] = u.astype(BF16)
            act = (g * _sigmoid(g) * u).astype(BF16)
            acc = acc + _dot(act, wd_ref[sl, :])
        x2_ref[...] = acc

    row = lambda w: pl.BlockSpec((tm, w), lambda i: (i, 0))
    return pl.pallas_call(
        body, grid=(T // tm,), name="ffn_fwd",
        in_specs=[row(D_MODEL)] + [WHOLE] * 4,
        out_specs=[row(D_MODEL), row(FFN_HIDDEN), row(FFN_HIDDEN)],
        out_shape=[jax.ShapeDtypeStruct((T, D_MODEL), F32), jax.ShapeDtypeStruct((T, FFN_HIDDEN), BF16),
                   jax.ShapeDtypeStruct((T, FFN_HIDDEN), BF16)],
        compiler_params=_cparams(),
    )(x1, gffn, wg, wu, wd)


def loss_head(y, target):
    T = y.shape[0]
    tm = min(ROW_TILE, T)

    def body(y_ref, t_ref, dy_ref, loss_ref):
        @pl.when(pl.program_id(0) == 0)
        def _():
            loss_ref[...] = jnp.zeros(loss_ref.shape, F32)

        err = y_ref[...] - t_ref[...]
        dy_ref[...] = err * (1.0 / D_MODEL)
        per_row = jnp.sum(err * err, axis=1, keepdims=True) * (1.0 / D_MODEL)
        loss_ref[...] += 0.5 * jnp.sum(per_row, axis=0, keepdims=True)

    row = pl.BlockSpec((tm, D_MODEL), lambda i: (i, 0))
    return pl.pallas_call(
        body, grid=(T // tm,), name="loss_head",
        in_specs=[row, row],
        out_specs=[row, pl.BlockSpec((1, 1), lambda i: (0, 0))],
        out_shape=[jax.ShapeDtypeStruct((T, D_MODEL), F32), jax.ShapeDtypeStruct((1, 1), F32)],
        compiler_params=_cparams(),
    )(y, target)


def _acc_spec(shape):
    return pl.BlockSpec(shape, lambda i: (0,) * len(shape))


def ffn_bwd(dx2, x1, gs, us, gffn, wdt, wgt, wut):
    T = x1.shape[0]
    tm = min(ROW_TILE // 2, T)

    def body(dx2_ref, x1_ref, gs_ref, us_ref, gffn_ref, wdt_ref, wgt_ref, wut_ref,
             dx1_ref, h2_ref, act_ref, dg_ref, du_ref, dgffn_ref):
        @pl.when(pl.program_id(0) == 0)
        def _():
            dgffn_ref[...] = jnp.zeros(dgffn_ref.shape, F32)

        dx2v = dx2_ref[...]
        dy = dx2v.astype(BF16)
        x1v = x1_ref[...]
        h2, r = _rms(x1v, gffn_ref[...], D_MODEL)
        h2_ref[...] = h2.astype(BF16)
        dh2 = jnp.zeros((tm, D_MODEL), F32)
        for c in range(FFN_HIDDEN // FFN_CHUNK):
            sl = slice(FFN_CHUNK * c, FFN_CHUNK * (c + 1))
            g = gs_ref[:, sl].astype(F32)
            u = us_ref[:, sl].astype(F32)
            dact = _dot(dy, wdt_ref[:, sl])
            sg = _sigmoid(g)
            silu = g * sg
            act_ref[:, sl] = (silu * u).astype(BF16)
            dg = (dact * u * (sg * (1.0 + g * (1.0 - sg)))).astype(BF16)
            du = (dact * silu).astype(BF16)
            dg_ref[:, sl] = dg
            du_ref[:, sl] = du
            dh2 = dh2 + _dot(dg, wgt_ref[sl, :]) + _dot(du, wut_ref[sl, :])
        dxn, dgn = _rms_bwd(x1v, r, gffn_ref[...], dh2, D_MODEL)
        dx1_ref[...] = dx2v + dxn
        dgffn_ref[...] += dgn

    row = lambda w: pl.BlockSpec((tm, w), lambda i: (i, 0))
    return pl.pallas_call(
        body, grid=(T // tm,), name="ffn_bwd",
        in_specs=[row(D_MODEL), row(D_MODEL), row(FFN_HIDDEN), row(FFN_HIDDEN)] + [WHOLE] * 4,
        out_specs=[row(D_MODEL), row(D_MODEL), row(FFN_HIDDEN), row(FFN_HIDDEN), row(FFN_HIDDEN), _acc_spec((1, D_MODEL))],
        out_shape=[jax.ShapeDtypeStruct((T, D_MODEL), F32), jax.ShapeDtypeStruct((T, D_MODEL), BF16),
                   jax.ShapeDtypeStruct((T, FFN_HIDDEN), BF16), jax.ShapeDtypeStruct((T, FFN_HIDDEN), BF16),
                   jax.ShapeDtypeStruct((T, FFN_HIDDEN), BF16), jax.ShapeDtypeStruct((1, D_MODEL), F32)],
        compiler_params=_cparams(),
    )(dx2, x1, gs, us, gffn, wdt, wgt, wut)


def _pick_tile(n, prefs):
    for p in prefs:
        if n % p == 0:
            return p
    return n


def matmul_tn(a, b, name):
    T, M = a.shape
    N = b.shape[1]
    tk = min(512, T)
    tm = _pick_tile(M, (512, 256))
    tn = _pick_tile(N, (1024, 640, 1408, 512))
    nk = T // tk

    def body(a_ref, b_ref, o_ref, acc):
        k = pl.program_id(2)

        @pl.when(k == 0)
        def _():
            acc[...] = jnp.zeros(acc.shape, F32)

        acc[...] += _dot_tn(a_ref[...].astype(BF16), b_ref[...].astype(BF16))

        @pl.when(k == nk - 1)
        def _():
            o_ref[...] = acc[...].astype(BF16)

    return pl.pallas_call(
        body, grid=(M // tm, N // tn, nk), name=name,
        in_specs=[pl.BlockSpec((tk, tm), lambda i, j, k: (k, i)), pl.BlockSpec((tk, tn), lambda i, j, k: (k, j))],
        out_specs=pl.BlockSpec((tm, tn), lambda i, j, k: (i, j)),
        out_shape=jax.ShapeDtypeStruct((M, N), BF16),
        scratch_shapes=[pltpu.VMEM((tm, tn), F32)],
        compiler_params=_cparams(),
    )(a, b)


def mixers_bwd(dx1, a, z, gsgu, wcat, wcat_t, bz, wp, wp_t, pscale, goa, gos, gop, wout_t):
    T = a.shape[0]
    tm = min(ROW_TILE, T)

    def body(dx1_ref, a_ref, u_ref, vs_ref, pin_ref, halo_ref, gsgu_ref, wcat_ref, wcatt_ref, bz_ref, wp_ref, wpt_ref,
             ps_ref, goa_ref, gos_ref, gop_ref, woutt_ref,
             da_ref, delta_ref, du_ref, dvs_ref, dm_ref, mix_ref,
             dgoa_ref, dgos_ref, dgop_ref, dps_ref, dwp_ref, dwsp_ref, db_ref, dgsgu_ref):
        i = pl.program_id(0)

        @pl.when(i == 0)
        def _():
            for r in (dgoa_ref, dgos_ref, dgop_ref, dps_ref, dwp_ref, dwsp_ref, db_ref, dgsgu_ref):
                r[...] = jnp.zeros(r.shape, F32)

        a_v, u, vs = a_ref[...], u_ref[...], vs_ref[...]
        goa, gos, gop, pscale_v = goa_ref[...], gos_ref[...], gop_ref[...], ps_ref[...]
        vn, rv, zc, gm, m, yp_pre, yp, ra, rg, rp, mix = _mixers_fwd_tile(
            i, tm, a_v, u, vs, pin_ref[...], halo_ref[...], gsgu_ref[...], wcat_ref[...], bz_ref[...], wp_ref[...],
            pscale_v, goa, gos, gop)
        mix_ref[...] = mix
        dmix = _dot(dx1_ref[...].astype(BF16), woutt_ref[...])
        da, dgoa = _rms_bwd(a_v, ra, goa, dmix[:, :MLA_WIDTH], MLA_WIDTH)
        dgm, dgos = _rms_bwd(gm, rg, gos, dmix[:, MLA_WIDTH:MLA_WIDTH + SGU_WIDTH], SGU_WIDTH)
        dyp, dgop = _rms_bwd(yp, rp, gop, dmix[:, MLA_WIDTH + SGU_WIDTH:], POOL_WIDTH)
        da_ref[...] = da
        dgoa_ref[...] += dgoa
        dgos_ref[...] += dgos
        dgop_ref[...] += dgop
        prod = da * a_v
        for h in range(HEADS):
            lo = HEAD_PAD * h
            delta_ref[h] = jnp.broadcast_to(jnp.sum(prod[:, lo:lo + HEAD_PAD], axis=1, keepdims=True), (tm, HEAD_PAD))
        dps_ref[...] += jnp.sum(dyp * yp_pre, axis=0, keepdims=True)
        dyp_pre = (dyp * pscale_v).astype(BF16)
        dwp_ref[...] += _dot_tn(m, dyp_pre)
        dm_ref[...] = _dot(dyp_pre, wpt_ref[...])
        du_ref[...] = dgm * zc
        dzc = dgm * u
        masks = _head_masks((CHUNK, SGU_WIDTH), SGU_HEAD_DIM)
        lane_b = lax.broadcasted_iota(jnp.int32, (CHUNK, HEAD_PAD), 1)
        dvn_parts = []
        dwsp = jnp.zeros(dwsp_ref.shape, F32)
        db = jnp.zeros(db_ref.shape, F32)
        for c in range(tm // CHUNK):
            dz_c = dzc[CHUNK * c:CHUNK * (c + 1)]
            dzstack = jnp.concatenate([jnp.where(mk, dz_c, 0.0).astype(BF16) for mk in masks], axis=0)
            dvn_parts.append(_dot(wcatt_ref[...], dzstack))
            dwsp = dwsp + _dot_nt(dzstack, vn[CHUNK * c:CHUNK * (c + 1)].astype(BF16))
            for h, mk in enumerate(masks):
                col = jnp.sum(jnp.where(mk, dz_c, 0.0), axis=1, keepdims=True)
                db = db + jnp.where(lane_b == h, col, 0.0)
        dwsp_ref[...] += dwsp
        db_ref[...] += db
        dvs, dgsgu = _rms_bwd(vs, rv, gsgu_ref[...], jnp.concatenate(dvn_parts, axis=0), SGU_WIDTH)
        dvs_ref[...] = dvs
        dgsgu_ref[...] += dgsgu

    row = lambda w: pl.BlockSpec((tm, w), lambda i: (i, 0))
    head = pl.BlockSpec((HEADS, tm, HEAD_PAD), lambda i: (0, i, 0))
    acc_shapes = [(1, MLA_WIDTH), (1, SGU_WIDTH), (1, POOL_WIDTH), (1, POOL_WIDTH), (POOL_WIDTH, POOL_WIDTH),
                  (HEADS * CHUNK, CHUNK), (CHUNK, HEAD_PAD), (1, SGU_WIDTH)]
    return pl.pallas_call(
        body, grid=(T // tm,), name="mixers_bwd",
        in_specs=[row(D_MODEL), row(MLA_WIDTH)] + _z_specs(tm) + [WHOLE] * 11,
        out_specs=[row(MLA_WIDTH), head, row(256), row(256), row(256), row(D_MODEL)] + [_acc_spec(s) for s in acc_shapes],
        out_shape=[jax.ShapeDtypeStruct((T, MLA_WIDTH), F32), jax.ShapeDtypeStruct((HEADS, T, HEAD_PAD), F32),
                   jax.ShapeDtypeStruct((T, 256), F32), jax.ShapeDtypeStruct((T, 256), F32),
                   jax.ShapeDtypeStruct((T, 256), F32), jax.ShapeDtypeStruct((T, D_MODEL), BF16)]
                  + [jax.ShapeDtypeStruct(s, F32) for s in acc_shapes],
        compiler_params=_cparams(),
    )(dx1, a, z, z, z, z, gsgu, wcat, wcat_t, bz, wp, wp_t, pscale, goa, gos, gop, wout_t)


def pool_bwd(dm):
    T = dm.shape[0]
    tm = min(ROW_TILE, T)
    nt = T // tm

    def body(dm_ref, next_ref, dpin_ref):
        i = pl.program_id(0)
        cnt, lane = _pool_counts(i, tm)
        dmv = dm_ref[...]
        win = _by_group(lane[:POOL_HALO], 2.0, 4.0, 8.0, 16.0)
        nxt = jnp.where(i < nt - 1, next_ref[...] / win, 0.0)
        r1 = jnp.concatenate([dmv / cnt, nxt], axis=0)
        n = tm + POOL_HALO
        r2 = r1 + pltpu.roll(r1, n - 1, 0)
        r4 = r2 + pltpu.roll(r2, n - 2, 0)
        r8 = r4 + pltpu.roll(r4, n - 4, 0)
        r16 = r8 + pltpu.roll(r8, n - 8, 0)
        dpin_ref[...] = _by_group(lane, r2[:tm], r4[:tm], r8[:tm], r16[:tm]) - dmv

    return pl.pallas_call(
        body, grid=(nt,), name="pool_bwd",
        in_specs=[pl.BlockSpec((tm, 256), lambda i: (i, 0)),
                  pl.BlockSpec((POOL_HALO, 256), lambda i: (jnp.minimum((i + 1) * (tm // POOL_HALO), T // POOL_HALO - 1), 0))],
        out_specs=pl.BlockSpec((tm, 256), lambda i: (i, 0)),
        out_shape=jax.ShapeDtypeStruct((T, 256), F32),
        compiler_params=_cparams(),
    )(dm, dm)


def attn_bwd(q, k, v, do, lse, delta):
    _, T, _ = q.shape
    tq = min(ATTN_TILE, T)
    nq = T // tq

    def body(q_ref, k_ref, v_ref, do_ref, lse_ref, delta_ref, dq_ref, dk_ref, dv_ref, dk_s, dv_s):
        i, j = pl.program_id(1), pl.program_id(2)

        @pl.when((i == 0) & (j == 0))
        def _():
            dq_ref[...] = jnp.zeros(dq_ref.shape, F32)

        def step(masked):
            qv, kv = q_ref[...], k_ref[...]
            s = _dot_nt(qv, kv) * ATTN_SCALE
            if masked:
                rows = lax.broadcasted_iota(jnp.int32, s.shape, 0)
                cols = lax.broadcasted_iota(jnp.int32, s.shape, 1)
                s = jnp.where(cols <= rows, s, NEG_BIG)
            p = jnp.exp(s - lse_ref[:, 0:1])
            dov = do_ref[...].astype(BF16)
            dv_new = _dot_tn(p.astype(BF16), dov)
            dp = _dot_nt(dov, v_ref[...])
            ds = (p * (dp - delta_ref[:, 0:1]) * ATTN_SCALE).astype(BF16)
            dk_new = _dot_tn(ds, qv)
            rows_q = pl.ds(pl.multiple_of(j * tq, tq), tq)
            dq_ref[rows_q, :] += _dot(ds, kv)
            return dk_new, dv_new

        @pl.when(j == i)
        def _():
            dk_new, dv_new = step(True)
            dk_s[...] = dk_new
            dv_s[...] = dv_new

        @pl.when(j > i)
        def _():
            dk_new, dv_new = step(False)
            dk_s[...] += dk_new
            dv_s[...] += dv_new

        @pl.when(j == nq - 1)
        def _():
            dk_ref[...] = dk_s[...]
            dv_ref[...] = dv_s[...]

    qspec = pl.BlockSpec((None, tq, HEAD_PAD), lambda h, i, j: (h, jnp.maximum(i, j), 0))
    kspec = pl.BlockSpec((None, tq, HEAD_PAD), lambda h, i, j: (h, i, 0))
    return pl.pallas_call(
        body, grid=(HEADS, nq, nq), name="attn_bwd",
        in_specs=[qspec, kspec, kspec, pl.BlockSpec((tq, HEAD_PAD), lambda h, i, j: (jnp.maximum(i, j), h)), qspec, qspec],
        out_specs=[pl.BlockSpec((None, T, HEAD_PAD), lambda h, i, j: (h, 0, 0)), kspec, kspec],
        out_shape=[jax.ShapeDtypeStruct((HEADS, T, HEAD_PAD), F32)] * 3,
        scratch_shapes=[pltpu.VMEM((tq, HEAD_PAD), F32), pltpu.VMEM((tq, HEAD_PAD), F32)],
        compiler_params=_cparams(),
    )(q, k, v, do, lse, delta)


def mixin_bwd(dres, x, z, ang, dq, dk, dv, du, dvs, dpin, gmix, win_t, gql, wq, wq_t, gkvl, wkv, wkv_t, gqh, gkh):
    T = x.shape[0]
    tm = min(ROW_TILE, T)

    def body(dres_ref, x_ref, z_ref, ang_ref, dq_ref, dk_ref, dv_ref, du_ref, dvs_ref, dpin_ref,
             gmix_ref, wint_ref, gql_ref, wq_ref, wqt_ref, gkvl_ref, wkv_ref, wkvt_ref, gqh_ref, gkh_ref,
             dx_ref, hn_ref, dz_ref, qn_ref, dqup_ref, kvn_ref, dkvup_ref,
             dgmix_ref, dgql_ref, dgkvl_ref, dgqh_ref, dgkh_ref):
        @pl.when(pl.program_id(0) == 0)
        def _():
            for r in (dgmix_ref, dgql_ref, dgkvl_ref, dgqh_ref, dgkh_ref):
                r[...] = jnp.zeros(r.shape, F32)

        xv = x_ref[...]
        hn, rx = _rms(xv, gmix_ref[...], D_MODEL)
        hn_ref[...] = hn.astype(BF16)
        ql, kvl, kr, qn, rq, kvn, rkv, q_up, kv_up = _qkv_pre(z_ref[...], gql_ref[...], wq_ref[...], gkvl_ref[...], wkv_ref[...])
        qn_ref[...] = qn
        kvn_ref[...] = kvn
        c, sa, sb = _rope_tables(ang_ref[...])
        lane = lax.broadcasted_iota(jnp.int32, (tm, HEAD_PAD), 1)
        rope_lanes = (lane >= QK_NOPE) & (lane < QK_HEAD)
        dkr = jnp.zeros((tm, HEAD_PAD), F32)
        dgqh = jnp.zeros((1, HEAD_PAD), F32)
        dgkh = jnp.zeros((1, HEAD_PAD), F32)
        dq_parts, dk_parts, dv_parts = [], [], []
        for h in range(HEADS):
            lo = HEAD_PAD * h
            qh = q_up[:, lo:lo + HEAD_PAD]
            rqh = lax.rsqrt(jnp.sum(qh * qh, axis=-1, keepdims=True) * (1.0 / QK_HEAD) + EPS)
            dqh, dg = _rms_bwd(qh, rqh, gqh_ref[...], _rope_bwd(dq_ref[h], c, sa, sb), QK_HEAD)
            dgqh = dgqh + dg
            dq_parts.append(dqh)
            kh = kv_up[:, lo:lo + HEAD_PAD] + kr
            rkh = lax.rsqrt(jnp.sum(kh * kh, axis=-1, keepdims=True) * (1.0 / QK_HEAD) + EPS)
            dkh, dg = _rms_bwd(kh, rkh, gkh_ref[...], _rope_bwd(dk_ref[h], c, sa, sb), QK_HEAD)
            dgkh = dgkh + dg
            dkr = dkr + jnp.where(rope_lanes, dkh, 0.0)
            dk_parts.append(dkh)
            dv_parts.append(dv_ref[h])
        dgqh_ref[...] += dgqh
        dgkh_ref[...] += dgkh
        dq_up = jnp.concatenate(dq_parts, axis=1).astype(BF16)
        dkv_up = jnp.concatenate(dk_parts + dv_parts, axis=1).astype(BF16)
        dqup_ref[...] = dq_up
        dkvup_ref[...] = dkv_up
        dql, dg = _rms_bwd(ql, rq, gql_ref[...], _dot(dq_up, wqt_ref[...]), Q_LORA)
        dgql_ref[...] += dg
        dkvl, dg = _rms_bwd(kvl, rkv, gkvl_ref[...], _dot(dkv_up, wkvt_ref[...]), KV_LORA)
        dgkvl_ref[...] += dg
        dz = jnp.concatenate([dql, dkvl, dkr, du_ref[...], dvs_ref[...], dpin_ref[...]], axis=1).astype(BF16)
        dz_ref[...] = dz
        dxn, dg = _rms_bwd(xv, rx, gmix_ref[...], _dot(dz, wint_ref[...]), D_MODEL)
        dgmix_ref[...] += dg
        dx_ref[...] = dres_ref[...] + dxn

    row = lambda w: pl.BlockSpec((tm, w), lambda i: (i, 0))
    head = pl.BlockSpec((HEADS, tm, HEAD_PAD), lambda i: (0, i, 0))
    acc_shapes = [(1, D_MODEL), (1, Q_LORA), (1, KV_LORA), (1, HEAD_PAD), (1, HEAD_PAD)]
    out_rows = [(D_MODEL, F32), (D_MODEL, BF16), (Z_WIDTH, BF16), (Q_LORA, BF16), (HEADS * HEAD_PAD, BF16),
                (KV_LORA, BF16), (2 * HEADS * HEAD_PAD, BF16)]
    return pl.pallas_call(
        body, grid=(T // tm,), name="mixin_bwd",
        in_specs=[row(D_MODEL), row(D_MODEL), row(Z_WIDTH), row(HEAD_PAD), head, head, head, row(256), row(256), row(256)]
                 + [WHOLE] * 10,
        out_specs=[row(w) for w, _ in out_rows] + [_acc_spec(s) for s in acc_shapes],
        out_shape=[jax.ShapeDtypeStruct((T, w), dt) for w, dt in out_rows] + [jax.ShapeDtypeStruct(s, F32) for s in acc_shapes],
        compiler_params=_cparams(),
    )(dres, x, z, ang, dq, dk, dv, du, dvs, dpin, gmix, win_t, gql, wq, wq_t, gkvl, wkv, wkv_t, gqh, gkh)


def _place():
    x, y, c = lax.axis_index("x"), lax.axis_index("y"), lax.axis_index("c")
    return x, y, c, 4 * x + 2 * y + c


def allgather_weights(shards):
    n = len(shards)

    def body(*refs):
        ins, outs = refs[:n], refs[n:2 * n]
        stage = refs[2 * n:3 * n]
        send_sems, recv_sems, local_sems = refs[3 * n:]
        x, y, c, me = _place()
        sibling = (x, y, 1 - c)
        chips = [(1 - x, y), (x, 1 - y), (1 - x, 1 - y)]

        def copy(k, w, block_id, to, from_stage):
            return pltpu.make_async_remote_copy(
                src_ref=stage[w] if from_stage else outs[w].at[block_id], dst_ref=outs[w].at[block_id],
                send_sem=send_sems.at[k, w], recv_sem=recv_sems.at[k, w], device_id=to, device_id_type=MESH)

        def block_of(cx, cy, cc):
            return 4 * cx + 2 * cy + cc

        local, sent = [], []
        for w in range(n):
            stage[w][...] = ins[w][...].astype(BF16)
            mine = pltpu.make_async_copy(stage[w], outs[w].at[me], local_sems.at[w])
            mine.start()
            local.append(mine)
            first = [copy(0, w, me, sibling, True)] + [copy(1 + j, w, me, (*chip, c), True) for j, chip in enumerate(chips)]
            for cp in first:
                cp.start()
            sent += first
        for j, chip in enumerate(chips):
            for w in range(n):
                copy(1 + j, w, block_of(*chip, c), (x, y, c), False).wait_recv()
                fwd = copy(4 + j, w, block_of(*chip, c), sibling, False)
                fwd.start()
                sent.append(fwd)
        for w in range(n):
            copy(0, w, block_of(x, y, 1 - c), (x, y, c), False).wait_recv()
            for j, chip in enumerate(chips):
                copy(4 + j, w, block_of(*chip, 1 - c), (x, y, c), False).wait_recv()
        for cp in local:
            cp.wait()
        for cp in sent:
            cp.wait_send()

    return pl.pallas_call(
        body, name="allgather_weights",
        in_specs=[WHOLE] * n, out_specs=[ANY] * n,
        out_shape=[jax.ShapeDtypeStruct((N_DEV,) + s.shape, BF16) for s in shards],
        scratch_shapes=[pltpu.VMEM(s.shape, BF16) for s in shards]
                       + [pltpu.SemaphoreType.DMA((7, n)), pltpu.SemaphoreType.DMA((7, n)), pltpu.SemaphoreType.DMA((n,))],
        compiler_params=_cparams(),
    )(*shards)


def exchange_grads(pieces):
    n = len(pieces)

    def body(*refs):
        ins, outs = refs[:n], refs[n:2 * n]
        send_sems, recv_sems, local_sems = refs[2 * n:]
        x, y, c, me = _place()
        local = [pltpu.make_async_copy(ins[w].at[me], outs[w].at[me], local_sems.at[w]) for w in range(n)]
        for cp in local:
            cp.start()
        remote = []
        for k in range(1, N_DEV):
            px = 1 - x if k & 4 else x
            py = 1 - y if k & 2 else y
            pc = 1 - c if k & 1 else c
            peer = 4 * px + 2 * py + pc
            for w in range(n):
                cp = pltpu.make_async_remote_copy(
                    src_ref=ins[w].at[peer], dst_ref=outs[w].at[me], send_sem=send_sems.at[k - 1, w],
                    recv_sem=recv_sems.at[k - 1, w], device_id=(px, py, pc), device_id_type=MESH)
                cp.start()
                remote.append(cp)
        for cp in local:
            cp.wait()
        for cp in remote:
            cp.wait_recv()
        for cp in remote:
            cp.wait_send()

    return pl.pallas_call(
        body, name="exchange_grads",
        in_specs=[ANY] * n, out_specs=[ANY] * n,
        out_shape=[jax.ShapeDtypeStruct(p.shape, BF16) for p in pieces],
        scratch_shapes=[pltpu.SemaphoreType.DMA((7, n)), pltpu.SemaphoreType.DMA((7, n)), pltpu.SemaphoreType.DMA((n,))],
        compiler_params=_cparams(),
    )(*pieces)


def _adamw(w, g, m, v):
    m2 = ADAM_B1 * m + (1.0 - ADAM_B1) * g
    v2 = ADAM_B2 * v + (1.0 - ADAM_B2) * (g * g)
    m_hat = m2 / (1.0 - ADAM_B1 ** ADAM_STEP)
    v_hat = v2 / (1.0 - ADAM_B2 ** ADAM_STEP)
    delta = -ADAM_LR * (m_hat / (jnp.sqrt(v_hat) + ADAM_EPS) + ADAM_WD * w)
    return delta, m2, v2


def adamw_sharded(parts, w, m, v, name):
    _, R, C = parts.shape
    br = max(d for d in range(16, min(R, 512) + 1, 16) if R % d == 0)

    def body(p_ref, w_ref, m_ref, v_ref, g_ref, d_ref, m2_ref, v2_ref):
        g = p_ref[0].astype(F32)
        for s in range(1, N_DEV):
            g = g + p_ref[s].astype(F32)
        g_ref[...] = g
        d_ref[...], m2_ref[...], v2_ref[...] = _adamw(w_ref[...], g, m_ref[...], v_ref[...])

    blk = pl.BlockSpec((br, C), lambda i: (i, 0))
    return pl.pallas_call(
        body, grid=(R // br,), name=name,
        in_specs=[pl.BlockSpec((N_DEV, br, C), lambda i: (0, i, 0)), blk, blk, blk],
        out_specs=[blk] * 4,
        out_shape=[jax.ShapeDtypeStruct((R, C), F32)] * 4,
        compiler_params=_cparams(),
    )(parts, w, m, v)


def allreduce_adamw_small(gpart, w, m, v):
    R = gpart.shape[0]

    def body(g_ref, w_ref, m_ref, v_ref, grad_ref, d_ref, m2_ref, v2_ref, all_ref, send_sems, recv_sems):
        x, y, c, me = _place()
        sibling = (x, y, 1 - c)
        chips = [(1 - x, y), (x, 1 - y), (1 - x, 1 - y)]

        def copy(k, block_id, to, from_input):
            return pltpu.make_async_remote_copy(
                src_ref=g_ref if from_input else all_ref.at[block_id], dst_ref=all_ref.at[block_id],
                send_sem=send_sems.at[k], recv_sem=recv_sems.at[k], device_id=to, device_id_type=MESH)

        def block_of(cx, cy, cc):
            return 4 * cx + 2 * cy + cc

        all_ref[me] = g_ref[...]
        first = [copy(0, me, sibling, True)] + [copy(1 + j, me, (*chip, c), True) for j, chip in enumerate(chips)]
        for cp in first:
            cp.start()
        passed = [copy(4 + j, block_of(*chip, c), sibling, False) for j, chip in enumerate(chips)]
        for j, chip in enumerate(chips):
            copy(1 + j, block_of(*chip, c), (x, y, c), False).wait_recv()
            passed[j].start()
        copy(0, block_of(x, y, 1 - c), (x, y, c), False).wait_recv()
        for j, chip in enumerate(chips):
            copy(4 + j, block_of(*chip, 1 - c), (x, y, c), False).wait_recv()
        for cp in first + passed:
            cp.wait_send()
        g = all_ref[0]
        for s in range(1, N_DEV):
            g = g + all_ref[s]
        grad_ref[...] = g
        d_ref[...], m2_ref[...], v2_ref[...] = _adamw(w_ref[...], g, m_ref[...], v_ref[...])

    return pl.pallas_call(
        body, name="allreduce_adamw_small",
        in_specs=[WHOLE] * 4, out_specs=[WHOLE] * 4,
        out_shape=[jax.ShapeDtypeStruct((R, 128), F32)] * 4,
        scratch_shapes=[pltpu.VMEM((N_DEV, R, 128), F32), pltpu.SemaphoreType.DMA((7,)), pltpu.SemaphoreType.DMA((7,))],
        compiler_params=_cparams(),
    )(gpart, w, m, v)


def _cols_from_pieces(p):
    return p.transpose(1, 0, 2).reshape(p.shape[1], -1)


def _pieces_from_cols(w):
    return w.reshape(w.shape[0], N_DEV, -1).transpose(1, 0, 2)


def _pad_heads(w, width):
    r = w.shape[0]
    return jnp.pad(w.reshape(r, HEADS, width), ((0, 0), (0, 0), (0, HEAD_PAD - width))).reshape(r, HEADS * HEAD_PAD)


def _unpad_heads(w, width):
    r = w.shape[0]
    return w.reshape(r, HEADS, HEAD_PAD)[:, :, :width].reshape(r, HEADS * width)


O1 = Q_LORA
O2 = O1 + KV_LORA
O3 = O2 + QK_ROPE


def _layer_weights(gw, l):
    w_in = _cols_from_pieces(gw["w_in"][:, l])
    zero = lambda n: jnp.zeros((D_MODEL, n), BF16)
    win = jnp.concatenate([w_in[:, :O2], zero(QK_NOPE), w_in[:, O2:O3], zero(HEAD_PAD - QK_HEAD), w_in[:, O3:]], axis=1)
    wq = _pad_heads(_cols_from_pieces(gw["w_q_up"][:, l]), QK_HEAD)
    w_kv = _cols_from_pieces(gw["w_kv_up"][:, l]).reshape(KV_LORA, HEADS, QK_NOPE + V_HEAD)
    wk = jnp.pad(w_kv[:, :, :QK_NOPE], ((0, 0), (0, 0), (0, HEAD_PAD - QK_NOPE))).reshape(KV_LORA, HEADS * HEAD_PAD)
    wv = w_kv[:, :, QK_NOPE:].reshape(KV_LORA, HEADS * V_HEAD)
    wkv = jnp.concatenate([wk, wv], axis=1)
    wout = gw["w_out"][:, l].reshape(D_MODEL, D_MODEL)
    wg = _cols_from_pieces(gw["w_gate"][:, l])
    wu = _cols_from_pieces(gw["w_up"][:, l])
    wd = gw["w_down"][:, l].reshape(FFN_HIDDEN, D_MODEL)
    return dict(win=win, win_t=win.T, wq=wq, wq_t=wq.T, wkv=wkv, wkv_t=wkv.T, wout=wout, wout_t=wout.T,
                wg=wg, wg_t=wg.T, wu=wu, wu_t=wu.T, wd=wd, wd_t=wd.T)


def _layer_small(p, l):
    row = lambda a: a.reshape(1, -1)
    pad_head = lambda g: jnp.pad(g, (0, HEAD_PAD - QK_HEAD)).reshape(1, HEAD_PAD)
    tril = jnp.tril(jnp.ones((CHUNK, CHUNK), F32))
    wsp = p["w_spatial"][l] * tril
    wcat = jnp.concatenate([wsp[h] for h in range(HEADS)], axis=1).astype(BF16)
    wcat_t = jnp.concatenate([wsp[h].T for h in range(HEADS)], axis=1).astype(BF16)
    bz = jnp.repeat(p["b_spatial"][l].T, SGU_HEAD_DIM, axis=1)
    wp = jax.scipy.linalg.block_diag(*[p["w_pool"][l][g] for g in range(HEADS)]).astype(BF16)
    return dict(gmix=row(p["g_mix_norm"][l]), gql=row(p["g_q_lat"][l]), gkvl=row(p["g_kv_lat"][l]),
                gqh=pad_head(p["g_q_head"][l]), gkh=pad_head(p["g_k_head"][l]), gsgu=row(p["g_sgu_v"][l]),
                wcat=wcat, wcat_t=wcat_t, bz=bz, wp=wp, wp_t=wp.T, pscale=row(p["pool_scale"][l]),
                goa=row(p["g_out_mla"][l]), gos=row(p["g_out_sgu"][l]), gop=row(p["g_out_pool"][l]),
                gffn=row(p["g_ffn_norm"][l]))


SHARDED = ("w_in", "w_q_up", "w_kv_up", "w_out", "w_gate", "w_up", "w_down")
SMALL = ("g_mix_norm", "g_q_lat", "g_kv_lat", "g_q_head", "g_k_head", "g_sgu_v", "w_spatial", "b_spatial", "w_pool",
         "pool_scale", "g_out_mla", "g_out_sgu", "g_out_pool", "g_ffn_norm")
WEIGHTS = ("g_mix_norm", "w_in", "g_q_lat", "w_q_up", "g_kv_lat", "w_kv_up", "g_q_head", "g_k_head", "g_sgu_v", "w_spatial",
           "b_spatial", "w_pool", "pool_scale", "g_out_mla", "g_out_sgu", "g_out_pool", "w_out", "g_ffn_norm", "w_gate",
           "w_up", "w_down")
PACK_ROWS = 8 * 128


def _pack_small(parts):
    flat = []
    for name in SMALL:
        a = parts[name].reshape(-1)
        flat.append(jnp.pad(a, (0, -a.shape[0] % PACK_ROWS)))
    return jnp.concatenate(flat).reshape(-1, 128)


def _unpack_small(packed, like):
    out, row = {}, 0
    for name in SMALL:
        n = math.prod(like[name].shape)
        rows = -(-n // PACK_ROWS) * 8
        out[name] = packed[row:row + rows].reshape(-1)[:n].reshape(like[name].shape)
        row += rows
    return out


def _forward_layer(x, ang, W, S):
    z, q, k, v = mixin_fwd(x, ang, S["gmix"], W["win"], S["gql"], W["wq"], S["gkvl"], W["wkv"], S["gqh"], S["gkh"])
    a, lse = attn_fwd(q, k, v)
    x1 = mixers_fwd(x, a, z, S["gsgu"], S["wcat"], S["bz"], S["wp"], S["pscale"], S["goa"], S["gos"], S["gop"], W["wout"])
    x2, gs, us = ffn_fwd(x1, S["gffn"], W["wg"], W["wu"], W["wd"])
    return x2, dict(x=x, z=z, q=q, k=k, v=v, a=a, lse=lse, x1=x1, gs=gs, us=us)


def _backward_layer(dx2, ang, W, S, A, l):
    dx1, h2, act, dg, du_ffn, dgffn = ffn_bwd(dx2, A["x1"], A["gs"], A["us"], S["gffn"], W["wd_t"], W["wg_t"], W["wu_t"])
    d_wd = matmul_tn(act, dx2, f"dw_down_{l}")
    d_wg = matmul_tn(h2, dg, f"dw_gate_{l}")
    d_wu = matmul_tn(h2, du_ffn, f"dw_up_{l}")
    (da, delta, du, dvs, dm, mix, dgoa, dgos, dgop, dps, dwp, dwsp, db, dgsgu) = mixers_bwd(
        dx1, A["a"], A["z"], S["gsgu"], S["wcat"], S["wcat_t"], S["bz"], S["wp"], S["wp_t"], S["pscale"], S["goa"], S["gos"],
        S["gop"], W["wout_t"])
    d_wout = matmul_tn(mix, dx1, f"dw_out_{l}")
    dpin = pool_bwd(dm)
    dq, dk, dv = attn_bwd(A["q"], A["k"], A["v"], da, A["lse"], delta)
    (dx, hn, dz, qn, dq_up, kvn, dkv_up, dgmix, dgql, dgkvl, dgqh, dgkh) = mixin_bwd(
        dx1, A["x"], A["z"], ang, dq, dk, dv, du, dvs, dpin, S["gmix"], W["win_t"], S["gql"], W["wq"], W["wq_t"], S["gkvl"],
        W["wkv"], W["wkv_t"], S["gqh"], S["gkh"])
    d_win = matmul_tn(hn, dz, f"dw_in_{l}")
    d_wq = matmul_tn(qn, dq_up, f"dw_q_up_{l}")
    d_wkv = matmul_tn(kvn, dkv_up, f"dw_kv_up_{l}")
    d_win = jnp.concatenate([d_win[:, :O2], d_win[:, O2 + QK_NOPE:O2 + QK_HEAD], d_win[:, O2 + HEAD_PAD:]], axis=1)
    d_wk = d_wkv[:, :HEADS * HEAD_PAD].reshape(KV_LORA, HEADS, HEAD_PAD)[:, :, :QK_NOPE]
    d_wv = d_wkv[:, HEADS * HEAD_PAD:].reshape(KV_LORA, HEADS, V_HEAD)
    d_wkv = jnp.concatenate([d_wk, d_wv], axis=2).reshape(KV_LORA, HEADS * (QK_NOPE + V_HEAD))
    big = dict(w_in=_pieces_from_cols(d_win), w_q_up=_pieces_from_cols(_unpad_heads(d_wq, QK_HEAD)),
               w_kv_up=_pieces_from_cols(d_wkv), w_out=d_wout.reshape(N_DEV, -1, D_MODEL),
               w_gate=_pieces_from_cols(d_wg), w_up=_pieces_from_cols(d_wu), w_down=d_wd.reshape(N_DEV, -1, D_MODEL))
    tril = jnp.tril(jnp.ones((CHUNK, CHUNK), F32))
    small = dict(g_mix_norm=dgmix[0], g_q_lat=dgql[0], g_kv_lat=dgkvl[0], g_q_head=dgqh[0, :QK_HEAD], g_k_head=dgkh[0, :QK_HEAD],
                 g_sgu_v=dgsgu[0], w_spatial=dwsp.reshape(HEADS, CHUNK, CHUNK) * tril, b_spatial=db[:, :HEADS].T,
                 w_pool=jnp.stack([dwp[64 * g:64 * (g + 1), 64 * g:64 * (g + 1)] for g in range(HEADS)]),
                 pool_scale=dps[0], g_out_mla=dgoa[0], g_out_sgu=dgos[0], g_out_pool=dgop[0], g_ffn_norm=dgffn[0])
    return dx, big, small


def kernel(x, positions, g_mix_norm, w_in, g_q_lat, w_q_up, g_kv_lat, w_kv_up, g_q_head, g_k_head, g_sgu_v, w_spatial, b_spatial, w_pool, pool_scale, g_out_mla, g_out_sgu, g_out_pool, w_out, g_ffn_norm, w_gate, w_up, w_down, loss_target, m_g_mix_norm, m_w_in, m_g_q_lat, m_w_q_up, m_g_kv_lat, m_w_kv_up, m_g_q_head, m_g_k_head, m_g_sgu_v, m_w_spatial, m_b_spatial, m_w_pool, m_pool_scale, m_g_out_mla, m_g_out_sgu, m_g_out_pool, m_w_out, m_g_ffn_norm, m_w_gate, m_w_up, m_w_down, v_g_mix_norm, v_w_in, v_g_q_lat, v_w_q_up, v_g_kv_lat, v_w_kv_up, v_g_q_head, v_g_k_head, v_g_sgu_v, v_w_spatial, v_b_spatial, v_w_pool, v_pool_scale, v_g_out_mla, v_g_out_sgu, v_g_out_pool, v_w_out, v_g_ffn_norm, v_w_gate, v_w_up, v_w_down):
    given = dict(locals())
    w = {n: given[n] for n in WEIGHTS}
    m = {n: given["m_" + n] for n in WEIGHTS}
    v = {n: given["v_" + n] for n in WEIGHTS}
    T = x.shape[1]
    xs = x.reshape(T, D_MODEL)

    half = QK_ROPE // 2
    inv_freq = 1.0 / (ROPE_THETA ** (jnp.arange(half, dtype=F32) / half))
    ang16 = positions.reshape(T).astype(F32)[:, None] * inv_freq
    ang = jnp.concatenate([jnp.zeros((T, QK_NOPE), F32), ang16, ang16, jnp.zeros((T, HEAD_PAD - QK_HEAD), F32)], axis=1)

    gathered = dict(zip(SHARDED, allgather_weights([w[n] for n in SHARDED])))
    Ws = [_layer_weights(gathered, l) for l in range(DEPTH)]
    Ss = [_layer_small(w, l) for l in range(DEPTH)]

    acts = []
    h = xs
    for l in range(DEPTH):
        h, A = _forward_layer(h, ang, Ws[l], Ss[l])
        acts.append(A)
    dh, loss_part = loss_head(h, loss_target.reshape(T, D_MODEL))
    loss = lax.psum(loss_part[0, 0], ("x", "y", "c"))

    bigs, smalls = [None] * DEPTH, [None] * DEPTH
    for l in reversed(range(DEPTH)):
        dh, bigs[l], smalls[l] = _backward_layer(dh, ang, Ws[l], Ss[l], acts[l], l)
    grad_x = dh.reshape(x.shape)

    pieces = [jnp.stack([bigs[l][n] for l in range(DEPTH)], axis=1) for n in SHARDED]
    received = exchange_grads(pieces)
    grad, delta, new_m, new_v = {}, {}, {}, {}
    for n, parts in zip(SHARDED, received):
        cols = w[n].shape[-1]
        flat = lambda a: a.reshape(-1, cols)
        outs = adamw_sharded(parts.reshape(N_DEV, -1, cols), flat(w[n]), flat(m[n]), flat(v[n]), f"adamw_{n}")
        grad[n], delta[n], new_m[n], new_v[n] = [o.reshape(w[n].shape) for o in outs]

    small_part = {n: jnp.stack([smalls[l][n] for l in range(DEPTH)]) for n in SMALL}
    outs = allreduce_adamw_small(_pack_small(small_part), _pack_small(w), _pack_small(m), _pack_small(v))
    for d, o in zip((grad, delta, new_m, new_v), outs):
        d.update(_unpack_small(o, w))

    return (loss, grad_x, *[grad[n] for n in WEIGHTS], *[delta[n] for n in WEIGHTS], *[new_m[n] for n in WEIGHTS],
            *[new_v[n] for n in WEIGHTS])
```

```python
import functools
import math

import jax
import jax.numpy as jnp
from jax import lax
from jax.experimental import pallas as pl
from jax.experimental.pallas import tpu as pltpu

F32 = jnp.float32
BF16 = jnp.bfloat16

N_DEV = 8
DEPTH = 2
D_MODEL = 1024
HEADS = 4
HEAD_PAD = 128
QK_NOPE = 64
QK_ROPE = 32
QK_HEAD = QK_NOPE + QK_ROPE
V_HEAD = 128
Q_LORA = 256
KV_LORA = 128
SGU_WIDTH = 256
SGU_HEAD_DIM = 64
CHUNK = 128
POOL_WIDTH = 256
POOL_WINDOWS = (2, 4, 8, 16)
POOL_HALO = 16
MLA_WIDTH = 512
IN_WIDTH = 1184
Z_WIDTH = 1280
FFN_HIDDEN = 2816
FFN_CHUNK = 256
ROPE_THETA = 10000.0
EPS = 1e-6
ATTN_SCALE = 1.0 / math.sqrt(QK_HEAD)
LOG2E = 1.4426950408889634
NEG_BIG = -1e30

ADAM_LR = 0.001
ADAM_B1 = 0.9
ADAM_B2 = 0.999
ADAM_EPS = 1e-08
ADAM_WD = 0.01
ADAM_STEP = 10

VMEM_LIMIT = 56 * 1024 * 1024
ROW_TILE = 512
ATTN_TILE = 1024
ATTN_SUB = 512
ATTN_BWD_HEADS = 2
MESH = pl.DeviceIdType.MESH

WHOLE = pl.BlockSpec(memory_space=pltpu.VMEM)
ANY = pl.BlockSpec(memory_space=pl.ANY)


def _cparams(**kw):
    return pltpu.CompilerParams(vmem_limit_bytes=VMEM_LIMIT, **kw)


def _dot(a, b):
    return jnp.dot(a, b, preferred_element_type=F32)


def _dot_nt(a, b):
    return lax.dot_general(a, b, (((1,), (1,)), ((), ())), preferred_element_type=F32)


def _dot_tn(a, b):
    return lax.dot_general(a, b, (((0,), (0,)), ((), ())), preferred_element_type=F32)


def _rms(x, g, n):
    r = lax.rsqrt(jnp.sum(x * x, axis=-1, keepdims=True) * (1.0 / n) + EPS)
    return x * r * g, r


def _rms_bwd(x, r, g, dy, n):
    gdy = dy * g
    dx = r * gdy - x * (r * r * r) * (jnp.sum(x * gdy, axis=-1, keepdims=True) * (1.0 / n))
    dg = jnp.sum(dy * (x * r), axis=0, keepdims=True)
    return dx, dg


def _sigmoid(x):
    return 1.0 / (1.0 + jnp.exp(-x))


def rope_tables(ang):
    T = ang.shape[0]
    tm = min(ROW_TILE, T)

    def body(ang_ref, c_ref, sa_ref, sb_ref):
        a = ang_ref[...]
        lane = lax.broadcasted_iota(jnp.int32, a.shape, 1)
        s = jnp.sin(a)
        c_ref[...] = jnp.cos(a)
        sa_ref[...] = jnp.where(lane < QK_NOPE + QK_ROPE // 2, -s, 0.0)
        sb_ref[...] = jnp.where(lane >= QK_NOPE + QK_ROPE // 2, s, 0.0)

    row = pl.BlockSpec((tm, HEAD_PAD), lambda i: (i, 0))
    return pl.pallas_call(
        body, grid=(T // tm,), name="rope_tables", in_specs=[row], out_specs=[row] * 3,
        out_shape=[jax.ShapeDtypeStruct((T, HEAD_PAD), F32)] * 3, compiler_params=_cparams(),
    )(ang)


def _rope(x, c, sa, sb):
    half = QK_ROPE // 2
    return x * c + pltpu.roll(x, HEAD_PAD - half, 1) * sa + pltpu.roll(x, half, 1) * sb


def _rope_bwd(dy, c, sa, sb):
    half = QK_ROPE // 2
    return dy * c + pltpu.roll(dy * sa, half, 1) + pltpu.roll(dy * sb, HEAD_PAD - half, 1)


def _head_masks(shape, width):
    lane = lax.broadcasted_iota(jnp.int32, shape, len(shape) - 1)
    return [(lane >= width * h) & (lane < width * (h + 1)) for h in range(HEADS)]


def _qkv_pre(z, gql, wq, gkvl, wkv):
    ql = z[:, 0:Q_LORA]
    kvl = z[:, Q_LORA:Q_LORA + KV_LORA]
    kr = z[:, Q_LORA + KV_LORA:Q_LORA + KV_LORA + HEAD_PAD]
    qn, rq = _rms(ql, gql, Q_LORA)
    kvn, rkv = _rms(kvl, gkvl, KV_LORA)
    qn = qn.astype(BF16)
    kvn = kvn.astype(BF16)
    q_up = _dot(qn, wq)
    kv_up = _dot(kvn, wkv)
    return ql, kvl, kr, qn, rq, kvn, rkv, q_up, kv_up


def mixin_fwd(x, rope, gmix, win, gql, wq, gkvl, wkv, gqh, gkh):
    T = x.shape[0]
    tm = min(ROW_TILE, T)

    def body(x_ref, c_ref, sa_ref, sb_ref, gmix_ref, win_ref, gql_ref, wq_ref, gkvl_ref, wkv_ref, gqh_ref, gkh_ref,
             z_ref, q_ref, k_ref, v_ref, kt_ref, vt_ref):
        hn = _rms(x_ref[...], gmix_ref[...], D_MODEL)[0].astype(BF16)
        z = _dot(hn, win_ref[...])
        z_ref[...] = z
        _, _, kr, _, _, _, _, q_up, kv_up = _qkv_pre(z, gql_ref[...], wq_ref[...], gkvl_ref[...], wkv_ref[...])
        c, sa, sb = c_ref[...], sa_ref[...], sb_ref[...]
        for h in range(HEADS):
            lo = HEAD_PAD * h
            qh = _rms(q_up[:, lo:lo + HEAD_PAD], gqh_ref[...], QK_HEAD)[0]
            q_ref[h] = (_rope(qh, c, sa, sb) * (ATTN_SCALE * LOG2E)).astype(BF16)
            kh = _rope(_rms(kv_up[:, lo:lo + HEAD_PAD] + kr, gkh_ref[...], QK_HEAD)[0], c, sa, sb)
            k_ref[h] = kh.astype(BF16)
            kt_ref[h] = jnp.transpose(kh).astype(BF16)
            vh = kv_up[:, HEADS * HEAD_PAD + lo:HEADS * HEAD_PAD + lo + HEAD_PAD]
            v_ref[h] = vh.astype(BF16)
            vt_ref[h] = jnp.transpose(vh).astype(BF16)

    row = lambda w: pl.BlockSpec((tm, w), lambda i: (i, 0))
    head = pl.BlockSpec((HEADS, tm, HEAD_PAD), lambda i: (0, i, 0))
    head_t = pl.BlockSpec((HEADS, HEAD_PAD, tm), lambda i: (0, 0, i))
    return pl.pallas_call(
        body, grid=(T // tm,), name="mixin_fwd",
        in_specs=[row(D_MODEL)] + [row(HEAD_PAD)] * 3 + [WHOLE] * 8,
        out_specs=[row(Z_WIDTH), head, head, head, head_t, head_t],
        out_shape=[jax.ShapeDtypeStruct((T, Z_WIDTH), F32)] + [jax.ShapeDtypeStruct((HEADS, T, HEAD_PAD), BF16)] * 3
                  + [jax.ShapeDtypeStruct((HEADS, HEAD_PAD, T), BF16)] * 2,
        compiler_params=_cparams(),
    )(x, *rope, gmix, win, gql, wq, gkvl, wkv, gqh, gkh)


def attn_fwd(q, k, vt):
    _, T, _ = q.shape
    tb = min(ATTN_TILE, T)
    sb = min(ATTN_SUB, tb)
    ns = tb // sb
    nb = T // tb

    def body(q_ref, k_ref, vt_ref, o_ref, lse_ref, m_s, l_s, acc_s):
        i, j = pl.program_id(0), pl.program_id(1)

        @pl.when(j == 0)
        def _():
            m_s[...] = jnp.full(m_s.shape, -jnp.inf, F32)
            l_s[...] = jnp.zeros(l_s.shape, F32)
            acc_s[...] = jnp.zeros(acc_s.shape, F32)

        def sub_block(h, a, b, masked):
            qa = slice(sb * a, sb * (a + 1))
            kb = slice(sb * b, sb * (b + 1))
            st = _dot_nt(k_ref[h, kb, :], q_ref[h, qa, :])
            if masked:
                krow = lax.broadcasted_iota(jnp.int32, st.shape, 0)
                qcol = lax.broadcasted_iota(jnp.int32, st.shape, 1)
                st = jnp.where(krow <= qcol, st, NEG_BIG)
            m_prev = m_s[h, :, qa]
            m_new = jnp.maximum(m_prev, jnp.max(st, axis=0, keepdims=True))
            alpha = jnp.exp2(m_prev - m_new)
            pt = jnp.exp2(st - m_new)
            l_s[h, :, qa] = alpha * l_s[h, :, qa] + jnp.sum(pt, axis=0, keepdims=True)
            acc_s[h, :, qa] = alpha * acc_s[h, :, qa] + _dot(vt_ref[h, :, kb], pt.astype(BF16))
            m_s[h, :, qa] = m_new

        @pl.when(j < i)
        def _():
            for b in range(ns):
                for h in range(HEADS):
                    for a in range(ns):
                        sub_block(h, a, b, False)

        @pl.when(j == i)
        def _():
            for b in range(ns):
                for h in range(HEADS):
                    for a in range(b, ns):
                        sub_block(h, a, b, a == b)
            for h in range(HEADS):
                l = l_s[h]
                o_ref[:, HEAD_PAD * h:HEAD_PAD * (h + 1)] = jnp.transpose(acc_s[h] / l)
                lse_ref[h] = m_s[h] + jnp.log2(l)

    qspec = pl.BlockSpec((HEADS, tb, HEAD_PAD), lambda i, j: (0, i, 0))
    kspec = pl.BlockSpec((HEADS, tb, HEAD_PAD), lambda i, j: (0, jnp.minimum(i, j), 0))
    vspec = pl.BlockSpec((HEADS, HEAD_PAD, tb), lambda i, j: (0, 0, jnp.minimum(i, j)))
    return pl.pallas_call(
        body, grid=(nb, nb), name="attn_fwd",
        in_specs=[qspec, kspec, vspec],
        out_specs=[pl.BlockSpec((tb, MLA_WIDTH), lambda i, j: (i, 0)), pl.BlockSpec((HEADS, 1, tb), lambda i, j: (0, 0, i))],
        out_shape=[jax.ShapeDtypeStruct((T, MLA_WIDTH), F32), jax.ShapeDtypeStruct((HEADS, 1, T), F32)],
        scratch_shapes=[pltpu.VMEM((HEADS, 1, tb), F32), pltpu.VMEM((HEADS, 1, tb), F32), pltpu.VMEM((HEADS, HEAD_PAD, tb), F32)],
        compiler_params=_cparams(),
    )(q, k, vt)


def _sgu_fwd_chunk(vn_c, wcat, bz, masks):
    vstack = jnp.concatenate([jnp.where(mk, vn_c, 0.0).astype(BF16) for mk in masks], axis=0)
    return _dot(wcat, vstack) + bz


def _pool_counts(i, tm):
    pos1 = (i * tm + 1 + lax.broadcasted_iota(jnp.int32, (tm, POOL_WIDTH), 0)).astype(F32)
    lane = lax.broadcasted_iota(jnp.int32, (tm, POOL_WIDTH), 1)
    win = jnp.where(lane < 64, 2.0, jnp.where(lane < 128, 4.0, jnp.where(lane < 192, 8.0, 16.0)))
    return jnp.minimum(pos1, win), lane


def _by_group(lane, s2, s4, s8, s16):
    return jnp.where(lane < 64, s2, jnp.where(lane < 128, s4, jnp.where(lane < 192, s8, s16)))


def _pool_means(pin, halo, i, tm):
    s1 = jnp.concatenate([halo, pin], axis=0)
    s2 = s1 + pltpu.roll(s1, 1, 0)
    s4 = s2 + pltpu.roll(s2, 2, 0)
    s8 = s4 + pltpu.roll(s4, 4, 0)
    s16 = s8 + pltpu.roll(s8, 8, 0)
    cnt, lane = _pool_counts(i, tm)
    sel = _by_group(lane, s2[POOL_HALO:], s4[POOL_HALO:], s8[POOL_HALO:], s16[POOL_HALO:])
    return sel / cnt - pin


def _mixers_fwd_tile(i, tm, a, u, vs, pin, halo, gsgu, wcat, bz, wp, pscale, goa, gos, gop):
    vn, rv = _rms(vs, gsgu, SGU_WIDTH)
    masks = _head_masks((CHUNK, SGU_WIDTH), SGU_HEAD_DIM)
    zc = jnp.concatenate([_sgu_fwd_chunk(vn[CHUNK * c:CHUNK * (c + 1)], wcat, bz, masks) for c in range(tm // CHUNK)], axis=0)
    gm = u * zc
    halo = jnp.where(i > 0, halo, 0.0)
    m = _pool_means(pin, halo, i, tm).astype(BF16)
    yp_pre = _dot(m, wp)
    yp = yp_pre * pscale
    na, ra = _rms(a, goa, MLA_WIDTH)
    ng, rg = _rms(gm, gos, SGU_WIDTH)
    npo, rp = _rms(yp, gop, POOL_WIDTH)
    mix = jnp.concatenate([na, ng, npo], axis=1).astype(BF16)
    return vn, rv, zc, gm, m, yp_pre, yp, ra, rg, rp, mix


def _z_specs(tm):
    col = lambda c: pl.BlockSpec((tm, 256), lambda i: (i, c))
    halo = pl.BlockSpec((POOL_HALO, 256), lambda i: (jnp.maximum(i * (tm // POOL_HALO) - 1, 0), 4))
    return [col(2), col(3), col(4), halo]


def mixers_fwd(x, a, z, gsgu, wcat, bz, wp, pscale, goa, gos, gop, wout):
    T = x.shape[0]
    tm = min(ROW_TILE, T)

    def body(x_ref, a_ref, u_ref, vs_ref, pin_ref, halo_ref, gsgu_ref, wcat_ref, bz_ref, wp_ref, ps_ref,
             goa_ref, gos_ref, gop_ref, wout_ref, x1_ref):
        i = pl.program_id(0)
        mix = _mixers_fwd_tile(i, tm, a_ref[...], u_ref[...], vs_ref[...], pin_ref[...], halo_ref[...], gsgu_ref[...],
                               wcat_ref[...], bz_ref[...], wp_ref[...], ps_ref[...], goa_ref[...], gos_ref[...],
                               gop_ref[...])[-1]
        x1_ref[...] = x_ref[...] + _dot(mix, wout_ref[...])

    row = lambda w: pl.BlockSpec((tm, w), lambda i: (i, 0))
    return pl.pallas_call(
        body, grid=(T // tm,), name="mixers_fwd",
        in_specs=[row(D_MODEL), row(MLA_WIDTH)] + _z_specs(tm) + [WHOLE] * 9,
        out_specs=row(D_MODEL),
        out_shape=jax.ShapeDtypeStruct((T, D_MODEL), F32),
        compiler_params=_cparams(),
    )(x, a, z, z, z, z, gsgu, wcat, bz, wp, pscale, goa, gos, gop, wout)


def ffn_fwd(x1, gffn, wg, wu, wd):
    T = x1.shape[0]
    tm = min(ROW_TILE, T)

    def body(x1_ref, gffn_ref, wg_ref, wu_ref, wd_ref, x2_ref, gs_ref, us_ref):
        x1v = x1_ref[...]
        h2 = _rms(x1v, gffn_ref[...], D_MODEL)[0].astype(BF16)
        acc = x1v
        for c in range(FFN_HIDDEN // FFN_CHUNK):
            sl = slice(FFN_CHUNK * c, FFN_CHUNK * (c + 1))
            g = _dot(h2, wg_ref[:, sl])
            u = _dot(h2, wu_ref[:, sl])
            gs_ref[:, sl] = g.astype(BF16)
            us_ref[:, sl] = u.astype(BF16)
            act = (g * _sigmoid(g) * u).astype(BF16)
            acc = acc + _dot(act, wd_ref[sl, :])
        x2_ref[...] = acc

    row = lambda w: pl.BlockSpec((tm, w), lambda i: (i, 0))
    return pl.pallas_call(
        body, grid=(T // tm,), name="ffn_fwd",
        in_specs=[row(D_MODEL)] + [WHOLE] * 4,
        out_specs=[row(D_MODEL), row(FFN_HIDDEN), row(FFN_HIDDEN)],
        out_shape=[jax.ShapeDtypeStruct((T, D_MODEL), F32), jax.ShapeDtypeStruct((T, FFN_HIDDEN), BF16),
                   jax.ShapeDtypeStruct((T, FFN_HIDDEN), BF16)],
        compiler_params=_cparams(),
    )(x1, gffn, wg, wu, wd)


def loss_head(y, target):
    T = y.shape[0]
    tm = min(ROW_TILE, T)

    def body(y_ref, t_ref, dy_ref, loss_ref):
        @pl.when(pl.program_id(0) == 0)
        def _():
            loss_ref[...] = jnp.zeros(loss_ref.shape, F32)

        err = y_ref[...] - t_ref[...]
        dy_ref[...] = err * (1.0 / D_MODEL)
        per_row = jnp.sum(err * err, axis=1, keepdims=True) * (1.0 / D_MODEL)
        loss_ref[...] += 0.5 * jnp.sum(per_row, axis=0, keepdims=True)

    row = pl.BlockSpec((tm, D_MODEL), lambda i: (i, 0))
    return pl.pallas_call(
        body, grid=(T // tm,), name="loss_head",
        in_specs=[row, row],
        out_specs=[row, pl.BlockSpec((1, 1), lambda i: (0, 0))],
        out_shape=[jax.ShapeDtypeStruct((T, D_MODEL), F32), jax.ShapeDtypeStruct((1, 1), F32)],
        compiler_params=_cparams(),
    )(y, target)


def _acc_spec(shape):
    return pl.BlockSpec(shape, lambda i: (0,) * len(shape))


def ffn_bwd(dx2, x1, gs, us, gffn, wg, wu, wd):
    T = x1.shape[0]
    tm = min(ROW_TILE // 2, T)

    def body(dx2_ref, x1_ref, gs_ref, us_ref, gffn_ref, wg_ref, wu_ref, wd_ref,
             dx1_ref, h2_ref, act_ref, dg_ref, du_ref, dgffn_ref):
        @pl.when(pl.program_id(0) == 0)
        def _():
            dgffn_ref[...] = jnp.zeros(dgffn_ref.shape, F32)

        dx2v = dx2_ref[...]
        dy = dx2v.astype(BF16)
        x1v = x1_ref[...]
        h2, r = _rms(x1v, gffn_ref[...], D_MODEL)
        h2_ref[...] = h2.astype(BF16)
        dh2 = jnp.zeros((tm, D_MODEL), F32)
        for c in range(FFN_HIDDEN // FFN_CHUNK):
            sl = slice(FFN_CHUNK * c, FFN_CHUNK * (c + 1))
            g = gs_ref[:, sl].astype(F32)
            u = us_ref[:, sl].astype(F32)
            dact = _dot_nt(dy, wd_ref[sl, :])
            sg = _sigmoid(g)
            silu = g * sg
            act_ref[:, sl] = (silu * u).astype(BF16)
            dg = (dact * u * (sg * (1.0 + g * (1.0 - sg)))).astype(BF16)
            du = (dact * silu).astype(BF16)
            dg_ref[:, sl] = dg
            du_ref[:, sl] = du
            dh2 = dh2 + _dot_nt(dg, wg_ref[:, sl]) + _dot_nt(du, wu_ref[:, sl])
        dxn, dgn = _rms_bwd(x1v, r, gffn_ref[...], dh2, D_MODEL)
        dx1_ref[...] = dx2v + dxn
        dgffn_ref[...] += dgn

    row = lambda w: pl.BlockSpec((tm, w), lambda i: (i, 0))
    return pl.pallas_call(
        body, grid=(T // tm,), name="ffn_bwd",
        in_specs=[row(D_MODEL), row(D_MODEL), row(FFN_HIDDEN), row(FFN_HIDDEN)] + [WHOLE] * 4,
        out_specs=[row(D_MODEL), row(D_MODEL), row(FFN_HIDDEN), row(FFN_HIDDEN), row(FFN_HIDDEN), _acc_spec((1, D_MODEL))],
        out_shape=[jax.ShapeDtypeStruct((T, D_MODEL), F32), jax.ShapeDtypeStruct((T, D_MODEL), BF16),
                   jax.ShapeDtypeStruct((T, FFN_HIDDEN), BF16), jax.ShapeDtypeStruct((T, FFN_HIDDEN), BF16),
                   jax.ShapeDtypeStruct((T, FFN_HIDDEN), BF16), jax.ShapeDtypeStruct((1, D_MODEL), F32)],
        compiler_params=_cparams(),
    )(dx2, x1, gs, us, gffn, wg, wu, wd)


TN_K_TILE = 1024
TN_ACC_BYTES = 6 * 1024 * 1024


def matmul_tn(a, b, name):
    T, M = a.shape
    N = b.shape[1]
    tk = min(TN_K_TILE, T)
    tm = M if M <= 1024 else M // 2
    tn = max(d for d in range(128, N + 1, 128) if N % d == 0 and tm * d * 4 <= TN_ACC_BYTES)
    nk = T // tk

    def body(a_ref, b_ref, o_ref, acc):
        k = pl.program_id(2)

        @pl.when(k == 0)
        def _():
            acc[...] = jnp.zeros(acc.shape, F32)

        acc[...] += _dot_tn(a_ref[...].astype(BF16), b_ref[...].astype(BF16))

        @pl.when(k == nk - 1)
        def _():
            o_ref[...] = acc[...].astype(BF16)

    return pl.pallas_call(
        body, grid=(M // tm, N // tn, nk), name=name,
        in_specs=[pl.BlockSpec((tk, tm), lambda i, j, k: (k, i)), pl.BlockSpec((tk, tn), lambda i, j, k: (k, j))],
        out_specs=pl.BlockSpec((tm, tn), lambda i, j, k: (i, j)),
        out_shape=jax.ShapeDtypeStruct((M, N), BF16),
        scratch_shapes=[pltpu.VMEM((tm, tn), F32)],
        compiler_params=_cparams(),
    )(a, b)


def mixers_bwd(dx1, a, z, gsgu, wcat, wcat_t, bz, wp, pscale, goa, gos, gop, wout):
    T = a.shape[0]
    tm = min(ROW_TILE, T)

    def body(dx1_ref, a_ref, u_ref, vs_ref, pin_ref, halo_ref, gsgu_ref, wcat_ref, wcatt_ref, bz_ref, wp_ref,
             ps_ref, goa_ref, gos_ref, gop_ref, wout_ref,
             da_ref, delta_ref, du_ref, dvs_ref, dm_ref, mix_ref,
             dgoa_ref, dgos_ref, dgop_ref, dps_ref, dwp_ref, dwsp_ref, db_ref, dgsgu_ref):
        i = pl.program_id(0)

        @pl.when(i == 0)
        def _():
            for r in (dgoa_ref, dgos_ref, dgop_ref, dps_ref, dwp_ref, dwsp_ref, db_ref, dgsgu_ref):
                r[...] = jnp.zeros(r.shape, F32)

        a_v, u, vs = a_ref[...], u_ref[...], vs_ref[...]
        goa, gos, gop, pscale_v = goa_ref[...], gos_ref[...], gop_ref[...], ps_ref[...]
        vn, rv, zc, gm, m, yp_pre, yp, ra, rg, rp, mix = _mixers_fwd_tile(
            i, tm, a_v, u, vs, pin_ref[...], halo_ref[...], gsgu_ref[...], wcat_ref[...], bz_ref[...], wp_ref[...],
            pscale_v, goa, gos, gop)
        mix_ref[...] = mix
        dmix = _dot_nt(dx1_ref[...].astype(BF16), wout_ref[...])
        da, dgoa = _rms_bwd(a_v, ra, goa, dmix[:, :MLA_WIDTH], MLA_WIDTH)
        dgm, dgos = _rms_bwd(gm, rg, gos, dmix[:, MLA_WIDTH:MLA_WIDTH + SGU_WIDTH], SGU_WIDTH)
        dyp, dgop = _rms_bwd(yp, rp, gop, dmix[:, MLA_WIDTH + SGU_WIDTH:], POOL_WIDTH)
        da_ref[...] = da
        dgoa_ref[...] += dgoa
        dgos_ref[...] += dgos
        dgop_ref[...] += dgop
        prod = da * a_v
        ones = jnp.ones((8, HEAD_PAD), F32)
        for h in range(HEADS):
            lo = HEAD_PAD * h
            sums = lax.dot_general(ones, prod[:, lo:lo + HEAD_PAD], (((1,), (1,)), ((), ())), preferred_element_type=F32,
                                   precision=lax.Precision.HIGHEST)
            delta_ref[h] = sums[0:1, :]
        dps_ref[...] += jnp.sum(dyp * yp_pre, axis=0, keepdims=True)
        dyp_pre = (dyp * pscale_v).astype(BF16)
        dwp_ref[...] += _dot_tn(m, dyp_pre)
        dm_ref[...] = _dot_nt(dyp_pre, wp_ref[...])
        du_ref[...] = dgm * zc
        dzc = dgm * u
        masks = _head_masks((CHUNK, SGU_WIDTH), SGU_HEAD_DIM)
        lane_b = lax.broadcasted_iota(jnp.int32, (CHUNK, HEAD_PAD), 1)
        dvn_parts = []
        dwsp = jnp.zeros(dwsp_ref.shape, F32)
        db = jnp.zeros(db_ref.shape, F32)
        for c in range(tm // CHUNK):
            dz_c = dzc[CHUNK * c:CHUNK * (c + 1)]
            dzstack = jnp.concatenate([jnp.where(mk, dz_c, 0.0).astype(BF16) for mk in masks], axis=0)
            dvn_parts.append(_dot(wcatt_ref[...], dzstack))
            dwsp = dwsp + _dot_nt(dzstack, vn[CHUNK * c:CHUNK * (c + 1)].astype(BF16))
            for h, mk in enumerate(masks):
                col = jnp.sum(jnp.where(mk, dz_c, 0.0), axis=1, keepdims=True)
                db = db + jnp.where(lane_b == h, col, 0.0)
        dwsp_ref[...] += dwsp
        db_ref[...] += db
        dvs, dgsgu = _rms_bwd(vs, rv, gsgu_ref[...], jnp.concatenate(dvn_parts, axis=0), SGU_WIDTH)
        dvs_ref[...] = dvs
        dgsgu_ref[...] += dgsgu

    row = lambda w: pl.BlockSpec((tm, w), lambda i: (i, 0))
    head = pl.BlockSpec((HEADS, 1, tm), lambda i: (0, 0, i))
    acc_shapes = [(1, MLA_WIDTH), (1, SGU_WIDTH), (1, POOL_WIDTH), (1, POOL_WIDTH), (POOL_WIDTH, POOL_WIDTH),
                  (HEADS * CHUNK, CHUNK), (CHUNK, HEAD_PAD), (1, SGU_WIDTH)]
    return pl.pallas_call(
        body, grid=(T // tm,), name="mixers_bwd",
        in_specs=[row(D_MODEL), row(MLA_WIDTH)] + _z_specs(tm) + [WHOLE] * 10,
        out_specs=[row(MLA_WIDTH), head, row(256), row(256), row(256), row(D_MODEL)] + [_acc_spec(s) for s in acc_shapes],
        out_shape=[jax.ShapeDtypeStruct((T, MLA_WIDTH), F32), jax.ShapeDtypeStruct((HEADS, 1, T), F32),
                   jax.ShapeDtypeStruct((T, 256), F32), jax.ShapeDtypeStruct((T, 256), F32),
                   jax.ShapeDtypeStruct((T, 256), F32), jax.ShapeDtypeStruct((T, D_MODEL), BF16)]
                  + [jax.ShapeDtypeStruct(s, F32) for s in acc_shapes],
        compiler_params=_cparams(),
    )(dx1, a, z, z, z, z, gsgu, wcat, wcat_t, bz, wp, pscale, goa, gos, gop, wout)


def pool_bwd(dm):
    T = dm.shape[0]
    tm = min(ROW_TILE, T)
    nt = T // tm

    def body(dm_ref, next_ref, dpin_ref):
        i = pl.program_id(0)
        cnt, lane = _pool_counts(i, tm)
        dmv = dm_ref[...]
        win = _by_group(lane[:POOL_HALO], 2.0, 4.0, 8.0, 16.0)
        nxt = jnp.where(i < nt - 1, next_ref[...] / win, 0.0)
        r1 = jnp.concatenate([dmv / cnt, nxt], axis=0)
        n = tm + POOL_HALO
        r2 = r1 + pltpu.roll(r1, n - 1, 0)
        r4 = r2 + pltpu.roll(r2, n - 2, 0)
        r8 = r4 + pltpu.roll(r4, n - 4, 0)
        r16 = r8 + pltpu.roll(r8, n - 8, 0)
        dpin_ref[...] = _by_group(lane, r2[:tm], r4[:tm], r8[:tm], r16[:tm]) - dmv

    return pl.pallas_call(
        body, grid=(nt,), name="pool_bwd",
        in_specs=[pl.BlockSpec((tm, 256), lambda i: (i, 0)),
                  pl.BlockSpec((POOL_HALO, 256), lambda i: (jnp.minimum((i + 1) * (tm // POOL_HALO), T // POOL_HALO - 1), 0))],
        out_specs=pl.BlockSpec((tm, 256), lambda i: (i, 0)),
        out_shape=jax.ShapeDtypeStruct((T, 256), F32),
        compiler_params=_cparams(),
    )(dm, dm)


def attn_bwd(q, k, kt, v, do, lse, delta):
    _, T, _ = q.shape
    tb = min(ATTN_TILE, T)
    sb = min(ATTN_SUB, tb)
    ns = tb // sb
    nb = T // tb
    hb = ATTN_BWD_HEADS

    def body(q_ref, k_ref, kt_ref, v_ref, do_ref, lse_ref, delta_ref, dqt_ref, dk_ref, dv_ref, dk_s, dv_s):
        i, j = pl.program_id(1), pl.program_id(2)

        @pl.when((i == 0) & (j == 0))
        def _():
            dqt_ref[...] = jnp.zeros(dqt_ref.shape, F32)

        @pl.when(j == i)
        def _():
            dk_s[...] = jnp.zeros(dk_s.shape, F32)
            dv_s[...] = jnp.zeros(dv_s.shape, F32)

        def sub_block(h, a, b, masked):
            qa = slice(sb * a, sb * (a + 1))
            kb = slice(sb * b, sb * (b + 1))
            qv = q_ref[h, qa, :]
            dov = do_ref[qa, HEAD_PAD * h:HEAD_PAD * (h + 1)].astype(BF16)
            st = _dot_nt(k_ref[h, kb, :], qv)
            pt = jnp.exp2(st - lse_ref[h, :, qa])
            if masked:
                krow = lax.broadcasted_iota(jnp.int32, st.shape, 0)
                qcol = lax.broadcasted_iota(jnp.int32, st.shape, 1)
                pt = jnp.where(krow <= qcol, pt, 0.0)
            dv_s[h, kb, :] += _dot(pt.astype(BF16), dov)
            dpt = _dot_nt(v_ref[h, kb, :], dov)
            dst = (pt * (dpt - delta_ref[h, :, qa])).astype(BF16)
            dk_s[h, kb, :] += _dot(dst, qv)
            cols = pl.ds(pl.multiple_of(j * tb + sb * a, sb), sb)
            dqt_ref[h, :, cols] += _dot(kt_ref[h, :, kb], dst)

        @pl.when(j > i)
        def _():
            for a in range(ns):
                for h in range(hb):
                    for b in range(ns):
                        sub_block(h, a, b, False)

        @pl.when(j == i)
        def _():
            for a in range(ns):
                for h in range(hb):
                    for b in range(a + 1):
                        sub_block(h, a, b, a == b)

        @pl.when(j == nb - 1)
        def _():
            dk_ref[...] = dk_s[...] * (1.0 / LOG2E)
            dv_ref[...] = dv_s[...]

    qspec = pl.BlockSpec((hb, tb, HEAD_PAD), lambda g, i, j: (g, jnp.maximum(i, j), 0))
    kspec = pl.BlockSpec((hb, tb, HEAD_PAD), lambda g, i, j: (g, i, 0))
    ktspec = pl.BlockSpec((hb, HEAD_PAD, tb), lambda g, i, j: (g, 0, i))
    rowspec = pl.BlockSpec((hb, 1, tb), lambda g, i, j: (g, 0, jnp.maximum(i, j)))
    return pl.pallas_call(
        body, grid=(HEADS // hb, nb, nb), name="attn_bwd",
        in_specs=[qspec, kspec, ktspec, kspec, pl.BlockSpec((tb, hb * HEAD_PAD), lambda g, i, j: (jnp.maximum(i, j), g)), rowspec, rowspec],
        out_specs=[pl.BlockSpec((hb, HEAD_PAD, T), lambda g, i, j: (g, 0, 0)), kspec, kspec],
        out_shape=[jax.ShapeDtypeStruct((HEADS, HEAD_PAD, T), F32)] + [jax.ShapeDtypeStruct((HEADS, T, HEAD_PAD), F32)] * 2,
        scratch_shapes=[pltpu.VMEM((hb, tb, HEAD_PAD), F32), pltpu.VMEM((hb, tb, HEAD_PAD), F32)],
        compiler_params=_cparams(),
    )(q, k, kt, v, do, lse, delta)


def mixin_bwd(dres, x, z, rope, dqt, dk, dv, du, dvs, dpin, gmix, win, gql, wq, gkvl, wkv, gqh, gkh):
    T = x.shape[0]
    tm = min(ROW_TILE, T)

    def body(dres_ref, x_ref, z_ref, c_ref, sa_ref, sb_ref, dqt_ref, dk_ref, dv_ref, du_ref, dvs_ref, dpin_ref,
             gmix_ref, win_ref, gql_ref, wq_ref, gkvl_ref, wkv_ref, gqh_ref, gkh_ref,
             dx_ref, hn_ref, dz_ref, qn_ref, dqup_ref, kvn_ref, dkvup_ref,
             dgmix_ref, dgql_ref, dgkvl_ref, dgqh_ref, dgkh_ref):
        @pl.when(pl.program_id(0) == 0)
        def _():
            for r in (dgmix_ref, dgql_ref, dgkvl_ref, dgqh_ref, dgkh_ref):
                r[...] = jnp.zeros(r.shape, F32)

        xv = x_ref[...]
        hn, rx = _rms(xv, gmix_ref[...], D_MODEL)
        hn_ref[...] = hn.astype(BF16)
        ql, kvl, kr, qn, rq, kvn, rkv, q_up, kv_up = _qkv_pre(z_ref[...], gql_ref[...], wq_ref[...], gkvl_ref[...], wkv_ref[...])
        qn_ref[...] = qn
        kvn_ref[...] = kvn
        c, sa, sb = c_ref[...], sa_ref[...], sb_ref[...]
        lane = lax.broadcasted_iota(jnp.int32, (tm, HEAD_PAD), 1)
        rope_lanes = (lane >= QK_NOPE) & (lane < QK_HEAD)
        dkr = jnp.zeros((tm, HEAD_PAD), F32)
        dgqh = jnp.zeros((1, HEAD_PAD), F32)
        dgkh = jnp.zeros((1, HEAD_PAD), F32)
        dq_parts, dk_parts, dv_parts = [], [], []
        for h in range(HEADS):
            lo = HEAD_PAD * h
            qh = q_up[:, lo:lo + HEAD_PAD]
            rqh = lax.rsqrt(jnp.sum(qh * qh, axis=-1, keepdims=True) * (1.0 / QK_HEAD) + EPS)
            dq_h = jnp.transpose(dqt_ref[h]) * ATTN_SCALE
            dqh, dg = _rms_bwd(qh, rqh, gqh_ref[...], _rope_bwd(dq_h, c, sa, sb), QK_HEAD)
            dgqh = dgqh + dg
            dq_parts.append(dqh)
            kh = kv_up[:, lo:lo + HEAD_PAD] + kr
            rkh = lax.rsqrt(jnp.sum(kh * kh, axis=-1, keepdims=True) * (1.0 / QK_HEAD) + EPS)
            dkh, dg = _rms_bwd(kh, rkh, gkh_ref[...], _rope_bwd(dk_ref[h], c, sa, sb), QK_HEAD)
            dgkh = dgkh + dg
            dkr = dkr + jnp.where(rope_lanes, dkh, 0.0)
            dk_parts.append(dkh)
            dv_parts.append(dv_ref[h])
        dgqh_ref[...] += dgqh
        dgkh_ref[...] += dgkh
        dq_up = jnp.concatenate(dq_parts, axis=1).astype(BF16)
        dkv_up = jnp.concatenate(dk_parts + dv_parts, axis=1).astype(BF16)
        dqup_ref[...] = dq_up
        dkvup_ref[...] = dkv_up
        dql, dg = _rms_bwd(ql, rq, gql_ref[...], _dot_nt(dq_up, wq_ref[...]), Q_LORA)
        dgql_ref[...] += dg
        dkvl, dg = _rms_bwd(kvl, rkv, gkvl_ref[...], _dot_nt(dkv_up, wkv_ref[...]), KV_LORA)
        dgkvl_ref[...] += dg
        dz = jnp.concatenate([dql, dkvl, dkr, du_ref[...], dvs_ref[...], dpin_ref[...]], axis=1).astype(BF16)
        dz_ref[...] = dz
        dxn, dg = _rms_bwd(xv, rx, gmix_ref[...], _dot_nt(dz, win_ref[...]), D_MODEL)
        dgmix_ref[...] += dg
        dx_ref[...] = dres_ref[...] + dxn

    row = lambda w: pl.BlockSpec((tm, w), lambda i: (i, 0))
    head = pl.BlockSpec((HEADS, tm, HEAD_PAD), lambda i: (0, i, 0))
    head_t = pl.BlockSpec((HEADS, HEAD_PAD, tm), lambda i: (0, 0, i))
    acc_shapes = [(1, D_MODEL), (1, Q_LORA), (1, KV_LORA), (1, HEAD_PAD), (1, HEAD_PAD)]
    out_rows = [(D_MODEL, F32), (D_MODEL, BF16), (Z_WIDTH, BF16), (Q_LORA, BF16), (HEADS * HEAD_PAD, BF16),
                (KV_LORA, BF16), (2 * HEADS * HEAD_PAD, BF16)]
    return pl.pallas_call(
        body, grid=(T // tm,), name="mixin_bwd",
        in_specs=[row(D_MODEL), row(D_MODEL), row(Z_WIDTH)] + [row(HEAD_PAD)] * 3 + [head_t, head, head, row(256), row(256), row(256)]
                 + [WHOLE] * 8,
        out_specs=[row(w) for w, _ in out_rows] + [_acc_spec(s) for s in acc_shapes],
        out_shape=[jax.ShapeDtypeStruct((T, w), dt) for w, dt in out_rows] + [jax.ShapeDtypeStruct(s, F32) for s in acc_shapes],
        compiler_params=_cparams(),
    )(dres, x, z, *rope, dqt, dk, dv, du, dvs, dpin, gmix, win, gql, wq, gkvl, wkv, gqh, gkh)


def _place():
    x, y, c = lax.axis_index("x"), lax.axis_index("y"), lax.axis_index("c")
    return x, y, c, 4 * x + 2 * y + c


def allgather_weights(shards):
    n = len(shards)

    def body(*refs):
        ins, outs = refs[:n], refs[n:2 * n]
        stage = refs[2 * n:3 * n]
        send_sems, recv_sems, local_sems = refs[3 * n:]
        x, y, c, me = _place()
        sibling = (x, y, 1 - c)
        chips = [(1 - x, y), (x, 1 - y), (1 - x, 1 - y)]

        def copy(k, w, block_id, to, from_stage):
            return pltpu.make_async_remote_copy(
                src_ref=stage[w] if from_stage else outs[w].at[block_id], dst_ref=outs[w].at[block_id],
                send_sem=send_sems.at[k, w], recv_sem=recv_sems.at[k, w], device_id=to, device_id_type=MESH)

        def block_of(cx, cy, cc):
            return 4 * cx + 2 * cy + cc

        local, sent = [], []
        for w in range(n):
            stage[w][...] = ins[w][...].astype(BF16)
            mine = pltpu.make_async_copy(stage[w], outs[w].at[me], local_sems.at[w])
            mine.start()
            local.append(mine)
            first = [copy(0, w, me, sibling, True)] + [copy(1 + j, w, me, (*chip, c), True) for j, chip in enumerate(chips)]
            for cp in first:
                cp.start()
            sent += first
        for j, chip in enumerate(chips):
            for w in range(n):
                copy(1 + j, w, block_of(*chip, c), (x, y, c), False).wait_recv()
                fwd = copy(4 + j, w, block_of(*chip, c), sibling, False)
                fwd.start()
                sent.append(fwd)
        for w in range(n):
            copy(0, w, block_of(x, y, 1 - c), (x, y, c), False).wait_recv()
            for j, chip in enumerate(chips):
                copy(4 + j, w, block_of(*chip, 1 - c), (x, y, c), False).wait_recv()
        for cp in local:
            cp.wait()
        for cp in sent:
            cp.wait_send()

    return pl.pallas_call(
        body, name="allgather_weights",
        in_specs=[WHOLE] * n, out_specs=[ANY] * n,
        out_shape=[jax.ShapeDtypeStruct((N_DEV,) + s.shape, BF16) for s in shards],
        scratch_shapes=[pltpu.VMEM(s.shape, BF16) for s in shards]
                       + [pltpu.SemaphoreType.DMA((7, n)), pltpu.SemaphoreType.DMA((7, n)), pltpu.SemaphoreType.DMA((n,))],
        compiler_params=_cparams(),
    )(*shards)


def exchange_grads(pieces):
    n = len(pieces)

    def body(*refs):
        ins, outs = refs[:n], refs[n:2 * n]
        send_sems, recv_sems, local_sems = refs[2 * n:]
        x, y, c, me = _place()
        local = [pltpu.make_async_copy(ins[w].at[me], outs[w].at[me], local_sems.at[w]) for w in range(n)]
        for cp in local:
            cp.start()
        remote = []
        for k in range(1, N_DEV):
            px = 1 - x if k & 4 else x
            py = 1 - y if k & 2 else y
            pc = 1 - c if k & 1 else c
            peer = 4 * px + 2 * py + pc
            for w in range(n):
                cp = pltpu.make_async_remote_copy(
                    src_ref=ins[w].at[peer], dst_ref=outs[w].at[me], send_sem=send_sems.at[k - 1, w],
                    recv_sem=recv_sems.at[k - 1, w], device_id=(px, py, pc), device_id_type=MESH)
                cp.start()
                remote.append(cp)
        for cp in local:
            cp.wait()
        for cp in remote:
            cp.wait_recv()
        for cp in remote:
            cp.wait_send()

    return pl.pallas_call(
        body, name="exchange_grads",
        in_specs=[ANY] * n, out_specs=[ANY] * n,
        out_shape=[jax.ShapeDtypeStruct(p.shape, BF16) for p in pieces],
        scratch_shapes=[pltpu.SemaphoreType.DMA((7, n)), pltpu.SemaphoreType.DMA((7, n)), pltpu.SemaphoreType.DMA((n,))],
        compiler_params=_cparams(),
    )(*pieces)


def _adamw(w, g, m, v):
    m2 = ADAM_B1 * m + (1.0 - ADAM_B1) * g
    v2 = ADAM_B2 * v + (1.0 - ADAM_B2) * (g * g)
    m_hat = m2 / (1.0 - ADAM_B1 ** ADAM_STEP)
    v_hat = v2 / (1.0 - ADAM_B2 ** ADAM_STEP)
    delta = -ADAM_LR * (m_hat / (jnp.sqrt(v_hat) + ADAM_EPS) + ADAM_WD * w)
    return delta, m2, v2


def adamw_sharded(parts, w, m, v, name):
    _, R, C = parts.shape
    br = max(d for d in range(16, min(R, 512) + 1, 16) if R % d == 0)

    def body(p_ref, w_ref, m_ref, v_ref, g_ref, d_ref, m2_ref, v2_ref):
        g = p_ref[0].astype(F32)
        for s in range(1, N_DEV):
            g = g + p_ref[s].astype(F32)
        g_ref[...] = g
        d_ref[...], m2_ref[...], v2_ref[...] = _adamw(w_ref[...], g, m_ref[...], v_ref[...])

    blk = pl.BlockSpec((br, C), lambda i: (i, 0))
    return pl.pallas_call(
        body, grid=(R // br,), name=name,
        in_specs=[pl.BlockSpec((N_DEV, br, C), lambda i: (0, i, 0)), blk, blk, blk],
        out_specs=[blk] * 4,
        out_shape=[jax.ShapeDtypeStruct((R, C), F32)] * 4,
        compiler_params=_cparams(),
    )(parts, w, m, v)


def allreduce_adamw_small(gpart, w, m, v):
    R = gpart.shape[0]

    def body(g_ref, w_ref, m_ref, v_ref, grad_ref, d_ref, m2_ref, v2_ref, all_ref, send_sems, recv_sems):
        x, y, c, me = _place()
        sibling = (x, y, 1 - c)
        chips = [(1 - x, y), (x, 1 - y), (1 - x, 1 - y)]

        def copy(k, block_id, to, from_input):
            return pltpu.make_async_remote_copy(
                src_ref=g_ref if from_input else all_ref.at[block_id], dst_ref=all_ref.at[block_id],
                send_sem=send_sems.at[k], recv_sem=recv_sems.at[k], device_id=to, device_id_type=MESH)

        def block_of(cx, cy, cc):
            return 4 * cx + 2 * cy + cc

        all_ref[me] = g_ref[...]
        first = [copy(0, me, sibling, True)] + [copy(1 + j, me, (*chip, c), True) for j, chip in enumerate(chips)]
        for cp in first:
            cp.start()
        passed = [copy(4 + j, block_of(*chip, c), sibling, False) for j, chip in enumerate(chips)]
        for j, chip in enumerate(chips):
            copy(1 + j, block_of(*chip, c), (x, y, c), False).wait_recv()
            passed[j].start()
        copy(0, block_of(x, y, 1 - c), (x, y, c), False).wait_recv()
        for j, chip in enumerate(chips):
            copy(4 + j, block_of(*chip, 1 - c), (x, y, c), False).wait_recv()
        for cp in first + passed:
            cp.wait_send()
        g = all_ref[0]
        for s in range(1, N_DEV):
            g = g + all_ref[s]
        grad_ref[...] = g
        d_ref[...], m2_ref[...], v2_ref[...] = _adamw(w_ref[...], g, m_ref[...], v_ref[...])

    return pl.pallas_call(
        body, name="allreduce_adamw_small",
        in_specs=[WHOLE] * 4, out_specs=[WHOLE] * 4,
        out_shape=[jax.ShapeDtypeStruct((R, 128), F32)] * 4,
        scratch_shapes=[pltpu.VMEM((N_DEV, R, 128), F32), pltpu.SemaphoreType.DMA((7,)), pltpu.SemaphoreType.DMA((7,))],
        compiler_params=_cparams(),
    )(gpart, w, m, v)


def _cols_from_pieces(p):
    return p.transpose(1, 0, 2).reshape(p.shape[1], -1)


def _pieces_from_cols(w):
    return w.reshape(w.shape[0], N_DEV, -1).transpose(1, 0, 2)


def _pad_heads(w, width):
    r = w.shape[0]
    return jnp.pad(w.reshape(r, HEADS, width), ((0, 0), (0, 0), (0, HEAD_PAD - width))).reshape(r, HEADS * HEAD_PAD)


def _unpad_heads(w, width):
    r = w.shape[0]
    return w.reshape(r, HEADS, HEAD_PAD)[:, :, :width].reshape(r, HEADS * width)


O1 = Q_LORA
O2 = O1 + KV_LORA
O3 = O2 + QK_ROPE


def _layer_weights(gw, l):
    w_in = _cols_from_pieces(gw["w_in"][:, l])
    zero = lambda n: jnp.zeros((D_MODEL, n), BF16)
    win = jnp.concatenate([w_in[:, :O2], zero(QK_NOPE), w_in[:, O2:O3], zero(HEAD_PAD - QK_HEAD), w_in[:, O3:]], axis=1)
    wq = _pad_heads(_cols_from_pieces(gw["w_q_up"][:, l]), QK_HEAD)
    w_kv = _cols_from_pieces(gw["w_kv_up"][:, l]).reshape(KV_LORA, HEADS, QK_NOPE + V_HEAD)
    wk = jnp.pad(w_kv[:, :, :QK_NOPE], ((0, 0), (0, 0), (0, HEAD_PAD - QK_NOPE))).reshape(KV_LORA, HEADS * HEAD_PAD)
    wv = w_kv[:, :, QK_NOPE:].reshape(KV_LORA, HEADS * V_HEAD)
    wkv = jnp.concatenate([wk, wv], axis=1)
    wout = gw["w_out"][:, l].reshape(D_MODEL, D_MODEL)
    wg = _cols_from_pieces(gw["w_gate"][:, l])
    wu = _cols_from_pieces(gw["w_up"][:, l])
    wd = gw["w_down"][:, l].reshape(FFN_HIDDEN, D_MODEL)
    return dict(win=win, wq=wq, wkv=wkv, wout=wout, wg=wg, wu=wu, wd=wd)


def _layer_small(p, l):
    row = lambda a: a.reshape(1, -1)
    pad_head = lambda g: jnp.pad(g, (0, HEAD_PAD - QK_HEAD)).reshape(1, HEAD_PAD)
    tril = jnp.tril(jnp.ones((CHUNK, CHUNK), F32))
    wsp = p["w_spatial"][l] * tril
    wcat = jnp.concatenate([wsp[h] for h in range(HEADS)], axis=1).astype(BF16)
    wcat_t = jnp.concatenate([wsp[h].T for h in range(HEADS)], axis=1).astype(BF16)
    bz = jnp.repeat(p["b_spatial"][l].T, SGU_HEAD_DIM, axis=1)
    wp = jax.scipy.linalg.block_diag(*[p["w_pool"][l][g] for g in range(HEADS)]).astype(BF16)
    return dict(gmix=row(p["g_mix_norm"][l]), gql=row(p["g_q_lat"][l]), gkvl=row(p["g_kv_lat"][l]),
                gqh=pad_head(p["g_q_head"][l]), gkh=pad_head(p["g_k_head"][l]), gsgu=row(p["g_sgu_v"][l]),
                wcat=wcat, wcat_t=wcat_t, bz=bz, wp=wp, pscale=row(p["pool_scale"][l]),
                goa=row(p["g_out_mla"][l]), gos=row(p["g_out_sgu"][l]), gop=row(p["g_out_pool"][l]),
                gffn=row(p["g_ffn_norm"][l]))


SHARDED = ("w_in", "w_q_up", "w_kv_up", "w_out", "w_gate", "w_up", "w_down")
SMALL = ("g_mix_norm", "g_q_lat", "g_kv_lat", "g_q_head", "g_k_head", "g_sgu_v", "w_spatial", "b_spatial", "w_pool",
         "pool_scale", "g_out_mla", "g_out_sgu", "g_out_pool", "g_ffn_norm")
WEIGHTS = ("g_mix_norm", "w_in", "g_q_lat", "w_q_up", "g_kv_lat", "w_kv_up", "g_q_head", "g_k_head", "g_sgu_v", "w_spatial",
           "b_spatial", "w_pool", "pool_scale", "g_out_mla", "g_out_sgu", "g_out_pool", "w_out", "g_ffn_norm", "w_gate",
           "w_up", "w_down")
PACK_ROWS = 8 * 128


def _pack_small(parts):
    flat = []
    for name in SMALL:
        a = parts[name].reshape(-1)
        flat.append(jnp.pad(a, (0, -a.shape[0] % PACK_ROWS)))
    return jnp.concatenate(flat).reshape(-1, 128)


def _unpack_small(packed, like):
    out, row = {}, 0
    for name in SMALL:
        n = math.prod(like[name].shape)
        rows = -(-n // PACK_ROWS) * 8
        out[name] = packed[row:row + rows].reshape(-1)[:n].reshape(like[name].shape)
        row += rows
    return out


def _forward_layer(x, rope, W, S):
    z, q, k, v, kt, vt = mixin_fwd(x, rope, S["gmix"], W["win"], S["gql"], W["wq"], S["gkvl"], W["wkv"], S["gqh"], S["gkh"])
    a, lse = attn_fwd(q, k, vt)
    x1 = mixers_fwd(x, a, z, S["gsgu"], S["wcat"], S["bz"], S["wp"], S["pscale"], S["goa"], S["gos"], S["gop"], W["wout"])
    x2, gs, us = ffn_fwd(x1, S["gffn"], W["wg"], W["wu"], W["wd"])
    return x2, dict(x=x, z=z, q=q, k=k, kt=kt, v=v, a=a, lse=lse, x1=x1, gs=gs, us=us)


def _backward_layer(dx2, rope, W, S, A, l):
    dx1, h2, act, dg, du_ffn, dgffn = ffn_bwd(dx2, A["x1"], A["gs"], A["us"], S["gffn"], W["wg"], W["wu"], W["wd"])
    d_wd = matmul_tn(act, dx2, f"dw_down_{l}")
    d_wg = matmul_tn(h2, dg, f"dw_gate_{l}")
    d_wu = matmul_tn(h2, du_ffn, f"dw_up_{l}")
    (da, delta, du, dvs, dm, mix, dgoa, dgos, dgop, dps, dwp, dwsp, db, dgsgu) = mixers_bwd(
        dx1, A["a"], A["z"], S["gsgu"], S["wcat"], S["wcat_t"], S["bz"], S["wp"], S["pscale"], S["goa"], S["gos"],
        S["gop"], W["wout"])
    d_wout = matmul_tn(mix, dx1, f"dw_out_{l}")
    dpin = pool_bwd(dm)
    dqt, dk, dv = attn_bwd(A["q"], A["k"], A["kt"], A["v"], da, A["lse"], delta)
    (dx, hn, dz, qn, dq_up, kvn, dkv_up, dgmix, dgql, dgkvl, dgqh, dgkh) = mixin_bwd(
        dx1, A["x"], A["z"], rope, dqt, dk, dv, du, dvs, dpin, S["gmix"], W["win"], S["gql"], W["wq"], S["gkvl"],
        W["wkv"], S["gqh"], S["gkh"])
    d_win = matmul_tn(hn, dz, f"dw_in_{l}")
    d_wq = matmul_tn(qn, dq_up, f"dw_q_up_{l}")
    d_wkv = matmul_tn(kvn, dkv_up, f"dw_kv_up_{l}")
    d_win = jnp.concatenate([d_win[:, :O2], d_win[:, O2 + QK_NOPE:O2 + QK_HEAD], d_win[:, O2 + HEAD_PAD:]], axis=1)
    d_wk = d_wkv[:, :HEADS * HEAD_PAD].reshape(KV_LORA, HEADS, HEAD_PAD)[:, :, :QK_NOPE]
    d_wv = d_wkv[:, HEADS * HEAD_PAD:].reshape(KV_LORA, HEADS, V_HEAD)
    d_wkv = jnp.concatenate([d_wk, d_wv], axis=2).reshape(KV_LORA, HEADS * (QK_NOPE + V_HEAD))
    big = dict(w_in=_pieces_from_cols(d_win), w_q_up=_pieces_from_cols(_unpad_heads(d_wq, QK_HEAD)),
               w_kv_up=_pieces_from_cols(d_wkv), w_out=d_wout.reshape(N_DEV, -1, D_MODEL),
               w_gate=_pieces_from_cols(d_wg), w_up=_pieces_from_cols(d_wu), w_down=d_wd.reshape(N_DEV, -1, D_MODEL))
    tril = jnp.tril(jnp.ones((CHUNK, CHUNK), F32))
    small = dict(g_mix_norm=dgmix[0], g_q_lat=dgql[0], g_kv_lat=dgkvl[0], g_q_head=dgqh[0, :QK_HEAD], g_k_head=dgkh[0, :QK_HEAD],
                 g_sgu_v=dgsgu[0], w_spatial=dwsp.reshape(HEADS, CHUNK, CHUNK) * tril, b_spatial=db[:, :HEADS].T,
                 w_pool=jnp.stack([dwp[64 * g:64 * (g + 1), 64 * g:64 * (g + 1)] for g in range(HEADS)]),
                 pool_scale=dps[0], g_out_mla=dgoa[0], g_out_sgu=dgos[0], g_out_pool=dgop[0], g_ffn_norm=dgffn[0])
    return dx, big, small


def kernel(x, positions, g_mix_norm, w_in, g_q_lat, w_q_up, g_kv_lat, w_kv_up, g_q_head, g_k_head, g_sgu_v, w_spatial, b_spatial, w_pool, pool_scale, g_out_mla, g_out_sgu, g_out_pool, w_out, g_ffn_norm, w_gate, w_up, w_down, loss_target, m_g_mix_norm, m_w_in, m_g_q_lat, m_w_q_up, m_g_kv_lat, m_w_kv_up, m_g_q_head, m_g_k_head, m_g_sgu_v, m_w_spatial, m_b_spatial, m_w_pool, m_pool_scale, m_g_out_mla, m_g_out_sgu, m_g_out_pool, m_w_out, m_g_ffn_norm, m_w_gate, m_w_up, m_w_down, v_g_mix_norm, v_w_in, v_g_q_lat, v_w_q_up, v_g_kv_lat, v_w_kv_up, v_g_q_head, v_g_k_head, v_g_sgu_v, v_w_spatial, v_b_spatial, v_w_pool, v_pool_scale, v_g_out_mla, v_g_out_sgu, v_g_out_pool, v_w_out, v_g_ffn_norm, v_w_gate, v_w_up, v_w_down):
    given = dict(locals())
    w = {n: given[n] for n in WEIGHTS}
    m = {n: given["m_" + n] for n in WEIGHTS}
    v = {n: given["v_" + n] for n in WEIGHTS}
    T = x.shape[1]
    xs = x.reshape(T, D_MODEL)

    half = QK_ROPE // 2
    inv_freq = 1.0 / (ROPE_THETA ** (jnp.arange(half, dtype=F32) / half))
    ang16 = positions.reshape(T).astype(F32)[:, None] * inv_freq
    ang = jnp.concatenate([jnp.zeros((T, QK_NOPE), F32), ang16, ang16, jnp.zeros((T, HEAD_PAD - QK_HEAD), F32)], axis=1)
    rope = rope_tables(ang)

    gathered = dict(zip(SHARDED, allgather_weights([w[n] for n in SHARDED])))
    Ws = [_layer_weights(gathered, l) for l in range(DEPTH)]
    Ss = [_layer_small(w, l) for l in range(DEPTH)]

    acts = []
    h = xs
    for l in range(DEPTH):
        h, A = _forward_layer(h, rope, Ws[l], Ss[l])
        acts.append(A)
    dh, loss_part = loss_head(h, loss_target.reshape(T, D_MODEL))
    loss = lax.psum(loss_part[0, 0], ("x", "y", "c"))

    bigs, smalls = [None] * DEPTH, [None] * DEPTH
    for l in reversed(range(DEPTH)):
        dh, bigs[l], smalls[l] = _backward_layer(dh, rope, Ws[l], Ss[l], acts[l], l)
    grad_x = dh.reshape(x.shape)

    pieces = [jnp.stack([bigs[l][n] for l in range(DEPTH)], axis=1) for n in SHARDED]
    received = exchange_grads(pieces)
    grad, delta, new_m, new_v = {}, {}, {}, {}
    for n, parts in zip(SHARDED, received):
        cols = w[n].shape[-1]
        flat = lambda a: a.reshape(-1, cols)
        outs = adamw_sharded(parts.reshape(N_DEV, -1, cols), flat(w[n]), flat(m[n]), flat(v[n]), f"adamw_{n}")
        grad[n], delta[n], new_m[n], new_v[n] = [o.reshape(w[n].shape) for o in outs]

    small_part = {n: jnp.stack([smalls[l][n] for l in range(DEPTH)]) for n in SMALL}
    outs = allreduce_adamw_small(_pack_small(small_part), _pack_small(w), _pack_small(m), _pack_small(v))
    for d, o in zip((grad, delta, new_m, new_v), outs):
        d.update(_unpack_small(o, w))

    return (loss, grad_x, *[grad[n] for n in WEIGHTS], *[delta[n] for n in WEIGHTS], *[new_m[n] for n in WEIGHTS],
            *[new_v[n] for n in WEIGHTS])
```

```python
import functools
import math

import jax
import jax.numpy as jnp
from jax import lax
from jax.experimental import pallas as pl
from jax.experimental.pallas import tpu as pltpu

F32 = jnp.float32
BF16 = jnp.bfloat16

N_DEV = 8
DEPTH = 2
D_MODEL = 1024
HEADS = 4
HEAD_PAD = 128
QK_NOPE = 64
QK_ROPE = 32
QK_HEAD = QK_NOPE + QK_ROPE
V_HEAD = 128
Q_LORA = 256
KV_LORA = 128
SGU_WIDTH = 256
SGU_HEAD_DIM = 64
CHUNK = 128
POOL_WIDTH = 256
POOL_WINDOWS = (2, 4, 8, 16)
POOL_HALO = 16
MLA_WIDTH = 512
IN_WIDTH = 1184
Z_WIDTH = 1280
FFN_HIDDEN = 2816
FFN_CHUNK = 256
ROPE_THETA = 10000.0
EPS = 1e-6
ATTN_SCALE = 1.0 / math.sqrt(QK_HEAD)
LOG2E = 1.4426950408889634
NEG_BIG = -1e30

ADAM_LR = 0.001
ADAM_B1 = 0.9
ADAM_B2 = 0.999
ADAM_EPS = 1e-08
ADAM_WD = 0.01
ADAM_STEP = 10

VMEM_LIMIT = 56 * 1024 * 1024
ROW_TILE = 512
ATTN_TILE = 1024
ATTN_SUB = 512
ATTN_BWD_HEADS = 2
MESH = pl.DeviceIdType.MESH

WHOLE = pl.BlockSpec(memory_space=pltpu.VMEM)
ANY = pl.BlockSpec(memory_space=pl.ANY)
HBM_SPEC = pl.BlockSpec(memory_space=pltpu.HBM)
SEM_SPEC = pl.BlockSpec(memory_space=pltpu.SEMAPHORE)


def _cparams(**kw):
    return pltpu.CompilerParams(vmem_limit_bytes=VMEM_LIMIT, **kw)


def _dot(a, b):
    return jnp.dot(a, b, preferred_element_type=F32)


def _dot_nt(a, b):
    return lax.dot_general(a, b, (((1,), (1,)), ((), ())), preferred_element_type=F32)


def _dot_tn(a, b):
    return lax.dot_general(a, b, (((0,), (0,)), ((), ())), preferred_element_type=F32)


def _rms(x, g, n):
    r = lax.rsqrt(jnp.sum(x * x, axis=-1, keepdims=True) * (1.0 / n) + EPS)
    return x * r * g, r


def _rms_bwd(x, r, g, dy, n):
    gdy = dy * g
    dx = r * gdy - x * (r * r * r) * (jnp.sum(x * gdy, axis=-1, keepdims=True) * (1.0 / n))
    dg = jnp.sum(dy * (x * r), axis=0, keepdims=True)
    return dx, dg


def _sigmoid(x):
    return 1.0 / (1.0 + jnp.exp(-x))


def rope_tables(ang):
    T = ang.shape[0]
    tm = min(ROW_TILE, T)

    def body(ang_ref, c_ref, sa_ref, sb_ref):
        a = ang_ref[...]
        lane = lax.broadcasted_iota(jnp.int32, a.shape, 1)
        s = jnp.sin(a)
        c_ref[...] = jnp.cos(a)
        sa_ref[...] = jnp.where(lane < QK_NOPE + QK_ROPE // 2, -s, 0.0)
        sb_ref[...] = jnp.where(lane >= QK_NOPE + QK_ROPE // 2, s, 0.0)

    row = pl.BlockSpec((tm, HEAD_PAD), lambda i: (i, 0))
    return pl.pallas_call(
        body, grid=(T // tm,), name="rope_tables", in_specs=[row], out_specs=[row] * 3,
        out_shape=[jax.ShapeDtypeStruct((T, HEAD_PAD), F32)] * 3, compiler_params=_cparams(),
    )(ang)


def _rope(x, c, sa, sb):
    half = QK_ROPE // 2
    return x * c + pltpu.roll(x, HEAD_PAD - half, 1) * sa + pltpu.roll(x, half, 1) * sb


def _rope_bwd(dy, c, sa, sb):
    half = QK_ROPE // 2
    return dy * c + pltpu.roll(dy * sa, half, 1) + pltpu.roll(dy * sb, HEAD_PAD - half, 1)


def _head_masks(shape, width):
    lane = lax.broadcasted_iota(jnp.int32, shape, len(shape) - 1)
    return [(lane >= width * h) & (lane < width * (h + 1)) for h in range(HEADS)]


def _qkv_pre(z, gql, wq, gkvl, wkv):
    ql = z[:, 0:Q_LORA]
    kvl = z[:, Q_LORA:Q_LORA + KV_LORA]
    kr = z[:, Q_LORA + KV_LORA:Q_LORA + KV_LORA + HEAD_PAD]
    qn, rq = _rms(ql, gql, Q_LORA)
    kvn, rkv = _rms(kvl, gkvl, KV_LORA)
    qn = qn.astype(BF16)
    kvn = kvn.astype(BF16)
    q_up = _dot(qn, wq)
    kv_up = _dot(kvn, wkv)
    return ql, kvl, kr, qn, rq, kvn, rkv, q_up, kv_up


def mixin_fwd(x, rope, gmix, win, gql, wq, gkvl, wkv, gqh, gkh):
    T = x.shape[0]
    tm = min(ROW_TILE, T)

    def body(x_ref, c_ref, sa_ref, sb_ref, gmix_ref, win_ref, gql_ref, wq_ref, gkvl_ref, wkv_ref, gqh_ref, gkh_ref,
             z_ref, q_ref, k_ref, v_ref, kt_ref, vt_ref):
        hn = _rms(x_ref[...], gmix_ref[...], D_MODEL)[0].astype(BF16)
        z = _dot(hn, win_ref[...])
        z_ref[...] = z
        _, _, kr, _, _, _, _, q_up, kv_up = _qkv_pre(z, gql_ref[...], wq_ref[...], gkvl_ref[...], wkv_ref[...])
        c, sa, sb = c_ref[...], sa_ref[...], sb_ref[...]
        for h in range(HEADS):
            lo = HEAD_PAD * h
            qh = _rms(q_up[:, lo:lo + HEAD_PAD], gqh_ref[...], QK_HEAD)[0]
            q_ref[h] = (_rope(qh, c, sa, sb) * (ATTN_SCALE * LOG2E)).astype(BF16)
            kh = _rope(_rms(kv_up[:, lo:lo + HEAD_PAD] + kr, gkh_ref[...], QK_HEAD)[0], c, sa, sb)
            k_ref[h] = kh.astype(BF16)
            kt_ref[h] = jnp.transpose(kh).astype(BF16)
            vh = kv_up[:, HEADS * HEAD_PAD + lo:HEADS * HEAD_PAD + lo + HEAD_PAD]
            v_ref[h] = vh.astype(BF16)
            vt_ref[h] = jnp.transpose(vh).astype(BF16)

    row = lambda w: pl.BlockSpec((tm, w), lambda i: (i, 0))
    head = pl.BlockSpec((HEADS, tm, HEAD_PAD), lambda i: (0, i, 0))
    head_t = pl.BlockSpec((HEADS, HEAD_PAD, tm), lambda i: (0, 0, i))
    return pl.pallas_call(
        body, grid=(T // tm,), name="mixin_fwd",
        in_specs=[row(D_MODEL)] + [row(HEAD_PAD)] * 3 + [WHOLE] * 8,
        out_specs=[row(Z_WIDTH), head, head, head, head_t, head_t],
        out_shape=[jax.ShapeDtypeStruct((T, Z_WIDTH), F32)] + [jax.ShapeDtypeStruct((HEADS, T, HEAD_PAD), BF16)] * 3
                  + [jax.ShapeDtypeStruct((HEADS, HEAD_PAD, T), BF16)] * 2,
        compiler_params=_cparams(),
    )(x, *rope, gmix, win, gql, wq, gkvl, wkv, gqh, gkh)


def attn_fwd(q, k, vt):
    _, T, _ = q.shape
    tb = min(ATTN_TILE, T)
    sb = min(ATTN_SUB, tb)
    ns = tb // sb
    nb = T // tb

    def body(q_ref, k_ref, vt_ref, o_ref, lse_ref, m_s, l_s, acc_s):
        i, j = pl.program_id(0), pl.program_id(1)

        @pl.when(j == 0)
        def _():
            m_s[...] = jnp.full(m_s.shape, -jnp.inf, F32)
            l_s[...] = jnp.zeros(l_s.shape, F32)
            acc_s[...] = jnp.zeros(acc_s.shape, F32)

        def sub_block(h, a, b, masked):
            qa = slice(sb * a, sb * (a + 1))
            kb = slice(sb * b, sb * (b + 1))
            st = _dot_nt(k_ref[h, kb, :], q_ref[h, qa, :])
            if masked:
                krow = lax.broadcasted_iota(jnp.int32, st.shape, 0)
                qcol = lax.broadcasted_iota(jnp.int32, st.shape, 1)
                st = jnp.where(krow <= qcol, st, NEG_BIG)
            m_prev = m_s[h, :, qa]
            m_new = jnp.maximum(m_prev, jnp.max(st, axis=0, keepdims=True))
            alpha = jnp.exp2(m_prev - m_new)
            pt = jnp.exp2(st - m_new)
            l_s[h, :, qa] = alpha * l_s[h, :, qa] + jnp.sum(pt, axis=0, keepdims=True)
            acc_s[h, :, qa] = alpha * acc_s[h, :, qa] + _dot(vt_ref[h, :, kb], pt.astype(BF16))
            m_s[h, :, qa] = m_new

        @pl.when(j < i)
        def _():
            for b in range(ns):
                for h in range(HEADS):
                    for a in range(ns):
                        sub_block(h, a, b, False)

        @pl.when(j == i)
        def _():
            for b in range(ns):
                for h in range(HEADS):
                    for a in range(b, ns):
                        sub_block(h, a, b, a == b)
            for h in range(HEADS):
                l = l_s[h]
                o_ref[:, HEAD_PAD * h:HEAD_PAD * (h + 1)] = jnp.transpose(acc_s[h] / l)
                lse_ref[h] = m_s[h] + jnp.log2(l)

    qspec = pl.BlockSpec((HEADS, tb, HEAD_PAD), lambda i, j: (0, i, 0))
    kspec = pl.BlockSpec((HEADS, tb, HEAD_PAD), lambda i, j: (0, jnp.minimum(i, j), 0))
    vspec = pl.BlockSpec((HEADS, HEAD_PAD, tb), lambda i, j: (0, 0, jnp.minimum(i, j)))
    return pl.pallas_call(
        body, grid=(nb, nb), name="attn_fwd",
        in_specs=[qspec, kspec, vspec],
        out_specs=[pl.BlockSpec((tb, MLA_WIDTH), lambda i, j: (i, 0)), pl.BlockSpec((HEADS, 1, tb), lambda i, j: (0, 0, i))],
        out_shape=[jax.ShapeDtypeStruct((T, MLA_WIDTH), F32), jax.ShapeDtypeStruct((HEADS, 1, T), F32)],
        scratch_shapes=[pltpu.VMEM((HEADS, 1, tb), F32), pltpu.VMEM((HEADS, 1, tb), F32), pltpu.VMEM((HEADS, HEAD_PAD, tb), F32)],
        compiler_params=_cparams(),
    )(q, k, vt)


def _sgu_fwd_chunk(vn_c, wcat, bz, masks):
    vstack = jnp.concatenate([jnp.where(mk, vn_c, 0.0).astype(BF16) for mk in masks], axis=0)
    return _dot(wcat, vstack) + bz


def _pool_counts(i, tm):
    pos1 = (i * tm + 1 + lax.broadcasted_iota(jnp.int32, (tm, POOL_WIDTH), 0)).astype(F32)
    lane = lax.broadcasted_iota(jnp.int32, (tm, POOL_WIDTH), 1)
    win = jnp.where(lane < 64, 2.0, jnp.where(lane < 128, 4.0, jnp.where(lane < 192, 8.0, 16.0)))
    return jnp.minimum(pos1, win), lane


def _by_group(lane, s2, s4, s8, s16):
    return jnp.where(lane < 64, s2, jnp.where(lane < 128, s4, jnp.where(lane < 192, s8, s16)))


def _pool_means(pin, halo, i, tm):
    s1 = jnp.concatenate([halo, pin], axis=0)
    s2 = s1 + pltpu.roll(s1, 1, 0)
    s4 = s2 + pltpu.roll(s2, 2, 0)
    s8 = s4 + pltpu.roll(s4, 4, 0)
    s16 = s8 + pltpu.roll(s8, 8, 0)
    cnt, lane = _pool_counts(i, tm)
    sel = _by_group(lane, s2[POOL_HALO:], s4[POOL_HALO:], s8[POOL_HALO:], s16[POOL_HALO:])
    return sel / cnt - pin


def _mixers_fwd_tile(i, tm, a, u, vs, pin, halo, gsgu, wcat, bz, wp, pscale, goa, gos, gop):
    vn, rv = _rms(vs, gsgu, SGU_WIDTH)
    masks = _head_masks((CHUNK, SGU_WIDTH), SGU_HEAD_DIM)
    zc = jnp.concatenate([_sgu_fwd_chunk(vn[CHUNK * c:CHUNK * (c + 1)], wcat, bz, masks) for c in range(tm // CHUNK)], axis=0)
    gm = u * zc
    halo = jnp.where(i > 0, halo, 0.0)
    m = _pool_means(pin, halo, i, tm).astype(BF16)
    yp_pre = _dot(m, wp)
    yp = yp_pre * pscale
    na, ra = _rms(a, goa, MLA_WIDTH)
    ng, rg = _rms(gm, gos, SGU_WIDTH)
    npo, rp = _rms(yp, gop, POOL_WIDTH)
    mix = jnp.concatenate([na, ng, npo], axis=1).astype(BF16)
    return vn, rv, zc, gm, m, yp_pre, yp, ra, rg, rp, mix


def _z_specs(tm):
    col = lambda c: pl.BlockSpec((tm, 256), lambda i: (i, c))
    halo = pl.BlockSpec((POOL_HALO, 256), lambda i: (jnp.maximum(i * (tm // POOL_HALO) - 1, 0), 4))
    return [col(2), col(3), col(4), halo]


def mixers_fwd(x, a, z, gsgu, wcat, bz, wp, pscale, goa, gos, gop, wout):
    T = x.shape[0]
    tm = min(ROW_TILE, T)

    def body(x_ref, a_ref, u_ref, vs_ref, pin_ref, halo_ref, gsgu_ref, wcat_ref, bz_ref, wp_ref, ps_ref,
             goa_ref, gos_ref, gop_ref, wout_ref, x1_ref):
        i = pl.program_id(0)
        mix = _mixers_fwd_tile(i, tm, a_ref[...], u_ref[...], vs_ref[...], pin_ref[...], halo_ref[...], gsgu_ref[...],
                               wcat_ref[...], bz_ref[...], wp_ref[...], ps_ref[...], goa_ref[...], gos_ref[...],
                               gop_ref[...])[-1]
        x1_ref[...] = x_ref[...] + _dot(mix, wout_ref[...])

    row = lambda w: pl.BlockSpec((tm, w), lambda i: (i, 0))
    return pl.pallas_call(
        body, grid=(T // tm,), name="mixers_fwd",
        in_specs=[row(D_MODEL), row(MLA_WIDTH)] + _z_specs(tm) + [WHOLE] * 9,
        out_specs=row(D_MODEL),
        out_shape=jax.ShapeDtypeStruct((T, D_MODEL), F32),
        compiler_params=_cparams(),
    )(x, a, z, z, z, z, gsgu, wcat, bz, wp, pscale, goa, gos, gop, wout)


def ffn_fwd(x1, gffn, wg, wu, wd):
    T = x1.shape[0]
    tm = min(ROW_TILE, T)

    def body(x1_ref, gffn_ref, wg_ref, wu_ref, wd_ref, x2_ref, gs_ref, us_ref):
        x1v = x1_ref[...]
        h2 = _rms(x1v, gffn_ref[...], D_MODEL)[0].astype(BF16)
        acc = x1v
        for c in range(FFN_HIDDEN // FFN_CHUNK):
            sl = slice(FFN_CHUNK * c, FFN_CHUNK * (c + 1))
            g = _dot(h2, wg_ref[:, sl])
            u = _dot(h2, wu_ref[:, sl])
            gs_ref[:, sl] = g.astype(BF16)
            us_ref[:, sl] = u.astype(BF16)
            act = (g * _sigmoid(g) * u).astype(BF16)
            acc = acc + _dot(act, wd_ref[sl, :])
        x2_ref[...] = acc

    row = lambda w: pl.BlockSpec((tm, w), lambda i: (i, 0))
    return pl.pallas_call(
        body, grid=(T // tm,), name="ffn_fwd",
        in_specs=[row(D_MODEL)] + [WHOLE] * 4,
        out_specs=[row(D_MODEL), row(FFN_HIDDEN), row(FFN_HIDDEN)],
        out_shape=[jax.ShapeDtypeStruct((T, D_MODEL), F32), jax.ShapeDtypeStruct((T, FFN_HIDDEN), BF16),
                   jax.ShapeDtypeStruct((T, FFN_HIDDEN), BF16)],
        compiler_params=_cparams(),
    )(x1, gffn, wg, wu, wd)


def loss_head(y, target):
    T = y.shape[0]
    tm = min(ROW_TILE, T)

    def body(y_ref, t_ref, dy_ref, loss_ref):
        @pl.when(pl.program_id(0) == 0)
        def _():
            loss_ref[...] = jnp.zeros(loss_ref.shape, F32)

        err = y_ref[...] - t_ref[...]
        dy_ref[...] = err * (1.0 / D_MODEL)
        per_row = jnp.sum(err * err, axis=1, keepdims=True) * (1.0 / D_MODEL)
        loss_ref[...] += 0.5 * jnp.sum(per_row, axis=0, keepdims=True)

    row = pl.BlockSpec((tm, D_MODEL), lambda i: (i, 0))
    return pl.pallas_call(
        body, grid=(T // tm,), name="loss_head",
        in_specs=[row, row],
        out_specs=[row, pl.BlockSpec((1, 1), lambda i: (0, 0))],
        out_shape=[jax.ShapeDtypeStruct((T, D_MODEL), F32), jax.ShapeDtypeStruct((1, 1), F32)],
        compiler_params=_cparams(),
    )(y, target)


def _acc_spec(shape):
    return pl.BlockSpec(shape, lambda i: (0,) * len(shape))


def ffn_bwd(dx2, x1, gs, us, gffn, wg, wu, wd):
    T = x1.shape[0]
    tm = min(ROW_TILE // 2, T)

    def body(dx2_ref, x1_ref, gs_ref, us_ref, gffn_ref, wg_ref, wu_ref, wd_ref,
             dx1_ref, h2_ref, act_ref, dg_ref, du_ref, dgffn_ref):
        @pl.when(pl.program_id(0) == 0)
        def _():
            dgffn_ref[...] = jnp.zeros(dgffn_ref.shape, F32)

        dx2v = dx2_ref[...]
        dy = dx2v.astype(BF16)
        x1v = x1_ref[...]
        h2, r = _rms(x1v, gffn_ref[...], D_MODEL)
        h2_ref[...] = h2.astype(BF16)
        dh2 = jnp.zeros((tm, D_MODEL), F32)
        for c in range(FFN_HIDDEN // FFN_CHUNK):
            sl = slice(FFN_CHUNK * c, FFN_CHUNK * (c + 1))
            g = gs_ref[:, sl].astype(F32)
            u = us_ref[:, sl].astype(F32)
            dact = _dot_nt(dy, wd_ref[sl, :])
            sg = _sigmoid(g)
            silu = g * sg
            act_ref[:, sl] = (silu * u).astype(BF16)
            dg = (dact * u * (sg * (1.0 + g * (1.0 - sg)))).astype(BF16)
            du = (dact * silu).astype(BF16)
            dg_ref[:, sl] = dg
            du_ref[:, sl] = du
            dh2 = dh2 + _dot_nt(dg, wg_ref[:, sl]) + _dot_nt(du, wu_ref[:, sl])
        dxn, dgn = _rms_bwd(x1v, r, gffn_ref[...], dh2, D_MODEL)
        dx1_ref[...] = dx2v + dxn
        dgffn_ref[...] += dgn

    row = lambda w: pl.BlockSpec((tm, w), lambda i: (i, 0))
    return pl.pallas_call(
        body, grid=(T // tm,), name="ffn_bwd",
        in_specs=[row(D_MODEL), row(D_MODEL), row(FFN_HIDDEN), row(FFN_HIDDEN)] + [WHOLE] * 4,
        out_specs=[row(D_MODEL), row(D_MODEL), row(FFN_HIDDEN), row(FFN_HIDDEN), row(FFN_HIDDEN), _acc_spec((1, D_MODEL))],
        out_shape=[jax.ShapeDtypeStruct((T, D_MODEL), F32), jax.ShapeDtypeStruct((T, D_MODEL), BF16),
                   jax.ShapeDtypeStruct((T, FFN_HIDDEN), BF16), jax.ShapeDtypeStruct((T, FFN_HIDDEN), BF16),
                   jax.ShapeDtypeStruct((T, FFN_HIDDEN), BF16), jax.ShapeDtypeStruct((1, D_MODEL), F32)],
        compiler_params=_cparams(),
    )(dx2, x1, gs, us, gffn, wg, wu, wd)


TN_K_TILE = 1024
TN_ACC_BYTES = 6 * 1024 * 1024


def matmul_tn(a, b, name):
    T, M = a.shape
    N = b.shape[1]
    tk = min(TN_K_TILE, T)
    tm = M if M <= 1024 else M // 2
    tn = max(d for d in range(128, N + 1, 128) if N % d == 0 and tm * d * 4 <= TN_ACC_BYTES)
    nk = T // tk

    def body(a_ref, b_ref, o_ref, acc):
        k = pl.program_id(2)

        @pl.when(k == 0)
        def _():
            acc[...] = jnp.zeros(acc.shape, F32)

        acc[...] += _dot_tn(a_ref[...].astype(BF16), b_ref[...].astype(BF16))

        @pl.when(k == nk - 1)
        def _():
            o_ref[...] = acc[...].astype(BF16)

    return pl.pallas_call(
        body, grid=(M // tm, N // tn, nk), name=name,
        in_specs=[pl.BlockSpec((tk, tm), lambda i, j, k: (k, i)), pl.BlockSpec((tk, tn), lambda i, j, k: (k, j))],
        out_specs=pl.BlockSpec((tm, tn), lambda i, j, k: (i, j)),
        out_shape=jax.ShapeDtypeStruct((M, N), BF16),
        scratch_shapes=[pltpu.VMEM((tm, tn), F32)],
        compiler_params=_cparams(),
    )(a, b)


def mixers_bwd(dx1, a, z, gsgu, wcat, wcat_t, bz, wp, pscale, goa, gos, gop, wout):
    T = a.shape[0]
    tm = min(ROW_TILE, T)

    def body(dx1_ref, a_ref, u_ref, vs_ref, pin_ref, halo_ref, gsgu_ref, wcat_ref, wcatt_ref, bz_ref, wp_ref,
             ps_ref, goa_ref, gos_ref, gop_ref, wout_ref,
             da_ref, delta_ref, du_ref, dvs_ref, dm_ref, mix_ref,
             dgoa_ref, dgos_ref, dgop_ref, dps_ref, dwp_ref, dwsp_ref, db_ref, dgsgu_ref):
        i = pl.program_id(0)

        @pl.when(i == 0)
        def _():
            for r in (dgoa_ref, dgos_ref, dgop_ref, dps_ref, dwp_ref, dwsp_ref, db_ref, dgsgu_ref):
                r[...] = jnp.zeros(r.shape, F32)

        a_v, u, vs = a_ref[...], u_ref[...], vs_ref[...]
        goa, gos, gop, pscale_v = goa_ref[...], gos_ref[...], gop_ref[...], ps_ref[...]
        vn, rv, zc, gm, m, yp_pre, yp, ra, rg, rp, mix = _mixers_fwd_tile(
            i, tm, a_v, u, vs, pin_ref[...], halo_ref[...], gsgu_ref[...], wcat_ref[...], bz_ref[...], wp_ref[...],
            pscale_v, goa, gos, gop)
        mix_ref[...] = mix
        dmix = _dot_nt(dx1_ref[...].astype(BF16), wout_ref[...])
        da, dgoa = _rms_bwd(a_v, ra, goa, dmix[:, :MLA_WIDTH], MLA_WIDTH)
        dgm, dgos = _rms_bwd(gm, rg, gos, dmix[:, MLA_WIDTH:MLA_WIDTH + SGU_WIDTH], SGU_WIDTH)
        dyp, dgop = _rms_bwd(yp, rp, gop, dmix[:, MLA_WIDTH + SGU_WIDTH:], POOL_WIDTH)
        da_ref[...] = da
        dgoa_ref[...] += dgoa
        dgos_ref[...] += dgos
        dgop_ref[...] += dgop
        prod = da * a_v
        ones = jnp.ones((8, HEAD_PAD), F32)
        for h in range(HEADS):
            lo = HEAD_PAD * h
            sums = lax.dot_general(ones, prod[:, lo:lo + HEAD_PAD], (((1,), (1,)), ((), ())), preferred_element_type=F32,
                                   precision=lax.Precision.HIGHEST)
            delta_ref[h] = sums[0:1, :]
        dps_ref[...] += jnp.sum(dyp * yp_pre, axis=0, keepdims=True)
        dyp_pre = (dyp * pscale_v).astype(BF16)
        dwp_ref[...] += _dot_tn(m, dyp_pre)
        dm_ref[...] = _dot_nt(dyp_pre, wp_ref[...])
        du_ref[...] = dgm * zc
        dzc = dgm * u
        masks = _head_masks((CHUNK, SGU_WIDTH), SGU_HEAD_DIM)
        lane_b = lax.broadcasted_iota(jnp.int32, (CHUNK, HEAD_PAD), 1)
        dvn_parts = []
        dwsp = jnp.zeros(dwsp_ref.shape, F32)
        db = jnp.zeros(db_ref.shape, F32)
        for c in range(tm // CHUNK):
            dz_c = dzc[CHUNK * c:CHUNK * (c + 1)]
            dzstack = jnp.concatenate([jnp.where(mk, dz_c, 0.0).astype(BF16) for mk in masks], axis=0)
            dvn_parts.append(_dot(wcatt_ref[...], dzstack))
            dwsp = dwsp + _dot_nt(dzstack, vn[CHUNK * c:CHUNK * (c + 1)].astype(BF16))
            for h, mk in enumerate(masks):
                col = jnp.sum(jnp.where(mk, dz_c, 0.0), axis=1, keepdims=True)
                db = db + jnp.where(lane_b == h, col, 0.0)
        dwsp_ref[...] += dwsp
        db_ref[...] += db
        dvs, dgsgu = _rms_bwd(vs, rv, gsgu_ref[...], jnp.concatenate(dvn_parts, axis=0), SGU_WIDTH)
        dvs_ref[...] = dvs
        dgsgu_ref[...] += dgsgu

    row = lambda w: pl.BlockSpec((tm, w), lambda i: (i, 0))
    head = pl.BlockSpec((HEADS, 1, tm), lambda i: (0, 0, i))
    acc_shapes = [(1, MLA_WIDTH), (1, SGU_WIDTH), (1, POOL_WIDTH), (1, POOL_WIDTH), (POOL_WIDTH, POOL_WIDTH),
                  (HEADS * CHUNK, CHUNK), (CHUNK, HEAD_PAD), (1, SGU_WIDTH)]
    return pl.pallas_call(
        body, grid=(T // tm,), name="mixers_bwd",
        in_specs=[row(D_MODEL), row(MLA_WIDTH)] + _z_specs(tm) + [WHOLE] * 10,
        out_specs=[row(MLA_WIDTH), head, row(256), row(256), row(256), row(D_MODEL)] + [_acc_spec(s) for s in acc_shapes],
        out_shape=[jax.ShapeDtypeStruct((T, MLA_WIDTH), F32), jax.ShapeDtypeStruct((HEADS, 1, T), F32),
                   jax.ShapeDtypeStruct((T, 256), F32), jax.ShapeDtypeStruct((T, 256), F32),
                   jax.ShapeDtypeStruct((T, 256), F32), jax.ShapeDtypeStruct((T, D_MODEL), BF16)]
                  + [jax.ShapeDtypeStruct(s, F32) for s in acc_shapes],
        compiler_params=_cparams(),
    )(dx1, a, z, z, z, z, gsgu, wcat, wcat_t, bz, wp, pscale, goa, gos, gop, wout)


def pool_bwd(dm):
    T = dm.shape[0]
    tm = min(ROW_TILE, T)
    nt = T // tm

    def body(dm_ref, next_ref, dpin_ref):
        i = pl.program_id(0)
        cnt, lane = _pool_counts(i, tm)
        dmv = dm_ref[...]
        win = _by_group(lane[:POOL_HALO], 2.0, 4.0, 8.0, 16.0)
        nxt = jnp.where(i < nt - 1, next_ref[...] / win, 0.0)
        r1 = jnp.concatenate([dmv / cnt, nxt], axis=0)
        n = tm + POOL_HALO
        r2 = r1 + pltpu.roll(r1, n - 1, 0)
        r4 = r2 + pltpu.roll(r2, n - 2, 0)
        r8 = r4 + pltpu.roll(r4, n - 4, 0)
        r16 = r8 + pltpu.roll(r8, n - 8, 0)
        dpin_ref[...] = _by_group(lane, r2[:tm], r4[:tm], r8[:tm], r16[:tm]) - dmv

    return pl.pallas_call(
        body, grid=(nt,), name="pool_bwd",
        in_specs=[pl.BlockSpec((tm, 256), lambda i: (i, 0)),
                  pl.BlockSpec((POOL_HALO, 256), lambda i: (jnp.minimum((i + 1) * (tm // POOL_HALO), T // POOL_HALO - 1), 0))],
        out_specs=pl.BlockSpec((tm, 256), lambda i: (i, 0)),
        out_shape=jax.ShapeDtypeStruct((T, 256), F32),
        compiler_params=_cparams(),
    )(dm, dm)


def attn_bwd(q, k, kt, v, do, lse, delta):
    _, T, _ = q.shape
    tb = min(ATTN_TILE, T)
    sb = min(ATTN_SUB, tb)
    ns = tb // sb
    nb = T // tb
    hb = ATTN_BWD_HEADS

    def body(q_ref, k_ref, kt_ref, v_ref, do_ref, lse_ref, delta_ref, dqt_ref, dk_ref, dv_ref, dk_s, dv_s):
        i, j = pl.program_id(1), pl.program_id(2)

        @pl.when((i == 0) & (j == 0))
        def _():
            dqt_ref[...] = jnp.zeros(dqt_ref.shape, F32)

        @pl.when(j == i)
        def _():
            dk_s[...] = jnp.zeros(dk_s.shape, F32)
            dv_s[...] = jnp.zeros(dv_s.shape, F32)

        def sub_block(h, a, b, masked):
            qa = slice(sb * a, sb * (a + 1))
            kb = slice(sb * b, sb * (b + 1))
            qv = q_ref[h, qa, :]
            dov = do_ref[qa, HEAD_PAD * h:HEAD_PAD * (h + 1)].astype(BF16)
            st = _dot_nt(k_ref[h, kb, :], qv)
            pt = jnp.exp2(st - lse_ref[h, :, qa])
            if masked:
                krow = lax.broadcasted_iota(jnp.int32, st.shape, 0)
                qcol = lax.broadcasted_iota(jnp.int32, st.shape, 1)
                pt = jnp.where(krow <= qcol, pt, 0.0)
            dv_s[h, kb, :] += _dot(pt.astype(BF16), dov)
            dpt = _dot_nt(v_ref[h, kb, :], dov)
            dst = (pt * (dpt - delta_ref[h, :, qa])).astype(BF16)
            dk_s[h, kb, :] += _dot(dst, qv)
            cols = pl.ds(pl.multiple_of(j * tb + sb * a, sb), sb)
            dqt_ref[h, :, cols] += _dot(kt_ref[h, :, kb], dst)

        @pl.when(j > i)
        def _():
            for a in range(ns):
                for h in range(hb):
                    for b in range(ns):
                        sub_block(h, a, b, False)

        @pl.when(j == i)
        def _():
            for a in range(ns):
                for h in range(hb):
                    for b in range(a + 1):
                        sub_block(h, a, b, a == b)

        @pl.when(j == nb - 1)
        def _():
            dk_ref[...] = dk_s[...] * (1.0 / LOG2E)
            dv_ref[...] = dv_s[...]

    qspec = pl.BlockSpec((hb, tb, HEAD_PAD), lambda g, i, j: (g, jnp.maximum(i, j), 0))
    kspec = pl.BlockSpec((hb, tb, HEAD_PAD), lambda g, i, j: (g, i, 0))
    ktspec = pl.BlockSpec((hb, HEAD_PAD, tb), lambda g, i, j: (g, 0, i))
    rowspec = pl.BlockSpec((hb, 1, tb), lambda g, i, j: (g, 0, jnp.maximum(i, j)))
    return pl.pallas_call(
        body, grid=(HEADS // hb, nb, nb), name="attn_bwd",
        in_specs=[qspec, kspec, ktspec, kspec, pl.BlockSpec((tb, hb * HEAD_PAD), lambda g, i, j: (jnp.maximum(i, j), g)), rowspec, rowspec],
        out_specs=[pl.BlockSpec((hb, HEAD_PAD, T), lambda g, i, j: (g, 0, 0)), kspec, kspec],
        out_shape=[jax.ShapeDtypeStruct((HEADS, HEAD_PAD, T), F32)] + [jax.ShapeDtypeStruct((HEADS, T, HEAD_PAD), F32)] * 2,
        scratch_shapes=[pltpu.VMEM((hb, tb, HEAD_PAD), F32), pltpu.VMEM((hb, tb, HEAD_PAD), F32)],
        compiler_params=_cparams(),
    )(q, k, kt, v, do, lse, delta)


def mixin_bwd(dres, x, z, rope, dqt, dk, dv, du, dvs, dpin, gmix, win, gql, wq, gkvl, wkv, gqh, gkh):
    T = x.shape[0]
    tm = min(ROW_TILE, T)

    def body(dres_ref, x_ref, z_ref, c_ref, sa_ref, sb_ref, dqt_ref, dk_ref, dv_ref, du_ref, dvs_ref, dpin_ref,
             gmix_ref, win_ref, gql_ref, wq_ref, gkvl_ref, wkv_ref, gqh_ref, gkh_ref,
             dx_ref, hn_ref, dz_ref, qn_ref, dqup_ref, kvn_ref, dkvup_ref,
             dgmix_ref, dgql_ref, dgkvl_ref, dgqh_ref, dgkh_ref):
        @pl.when(pl.program_id(0) == 0)
        def _():
            for r in (dgmix_ref, dgql_ref, dgkvl_ref, dgqh_ref, dgkh_ref):
                r[...] = jnp.zeros(r.shape, F32)

        xv = x_ref[...]
        hn, rx = _rms(xv, gmix_ref[...], D_MODEL)
        hn_ref[...] = hn.astype(BF16)
        ql, kvl, kr, qn, rq, kvn, rkv, q_up, kv_up = _qkv_pre(z_ref[...], gql_ref[...], wq_ref[...], gkvl_ref[...], wkv_ref[...])
        qn_ref[...] = qn
        kvn_ref[...] = kvn
        c, sa, sb = c_ref[...], sa_ref[...], sb_ref[...]
        lane = lax.broadcasted_iota(jnp.int32, (tm, HEAD_PAD), 1)
        rope_lanes = (lane >= QK_NOPE) & (lane < QK_HEAD)
        dkr = jnp.zeros((tm, HEAD_PAD), F32)
        dgqh = jnp.zeros((1, HEAD_PAD), F32)
        dgkh = jnp.zeros((1, HEAD_PAD), F32)
        dq_parts, dk_parts, dv_parts = [], [], []
        for h in range(HEADS):
            lo = HEAD_PAD * h
            qh = q_up[:, lo:lo + HEAD_PAD]
            rqh = lax.rsqrt(jnp.sum(qh * qh, axis=-1, keepdims=True) * (1.0 / QK_HEAD) + EPS)
            dq_h = jnp.transpose(dqt_ref[h]) * ATTN_SCALE
            dqh, dg = _rms_bwd(qh, rqh, gqh_ref[...], _rope_bwd(dq_h, c, sa, sb), QK_HEAD)
            dgqh = dgqh + dg
            dq_parts.append(dqh)
            kh = kv_up[:, lo:lo + HEAD_PAD] + kr
            rkh = lax.rsqrt(jnp.sum(kh * kh, axis=-1, keepdims=True) * (1.0 / QK_HEAD) + EPS)
            dkh, dg = _rms_bwd(kh, rkh, gkh_ref[...], _rope_bwd(dk_ref[h], c, sa, sb), QK_HEAD)
            dgkh = dgkh + dg
            dkr = dkr + jnp.where(rope_lanes, dkh, 0.0)
            dk_parts.append(dkh)
            dv_parts.append(dv_ref[h])
        dgqh_ref[...] += dgqh
        dgkh_ref[...] += dgkh
        dq_up = jnp.concatenate(dq_parts, axis=1).astype(BF16)
        dkv_up = jnp.concatenate(dk_parts + dv_parts, axis=1).astype(BF16)
        dqup_ref[...] = dq_up
        dkvup_ref[...] = dkv_up
        dql, dg = _rms_bwd(ql, rq, gql_ref[...], _dot_nt(dq_up, wq_ref[...]), Q_LORA)
        dgql_ref[...] += dg
        dkvl, dg = _rms_bwd(kvl, rkv, gkvl_ref[...], _dot_nt(dkv_up, wkv_ref[...]), KV_LORA)
        dgkvl_ref[...] += dg
        dz = jnp.concatenate([dql, dkvl, dkr, du_ref[...], dvs_ref[...], dpin_ref[...]], axis=1).astype(BF16)
        dz_ref[...] = dz
        dxn, dg = _rms_bwd(xv, rx, gmix_ref[...], _dot_nt(dz, win_ref[...]), D_MODEL)
        dgmix_ref[...] += dg
        dx_ref[...] = dres_ref[...] + dxn

    row = lambda w: pl.BlockSpec((tm, w), lambda i: (i, 0))
    head = pl.BlockSpec((HEADS, tm, HEAD_PAD), lambda i: (0, i, 0))
    head_t = pl.BlockSpec((HEADS, HEAD_PAD, tm), lambda i: (0, 0, i))
    acc_shapes = [(1, D_MODEL), (1, Q_LORA), (1, KV_LORA), (1, HEAD_PAD), (1, HEAD_PAD)]
    out_rows = [(D_MODEL, F32), (D_MODEL, BF16), (Z_WIDTH, BF16), (Q_LORA, BF16), (HEADS * HEAD_PAD, BF16),
                (KV_LORA, BF16), (2 * HEADS * HEAD_PAD, BF16)]
    return pl.pallas_call(
        body, grid=(T // tm,), name="mixin_bwd",
        in_specs=[row(D_MODEL), row(D_MODEL), row(Z_WIDTH)] + [row(HEAD_PAD)] * 3 + [head_t, head, head, row(256), row(256), row(256)]
                 + [WHOLE] * 8,
        out_specs=[row(w) for w, _ in out_rows] + [_acc_spec(s) for s in acc_shapes],
        out_shape=[jax.ShapeDtypeStruct((T, w), dt) for w, dt in out_rows] + [jax.ShapeDtypeStruct(s, F32) for s in acc_shapes],
        compiler_params=_cparams(),
    )(dres, x, z, *rope, dqt, dk, dv, du, dvs, dpin, gmix, win, gql, wq, gkvl, wkv, gqh, gkh)


def _place():
    x, y, c = lax.axis_index("x"), lax.axis_index("y"), lax.axis_index("c")
    return x, y, c, 4 * x + 2 * y + c


def cast_shards(shards, wanted):
    n = len(shards)

    def body(*refs):
        for o_ref, (w, l) in zip(refs[n:], wanted):
            o_ref[...] = refs[w][l].astype(BF16)

    return pl.pallas_call(
        body, name="cast_shards", in_specs=[WHOLE] * n, out_specs=[WHOLE] * len(wanted),
        out_shape=[jax.ShapeDtypeStruct(shards[w].shape[1:], BF16) for w, _ in wanted],
        compiler_params=_cparams(),
    )(*shards)


def allgather_layer0(shards):
    n = len(shards)

    def body(*refs):
        ins, outs = refs[:n], refs[n:2 * n]
        stage = refs[2 * n:3 * n]
        send_sems, recv_sems, local_sems = refs[3 * n:]
        x, y, c, me = _place()
        sibling = (x, y, 1 - c)
        chips = [(1 - x, y), (x, 1 - y), (1 - x, 1 - y)]

        def copy(k, w, block_id, to, from_stage):
            return pltpu.make_async_remote_copy(
                src_ref=stage[w] if from_stage else outs[w].at[block_id], dst_ref=outs[w].at[block_id],
                send_sem=send_sems.at[k, w], recv_sem=recv_sems.at[k, w], device_id=to, device_id_type=MESH)

        def block_of(cx, cy, cc):
            return 4 * cx + 2 * cy + cc

        local, sent = [], []
        for w in range(n):
            stage[w][...] = ins[w][0].astype(BF16)
            mine = pltpu.make_async_copy(stage[w], outs[w].at[me], local_sems.at[w])
            mine.start()
            local.append(mine)
            first = [copy(0, w, me, sibling, True)] + [copy(1 + j, w, me, (*chip, c), True) for j, chip in enumerate(chips)]
            for cp in first:
                cp.start()
            sent += first
        for j, chip in enumerate(chips):
            for w in range(n):
                copy(1 + j, w, block_of(*chip, c), (x, y, c), False).wait_recv()
                fwd = copy(4 + j, w, block_of(*chip, c), sibling, False)
                fwd.start()
                sent.append(fwd)
        for w in range(n):
            copy(0, w, block_of(x, y, 1 - c), (x, y, c), False).wait_recv()
            for j, chip in enumerate(chips):
                copy(4 + j, w, block_of(*chip, 1 - c), (x, y, c), False).wait_recv()
        for cp in local:
            cp.wait()
        for cp in sent:
            cp.wait_send()

    return pl.pallas_call(
        body, name="allgather_layer0",
        in_specs=[WHOLE] * n, out_specs=[ANY] * n,
        out_shape=[jax.ShapeDtypeStruct((N_DEV,) + s.shape[1:], BF16) for s in shards],
        scratch_shapes=[pltpu.VMEM(s.shape[1:], BF16) for s in shards]
                       + [pltpu.SemaphoreType.DMA((7, n)), pltpu.SemaphoreType.DMA((7, n)), pltpu.SemaphoreType.DMA((n,))],
        compiler_params=_cparams(),
    )(*shards)


def _peer(k):
    x, y, c, _ = _place()
    px = 1 - x if k & 4 else x
    py = 1 - y if k & 2 else y
    pc = 1 - c if k & 1 else c
    return (px, py, pc), 4 * px + 2 * py + pc


def exchange_start(srcs, after, gather, name):
    n = len(srcs)
    land_shapes = [((N_DEV,) + s.shape) if gather else s.shape for s in srcs]

    def body(*refs):
        src_refs, land_refs = refs[:n], refs[n:2 * n]
        send_sems, recv_sems = refs[2 * n + 1:3 * n + 1], refs[3 * n + 1:4 * n + 1]
        token = refs[-1]
        _, _, _, me = _place()
        for k in range(1, N_DEV):
            peer, peer_id = _peer(k)
            for w in range(n):
                pltpu.make_async_remote_copy(
                    src_ref=src_refs[w] if gather else src_refs[w].at[peer_id], dst_ref=land_refs[w].at[me],
                    send_sem=send_sems[w], recv_sem=recv_sems[w], device_id=peer, device_id_type=MESH).start()
        token[...] = jnp.zeros(token.shape, F32)

    hbm = lambda a: pltpu.with_memory_space_constraint(a, pltpu.HBM)
    outs = pl.pallas_call(
        body, name=name,
        out_shape=(pltpu.SemaphoreType.DMA(()),) * (2 * n)
                  + tuple(pltpu.HBM(s.shape, BF16) for s in srcs) + tuple(pltpu.HBM(s, BF16) for s in land_shapes)
                  + (jax.ShapeDtypeStruct((8, 128), F32),),
        in_specs=[HBM_SPEC] * (2 * n) + [ANY],
        out_specs=(SEM_SPEC,) * (2 * n) + (HBM_SPEC,) * (2 * n) + (WHOLE,),
        input_output_aliases={i: 2 * n + i for i in range(2 * n)},
        compiler_params=pltpu.CompilerParams(has_side_effects=pltpu.SideEffectType.DATAFLOW_SIDE_EFFECTING),
    )(*[hbm(s) for s in srcs], *[hbm(lax.empty(s, BF16)) for s in land_shapes], after)
    return list(outs[:n]), list(outs[n:2 * n]), list(outs[2 * n:3 * n]), list(outs[3 * n:4 * n]), outs[-1]


def exchange_wait(started, after, name):
    send_sems, recv_sems, srcs, lands, _ = started
    n = len(srcs)

    def body(*refs):
        land_refs = refs[n:2 * n]
        send_sems, recv_sems = refs[2 * n:3 * n], refs[3 * n:4 * n]
        x, y, c, _ = _place()
        for w in range(n):
            seven = land_refs[w].at[pl.ds(0, N_DEV - 1)]
            cp = pltpu.make_async_remote_copy(src_ref=seven, dst_ref=seven, send_sem=send_sems[w], recv_sem=recv_sems[w],
                                              device_id=(x, y, c), device_id_type=MESH)
            cp.wait_send()
            cp.wait_recv()

    outs = pl.pallas_call(
        body, name=name,
        out_shape=tuple(pltpu.HBM(s.shape, BF16) for s in srcs) + tuple(pltpu.HBM(l.shape, BF16) for l in lands),
        in_specs=[HBM_SPEC] * (2 * n) + [SEM_SPEC] * (2 * n) + [ANY],
        out_specs=(HBM_SPEC,) * (2 * n),
        input_output_aliases={i: i for i in range(2 * n)},
        compiler_params=pltpu.CompilerParams(has_side_effects=pltpu.SideEffectType.DATAFLOW_SIDE_EFFECTING),
    )(*srcs, *lands, *send_sems, *recv_sems, after)
    _, _, _, me = _place()
    filled = []
    for src, land in zip(outs[:n], outs[n:]):
        own = src[None] if src.ndim == 2 else lax.dynamic_slice_in_dim(src, me, 1, axis=0)
        filled.append(lax.dynamic_update_slice_in_dim(land, own, me, axis=0))
    return filled


def exchange_grads(pieces):
    n = len(pieces)

    def body(*refs):
        ins, outs = refs[:n], refs[n:2 * n]
        send_sems, recv_sems, local_sems = refs[2 * n:]
        x, y, c, me = _place()
        local = [pltpu.make_async_copy(ins[w].at[me], outs[w].at[me], local_sems.at[w]) for w in range(n)]
        for cp in local:
            cp.start()
        remote = []
        for k in range(1, N_DEV):
            px = 1 - x if k & 4 else x
            py = 1 - y if k & 2 else y
            pc = 1 - c if k & 1 else c
            peer = 4 * px + 2 * py + pc
            for w in range(n):
                cp = pltpu.make_async_remote_copy(
                    src_ref=ins[w].at[peer], dst_ref=outs[w].at[me], send_sem=send_sems.at[k - 1, w],
                    recv_sem=recv_sems.at[k - 1, w], device_id=(px, py, pc), device_id_type=MESH)
                cp.start()
                remote.append(cp)
        for cp in local:
            cp.wait()
        for cp in remote:
            cp.wait_recv()
        for cp in remote:
            cp.wait_send()

    return pl.pallas_call(
        body, name="exchange_grads",
        in_specs=[ANY] * n, out_specs=[ANY] * n,
        out_shape=[jax.ShapeDtypeStruct(p.shape, BF16) for p in pieces],
        scratch_shapes=[pltpu.SemaphoreType.DMA((7, n)), pltpu.SemaphoreType.DMA((7, n)), pltpu.SemaphoreType.DMA((n,))],
        compiler_params=_cparams(),
    )(*pieces)


def _adamw(w, g, m, v):
    m2 = ADAM_B1 * m + (1.0 - ADAM_B1) * g
    v2 = ADAM_B2 * v + (1.0 - ADAM_B2) * (g * g)
    m_hat = m2 / (1.0 - ADAM_B1 ** ADAM_STEP)
    v_hat = v2 / (1.0 - ADAM_B2 ** ADAM_STEP)
    delta = -ADAM_LR * (m_hat / (jnp.sqrt(v_hat) + ADAM_EPS) + ADAM_WD * w)
    return delta, m2, v2


def adamw_sharded(parts0, parts1, w, m, v, name):
    _, R, C = parts0.shape
    br = max(d for d in range(16, min(R, 512) + 1, 16) if R % d == 0)
    nblk = R // br

    def body(p0_ref, p1_ref, w_ref, m_ref, v_ref, g_ref, d_ref, m2_ref, v2_ref):
        def total(p_ref):
            g = p_ref[0].astype(F32)
            for s in range(1, N_DEV):
                g = g + p_ref[s].astype(F32)
            return g

        g = jnp.where(pl.program_id(0) == 0, total(p0_ref), total(p1_ref))
        g_ref[...] = g
        d_ref[...], m2_ref[...], v2_ref[...] = _adamw(w_ref[...], g, m_ref[...], v_ref[...])

    blk = pl.BlockSpec((None, br, C), lambda l, i: (l, i, 0))
    p0 = pl.BlockSpec((N_DEV, br, C), lambda l, i: (0, jnp.where(l == 0, i, nblk - 1), 0))
    p1 = pl.BlockSpec((N_DEV, br, C), lambda l, i: (0, jnp.where(l == 1, i, 0), 0))
    return pl.pallas_call(
        body, grid=(DEPTH, nblk), name=name,
        in_specs=[p0, p1, blk, blk, blk],
        out_specs=[blk] * 4,
        out_shape=[jax.ShapeDtypeStruct((DEPTH, R, C), F32)] * 4,
        compiler_params=_cparams(),
    )(parts0, parts1, w, m, v)


def allreduce_adamw_small(gpart, w, m, v):
    R = gpart.shape[0]

    def body(g_ref, w_ref, m_ref, v_ref, grad_ref, d_ref, m2_ref, v2_ref, all_ref, send_sems, recv_sems):
        x, y, c, me = _place()
        sibling = (x, y, 1 - c)
        chips = [(1 - x, y), (x, 1 - y), (1 - x, 1 - y)]

        def copy(k, block_id, to, from_input):
            return pltpu.make_async_remote_copy(
                src_ref=g_ref if from_input else all_ref.at[block_id], dst_ref=all_ref.at[block_id],
                send_sem=send_sems.at[k], recv_sem=recv_sems.at[k], device_id=to, device_id_type=MESH)

        def block_of(cx, cy, cc):
            return 4 * cx + 2 * cy + cc

        all_ref[me] = g_ref[...]
        first = [copy(0, me, sibling, True)] + [copy(1 + j, me, (*chip, c), True) for j, chip in enumerate(chips)]
        for cp in first:
            cp.start()
        passed = [copy(4 + j, block_of(*chip, c), sibling, False) for j, chip in enumerate(chips)]
        for j, chip in enumerate(chips):
            copy(1 + j, block_of(*chip, c), (x, y, c), False).wait_recv()
            passed[j].start()
        copy(0, block_of(x, y, 1 - c), (x, y, c), False).wait_recv()
        for j, chip in enumerate(chips):
            copy(4 + j, block_of(*chip, 1 - c), (x, y, c), False).wait_recv()
        for cp in first + passed:
            cp.wait_send()
        g = all_ref[0]
        for s in range(1, N_DEV):
            g = g + all_ref[s]
        grad_ref[...] = g
        d_ref[...], m2_ref[...], v2_ref[...] = _adamw(w_ref[...], g, m_ref[...], v_ref[...])

    return pl.pallas_call(
        body, name="allreduce_adamw_small",
        in_specs=[WHOLE] * 4, out_specs=[WHOLE] * 4,
        out_shape=[jax.ShapeDtypeStruct((R, 128), F32)] * 4,
        scratch_shapes=[pltpu.VMEM((N_DEV, R, 128), F32), pltpu.SemaphoreType.DMA((7,)), pltpu.SemaphoreType.DMA((7,))],
        compiler_params=_cparams(),
    )(gpart, w, m, v)


def _cols_from_pieces(p):
    return p.transpose(1, 0, 2).reshape(p.shape[1], -1)


def _pieces_from_cols(w):
    return w.reshape(w.shape[0], N_DEV, -1).transpose(1, 0, 2)


def _pad_heads(w, width):
    r = w.shape[0]
    return jnp.pad(w.reshape(r, HEADS, width), ((0, 0), (0, 0), (0, HEAD_PAD - width))).reshape(r, HEADS * HEAD_PAD)


def _unpad_heads(w, width):
    r = w.shape[0]
    return w.reshape(r, HEADS, HEAD_PAD)[:, :, :width].reshape(r, HEADS * width)


O1 = Q_LORA
O2 = O1 + KV_LORA
O3 = O2 + QK_ROPE


def _mixer_weights(gw):
    w_in = _cols_from_pieces(gw["w_in"])
    zero = lambda n: jnp.zeros((D_MODEL, n), BF16)
    win = jnp.concatenate([w_in[:, :O2], zero(QK_NOPE), w_in[:, O2:O3], zero(HEAD_PAD - QK_HEAD), w_in[:, O3:]], axis=1)
    wq = _pad_heads(_cols_from_pieces(gw["w_q_up"]), QK_HEAD)
    w_kv = _cols_from_pieces(gw["w_kv_up"]).reshape(KV_LORA, HEADS, QK_NOPE + V_HEAD)
    wk = jnp.pad(w_kv[:, :, :QK_NOPE], ((0, 0), (0, 0), (0, HEAD_PAD - QK_NOPE))).reshape(KV_LORA, HEADS * HEAD_PAD)
    wv = w_kv[:, :, QK_NOPE:].reshape(KV_LORA, HEADS * V_HEAD)
    wkv = jnp.concatenate([wk, wv], axis=1)
    wout = gw["w_out"].reshape(D_MODEL, D_MODEL)
    return dict(win=win, wq=wq, wkv=wkv, wout=wout)


def _ffn_weights(gw):
    return dict(wg=_cols_from_pieces(gw["w_gate"]), wu=_cols_from_pieces(gw["w_up"]),
                wd=gw["w_down"].reshape(FFN_HIDDEN, D_MODEL))


def _layer_small(p, l):
    row = lambda a: a.reshape(1, -1)
    pad_head = lambda g: jnp.pad(g, (0, HEAD_PAD - QK_HEAD)).reshape(1, HEAD_PAD)
    tril = jnp.tril(jnp.ones((CHUNK, CHUNK), F32))
    wsp = p["w_spatial"][l] * tril
    wcat = jnp.concatenate([wsp[h] for h in range(HEADS)], axis=1).astype(BF16)
    wcat_t = jnp.concatenate([wsp[h].T for h in range(HEADS)], axis=1).astype(BF16)
    bz = jnp.repeat(p["b_spatial"][l].T, SGU_HEAD_DIM, axis=1)
    wp = jax.scipy.linalg.block_diag(*[p["w_pool"][l][g] for g in range(HEADS)]).astype(BF16)
    return dict(gmix=row(p["g_mix_norm"][l]), gql=row(p["g_q_lat"][l]), gkvl=row(p["g_kv_lat"][l]),
                gqh=pad_head(p["g_q_head"][l]), gkh=pad_head(p["g_k_head"][l]), gsgu=row(p["g_sgu_v"][l]),
                wcat=wcat, wcat_t=wcat_t, bz=bz, wp=wp, pscale=row(p["pool_scale"][l]),
                goa=row(p["g_out_mla"][l]), gos=row(p["g_out_sgu"][l]), gop=row(p["g_out_pool"][l]),
                gffn=row(p["g_ffn_norm"][l]))


MIXER_SIDE = ("w_in", "w_q_up", "w_kv_up", "w_out")
FFN_SIDE = ("w_gate", "w_up", "w_down")
SHARDED = MIXER_SIDE + FFN_SIDE
SMALL = ("g_mix_norm", "g_q_lat", "g_kv_lat", "g_q_head", "g_k_head", "g_sgu_v", "w_spatial", "b_spatial", "w_pool",
         "pool_scale", "g_out_mla", "g_out_sgu", "g_out_pool", "g_ffn_norm")
WEIGHTS = ("g_mix_norm", "w_in", "g_q_lat", "w_q_up", "g_kv_lat", "w_kv_up", "g_q_head", "g_k_head", "g_sgu_v", "w_spatial",
           "b_spatial", "w_pool", "pool_scale", "g_out_mla", "g_out_sgu", "g_out_pool", "w_out", "g_ffn_norm", "w_gate",
           "w_up", "w_down")
PACK_ROWS = 8 * 128


def _pack_small(parts):
    flat = []
    for name in SMALL:
        a = parts[name].reshape(-1)
        flat.append(jnp.pad(a, (0, -a.shape[0] % PACK_ROWS)))
    return jnp.concatenate(flat).reshape(-1, 128)


def _unpack_small(packed, like):
    out, row = {}, 0
    for name in SMALL:
        n = math.prod(like[name].shape)
        rows = -(-n // PACK_ROWS) * 8
        out[name] = packed[row:row + rows].reshape(-1)[:n].reshape(like[name].shape)
        row += rows
    return out


def _forward_mixers(x, rope, W, S):
    z, q, k, v, kt, vt = mixin_fwd(x, rope, S["gmix"], W["win"], S["gql"], W["wq"], S["gkvl"], W["wkv"], S["gqh"], S["gkh"])
    a, lse = attn_fwd(q, k, vt)
    x1 = mixers_fwd(x, a, z, S["gsgu"], S["wcat"], S["bz"], S["wp"], S["pscale"], S["goa"], S["gos"], S["gop"], W["wout"])
    return x1, dict(x=x, z=z, q=q, k=k, kt=kt, v=v, a=a, lse=lse, x1=x1)


def _backward_ffn(dx2, W, S, A, l):
    dx1, h2, act, dg, du_ffn, dgffn = ffn_bwd(dx2, A["x1"], A["gs"], A["us"], S["gffn"], W["wg"], W["wu"], W["wd"])
    d_wd = matmul_tn(act, dx2, f"dw_down_{l}")
    d_wg = matmul_tn(h2, dg, f"dw_gate_{l}")
    d_wu = matmul_tn(h2, du_ffn, f"dw_up_{l}")
    big = dict(w_gate=_pieces_from_cols(d_wg), w_up=_pieces_from_cols(d_wu), w_down=d_wd.reshape(N_DEV, -1, D_MODEL))
    return dx1, big, dgffn


def _backward_mixers(dx1, dgffn, rope, W, S, A, l):
    (da, delta, du, dvs, dm, mix, dgoa, dgos, dgop, dps, dwp, dwsp, db, dgsgu) = mixers_bwd(
        dx1, A["a"], A["z"], S["gsgu"], S["wcat"], S["wcat_t"], S["bz"], S["wp"], S["pscale"], S["goa"], S["gos"],
        S["gop"], W["wout"])
    d_wout = matmul_tn(mix, dx1, f"dw_out_{l}")
    dpin = pool_bwd(dm)
    dqt, dk, dv = attn_bwd(A["q"], A["k"], A["kt"], A["v"], da, A["lse"], delta)
    (dx, hn, dz, qn, dq_up, kvn, dkv_up, dgmix, dgql, dgkvl, dgqh, dgkh) = mixin_bwd(
        dx1, A["x"], A["z"], rope, dqt, dk, dv, du, dvs, dpin, S["gmix"], W["win"], S["gql"], W["wq"], S["gkvl"],
        W["wkv"], S["gqh"], S["gkh"])
    d_win = matmul_tn(hn, dz, f"dw_in_{l}")
    d_wq = matmul_tn(qn, dq_up, f"dw_q_up_{l}")
    d_wkv = matmul_tn(kvn, dkv_up, f"dw_kv_up_{l}")
    d_win = jnp.concatenate([d_win[:, :O2], d_win[:, O2 + QK_NOPE:O2 + QK_HEAD], d_win[:, O2 + HEAD_PAD:]], axis=1)
    d_wk = d_wkv[:, :HEADS * HEAD_PAD].reshape(KV_LORA, HEADS, HEAD_PAD)[:, :, :QK_NOPE]
    d_wv = d_wkv[:, HEADS * HEAD_PAD:].reshape(KV_LORA, HEADS, V_HEAD)
    d_wkv = jnp.concatenate([d_wk, d_wv], axis=2).reshape(KV_LORA, HEADS * (QK_NOPE + V_HEAD))
    big = dict(w_in=_pieces_from_cols(d_win), w_q_up=_pieces_from_cols(_unpad_heads(d_wq, QK_HEAD)),
               w_kv_up=_pieces_from_cols(d_wkv), w_out=d_wout.reshape(N_DEV, -1, D_MODEL))
    tril = jnp.tril(jnp.ones((CHUNK, CHUNK), F32))
    small = dict(g_mix_norm=dgmix[0], g_q_lat=dgql[0], g_kv_lat=dgkvl[0], g_q_head=dgqh[0, :QK_HEAD], g_k_head=dgkh[0, :QK_HEAD],
                 g_sgu_v=dgsgu[0], w_spatial=dwsp.reshape(HEADS, CHUNK, CHUNK) * tril, b_spatial=db[:, :HEADS].T,
                 w_pool=jnp.stack([dwp[64 * g:64 * (g + 1), 64 * g:64 * (g + 1)] for g in range(HEADS)]),
                 pool_scale=dps[0], g_out_mla=dgoa[0], g_out_sgu=dgos[0], g_out_pool=dgop[0], g_ffn_norm=dgffn[0])
    return dx, big, small


def kernel(x, positions, g_mix_norm, w_in, g_q_lat, w_q_up, g_kv_lat, w_kv_up, g_q_head, g_k_head, g_sgu_v, w_spatial, b_spatial, w_pool, pool_scale, g_out_mla, g_out_sgu, g_out_pool, w_out, g_ffn_norm, w_gate, w_up, w_down, loss_target, m_g_mix_norm, m_w_in, m_g_q_lat, m_w_q_up, m_g_kv_lat, m_w_kv_up, m_g_q_head, m_g_k_head, m_g_sgu_v, m_w_spatial, m_b_spatial, m_w_pool, m_pool_scale, m_g_out_mla, m_g_out_sgu, m_g_out_pool, m_w_out, m_g_ffn_norm, m_w_gate, m_w_up, m_w_down, v_g_mix_norm, v_w_in, v_g_q_lat, v_w_q_up, v_g_kv_lat, v_w_kv_up, v_g_q_head, v_g_k_head, v_g_sgu_v, v_w_spatial, v_b_spatial, v_w_pool, v_pool_scale, v_g_out_mla, v_g_out_sgu, v_g_out_pool, v_w_out, v_g_ffn_norm, v_w_gate, v_w_up, v_w_down):
    given = dict(locals())
    w = {n: given[n] for n in WEIGHTS}
    m = {n: given["m_" + n] for n in WEIGHTS}
    v = {n: given["v_" + n] for n in WEIGHTS}
    T = x.shape[1]
    xs = x.reshape(T, D_MODEL)

    half = QK_ROPE // 2
    inv_freq = 1.0 / (ROPE_THETA ** (jnp.arange(half, dtype=F32) / half))
    ang16 = positions.reshape(T).astype(F32)[:, None] * inv_freq
    ang = jnp.concatenate([jnp.zeros((T, QK_NOPE), F32), ang16, ang16, jnp.zeros((T, HEAD_PAD - QK_HEAD), F32)], axis=1)

    wanted = [(SHARDED.index(n), 0) for n in FFN_SIDE] + [(i, 1) for i in range(len(SHARDED))]
    bf = cast_shards([w[n] for n in SHARDED], wanted)
    mixer0 = allgather_layer0([w[n] for n in MIXER_SIDE])
    ag_ffn0 = exchange_start(bf[:len(FFN_SIDE)], mixer0[0], True, "ag_start_ffn0")
    ag_l1 = exchange_start(bf[len(FFN_SIDE):], ag_ffn0[-1], True, "ag_start_l1")
    rope = rope_tables(ang + (ag_ffn0[-1][0, 0] + ag_l1[-1][0, 0]))
    Ss = [_layer_small(w, l) for l in range(DEPTH)]

    x1, A0 = _forward_mixers(xs, rope, _mixer_weights(dict(zip(MIXER_SIDE, mixer0))), Ss[0])
    F0 = _ffn_weights(dict(zip(FFN_SIDE, exchange_wait(ag_ffn0, x1, "ag_wait_ffn0"))))
    h, A0["gs"], A0["us"] = ffn_fwd(x1, Ss[0]["gffn"], F0["wg"], F0["wu"], F0["wd"])
    layer1 = dict(zip(SHARDED, exchange_wait(ag_l1, h, "ag_wait_l1")))
    W1 = {**_mixer_weights(layer1), **_ffn_weights(layer1)}
    x1, A1 = _forward_mixers(h, rope, W1, Ss[1])
    h, A1["gs"], A1["us"] = ffn_fwd(x1, Ss[1]["gffn"], W1["wg"], W1["wu"], W1["wd"])
    dh, loss_part = loss_head(h, loss_target.reshape(T, D_MODEL))
    loss = lax.psum(loss_part[0, 0], ("x", "y", "c"))

    dx1, big_f1, dgffn1 = _backward_ffn(dh, W1, Ss[1], A1, 1)
    dh, big_m1, small1 = _backward_mixers(dx1, dgffn1, rope, W1, Ss[1], A1, 1)
    rs_l1 = exchange_start([{**big_m1, **big_f1}[n] for n in SHARDED], dh, False, "rs_start_l1")
    W0 = {**_mixer_weights(dict(zip(MIXER_SIDE, mixer0))), **F0}
    S0 = dict(Ss[0], gffn=Ss[0]["gffn"] + rs_l1[-1][0, 0])
    dx1, big_f0, dgffn0 = _backward_ffn(dh, W0, S0, A0, 0)
    rs_ffn0 = exchange_start([big_f0[n] for n in FFN_SIDE], dx1, False, "rs_start_ffn0")
    S0 = dict(Ss[0], gsgu=Ss[0]["gsgu"] + rs_ffn0[-1][0, 0])
    dh, big_m0, small0 = _backward_mixers(dx1, dgffn0, rope, W0, S0, A0, 0)
    grad_x = dh.reshape(x.shape)
    smalls = [small0, small1]

    parts1 = dict(zip(SHARDED, exchange_wait(rs_l1, dh, "rs_wait_l1")))
    parts0 = dict(zip(FFN_SIDE, exchange_wait(rs_ffn0, dh, "rs_wait_ffn0")))
    parts0.update(zip(MIXER_SIDE, exchange_grads([big_m0[n] for n in MIXER_SIDE])))
    grad, delta, new_m, new_v = {}, {}, {}, {}
    for n in SHARDED:
        grad[n], delta[n], new_m[n], new_v[n] = adamw_sharded(parts0[n], parts1[n], w[n], m[n], v[n], f"adamw_{n}")

    small_part = {n: jnp.stack([smalls[l][n] for l in range(DEPTH)]) for n in SMALL}
    outs = allreduce_adamw_small(_pack_small(small_part), _pack_small(w), _pack_small(m), _pack_small(v))
    for d, o in zip((grad, delta, new_m, new_v), outs):
        d.update(_unpack_small(o, w))

    return (loss, grad_x, *[grad[n] for n in WEIGHTS], *[delta[n] for n in WEIGHTS], *[new_m[n] for n in WEIGHTS],
            *[new_v[n] for n in WEIGHTS])
```

```python
import functools
import math

import jax
import jax.numpy as jnp
from jax import lax
from jax.experimental import pallas as pl
from jax.experimental.pallas import tpu as pltpu

F32 = jnp.float32
BF16 = jnp.bfloat16

N_DEV = 8
DEPTH = 2
D_MODEL = 1024
HEADS = 4
HEAD_PAD = 128
QK_NOPE = 64
QK_ROPE = 32
QK_HEAD = QK_NOPE + QK_ROPE
V_HEAD = 128
Q_LORA = 256
KV_LORA = 128
SGU_WIDTH = 256
SGU_HEAD_DIM = 64
CHUNK = 128
POOL_WIDTH = 256
POOL_WINDOWS = (2, 4, 8, 16)
POOL_HALO = 16
MLA_WIDTH = 512
IN_WIDTH = 1184
Z_WIDTH = 1280
FFN_HIDDEN = 2816
FFN_CHUNK = 256
ROPE_THETA = 10000.0
EPS = 1e-6
ATTN_SCALE = 1.0 / math.sqrt(QK_HEAD)
LOG2E = 1.4426950408889634
NEG_BIG = -1e30

ADAM_LR = 0.001
ADAM_B1 = 0.9
ADAM_B2 = 0.999
ADAM_EPS = 1e-08
ADAM_WD = 0.01
ADAM_STEP = 10

VMEM_LIMIT = 56 * 1024 * 1024
ROW_TILE = 512
MIXIN_PART = 256
ATTN_TILE = 1024
ATTN_SUB = 512
ATTN_FWD_SUB = 1024
ATTN_BWD_HEADS = 2
MESH = pl.DeviceIdType.MESH

WHOLE = pl.BlockSpec(memory_space=pltpu.VMEM)
ANY = pl.BlockSpec(memory_space=pl.ANY)
HBM_SPEC = pl.BlockSpec(memory_space=pltpu.HBM)
SEM_SPEC = pl.BlockSpec(memory_space=pltpu.SEMAPHORE)


def _cparams(**kw):
    return pltpu.CompilerParams(vmem_limit_bytes=VMEM_LIMIT, **kw)


def _dot(a, b):
    return jnp.dot(a, b, preferred_element_type=F32)


def _dot_nt(a, b):
    return lax.dot_general(a, b, (((1,), (1,)), ((), ())), preferred_element_type=F32)


def _dot_tn(a, b):
    return lax.dot_general(a, b, (((0,), (0,)), ((), ())), preferred_element_type=F32)


def _rms(x, g, n):
    r = lax.rsqrt(jnp.sum(x * x, axis=-1, keepdims=True) * (1.0 / n) + EPS)
    return x * r * g, r


def _rms_bwd(x, r, g, dy, n):
    gdy = dy * g
    dx = r * gdy - x * (r * r * r) * (jnp.sum(x * gdy, axis=-1, keepdims=True) * (1.0 / n))
    dg = jnp.sum(dy * (x * r), axis=0, keepdims=True)
    return dx, dg


def _sigmoid(x):
    return 1.0 / (1.0 + jnp.exp(-x))


def rope_tables(ang):
    T = ang.shape[0]
    tm = min(ROW_TILE, T)

    def body(ang_ref, c_ref, sa_ref, sb_ref):
        a = ang_ref[...]
        lane = lax.broadcasted_iota(jnp.int32, a.shape, 1)
        s = jnp.sin(a)
        c_ref[...] = jnp.cos(a)
        sa_ref[...] = jnp.where(lane < QK_NOPE + QK_ROPE // 2, -s, 0.0)
        sb_ref[...] = jnp.where(lane >= QK_NOPE + QK_ROPE // 2, s, 0.0)

    row = pl.BlockSpec((tm, HEAD_PAD), lambda i: (i, 0))
    return pl.pallas_call(
        body, grid=(T // tm,), name="rope_tables", in_specs=[row], out_specs=[row] * 3,
        out_shape=[jax.ShapeDtypeStruct((T, HEAD_PAD), F32)] * 3, compiler_params=_cparams(),
    )(ang)


def _rope(x, c, sa, sb):
    half = QK_ROPE // 2
    return x * c + pltpu.roll(x, HEAD_PAD - half, 1) * sa + pltpu.roll(x, half, 1) * sb


def _rope_bwd(dy, c, sa, sb):
    half = QK_ROPE // 2
    return dy * c + pltpu.roll(dy * sa, half, 1) + pltpu.roll(dy * sb, HEAD_PAD - half, 1)


def _head_masks(shape, width):
    lane = lax.broadcasted_iota(jnp.int32, shape, len(shape) - 1)
    return [(lane >= width * h) & (lane < width * (h + 1)) for h in range(HEADS)]


def _qkv_pre(z, gql, wq, gkvl, wkv):
    ql = z[:, 0:Q_LORA]
    kvl = z[:, Q_LORA:Q_LORA + KV_LORA]
    kr = z[:, Q_LORA + KV_LORA:Q_LORA + KV_LORA + HEAD_PAD]
    qn, rq = _rms(ql, gql, Q_LORA)
    kvn, rkv = _rms(kvl, gkvl, KV_LORA)
    qn = qn.astype(BF16)
    kvn = kvn.astype(BF16)
    q_up = _dot(qn, wq)
    kv_up = _dot(kvn, wkv)
    return ql, kvl, kr, qn, rq, kvn, rkv, q_up, kv_up


def mixin_fwd(x, rope, gmix, win, gql, wq, gkvl, wkv, gqh, gkh):
    T = x.shape[0]
    tm = min(2 * ROW_TILE, T)
    part = min(MIXIN_PART, tm)

    def body(x_ref, c_ref, sa_ref, sb_ref, gmix_ref, win_ref, gql_ref, wq_ref, gkvl_ref, wkv_ref, gqh_ref, gkh_ref,
             z_ref, q_ref, k_ref, v_ref, kt_ref, vt_ref):
        for p in range(tm // part):
            rows = pl.ds(part * p, part)
            hn = _rms(x_ref[rows, :], gmix_ref[...], D_MODEL)[0].astype(BF16)
            z = _dot(hn, win_ref[...])
            z_ref[rows, :] = z
            _, _, kr, _, _, _, _, q_up, kv_up = _qkv_pre(z, gql_ref[...], wq_ref[...], gkvl_ref[...], wkv_ref[...])
            c, sa, sb = c_ref[rows, :], sa_ref[rows, :], sb_ref[rows, :]
            for h in range(HEADS):
                lo = HEAD_PAD * h
                qh = _rms(q_up[:, lo:lo + HEAD_PAD], gqh_ref[...], QK_HEAD)[0]
                q_ref[h, rows, :] = (_rope(qh, c, sa, sb) * (ATTN_SCALE * LOG2E)).astype(BF16)
                kh = _rope(_rms(kv_up[:, lo:lo + HEAD_PAD] + kr, gkh_ref[...], QK_HEAD)[0], c, sa, sb)
                k_ref[h, rows, :] = kh.astype(BF16)
                kt_ref[h, :, rows] = jnp.transpose(kh).astype(BF16)
                vh = kv_up[:, HEADS * HEAD_PAD + lo:HEADS * HEAD_PAD + lo + HEAD_PAD]
                v_ref[h, rows, :] = vh.astype(BF16)
                vt_ref[h, :, rows] = jnp.transpose(vh).astype(BF16)

    row = lambda w: pl.BlockSpec((tm, w), lambda i: (i, 0))
    head = pl.BlockSpec((HEADS, tm, HEAD_PAD), lambda i: (0, i, 0))
    head_t = pl.BlockSpec((HEADS, HEAD_PAD, tm), lambda i: (0, 0, i))
    return pl.pallas_call(
        body, grid=(T // tm,), name="mixin_fwd",
        in_specs=[row(D_MODEL)] + [row(HEAD_PAD)] * 3 + [WHOLE] * 8,
        out_specs=[row(Z_WIDTH), head, head, head, head_t, head_t],
        out_shape=[jax.ShapeDtypeStruct((T, Z_WIDTH), F32)] + [jax.ShapeDtypeStruct((HEADS, T, HEAD_PAD), BF16)] * 3
                  + [jax.ShapeDtypeStruct((HEADS, HEAD_PAD, T), BF16)] * 2,
        compiler_params=_cparams(),
    )(x, *rope, gmix, win, gql, wq, gkvl, wkv, gqh, gkh)


def attn_fwd(q, k, vt):
    _, T, _ = q.shape
    tb = min(ATTN_TILE, T)
    sb = min(ATTN_FWD_SUB, tb)
    ns = tb // sb
    nb = T // tb

    def body(q_ref, k_ref, vt_ref, o_ref, lse_ref, m_s, l_s, acc_s):
        i, j = pl.program_id(0), pl.program_id(1)

        @pl.when(j == 0)
        def _():
            m_s[...] = jnp.full(m_s.shape, -jnp.inf, F32)
            l_s[...] = jnp.zeros(l_s.shape, F32)
            acc_s[...] = jnp.zeros(acc_s.shape, F32)

        def sub_block(h, a, b, masked):
            qa = slice(sb * a, sb * (a + 1))
            kb = slice(sb * b, sb * (b + 1))
            st = _dot_nt(k_ref[h, kb, :], q_ref[h, qa, :])
            if masked:
                krow = lax.broadcasted_iota(jnp.int32, st.shape, 0)
                qcol = lax.broadcasted_iota(jnp.int32, st.shape, 1)
                st = jnp.where(krow <= qcol, st, NEG_BIG)
            m_prev = m_s[h, :, qa]
            m_new = jnp.maximum(m_prev, jnp.max(st, axis=0, keepdims=True))
            alpha = jnp.exp2(m_prev - m_new)
            pt = jnp.exp2(st - m_new)
            l_s[h, :, qa] = alpha * l_s[h, :, qa] + jnp.sum(pt, axis=0, keepdims=True)
            acc_s[h, :, qa] = alpha * acc_s[h, :, qa] + _dot(vt_ref[h, :, kb], pt.astype(BF16))
            m_s[h, :, qa] = m_new

        @pl.when(j < i)
        def _():
            for b in range(ns):
                for h in range(HEADS):
                    for a in range(ns):
                        sub_block(h, a, b, False)

        @pl.when(j == i)
        def _():
            for b in range(ns):
                for h in range(HEADS):
                    for a in range(b, ns):
                        sub_block(h, a, b, a == b)
            for h in range(HEADS):
                l = l_s[h]
                o_ref[:, HEAD_PAD * h:HEAD_PAD * (h + 1)] = jnp.transpose(acc_s[h] / l)
                lse_ref[h] = m_s[h] + jnp.log2(l)

    qspec = pl.BlockSpec((HEADS, tb, HEAD_PAD), lambda i, j: (0, i, 0))
    kspec = pl.BlockSpec((HEADS, tb, HEAD_PAD), lambda i, j: (0, jnp.minimum(i, j), 0))
    vspec = pl.BlockSpec((HEADS, HEAD_PAD, tb), lambda i, j: (0, 0, jnp.minimum(i, j)))
    return pl.pallas_call(
        body, grid=(nb, nb), name="attn_fwd",
        in_specs=[qspec, kspec, vspec],
        out_specs=[pl.BlockSpec((tb, MLA_WIDTH), lambda i, j: (i, 0)), pl.BlockSpec((HEADS, 1, tb), lambda i, j: (0, 0, i))],
        out_shape=[jax.ShapeDtypeStruct((T, MLA_WIDTH), F32), jax.ShapeDtypeStruct((HEADS, 1, T), F32)],
        scratch_shapes=[pltpu.VMEM((HEADS, 1, tb), F32), pltpu.VMEM((HEADS, 1, tb), F32), pltpu.VMEM((HEADS, HEAD_PAD, tb), F32)],
        compiler_params=_cparams(),
    )(q, k, vt)


def _sgu_fwd_chunk(vn_c, wcat, bz, masks):
    vstack = jnp.concatenate([jnp.where(mk, vn_c, 0.0).astype(BF16) for mk in masks], axis=0)
    return _dot(wcat, vstack) + bz


def _pool_counts(i, tm):
    pos1 = (i * tm + 1 + lax.broadcasted_iota(jnp.int32, (tm, POOL_WIDTH), 0)).astype(F32)
    lane = lax.broadcasted_iota(jnp.int32, (tm, POOL_WIDTH), 1)
    win = jnp.where(lane < 64, 2.0, jnp.where(lane < 128, 4.0, jnp.where(lane < 192, 8.0, 16.0)))
    return jnp.minimum(pos1, win), lane


def _by_group(lane, s2, s4, s8, s16):
    return jnp.where(lane < 64, s2, jnp.where(lane < 128, s4, jnp.where(lane < 192, s8, s16)))


def _pool_means(pin, halo, i, tm):
    s1 = jnp.concatenate([halo, pin], axis=0)
    s2 = s1 + pltpu.roll(s1, 1, 0)
    s4 = s2 + pltpu.roll(s2, 2, 0)
    s8 = s4 + pltpu.roll(s4, 4, 0)
    s16 = s8 + pltpu.roll(s8, 8, 0)
    cnt, lane = _pool_counts(i, tm)
    sel = _by_group(lane, s2[POOL_HALO:], s4[POOL_HALO:], s8[POOL_HALO:], s16[POOL_HALO:])
    return sel / cnt - pin


def _mixers_fwd_tile(i, tm, a, u, vs, pin, halo, gsgu, wcat, bz, wp, pscale, goa, gos, gop):
    vn, rv = _rms(vs, gsgu, SGU_WIDTH)
    masks = _head_masks((CHUNK, SGU_WIDTH), SGU_HEAD_DIM)
    zc = jnp.concatenate([_sgu_fwd_chunk(vn[CHUNK * c:CHUNK * (c + 1)], wcat, bz, masks) for c in range(tm // CHUNK)], axis=0)
    gm = u * zc
    halo = jnp.where(i > 0, halo, 0.0)
    m = _pool_means(pin, halo, i, tm).astype(BF16)
    yp_pre = _dot(m, wp)
    yp = yp_pre * pscale
    na, ra = _rms(a, goa, MLA_WIDTH)
    ng, rg = _rms(gm, gos, SGU_WIDTH)
    npo, rp = _rms(yp, gop, POOL_WIDTH)
    mix = jnp.concatenate([na, ng, npo], axis=1).astype(BF16)
    return vn, rv, zc, gm, m, yp_pre, yp, ra, rg, rp, mix


def _z_specs(tm):
    col = lambda c: pl.BlockSpec((tm, 256), lambda i: (i, c))
    halo = pl.BlockSpec((POOL_HALO, 256), lambda i: (jnp.maximum(i * (tm // POOL_HALO) - 1, 0), 4))
    return [col(2), col(3), col(4), halo]


def mixers_fwd(x, a, z, gsgu, wcat, bz, wp, pscale, goa, gos, gop, wout):
    T = x.shape[0]
    tm = min(ROW_TILE, T)

    def body(x_ref, a_ref, u_ref, vs_ref, pin_ref, halo_ref, gsgu_ref, wcat_ref, bz_ref, wp_ref, ps_ref,
             goa_ref, gos_ref, gop_ref, wout_ref, x1_ref):
        i = pl.program_id(0)
        mix = _mixers_fwd_tile(i, tm, a_ref[...], u_ref[...], vs_ref[...], pin_ref[...], halo_ref[...], gsgu_ref[...],
                               wcat_ref[...], bz_ref[...], wp_ref[...], ps_ref[...], goa_ref[...], gos_ref[...],
                               gop_ref[...])[-1]
        x1_ref[...] = x_ref[...] + _dot(mix, wout_ref[...])

    row = lambda w: pl.BlockSpec((tm, w), lambda i: (i, 0))
    return pl.pallas_call(
        body, grid=(T // tm,), name="mixers_fwd",
        in_specs=[row(D_MODEL), row(MLA_WIDTH)] + _z_specs(tm) + [WHOLE] * 9,
        out_specs=row(D_MODEL),
        out_shape=jax.ShapeDtypeStruct((T, D_MODEL), F32),
        compiler_params=_cparams(),
    )(x, a, z, z, z, z, gsgu, wcat, bz, wp, pscale, goa, gos, gop, wout)


def ffn_fwd(x1, gffn, wg, wu, wd):
    T = x1.shape[0]
    tm = min(ROW_TILE, T)

    def body(x1_ref, gffn_ref, wg_ref, wu_ref, wd_ref, x2_ref, gs_ref, us_ref):
        x1v = x1_ref[...]
        h2 = _rms(x1v, gffn_ref[...], D_MODEL)[0].astype(BF16)
        acc = x1v
        for c in range(FFN_HIDDEN // FFN_CHUNK):
            sl = slice(FFN_CHUNK * c, FFN_CHUNK * (c + 1))
            g = _dot(h2, wg_ref[:, sl])
            u = _dot(h2, wu_ref[:, sl])
            gs_ref[:, sl] = g.astype(BF16)
            us_ref[:, sl] = u.astype(BF16)
            act = (g * _sigmoid(g) * u).astype(BF16)
            acc = acc + _dot(act, wd_ref[sl, :])
        x2_ref[...] = acc

    row = lambda w: pl.BlockSpec((tm, w), lambda i: (i, 0))
    return pl.pallas_call(
        body, grid=(T // tm,), name="ffn_fwd",
        in_specs=[row(D_MODEL)] + [WHOLE] * 4,
        out_specs=[row(D_MODEL), row(FFN_HIDDEN), row(FFN_HIDDEN)],
        out_shape=[jax.ShapeDtypeStruct((T, D_MODEL), F32), jax.ShapeDtypeStruct((T, FFN_HIDDEN), BF16),
                   jax.ShapeDtypeStruct((T, FFN_HIDDEN), BF16)],
        compiler_params=_cparams(),
    )(x1, gffn, wg, wu, wd)


def loss_head(y, target):
    T = y.shape[0]
    tm = min(ROW_TILE, T)

    def body(y_ref, t_ref, dy_ref, loss_ref):
        @pl.when(pl.program_id(0) == 0)
        def _():
            loss_ref[...] = jnp.zeros(loss_ref.shape, F32)

        err = y_ref[...] - t_ref[...]
        dy_ref[...] = err * (1.0 / D_MODEL)
        per_row = jnp.sum(err * err, axis=1, keepdims=True) * (1.0 / D_MODEL)
        loss_ref[...] += 0.5 * jnp.sum(per_row, axis=0, keepdims=True)

    row = pl.BlockSpec((tm, D_MODEL), lambda i: (i, 0))
    return pl.pallas_call(
        body, grid=(T // tm,), name="loss_head",
        in_specs=[row, row],
        out_specs=[row, pl.BlockSpec((1, 1), lambda i: (0, 0))],
        out_shape=[jax.ShapeDtypeStruct((T, D_MODEL), F32), jax.ShapeDtypeStruct((1, 1), F32)],
        compiler_params=_cparams(),
    )(y, target)


def _acc_spec(shape):
    return pl.BlockSpec(shape, lambda i: (0,) * len(shape))


def ffn_bwd(dx2, x1, gs, us, gffn, wg, wu, wd):
    T = x1.shape[0]
    tm = min(2 * ROW_TILE, T)
    nc = FFN_HIDDEN // FFN_CHUNK

    def body(dx2_ref, x1_ref, gs_ref, us_ref, gffn_ref, wg_ref, wu_ref, wd_ref,
             dx1_ref, h2_ref, act_ref, dg_ref, du_ref, dgffn_ref, dy_s, dh2_s):
        i, c = pl.program_id(0), pl.program_id(1)

        @pl.when((i == 0) & (c == 0))
        def _():
            dgffn_ref[...] = jnp.zeros(dgffn_ref.shape, F32)

        @pl.when(c == 0)
        def _():
            dy_s[...] = dx2_ref[...].astype(BF16)
            h2_ref[...] = _rms(x1_ref[...], gffn_ref[...], D_MODEL)[0].astype(BF16)
            dh2_s[...] = jnp.zeros(dh2_s.shape, F32)

        g = gs_ref[...].astype(F32)
        u = us_ref[...].astype(F32)
        dact = _dot_nt(dy_s[...], wd_ref[...])
        sg = _sigmoid(g)
        silu = g * sg
        act_ref[...] = (silu * u).astype(BF16)
        dg = (dact * u * (sg * (1.0 + g * (1.0 - sg)))).astype(BF16)
        du = (dact * silu).astype(BF16)
        dg_ref[...] = dg
        du_ref[...] = du
        dh2_s[...] += _dot_nt(dg, wg_ref[...]) + _dot_nt(du, wu_ref[...])

        @pl.when(c == nc - 1)
        def _():
            x1v = x1_ref[...]
            r = _rms(x1v, gffn_ref[...], D_MODEL)[1]
            dxn, dgn = _rms_bwd(x1v, r, gffn_ref[...], dh2_s[...], D_MODEL)
            dx1_ref[...] = dx2_ref[...] + dxn
            dgffn_ref[...] += dgn

    row = lambda w: pl.BlockSpec((tm, w), lambda i, c: (i, 0))
    chunk = pl.BlockSpec((tm, FFN_CHUNK), lambda i, c: (i, c))
    w_cols = pl.BlockSpec((D_MODEL, FFN_CHUNK), lambda i, c: (0, c))
    return pl.pallas_call(
        body, grid=(T // tm, nc), name="ffn_bwd",
        in_specs=[row(D_MODEL), row(D_MODEL), chunk, chunk, WHOLE, w_cols, w_cols,
                  pl.BlockSpec((FFN_CHUNK, D_MODEL), lambda i, c: (c, 0))],
        out_specs=[row(D_MODEL), row(D_MODEL), chunk, chunk, chunk, pl.BlockSpec((1, D_MODEL), lambda i, c: (0, 0))],
        out_shape=[jax.ShapeDtypeStruct((T, D_MODEL), F32), jax.ShapeDtypeStruct((T, D_MODEL), BF16),
                   jax.ShapeDtypeStruct((T, FFN_HIDDEN), BF16), jax.ShapeDtypeStruct((T, FFN_HIDDEN), BF16),
                   jax.ShapeDtypeStruct((T, FFN_HIDDEN), BF16), jax.ShapeDtypeStruct((1, D_MODEL), F32)],
        scratch_shapes=[pltpu.VMEM((tm, D_MODEL), BF16), pltpu.VMEM((tm, D_MODEL), F32)],
        compiler_params=_cparams(),
    )(dx2, x1, gs, us, gffn, wg, wu, wd)


TN_K_TILE = 1024
TN_ACC_BYTES = 6 * 1024 * 1024


def matmul_tn(a, b, name):
    T, M = a.shape
    N = b.shape[1]
    tk = min(TN_K_TILE, T)
    tm = M if M <= 1024 else M // 2
    tn = max(d for d in range(128, N + 1, 128) if N % d == 0 and tm * d * 4 <= TN_ACC_BYTES)
    nk = T // tk

    def body(a_ref, b_ref, o_ref, acc):
        k = pl.program_id(2)

        @pl.when(k == 0)
        def _():
            acc[...] = jnp.zeros(acc.shape, F32)

        acc[...] += _dot_tn(a_ref[...].astype(BF16), b_ref[...].astype(BF16))

        @pl.when(k == nk - 1)
        def _():
            o_ref[...] = acc[...].astype(BF16)

    return pl.pallas_call(
        body, grid=(M // tm, N // tn, nk), name=name,
        in_specs=[pl.BlockSpec((tk, tm), lambda i, j, k: (k, i)), pl.BlockSpec((tk, tn), lambda i, j, k: (k, j))],
        out_specs=pl.BlockSpec((tm, tn), lambda i, j, k: (i, j)),
        out_shape=jax.ShapeDtypeStruct((M, N), BF16),
        scratch_shapes=[pltpu.VMEM((tm, tn), F32)],
        compiler_params=_cparams(),
    )(a, b)


def mixers_bwd(dx1, a, z, gsgu, wcat, wcat_t, bz, wp, pscale, goa, gos, gop, wout):
    T = a.shape[0]
    tm = min(ROW_TILE, T)

    def body(dx1_ref, a_ref, u_ref, vs_ref, pin_ref, halo_ref, gsgu_ref, wcat_ref, wcatt_ref, bz_ref, wp_ref,
             ps_ref, goa_ref, gos_ref, gop_ref, wout_ref,
             da_ref, delta_ref, du_ref, dvs_ref, dm_ref, mix_ref,
             dgoa_ref, dgos_ref, dgop_ref, dps_ref, dwp_ref, dwsp_ref, db_ref, dgsgu_ref):
        i = pl.program_id(0)

        @pl.when(i == 0)
        def _():
            for r in (dgoa_ref, dgos_ref, dgop_ref, dps_ref, dwp_ref, dwsp_ref, db_ref, dgsgu_ref):
                r[...] = jnp.zeros(r.shape, F32)

        a_v, u, vs = a_ref[...], u_ref[...], vs_ref[...]
        goa, gos, gop, pscale_v = goa_ref[...], gos_ref[...], gop_ref[...], ps_ref[...]
        vn, rv, zc, gm, m, yp_pre, yp, ra, rg, rp, mix = _mixers_fwd_tile(
            i, tm, a_v, u, vs, pin_ref[...], halo_ref[...], gsgu_ref[...], wcat_ref[...], bz_ref[...], wp_ref[...],
            pscale_v, goa, gos, gop)
        mix_ref[...] = mix
        dmix = _dot_nt(dx1_ref[...].astype(BF16), wout_ref[...])
        da, dgoa = _rms_bwd(a_v, ra, goa, dmix[:, :MLA_WIDTH], MLA_WIDTH)
        dgm, dgos = _rms_bwd(gm, rg, gos, dmix[:, MLA_WIDTH:MLA_WIDTH + SGU_WIDTH], SGU_WIDTH)
        dyp, dgop = _rms_bwd(yp, rp, gop, dmix[:, MLA_WIDTH + SGU_WIDTH:], POOL_WIDTH)
        da_ref[...] = da
        dgoa_ref[...] += dgoa
        dgos_ref[...] += dgos
        dgop_ref[...] += dgop
        prod = da * a_v
        ones = jnp.ones((8, HEAD_PAD), F32)
        for h in range(HEADS):
            lo = HEAD_PAD * h
            sums = lax.dot_general(ones, prod[:, lo:lo + HEAD_PAD], (((1,), (1,)), ((), ())), preferred_element_type=F32,
                                   precision=lax.Precision.HIGHEST)
            delta_ref[h] = sums[0:1, :]
        dps_ref[...] += jnp.sum(dyp * yp_pre, axis=0, keepdims=True)
        dyp_pre = (dyp * pscale_v).astype(BF16)
        dwp_ref[...] += _dot_tn(m, dyp_pre)
        dm_ref[...] = _dot_nt(dyp_pre, wp_ref[...])
        du_ref[...] = dgm * zc
        dzc = dgm * u
        masks = _head_masks((CHUNK, SGU_WIDTH), SGU_HEAD_DIM)
        lane_b = lax.broadcasted_iota(jnp.int32, (CHUNK, HEAD_PAD), 1)
        dvn_parts = []
        dwsp = jnp.zeros(dwsp_ref.shape, F32)
        db = jnp.zeros(db_ref.shape, F32)
        for c in range(tm // CHUNK):
            dz_c = dzc[CHUNK * c:CHUNK * (c + 1)]
            dzstack = jnp.concatenate([jnp.where(mk, dz_c, 0.0).astype(BF16) for mk in masks], axis=0)
            dvn_parts.append(_dot(wcatt_ref[...], dzstack))
            dwsp = dwsp + _dot_nt(dzstack, vn[CHUNK * c:CHUNK * (c + 1)].astype(BF16))
            for h, mk in enumerate(masks):
                col = jnp.sum(jnp.where(mk, dz_c, 0.0), axis=1, keepdims=True)
                db = db + jnp.where(lane_b == h, col, 0.0)
        dwsp_ref[...] += dwsp
        db_ref[...] += db
        dvs, dgsgu = _rms_bwd(vs, rv, gsgu_ref[...], jnp.concatenate(dvn_parts, axis=0), SGU_WIDTH)
        dvs_ref[...] = dvs
        dgsgu_ref[...] += dgsgu

    row = lambda w: pl.BlockSpec((tm, w), lambda i: (i, 0))
    head = pl.BlockSpec((HEADS, 1, tm), lambda i: (0, 0, i))
    acc_shapes = [(1, MLA_WIDTH), (1, SGU_WIDTH), (1, POOL_WIDTH), (1, POOL_WIDTH), (POOL_WIDTH, POOL_WIDTH),
                  (HEADS * CHUNK, CHUNK), (CHUNK, HEAD_PAD), (1, SGU_WIDTH)]
    return pl.pallas_call(
        body, grid=(T // tm,), name="mixers_bwd",
        in_specs=[row(D_MODEL), row(MLA_WIDTH)] + _z_specs(tm) + [WHOLE] * 10,
        out_specs=[row(MLA_WIDTH), head, row(256), row(256), row(256), row(D_MODEL)] + [_acc_spec(s) for s in acc_shapes],
        out_shape=[jax.ShapeDtypeStruct((T, MLA_WIDTH), F32), jax.ShapeDtypeStruct((HEADS, 1, T), F32),
                   jax.ShapeDtypeStruct((T, 256), F32), jax.ShapeDtypeStruct((T, 256), F32),
                   jax.ShapeDtypeStruct((T, 256), F32), jax.ShapeDtypeStruct((T, D_MODEL), BF16)]
                  + [jax.ShapeDtypeStruct(s, F32) for s in acc_shapes],
        compiler_params=_cparams(),
    )(dx1, a, z, z, z, z, gsgu, wcat, wcat_t, bz, wp, pscale, goa, gos, gop, wout)


def pool_bwd(dm):
    T = dm.shape[0]
    tm = min(ROW_TILE, T)
    nt = T // tm

    def body(dm_ref, next_ref, dpin_ref):
        i = pl.program_id(0)
        cnt, lane = _pool_counts(i, tm)
        dmv = dm_ref[...]
        win = _by_group(lane[:POOL_HALO], 2.0, 4.0, 8.0, 16.0)
        nxt = jnp.where(i < nt - 1, next_ref[...] / win, 0.0)
        r1 = jnp.concatenate([dmv / cnt, nxt], axis=0)
        n = tm + POOL_HALO
        r2 = r1 + pltpu.roll(r1, n - 1, 0)
        r4 = r2 + pltpu.roll(r2, n - 2, 0)
        r8 = r4 + pltpu.roll(r4, n - 4, 0)
        r16 = r8 + pltpu.roll(r8, n - 8, 0)
        dpin_ref[...] = _by_group(lane, r2[:tm], r4[:tm], r8[:tm], r16[:tm]) - dmv

    return pl.pallas_call(
        body, grid=(nt,), name="pool_bwd",
        in_specs=[pl.BlockSpec((tm, 256), lambda i: (i, 0)),
                  pl.BlockSpec((POOL_HALO, 256), lambda i: (jnp.minimum((i + 1) * (tm // POOL_HALO), T // POOL_HALO - 1), 0))],
        out_specs=pl.BlockSpec((tm, 256), lambda i: (i, 0)),
        out_shape=jax.ShapeDtypeStruct((T, 256), F32),
        compiler_params=_cparams(),
    )(dm, dm)


def attn_bwd(q, k, kt, v, do, lse, delta):
    _, T, _ = q.shape
    tb = min(ATTN_TILE, T)
    sb = min(ATTN_SUB, tb)
    ns = tb // sb
    nb = T // tb
    hb = ATTN_BWD_HEADS

    def body(q_ref, k_ref, kt_ref, v_ref, do_ref, lse_ref, delta_ref, dqt_ref, dk_ref, dv_ref, dk_s, dv_s):
        i, j = pl.program_id(1), pl.program_id(2)

        @pl.when((i == 0) & (j == 0))
        def _():
            dqt_ref[...] = jnp.zeros(dqt_ref.shape, F32)

        @pl.when(j == i)
        def _():
            dk_s[...] = jnp.zeros(dk_s.shape, F32)
            dv_s[...] = jnp.zeros(dv_s.shape, F32)

        def sub_block(h, a, b, masked):
            qa = slice(sb * a, sb * (a + 1))
            kb = slice(sb * b, sb * (b + 1))
            qv = q_ref[h, qa, :]
            dov = do_ref[qa, HEAD_PAD * h:HEAD_PAD * (h + 1)].astype(BF16)
            st = _dot_nt(k_ref[h, kb, :], qv)
            pt = jnp.exp2(st - lse_ref[h, :, qa])
            if masked:
                krow = lax.broadcasted_iota(jnp.int32, st.shape, 0)
                qcol = lax.broadcasted_iota(jnp.int32, st.shape, 1)
                pt = jnp.where(krow <= qcol, pt, 0.0)
            dv_s[h, kb, :] += _dot(pt.astype(BF16), dov)
            dpt = _dot_nt(v_ref[h, kb, :], dov)
            dst = (pt * (dpt - delta_ref[h, :, qa])).astype(BF16)
            dk_s[h, kb, :] += _dot(dst, qv)
            cols = pl.ds(pl.multiple_of(j * tb + sb * a, sb), sb)
            dqt_ref[h, :, cols] += _dot(kt_ref[h, :, kb], dst)

        @pl.when(j > i)
        def _():
            for a in range(ns):
                for h in range(hb):
                    for b in range(ns):
                        sub_block(h, a, b, False)

        @pl.when(j == i)
        def _():
            for a in range(ns):
                for h in range(hb):
                    for b in range(a + 1):
                        sub_block(h, a, b, a == b)

        @pl.when(j == nb - 1)
        def _():
            dk_ref[...] = dk_s[...] * (1.0 / LOG2E)
            dv_ref[...] = dv_s[...]

    qspec = pl.BlockSpec((hb, tb, HEAD_PAD), lambda g, i, j: (g, jnp.maximum(i, j), 0))
    kspec = pl.BlockSpec((hb, tb, HEAD_PAD), lambda g, i, j: (g, i, 0))
    ktspec = pl.BlockSpec((hb, HEAD_PAD, tb), lambda g, i, j: (g, 0, i))
    rowspec = pl.BlockSpec((hb, 1, tb), lambda g, i, j: (g, 0, jnp.maximum(i, j)))
    return pl.pallas_call(
        body, grid=(HEADS // hb, nb, nb), name="attn_bwd",
        in_specs=[qspec, kspec, ktspec, kspec, pl.BlockSpec((tb, hb * HEAD_PAD), lambda g, i, j: (jnp.maximum(i, j), g)), rowspec, rowspec],
        out_specs=[pl.BlockSpec((hb, HEAD_PAD, T), lambda g, i, j: (g, 0, 0)), kspec, kspec],
        out_shape=[jax.ShapeDtypeStruct((HEADS, HEAD_PAD, T), F32)] + [jax.ShapeDtypeStruct((HEADS, T, HEAD_PAD), F32)] * 2,
        scratch_shapes=[pltpu.VMEM((hb, tb, HEAD_PAD), F32), pltpu.VMEM((hb, tb, HEAD_PAD), F32)],
        compiler_params=_cparams(),
    )(q, k, kt, v, do, lse, delta)


def mixin_bwd(dres, x, z, rope, dqt, dk, dv, du, dvs, dpin, gmix, win, gql, wq, gkvl, wkv, gqh, gkh):
    T = x.shape[0]
    tm = min(ROW_TILE, T)
    part = min(MIXIN_PART, tm)

    def body(dres_ref, x_ref, z_ref, c_ref, sa_ref, sb_ref, dqt_ref, dk_ref, dv_ref, du_ref, dvs_ref, dpin_ref,
             gmix_ref, win_ref, gql_ref, wq_ref, gkvl_ref, wkv_ref, gqh_ref, gkh_ref,
             dx_ref, hn_ref, dz_ref, qn_ref, dqup_ref, kvn_ref, dkvup_ref,
             dgmix_ref, dgql_ref, dgkvl_ref, dgqh_ref, dgkh_ref):
        @pl.when(pl.program_id(0) == 0)
        def _():
            for r in (dgmix_ref, dgql_ref, dgkvl_ref, dgqh_ref, dgkh_ref):
                r[...] = jnp.zeros(r.shape, F32)

        lane = lax.broadcasted_iota(jnp.int32, (part, HEAD_PAD), 1)
        rope_lanes = (lane >= QK_NOPE) & (lane < QK_HEAD)
        for p in range(tm // part):
            rows = pl.ds(part * p, part)
            xv = x_ref[rows, :]
            hn, rx = _rms(xv, gmix_ref[...], D_MODEL)
            hn_ref[rows, :] = hn.astype(BF16)
            ql, kvl, kr, qn, rq, kvn, rkv, q_up, kv_up = _qkv_pre(z_ref[rows, :], gql_ref[...], wq_ref[...], gkvl_ref[...],
                                                                  wkv_ref[...])
            qn_ref[rows, :] = qn
            kvn_ref[rows, :] = kvn
            c, sa, sb = c_ref[rows, :], sa_ref[rows, :], sb_ref[rows, :]
            dkr = jnp.zeros((part, HEAD_PAD), F32)
            dgqh = jnp.zeros((1, HEAD_PAD), F32)
            dgkh = jnp.zeros((1, HEAD_PAD), F32)
            dq_parts, dk_parts, dv_parts = [], [], []
            for h in range(HEADS):
                lo = HEAD_PAD * h
                qh = q_up[:, lo:lo + HEAD_PAD]
                rqh = lax.rsqrt(jnp.sum(qh * qh, axis=-1, keepdims=True) * (1.0 / QK_HEAD) + EPS)
                dq_h = jnp.transpose(dqt_ref[h, :, rows]) * ATTN_SCALE
                dqh, dg = _rms_bwd(qh, rqh, gqh_ref[...], _rope_bwd(dq_h, c, sa, sb), QK_HEAD)
                dgqh = dgqh + dg
                dq_parts.append(dqh)
                kh = kv_up[:, lo:lo + HEAD_PAD] + kr
                rkh = lax.rsqrt(jnp.sum(kh * kh, axis=-1, keepdims=True) * (1.0 / QK_HEAD) + EPS)
                dkh, dg = _rms_bwd(kh, rkh, gkh_ref[...], _rope_bwd(dk_ref[h, rows, :], c, sa, sb), QK_HEAD)
                dgkh = dgkh + dg
                dkr = dkr + jnp.where(rope_lanes, dkh, 0.0)
                dk_parts.append(dkh)
                dv_parts.append(dv_ref[h, rows, :])
            dgqh_ref[...] += dgqh
            dgkh_ref[...] += dgkh
            dq_up = jnp.concatenate(dq_parts, axis=1).astype(BF16)
            dkv_up = jnp.concatenate(dk_parts + dv_parts, axis=1).astype(BF16)
            dqup_ref[rows, :] = dq_up
            dkvup_ref[rows, :] = dkv_up
            dql, dg = _rms_bwd(ql, rq, gql_ref[...], _dot_nt(dq_up, wq_ref[...]), Q_LORA)
            dgql_ref[...] += dg
            dkvl, dg = _rms_bwd(kvl, rkv, gkvl_ref[...], _dot_nt(dkv_up, wkv_ref[...]), KV_LORA)
            dgkvl_ref[...] += dg
            dz = jnp.concatenate([dql, dkvl, dkr, du_ref[rows, :], dvs_ref[rows, :], dpin_ref[rows, :]], axis=1).astype(BF16)
            dz_ref[rows, :] = dz
            dxn, dg = _rms_bwd(xv, rx, gmix_ref[...], _dot_nt(dz, win_ref[...]), D_MODEL)
            dgmix_ref[...] += dg
            dx_ref[rows, :] = dres_ref[rows, :] + dxn

    row = lambda w: pl.BlockSpec((tm, w), lambda i: (i, 0))
    head = pl.BlockSpec((HEADS, tm, HEAD_PAD), lambda i: (0, i, 0))
    head_t = pl.BlockSpec((HEADS, HEAD_PAD, tm), lambda i: (0, 0, i))
    acc_shapes = [(1, D_MODEL), (1, Q_LORA), (1, KV_LORA), (1, HEAD_PAD), (1, HEAD_PAD)]
    out_rows = [(D_MODEL, F32), (D_MODEL, BF16), (Z_WIDTH, BF16), (Q_LORA, BF16), (HEADS * HEAD_PAD, BF16),
                (KV_LORA, BF16), (2 * HEADS * HEAD_PAD, BF16)]
    return pl.pallas_call(
        body, grid=(T // tm,), name="mixin_bwd",
        in_specs=[row(D_MODEL), row(D_MODEL), row(Z_WIDTH)] + [row(HEAD_PAD)] * 3 + [head_t, head, head, row(256), row(256), row(256)]
                 + [WHOLE] * 8,
        out_specs=[row(w) for w, _ in out_rows] + [_acc_spec(s) for s in acc_shapes],
        out_shape=[jax.ShapeDtypeStruct((T, w), dt) for w, dt in out_rows] + [jax.ShapeDtypeStruct(s, F32) for s in acc_shapes],
        compiler_params=_cparams(),
    )(dres, x, z, *rope, dqt, dk, dv, du, dvs, dpin, gmix, win, gql, wq, gkvl, wkv, gqh, gkh)


def _place():
    x, y, c = lax.axis_index("x"), lax.axis_index("y"), lax.axis_index("c")
    return x, y, c, 4 * x + 2 * y + c


def cast_shards(shards, wanted):
    n = len(shards)

    def body(*refs):
        for o_ref, (w, l) in zip(refs[n:], wanted):
            o_ref[...] = refs[w][l].astype(BF16)

    return pl.pallas_call(
        body, name="cast_shards", in_specs=[WHOLE] * n, out_specs=[WHOLE] * len(wanted),
        out_shape=[jax.ShapeDtypeStruct(shards[w].shape[1:], BF16) for w, _ in wanted],
        compiler_params=_cparams(),
    )(*shards)


def allgather_layer0(shards):
    n = len(shards)

    def body(*refs):
        ins, outs = refs[:n], refs[n:2 * n]
        stage = refs[2 * n:3 * n]
        send_sems, recv_sems, local_sems = refs[3 * n:]
        x, y, c, me = _place()
        sibling = (x, y, 1 - c)
        chips = [(1 - x, y), (x, 1 - y), (1 - x, 1 - y)]

        def copy(k, w, block_id, to, from_stage):
            return pltpu.make_async_remote_copy(
                src_ref=stage[w] if from_stage else outs[w].at[block_id], dst_ref=outs[w].at[block_id],
                send_sem=send_sems.at[k, w], recv_sem=recv_sems.at[k, w], device_id=to, device_id_type=MESH)

        def block_of(cx, cy, cc):
            return 4 * cx + 2 * cy + cc

        local, sent = [], []
        for w in range(n):
            stage[w][...] = ins[w][0].astype(BF16)
            mine = pltpu.make_async_copy(stage[w], outs[w].at[me], local_sems.at[w])
            mine.start()
            local.append(mine)
            first = [copy(0, w, me, sibling, True)] + [copy(1 + j, w, me, (*chip, c), True) for j, chip in enumerate(chips)]
            for cp in first:
                cp.start()
            sent += first
        for j, chip in enumerate(chips):
            for w in range(n):
                copy(1 + j, w, block_of(*chip, c), (x, y, c), False).wait_recv()
                fwd = copy(4 + j, w, block_of(*chip, c), sibling, False)
                fwd.start()
                sent.append(fwd)
        for w in range(n):
            copy(0, w, block_of(x, y, 1 - c), (x, y, c), False).wait_recv()
            for j, chip in enumerate(chips):
                copy(4 + j, w, block_of(*chip, 1 - c), (x, y, c), False).wait_recv()
        for cp in local:
            cp.wait()
        for cp in sent:
            cp.wait_send()

    return pl.pallas_call(
        body, name="allgather_layer0",
        in_specs=[WHOLE] * n, out_specs=[ANY] * n,
        out_shape=[jax.ShapeDtypeStruct((N_DEV,) + s.shape[1:], BF16) for s in shards],
        scratch_shapes=[pltpu.VMEM(s.shape[1:], BF16) for s in shards]
                       + [pltpu.SemaphoreType.DMA((7, n)), pltpu.SemaphoreType.DMA((7, n)), pltpu.SemaphoreType.DMA((n,))],
        compiler_params=_cparams(),
    )(*shards)


def _peer(k):
    x, y, c, _ = _place()
    px = 1 - x if k & 4 else x
    py = 1 - y if k & 2 else y
    pc = 1 - c if k & 1 else c
    return (px, py, pc), 4 * px + 2 * py + pc


def exchange_start(srcs, after, gather, name):
    n = len(srcs)
    land_shapes = [((N_DEV,) + s.shape) if gather else s.shape for s in srcs]

    def body(*refs):
        src_refs, land_refs = refs[:n], refs[n:2 * n]
        send_sems, recv_sems = refs[2 * n + 1:3 * n + 1], refs[3 * n + 1:4 * n + 1]
        token = refs[-1]
        _, _, _, me = _place()
        for k in range(1, N_DEV):
            peer, peer_id = _peer(k)
            for w in range(n):
                pltpu.make_async_remote_copy(
                    src_ref=src_refs[w] if gather else src_refs[w].at[peer_id], dst_ref=land_refs[w].at[me],
                    send_sem=send_sems[w], recv_sem=recv_sems[w], device_id=peer, device_id_type=MESH).start()
        token[...] = jnp.zeros(token.shape, F32)

    hbm = lambda a: pltpu.with_memory_space_constraint(a, pltpu.HBM)
    outs = pl.pallas_call(
        body, name=name,
        out_shape=(pltpu.SemaphoreType.DMA(()),) * (2 * n)
                  + tuple(pltpu.HBM(s.shape, BF16) for s in srcs) + tuple(pltpu.HBM(s, BF16) for s in land_shapes)
                  + (jax.ShapeDtypeStruct((8, 128), F32),),
        in_specs=[HBM_SPEC] * (2 * n) + [ANY],
        out_specs=(SEM_SPEC,) * (2 * n) + (HBM_SPEC,) * (2 * n) + (WHOLE,),
        input_output_aliases={i: 2 * n + i for i in range(2 * n)},
        compiler_params=pltpu.CompilerParams(has_side_effects=pltpu.SideEffectType.DATAFLOW_SIDE_EFFECTING),
    )(*[hbm(s) for s in srcs], *[hbm(lax.empty(s, BF16)) for s in land_shapes], after)
    return list(outs[:n]), list(outs[n:2 * n]), list(outs[2 * n:3 * n]), list(outs[3 * n:4 * n]), outs[-1]


def exchange_wait(started, after, name):
    send_sems, recv_sems, srcs, lands, _ = started
    n = len(srcs)

    def body(*refs):
        land_refs = refs[n:2 * n]
        send_sems, recv_sems = refs[2 * n:3 * n], refs[3 * n:4 * n]
        x, y, c, _ = _place()
        for w in range(n):
            seven = land_refs[w].at[pl.ds(0, N_DEV - 1)]
            cp = pltpu.make_async_remote_copy(src_ref=seven, dst_ref=seven, send_sem=send_sems[w], recv_sem=recv_sems[w],
                                              device_id=(x, y, c), device_id_type=MESH)
            cp.wait_send()
            cp.wait_recv()

    outs = pl.pallas_call(
        body, name=name,
        out_shape=tuple(pltpu.HBM(s.shape, BF16) for s in srcs) + tuple(pltpu.HBM(l.shape, BF16) for l in lands),
        in_specs=[HBM_SPEC] * (2 * n) + [SEM_SPEC] * (2 * n) + [ANY],
        out_specs=(HBM_SPEC,) * (2 * n),
        input_output_aliases={i: i for i in range(2 * n)},
        compiler_params=pltpu.CompilerParams(has_side_effects=pltpu.SideEffectType.DATAFLOW_SIDE_EFFECTING),
    )(*srcs, *lands, *send_sems, *recv_sems, after)
    _, _, _, me = _place()
    filled = []
    for src, land in zip(outs[:n], outs[n:]):
        own = src[None] if src.ndim == 2 else lax.dynamic_slice_in_dim(src, me, 1, axis=0)
        filled.append(lax.dynamic_update_slice_in_dim(land, own, me, axis=0))
    return filled


def exchange_grads(pieces):
    n = len(pieces)

    def body(*refs):
        ins, outs = refs[:n], refs[n:2 * n]
        send_sems, recv_sems, local_sems = refs[2 * n:]
        x, y, c, me = _place()
        local = [pltpu.make_async_copy(ins[w].at[me], outs[w].at[me], local_sems.at[w]) for w in range(n)]
        for cp in local:
            cp.start()
        remote = []
        for k in range(1, N_DEV):
            px = 1 - x if k & 4 else x
            py = 1 - y if k & 2 else y
            pc = 1 - c if k & 1 else c
            peer = 4 * px + 2 * py + pc
            for w in range(n):
                cp = pltpu.make_async_remote_copy(
                    src_ref=ins[w].at[peer], dst_ref=outs[w].at[me], send_sem=send_sems.at[k - 1, w],
                    recv_sem=recv_sems.at[k - 1, w], device_id=(px, py, pc), device_id_type=MESH)
                cp.start()
                remote.append(cp)
        for cp in local:
            cp.wait()
        for cp in remote:
            cp.wait_recv()
        for cp in remote:
            cp.wait_send()

    return pl.pallas_call(
        body, name="exchange_grads",
        in_specs=[ANY] * n, out_specs=[ANY] * n,
        out_shape=[jax.ShapeDtypeStruct(p.shape, BF16) for p in pieces],
        scratch_shapes=[pltpu.SemaphoreType.DMA((7, n)), pltpu.SemaphoreType.DMA((7, n)), pltpu.SemaphoreType.DMA((n,))],
        compiler_params=_cparams(),
    )(*pieces)


def _adamw(w, g, m, v):
    m2 = ADAM_B1 * m + (1.0 - ADAM_B1) * g
    v2 = ADAM_B2 * v + (1.0 - ADAM_B2) * (g * g)
    m_hat = m2 / (1.0 - ADAM_B1 ** ADAM_STEP)
    v_hat = v2 / (1.0 - ADAM_B2 ** ADAM_STEP)
    delta = -ADAM_LR * (m_hat / (jnp.sqrt(v_hat) + ADAM_EPS) + ADAM_WD * w)
    return delta, m2, v2


def adamw_sharded(parts0, parts1, w, m, v, name):
    _, R, C = parts0.shape
    br = max(d for d in range(16, min(R, 512) + 1, 16) if R % d == 0)
    nblk = R // br

    def body(p0_ref, p1_ref, w_ref, m_ref, v_ref, g_ref, d_ref, m2_ref, v2_ref):
        def total(p_ref):
            g = p_ref[0].astype(F32)
            for s in range(1, N_DEV):
                g = g + p_ref[s].astype(F32)
            return g

        g = jnp.where(pl.program_id(0) == 0, total(p0_ref), total(p1_ref))
        g_ref[...] = g
        d_ref[...], m2_ref[...], v2_ref[...] = _adamw(w_ref[...], g, m_ref[...], v_ref[...])

    blk = pl.BlockSpec((None, br, C), lambda l, i: (l, i, 0))
    p0 = pl.BlockSpec((N_DEV, br, C), lambda l, i: (0, jnp.where(l == 0, i, nblk - 1), 0))
    p1 = pl.BlockSpec((N_DEV, br, C), lambda l, i: (0, jnp.where(l == 1, i, 0), 0))
    return pl.pallas_call(
        body, grid=(DEPTH, nblk), name=name,
        in_specs=[p0, p1, blk, blk, blk],
        out_specs=[blk] * 4,
        out_shape=[jax.ShapeDtypeStruct((DEPTH, R, C), F32)] * 4,
        compiler_params=_cparams(),
    )(parts0, parts1, w, m, v)


def allreduce_adamw_small(gpart, w, m, v):
    R = gpart.shape[0]

    def body(g_ref, w_ref, m_ref, v_ref, grad_ref, d_ref, m2_ref, v2_ref, all_ref, send_sems, recv_sems):
        x, y, c, me = _place()
        sibling = (x, y, 1 - c)
        chips = [(1 - x, y), (x, 1 - y), (1 - x, 1 - y)]

        def copy(k, block_id, to, from_input):
            return pltpu.make_async_remote_copy(
                src_ref=g_ref if from_input else all_ref.at[block_id], dst_ref=all_ref.at[block_id],
                send_sem=send_sems.at[k], recv_sem=recv_sems.at[k], device_id=to, device_id_type=MESH)

        def block_of(cx, cy, cc):
            return 4 * cx + 2 * cy + cc

        all_ref[me] = g_ref[...]
        first = [copy(0, me, sibling, True)] + [copy(1 + j, me, (*chip, c), True) for j, chip in enumerate(chips)]
        for cp in first:
            cp.start()
        passed = [copy(4 + j, block_of(*chip, c), sibling, False) for j, chip in enumerate(chips)]
        for j, chip in enumerate(chips):
            copy(1 + j, block_of(*chip, c), (x, y, c), False).wait_recv()
            passed[j].start()
        copy(0, block_of(x, y, 1 - c), (x, y, c), False).wait_recv()
        for j, chip in enumerate(chips):
            copy(4 + j, block_of(*chip, 1 - c), (x, y, c), False).wait_recv()
        for cp in first + passed:
            cp.wait_send()
        g = all_ref[0]
        for s in range(1, N_DEV):
            g = g + all_ref[s]
        grad_ref[...] = g
        d_ref[...], m2_ref[...], v2_ref[...] = _adamw(w_ref[...], g, m_ref[...], v_ref[...])

    return pl.pallas_call(
        body, name="allreduce_adamw_small",
        in_specs=[WHOLE] * 4, out_specs=[WHOLE] * 4,
        out_shape=[jax.ShapeDtypeStruct((R, 128), F32)] * 4,
        scratch_shapes=[pltpu.VMEM((N_DEV, R, 128), F32), pltpu.SemaphoreType.DMA((7,)), pltpu.SemaphoreType.DMA((7,))],
        compiler_params=_cparams(),
    )(gpart, w, m, v)


def _cols_from_pieces(p):
    return p.transpose(1, 0, 2).reshape(p.shape[1], -1)


def _pieces_from_cols(w):
    return w.reshape(w.shape[0], N_DEV, -1).transpose(1, 0, 2)


def _pad_heads(w, width):
    r = w.shape[0]
    return jnp.pad(w.reshape(r, HEADS, width), ((0, 0), (0, 0), (0, HEAD_PAD - width))).reshape(r, HEADS * HEAD_PAD)


def _unpad_heads(w, width):
    r = w.shape[0]
    return w.reshape(r, HEADS, HEAD_PAD)[:, :, :width].reshape(r, HEADS * width)


O1 = Q_LORA
O2 = O1 + KV_LORA
O3 = O2 + QK_ROPE


def _mixer_weights(gw):
    w_in = _cols_from_pieces(gw["w_in"])
    zero = lambda n: jnp.zeros((D_MODEL, n), BF16)
    win = jnp.concatenate([w_in[:, :O2], zero(QK_NOPE), w_in[:, O2:O3], zero(HEAD_PAD - QK_HEAD), w_in[:, O3:]], axis=1)
    wq = _pad_heads(_cols_from_pieces(gw["w_q_up"]), QK_HEAD)
    w_kv = _cols_from_pieces(gw["w_kv_up"]).reshape(KV_LORA, HEADS, QK_NOPE + V_HEAD)
    wk = jnp.pad(w_kv[:, :, :QK_NOPE], ((0, 0), (0, 0), (0, HEAD_PAD - QK_NOPE))).reshape(KV_LORA, HEADS * HEAD_PAD)
    wv = w_kv[:, :, QK_NOPE:].reshape(KV_LORA, HEADS * V_HEAD)
    wkv = jnp.concatenate([wk, wv], axis=1)
    wout = gw["w_out"].reshape(D_MODEL, D_MODEL)
    return dict(win=win, wq=wq, wkv=wkv, wout=wout)


def _ffn_weights(gw):
    return dict(wg=_cols_from_pieces(gw["w_gate"]), wu=_cols_from_pieces(gw["w_up"]),
                wd=gw["w_down"].reshape(FFN_HIDDEN, D_MODEL))


def _layer_small(p, l):
    row = lambda a: a.reshape(1, -1)
    pad_head = lambda g: jnp.pad(g, (0, HEAD_PAD - QK_HEAD)).reshape(1, HEAD_PAD)
    tril = jnp.tril(jnp.ones((CHUNK, CHUNK), F32))
    wsp = p["w_spatial"][l] * tril
    wcat = jnp.concatenate([wsp[h] for h in range(HEADS)], axis=1).astype(BF16)
    wcat_t = jnp.concatenate([wsp[h].T for h in range(HEADS)], axis=1).astype(BF16)
    bz = jnp.repeat(p["b_spatial"][l].T, SGU_HEAD_DIM, axis=1)
    wp = jax.scipy.linalg.block_diag(*[p["w_pool"][l][g] for g in range(HEADS)]).astype(BF16)
    return dict(gmix=row(p["g_mix_norm"][l]), gql=row(p["g_q_lat"][l]), gkvl=row(p["g_kv_lat"][l]),
                gqh=pad_head(p["g_q_head"][l]), gkh=pad_head(p["g_k_head"][l]), gsgu=row(p["g_sgu_v"][l]),
                wcat=wcat, wcat_t=wcat_t, bz=bz, wp=wp, pscale=row(p["pool_scale"][l]),
                goa=row(p["g_out_mla"][l]), gos=row(p["g_out_sgu"][l]), gop=row(p["g_out_pool"][l]),
                gffn=row(p["g_ffn_norm"][l]))


MIXER_SIDE = ("w_in", "w_q_up", "w_kv_up", "w_out")
FFN_SIDE = ("w_gate", "w_up", "w_down")
SHARDED = MIXER_SIDE + FFN_SIDE
SMALL = ("g_mix_norm", "g_q_lat", "g_kv_lat", "g_q_head", "g_k_head", "g_sgu_v", "w_spatial", "b_spatial", "w_pool",
         "pool_scale", "g_out_mla", "g_out_sgu", "g_out_pool", "g_ffn_norm")
WEIGHTS = ("g_mix_norm", "w_in", "g_q_lat", "w_q_up", "g_kv_lat", "w_kv_up", "g_q_head", "g_k_head", "g_sgu_v", "w_spatial",
           "b_spatial", "w_pool", "pool_scale", "g_out_mla", "g_out_sgu", "g_out_pool", "w_out", "g_ffn_norm", "w_gate",
           "w_up", "w_down")
PACK_ROWS = 8 * 128


def _pack_small(parts):
    flat = []
    for name in SMALL:
        a = parts[name].reshape(-1)
        flat.append(jnp.pad(a, (0, -a.shape[0] % PACK_ROWS)))
    return jnp.concatenate(flat).reshape(-1, 128)


def _unpack_small(packed, like):
    out, row = {}, 0
    for name in SMALL:
        n = math.prod(like[name].shape)
        rows = -(-n // PACK_ROWS) * 8
        out[name] = packed[row:row + rows].reshape(-1)[:n].reshape(like[name].shape)
        row += rows
    return out


def _forward_mixers(x, rope, W, S):
    z, q, k, v, kt, vt = mixin_fwd(x, rope, S["gmix"], W["win"], S["gql"], W["wq"], S["gkvl"], W["wkv"], S["gqh"], S["gkh"])
    a, lse = attn_fwd(q, k, vt)
    x1 = mixers_fwd(x, a, z, S["gsgu"], S["wcat"], S["bz"], S["wp"], S["pscale"], S["goa"], S["gos"], S["gop"], W["wout"])
    return x1, dict(x=x, z=z, q=q, k=k, kt=kt, v=v, a=a, lse=lse, x1=x1)


def _backward_ffn(dx2, W, S, A, l):
    dx1, h2, act, dg, du_ffn, dgffn = ffn_bwd(dx2, A["x1"], A["gs"], A["us"], S["gffn"], W["wg"], W["wu"], W["wd"])
    d_wd = matmul_tn(act, dx2, f"dw_down_{l}")
    d_wg = matmul_tn(h2, dg, f"dw_gate_{l}")
    d_wu = matmul_tn(h2, du_ffn, f"dw_up_{l}")
    big = dict(w_gate=_pieces_from_cols(d_wg), w_up=_pieces_from_cols(d_wu), w_down=d_wd.reshape(N_DEV, -1, D_MODEL))
    return dx1, big, dgffn


def _backward_mixers(dx1, dgffn, rope, W, S, A, l):
    (da, delta, du, dvs, dm, mix, dgoa, dgos, dgop, dps, dwp, dwsp, db, dgsgu) = mixers_bwd(
        dx1, A["a"], A["z"], S["gsgu"], S["wcat"], S["wcat_t"], S["bz"], S["wp"], S["pscale"], S["goa"], S["gos"],
        S["gop"], W["wout"])
    d_wout = matmul_tn(mix, dx1, f"dw_out_{l}")
    dpin = pool_bwd(dm)
    dqt, dk, dv = attn_bwd(A["q"], A["k"], A["kt"], A["v"], da, A["lse"], delta)
    (dx, hn, dz, qn, dq_up, kvn, dkv_up, dgmix, dgql, dgkvl, dgqh, dgkh) = mixin_bwd(
        dx1, A["x"], A["z"], rope, dqt, dk, dv, du, dvs, dpin, S["gmix"], W["win"], S["gql"], W["wq"], S["gkvl"],
        W["wkv"], S["gqh"], S["gkh"])
    d_win = matmul_tn(hn, dz, f"dw_in_{l}")
    d_wq = matmul_tn(qn, dq_up, f"dw_q_up_{l}")
    d_wkv = matmul_tn(kvn, dkv_up, f"dw_kv_up_{l}")
    d_win = jnp.concatenate([d_win[:, :O2], d_win[:, O2 + QK_NOPE:O2 + QK_HEAD], d_win[:, O2 + HEAD_PAD:]], axis=1)
    d_wk = d_wkv[:, :HEADS * HEAD_PAD].reshape(KV_LORA, HEADS, HEAD_PAD)[:, :, :QK_NOPE]
    d_wv = d_wkv[:, HEADS * HEAD_PAD:].reshape(KV_LORA, HEADS, V_HEAD)
    d_wkv = jnp.concatenate([d_wk, d_wv], axis=2).reshape(KV_LORA, HEADS * (QK_NOPE + V_HEAD))
    big = dict(w_in=_pieces_from_cols(d_win), w_q_up=_pieces_from_cols(_unpad_heads(d_wq, QK_HEAD)),
               w_kv_up=_pieces_from_cols(d_wkv), w_out=d_wout.reshape(N_DEV, -1, D_MODEL))
    tril = jnp.tril(jnp.ones((CHUNK, CHUNK), F32))
    small = dict(g_mix_norm=dgmix[0], g_q_lat=dgql[0], g_kv_lat=dgkvl[0], g_q_head=dgqh[0, :QK_HEAD], g_k_head=dgkh[0, :QK_HEAD],
                 g_sgu_v=dgsgu[0], w_spatial=dwsp.reshape(HEADS, CHUNK, CHUNK) * tril, b_spatial=db[:, :HEADS].T,
                 w_pool=jnp.stack([dwp[64 * g:64 * (g + 1), 64 * g:64 * (g + 1)] for g in range(HEADS)]),
                 pool_scale=dps[0], g_out_mla=dgoa[0], g_out_sgu=dgos[0], g_out_pool=dgop[0], g_ffn_norm=dgffn[0])
    return dx, big, small


def kernel(x, positions, g_mix_norm, w_in, g_q_lat, w_q_up, g_kv_lat, w_kv_up, g_q_head, g_k_head, g_sgu_v, w_spatial, b_spatial, w_pool, pool_scale, g_out_mla, g_out_sgu, g_out_pool, w_out, g_ffn_norm, w_gate, w_up, w_down, loss_target, m_g_mix_norm, m_w_in, m_g_q_lat, m_w_q_up, m_g_kv_lat, m_w_kv_up, m_g_q_head, m_g_k_head, m_g_sgu_v, m_w_spatial, m_b_spatial, m_w_pool, m_pool_scale, m_g_out_mla, m_g_out_sgu, m_g_out_pool, m_w_out, m_g_ffn_norm, m_w_gate, m_w_up, m_w_down, v_g_mix_norm, v_w_in, v_g_q_lat, v_w_q_up, v_g_kv_lat, v_w_kv_up, v_g_q_head, v_g_k_head, v_g_sgu_v, v_w_spatial, v_b_spatial, v_w_pool, v_pool_scale, v_g_out_mla, v_g_out_sgu, v_g_out_pool, v_w_out, v_g_ffn_norm, v_w_gate, v_w_up, v_w_down):
    given = dict(locals())
    w = {n: given[n] for n in WEIGHTS}
    m = {n: given["m_" + n] for n in WEIGHTS}
    v = {n: given["v_" + n] for n in WEIGHTS}
    T = x.shape[1]
    xs = x.reshape(T, D_MODEL)

    half = QK_ROPE // 2
    inv_freq = 1.0 / (ROPE_THETA ** (jnp.arange(half, dtype=F32) / half))
    ang16 = positions.reshape(T).astype(F32)[:, None] * inv_freq
    ang = jnp.concatenate([jnp.zeros((T, QK_NOPE), F32), ang16, ang16, jnp.zeros((T, HEAD_PAD - QK_HEAD), F32)], axis=1)

    wanted = [(SHARDED.index(n), 0) for n in FFN_SIDE] + [(i, 1) for i in range(len(SHARDED))]
    bf = cast_shards([w[n] for n in SHARDED], wanted)
    mixer0 = allgather_layer0([w[n] for n in MIXER_SIDE])
    ag_ffn0 = exchange_start(bf[:len(FFN_SIDE)], mixer0[0], True, "ag_start_ffn0")
    ag_l1 = exchange_start(bf[len(FFN_SIDE):], ag_ffn0[-1], True, "ag_start_l1")
    rope = rope_tables(ang + (ag_ffn0[-1][0, 0] + ag_l1[-1][0, 0]))
    Ss = [_layer_small(w, l) for l in range(DEPTH)]

    x1, A0 = _forward_mixers(xs, rope, _mixer_weights(dict(zip(MIXER_SIDE, mixer0))), Ss[0])
    F0 = _ffn_weights(dict(zip(FFN_SIDE, exchange_wait(ag_ffn0, x1, "ag_wait_ffn0"))))
    h, A0["gs"], A0["us"] = ffn_fwd(x1, Ss[0]["gffn"], F0["wg"], F0["wu"], F0["wd"])
    layer1 = dict(zip(SHARDED, exchange_wait(ag_l1, h, "ag_wait_l1")))
    W1 = {**_mixer_weights(layer1), **_ffn_weights(layer1)}
    x1, A1 = _forward_mixers(h, rope, W1, Ss[1])
    h, A1["gs"], A1["us"] = ffn_fwd(x1, Ss[1]["gffn"], W1["wg"], W1["wu"], W1["wd"])
    dh, loss_part = loss_head(h, loss_target.reshape(T, D_MODEL))
    loss = lax.psum(loss_part[0, 0], ("x", "y", "c"))

    dx1, big_f1, dgffn1 = _backward_ffn(dh, W1, Ss[1], A1, 1)
    dh, big_m1, small1 = _backward_mixers(dx1, dgffn1, rope, W1, Ss[1], A1, 1)
    rs_l1 = exchange_start([{**big_m1, **big_f1}[n] for n in SHARDED], dh, False, "rs_start_l1")
    W0 = {**_mixer_weights(dict(zip(MIXER_SIDE, mixer0))), **F0}
    S0 = dict(Ss[0], gffn=Ss[0]["gffn"] + rs_l1[-1][0, 0])
    dx1, big_f0, dgffn0 = _backward_ffn(dh, W0, S0, A0, 0)
    rs_ffn0 = exchange_start([big_f0[n] for n in FFN_SIDE], dx1, False, "rs_start_ffn0")
    S0 = dict(Ss[0], gsgu=Ss[0]["gsgu"] + rs_ffn0[-1][0, 0])
    dh, big_m0, small0 = _backward_mixers(dx1, dgffn0, rope, W0, S0, A0, 0)
    grad_x = dh.reshape(x.shape)
    smalls = [small0, small1]

    parts1 = dict(zip(SHARDED, exchange_wait(rs_l1, dh, "rs_wait_l1")))
    parts0 = dict(zip(FFN_SIDE, exchange_wait(rs_ffn0, dh, "rs_wait_ffn0")))
    parts0.update(zip(MIXER_SIDE, exchange_grads([big_m0[n] for n in MIXER_SIDE])))
    grad, delta, new_m, new_v = {}, {}, {}, {}
    for n in SHARDED:
        grad[n], delta[n], new_m[n], new_v[n] = adamw_sharded(parts0[n], parts1[n], w[n], m[n], v[n], f"adamw_{n}")

    small_part = {n: jnp.stack([smalls[l][n] for l in range(DEPTH)]) for n in SMALL}
    outs = allreduce_adamw_small(_pack_small(small_part), _pack_small(w), _pack_small(m), _pack_small(v))
    for d, o in zip((grad, delta, new_m, new_v), outs):
        d.update(_unpack_small(o, w))

    return (loss, grad_x, *[grad[n] for n in WEIGHTS], *[delta[n] for n in WEIGHTS], *[new_m[n] for n in WEIGHTS],
            *[new_v[n] for n in WEIGHTS])
```

```python
import functools
import math

import jax
import jax.numpy as jnp
from jax import lax
from jax.experimental import pallas as pl
from jax.experimental.pallas import tpu as pltpu

F32 = jnp.float32
BF16 = jnp.bfloat16

N_DEV = 8
DEPTH = 2
D_MODEL = 1024
HEADS = 4
HEAD_PAD = 128
QK_NOPE = 64
QK_ROPE = 32
QK_HEAD = QK_NOPE + QK_ROPE
V_HEAD = 128
Q_LORA = 256
KV_LORA = 128
SGU_WIDTH = 256
SGU_HEAD_DIM = 64
CHUNK = 128
POOL_WIDTH = 256
POOL_WINDOWS = (2, 4, 8, 16)
POOL_HALO = 16
MLA_WIDTH = 512
IN_WIDTH = 1184
Z_WIDTH = 1280
FFN_HIDDEN = 2816
FFN_CHUNK = 256
ROPE_THETA = 10000.0
EPS = 1e-6
ATTN_SCALE = 1.0 / math.sqrt(QK_HEAD)
LOG2E = 1.4426950408889634
NEG_BIG = -1e30

ADAM_LR = 0.001
ADAM_B1 = 0.9
ADAM_B2 = 0.999
ADAM_EPS = 1e-08
ADAM_WD = 0.01
ADAM_STEP = 10

VMEM_LIMIT = 56 * 1024 * 1024
ROW_TILE = 512
MIXIN_PART = 256
MIXIN_BWD_PART = 512
ATTN_TILE = 1024
ATTN_SUB = 512
ATTN_FWD_SUB = 1024
ATTN_BWD_HEADS = 2
MESH = pl.DeviceIdType.MESH

WHOLE = pl.BlockSpec(memory_space=pltpu.VMEM)
ANY = pl.BlockSpec(memory_space=pl.ANY)
HBM_SPEC = pl.BlockSpec(memory_space=pltpu.HBM)
SEM_SPEC = pl.BlockSpec(memory_space=pltpu.SEMAPHORE)


def _cparams(**kw):
    return pltpu.CompilerParams(vmem_limit_bytes=VMEM_LIMIT, **kw)


def _dot(a, b):
    return jnp.dot(a, b, preferred_element_type=F32)


def _dot_nt(a, b):
    return lax.dot_general(a, b, (((1,), (1,)), ((), ())), preferred_element_type=F32)


def _dot_tn(a, b):
    return lax.dot_general(a, b, (((0,), (0,)), ((), ())), preferred_element_type=F32)


def _rms(x, g, n):
    r = lax.rsqrt(jnp.sum(x * x, axis=-1, keepdims=True) * (1.0 / n) + EPS)
    return x * r * g, r


def _rms_bwd(x, r, g, dy, n):
    gdy = dy * g
    dx = r * gdy - x * (r * r * r) * (jnp.sum(x * gdy, axis=-1, keepdims=True) * (1.0 / n))
    dg = jnp.sum(dy * (x * r), axis=0, keepdims=True)
    return dx, dg


def _sigmoid(x):
    return 1.0 / (1.0 + jnp.exp(-x))


def rope_tables(ang):
    T = ang.shape[0]
    tm = min(ROW_TILE, T)

    def body(ang_ref, c_ref, sa_ref, sb_ref):
        a = ang_ref[...]
        lane = lax.broadcasted_iota(jnp.int32, a.shape, 1)
        s = jnp.sin(a)
        c_ref[...] = jnp.cos(a)
        sa_ref[...] = jnp.where(lane < QK_NOPE + QK_ROPE // 2, -s, 0.0)
        sb_ref[...] = jnp.where(lane >= QK_NOPE + QK_ROPE // 2, s, 0.0)

    row = pl.BlockSpec((tm, HEAD_PAD), lambda i: (i, 0))
    return pl.pallas_call(
        body, grid=(T // tm,), name="rope_tables", in_specs=[row], out_specs=[row] * 3,
        out_shape=[jax.ShapeDtypeStruct((T, HEAD_PAD), F32)] * 3, compiler_params=_cparams(),
    )(ang)


def _rope(x, c, sa, sb):
    half = QK_ROPE // 2
    return x * c + pltpu.roll(x, HEAD_PAD - half, 1) * sa + pltpu.roll(x, half, 1) * sb


def _rope_bwd(dy, c, sa, sb):
    half = QK_ROPE // 2
    return dy * c + pltpu.roll(dy * sa, half, 1) + pltpu.roll(dy * sb, HEAD_PAD - half, 1)


def _head_masks(shape, width):
    lane = lax.broadcasted_iota(jnp.int32, shape, len(shape) - 1)
    return [(lane >= width * h) & (lane < width * (h + 1)) for h in range(HEADS)]


def _qkv_pre(z, gql, wq, gkvl, wkv):
    ql = z[:, 0:Q_LORA]
    kvl = z[:, Q_LORA:Q_LORA + KV_LORA]
    kr = z[:, Q_LORA + KV_LORA:Q_LORA + KV_LORA + HEAD_PAD]
    qn, rq = _rms(ql, gql, Q_LORA)
    kvn, rkv = _rms(kvl, gkvl, KV_LORA)
    qn = qn.astype(BF16)
    kvn = kvn.astype(BF16)
    q_up = _dot(qn, wq)
    kv_up = _dot(kvn, wkv)
    return ql, kvl, kr, qn, rq, kvn, rkv, q_up, kv_up


def mixin_fwd(x, rope, gmix, win, gql, wq, gkvl, wkv, gqh, gkh):
    T = x.shape[0]
    tm = min(2 * ROW_TILE, T)
    part = min(MIXIN_PART, tm)

    def body(x_ref, c_ref, sa_ref, sb_ref, gmix_ref, win_ref, gql_ref, wq_ref, gkvl_ref, wkv_ref, gqh_ref, gkh_ref,
             z_ref, q_ref, k_ref, v_ref, kt_ref, vt_ref):
        for p in range(tm // part):
            rows = pl.ds(part * p, part)
            hn = _rms(x_ref[rows, :], gmix_ref[...], D_MODEL)[0].astype(BF16)
            z = _dot(hn, win_ref[...])
            z_ref[rows, :] = z
            _, _, kr, _, _, _, _, q_up, kv_up = _qkv_pre(z, gql_ref[...], wq_ref[...], gkvl_ref[...], wkv_ref[...])
            c, sa, sb = c_ref[rows, :], sa_ref[rows, :], sb_ref[rows, :]
            for h in range(HEADS):
                lo = HEAD_PAD * h
                qh = _rms(q_up[:, lo:lo + HEAD_PAD], gqh_ref[...], QK_HEAD)[0]
                q_ref[h, rows, :] = (_rope(qh, c, sa, sb) * (ATTN_SCALE * LOG2E)).astype(BF16)
                kh = _rope(_rms(kv_up[:, lo:lo + HEAD_PAD] + kr, gkh_ref[...], QK_HEAD)[0], c, sa, sb)
                k_ref[h, rows, :] = kh.astype(BF16)
                kt_ref[h, :, rows] = jnp.transpose(kh).astype(BF16)
                vh = kv_up[:, HEADS * HEAD_PAD + lo:HEADS * HEAD_PAD + lo + HEAD_PAD]
                v_ref[h, rows, :] = vh.astype(BF16)
                vt_ref[h, :, rows] = jnp.transpose(vh).astype(BF16)

    row = lambda w: pl.BlockSpec((tm, w), lambda i: (i, 0))
    head = pl.BlockSpec((HEADS, tm, HEAD_PAD), lambda i: (0, i, 0))
    head_t = pl.BlockSpec((HEADS, HEAD_PAD, tm), lambda i: (0, 0, i))
    return pl.pallas_call(
        body, grid=(T // tm,), name="mixin_fwd",
        in_specs=[row(D_MODEL)] + [row(HEAD_PAD)] * 3 + [WHOLE] * 8,
        out_specs=[row(Z_WIDTH), head, head, head, head_t, head_t],
        out_shape=[jax.ShapeDtypeStruct((T, Z_WIDTH), F32)] + [jax.ShapeDtypeStruct((HEADS, T, HEAD_PAD), BF16)] * 3
                  + [jax.ShapeDtypeStruct((HEADS, HEAD_PAD, T), BF16)] * 2,
        compiler_params=_cparams(),
    )(x, *rope, gmix, win, gql, wq, gkvl, wkv, gqh, gkh)


def attn_fwd(q, k, vt):
    _, T, _ = q.shape
    tb = min(ATTN_TILE, T)
    sb = min(ATTN_FWD_SUB, tb)
    ns = tb // sb
    nb = T // tb

    def body(q_ref, k_ref, vt_ref, o_ref, lse_ref, m_s, l_s, acc_s):
        i, j = pl.program_id(0), pl.program_id(1)

        @pl.when(j == 0)
        def _():
            m_s[...] = jnp.full(m_s.shape, -jnp.inf, F32)
            l_s[...] = jnp.zeros(l_s.shape, F32)
            acc_s[...] = jnp.zeros(acc_s.shape, F32)

        def sub_block(h, a, b, masked):
            qa = slice(sb * a, sb * (a + 1))
            kb = slice(sb * b, sb * (b + 1))
            st = _dot_nt(k_ref[h, kb, :], q_ref[h, qa, :])
            if masked:
                krow = lax.broadcasted_iota(jnp.int32, st.shape, 0)
                qcol = lax.broadcasted_iota(jnp.int32, st.shape, 1)
                st = jnp.where(krow <= qcol, st, NEG_BIG)
            m_prev = m_s[h, :, qa]
            m_new = jnp.maximum(m_prev, jnp.max(st, axis=0, keepdims=True))
            alpha = jnp.exp2(m_prev - m_new)
            pt = jnp.exp2(st - m_new)
            l_s[h, :, qa] = alpha * l_s[h, :, qa] + jnp.sum(pt, axis=0, keepdims=True)
            acc_s[h, :, qa] = alpha * acc_s[h, :, qa] + _dot(vt_ref[h, :, kb], pt.astype(BF16))
            m_s[h, :, qa] = m_new

        @pl.when(j < i)
        def _():
            for b in range(ns):
                for h in range(HEADS):
                    for a in range(ns):
                        sub_block(h, a, b, False)

        @pl.when(j == i)
        def _():
            for b in range(ns):
                for h in range(HEADS):
                    for a in range(b, ns):
                        sub_block(h, a, b, a == b)
            for h in range(HEADS):
                l = l_s[h]
                o_ref[:, HEAD_PAD * h:HEAD_PAD * (h + 1)] = jnp.transpose(acc_s[h] / l)
                lse_ref[h] = m_s[h] + jnp.log2(l)

    qspec = pl.BlockSpec((HEADS, tb, HEAD_PAD), lambda i, j: (0, i, 0))
    kspec = pl.BlockSpec((HEADS, tb, HEAD_PAD), lambda i, j: (0, jnp.minimum(i, j), 0))
    vspec = pl.BlockSpec((HEADS, HEAD_PAD, tb), lambda i, j: (0, 0, jnp.minimum(i, j)))
    return pl.pallas_call(
        body, grid=(nb, nb), name="attn_fwd",
        in_specs=[qspec, kspec, vspec],
        out_specs=[pl.BlockSpec((tb, MLA_WIDTH), lambda i, j: (i, 0)), pl.BlockSpec((HEADS, 1, tb), lambda i, j: (0, 0, i))],
        out_shape=[jax.ShapeDtypeStruct((T, MLA_WIDTH), F32), jax.ShapeDtypeStruct((HEADS, 1, T), F32)],
        scratch_shapes=[pltpu.VMEM((HEADS, 1, tb), F32), pltpu.VMEM((HEADS, 1, tb), F32), pltpu.VMEM((HEADS, HEAD_PAD, tb), F32)],
        compiler_params=_cparams(),
    )(q, k, vt)


def _sgu_fwd_chunk(vn_c, wcat, bz, masks):
    vstack = jnp.concatenate([jnp.where(mk, vn_c, 0.0).astype(BF16) for mk in masks], axis=0)
    return _dot(wcat, vstack) + bz


def _pool_counts(i, tm):
    pos1 = (i * tm + 1 + lax.broadcasted_iota(jnp.int32, (tm, POOL_WIDTH), 0)).astype(F32)
    lane = lax.broadcasted_iota(jnp.int32, (tm, POOL_WIDTH), 1)
    win = jnp.where(lane < 64, 2.0, jnp.where(lane < 128, 4.0, jnp.where(lane < 192, 8.0, 16.0)))
    return jnp.minimum(pos1, win), lane


def _by_group(lane, s2, s4, s8, s16):
    return jnp.where(lane < 64, s2, jnp.where(lane < 128, s4, jnp.where(lane < 192, s8, s16)))


def _pool_means(pin, halo, i, tm):
    s1 = jnp.concatenate([halo, pin], axis=0)
    s2 = s1 + pltpu.roll(s1, 1, 0)
    s4 = s2 + pltpu.roll(s2, 2, 0)
    s8 = s4 + pltpu.roll(s4, 4, 0)
    s16 = s8 + pltpu.roll(s8, 8, 0)
    cnt, lane = _pool_counts(i, tm)
    sel = _by_group(lane, s2[POOL_HALO:], s4[POOL_HALO:], s8[POOL_HALO:], s16[POOL_HALO:])
    return sel / cnt - pin


def _mixers_fwd_tile(i, tm, a, u, vs, pin, halo, gsgu, wcat, bz, wp, pscale, goa, gos, gop):
    vn, rv = _rms(vs, gsgu, SGU_WIDTH)
    masks = _head_masks((CHUNK, SGU_WIDTH), SGU_HEAD_DIM)
    zc = jnp.concatenate([_sgu_fwd_chunk(vn[CHUNK * c:CHUNK * (c + 1)], wcat, bz, masks) for c in range(tm // CHUNK)], axis=0)
    gm = u * zc
    halo = jnp.where(i > 0, halo, 0.0)
    m = _pool_means(pin, halo, i, tm).astype(BF16)
    yp_pre = _dot(m, wp)
    yp = yp_pre * pscale
    na, ra = _rms(a, goa, MLA_WIDTH)
    ng, rg = _rms(gm, gos, SGU_WIDTH)
    npo, rp = _rms(yp, gop, POOL_WIDTH)
    mix = jnp.concatenate([na, ng, npo], axis=1).astype(BF16)
    return vn, rv, zc, gm, m, yp_pre, yp, ra, rg, rp, mix


def _z_specs(tm):
    col = lambda c: pl.BlockSpec((tm, 256), lambda i: (i, c))
    halo = pl.BlockSpec((POOL_HALO, 256), lambda i: (jnp.maximum(i * (tm // POOL_HALO) - 1, 0), 4))
    return [col(2), col(3), col(4), halo]


def mixers_fwd(x, a, z, gsgu, wcat, bz, wp, pscale, goa, gos, gop, wout):
    T = x.shape[0]
    tm = min(ROW_TILE, T)

    def body(x_ref, a_ref, u_ref, vs_ref, pin_ref, halo_ref, gsgu_ref, wcat_ref, bz_ref, wp_ref, ps_ref,
             goa_ref, gos_ref, gop_ref, wout_ref, x1_ref):
        i = pl.program_id(0)
        mix = _mixers_fwd_tile(i, tm, a_ref[...], u_ref[...], vs_ref[...], pin_ref[...], halo_ref[...], gsgu_ref[...],
                               wcat_ref[...], bz_ref[...], wp_ref[...], ps_ref[...], goa_ref[...], gos_ref[...],
                               gop_ref[...])[-1]
        x1_ref[...] = x_ref[...] + _dot(mix, wout_ref[...])

    row = lambda w: pl.BlockSpec((tm, w), lambda i: (i, 0))
    return pl.pallas_call(
        body, grid=(T // tm,), name="mixers_fwd",
        in_specs=[row(D_MODEL), row(MLA_WIDTH)] + _z_specs(tm) + [WHOLE] * 9,
        out_specs=row(D_MODEL),
        out_shape=jax.ShapeDtypeStruct((T, D_MODEL), F32),
        compiler_params=_cparams(),
    )(x, a, z, z, z, z, gsgu, wcat, bz, wp, pscale, goa, gos, gop, wout)


def ffn_fwd(x1, gffn, wg, wu, wd):
    T = x1.shape[0]
    tm = min(ROW_TILE, T)

    def body(x1_ref, gffn_ref, wg_ref, wu_ref, wd_ref, x2_ref, gs_ref, us_ref):
        x1v = x1_ref[...]
        h2 = _rms(x1v, gffn_ref[...], D_MODEL)[0].astype(BF16)
        acc = x1v
        for c in range(FFN_HIDDEN // FFN_CHUNK):
            sl = slice(FFN_CHUNK * c, FFN_CHUNK * (c + 1))
            g = _dot(h2, wg_ref[:, sl])
            u = _dot(h2, wu_ref[:, sl])
            gs_ref[:, sl] = g.astype(BF16)
            us_ref[:, sl] = u.astype(BF16)
            act = (g * _sigmoid(g) * u).astype(BF16)
            acc = acc + _dot(act, wd_ref[sl, :])
        x2_ref[...] = acc

    row = lambda w: pl.BlockSpec((tm, w), lambda i: (i, 0))
    return pl.pallas_call(
        body, grid=(T // tm,), name="ffn_fwd",
        in_specs=[row(D_MODEL)] + [WHOLE] * 4,
        out_specs=[row(D_MODEL), row(FFN_HIDDEN), row(FFN_HIDDEN)],
        out_shape=[jax.ShapeDtypeStruct((T, D_MODEL), F32), jax.ShapeDtypeStruct((T, FFN_HIDDEN), BF16),
                   jax.ShapeDtypeStruct((T, FFN_HIDDEN), BF16)],
        compiler_params=_cparams(),
    )(x1, gffn, wg, wu, wd)


def loss_head(y, target):
    T = y.shape[0]
    tm = min(ROW_TILE, T)

    def body(y_ref, t_ref, dy_ref, loss_ref):
        @pl.when(pl.program_id(0) == 0)
        def _():
            loss_ref[...] = jnp.zeros(loss_ref.shape, F32)

        err = y_ref[...] - t_ref[...]
        dy_ref[...] = err * (1.0 / D_MODEL)
        per_row = jnp.sum(err * err, axis=1, keepdims=True) * (1.0 / D_MODEL)
        loss_ref[...] += 0.5 * jnp.sum(per_row, axis=0, keepdims=True)

    row = pl.BlockSpec((tm, D_MODEL), lambda i: (i, 0))
    return pl.pallas_call(
        body, grid=(T // tm,), name="loss_head",
        in_specs=[row, row],
        out_specs=[row, pl.BlockSpec((1, 1), lambda i: (0, 0))],
        out_shape=[jax.ShapeDtypeStruct((T, D_MODEL), F32), jax.ShapeDtypeStruct((1, 1), F32)],
        compiler_params=_cparams(),
    )(y, target)


def _acc_spec(shape):
    return pl.BlockSpec(shape, lambda i: (0,) * len(shape))


def ffn_bwd(dx2, x1, gs, us, gffn, wg, wu, wd):
    T = x1.shape[0]
    tm = min(ROW_TILE // 2, T)

    def body(dx2_ref, x1_ref, gs_ref, us_ref, gffn_ref, wg_ref, wu_ref, wd_ref,
             dx1_ref, h2_ref, act_ref, dg_ref, du_ref, dgffn_ref):
        @pl.when(pl.program_id(0) == 0)
        def _():
            dgffn_ref[...] = jnp.zeros(dgffn_ref.shape, F32)

        dx2v = dx2_ref[...]
        dy = dx2v.astype(BF16)
        x1v = x1_ref[...]
        h2, r = _rms(x1v, gffn_ref[...], D_MODEL)
        h2_ref[...] = h2.astype(BF16)
        for c in range(FFN_HIDDEN // FFN_CHUNK):
            sl = slice(FFN_CHUNK * c, FFN_CHUNK * (c + 1))
            g = gs_ref[:, sl].astype(F32)
            u = us_ref[:, sl].astype(F32)
            dact = _dot_nt(dy, wd_ref[sl, :])
            sg = _sigmoid(g)
            silu = g * sg
            act_ref[:, sl] = (silu * u).astype(BF16)
            dg_ref[:, sl] = (dact * u * (sg * (1.0 + g * (1.0 - sg)))).astype(BF16)
            du_ref[:, sl] = (dact * silu).astype(BF16)
        dh2 = _dot_nt(dg_ref[...], wg_ref[...]) + _dot_nt(du_ref[...], wu_ref[...])
        dxn, dgn = _rms_bwd(x1v, r, gffn_ref[...], dh2, D_MODEL)
        dx1_ref[...] = dx2v + dxn
        dgffn_ref[...] += dgn

    row = lambda w: pl.BlockSpec((tm, w), lambda i: (i, 0))
    return pl.pallas_call(
        body, grid=(T // tm,), name="ffn_bwd",
        in_specs=[row(D_MODEL), row(D_MODEL), row(FFN_HIDDEN), row(FFN_HIDDEN)] + [WHOLE] * 4,
        out_specs=[row(D_MODEL), row(D_MODEL), row(FFN_HIDDEN), row(FFN_HIDDEN), row(FFN_HIDDEN), _acc_spec((1, D_MODEL))],
        out_shape=[jax.ShapeDtypeStruct((T, D_MODEL), F32), jax.ShapeDtypeStruct((T, D_MODEL), BF16),
                   jax.ShapeDtypeStruct((T, FFN_HIDDEN), BF16), jax.ShapeDtypeStruct((T, FFN_HIDDEN), BF16),
                   jax.ShapeDtypeStruct((T, FFN_HIDDEN), BF16), jax.ShapeDtypeStruct((1, D_MODEL), F32)],
        compiler_params=_cparams(),
    )(dx2, x1, gs, us, gffn, wg, wu, wd)


TN_K_TILE = 1024
TN_ACC_BYTES = 6 * 1024 * 1024


def matmul_tn(a, b, name):
    T, M = a.shape
    N = b.shape[1]
    tk = min(TN_K_TILE, T)
    tm = M if M <= 1024 else M // 2
    tn = max(d for d in range(128, N + 1, 128) if N % d == 0 and tm * d * 4 <= TN_ACC_BYTES)
    nk = T // tk

    def body(a_ref, b_ref, o_ref, acc):
        k = pl.program_id(2)

        @pl.when(k == 0)
        def _():
            acc[...] = jnp.zeros(acc.shape, F32)

        acc[...] += _dot_tn(a_ref[...].astype(BF16), b_ref[...].astype(BF16))

        @pl.when(k == nk - 1)
        def _():
            o_ref[...] = acc[...].astype(BF16)

    return pl.pallas_call(
        body, grid=(M // tm, N // tn, nk), name=name,
        in_specs=[pl.BlockSpec((tk, tm), lambda i, j, k: (k, i)), pl.BlockSpec((tk, tn), lambda i, j, k: (k, j))],
        out_specs=pl.BlockSpec((tm, tn), lambda i, j, k: (i, j)),
        out_shape=jax.ShapeDtypeStruct((M, N), BF16),
        scratch_shapes=[pltpu.VMEM((tm, tn), F32)],
        compiler_params=_cparams(),
    )(a, b)


def mixers_bwd(dx1, a, z, gsgu, wcat, wcat_t, bz, wp, pscale, goa, gos, gop, wout):
    T = a.shape[0]
    tm = min(ROW_TILE, T)

    def body(dx1_ref, a_ref, u_ref, vs_ref, pin_ref, halo_ref, gsgu_ref, wcat_ref, wcatt_ref, bz_ref, wp_ref,
             ps_ref, goa_ref, gos_ref, gop_ref, wout_ref,
             da_ref, delta_ref, du_ref, dvs_ref, dm_ref, mix_ref,
             dgoa_ref, dgos_ref, dgop_ref, dps_ref, dwp_ref, dwsp_ref, db_ref, dgsgu_ref):
        i = pl.program_id(0)

        @pl.when(i == 0)
        def _():
            for r in (dgoa_ref, dgos_ref, dgop_ref, dps_ref, dwp_ref, dwsp_ref, db_ref, dgsgu_ref):
                r[...] = jnp.zeros(r.shape, F32)

        a_v, u, vs = a_ref[...], u_ref[...], vs_ref[...]
        goa, gos, gop, pscale_v = goa_ref[...], gos_ref[...], gop_ref[...], ps_ref[...]
        vn, rv, zc, gm, m, yp_pre, yp, ra, rg, rp, mix = _mixers_fwd_tile(
            i, tm, a_v, u, vs, pin_ref[...], halo_ref[...], gsgu_ref[...], wcat_ref[...], bz_ref[...], wp_ref[...],
            pscale_v, goa, gos, gop)
        mix_ref[...] = mix
        dmix = _dot_nt(dx1_ref[...].astype(BF16), wout_ref[...])
        da, dgoa = _rms_bwd(a_v, ra, goa, dmix[:, :MLA_WIDTH], MLA_WIDTH)
        dgm, dgos = _rms_bwd(gm, rg, gos, dmix[:, MLA_WIDTH:MLA_WIDTH + SGU_WIDTH], SGU_WIDTH)
        dyp, dgop = _rms_bwd(yp, rp, gop, dmix[:, MLA_WIDTH + SGU_WIDTH:], POOL_WIDTH)
        da_ref[...] = da
        dgoa_ref[...] += dgoa
        dgos_ref[...] += dgos
        dgop_ref[...] += dgop
        prod = da * a_v
        ones = jnp.ones((8, HEAD_PAD), F32)
        for h in range(HEADS):
            lo = HEAD_PAD * h
            sums = lax.dot_general(ones, prod[:, lo:lo + HEAD_PAD], (((1,), (1,)), ((), ())), preferred_element_type=F32,
                                   precision=lax.Precision.HIGHEST)
            delta_ref[h] = sums[0:1, :]
        dps_ref[...] += jnp.sum(dyp * yp_pre, axis=0, keepdims=True)
        dyp_pre = (dyp * pscale_v).astype(BF16)
        dwp_ref[...] += _dot_tn(m, dyp_pre)
        dm_ref[...] = _dot_nt(dyp_pre, wp_ref[...])
        du_ref[...] = dgm * zc
        dzc = dgm * u
        masks = _head_masks((CHUNK, SGU_WIDTH), SGU_HEAD_DIM)
        lane_b = lax.broadcasted_iota(jnp.int32, (CHUNK, HEAD_PAD), 1)
        dvn_parts = []
        dwsp = jnp.zeros(dwsp_ref.shape, F32)
        db = jnp.zeros(db_ref.shape, F32)
        for c in range(tm // CHUNK):
            dz_c = dzc[CHUNK * c:CHUNK * (c + 1)]
            dzstack = jnp.concatenate([jnp.where(mk, dz_c, 0.0).astype(BF16) for mk in masks], axis=0)
            dvn_parts.append(_dot(wcatt_ref[...], dzstack))
            dwsp = dwsp + _dot_nt(dzstack, vn[CHUNK * c:CHUNK * (c + 1)].astype(BF16))
            for h, mk in enumerate(masks):
                col = jnp.sum(jnp.where(mk, dz_c, 0.0), axis=1, keepdims=True)
                db = db + jnp.where(lane_b == h, col, 0.0)
        dwsp_ref[...] += dwsp
        db_ref[...] += db
        dvs, dgsgu = _rms_bwd(vs, rv, gsgu_ref[...], jnp.concatenate(dvn_parts, axis=0), SGU_WIDTH)
        dvs_ref[...] = dvs
        dgsgu_ref[...] += dgsgu

    row = lambda w: pl.BlockSpec((tm, w), lambda i: (i, 0))
    head = pl.BlockSpec((HEADS, 1, tm), lambda i: (0, 0, i))
    acc_shapes = [(1, MLA_WIDTH), (1, SGU_WIDTH), (1, POOL_WIDTH), (1, POOL_WIDTH), (POOL_WIDTH, POOL_WIDTH),
                  (HEADS * CHUNK, CHUNK), (CHUNK, HEAD_PAD), (1, SGU_WIDTH)]
    return pl.pallas_call(
        body, grid=(T // tm,), name="mixers_bwd",
        in_specs=[row(D_MODEL), row(MLA_WIDTH)] + _z_specs(tm) + [WHOLE] * 10,
        out_specs=[row(MLA_WIDTH), head, row(256), row(256), row(256), row(D_MODEL)] + [_acc_spec(s) for s in acc_shapes],
        out_shape=[jax.ShapeDtypeStruct((T, MLA_WIDTH), F32), jax.ShapeDtypeStruct((HEADS, 1, T), F32),
                   jax.ShapeDtypeStruct((T, 256), F32), jax.ShapeDtypeStruct((T, 256), F32),
                   jax.ShapeDtypeStruct((T, 256), F32), jax.ShapeDtypeStruct((T, D_MODEL), BF16)]
                  + [jax.ShapeDtypeStruct(s, F32) for s in acc_shapes],
        compiler_params=_cparams(),
    )(dx1, a, z, z, z, z, gsgu, wcat, wcat_t, bz, wp, pscale, goa, gos, gop, wout)


def pool_bwd(dm):
    T = dm.shape[0]
    tm = min(ROW_TILE, T)
    nt = T // tm

    def body(dm_ref, next_ref, dpin_ref):
        i = pl.program_id(0)
        cnt, lane = _pool_counts(i, tm)
        dmv = dm_ref[...]
        win = _by_group(lane[:POOL_HALO], 2.0, 4.0, 8.0, 16.0)
        nxt = jnp.where(i < nt - 1, next_ref[...] / win, 0.0)
        r1 = jnp.concatenate([dmv / cnt, nxt], axis=0)
        n = tm + POOL_HALO
        r2 = r1 + pltpu.roll(r1, n - 1, 0)
        r4 = r2 + pltpu.roll(r2, n - 2, 0)
        r8 = r4 + pltpu.roll(r4, n - 4, 0)
        r16 = r8 + pltpu.roll(r8, n - 8, 0)
        dpin_ref[...] = _by_group(lane, r2[:tm], r4[:tm], r8[:tm], r16[:tm]) - dmv

    return pl.pallas_call(
        body, grid=(nt,), name="pool_bwd",
        in_specs=[pl.BlockSpec((tm, 256), lambda i: (i, 0)),
                  pl.BlockSpec((POOL_HALO, 256), lambda i: (jnp.minimum((i + 1) * (tm // POOL_HALO), T // POOL_HALO - 1), 0))],
        out_specs=pl.BlockSpec((tm, 256), lambda i: (i, 0)),
        out_shape=jax.ShapeDtypeStruct((T, 256), F32),
        compiler_params=_cparams(),
    )(dm, dm)


def attn_bwd(q, k, kt, v, do, lse, delta):
    _, T, _ = q.shape
    tb = min(ATTN_TILE, T)
    sb = min(ATTN_SUB, tb)
    ns = tb // sb
    nb = T // tb
    hb = ATTN_BWD_HEADS

    def body(q_ref, k_ref, kt_ref, v_ref, do_ref, lse_ref, delta_ref, dqt_ref, dk_ref, dv_ref, dk_s, dv_s):
        i, j = pl.program_id(1), pl.program_id(2)

        @pl.when((i == 0) & (j == 0))
        def _():
            dqt_ref[...] = jnp.zeros(dqt_ref.shape, F32)

        @pl.when(j == i)
        def _():
            dk_s[...] = jnp.zeros(dk_s.shape, F32)
            dv_s[...] = jnp.zeros(dv_s.shape, F32)

        def sub_block(h, a, b, masked):
            qa = slice(sb * a, sb * (a + 1))
            kb = slice(sb * b, sb * (b + 1))
            qv = q_ref[h, qa, :]
            dov = do_ref[qa, HEAD_PAD * h:HEAD_PAD * (h + 1)].astype(BF16)
            st = _dot_nt(k_ref[h, kb, :], qv)
            pt = jnp.exp2(st - lse_ref[h, :, qa])
            if masked:
                krow = lax.broadcasted_iota(jnp.int32, st.shape, 0)
                qcol = lax.broadcasted_iota(jnp.int32, st.shape, 1)
                pt = jnp.where(krow <= qcol, pt, 0.0)
            dv_s[h, kb, :] += _dot(pt.astype(BF16), dov)
            dpt = _dot_nt(v_ref[h, kb, :], dov)
            dst = (pt * (dpt - delta_ref[h, :, qa])).astype(BF16)
            dk_s[h, kb, :] += _dot(dst, qv)
            cols = pl.ds(pl.multiple_of(j * tb + sb * a, sb), sb)
            dqt_ref[h, :, cols] += _dot(kt_ref[h, :, kb], dst)

        @pl.when(j > i)
        def _():
            for a in range(ns):
                for h in range(hb):
                    for b in range(ns):
                        sub_block(h, a, b, False)

        @pl.when(j == i)
        def _():
            for a in range(ns):
                for h in range(hb):
                    for b in range(a + 1):
                        sub_block(h, a, b, a == b)

        @pl.when(j == nb - 1)
        def _():
            dk_ref[...] = dk_s[...] * (1.0 / LOG2E)
            dv_ref[...] = dv_s[...]

    qspec = pl.BlockSpec((hb, tb, HEAD_PAD), lambda g, i, j: (g, jnp.maximum(i, j), 0))
    kspec = pl.BlockSpec((hb, tb, HEAD_PAD), lambda g, i, j: (g, i, 0))
    ktspec = pl.BlockSpec((hb, HEAD_PAD, tb), lambda g, i, j: (g, 0, i))
    rowspec = pl.BlockSpec((hb, 1, tb), lambda g, i, j: (g, 0, jnp.maximum(i, j)))
    return pl.pallas_call(
        body, grid=(HEADS // hb, nb, nb), name="attn_bwd",
        in_specs=[qspec, kspec, ktspec, kspec, pl.BlockSpec((tb, hb * HEAD_PAD), lambda g, i, j: (jnp.maximum(i, j), g)), rowspec, rowspec],
        out_specs=[pl.BlockSpec((hb, HEAD_PAD, T), lambda g, i, j: (g, 0, 0)), kspec, kspec],
        out_shape=[jax.ShapeDtypeStruct((HEADS, HEAD_PAD, T), F32)] + [jax.ShapeDtypeStruct((HEADS, T, HEAD_PAD), F32)] * 2,
        scratch_shapes=[pltpu.VMEM((hb, tb, HEAD_PAD), F32), pltpu.VMEM((hb, tb, HEAD_PAD), F32)],
        compiler_params=_cparams(),
    )(q, k, kt, v, do, lse, delta)


def mixin_bwd(dres, x, z, rope, dqt, dk, dv, du, dvs, dpin, gmix, win, gql, wq, gkvl, wkv, gqh, gkh):
    T = x.shape[0]
    tm = min(ROW_TILE, T)
    part = min(MIXIN_BWD_PART, tm)

    def body(dres_ref, x_ref, z_ref, c_ref, sa_ref, sb_ref, dqt_ref, dk_ref, dv_ref, du_ref, dvs_ref, dpin_ref,
             gmix_ref, win_ref, gql_ref, wq_ref, gkvl_ref, wkv_ref, gqh_ref, gkh_ref,
             dx_ref, hn_ref, dz_ref, qn_ref, dqup_ref, kvn_ref, dkvup_ref,
             dgmix_ref, dgql_ref, dgkvl_ref, dgqh_ref, dgkh_ref):
        @pl.when(pl.program_id(0) == 0)
        def _():
            for r in (dgmix_ref, dgql_ref, dgkvl_ref, dgqh_ref, dgkh_ref):
                r[...] = jnp.zeros(r.shape, F32)

        lane = lax.broadcasted_iota(jnp.int32, (part, HEAD_PAD), 1)
        rope_lanes = (lane >= QK_NOPE) & (lane < QK_HEAD)
        for p in range(tm // part):
            rows = pl.ds(part * p, part)
            xv = x_ref[rows, :]
            hn, rx = _rms(xv, gmix_ref[...], D_MODEL)
            hn_ref[rows, :] = hn.astype(BF16)
            ql, kvl, kr, qn, rq, kvn, rkv, q_up, kv_up = _qkv_pre(z_ref[rows, :], gql_ref[...], wq_ref[...], gkvl_ref[...],
                                                                  wkv_ref[...])
            qn_ref[rows, :] = qn
            kvn_ref[rows, :] = kvn
            c, sa, sb = c_ref[rows, :], sa_ref[rows, :], sb_ref[rows, :]
            dkr = jnp.zeros((part, HEAD_PAD), F32)
            dgqh = jnp.zeros((1, HEAD_PAD), F32)
            dgkh = jnp.zeros((1, HEAD_PAD), F32)
            dq_parts, dk_parts, dv_parts = [], [], []
            for h in range(HEADS):
                lo = HEAD_PAD * h
                qh = q_up[:, lo:lo + HEAD_PAD]
                rqh = lax.rsqrt(jnp.sum(qh * qh, axis=-1, keepdims=True) * (1.0 / QK_HEAD) + EPS)
                dq_h = jnp.transpose(dqt_ref[h, :, rows]) * ATTN_SCALE
                dqh, dg = _rms_bwd(qh, rqh, gqh_ref[...], _rope_bwd(dq_h, c, sa, sb), QK_HEAD)
                dgqh = dgqh + dg
                dq_parts.append(dqh)
                kh = kv_up[:, lo:lo + HEAD_PAD] + kr
                rkh = lax.rsqrt(jnp.sum(kh * kh, axis=-1, keepdims=True) * (1.0 / QK_HEAD) + EPS)
                dkh, dg = _rms_bwd(kh, rkh, gkh_ref[...], _rope_bwd(dk_ref[h, rows, :], c, sa, sb), QK_HEAD)
                dgkh = dgkh + dg
                dkr = dkr + jnp.where(rope_lanes, dkh, 0.0)
                dk_parts.append(dkh)
                dv_parts.append(dv_ref[h, rows, :])
            dgqh_ref[...] += dgqh
            dgkh_ref[...] += dgkh
            dq_up = jnp.concatenate(dq_parts, axis=1).astype(BF16)
            dkv_up = jnp.concatenate(dk_parts + dv_parts, axis=1).astype(BF16)
            dqup_ref[rows, :] = dq_up
            dkvup_ref[rows, :] = dkv_up
            dql, dg = _rms_bwd(ql, rq, gql_ref[...], _dot_nt(dq_up, wq_ref[...]), Q_LORA)
            dgql_ref[...] += dg
            dkvl, dg = _rms_bwd(kvl, rkv, gkvl_ref[...], _dot_nt(dkv_up, wkv_ref[...]), KV_LORA)
            dgkvl_ref[...] += dg
            dz = jnp.concatenate([dql, dkvl, dkr, du_ref[rows, :], dvs_ref[rows, :], dpin_ref[rows, :]], axis=1).astype(BF16)
            dz_ref[rows, :] = dz
            dxn, dg = _rms_bwd(xv, rx, gmix_ref[...], _dot_nt(dz, win_ref[...]), D_MODEL)
            dgmix_ref[...] += dg
            dx_ref[rows, :] = dres_ref[rows, :] + dxn

    row = lambda w: pl.BlockSpec((tm, w), lambda i: (i, 0))
    head = pl.BlockSpec((HEADS, tm, HEAD_PAD), lambda i: (0, i, 0))
    head_t = pl.BlockSpec((HEADS, HEAD_PAD, tm), lambda i: (0, 0, i))
    acc_shapes = [(1, D_MODEL), (1, Q_LORA), (1, KV_LORA), (1, HEAD_PAD), (1, HEAD_PAD)]
    out_rows = [(D_MODEL, F32), (D_MODEL, BF16), (Z_WIDTH, BF16), (Q_LORA, BF16), (HEADS * HEAD_PAD, BF16),
                (KV_LORA, BF16), (2 * HEADS * HEAD_PAD, BF16)]
    return pl.pallas_call(
        body, grid=(T // tm,), name="mixin_bwd",
        in_specs=[row(D_MODEL), row(D_MODEL), row(Z_WIDTH)] + [row(HEAD_PAD)] * 3 + [head_t, head, head, row(256), row(256), row(256)]
                 + [WHOLE] * 8,
        out_specs=[row(w) for w, _ in out_rows] + [_acc_spec(s) for s in acc_shapes],
        out_shape=[jax.ShapeDtypeStruct((T, w), dt) for w, dt in out_rows] + [jax.ShapeDtypeStruct(s, F32) for s in acc_shapes],
        compiler_params=_cparams(),
    )(dres, x, z, *rope, dqt, dk, dv, du, dvs, dpin, gmix, win, gql, wq, gkvl, wkv, gqh, gkh)


def _place():
    x, y, c = lax.axis_index("x"), lax.axis_index("y"), lax.axis_index("c")
    return x, y, c, 4 * x + 2 * y + c


def cast_shards(shards, wanted):
    n = len(shards)

    def body(*refs):
        for o_ref, (w, l) in zip(refs[n:], wanted):
            o_ref[...] = refs[w][l].astype(BF16)

    return pl.pallas_call(
        body, name="cast_shards", in_specs=[WHOLE] * n, out_specs=[WHOLE] * len(wanted),
        out_shape=[jax.ShapeDtypeStruct(shards[w].shape[1:], BF16) for w, _ in wanted],
        compiler_params=_cparams(),
    )(*shards)


def allgather_layer0(shards):
    n = len(shards)

    def body(*refs):
        ins, outs = refs[:n], refs[n:2 * n]
        stage = refs[2 * n:3 * n]
        send_sems, recv_sems, local_sems = refs[3 * n:]
        x, y, c, me = _place()
        sibling = (x, y, 1 - c)
        chips = [(1 - x, y), (x, 1 - y), (1 - x, 1 - y)]

        def copy(k, w, block_id, to, from_stage):
            return pltpu.make_async_remote_copy(
                src_ref=stage[w] if from_stage else outs[w].at[block_id], dst_ref=outs[w].at[block_id],
                send_sem=send_sems.at[k, w], recv_sem=recv_sems.at[k, w], device_id=to, device_id_type=MESH)

        def block_of(cx, cy, cc):
            return 4 * cx + 2 * cy + cc

        local, sent = [], []
        for w in range(n):
            stage[w][...] = ins[w][0].astype(BF16)
            mine = pltpu.make_async_copy(stage[w], outs[w].at[me], local_sems.at[w])
            mine.start()
            local.append(mine)
            first = [copy(0, w, me, sibling, True)] + [copy(1 + j, w, me, (*chip, c), True) for j, chip in enumerate(chips)]
            for cp in first:
                cp.start()
            sent += first
        for j, chip in enumerate(chips):
            for w in range(n):
                copy(1 + j, w, block_of(*chip, c), (x, y, c), False).wait_recv()
                fwd = copy(4 + j, w, block_of(*chip, c), sibling, False)
                fwd.start()
                sent.append(fwd)
        for w in range(n):
            copy(0, w, block_of(x, y, 1 - c), (x, y, c), False).wait_recv()
            for j, chip in enumerate(chips):
                copy(4 + j, w, block_of(*chip, 1 - c), (x, y, c), False).wait_recv()
        for cp in local:
            cp.wait()
        for cp in sent:
            cp.wait_send()

    return pl.pallas_call(
        body, name="allgather_layer0",
        in_specs=[WHOLE] * n, out_specs=[ANY] * n,
        out_shape=[jax.ShapeDtypeStruct((N_DEV,) + s.shape[1:], BF16) for s in shards],
        scratch_shapes=[pltpu.VMEM(s.shape[1:], BF16) for s in shards]
                       + [pltpu.SemaphoreType.DMA((7, n)), pltpu.SemaphoreType.DMA((7, n)), pltpu.SemaphoreType.DMA((n,))],
        compiler_params=_cparams(),
    )(*shards)


def _peer(k):
    x, y, c, _ = _place()
    px = 1 - x if k & 4 else x
    py = 1 - y if k & 2 else y
    pc = 1 - c if k & 1 else c
    return (px, py, pc), 4 * px + 2 * py + pc


def exchange_start(srcs, after, gather, name):
    n = len(srcs)
    land_shapes = [((N_DEV,) + s.shape) if gather else s.shape for s in srcs]

    def body(*refs):
        src_refs, land_refs = refs[:n], refs[n:2 * n]
        send_sems, recv_sems = refs[2 * n + 1:3 * n + 1], refs[3 * n + 1:4 * n + 1]
        token = refs[-1]
        _, _, _, me = _place()
        for k in range(1, N_DEV):
            peer, peer_id = _peer(k)
            for w in range(n):
                pltpu.make_async_remote_copy(
                    src_ref=src_refs[w] if gather else src_refs[w].at[peer_id], dst_ref=land_refs[w].at[me],
                    send_sem=send_sems[w], recv_sem=recv_sems[w], device_id=peer, device_id_type=MESH).start()
        token[...] = jnp.zeros(token.shape, F32)

    hbm = lambda a: pltpu.with_memory_space_constraint(a, pltpu.HBM)
    outs = pl.pallas_call(
        body, name=name,
        out_shape=(pltpu.SemaphoreType.DMA(()),) * (2 * n)
                  + tuple(pltpu.HBM(s.shape, BF16) for s in srcs) + tuple(pltpu.HBM(s, BF16) for s in land_shapes)
                  + (jax.ShapeDtypeStruct((8, 128), F32),),
        in_specs=[HBM_SPEC] * (2 * n) + [ANY],
        out_specs=(SEM_SPEC,) * (2 * n) + (HBM_SPEC,) * (2 * n) + (WHOLE,),
        input_output_aliases={i: 2 * n + i for i in range(2 * n)},
        compiler_params=pltpu.CompilerParams(has_side_effects=pltpu.SideEffectType.DATAFLOW_SIDE_EFFECTING),
    )(*[hbm(s) for s in srcs], *[hbm(lax.empty(s, BF16)) for s in land_shapes], after)
    return list(outs[:n]), list(outs[n:2 * n]), list(outs[2 * n:3 * n]), list(outs[3 * n:4 * n]), outs[-1]


def exchange_wait(started, after, name):
    send_sems, recv_sems, srcs, lands, _ = started
    n = len(srcs)

    def body(*refs):
        land_refs = refs[n:2 * n]
        send_sems, recv_sems = refs[2 * n:3 * n], refs[3 * n:4 * n]
        x, y, c, _ = _place()
        for w in range(n):
            seven = land_refs[w].at[pl.ds(0, N_DEV - 1)]
            cp = pltpu.make_async_remote_copy(src_ref=seven, dst_ref=seven, send_sem=send_sems[w], recv_sem=recv_sems[w],
                                              device_id=(x, y, c), device_id_type=MESH)
            cp.wait_send()
            cp.wait_recv()

    outs = pl.pallas_call(
        body, name=name,
        out_shape=tuple(pltpu.HBM(s.shape, BF16) for s in srcs) + tuple(pltpu.HBM(l.shape, BF16) for l in lands),
        in_specs=[HBM_SPEC] * (2 * n) + [SEM_SPEC] * (2 * n) + [ANY],
        out_specs=(HBM_SPEC,) * (2 * n),
        input_output_aliases={i: i for i in range(2 * n)},
        compiler_params=pltpu.CompilerParams(has_side_effects=pltpu.SideEffectType.DATAFLOW_SIDE_EFFECTING),
    )(*srcs, *lands, *send_sems, *recv_sems, after)
    _, _, _, me = _place()
    filled = []
    for src, land in zip(outs[:n], outs[n:]):
        own = src[None] if src.ndim == 2 else lax.dynamic_slice_in_dim(src, me, 1, axis=0)
        filled.append(lax.dynamic_update_slice_in_dim(land, own, me, axis=0))
    return filled


def exchange_grads(pieces):
    n = len(pieces)

    def body(*refs):
        ins, outs = refs[:n], refs[n:2 * n]
        send_sems, recv_sems, local_sems = refs[2 * n:]
        x, y, c, me = _place()
        local = [pltpu.make_async_copy(ins[w].at[me], outs[w].at[me], local_sems.at[w]) for w in range(n)]
        for cp in local:
            cp.start()
        remote = []
        for k in range(1, N_DEV):
            px = 1 - x if k & 4 else x
            py = 1 - y if k & 2 else y
            pc = 1 - c if k & 1 else c
            peer = 4 * px + 2 * py + pc
            for w in range(n):
                cp = pltpu.make_async_remote_copy(
                    src_ref=ins[w].at[peer], dst_ref=outs[w].at[me], send_sem=send_sems.at[k - 1, w],
                    recv_sem=recv_sems.at[k - 1, w], device_id=(px, py, pc), device_id_type=MESH)
                cp.start()
                remote.append(cp)
        for cp in local:
            cp.wait()
        for cp in remote:
            cp.wait_recv()
        for cp in remote:
            cp.wait_send()

    return pl.pallas_call(
        body, name="exchange_grads",
        in_specs=[ANY] * n, out_specs=[ANY] * n,
        out_shape=[jax.ShapeDtypeStruct(p.shape, BF16) for p in pieces],
        scratch_shapes=[pltpu.SemaphoreType.DMA((7, n)), pltpu.SemaphoreType.DMA((7, n)), pltpu.SemaphoreType.DMA((n,))],
        compiler_params=_cparams(),
    )(*pieces)


def _adamw(w, g, m, v):
    m2 = ADAM_B1 * m + (1.0 - ADAM_B1) * g
    v2 = ADAM_B2 * v + (1.0 - ADAM_B2) * (g * g)
    m_hat = m2 / (1.0 - ADAM_B1 ** ADAM_STEP)
    v_hat = v2 / (1.0 - ADAM_B2 ** ADAM_STEP)
    delta = -ADAM_LR * (m_hat / (jnp.sqrt(v_hat) + ADAM_EPS) + ADAM_WD * w)
    return delta, m2, v2


def adamw_sharded(parts0, parts1, w, m, v, name):
    _, R, C = parts0.shape
    br = max(d for d in range(16, min(R, 512) + 1, 16) if R % d == 0)
    nblk = R // br

    def body(p0_ref, p1_ref, w_ref, m_ref, v_ref, g_ref, d_ref, m2_ref, v2_ref):
        def total(p_ref):
            g = p_ref[0].astype(F32)
            for s in range(1, N_DEV):
                g = g + p_ref[s].astype(F32)
            return g

        g = jnp.where(pl.program_id(0) == 0, total(p0_ref), total(p1_ref))
        g_ref[...] = g
        d_ref[...], m2_ref[...], v2_ref[...] = _adamw(w_ref[...], g, m_ref[...], v_ref[...])

    blk = pl.BlockSpec((None, br, C), lambda l, i: (l, i, 0))
    p0 = pl.BlockSpec((N_DEV, br, C), lambda l, i: (0, jnp.where(l == 0, i, nblk - 1), 0))
    p1 = pl.BlockSpec((N_DEV, br, C), lambda l, i: (0, jnp.where(l == 1, i, 0), 0))
    return pl.pallas_call(
        body, grid=(DEPTH, nblk), name=name,
        in_specs=[p0, p1, blk, blk, blk],
        out_specs=[blk] * 4,
        out_shape=[jax.ShapeDtypeStruct((DEPTH, R, C), F32)] * 4,
        compiler_params=_cparams(),
    )(parts0, parts1, w, m, v)


def allreduce_adamw_small(gpart, w, m, v):
    R = gpart.shape[0]

    def body(g_ref, w_ref, m_ref, v_ref, grad_ref, d_ref, m2_ref, v2_ref, all_ref, send_sems, recv_sems):
        x, y, c, me = _place()
        sibling = (x, y, 1 - c)
        chips = [(1 - x, y), (x, 1 - y), (1 - x, 1 - y)]

        def copy(k, block_id, to, from_input):
            return pltpu.make_async_remote_copy(
                src_ref=g_ref if from_input else all_ref.at[block_id], dst_ref=all_ref.at[block_id],
                send_sem=send_sems.at[k], recv_sem=recv_sems.at[k], device_id=to, device_id_type=MESH)

        def block_of(cx, cy, cc):
            return 4 * cx + 2 * cy + cc

        all_ref[me] = g_ref[...]
        first = [copy(0, me, sibling, True)] + [copy(1 + j, me, (*chip, c), True) for j, chip in enumerate(chips)]
        for cp in first:
            cp.start()
        passed = [copy(4 + j, block_of(*chip, c), sibling, False) for j, chip in enumerate(chips)]
        for j, chip in enumerate(chips):
            copy(1 + j, block_of(*chip, c), (x, y, c), False).wait_recv()
            passed[j].start()
        copy(0, block_of(x, y, 1 - c), (x, y, c), False).wait_recv()
        for j, chip in enumerate(chips):
            copy(4 + j, block_of(*chip, 1 - c), (x, y, c), False).wait_recv()
        for cp in first + passed:
            cp.wait_send()
        g = all_ref[0]
        for s in range(1, N_DEV):
            g = g + all_ref[s]
        grad_ref[...] = g
        d_ref[...], m2_ref[...], v2_ref[...] = _adamw(w_ref[...], g, m_ref[...], v_ref[...])

    return pl.pallas_call(
        body, name="allreduce_adamw_small",
        in_specs=[WHOLE] * 4, out_specs=[WHOLE] * 4,
        out_shape=[jax.ShapeDtypeStruct((R, 128), F32)] * 4,
        scratch_shapes=[pltpu.VMEM((N_DEV, R, 128), F32), pltpu.SemaphoreType.DMA((7,)), pltpu.SemaphoreType.DMA((7,))],
        compiler_params=_cparams(),
    )(gpart, w, m, v)


def _cols_from_pieces(p):
    return p.transpose(1, 0, 2).reshape(p.shape[1], -1)


def _pieces_from_cols(w):
    return w.reshape(w.shape[0], N_DEV, -1).transpose(1, 0, 2)


def _pad_heads(w, width):
    r = w.shape[0]
    return jnp.pad(w.reshape(r, HEADS, width), ((0, 0), (0, 0), (0, HEAD_PAD - width))).reshape(r, HEADS * HEAD_PAD)


def _unpad_heads(w, width):
    r = w.shape[0]
    return w.reshape(r, HEADS, HEAD_PAD)[:, :, :width].reshape(r, HEADS * width)


O1 = Q_LORA
O2 = O1 + KV_LORA
O3 = O2 + QK_ROPE


def _mixer_weights(gw):
    w_in = _cols_from_pieces(gw["w_in"])
    zero = lambda n: jnp.zeros((D_MODEL, n), BF16)
    win = jnp.concatenate([w_in[:, :O2], zero(QK_NOPE), w_in[:, O2:O3], zero(HEAD_PAD - QK_HEAD), w_in[:, O3:]], axis=1)
    wq = _pad_heads(_cols_from_pieces(gw["w_q_up"]), QK_HEAD)
    w_kv = _cols_from_pieces(gw["w_kv_up"]).reshape(KV_LORA, HEADS, QK_NOPE + V_HEAD)
    wk = jnp.pad(w_kv[:, :, :QK_NOPE], ((0, 0), (0, 0), (0, HEAD_PAD - QK_NOPE))).reshape(KV_LORA, HEADS * HEAD_PAD)
    wv = w_kv[:, :, QK_NOPE:].reshape(KV_LORA, HEADS * V_HEAD)
    wkv = jnp.concatenate([wk, wv], axis=1)
    wout = gw["w_out"].reshape(D_MODEL, D_MODEL)
    return dict(win=win, wq=wq, wkv=wkv, wout=wout)


def _ffn_weights(gw):
    return dict(wg=_cols_from_pieces(gw["w_gate"]), wu=_cols_from_pieces(gw["w_up"]),
                wd=gw["w_down"].reshape(FFN_HIDDEN, D_MODEL))


def _layer_small(p, l):
    row = lambda a: a.reshape(1, -1)
    pad_head = lambda g: jnp.pad(g, (0, HEAD_PAD - QK_HEAD)).reshape(1, HEAD_PAD)
    tril = jnp.tril(jnp.ones((CHUNK, CHUNK), F32))
    wsp = p["w_spatial"][l] * tril
    wcat = jnp.concatenate([wsp[h] for h in range(HEADS)], axis=1).astype(BF16)
    wcat_t = jnp.concatenate([wsp[h].T for h in range(HEADS)], axis=1).astype(BF16)
    bz = jnp.repeat(p["b_spatial"][l].T, SGU_HEAD_DIM, axis=1)
    wp = jax.scipy.linalg.block_diag(*[p["w_pool"][l][g] for g in range(HEADS)]).astype(BF16)
    return dict(gmix=row(p["g_mix_norm"][l]), gql=row(p["g_q_lat"][l]), gkvl=row(p["g_kv_lat"][l]),
                gqh=pad_head(p["g_q_head"][l]), gkh=pad_head(p["g_k_head"][l]), gsgu=row(p["g_sgu_v"][l]),
                wcat=wcat, wcat_t=wcat_t, bz=bz, wp=wp, pscale=row(p["pool_scale"][l]),
                goa=row(p["g_out_mla"][l]), gos=row(p["g_out_sgu"][l]), gop=row(p["g_out_pool"][l]),
                gffn=row(p["g_ffn_norm"][l]))


MIXER_SIDE = ("w_in", "w_q_up", "w_kv_up", "w_out")
FFN_SIDE = ("w_gate", "w_up", "w_down")
SHARDED = MIXER_SIDE + FFN_SIDE
SMALL = ("g_mix_norm", "g_q_lat", "g_kv_lat", "g_q_head", "g_k_head", "g_sgu_v", "w_spatial", "b_spatial", "w_pool",
         "pool_scale", "g_out_mla", "g_out_sgu", "g_out_pool", "g_ffn_norm")
WEIGHTS = ("g_mix_norm", "w_in", "g_q_lat", "w_q_up", "g_kv_lat", "w_kv_up", "g_q_head", "g_k_head", "g_sgu_v", "w_spatial",
           "b_spatial", "w_pool", "pool_scale", "g_out_mla", "g_out_sgu", "g_out_pool", "w_out", "g_ffn_norm", "w_gate",
           "w_up", "w_down")
PACK_ROWS = 8 * 128


def _pack_small(parts):
    flat = []
    for name in SMALL:
        a = parts[name].reshape(-1)
        flat.append(jnp.pad(a, (0, -a.shape[0] % PACK_ROWS)))
    return jnp.concatenate(flat).reshape(-1, 128)


def _unpack_small(packed, like):
    out, row = {}, 0
    for name in SMALL:
        n = math.prod(like[name].shape)
        rows = -(-n // PACK_ROWS) * 8
        out[name] = packed[row:row + rows].reshape(-1)[:n].reshape(like[name].shape)
        row += rows
    return out


def _forward_mixers(x, rope, W, S):
    z, q, k, v, kt, vt = mixin_fwd(x, rope, S["gmix"], W["win"], S["gql"], W["wq"], S["gkvl"], W["wkv"], S["gqh"], S["gkh"])
    a, lse = attn_fwd(q, k, vt)
    x1 = mixers_fwd(x, a, z, S["gsgu"], S["wcat"], S["bz"], S["wp"], S["pscale"], S["goa"], S["gos"], S["gop"], W["wout"])
    return x1, dict(x=x, z=z, q=q, k=k, kt=kt, v=v, a=a, lse=lse, x1=x1)


def _backward_ffn(dx2, W, S, A, l):
    dx1, h2, act, dg, du_ffn, dgffn = ffn_bwd(dx2, A["x1"], A["gs"], A["us"], S["gffn"], W["wg"], W["wu"], W["wd"])
    d_wd = matmul_tn(act, dx2, f"dw_down_{l}")
    d_wg = matmul_tn(h2, dg, f"dw_gate_{l}")
    d_wu = matmul_tn(h2, du_ffn, f"dw_up_{l}")
    big = dict(w_gate=_pieces_from_cols(d_wg), w_up=_pieces_from_cols(d_wu), w_down=d_wd.reshape(N_DEV, -1, D_MODEL))
    return dx1, big, dgffn


def _backward_mixers(dx1, dgffn, rope, W, S, A, l):
    (da, delta, du, dvs, dm, mix, dgoa, dgos, dgop, dps, dwp, dwsp, db, dgsgu) = mixers_bwd(
        dx1, A["a"], A["z"], S["gsgu"], S["wcat"], S["wcat_t"], S["bz"], S["wp"], S["pscale"], S["goa"], S["gos"],
        S["gop"], W["wout"])
    d_wout = matmul_tn(mix, dx1, f"dw_out_{l}")
    dpin = pool_bwd(dm)
    dqt, dk, dv = attn_bwd(A["q"], A["k"], A["kt"], A["v"], da, A["lse"], delta)
    (dx, hn, dz, qn, dq_up, kvn, dkv_up, dgmix, dgql, dgkvl, dgqh, dgkh) = mixin_bwd(
        dx1, A["x"], A["z"], rope, dqt, dk, dv, du, dvs, dpin, S["gmix"], W["win"], S["gql"], W["wq"], S["gkvl"],
        W["wkv"], S["gqh"], S["gkh"])
    d_win = matmul_tn(hn, dz, f"dw_in_{l}")
    d_wq = matmul_tn(qn, dq_up, f"dw_q_up_{l}")
    d_wkv = matmul_tn(kvn, dkv_up, f"dw_kv_up_{l}")
    d_win = jnp.concatenate([d_win[:, :O2], d_win[:, O2 + QK_NOPE:O2 + QK_HEAD], d_win[:, O2 + HEAD_PAD:]], axis=1)
    d_wk = d_wkv[:, :HEADS * HEAD_PAD].reshape(KV_LORA, HEADS, HEAD_PAD)[:, :, :QK_NOPE]
    d_wv = d_wkv[:, HEADS * HEAD_PAD:].reshape(KV_LORA, HEADS, V_HEAD)
    d_wkv = jnp.concatenate([d_wk, d_wv], axis=2).reshape(KV_LORA, HEADS * (QK_NOPE + V_HEAD))
    big = dict(w_in=_pieces_from_cols(d_win), w_q_up=_pieces_from_cols(_unpad_heads(d_wq, QK_HEAD)),
               w_kv_up=_pieces_from_cols(d_wkv), w_out=d_wout.reshape(N_DEV, -1, D_MODEL))
    tril = jnp.tril(jnp.ones((CHUNK, CHUNK), F32))
    small = dict(g_mix_norm=dgmix[0], g_q_lat=dgql[0], g_kv_lat=dgkvl[0], g_q_head=dgqh[0, :QK_HEAD], g_k_head=dgkh[0, :QK_HEAD],
                 g_sgu_v=dgsgu[0], w_spatial=dwsp.reshape(HEADS, CHUNK, CHUNK) * tril, b_spatial=db[:, :HEADS].T,
                 w_pool=jnp.stack([dwp[64 * g:64 * (g + 1), 64 * g:64 * (g + 1)] for g in range(HEADS)]),
                 pool_scale=dps[0], g_out_mla=dgoa[0], g_out_sgu=dgos[0], g_out_pool=dgop[0], g_ffn_norm=dgffn[0])
    return dx, big, small


def kernel(x, positions, g_mix_norm, w_in, g_q_lat, w_q_up, g_kv_lat, w_kv_up, g_q_head, g_k_head, g_sgu_v, w_spatial, b_spatial, w_pool, pool_scale, g_out_mla, g_out_sgu, g_out_pool, w_out, g_ffn_norm, w_gate, w_up, w_down, loss_target, m_g_mix_norm, m_w_in, m_g_q_lat, m_w_q_up, m_g_kv_lat, m_w_kv_up, m_g_q_head, m_g_k_head, m_g_sgu_v, m_w_spatial, m_b_spatial, m_w_pool, m_pool_scale, m_g_out_mla, m_g_out_sgu, m_g_out_pool, m_w_out, m_g_ffn_norm, m_w_gate, m_w_up, m_w_down, v_g_mix_norm, v_w_in, v_g_q_lat, v_w_q_up, v_g_kv_lat, v_w_kv_up, v_g_q_head, v_g_k_head, v_g_sgu_v, v_w_spatial, v_b_spatial, v_w_pool, v_pool_scale, v_g_out_mla, v_g_out_sgu, v_g_out_pool, v_w_out, v_g_ffn_norm, v_w_gate, v_w_up, v_w_down):
    given = dict(locals())
    w = {n: given[n] for n in WEIGHTS}
    m = {n: given["m_" + n] for n in WEIGHTS}
    v = {n: given["v_" + n] for n in WEIGHTS}
    T = x.shape[1]
    xs = x.reshape(T, D_MODEL)

    half = QK_ROPE // 2
    inv_freq = 1.0 / (ROPE_THETA ** (jnp.arange(half, dtype=F32) / half))
    ang16 = positions.reshape(T).astype(F32)[:, None] * inv_freq
    ang = jnp.concatenate([jnp.zeros((T, QK_NOPE), F32), ang16, ang16, jnp.zeros((T, HEAD_PAD - QK_HEAD), F32)], axis=1)

    wanted = [(SHARDED.index(n), 0) for n in FFN_SIDE] + [(i, 1) for i in range(len(SHARDED))]
    bf = cast_shards([w[n] for n in SHARDED], wanted)
    mixer0 = allgather_layer0([w[n] for n in MIXER_SIDE])
    ag_ffn0 = exchange_start(bf[:len(FFN_SIDE)], mixer0[0], True, "ag_start_ffn0")
    ag_l1 = exchange_start(bf[len(FFN_SIDE):], ag_ffn0[-1], True, "ag_start_l1")
    rope = rope_tables(ang + (ag_ffn0[-1][0, 0] + ag_l1[-1][0, 0]))
    Ss = [_layer_small(w, l) for l in range(DEPTH)]

    x1, A0 = _forward_mixers(xs, rope, _mixer_weights(dict(zip(MIXER_SIDE, mixer0))), Ss[0])
    F0 = _ffn_weights(dict(zip(FFN_SIDE, exchange_wait(ag_ffn0, x1, "ag_wait_ffn0"))))
    h, A0["gs"], A0["us"] = ffn_fwd(x1, Ss[0]["gffn"], F0["wg"], F0["wu"], F0["wd"])
    layer1 = dict(zip(SHARDED, exchange_wait(ag_l1, h, "ag_wait_l1")))
    W1 = {**_mixer_weights(layer1), **_ffn_weights(layer1)}
    x1, A1 = _forward_mixers(h, rope, W1, Ss[1])
    h, A1["gs"], A1["us"] = ffn_fwd(x1, Ss[1]["gffn"], W1["wg"], W1["wu"], W1["wd"])
    dh, loss_part = loss_head(h, loss_target.reshape(T, D_MODEL))
    loss = lax.psum(loss_part[0, 0], ("x", "y", "c"))

    dx1, big_f1, dgffn1 = _backward_ffn(dh, W1, Ss[1], A1, 1)
    dh, big_m1, small1 = _backward_mixers(dx1, dgffn1, rope, W1, Ss[1], A1, 1)
    rs_l1 = exchange_start([{**big_m1, **big_f1}[n] for n in SHARDED], dh, False, "rs_start_l1")
    W0 = {**_mixer_weights(dict(zip(MIXER_SIDE, mixer0))), **F0}
    S0 = dict(Ss[0], gffn=Ss[0]["gffn"] + rs_l1[-1][0, 0])
    dx1, big_f0, dgffn0 = _backward_ffn(dh, W0, S0, A0, 0)
    rs_ffn0 = exchange_start([big_f0[n] for n in FFN_SIDE], dx1, False, "rs_start_ffn0")
    S0 = dict(Ss[0], gsgu=Ss[0]["gsgu"] + rs_ffn0[-1][0, 0])
    dh, big_m0, small0 = _backward_mixers(dx1, dgffn0, rope, W0, S0, A0, 0)
    grad_x = dh.reshape(x.shape)
    smalls = [small0, small1]

    parts1 = dict(zip(SHARDED, exchange_wait(rs_l1, dh, "rs_wait_l1")))
    parts0 = dict(zip(FFN_SIDE, exchange_wait(rs_ffn0, dh, "rs_wait_ffn0")))
    parts0.update(zip(MIXER_SIDE, exchange_grads([big_m0[n] for n in MIXER_SIDE])))
    grad, delta, new_m, new_v = {}, {}, {}, {}
    for n in SHARDED:
        grad[n], delta[n], new_m[n], new_v[n] = adamw_sharded(parts0[n], parts1[n], w[n], m[n], v[n], f"adamw_{n}")

    small_part = {n: jnp.stack([smalls[l][n] for l in range(DEPTH)]) for n in SMALL}
    outs = allreduce_adamw_small(_pack_small(small_part), _pack_small(w), _pack_small(m), _pack_small(v))
    for d, o in zip((grad, delta, new_m, new_v), outs):
        d.update(_unpack_small(o, w))

    return (loss, grad_x, *[grad[n] for n in WEIGHTS], *[delta[n] for n in WEIGHTS], *[new_m[n] for n in WEIGHTS],
            *[new_v[n] for n in WEIGHTS])
```

```python
import functools
import math

import jax
import jax.numpy as jnp
from jax import lax
from jax.experimental import pallas as pl
from jax.experimental.pallas import tpu as pltpu

F32 = jnp.float32
BF16 = jnp.bfloat16

N_DEV = 8
DEPTH = 2
D_MODEL = 1024
HEADS = 4
HEAD_PAD = 128
QK_NOPE = 64
QK_ROPE = 32
QK_HEAD = QK_NOPE + QK_ROPE
V_HEAD = 128
Q_LORA = 256
KV_LORA = 128
SGU_WIDTH = 256
SGU_HEAD_DIM = 64
CHUNK = 128
POOL_WIDTH = 256
POOL_WINDOWS = (2, 4, 8, 16)
POOL_HALO = 16
MLA_WIDTH = 512
IN_WIDTH = 1184
Z_WIDTH = 1280
FFN_HIDDEN = 2816
FFN_CHUNK = 256
ROPE_THETA = 10000.0
EPS = 1e-6
ATTN_SCALE = 1.0 / math.sqrt(QK_HEAD)
LOG2E = 1.4426950408889634
NEG_BIG = -1e30

ADAM_LR = 0.001
ADAM_B1 = 0.9
ADAM_B2 = 0.999
ADAM_EPS = 1e-08
ADAM_WD = 0.01
ADAM_STEP = 10

VMEM_LIMIT = 56 * 1024 * 1024
ROW_TILE = 512
MIXIN_PART = 256
MIXIN_BWD_PART = 512
ATTN_TILE = 1024
ATTN_SUB = 512
ATTN_FWD_SUB = 1024
ATTN_BWD_HEADS = 2
MESH = pl.DeviceIdType.MESH

WHOLE = pl.BlockSpec(memory_space=pltpu.VMEM)
ANY = pl.BlockSpec(memory_space=pl.ANY)
HBM_SPEC = pl.BlockSpec(memory_space=pltpu.HBM)
SEM_SPEC = pl.BlockSpec(memory_space=pltpu.SEMAPHORE)


def _cparams(**kw):
    return pltpu.CompilerParams(vmem_limit_bytes=VMEM_LIMIT, **kw)


def _dot(a, b):
    return jnp.dot(a, b, preferred_element_type=F32)


def _dot_nt(a, b):
    return lax.dot_general(a, b, (((1,), (1,)), ((), ())), preferred_element_type=F32)


def _dot_tn(a, b):
    return lax.dot_general(a, b, (((0,), (0,)), ((), ())), preferred_element_type=F32)


def _rms(x, g, n):
    r = lax.rsqrt(jnp.sum(x * x, axis=-1, keepdims=True) * (1.0 / n) + EPS)
    return x * r * g, r


def _rms_bwd(x, r, g, dy, n):
    gdy = dy * g
    dx = r * gdy - x * (r * r * r) * (jnp.sum(x * gdy, axis=-1, keepdims=True) * (1.0 / n))
    dg = jnp.sum(dy * (x * r), axis=0, keepdims=True)
    return dx, dg


def _sigmoid(x):
    return 1.0 / (1.0 + jnp.exp(-x))


def rope_tables(ang):
    T = ang.shape[0]
    tm = min(ROW_TILE, T)

    def body(ang_ref, c_ref, sa_ref, sb_ref):
        a = ang_ref[...]
        lane = lax.broadcasted_iota(jnp.int32, a.shape, 1)
        s = jnp.sin(a)
        c_ref[...] = jnp.cos(a)
        sa_ref[...] = jnp.where(lane < QK_NOPE + QK_ROPE // 2, -s, 0.0)
        sb_ref[...] = jnp.where(lane >= QK_NOPE + QK_ROPE // 2, s, 0.0)

    row = pl.BlockSpec((tm, HEAD_PAD), lambda i: (i, 0))
    return pl.pallas_call(
        body, grid=(T // tm,), name="rope_tables", in_specs=[row], out_specs=[row] * 3,
        out_shape=[jax.ShapeDtypeStruct((T, HEAD_PAD), F32)] * 3, compiler_params=_cparams(),
    )(ang)


def _rope(x, c, sa, sb):
    half = QK_ROPE // 2
    return x * c + pltpu.roll(x, HEAD_PAD - half, 1) * sa + pltpu.roll(x, half, 1) * sb


def _rope_bwd(dy, c, sa, sb):
    half = QK_ROPE // 2
    return dy * c + pltpu.roll(dy * sa, half, 1) + pltpu.roll(dy * sb, HEAD_PAD - half, 1)


def _head_masks(shape, width):
    lane = lax.broadcasted_iota(jnp.int32, shape, len(shape) - 1)
    return [(lane >= width * h) & (lane < width * (h + 1)) for h in range(HEADS)]


def _qkv_pre(z, gql, wq, gkvl, wkv):
    ql = z[:, 0:Q_LORA]
    kvl = z[:, Q_LORA:Q_LORA + KV_LORA]
    kr = z[:, Q_LORA + KV_LORA:Q_LORA + KV_LORA + HEAD_PAD]
    qn, rq = _rms(ql, gql, Q_LORA)
    kvn, rkv = _rms(kvl, gkvl, KV_LORA)
    qn = qn.astype(BF16)
    kvn = kvn.astype(BF16)
    q_up = _dot_nt(qn, wq)
    kv_up = _dot_nt(kvn, wkv)
    return ql, kvl, kr, qn, rq, kvn, rkv, q_up, kv_up


def mixin_fwd(x, rope, gmix, win, gql, wq, gkvl, wkv, gqh, gkh):
    T = x.shape[0]
    tm = min(2 * ROW_TILE, T)
    part = min(MIXIN_PART, tm)

    def body(x_ref, c_ref, sa_ref, sb_ref, gmix_ref, win_ref, gql_ref, wq_ref, gkvl_ref, wkv_ref, gqh_ref, gkh_ref,
             z_ref, q_ref, k_ref, v_ref, kt_ref, vt_ref):
        for p in range(tm // part):
            rows = pl.ds(part * p, part)
            hn = _rms(x_ref[rows, :], gmix_ref[...], D_MODEL)[0].astype(BF16)
            z = _dot_nt(hn, win_ref[...])
            z_ref[rows, :] = z
            _, _, kr, _, _, _, _, q_up, kv_up = _qkv_pre(z, gql_ref[...], wq_ref[...], gkvl_ref[...], wkv_ref[...])
            c, sa, sb = c_ref[rows, :], sa_ref[rows, :], sb_ref[rows, :]
            for h in range(HEADS):
                lo = HEAD_PAD * h
                qh = _rms(q_up[:, lo:lo + HEAD_PAD], gqh_ref[...], QK_HEAD)[0]
                q_ref[h, rows, :] = (_rope(qh, c, sa, sb) * (ATTN_SCALE * LOG2E)).astype(BF16)
                kh = _rope(_rms(kv_up[:, lo:lo + HEAD_PAD] + kr, gkh_ref[...], QK_HEAD)[0], c, sa, sb)
                k_ref[h, rows, :] = kh.astype(BF16)
                kt_ref[h, :, rows] = jnp.transpose(kh).astype(BF16)
                vh = kv_up[:, HEADS * HEAD_PAD + lo:HEADS * HEAD_PAD + lo + HEAD_PAD]
                v_ref[h, rows, :] = vh.astype(BF16)
                vt_ref[h, :, rows] = jnp.transpose(vh).astype(BF16)

    row = lambda w: pl.BlockSpec((tm, w), lambda i: (i, 0))
    head = pl.BlockSpec((HEADS, tm, HEAD_PAD), lambda i: (0, i, 0))
    head_t = pl.BlockSpec((HEADS, HEAD_PAD, tm), lambda i: (0, 0, i))
    return pl.pallas_call(
        body, grid=(T // tm,), name="mixin_fwd",
        in_specs=[row(D_MODEL)] + [row(HEAD_PAD)] * 3 + [WHOLE] * 8,
        out_specs=[row(Z_WIDTH), head, head, head, head_t, head_t],
        out_shape=[jax.ShapeDtypeStruct((T, Z_WIDTH), F32)] + [jax.ShapeDtypeStruct((HEADS, T, HEAD_PAD), BF16)] * 3
                  + [jax.ShapeDtypeStruct((HEADS, HEAD_PAD, T), BF16)] * 2,
        compiler_params=_cparams(),
    )(x, *rope, gmix, win, gql, wq, gkvl, wkv, gqh, gkh)


def attn_fwd(q, k, vt):
    _, T, _ = q.shape
    tb = min(ATTN_TILE, T)
    sb = min(ATTN_FWD_SUB, tb)
    ns = tb // sb
    nb = T // tb

    def body(q_ref, k_ref, vt_ref, o_ref, lse_ref, m_s, l_s, acc_s):
        i, j = pl.program_id(0), pl.program_id(1)

        @pl.when(j == 0)
        def _():
            m_s[...] = jnp.full(m_s.shape, -jnp.inf, F32)
            l_s[...] = jnp.zeros(l_s.shape, F32)
            acc_s[...] = jnp.zeros(acc_s.shape, F32)

        def sub_block(h, a, b, masked):
            qa = slice(sb * a, sb * (a + 1))
            kb = slice(sb * b, sb * (b + 1))
            st = _dot_nt(k_ref[h, kb, :], q_ref[h, qa, :])
            if masked:
                krow = lax.broadcasted_iota(jnp.int32, st.shape, 0)
                qcol = lax.broadcasted_iota(jnp.int32, st.shape, 1)
                st = jnp.where(krow <= qcol, st, NEG_BIG)
            m_prev = m_s[h, :, qa]
            m_new = jnp.maximum(m_prev, jnp.max(st, axis=0, keepdims=True))
            alpha = jnp.exp2(m_prev - m_new)
            pt = jnp.exp2(st - m_new)
            l_s[h, :, qa] = alpha * l_s[h, :, qa] + jnp.sum(pt, axis=0, keepdims=True)
            acc_s[h, :, qa] = alpha * acc_s[h, :, qa] + _dot(vt_ref[h, :, kb], pt.astype(BF16))
            m_s[h, :, qa] = m_new

        @pl.when(j < i)
        def _():
            for b in range(ns):
                for h in range(HEADS):
                    for a in range(ns):
                        sub_block(h, a, b, False)

        @pl.when(j == i)
        def _():
            for b in range(ns):
                for h in range(HEADS):
                    for a in range(b, ns):
                        sub_block(h, a, b, a == b)
            for h in range(HEADS):
                l = l_s[h]
                o_ref[:, HEAD_PAD * h:HEAD_PAD * (h + 1)] = jnp.transpose(acc_s[h] / l)
                lse_ref[h] = m_s[h] + jnp.log2(l)

    qspec = pl.BlockSpec((HEADS, tb, HEAD_PAD), lambda i, j: (0, i, 0))
    kspec = pl.BlockSpec((HEADS, tb, HEAD_PAD), lambda i, j: (0, jnp.minimum(i, j), 0))
    vspec = pl.BlockSpec((HEADS, HEAD_PAD, tb), lambda i, j: (0, 0, jnp.minimum(i, j)))
    return pl.pallas_call(
        body, grid=(nb, nb), name="attn_fwd",
        in_specs=[qspec, kspec, vspec],
        out_specs=[pl.BlockSpec((tb, MLA_WIDTH), lambda i, j: (i, 0)), pl.BlockSpec((HEADS, 1, tb), lambda i, j: (0, 0, i))],
        out_shape=[jax.ShapeDtypeStruct((T, MLA_WIDTH), F32), jax.ShapeDtypeStruct((HEADS, 1, T), F32)],
        scratch_shapes=[pltpu.VMEM((HEADS, 1, tb), F32), pltpu.VMEM((HEADS, 1, tb), F32), pltpu.VMEM((HEADS, HEAD_PAD, tb), F32)],
        compiler_params=_cparams(),
    )(q, k, vt)


def _sgu_fwd_chunk(vn_c, wcat, bz, masks):
    vstack = jnp.concatenate([jnp.where(mk, vn_c, 0.0).astype(BF16) for mk in masks], axis=0)
    return _dot(wcat, vstack) + bz


def _pool_counts(i, tm):
    pos1 = (i * tm + 1 + lax.broadcasted_iota(jnp.int32, (tm, POOL_WIDTH), 0)).astype(F32)
    lane = lax.broadcasted_iota(jnp.int32, (tm, POOL_WIDTH), 1)
    win = jnp.where(lane < 64, 2.0, jnp.where(lane < 128, 4.0, jnp.where(lane < 192, 8.0, 16.0)))
    return jnp.minimum(pos1, win), lane


def _by_group(lane, s2, s4, s8, s16):
    return jnp.where(lane < 64, s2, jnp.where(lane < 128, s4, jnp.where(lane < 192, s8, s16)))


def _pool_means(pin, halo, i, tm):
    s1 = jnp.concatenate([halo, pin], axis=0)
    s2 = s1 + pltpu.roll(s1, 1, 0)
    s4 = s2 + pltpu.roll(s2, 2, 0)
    s8 = s4 + pltpu.roll(s4, 4, 0)
    s16 = s8 + pltpu.roll(s8, 8, 0)
    cnt, lane = _pool_counts(i, tm)
    sel = _by_group(lane, s2[POOL_HALO:], s4[POOL_HALO:], s8[POOL_HALO:], s16[POOL_HALO:])
    return sel / cnt - pin


def _mixers_fwd_tile(i, tm, a, u, vs, pin, halo, gsgu, wcat, bz, wp, pscale, goa, gos, gop):
    vn, rv = _rms(vs, gsgu, SGU_WIDTH)
    masks = _head_masks((CHUNK, SGU_WIDTH), SGU_HEAD_DIM)
    zc = jnp.concatenate([_sgu_fwd_chunk(vn[CHUNK * c:CHUNK * (c + 1)], wcat, bz, masks) for c in range(tm // CHUNK)], axis=0)
    gm = u * zc
    halo = jnp.where(i > 0, halo, 0.0)
    m = _pool_means(pin, halo, i, tm).astype(BF16)
    yp_pre = _dot(m, wp)
    yp = yp_pre * pscale
    na, ra = _rms(a, goa, MLA_WIDTH)
    ng, rg = _rms(gm, gos, SGU_WIDTH)
    npo, rp = _rms(yp, gop, POOL_WIDTH)
    mix = jnp.concatenate([na, ng, npo], axis=1).astype(BF16)
    return vn, rv, zc, gm, m, yp_pre, yp, ra, rg, rp, mix


def _z_specs(tm):
    col = lambda c: pl.BlockSpec((tm, 256), lambda i: (i, c))
    halo = pl.BlockSpec((POOL_HALO, 256), lambda i: (jnp.maximum(i * (tm // POOL_HALO) - 1, 0), 4))
    return [col(2), col(3), col(4), halo]


def mixers_fwd(x, a, z, gsgu, wcat, bz, wp, pscale, goa, gos, gop, wout):
    T = x.shape[0]
    tm = min(ROW_TILE, T)

    def body(x_ref, a_ref, u_ref, vs_ref, pin_ref, halo_ref, gsgu_ref, wcat_ref, bz_ref, wp_ref, ps_ref,
             goa_ref, gos_ref, gop_ref, wout_ref, x1_ref):
        i = pl.program_id(0)
        mix = _mixers_fwd_tile(i, tm, a_ref[...], u_ref[...], vs_ref[...], pin_ref[...], halo_ref[...], gsgu_ref[...],
                               wcat_ref[...], bz_ref[...], wp_ref[...], ps_ref[...], goa_ref[...], gos_ref[...],
                               gop_ref[...])[-1]
        x1_ref[...] = x_ref[...] + _dot(mix, wout_ref[...])

    row = lambda w: pl.BlockSpec((tm, w), lambda i: (i, 0))
    return pl.pallas_call(
        body, grid=(T // tm,), name="mixers_fwd",
        in_specs=[row(D_MODEL), row(MLA_WIDTH)] + _z_specs(tm) + [WHOLE] * 9,
        out_specs=row(D_MODEL),
        out_shape=jax.ShapeDtypeStruct((T, D_MODEL), F32),
        compiler_params=_cparams(),
    )(x, a, z, z, z, z, gsgu, wcat, bz, wp, pscale, goa, gos, gop, wout)


def ffn_fwd(x1, gffn, wg, wu, wd):
    T = x1.shape[0]
    tm = min(ROW_TILE, T)

    def body(x1_ref, gffn_ref, wg_ref, wu_ref, wd_ref, x2_ref, gs_ref, us_ref):
        x1v = x1_ref[...]
        h2 = _rms(x1v, gffn_ref[...], D_MODEL)[0].astype(BF16)
        acc = x1v
        for c in range(FFN_HIDDEN // FFN_CHUNK):
            sl = slice(FFN_CHUNK * c, FFN_CHUNK * (c + 1))
            g = _dot_nt(h2, wg_ref[sl, :])
            u = _dot_nt(h2, wu_ref[sl, :])
            gs_ref[:, sl] = g.astype(BF16)
            us_ref[:, sl] = u.astype(BF16)
            act = (g * _sigmoid(g) * u).astype(BF16)
            acc = acc + _dot(act, wd_ref[sl, :])
        x2_ref[...] = acc

    row = lambda w: pl.BlockSpec((tm, w), lambda i: (i, 0))
    return pl.pallas_call(
        body, grid=(T // tm,), name="ffn_fwd",
        in_specs=[row(D_MODEL)] + [WHOLE] * 4,
        out_specs=[row(D_MODEL), row(FFN_HIDDEN), row(FFN_HIDDEN)],
        out_shape=[jax.ShapeDtypeStruct((T, D_MODEL), F32), jax.ShapeDtypeStruct((T, FFN_HIDDEN), BF16),
                   jax.ShapeDtypeStruct((T, FFN_HIDDEN), BF16)],
        compiler_params=_cparams(),
    )(x1, gffn, wg, wu, wd)


def loss_head(y, target):
    T = y.shape[0]
    tm = min(ROW_TILE, T)

    def body(y_ref, t_ref, dy_ref, loss_ref):
        @pl.when(pl.program_id(0) == 0)
        def _():
            loss_ref[...] = jnp.zeros(loss_ref.shape, F32)

        err = y_ref[...] - t_ref[...]
        dy_ref[...] = err * (1.0 / D_MODEL)
        per_row = jnp.sum(err * err, axis=1, keepdims=True) * (1.0 / D_MODEL)
        loss_ref[...] += 0.5 * jnp.sum(per_row, axis=0, keepdims=True)

    row = pl.BlockSpec((tm, D_MODEL), lambda i: (i, 0))
    return pl.pallas_call(
        body, grid=(T // tm,), name="loss_head",
        in_specs=[row, row],
        out_specs=[row, pl.BlockSpec((1, 1), lambda i: (0, 0))],
        out_shape=[jax.ShapeDtypeStruct((T, D_MODEL), F32), jax.ShapeDtypeStruct((1, 1), F32)],
        compiler_params=_cparams(),
    )(y, target)


def _acc_spec(shape):
    return pl.BlockSpec(shape, lambda i: (0,) * len(shape))


def ffn_bwd(dx2, x1, gs, us, gffn, wg, wu, wd):
    T = x1.shape[0]
    tm = min(ROW_TILE // 2, T)

    def body(dx2_ref, x1_ref, gs_ref, us_ref, gffn_ref, wg_ref, wu_ref, wd_ref,
             dx1_ref, h2_ref, act_ref, dg_ref, du_ref, dgffn_ref):
        @pl.when(pl.program_id(0) == 0)
        def _():
            dgffn_ref[...] = jnp.zeros(dgffn_ref.shape, F32)

        dx2v = dx2_ref[...]
        dy = dx2v.astype(BF16)
        x1v = x1_ref[...]
        h2, r = _rms(x1v, gffn_ref[...], D_MODEL)
        h2_ref[...] = h2.astype(BF16)
        for c in range(FFN_HIDDEN // FFN_CHUNK):
            sl = slice(FFN_CHUNK * c, FFN_CHUNK * (c + 1))
            g = gs_ref[:, sl].astype(F32)
            u = us_ref[:, sl].astype(F32)
            dact = _dot_nt(dy, wd_ref[sl, :])
            sg = _sigmoid(g)
            silu = g * sg
            act_ref[:, sl] = (silu * u).astype(BF16)
            dg_ref[:, sl] = (dact * u * (sg * (1.0 + g * (1.0 - sg)))).astype(BF16)
            du_ref[:, sl] = (dact * silu).astype(BF16)
        dh2 = _dot(dg_ref[...], wg_ref[...]) + _dot(du_ref[...], wu_ref[...])
        dxn, dgn = _rms_bwd(x1v, r, gffn_ref[...], dh2, D_MODEL)
        dx1_ref[...] = dx2v + dxn
        dgffn_ref[...] += dgn

    row = lambda w: pl.BlockSpec((tm, w), lambda i: (i, 0))
    return pl.pallas_call(
        body, grid=(T // tm,), name="ffn_bwd",
        in_specs=[row(D_MODEL), row(D_MODEL), row(FFN_HIDDEN), row(FFN_HIDDEN)] + [WHOLE] * 4,
        out_specs=[row(D_MODEL), row(D_MODEL), row(FFN_HIDDEN), row(FFN_HIDDEN), row(FFN_HIDDEN), _acc_spec((1, D_MODEL))],
        out_shape=[jax.ShapeDtypeStruct((T, D_MODEL), F32), jax.ShapeDtypeStruct((T, D_MODEL), BF16),
                   jax.ShapeDtypeStruct((T, FFN_HIDDEN), BF16), jax.ShapeDtypeStruct((T, FFN_HIDDEN), BF16),
                   jax.ShapeDtypeStruct((T, FFN_HIDDEN), BF16), jax.ShapeDtypeStruct((1, D_MODEL), F32)],
        compiler_params=_cparams(),
    )(dx2, x1, gs, us, gffn, wg, wu, wd)


TN_K_TILE = 1024
TN_ACC_BYTES = 6 * 1024 * 1024


def matmul_tn(a, b, name):
    T, M = a.shape
    N = b.shape[1]
    tk = min(TN_K_TILE, T)
    tm = M if M <= 1024 else M // 2
    tn = max(d for d in range(128, N + 1, 128) if N % d == 0 and tm * d * 4 <= TN_ACC_BYTES)
    nk = T // tk

    def body(a_ref, b_ref, o_ref, acc):
        k = pl.program_id(2)

        @pl.when(k == 0)
        def _():
            acc[...] = jnp.zeros(acc.shape, F32)

        acc[...] += _dot_tn(a_ref[...].astype(BF16), b_ref[...].astype(BF16))

        @pl.when(k == nk - 1)
        def _():
            o_ref[...] = acc[...].astype(BF16)

    return pl.pallas_call(
        body, grid=(M // tm, N // tn, nk), name=name,
        in_specs=[pl.BlockSpec((tk, tm), lambda i, j, k: (k, i)), pl.BlockSpec((tk, tn), lambda i, j, k: (k, j))],
        out_specs=pl.BlockSpec((tm, tn), lambda i, j, k: (i, j)),
        out_shape=jax.ShapeDtypeStruct((M, N), BF16),
        scratch_shapes=[pltpu.VMEM((tm, tn), F32)],
        compiler_params=_cparams(),
    )(a, b)


def mixers_bwd(dx1, a, z, gsgu, wcat, wcat_t, bz, wp, pscale, goa, gos, gop, wout):
    T = a.shape[0]
    tm = min(ROW_TILE, T)

    def body(dx1_ref, a_ref, u_ref, vs_ref, pin_ref, halo_ref, gsgu_ref, wcat_ref, wcatt_ref, bz_ref, wp_ref,
             ps_ref, goa_ref, gos_ref, gop_ref, wout_ref,
             da_ref, delta_ref, du_ref, dvs_ref, dm_ref, mix_ref,
             dgoa_ref, dgos_ref, dgop_ref, dps_ref, dwp_ref, dwsp_ref, db_ref, dgsgu_ref):
        i = pl.program_id(0)

        @pl.when(i == 0)
        def _():
            for r in (dgoa_ref, dgos_ref, dgop_ref, dps_ref, dwp_ref, dwsp_ref, db_ref, dgsgu_ref):
                r[...] = jnp.zeros(r.shape, F32)

        a_v, u, vs = a_ref[...], u_ref[...], vs_ref[...]
        goa, gos, gop, pscale_v = goa_ref[...], gos_ref[...], gop_ref[...], ps_ref[...]
        vn, rv, zc, gm, m, yp_pre, yp, ra, rg, rp, mix = _mixers_fwd_tile(
            i, tm, a_v, u, vs, pin_ref[...], halo_ref[...], gsgu_ref[...], wcat_ref[...], bz_ref[...], wp_ref[...],
            pscale_v, goa, gos, gop)
        mix_ref[...] = mix
        dmix = _dot_nt(dx1_ref[...].astype(BF16), wout_ref[...])
        da, dgoa = _rms_bwd(a_v, ra, goa, dmix[:, :MLA_WIDTH], MLA_WIDTH)
        dgm, dgos = _rms_bwd(gm, rg, gos, dmix[:, MLA_WIDTH:MLA_WIDTH + SGU_WIDTH], SGU_WIDTH)
        dyp, dgop = _rms_bwd(yp, rp, gop, dmix[:, MLA_WIDTH + SGU_WIDTH:], POOL_WIDTH)
        da_ref[...] = da
        dgoa_ref[...] += dgoa
        dgos_ref[...] += dgos
        dgop_ref[...] += dgop
        prod = da * a_v
        ones = jnp.ones((8, HEAD_PAD), F32)
        for h in range(HEADS):
            lo = HEAD_PAD * h
            sums = lax.dot_general(ones, prod[:, lo:lo + HEAD_PAD], (((1,), (1,)), ((), ())), preferred_element_type=F32,
                                   precision=lax.Precision.HIGHEST)
            delta_ref[h] = sums[0:1, :]
        dps_ref[...] += jnp.sum(dyp * yp_pre, axis=0, keepdims=True)
        dyp_pre = (dyp * pscale_v).astype(BF16)
        dwp_ref[...] += _dot_tn(m, dyp_pre)
        dm_ref[...] = _dot_nt(dyp_pre, wp_ref[...])
        du_ref[...] = dgm * zc
        dzc = dgm * u
        masks = _head_masks((CHUNK, SGU_WIDTH), SGU_HEAD_DIM)
        lane_b = lax.broadcasted_iota(jnp.int32, (CHUNK, HEAD_PAD), 1)
        dvn_parts = []
        dwsp = jnp.zeros(dwsp_ref.shape, F32)
        db = jnp.zeros(db_ref.shape, F32)
        for c in range(tm // CHUNK):
            dz_c = dzc[CHUNK * c:CHUNK * (c + 1)]
            dzstack = jnp.concatenate([jnp.where(mk, dz_c, 0.0).astype(BF16) for mk in masks], axis=0)
            dvn_parts.append(_dot(wcatt_ref[...], dzstack))
            dwsp = dwsp + _dot_nt(dzstack, vn[CHUNK * c:CHUNK * (c + 1)].astype(BF16))
            for h, mk in enumerate(masks):
                col = jnp.sum(jnp.where(mk, dz_c, 0.0), axis=1, keepdims=True)
                db = db + jnp.where(lane_b == h, col, 0.0)
        dwsp_ref[...] += dwsp
        db_ref[...] += db
        dvs, dgsgu = _rms_bwd(vs, rv, gsgu_ref[...], jnp.concatenate(dvn_parts, axis=0), SGU_WIDTH)
        dvs_ref[...] = dvs
        dgsgu_ref[...] += dgsgu

    row = lambda w: pl.BlockSpec((tm, w), lambda i: (i, 0))
    head = pl.BlockSpec((HEADS, 1, tm), lambda i: (0, 0, i))
    acc_shapes = [(1, MLA_WIDTH), (1, SGU_WIDTH), (1, POOL_WIDTH), (1, POOL_WIDTH), (POOL_WIDTH, POOL_WIDTH),
                  (HEADS * CHUNK, CHUNK), (CHUNK, HEAD_PAD), (1, SGU_WIDTH)]
    return pl.pallas_call(
        body, grid=(T // tm,), name="mixers_bwd",
        in_specs=[row(D_MODEL), row(MLA_WIDTH)] + _z_specs(tm) + [WHOLE] * 10,
        out_specs=[row(MLA_WIDTH), head, row(256), row(256), row(256), row(D_MODEL)] + [_acc_spec(s) for s in acc_shapes],
        out_shape=[jax.ShapeDtypeStruct((T, MLA_WIDTH), F32), jax.ShapeDtypeStruct((HEADS, 1, T), F32),
                   jax.ShapeDtypeStruct((T, 256), F32), jax.ShapeDtypeStruct((T, 256), F32),
                   jax.ShapeDtypeStruct((T, 256), F32), jax.ShapeDtypeStruct((T, D_MODEL), BF16)]
                  + [jax.ShapeDtypeStruct(s, F32) for s in acc_shapes],
        compiler_params=_cparams(),
    )(dx1, a, z, z, z, z, gsgu, wcat, wcat_t, bz, wp, pscale, goa, gos, gop, wout)


def pool_bwd(dm):
    T = dm.shape[0]
    tm = min(ROW_TILE, T)
    nt = T // tm

    def body(dm_ref, next_ref, dpin_ref):
        i = pl.program_id(0)
        cnt, lane = _pool_counts(i, tm)
        dmv = dm_ref[...]
        win = _by_group(lane[:POOL_HALO], 2.0, 4.0, 8.0, 16.0)
        nxt = jnp.where(i < nt - 1, next_ref[...] / win, 0.0)
        r1 = jnp.concatenate([dmv / cnt, nxt], axis=0)
        n = tm + POOL_HALO
        r2 = r1 + pltpu.roll(r1, n - 1, 0)
        r4 = r2 + pltpu.roll(r2, n - 2, 0)
        r8 = r4 + pltpu.roll(r4, n - 4, 0)
        r16 = r8 + pltpu.roll(r8, n - 8, 0)
        dpin_ref[...] = _by_group(lane, r2[:tm], r4[:tm], r8[:tm], r16[:tm]) - dmv

    return pl.pallas_call(
        body, grid=(nt,), name="pool_bwd",
        in_specs=[pl.BlockSpec((tm, 256), lambda i: (i, 0)),
                  pl.BlockSpec((POOL_HALO, 256), lambda i: (jnp.minimum((i + 1) * (tm // POOL_HALO), T // POOL_HALO - 1), 0))],
        out_specs=pl.BlockSpec((tm, 256), lambda i: (i, 0)),
        out_shape=jax.ShapeDtypeStruct((T, 256), F32),
        compiler_params=_cparams(),
    )(dm, dm)


def attn_bwd(q, k, kt, v, do, lse, delta):
    _, T, _ = q.shape
    tb = min(ATTN_TILE, T)
    sb = min(ATTN_SUB, tb)
    ns = tb // sb
    nb = T // tb
    hb = ATTN_BWD_HEADS

    def body(q_ref, k_ref, kt_ref, v_ref, do_ref, lse_ref, delta_ref, dqt_ref, dk_ref, dv_ref, dk_s, dv_s):
        i, j = pl.program_id(1), pl.program_id(2)

        @pl.when((i == 0) & (j == 0))
        def _():
            dqt_ref[...] = jnp.zeros(dqt_ref.shape, F32)

        @pl.when(j == i)
        def _():
            dk_s[...] = jnp.zeros(dk_s.shape, F32)
            dv_s[...] = jnp.zeros(dv_s.shape, F32)

        def sub_block(h, a, b, masked):
            qa = slice(sb * a, sb * (a + 1))
            kb = slice(sb * b, sb * (b + 1))
            qv = q_ref[h, qa, :]
            dov = do_ref[qa, HEAD_PAD * h:HEAD_PAD * (h + 1)].astype(BF16)
            st = _dot_nt(k_ref[h, kb, :], qv)
            pt = jnp.exp2(st - lse_ref[h, :, qa])
            if masked:
                krow = lax.broadcasted_iota(jnp.int32, st.shape, 0)
                qcol = lax.broadcasted_iota(jnp.int32, st.shape, 1)
                pt = jnp.where(krow <= qcol, pt, 0.0)
            dv_s[h, kb, :] += _dot(pt.astype(BF16), dov)
            dpt = _dot_nt(v_ref[h, kb, :], dov)
            dst = (pt * (dpt - delta_ref[h, :, qa])).astype(BF16)
            dk_s[h, kb, :] += _dot(dst, qv)
            cols = pl.ds(pl.multiple_of(j * tb + sb * a, sb), sb)
            dqt_ref[h, :, cols] += _dot(kt_ref[h, :, kb], dst)

        @pl.when(j > i)
        def _():
            for a in range(ns):
                for h in range(hb):
                    for b in range(ns):
                        sub_block(h, a, b, False)

        @pl.when(j == i)
        def _():
            for a in range(ns):
                for h in range(hb):
                    for b in range(a + 1):
                        sub_block(h, a, b, a == b)

        @pl.when(j == nb - 1)
        def _():
            dk_ref[...] = dk_s[...] * (1.0 / LOG2E)
            dv_ref[...] = dv_s[...]

    qspec = pl.BlockSpec((hb, tb, HEAD_PAD), lambda g, i, j: (g, jnp.maximum(i, j), 0))
    kspec = pl.BlockSpec((hb, tb, HEAD_PAD), lambda g, i, j: (g, i, 0))
    ktspec = pl.BlockSpec((hb, HEAD_PAD, tb), lambda g, i, j: (g, 0, i))
    rowspec = pl.BlockSpec((hb, 1, tb), lambda g, i, j: (g, 0, jnp.maximum(i, j)))
    return pl.pallas_call(
        body, grid=(HEADS // hb, nb, nb), name="attn_bwd",
        in_specs=[qspec, kspec, ktspec, kspec, pl.BlockSpec((tb, hb * HEAD_PAD), lambda g, i, j: (jnp.maximum(i, j), g)), rowspec, rowspec],
        out_specs=[pl.BlockSpec((hb, HEAD_PAD, T), lambda g, i, j: (g, 0, 0)), kspec, kspec],
        out_shape=[jax.ShapeDtypeStruct((HEADS, HEAD_PAD, T), F32)] + [jax.ShapeDtypeStruct((HEADS, T, HEAD_PAD), F32)] * 2,
        scratch_shapes=[pltpu.VMEM((hb, tb, HEAD_PAD), F32), pltpu.VMEM((hb, tb, HEAD_PAD), F32)],
        compiler_params=_cparams(),
    )(q, k, kt, v, do, lse, delta)


def mixin_bwd(dres, x, z, rope, dqt, dk, dv, du, dvs, dpin, gmix, win, gql, wq, gkvl, wkv, gqh, gkh):
    T = x.shape[0]
    tm = min(ROW_TILE, T)
    part = min(MIXIN_BWD_PART, tm)

    def body(dres_ref, x_ref, z_ref, c_ref, sa_ref, sb_ref, dqt_ref, dk_ref, dv_ref, du_ref, dvs_ref, dpin_ref,
             gmix_ref, win_ref, gql_ref, wq_ref, gkvl_ref, wkv_ref, gqh_ref, gkh_ref,
             dx_ref, hn_ref, dz_ref, qn_ref, dqup_ref, kvn_ref, dkvup_ref,
             dgmix_ref, dgql_ref, dgkvl_ref, dgqh_ref, dgkh_ref):
        @pl.when(pl.program_id(0) == 0)
        def _():
            for r in (dgmix_ref, dgql_ref, dgkvl_ref, dgqh_ref, dgkh_ref):
                r[...] = jnp.zeros(r.shape, F32)

        lane = lax.broadcasted_iota(jnp.int32, (part, HEAD_PAD), 1)
        rope_lanes = (lane >= QK_NOPE) & (lane < QK_HEAD)
        for p in range(tm // part):
            rows = pl.ds(part * p, part)
            xv = x_ref[rows, :]
            hn, rx = _rms(xv, gmix_ref[...], D_MODEL)
            hn_ref[rows, :] = hn.astype(BF16)
            ql, kvl, kr, qn, rq, kvn, rkv, q_up, kv_up = _qkv_pre(z_ref[rows, :], gql_ref[...], wq_ref[...], gkvl_ref[...],
                                                                  wkv_ref[...])
            qn_ref[rows, :] = qn
            kvn_ref[rows, :] = kvn
            c, sa, sb = c_ref[rows, :], sa_ref[rows, :], sb_ref[rows, :]
            dkr = jnp.zeros((part, HEAD_PAD), F32)
            dgqh = jnp.zeros((1, HEAD_PAD), F32)
            dgkh = jnp.zeros((1, HEAD_PAD), F32)
            dq_parts, dk_parts, dv_parts = [], [], []
            for h in range(HEADS):
                lo = HEAD_PAD * h
                qh = q_up[:, lo:lo + HEAD_PAD]
                rqh = lax.rsqrt(jnp.sum(qh * qh, axis=-1, keepdims=True) * (1.0 / QK_HEAD) + EPS)
                dq_h = jnp.transpose(dqt_ref[h, :, rows]) * ATTN_SCALE
                dqh, dg = _rms_bwd(qh, rqh, gqh_ref[...], _rope_bwd(dq_h, c, sa, sb), QK_HEAD)
                dgqh = dgqh + dg
                dq_parts.append(dqh)
                kh = kv_up[:, lo:lo + HEAD_PAD] + kr
                rkh = lax.rsqrt(jnp.sum(kh * kh, axis=-1, keepdims=True) * (1.0 / QK_HEAD) + EPS)
                dkh, dg = _rms_bwd(kh, rkh, gkh_ref[...], _rope_bwd(dk_ref[h, rows, :], c, sa, sb), QK_HEAD)
                dgkh = dgkh + dg
                dkr = dkr + jnp.where(rope_lanes, dkh, 0.0)
                dk_parts.append(dkh)
                dv_parts.append(dv_ref[h, rows, :])
            dgqh_ref[...] += dgqh
            dgkh_ref[...] += dgkh
            dq_up = jnp.concatenate(dq_parts, axis=1).astype(BF16)
            dkv_up = jnp.concatenate(dk_parts + dv_parts, axis=1).astype(BF16)
            dqup_ref[rows, :] = dq_up
            dkvup_ref[rows, :] = dkv_up
            dql, dg = _rms_bwd(ql, rq, gql_ref[...], _dot(dq_up, wq_ref[...]), Q_LORA)
            dgql_ref[...] += dg
            dkvl, dg = _rms_bwd(kvl, rkv, gkvl_ref[...], _dot(dkv_up, wkv_ref[...]), KV_LORA)
            dgkvl_ref[...] += dg
            dz = jnp.concatenate([dql, dkvl, dkr, du_ref[rows, :], dvs_ref[rows, :], dpin_ref[rows, :]], axis=1).astype(BF16)
            dz_ref[rows, :] = dz
            dxn, dg = _rms_bwd(xv, rx, gmix_ref[...], _dot(dz, win_ref[...]), D_MODEL)
            dgmix_ref[...] += dg
            dx_ref[rows, :] = dres_ref[rows, :] + dxn

    row = lambda w: pl.BlockSpec((tm, w), lambda i: (i, 0))
    head = pl.BlockSpec((HEADS, tm, HEAD_PAD), lambda i: (0, i, 0))
    head_t = pl.BlockSpec((HEADS, HEAD_PAD, tm), lambda i: (0, 0, i))
    acc_shapes = [(1, D_MODEL), (1, Q_LORA), (1, KV_LORA), (1, HEAD_PAD), (1, HEAD_PAD)]
    out_rows = [(D_MODEL, F32), (D_MODEL, BF16), (Z_WIDTH, BF16), (Q_LORA, BF16), (HEADS * HEAD_PAD, BF16),
                (KV_LORA, BF16), (2 * HEADS * HEAD_PAD, BF16)]
    return pl.pallas_call(
        body, grid=(T // tm,), name="mixin_bwd",
        in_specs=[row(D_MODEL), row(D_MODEL), row(Z_WIDTH)] + [row(HEAD_PAD)] * 3 + [head_t, head, head, row(256), row(256), row(256)]
                 + [WHOLE] * 8,
        out_specs=[row(w) for w, _ in out_rows] + [_acc_spec(s) for s in acc_shapes],
        out_shape=[jax.ShapeDtypeStruct((T, w), dt) for w, dt in out_rows] + [jax.ShapeDtypeStruct(s, F32) for s in acc_shapes],
        compiler_params=_cparams(),
    )(dres, x, z, *rope, dqt, dk, dv, du, dvs, dpin, gmix, win, gql, wq, gkvl, wkv, gqh, gkh)


def _place():
    x, y, c = lax.axis_index("x"), lax.axis_index("y"), lax.axis_index("c")
    return x, y, c, 4 * x + 2 * y + c


def _layer_of(ref, l):
    return ref[:, l, :] if ref.shape[1] == DEPTH else ref[l]


def _layer_shape(shard):
    return (shard.shape[0], shard.shape[2]) if shard.shape[1] == DEPTH else shard.shape[1:]


def cast_shards(shards, wanted):
    n = len(shards)

    def body(*refs):
        for o_ref, (w, l) in zip(refs[n:], wanted):
            o_ref[...] = _layer_of(refs[w], l).astype(BF16)

    return pl.pallas_call(
        body, name="cast_shards", in_specs=[WHOLE] * n, out_specs=[WHOLE] * len(wanted),
        out_shape=[jax.ShapeDtypeStruct(_layer_shape(shards[w]), BF16) for w, _ in wanted],
        compiler_params=_cparams(),
    )(*shards)


def allgather_layer0(shards):
    n = len(shards)

    def body(*refs):
        ins, outs = refs[:n], refs[n:2 * n]
        stage = refs[2 * n:3 * n]
        send_sems, recv_sems, local_sems = refs[3 * n:]
        x, y, c, me = _place()
        sibling = (x, y, 1 - c)
        chips = [(1 - x, y), (x, 1 - y), (1 - x, 1 - y)]

        def copy(k, w, block_id, to, from_stage):
            return pltpu.make_async_remote_copy(
                src_ref=stage[w] if from_stage else outs[w].at[block_id], dst_ref=outs[w].at[block_id],
                send_sem=send_sems.at[k, w], recv_sem=recv_sems.at[k, w], device_id=to, device_id_type=MESH)

        def block_of(cx, cy, cc):
            return 4 * cx + 2 * cy + cc

        local, sent = [], []
        for w in range(n):
            stage[w][...] = _layer_of(ins[w], 0).astype(BF16)
            mine = pltpu.make_async_copy(stage[w], outs[w].at[me], local_sems.at[w])
            mine.start()
            local.append(mine)
            first = [copy(0, w, me, sibling, True)] + [copy(1 + j, w, me, (*chip, c), True) for j, chip in enumerate(chips)]
            for cp in first:
                cp.start()
            sent += first
        for j, chip in enumerate(chips):
            for w in range(n):
                copy(1 + j, w, block_of(*chip, c), (x, y, c), False).wait_recv()
                fwd = copy(4 + j, w, block_of(*chip, c), sibling, False)
                fwd.start()
                sent.append(fwd)
        for w in range(n):
            copy(0, w, block_of(x, y, 1 - c), (x, y, c), False).wait_recv()
            for j, chip in enumerate(chips):
                copy(4 + j, w, block_of(*chip, 1 - c), (x, y, c), False).wait_recv()
        for cp in local:
            cp.wait()
        for cp in sent:
            cp.wait_send()

    return pl.pallas_call(
        body, name="allgather_layer0",
        in_specs=[WHOLE] * n, out_specs=[ANY] * n,
        out_shape=[jax.ShapeDtypeStruct((N_DEV,) + _layer_shape(s), BF16) for s in shards],
        scratch_shapes=[pltpu.VMEM(_layer_shape(s), BF16) for s in shards]
                       + [pltpu.SemaphoreType.DMA((7, n)), pltpu.SemaphoreType.DMA((7, n)), pltpu.SemaphoreType.DMA((n,))],
        compiler_params=_cparams(),
    )(*shards)


def _peer(k):
    x, y, c, _ = _place()
    px = 1 - x if k & 4 else x
    py = 1 - y if k & 2 else y
    pc = 1 - c if k & 1 else c
    return (px, py, pc), 4 * px + 2 * py + pc


def exchange_start(srcs, after, gather, name):
    n = len(srcs)
    land_shapes = [((N_DEV,) + s.shape) if gather else s.shape for s in srcs]

    def body(*refs):
        src_refs, land_refs = refs[:n], refs[n:2 * n]
        send_sems, recv_sems = refs[2 * n + 1:3 * n + 1], refs[3 * n + 1:4 * n + 1]
        token = refs[-1]
        _, _, _, me = _place()
        for k in range(1, N_DEV):
            peer, peer_id = _peer(k)
            for w in range(n):
                pltpu.make_async_remote_copy(
                    src_ref=src_refs[w] if gather else src_refs[w].at[peer_id], dst_ref=land_refs[w].at[me],
                    send_sem=send_sems[w], recv_sem=recv_sems[w], device_id=peer, device_id_type=MESH).start()
        token[...] = jnp.zeros(token.shape, F32)

    hbm = lambda a: pltpu.with_memory_space_constraint(a, pltpu.HBM)
    outs = pl.pallas_call(
        body, name=name,
        out_shape=(pltpu.SemaphoreType.DMA(()),) * (2 * n)
                  + tuple(pltpu.HBM(s.shape, BF16) for s in srcs) + tuple(pltpu.HBM(s, BF16) for s in land_shapes)
                  + (jax.ShapeDtypeStruct((8, 128), F32),),
        in_specs=[HBM_SPEC] * (2 * n) + [ANY],
        out_specs=(SEM_SPEC,) * (2 * n) + (HBM_SPEC,) * (2 * n) + (WHOLE,),
        input_output_aliases={i: 2 * n + i for i in range(2 * n)},
        compiler_params=pltpu.CompilerParams(has_side_effects=pltpu.SideEffectType.DATAFLOW_SIDE_EFFECTING),
    )(*[hbm(s) for s in srcs], *[hbm(lax.empty(s, BF16)) for s in land_shapes], after)
    return list(outs[:n]), list(outs[n:2 * n]), list(outs[2 * n:3 * n]), list(outs[3 * n:4 * n]), outs[-1]


def exchange_wait(started, after, name):
    send_sems, recv_sems, srcs, lands, _ = started
    n = len(srcs)

    def body(*refs):
        land_refs = refs[n:2 * n]
        send_sems, recv_sems = refs[2 * n:3 * n], refs[3 * n:4 * n]
        x, y, c, _ = _place()
        for w in range(n):
            seven = land_refs[w].at[pl.ds(0, N_DEV - 1)]
            cp = pltpu.make_async_remote_copy(src_ref=seven, dst_ref=seven, send_sem=send_sems[w], recv_sem=recv_sems[w],
                                              device_id=(x, y, c), device_id_type=MESH)
            cp.wait_send()
            cp.wait_recv()

    outs = pl.pallas_call(
        body, name=name,
        out_shape=tuple(pltpu.HBM(s.shape, BF16) for s in srcs) + tuple(pltpu.HBM(l.shape, BF16) for l in lands),
        in_specs=[HBM_SPEC] * (2 * n) + [SEM_SPEC] * (2 * n) + [ANY],
        out_specs=(HBM_SPEC,) * (2 * n),
        input_output_aliases={i: i for i in range(2 * n)},
        compiler_params=pltpu.CompilerParams(has_side_effects=pltpu.SideEffectType.DATAFLOW_SIDE_EFFECTING),
    )(*srcs, *lands, *send_sems, *recv_sems, after)
    _, _, _, me = _place()
    filled = []
    for src, land in zip(outs[:n], outs[n:]):
        own = src[None] if src.ndim == 2 else lax.dynamic_slice_in_dim(src, me, 1, axis=0)
        filled.append(lax.dynamic_update_slice_in_dim(land, own, me, axis=0))
    return filled


def exchange_grads(pieces):
    n = len(pieces)

    def body(*refs):
        ins, outs = refs[:n], refs[n:2 * n]
        send_sems, recv_sems, local_sems = refs[2 * n:]
        x, y, c, me = _place()
        local = [pltpu.make_async_copy(ins[w].at[me], outs[w].at[me], local_sems.at[w]) for w in range(n)]
        for cp in local:
            cp.start()
        remote = []
        for k in range(1, N_DEV):
            px = 1 - x if k & 4 else x
            py = 1 - y if k & 2 else y
            pc = 1 - c if k & 1 else c
            peer = 4 * px + 2 * py + pc
            for w in range(n):
                cp = pltpu.make_async_remote_copy(
                    src_ref=ins[w].at[peer], dst_ref=outs[w].at[me], send_sem=send_sems.at[k - 1, w],
                    recv_sem=recv_sems.at[k - 1, w], device_id=(px, py, pc), device_id_type=MESH)
                cp.start()
                remote.append(cp)
        for cp in local:
            cp.wait()
        for cp in remote:
            cp.wait_recv()
        for cp in remote:
            cp.wait_send()

    return pl.pallas_call(
        body, name="exchange_grads",
        in_specs=[ANY] * n, out_specs=[ANY] * n,
        out_shape=[jax.ShapeDtypeStruct(p.shape, BF16) for p in pieces],
        scratch_shapes=[pltpu.SemaphoreType.DMA((7, n)), pltpu.SemaphoreType.DMA((7, n)), pltpu.SemaphoreType.DMA((n,))],
        compiler_params=_cparams(),
    )(*pieces)


def _adamw(w, g, m, v):
    m2 = ADAM_B1 * m + (1.0 - ADAM_B1) * g
    v2 = ADAM_B2 * v + (1.0 - ADAM_B2) * (g * g)
    m_hat = m2 / (1.0 - ADAM_B1 ** ADAM_STEP)
    v_hat = v2 / (1.0 - ADAM_B2 ** ADAM_STEP)
    delta = -ADAM_LR * (m_hat / (jnp.sqrt(v_hat) + ADAM_EPS) + ADAM_WD * w)
    return delta, m2, v2


def adamw_sharded(parts, w, m, v, name):
    L, R, C = w.shape
    fits = [d for d in range(16, min(R, 512) + 1, 16) if R % d == 0]
    br = max(fits) if fits else R
    nblk = R // br

    def body(*refs):
        p_refs = refs[:L]
        w_ref, m_ref, v_ref, g_ref, d_ref, m2_ref, v2_ref = refs[L:]

        def total(p_ref):
            g = p_ref[0].astype(F32)
            for s in range(1, N_DEV):
                g = g + p_ref[s].astype(F32)
            return g

        g = total(p_refs[0])
        for l in range(1, L):
            g = jnp.where(pl.program_id(0) == l, total(p_refs[l]), g)
        g_ref[...] = g
        d_ref[...], m2_ref[...], v2_ref[...] = _adamw(w_ref[...], g, m_ref[...], v_ref[...])

    blk = pl.BlockSpec((None, br, C), lambda l, i: (l, i, 0))

    def part_spec(k):
        return pl.BlockSpec((N_DEV, br, C), lambda l, i: (0, jnp.where(l == k, i, jnp.where(l < k, 0, nblk - 1)), 0))

    return pl.pallas_call(
        body, grid=(L, nblk), name=name,
        in_specs=[part_spec(k) for k in range(L)] + [blk, blk, blk],
        out_specs=[blk] * 4,
        out_shape=[jax.ShapeDtypeStruct((L, R, C), F32)] * 4,
        compiler_params=_cparams(),
    )(*parts, w, m, v)


def allreduce_adamw_small(gpart, w, m, v):
    R = gpart.shape[0]

    def body(g_ref, w_ref, m_ref, v_ref, grad_ref, d_ref, m2_ref, v2_ref, all_ref, send_sems, recv_sems):
        x, y, c, me = _place()
        sibling = (x, y, 1 - c)
        chips = [(1 - x, y), (x, 1 - y), (1 - x, 1 - y)]

        def copy(k, block_id, to, from_input):
            return pltpu.make_async_remote_copy(
                src_ref=g_ref if from_input else all_ref.at[block_id], dst_ref=all_ref.at[block_id],
                send_sem=send_sems.at[k], recv_sem=recv_sems.at[k], device_id=to, device_id_type=MESH)

        def block_of(cx, cy, cc):
            return 4 * cx + 2 * cy + cc

        all_ref[me] = g_ref[...]
        first = [copy(0, me, sibling, True)] + [copy(1 + j, me, (*chip, c), True) for j, chip in enumerate(chips)]
        for cp in first:
            cp.start()
        passed = [copy(4 + j, block_of(*chip, c), sibling, False) for j, chip in enumerate(chips)]
        for j, chip in enumerate(chips):
            copy(1 + j, block_of(*chip, c), (x, y, c), False).wait_recv()
            passed[j].start()
        copy(0, block_of(x, y, 1 - c), (x, y, c), False).wait_recv()
        for j, chip in enumerate(chips):
            copy(4 + j, block_of(*chip, 1 - c), (x, y, c), False).wait_recv()
        for cp in first + passed:
            cp.wait_send()
        g = all_ref[0]
        for s in range(1, N_DEV):
            g = g + all_ref[s]
        grad_ref[...] = g
        d_ref[...], m2_ref[...], v2_ref[...] = _adamw(w_ref[...], g, m_ref[...], v_ref[...])

    return pl.pallas_call(
        body, name="allreduce_adamw_small",
        in_specs=[WHOLE] * 4, out_specs=[WHOLE] * 4,
        out_shape=[jax.ShapeDtypeStruct((R, 128), F32)] * 4,
        scratch_shapes=[pltpu.VMEM((N_DEV, R, 128), F32), pltpu.SemaphoreType.DMA((7,)), pltpu.SemaphoreType.DMA((7,))],
        compiler_params=_cparams(),
    )(gpart, w, m, v)


def _shard_view(name, a):
    if name == "w_in":
        return a.transpose(2, 0, 1)
    return a.swapaxes(1, 2) if name in TRANSPOSED else a


def _shard_unview(name, a):
    if name == "w_in":
        return a.transpose(1, 2, 0)
    return a.swapaxes(1, 2) if name in TRANSPOSED else a


def _pad_head_rows(w, width):
    c = w.shape[1]
    return jnp.pad(w.reshape(HEADS, width, c), ((0, 0), (0, HEAD_PAD - width), (0, 0))).reshape(HEADS * HEAD_PAD, c)


def _unpad_head_rows(w, width):
    c = w.shape[1]
    return w.reshape(HEADS, HEAD_PAD, c)[:, :width].reshape(HEADS * width, c)


O1 = Q_LORA
O2 = O1 + KV_LORA
O3 = O2 + QK_ROPE


def _mixer_weights(gw):
    w_in = gw["w_in"].reshape(IN_WIDTH, D_MODEL)
    zero = lambda n: jnp.zeros((n, D_MODEL), BF16)
    win = jnp.concatenate([w_in[:O2], zero(QK_NOPE), w_in[O2:O3], zero(HEAD_PAD - QK_HEAD), w_in[O3:]], axis=0)
    wq = _pad_head_rows(gw["w_q_up"].reshape(HEADS * QK_HEAD, Q_LORA), QK_HEAD)
    w_kv = gw["w_kv_up"].reshape(HEADS, QK_NOPE + V_HEAD, KV_LORA)
    wk = jnp.pad(w_kv[:, :QK_NOPE], ((0, 0), (0, HEAD_PAD - QK_NOPE), (0, 0))).reshape(HEADS * HEAD_PAD, KV_LORA)
    wv = w_kv[:, QK_NOPE:].reshape(HEADS * V_HEAD, KV_LORA)
    wkv = jnp.concatenate([wk, wv], axis=0)
    wout = gw["w_out"].reshape(D_MODEL, D_MODEL)
    return dict(win=win, wq=wq, wkv=wkv, wout=wout)


def _ffn_weights(gw):
    return dict(wg=gw["w_gate"].reshape(FFN_HIDDEN, D_MODEL), wu=gw["w_up"].reshape(FFN_HIDDEN, D_MODEL),
                wd=gw["w_down"].reshape(FFN_HIDDEN, D_MODEL))


def _layer_small(p, l):
    row = lambda a: a.reshape(1, -1)
    pad_head = lambda g: jnp.pad(g, (0, HEAD_PAD - QK_HEAD)).reshape(1, HEAD_PAD)
    tril = jnp.tril(jnp.ones((CHUNK, CHUNK), F32))
    wsp = p["w_spatial"][l] * tril
    wcat = jnp.concatenate([wsp[h] for h in range(HEADS)], axis=1).astype(BF16)
    wcat_t = jnp.concatenate([wsp[h].T for h in range(HEADS)], axis=1).astype(BF16)
    bz = jnp.repeat(p["b_spatial"][l].T, SGU_HEAD_DIM, axis=1)
    wp = jax.scipy.linalg.block_diag(*[p["w_pool"][l][g] for g in range(HEADS)]).astype(BF16)
    return dict(gmix=row(p["g_mix_norm"][l]), gql=row(p["g_q_lat"][l]), gkvl=row(p["g_kv_lat"][l]),
                gqh=pad_head(p["g_q_head"][l]), gkh=pad_head(p["g_k_head"][l]), gsgu=row(p["g_sgu_v"][l]),
                wcat=wcat, wcat_t=wcat_t, bz=bz, wp=wp, pscale=row(p["pool_scale"][l]),
                goa=row(p["g_out_mla"][l]), gos=row(p["g_out_sgu"][l]), gop=row(p["g_out_pool"][l]),
                gffn=row(p["g_ffn_norm"][l]))


MIXER_SIDE = ("w_in", "w_q_up", "w_kv_up", "w_out")
FFN_SIDE = ("w_gate", "w_up", "w_down")
TRANSPOSED = ("w_in", "w_q_up", "w_kv_up", "w_gate", "w_up")
SHARDED = MIXER_SIDE + FFN_SIDE
SMALL = ("g_mix_norm", "g_q_lat", "g_kv_lat", "g_q_head", "g_k_head", "g_sgu_v", "w_spatial", "b_spatial", "w_pool",
         "pool_scale", "g_out_mla", "g_out_sgu", "g_out_pool", "g_ffn_norm")
WEIGHTS = ("g_mix_norm", "w_in", "g_q_lat", "w_q_up", "g_kv_lat", "w_kv_up", "g_q_head", "g_k_head", "g_sgu_v", "w_spatial",
           "b_spatial", "w_pool", "pool_scale", "g_out_mla", "g_out_sgu", "g_out_pool", "w_out", "g_ffn_norm", "w_gate",
           "w_up", "w_down")
PACK_ROWS = 8 * 128


def _pack_small(parts):
    flat = []
    for name in SMALL:
        a = parts[name].reshape(-1)
        flat.append(jnp.pad(a, (0, -a.shape[0] % PACK_ROWS)))
    return jnp.concatenate(flat).reshape(-1, 128)


def _unpack_small(packed, like):
    out, row = {}, 0
    for name in SMALL:
        n = math.prod(like[name].shape)
        rows = -(-n // PACK_ROWS) * 8
        out[name] = packed[row:row + rows].reshape(-1)[:n].reshape(like[name].shape)
        row += rows
    return out


def _forward_mixers(x, rope, W, S):
    z, q, k, v, kt, vt = mixin_fwd(x, rope, S["gmix"], W["win"], S["gql"], W["wq"], S["gkvl"], W["wkv"], S["gqh"], S["gkh"])
    a, lse = attn_fwd(q, k, vt)
    x1 = mixers_fwd(x, a, z, S["gsgu"], S["wcat"], S["bz"], S["wp"], S["pscale"], S["goa"], S["gos"], S["gop"], W["wout"])
    return x1, dict(x=x, z=z, q=q, k=k, kt=kt, v=v, a=a, lse=lse, x1=x1)


def _backward_ffn(dx2, W, S, A, l):
    dx1, h2, act, dg, du_ffn, dgffn = ffn_bwd(dx2, A["x1"], A["gs"], A["us"], S["gffn"], W["wg"], W["wu"], W["wd"])
    d_wd = matmul_tn(act, dx2, f"dw_down_{l}")
    d_wg = matmul_tn(dg, h2, f"dw_gate_{l}")
    d_wu = matmul_tn(du_ffn, h2, f"dw_up_{l}")
    big = {n: d.reshape(N_DEV, -1, D_MODEL) for n, d in (("w_gate", d_wg), ("w_up", d_wu), ("w_down", d_wd))}
    return dx1, big, dgffn


def _backward_mixers(dx1, dgffn, rope, W, S, A, l):
    (da, delta, du, dvs, dm, mix, dgoa, dgos, dgop, dps, dwp, dwsp, db, dgsgu) = mixers_bwd(
        dx1, A["a"], A["z"], S["gsgu"], S["wcat"], S["wcat_t"], S["bz"], S["wp"], S["pscale"], S["goa"], S["gos"],
        S["gop"], W["wout"])
    d_wout = matmul_tn(mix, dx1, f"dw_out_{l}")
    dpin = pool_bwd(dm)
    dqt, dk, dv = attn_bwd(A["q"], A["k"], A["kt"], A["v"], da, A["lse"], delta)
    (dx, hn, dz, qn, dq_up, kvn, dkv_up, dgmix, dgql, dgkvl, dgqh, dgkh) = mixin_bwd(
        dx1, A["x"], A["z"], rope, dqt, dk, dv, du, dvs, dpin, S["gmix"], W["win"], S["gql"], W["wq"], S["gkvl"],
        W["wkv"], S["gqh"], S["gkh"])
    d_win = matmul_tn(dz, hn, f"dw_in_{l}")
    d_wq = matmul_tn(dq_up, qn, f"dw_q_up_{l}")
    d_wkv = matmul_tn(dkv_up, kvn, f"dw_kv_up_{l}")
    d_win = jnp.concatenate([d_win[:O2], d_win[O2 + QK_NOPE:O2 + QK_HEAD], d_win[O2 + HEAD_PAD:]], axis=0)
    d_wk = d_wkv[:HEADS * HEAD_PAD].reshape(HEADS, HEAD_PAD, KV_LORA)[:, :QK_NOPE]
    d_wv = d_wkv[HEADS * HEAD_PAD:].reshape(HEADS, V_HEAD, KV_LORA)
    d_wkv = jnp.concatenate([d_wk, d_wv], axis=1)
    big = dict(w_in=d_win.reshape(N_DEV, -1, D_MODEL), w_q_up=_unpad_head_rows(d_wq, QK_HEAD).reshape(N_DEV, -1, Q_LORA),
               w_kv_up=d_wkv.reshape(N_DEV, -1, KV_LORA), w_out=d_wout.reshape(N_DEV, -1, D_MODEL))
    tril = jnp.tril(jnp.ones((CHUNK, CHUNK), F32))
    small = dict(g_mix_norm=dgmix[0], g_q_lat=dgql[0], g_kv_lat=dgkvl[0], g_q_head=dgqh[0, :QK_HEAD], g_k_head=dgkh[0, :QK_HEAD],
                 g_sgu_v=dgsgu[0], w_spatial=dwsp.reshape(HEADS, CHUNK, CHUNK) * tril, b_spatial=db[:, :HEADS].T,
                 w_pool=jnp.stack([dwp[64 * g:64 * (g + 1), 64 * g:64 * (g + 1)] for g in range(HEADS)]),
                 pool_scale=dps[0], g_out_mla=dgoa[0], g_out_sgu=dgos[0], g_out_pool=dgop[0], g_ffn_norm=dgffn[0])
    return dx, big, small


def kernel(x, positions, g_mix_norm, w_in, g_q_lat, w_q_up, g_kv_lat, w_kv_up, g_q_head, g_k_head, g_sgu_v, w_spatial, b_spatial, w_pool, pool_scale, g_out_mla, g_out_sgu, g_out_pool, w_out, g_ffn_norm, w_gate, w_up, w_down, loss_target, m_g_mix_norm, m_w_in, m_g_q_lat, m_w_q_up, m_g_kv_lat, m_w_kv_up, m_g_q_head, m_g_k_head, m_g_sgu_v, m_w_spatial, m_b_spatial, m_w_pool, m_pool_scale, m_g_out_mla, m_g_out_sgu, m_g_out_pool, m_w_out, m_g_ffn_norm, m_w_gate, m_w_up, m_w_down, v_g_mix_norm, v_w_in, v_g_q_lat, v_w_q_up, v_g_kv_lat, v_w_kv_up, v_g_q_head, v_g_k_head, v_g_sgu_v, v_w_spatial, v_b_spatial, v_w_pool, v_pool_scale, v_g_out_mla, v_g_out_sgu, v_g_out_pool, v_w_out, v_g_ffn_norm, v_w_gate, v_w_up, v_w_down):
    given = dict(locals())
    w = {n: given[n] for n in WEIGHTS}
    m = {n: given["m_" + n] for n in WEIGHTS}
    v = {n: given["v_" + n] for n in WEIGHTS}
    T = x.shape[1]
    xs = x.reshape(T, D_MODEL)

    half = QK_ROPE // 2
    inv_freq = 1.0 / (ROPE_THETA ** (jnp.arange(half, dtype=F32) / half))
    ang16 = positions.reshape(T).astype(F32)[:, None] * inv_freq
    ang = jnp.concatenate([jnp.zeros((T, QK_NOPE), F32), ang16, ang16, jnp.zeros((T, HEAD_PAD - QK_HEAD), F32)], axis=1)

    wv = {n: _shard_view(n, w[n]) for n in SHARDED}
    wanted = [(SHARDED.index(n), 0) for n in FFN_SIDE] + [(i, 1) for i in range(len(SHARDED))]
    bf = cast_shards([wv[n] for n in SHARDED], wanted)
    mixer0 = allgather_layer0([wv[n] for n in MIXER_SIDE])
    ag_ffn0 = exchange_start(bf[:len(FFN_SIDE)], mixer0[0], True, "ag_start_ffn0")
    ag_l1 = exchange_start(bf[len(FFN_SIDE):], ag_ffn0[-1], True, "ag_start_l1")
    rope = rope_tables(ang + (ag_ffn0[-1][0, 0] + ag_l1[-1][0, 0]))
    Ss = [_layer_small(w, l) for l in range(DEPTH)]

    x1, A0 = _forward_mixers(xs, rope, _mixer_weights(dict(zip(MIXER_SIDE, mixer0))), Ss[0])
    F0 = _ffn_weights(dict(zip(FFN_SIDE, exchange_wait(ag_ffn0, x1, "ag_wait_ffn0"))))
    h, A0["gs"], A0["us"] = ffn_fwd(x1, Ss[0]["gffn"], F0["wg"], F0["wu"], F0["wd"])
    layer1 = dict(zip(SHARDED, exchange_wait(ag_l1, h, "ag_wait_l1")))
    W1 = {**_mixer_weights(layer1), **_ffn_weights(layer1)}
    x1, A1 = _forward_mixers(h, rope, W1, Ss[1])
    h, A1["gs"], A1["us"] = ffn_fwd(x1, Ss[1]["gffn"], W1["wg"], W1["wu"], W1["wd"])
    dh, loss_part = loss_head(h, loss_target.reshape(T, D_MODEL))
    loss = lax.psum(loss_part[0, 0], ("x", "y", "c"))

    dx1, big_f1, dgffn1 = _backward_ffn(dh, W1, Ss[1], A1, 1)
    dh, big_m1, small1 = _backward_mixers(dx1, dgffn1, rope, W1, Ss[1], A1, 1)
    rs_l1 = exchange_start([{**big_m1, **big_f1}[n] for n in SHARDED], dh, False, "rs_start_l1")
    W0 = {**_mixer_weights(dict(zip(MIXER_SIDE, mixer0))), **F0}
    S0 = dict(Ss[0], gffn=Ss[0]["gffn"] + rs_l1[-1][0, 0])
    dx1, big_f0, dgffn0 = _backward_ffn(dh, W0, S0, A0, 0)
    rs_ffn0 = exchange_start([big_f0[n] for n in FFN_SIDE], dx1, False, "rs_start_ffn0")
    S0 = dict(Ss[0], gsgu=Ss[0]["gsgu"] + rs_ffn0[-1][0, 0])
    dh, big_m0, small0 = _backward_mixers(dx1, dgffn0, rope, W0, S0, A0, 0)
    grad_x = dh.reshape(x.shape)
    smalls = [small0, small1]

    parts1 = dict(zip(SHARDED, exchange_wait(rs_l1, dh, "rs_wait_l1")))
    parts0 = dict(zip(FFN_SIDE, exchange_wait(rs_ffn0, dh, "rs_wait_ffn0")))
    parts0.update(zip(MIXER_SIDE, exchange_grads([big_m0[n] for n in MIXER_SIDE])))
    grad, delta, new_m, new_v = {}, {}, {}, {}
    for n in SHARDED:
        parts, views = [parts0[n], parts1[n]], [_shard_view(n, a[n]) for a in (w, m, v)]
        if n == "w_in":
            parts = [jnp.stack(parts, axis=2).reshape(N_DEV, -1, D_MODEL)]
            views = [a.reshape(1, -1, D_MODEL) for a in views]
        outs = adamw_sharded(parts, *views, f"adamw_{n}")
        if n == "w_in":
            outs = [o.reshape(-1, DEPTH, D_MODEL) for o in outs]
        grad[n], delta[n], new_m[n], new_v[n] = [_shard_unview(n, o) for o in outs]

    small_part = {n: jnp.stack([smalls[l][n] for l in range(DEPTH)]) for n in SMALL}
    outs = allreduce_adamw_small(_pack_small(small_part), _pack_small(w), _pack_small(m), _pack_small(v))
    for d, o in zip((grad, delta, new_m, new_v), outs):
        d.update(_unpack_small(o, w))

    return (loss, grad_x, *[grad[n] for n in WEIGHTS], *[delta[n] for n in WEIGHTS], *[new_m[n] for n in WEIGHTS],
            *[new_v[n] for n in WEIGHTS])
```

```python
import functools
import math

import jax
import jax.numpy as jnp
from jax import lax
from jax.experimental import pallas as pl
from jax.experimental.pallas import tpu as pltpu

F32 = jnp.float32
BF16 = jnp.bfloat16

N_DEV = 8
DEPTH = 2
D_MODEL = 1024
HEADS = 4
HEAD_PAD = 128
QK_NOPE = 64
QK_ROPE = 32
QK_HEAD = QK_NOPE + QK_ROPE
V_HEAD = 128
Q_LORA = 256
KV_LORA = 128
SGU_WIDTH = 256
SGU_HEAD_DIM = 64
CHUNK = 128
POOL_WIDTH = 256
POOL_WINDOWS = (2, 4, 8, 16)
POOL_HALO = 16
MLA_WIDTH = 512
IN_WIDTH = 1184
Z_WIDTH = 1280
FFN_HIDDEN = 2816
FFN_CHUNK = 256
ROPE_THETA = 10000.0
EPS = 1e-6
ATTN_SCALE = 1.0 / math.sqrt(QK_HEAD)
LOG2E = 1.4426950408889634
NEG_BIG = -1e30

ADAM_LR = 0.001
ADAM_B1 = 0.9
ADAM_B2 = 0.999
ADAM_EPS = 1e-08
ADAM_WD = 0.01
ADAM_STEP = 10

VMEM_LIMIT = 56 * 1024 * 1024
ROW_TILE = 512
MIXIN_PART = 256
MIXIN_BWD_PART = 512
ATTN_TILE = 1024
ATTN_SUB = 512
ATTN_FWD_SUB = 1024
ATTN_BWD_HEADS = 2
MESH = pl.DeviceIdType.MESH

WHOLE = pl.BlockSpec(memory_space=pltpu.VMEM)
ANY = pl.BlockSpec(memory_space=pl.ANY)
HBM_SPEC = pl.BlockSpec(memory_space=pltpu.HBM)
SEM_SPEC = pl.BlockSpec(memory_space=pltpu.SEMAPHORE)


def _cparams(**kw):
    return pltpu.CompilerParams(vmem_limit_bytes=VMEM_LIMIT, **kw)


def _dot(a, b):
    return jnp.dot(a, b, preferred_element_type=F32)


def _dot_nt(a, b):
    return lax.dot_general(a, b, (((1,), (1,)), ((), ())), preferred_element_type=F32)


def _dot_tn(a, b):
    return lax.dot_general(a, b, (((0,), (0,)), ((), ())), preferred_element_type=F32)


def _rms(x, g, n):
    r = lax.rsqrt(jnp.sum(x * x, axis=-1, keepdims=True) * (1.0 / n) + EPS)
    return x * r * g, r


def _rms_bwd(x, r, g, dy, n):
    gdy = dy * g
    dx = r * gdy - x * (r * r * r) * (jnp.sum(x * gdy, axis=-1, keepdims=True) * (1.0 / n))
    dg = jnp.sum(dy * (x * r), axis=0, keepdims=True)
    return dx, dg


def _sigmoid(x):
    return 1.0 / (1.0 + jnp.exp(-x))


def rope_tables(ang):
    T = ang.shape[0]
    tm = min(ROW_TILE, T)

    def body(ang_ref, c_ref, sa_ref, sb_ref):
        a = ang_ref[...]
        lane = lax.broadcasted_iota(jnp.int32, a.shape, 1)
        s = jnp.sin(a)
        c_ref[...] = jnp.cos(a)
        sa_ref[...] = jnp.where(lane < QK_NOPE + QK_ROPE // 2, -s, 0.0)
        sb_ref[...] = jnp.where(lane >= QK_NOPE + QK_ROPE // 2, s, 0.0)

    row = pl.BlockSpec((tm, HEAD_PAD), lambda i: (i, 0))
    return pl.pallas_call(
        body, grid=(T // tm,), name="rope_tables", in_specs=[row], out_specs=[row] * 3,
        out_shape=[jax.ShapeDtypeStruct((T, HEAD_PAD), F32)] * 3, compiler_params=_cparams(),
    )(ang)


def _rope(x, c, sa, sb):
    half = QK_ROPE // 2
    return x * c + pltpu.roll(x, HEAD_PAD - half, 1) * sa + pltpu.roll(x, half, 1) * sb


def _rope_bwd(dy, c, sa, sb):
    half = QK_ROPE // 2
    return dy * c + pltpu.roll(dy * sa, half, 1) + pltpu.roll(dy * sb, HEAD_PAD - half, 1)


def _head_masks(shape, width):
    lane = lax.broadcasted_iota(jnp.int32, shape, len(shape) - 1)
    return [(lane >= width * h) & (lane < width * (h + 1)) for h in range(HEADS)]


def _qkv_pre(z, gql, wq, gkvl, wkv):
    ql = z[:, 0:Q_LORA]
    kvl = z[:, Q_LORA:Q_LORA + KV_LORA]
    kr = z[:, Q_LORA + KV_LORA:Q_LORA + KV_LORA + HEAD_PAD]
    qn, rq = _rms(ql, gql, Q_LORA)
    kvn, rkv = _rms(kvl, gkvl, KV_LORA)
    qn = qn.astype(BF16)
    kvn = kvn.astype(BF16)
    q_up = _dot_nt(qn, wq)
    kv_up = _dot_nt(kvn, wkv)
    return ql, kvl, kr, qn, rq, kvn, rkv, q_up, kv_up


def mixin_fwd(x, rope, gmix, win, gql, wq, gkvl, wkv, gqh, gkh):
    T = x.shape[0]
    tm = min(2 * ROW_TILE, T)
    part = min(MIXIN_PART, tm)

    def body(x_ref, c_ref, sa_ref, sb_ref, gmix_ref, win_ref, gql_ref, wq_ref, gkvl_ref, wkv_ref, gqh_ref, gkh_ref,
             z_ref, q_ref, k_ref, v_ref, kt_ref, vt_ref):
        for p in range(tm // part):
            rows = pl.ds(part * p, part)
            hn = _rms(x_ref[rows, :], gmix_ref[...], D_MODEL)[0].astype(BF16)
            z = _dot_nt(hn, win_ref[...])
            z_ref[rows, :] = z
            _, _, kr, _, _, _, _, q_up, kv_up = _qkv_pre(z, gql_ref[...], wq_ref[...], gkvl_ref[...], wkv_ref[...])
            c, sa, sb = c_ref[rows, :], sa_ref[rows, :], sb_ref[rows, :]
            for h in range(HEADS):
                lo = HEAD_PAD * h
                qh = _rms(q_up[:, lo:lo + HEAD_PAD], gqh_ref[...], QK_HEAD)[0]
                q_ref[h, rows, :] = (_rope(qh, c, sa, sb) * (ATTN_SCALE * LOG2E)).astype(BF16)
                kh = _rope(_rms(kv_up[:, lo:lo + HEAD_PAD] + kr, gkh_ref[...], QK_HEAD)[0], c, sa, sb)
                k_ref[h, rows, :] = kh.astype(BF16)
                kt_ref[h, :, rows] = jnp.transpose(kh).astype(BF16)
                vh = kv_up[:, HEADS * HEAD_PAD + lo:HEADS * HEAD_PAD + lo + HEAD_PAD]
                v_ref[h, rows, :] = vh.astype(BF16)
                vt_ref[h, :, rows] = jnp.transpose(vh).astype(BF16)

    row = lambda w: pl.BlockSpec((tm, w), lambda i: (i, 0))
    head = pl.BlockSpec((HEADS, tm, HEAD_PAD), lambda i: (0, i, 0))
    head_t = pl.BlockSpec((HEADS, HEAD_PAD, tm), lambda i: (0, 0, i))
    return pl.pallas_call(
        body, grid=(T // tm,), name="mixin_fwd",
        in_specs=[row(D_MODEL)] + [row(HEAD_PAD)] * 3 + [WHOLE] * 8,
        out_specs=[row(Z_WIDTH), head, head, head, head_t, head_t],
        out_shape=[jax.ShapeDtypeStruct((T, Z_WIDTH), F32)] + [jax.ShapeDtypeStruct((HEADS, T, HEAD_PAD), BF16)] * 3
                  + [jax.ShapeDtypeStruct((HEADS, HEAD_PAD, T), BF16)] * 2,
        compiler_params=_cparams(),
    )(x, *rope, gmix, win, gql, wq, gkvl, wkv, gqh, gkh)


def attn_fwd(q, k, vt):
    _, T, _ = q.shape
    tb = min(ATTN_TILE, T)
    sb = min(ATTN_FWD_SUB, tb)
    ns = tb // sb
    nb = T // tb

    def body(q_ref, k_ref, vt_ref, o_ref, lse_ref, m_s, l_s, acc_s):
        i, j = pl.program_id(0), pl.program_id(1)

        @pl.when(j == 0)
        def _():
            m_s[...] = jnp.full(m_s.shape, -jnp.inf, F32)
            l_s[...] = jnp.zeros(l_s.shape, F32)
            acc_s[...] = jnp.zeros(acc_s.shape, F32)

        def sub_block(h, a, b, masked):
            qa = slice(sb * a, sb * (a + 1))
            kb = slice(sb * b, sb * (b + 1))
            st = _dot_nt(k_ref[h, kb, :], q_ref[h, qa, :])
            if masked:
                krow = lax.broadcasted_iota(jnp.int32, st.shape, 0)
                qcol = lax.broadcasted_iota(jnp.int32, st.shape, 1)
                st = jnp.where(krow <= qcol, st, NEG_BIG)
            m_prev = m_s[h, :, qa]
            m_new = jnp.maximum(m_prev, jnp.max(st, axis=0, keepdims=True))
            alpha = jnp.exp2(m_prev - m_new)
            pt = jnp.exp2(st - m_new)
            l_s[h, :, qa] = alpha * l_s[h, :, qa] + jnp.sum(pt, axis=0, keepdims=True)
            acc_s[h, :, qa] = alpha * acc_s[h, :, qa] + _dot(vt_ref[h, :, kb], pt.astype(BF16))
            m_s[h, :, qa] = m_new

        @pl.when(j < i)
        def _():
            for b in range(ns):
                for h in range(HEADS):
                    for a in range(ns):
                        sub_block(h, a, b, False)

        @pl.when(j == i)
        def _():
            for b in range(ns):
                for h in range(HEADS):
                    for a in range(b, ns):
                        sub_block(h, a, b, a == b)
            for h in range(HEADS):
                l = l_s[h]
                o_ref[:, HEAD_PAD * h:HEAD_PAD * (h + 1)] = jnp.transpose(acc_s[h] / l)
                lse_ref[h] = m_s[h] + jnp.log2(l)

    qspec = pl.BlockSpec((HEADS, tb, HEAD_PAD), lambda i, j: (0, i, 0))
    kspec = pl.BlockSpec((HEADS, tb, HEAD_PAD), lambda i, j: (0, jnp.minimum(i, j), 0))
    vspec = pl.BlockSpec((HEADS, HEAD_PAD, tb), lambda i, j: (0, 0, jnp.minimum(i, j)))
    return pl.pallas_call(
        body, grid=(nb, nb), name="attn_fwd",
        in_specs=[qspec, kspec, vspec],
        out_specs=[pl.BlockSpec((tb, MLA_WIDTH), lambda i, j: (i, 0)), pl.BlockSpec((HEADS, 1, tb), lambda i, j: (0, 0, i))],
        out_shape=[jax.ShapeDtypeStruct((T, MLA_WIDTH), F32), jax.ShapeDtypeStruct((HEADS, 1, T), F32)],
        scratch_shapes=[pltpu.VMEM((HEADS, 1, tb), F32), pltpu.VMEM((HEADS, 1, tb), F32), pltpu.VMEM((HEADS, HEAD_PAD, tb), F32)],
        compiler_params=_cparams(),
    )(q, k, vt)


def _sgu_fwd_chunk(vn_c, wcat, bz, masks):
    vstack = jnp.concatenate([jnp.where(mk, vn_c, 0.0).astype(BF16) for mk in masks], axis=0)
    return _dot(wcat, vstack) + bz


def _pool_counts(i, tm):
    pos1 = (i * tm + 1 + lax.broadcasted_iota(jnp.int32, (tm, POOL_WIDTH), 0)).astype(F32)
    lane = lax.broadcasted_iota(jnp.int32, (tm, POOL_WIDTH), 1)
    win = jnp.where(lane < 64, 2.0, jnp.where(lane < 128, 4.0, jnp.where(lane < 192, 8.0, 16.0)))
    return jnp.minimum(pos1, win), lane


def _by_group(lane, s2, s4, s8, s16):
    return jnp.where(lane < 64, s2, jnp.where(lane < 128, s4, jnp.where(lane < 192, s8, s16)))


def _pool_means(pin, halo, i, tm):
    s1 = jnp.concatenate([halo, pin], axis=0)
    s2 = s1 + pltpu.roll(s1, 1, 0)
    s4 = s2 + pltpu.roll(s2, 2, 0)
    s8 = s4 + pltpu.roll(s4, 4, 0)
    s16 = s8 + pltpu.roll(s8, 8, 0)
    cnt, lane = _pool_counts(i, tm)
    sel = _by_group(lane, s2[POOL_HALO:], s4[POOL_HALO:], s8[POOL_HALO:], s16[POOL_HALO:])
    return sel / cnt - pin


def _mixers_fwd_tile(i, tm, a, u, vs, pin, halo, gsgu, wcat, bz, wp, pscale, goa, gos, gop):
    vn, rv = _rms(vs, gsgu, SGU_WIDTH)
    masks = _head_masks((CHUNK, SGU_WIDTH), SGU_HEAD_DIM)
    zc = jnp.concatenate([_sgu_fwd_chunk(vn[CHUNK * c:CHUNK * (c + 1)], wcat, bz, masks) for c in range(tm // CHUNK)], axis=0)
    gm = u * zc
    halo = jnp.where(i > 0, halo, 0.0)
    m = _pool_means(pin, halo, i, tm).astype(BF16)
    yp_pre = _dot(m, wp)
    yp = yp_pre * pscale
    na, ra = _rms(a, goa, MLA_WIDTH)
    ng, rg = _rms(gm, gos, SGU_WIDTH)
    npo, rp = _rms(yp, gop, POOL_WIDTH)
    mix = jnp.concatenate([na, ng, npo], axis=1).astype(BF16)
    return vn, rv, zc, gm, m, yp_pre, yp, ra, rg, rp, mix


def _z_specs(tm):
    col = lambda c: pl.BlockSpec((tm, 256), lambda i: (i, c))
    halo = pl.BlockSpec((POOL_HALO, 256), lambda i: (jnp.maximum(i * (tm // POOL_HALO) - 1, 0), 4))
    return [col(2), col(3), col(4), halo]


def mixers_fwd(x, a, z, gsgu, wcat, bz, wp, pscale, goa, gos, gop, wout):
    T = x.shape[0]
    tm = min(ROW_TILE, T)

    def body(x_ref, a_ref, u_ref, vs_ref, pin_ref, halo_ref, gsgu_ref, wcat_ref, bz_ref, wp_ref, ps_ref,
             goa_ref, gos_ref, gop_ref, wout_ref, x1_ref):
        i = pl.program_id(0)
        mix = _mixers_fwd_tile(i, tm, a_ref[...], u_ref[...], vs_ref[...], pin_ref[...], halo_ref[...], gsgu_ref[...],
                               wcat_ref[...], bz_ref[...], wp_ref[...], ps_ref[...], goa_ref[...], gos_ref[...],
                               gop_ref[...])[-1]
        x1_ref[...] = x_ref[...] + _dot(mix, wout_ref[...])

    row = lambda w: pl.BlockSpec((tm, w), lambda i: (i, 0))
    return pl.pallas_call(
        body, grid=(T // tm,), name="mixers_fwd",
        in_specs=[row(D_MODEL), row(MLA_WIDTH)] + _z_specs(tm) + [WHOLE] * 9,
        out_specs=row(D_MODEL),
        out_shape=jax.ShapeDtypeStruct((T, D_MODEL), F32),
        compiler_params=_cparams(),
    )(x, a, z, z, z, z, gsgu, wcat, bz, wp, pscale, goa, gos, gop, wout)


def ffn_fwd(x1, gffn, wg, wu, wd, target=None):
    T = x1.shape[0]
    tm = min(ROW_TILE, T)
    with_loss = target is not None

    def body(*refs):
        x1_ref, gffn_ref, wg_ref, wu_ref, wd_ref = refs[:5]
        outs = refs[6:] if with_loss else refs[5:]
        x1v = x1_ref[...]
        h2 = _rms(x1v, gffn_ref[...], D_MODEL)[0].astype(BF16)
        acc = x1v
        for c in range(FFN_HIDDEN // FFN_CHUNK):
            sl = slice(FFN_CHUNK * c, FFN_CHUNK * (c + 1))
            g = _dot_nt(h2, wg_ref[sl, :])
            u = _dot_nt(h2, wu_ref[sl, :])
            outs[-2][:, sl] = g.astype(BF16)
            outs[-1][:, sl] = u.astype(BF16)
            act = (g * _sigmoid(g) * u).astype(BF16)
            acc = acc + _dot(act, wd_ref[sl, :])
        if not with_loss:
            outs[0][...] = acc
            return
        dy_ref, loss_ref = outs[0], outs[1]

        @pl.when(pl.program_id(0) == 0)
        def _():
            loss_ref[...] = jnp.zeros(loss_ref.shape, F32)

        err = acc - refs[5][...]
        dy_ref[...] = err * (1.0 / D_MODEL)
        per_row = jnp.sum(err * err, axis=1, keepdims=True) * (1.0 / D_MODEL)
        loss_ref[...] += 0.5 * jnp.sum(per_row, axis=0, keepdims=True)

    row = lambda w: pl.BlockSpec((tm, w), lambda i: (i, 0))
    hidden = [jax.ShapeDtypeStruct((T, FFN_HIDDEN), BF16)] * 2
    if with_loss:
        return pl.pallas_call(
            body, grid=(T // tm,), name="ffn_fwd_loss",
            in_specs=[row(D_MODEL)] + [WHOLE] * 4 + [row(D_MODEL)],
            out_specs=[row(D_MODEL), pl.BlockSpec((1, 1), lambda i: (0, 0)), row(FFN_HIDDEN), row(FFN_HIDDEN)],
            out_shape=[jax.ShapeDtypeStruct((T, D_MODEL), F32), jax.ShapeDtypeStruct((1, 1), F32)] + hidden,
            compiler_params=_cparams(),
        )(x1, gffn, wg, wu, wd, target)
    return pl.pallas_call(
        body, grid=(T // tm,), name="ffn_fwd",
        in_specs=[row(D_MODEL)] + [WHOLE] * 4,
        out_specs=[row(D_MODEL), row(FFN_HIDDEN), row(FFN_HIDDEN)],
        out_shape=[jax.ShapeDtypeStruct((T, D_MODEL), F32)] + hidden,
        compiler_params=_cparams(),
    )(x1, gffn, wg, wu, wd)


def _acc_spec(shape):
    return pl.BlockSpec(shape, lambda i: (0,) * len(shape))


def ffn_bwd(dx2, x1, gs, us, gffn, wg, wu, wd):
    T = x1.shape[0]
    tm = min(ROW_TILE // 2, T)

    def body(dx2_ref, x1_ref, gs_ref, us_ref, gffn_ref, wg_ref, wu_ref, wd_ref,
             dx1_ref, h2_ref, act_ref, dg_ref, du_ref, dgffn_ref):
        @pl.when(pl.program_id(0) == 0)
        def _():
            dgffn_ref[...] = jnp.zeros(dgffn_ref.shape, F32)

        dx2v = dx2_ref[...]
        dy = dx2v.astype(BF16)
        x1v = x1_ref[...]
        h2, r = _rms(x1v, gffn_ref[...], D_MODEL)
        h2_ref[...] = h2.astype(BF16)
        for c in range(FFN_HIDDEN // FFN_CHUNK):
            sl = slice(FFN_CHUNK * c, FFN_CHUNK * (c + 1))
            g = gs_ref[:, sl].astype(F32)
            u = us_ref[:, sl].astype(F32)
            dact = _dot_nt(dy, wd_ref[sl, :])
            sg = _sigmoid(g)
            silu = g * sg
            act_ref[:, sl] = (silu * u).astype(BF16)
            dg_ref[:, sl] = (dact * u * (sg * (1.0 + g * (1.0 - sg)))).astype(BF16)
            du_ref[:, sl] = (dact * silu).astype(BF16)
        dh2 = _dot(dg_ref[...], wg_ref[...]) + _dot(du_ref[...], wu_ref[...])
        dxn, dgn = _rms_bwd(x1v, r, gffn_ref[...], dh2, D_MODEL)
        dx1_ref[...] = dx2v + dxn
        dgffn_ref[...] += dgn

    row = lambda w: pl.BlockSpec((tm, w), lambda i: (i, 0))
    return pl.pallas_call(
        body, grid=(T // tm,), name="ffn_bwd",
        in_specs=[row(D_MODEL), row(D_MODEL), row(FFN_HIDDEN), row(FFN_HIDDEN)] + [WHOLE] * 4,
        out_specs=[row(D_MODEL), row(D_MODEL), row(FFN_HIDDEN), row(FFN_HIDDEN), row(FFN_HIDDEN), _acc_spec((1, D_MODEL))],
        out_shape=[jax.ShapeDtypeStruct((T, D_MODEL), F32), jax.ShapeDtypeStruct((T, D_MODEL), BF16),
                   jax.ShapeDtypeStruct((T, FFN_HIDDEN), BF16), jax.ShapeDtypeStruct((T, FFN_HIDDEN), BF16),
                   jax.ShapeDtypeStruct((T, FFN_HIDDEN), BF16), jax.ShapeDtypeStruct((1, D_MODEL), F32)],
        compiler_params=_cparams(),
    )(dx2, x1, gs, us, gffn, wg, wu, wd)


TN_K_TILE = 2048
TN_ACC_BYTES = 6 * 1024 * 1024


def matmul_tn(a, b, name):
    T, M = a.shape
    N = b.shape[1]
    tk = min(TN_K_TILE, T)
    tm = M if M <= 1024 else M // 2
    tn = max(d for d in range(128, N + 1, 128) if N % d == 0 and tm * d * 4 <= TN_ACC_BYTES)
    nk = T // tk

    def body(a_ref, b_ref, o_ref, acc):
        k = pl.program_id(2)

        @pl.when(k == 0)
        def _():
            acc[...] = jnp.zeros(acc.shape, F32)

        acc[...] += _dot_tn(a_ref[...].astype(BF16), b_ref[...].astype(BF16))

        @pl.when(k == nk - 1)
        def _():
            o_ref[...] = acc[...].astype(BF16)

    return pl.pallas_call(
        body, grid=(M // tm, N // tn, nk), name=name,
        in_specs=[pl.BlockSpec((tk, tm), lambda i, j, k: (k, i)), pl.BlockSpec((tk, tn), lambda i, j, k: (k, j))],
        out_specs=pl.BlockSpec((tm, tn), lambda i, j, k: (i, j)),
        out_shape=jax.ShapeDtypeStruct((M, N), BF16),
        scratch_shapes=[pltpu.VMEM((tm, tn), F32)],
        compiler_params=_cparams(),
    )(a, b)


def mixers_bwd(dx1, a, z, gsgu, wcat, wcat_t, bz, wp, pscale, goa, gos, gop, wout):
    T = a.shape[0]
    tm = min(ROW_TILE, T)

    def body(dx1_ref, a_ref, u_ref, vs_ref, pin_ref, halo_ref, gsgu_ref, wcat_ref, wcatt_ref, bz_ref, wp_ref,
             ps_ref, goa_ref, gos_ref, gop_ref, wout_ref,
             da_ref, delta_ref, du_ref, dvs_ref, dm_ref, mix_ref,
             dgoa_ref, dgos_ref, dgop_ref, dps_ref, dwp_ref, dwsp_ref, db_ref, dgsgu_ref):
        i = pl.program_id(0)

        @pl.when(i == 0)
        def _():
            for r in (dgoa_ref, dgos_ref, dgop_ref, dps_ref, dwp_ref, dwsp_ref, db_ref, dgsgu_ref):
                r[...] = jnp.zeros(r.shape, F32)

        a_v, u, vs = a_ref[...], u_ref[...], vs_ref[...]
        goa, gos, gop, pscale_v = goa_ref[...], gos_ref[...], gop_ref[...], ps_ref[...]
        vn, rv, zc, gm, m, yp_pre, yp, ra, rg, rp, mix = _mixers_fwd_tile(
            i, tm, a_v, u, vs, pin_ref[...], halo_ref[...], gsgu_ref[...], wcat_ref[...], bz_ref[...], wp_ref[...],
            pscale_v, goa, gos, gop)
        mix_ref[...] = mix
        dmix = _dot_nt(dx1_ref[...].astype(BF16), wout_ref[...])
        da, dgoa = _rms_bwd(a_v, ra, goa, dmix[:, :MLA_WIDTH], MLA_WIDTH)
        dgm, dgos = _rms_bwd(gm, rg, gos, dmix[:, MLA_WIDTH:MLA_WIDTH + SGU_WIDTH], SGU_WIDTH)
        dyp, dgop = _rms_bwd(yp, rp, gop, dmix[:, MLA_WIDTH + SGU_WIDTH:], POOL_WIDTH)
        da_ref[...] = da
        dgoa_ref[...] += dgoa
        dgos_ref[...] += dgos
        dgop_ref[...] += dgop
        prod = da * a_v
        ones = jnp.ones((8, HEAD_PAD), F32)
        for h in range(HEADS):
            lo = HEAD_PAD * h
            sums = lax.dot_general(ones, prod[:, lo:lo + HEAD_PAD], (((1,), (1,)), ((), ())), preferred_element_type=F32,
                                   precision=lax.Precision.HIGHEST)
            delta_ref[h] = sums[0:1, :]
        dps_ref[...] += jnp.sum(dyp * yp_pre, axis=0, keepdims=True)
        dyp_pre = (dyp * pscale_v).astype(BF16)
        dwp_ref[...] += _dot_tn(m, dyp_pre)
        dm_ref[...] = _dot_nt(dyp_pre, wp_ref[...])
        du_ref[...] = dgm * zc
        dzc = dgm * u
        masks = _head_masks((CHUNK, SGU_WIDTH), SGU_HEAD_DIM)
        lane_b = lax.broadcasted_iota(jnp.int32, (CHUNK, HEAD_PAD), 1)
        dvn_parts = []
        dwsp = jnp.zeros(dwsp_ref.shape, F32)
        db = jnp.zeros(db_ref.shape, F32)
        for c in range(tm // CHUNK):
            dz_c = dzc[CHUNK * c:CHUNK * (c + 1)]
            dzstack = jnp.concatenate([jnp.where(mk, dz_c, 0.0).astype(BF16) for mk in masks], axis=0)
            dvn_parts.append(_dot(wcatt_ref[...], dzstack))
            dwsp = dwsp + _dot_nt(dzstack, vn[CHUNK * c:CHUNK * (c + 1)].astype(BF16))
            for h, mk in enumerate(masks):
                col = jnp.sum(jnp.where(mk, dz_c, 0.0), axis=1, keepdims=True)
                db = db + jnp.where(lane_b == h, col, 0.0)
        dwsp_ref[...] += dwsp
        db_ref[...] += db
        dvs, dgsgu = _rms_bwd(vs, rv, gsgu_ref[...], jnp.concatenate(dvn_parts, axis=0), SGU_WIDTH)
        dvs_ref[...] = dvs
        dgsgu_ref[...] += dgsgu

    row = lambda w: pl.BlockSpec((tm, w), lambda i: (i, 0))
    head = pl.BlockSpec((HEADS, 1, tm), lambda i: (0, 0, i))
    acc_shapes = [(1, MLA_WIDTH), (1, SGU_WIDTH), (1, POOL_WIDTH), (1, POOL_WIDTH), (POOL_WIDTH, POOL_WIDTH),
                  (HEADS * CHUNK, CHUNK), (CHUNK, HEAD_PAD), (1, SGU_WIDTH)]
    return pl.pallas_call(
        body, grid=(T // tm,), name="mixers_bwd",
        in_specs=[row(D_MODEL), row(MLA_WIDTH)] + _z_specs(tm) + [WHOLE] * 10,
        out_specs=[row(MLA_WIDTH), head, row(256), row(256), row(256), row(D_MODEL)] + [_acc_spec(s) for s in acc_shapes],
        out_shape=[jax.ShapeDtypeStruct((T, MLA_WIDTH), F32), jax.ShapeDtypeStruct((HEADS, 1, T), F32),
                   jax.ShapeDtypeStruct((T, 256), F32), jax.ShapeDtypeStruct((T, 256), F32),
                   jax.ShapeDtypeStruct((T, 256), F32), jax.ShapeDtypeStruct((T, D_MODEL), BF16)]
                  + [jax.ShapeDtypeStruct(s, F32) for s in acc_shapes],
        compiler_params=_cparams(),
    )(dx1, a, z, z, z, z, gsgu, wcat, wcat_t, bz, wp, pscale, goa, gos, gop, wout)


def pool_bwd(dm):
    T = dm.shape[0]
    tm = min(ROW_TILE, T)
    nt = T // tm

    def body(dm_ref, next_ref, dpin_ref):
        i = pl.program_id(0)
        cnt, lane = _pool_counts(i, tm)
        dmv = dm_ref[...]
        win = _by_group(lane[:POOL_HALO], 2.0, 4.0, 8.0, 16.0)
        nxt = jnp.where(i < nt - 1, next_ref[...] / win, 0.0)
        r1 = jnp.concatenate([dmv / cnt, nxt], axis=0)
        n = tm + POOL_HALO
        r2 = r1 + pltpu.roll(r1, n - 1, 0)
        r4 = r2 + pltpu.roll(r2, n - 2, 0)
        r8 = r4 + pltpu.roll(r4, n - 4, 0)
        r16 = r8 + pltpu.roll(r8, n - 8, 0)
        dpin_ref[...] = _by_group(lane, r2[:tm], r4[:tm], r8[:tm], r16[:tm]) - dmv

    return pl.pallas_call(
        body, grid=(nt,), name="pool_bwd",
        in_specs=[pl.BlockSpec((tm, 256), lambda i: (i, 0)),
                  pl.BlockSpec((POOL_HALO, 256), lambda i: (jnp.minimum((i + 1) * (tm // POOL_HALO), T // POOL_HALO - 1), 0))],
        out_specs=pl.BlockSpec((tm, 256), lambda i: (i, 0)),
        out_shape=jax.ShapeDtypeStruct((T, 256), F32),
        compiler_params=_cparams(),
    )(dm, dm)


def attn_bwd(q, k, kt, v, do, lse, delta):
    _, T, _ = q.shape
    tb = min(ATTN_TILE, T)
    sb = min(ATTN_SUB, tb)
    ns = tb // sb
    nb = T // tb
    hb = ATTN_BWD_HEADS

    def body(q_ref, k_ref, kt_ref, v_ref, do_ref, lse_ref, delta_ref, dqt_ref, dk_ref, dv_ref, dk_s, dv_s):
        i, j = pl.program_id(1), pl.program_id(2)

        @pl.when((i == 0) & (j == 0))
        def _():
            dqt_ref[...] = jnp.zeros(dqt_ref.shape, F32)

        @pl.when(j == i)
        def _():
            dk_s[...] = jnp.zeros(dk_s.shape, F32)
            dv_s[...] = jnp.zeros(dv_s.shape, F32)

        def sub_block(h, a, b, masked):
            qa = slice(sb * a, sb * (a + 1))
            kb = slice(sb * b, sb * (b + 1))
            qv = q_ref[h, qa, :]
            dov = do_ref[qa, HEAD_PAD * h:HEAD_PAD * (h + 1)].astype(BF16)
            st = _dot_nt(k_ref[h, kb, :], qv)
            pt = jnp.exp2(st - lse_ref[h, :, qa])
            if masked:
                krow = lax.broadcasted_iota(jnp.int32, st.shape, 0)
                qcol = lax.broadcasted_iota(jnp.int32, st.shape, 1)
                pt = jnp.where(krow <= qcol, pt, 0.0)
            dv_s[h, kb, :] += _dot(pt.astype(BF16), dov)
            dpt = _dot_nt(v_ref[h, kb, :], dov)
            dst = (pt * (dpt - delta_ref[h, :, qa])).astype(BF16)
            dk_s[h, kb, :] += _dot(dst, qv)
            cols = pl.ds(pl.multiple_of(j * tb + sb * a, sb), sb)
            dqt_ref[h, :, cols] += _dot(kt_ref[h, :, kb], dst)

        @pl.when(j > i)
        def _():
            for a in range(ns):
                for h in range(hb):
                    for b in range(ns):
                        sub_block(h, a, b, False)

        @pl.when(j == i)
        def _():
            for a in range(ns):
                for h in range(hb):
                    for b in range(a + 1):
                        sub_block(h, a, b, a == b)

        @pl.when(j == nb - 1)
        def _():
            dk_ref[...] = dk_s[...] * (1.0 / LOG2E)
            dv_ref[...] = dv_s[...]

    qspec = pl.BlockSpec((hb, tb, HEAD_PAD), lambda g, i, j: (g, jnp.maximum(i, j), 0))
    kspec = pl.BlockSpec((hb, tb, HEAD_PAD), lambda g, i, j: (g, i, 0))
    ktspec = pl.BlockSpec((hb, HEAD_PAD, tb), lambda g, i, j: (g, 0, i))
    rowspec = pl.BlockSpec((hb, 1, tb), lambda g, i, j: (g, 0, jnp.maximum(i, j)))
    return pl.pallas_call(
        body, grid=(HEADS // hb, nb, nb), name="attn_bwd",
        in_specs=[qspec, kspec, ktspec, kspec, pl.BlockSpec((tb, hb * HEAD_PAD), lambda g, i, j: (jnp.maximum(i, j), g)), rowspec, rowspec],
        out_specs=[pl.BlockSpec((hb, HEAD_PAD, T), lambda g, i, j: (g, 0, 0)), kspec, kspec],
        out_shape=[jax.ShapeDtypeStruct((HEADS, HEAD_PAD, T), F32)] + [jax.ShapeDtypeStruct((HEADS, T, HEAD_PAD), F32)] * 2,
        scratch_shapes=[pltpu.VMEM((hb, tb, HEAD_PAD), F32), pltpu.VMEM((hb, tb, HEAD_PAD), F32)],
        compiler_params=_cparams(),
    )(q, k, kt, v, do, lse, delta)


def mixin_bwd(dres, x, z, rope, dqt, dk, dv, du, dvs, dpin, gmix, win, gql, wq, gkvl, wkv, gqh, gkh):
    T = x.shape[0]
    tm = min(ROW_TILE, T)
    part = min(MIXIN_BWD_PART, tm)

    def body(dres_ref, x_ref, z_ref, c_ref, sa_ref, sb_ref, dqt_ref, dk_ref, dv_ref, du_ref, dvs_ref, dpin_ref,
             gmix_ref, win_ref, gql_ref, wq_ref, gkvl_ref, wkv_ref, gqh_ref, gkh_ref,
             dx_ref, hn_ref, dz_ref, qn_ref, dqup_ref, kvn_ref, dkvup_ref,
             dgmix_ref, dgql_ref, dgkvl_ref, dgqh_ref, dgkh_ref):
        @pl.when(pl.program_id(0) == 0)
        def _():
            for r in (dgmix_ref, dgql_ref, dgkvl_ref, dgqh_ref, dgkh_ref):
                r[...] = jnp.zeros(r.shape, F32)

        lane = lax.broadcasted_iota(jnp.int32, (part, HEAD_PAD), 1)
        rope_lanes = (lane >= QK_NOPE) & (lane < QK_HEAD)
        for p in range(tm // part):
            rows = pl.ds(part * p, part)
            xv = x_ref[rows, :]
            hn, rx = _rms(xv, gmix_ref[...], D_MODEL)
            hn_ref[rows, :] = hn.astype(BF16)
            ql, kvl, kr, qn, rq, kvn, rkv, q_up, kv_up = _qkv_pre(z_ref[rows, :], gql_ref[...], wq_ref[...], gkvl_ref[...],
                                                                  wkv_ref[...])
            qn_ref[rows, :] = qn
            kvn_ref[rows, :] = kvn
            c, sa, sb = c_ref[rows, :], sa_ref[rows, :], sb_ref[rows, :]
            dkr = jnp.zeros((part, HEAD_PAD), F32)
            dgqh = jnp.zeros((1, HEAD_PAD), F32)
            dgkh = jnp.zeros((1, HEAD_PAD), F32)
            dq_parts, dk_parts, dv_parts = [], [], []
            for h in range(HEADS):
                lo = HEAD_PAD * h
                qh = q_up[:, lo:lo + HEAD_PAD]
                rqh = lax.rsqrt(jnp.sum(qh * qh, axis=-1, keepdims=True) * (1.0 / QK_HEAD) + EPS)
                dq_h = jnp.transpose(dqt_ref[h, :, rows]) * ATTN_SCALE
                dqh, dg = _rms_bwd(qh, rqh, gqh_ref[...], _rope_bwd(dq_h, c, sa, sb), QK_HEAD)
                dgqh = dgqh + dg
                dq_parts.append(dqh)
                kh = kv_up[:, lo:lo + HEAD_PAD] + kr
                rkh = lax.rsqrt(jnp.sum(kh * kh, axis=-1, keepdims=True) * (1.0 / QK_HEAD) + EPS)
                dkh, dg = _rms_bwd(kh, rkh, gkh_ref[...], _rope_bwd(dk_ref[h, rows, :], c, sa, sb), QK_HEAD)
                dgkh = dgkh + dg
                dkr = dkr + jnp.where(rope_lanes, dkh, 0.0)
                dk_parts.append(dkh)
                dv_parts.append(dv_ref[h, rows, :])
            dgqh_ref[...] += dgqh
            dgkh_ref[...] += dgkh
            dq_up = jnp.concatenate(dq_parts, axis=1).astype(BF16)
            dkv_up = jnp.concatenate(dk_parts + dv_parts, axis=1).astype(BF16)
            dqup_ref[rows, :] = dq_up
            dkvup_ref[rows, :] = dkv_up
            dql, dg = _rms_bwd(ql, rq, gql_ref[...], _dot(dq_up, wq_ref[...]), Q_LORA)
            dgql_ref[...] += dg
            dkvl, dg = _rms_bwd(kvl, rkv, gkvl_ref[...], _dot(dkv_up, wkv_ref[...]), KV_LORA)
            dgkvl_ref[...] += dg
            dz = jnp.concatenate([dql, dkvl, dkr, du_ref[rows, :], dvs_ref[rows, :], dpin_ref[rows, :]], axis=1).astype(BF16)
            dz_ref[rows, :] = dz
            dxn, dg = _rms_bwd(xv, rx, gmix_ref[...], _dot(dz, win_ref[...]), D_MODEL)
            dgmix_ref[...] += dg
            dx_ref[rows, :] = dres_ref[rows, :] + dxn

    row = lambda w: pl.BlockSpec((tm, w), lambda i: (i, 0))
    head = pl.BlockSpec((HEADS, tm, HEAD_PAD), lambda i: (0, i, 0))
    head_t = pl.BlockSpec((HEADS, HEAD_PAD, tm), lambda i: (0, 0, i))
    acc_shapes = [(1, D_MODEL), (1, Q_LORA), (1, KV_LORA), (1, HEAD_PAD), (1, HEAD_PAD)]
    out_rows = [(D_MODEL, F32), (D_MODEL, BF16), (Z_WIDTH, BF16), (Q_LORA, BF16), (HEADS * HEAD_PAD, BF16),
                (KV_LORA, BF16), (2 * HEADS * HEAD_PAD, BF16)]
    return pl.pallas_call(
        body, grid=(T // tm,), name="mixin_bwd",
        in_specs=[row(D_MODEL), row(D_MODEL), row(Z_WIDTH)] + [row(HEAD_PAD)] * 3 + [head_t, head, head, row(256), row(256), row(256)]
                 + [WHOLE] * 8,
        out_specs=[row(w) for w, _ in out_rows] + [_acc_spec(s) for s in acc_shapes],
        out_shape=[jax.ShapeDtypeStruct((T, w), dt) for w, dt in out_rows] + [jax.ShapeDtypeStruct(s, F32) for s in acc_shapes],
        compiler_params=_cparams(),
    )(dres, x, z, *rope, dqt, dk, dv, du, dvs, dpin, gmix, win, gql, wq, gkvl, wkv, gqh, gkh)


def _place():
    x, y, c = lax.axis_index("x"), lax.axis_index("y"), lax.axis_index("c")
    return x, y, c, 4 * x + 2 * y + c


def _layer_of(ref, l):
    return ref[:, l, :] if ref.shape[1] == DEPTH else ref[l]


def _layer_shape(shard):
    return (shard.shape[0], shard.shape[2]) if shard.shape[1] == DEPTH else shard.shape[1:]


def cast_shards(shards, wanted):
    n = len(shards)

    def body(*refs):
        for o_ref, (w, l) in zip(refs[n:], wanted):
            o_ref[...] = _layer_of(refs[w], l).astype(BF16)

    return pl.pallas_call(
        body, name="cast_shards", in_specs=[WHOLE] * n, out_specs=[WHOLE] * len(wanted),
        out_shape=[jax.ShapeDtypeStruct(_layer_shape(shards[w]), BF16) for w, _ in wanted],
        compiler_params=_cparams(),
    )(*shards)


def allgather_layer0(shards):
    n = len(shards)

    def body(*refs):
        ins, outs = refs[:n], refs[n:2 * n]
        stage = refs[2 * n:3 * n]
        send_sems, recv_sems, local_sems = refs[3 * n:]
        x, y, c, me = _place()
        sibling = (x, y, 1 - c)
        chips = [(1 - x, y), (x, 1 - y), (1 - x, 1 - y)]

        def copy(k, w, block_id, to, from_stage):
            return pltpu.make_async_remote_copy(
                src_ref=stage[w] if from_stage else outs[w].at[block_id], dst_ref=outs[w].at[block_id],
                send_sem=send_sems.at[k, w], recv_sem=recv_sems.at[k, w], device_id=to, device_id_type=MESH)

        def block_of(cx, cy, cc):
            return 4 * cx + 2 * cy + cc

        local, sent = [], []
        for w in range(n):
            stage[w][...] = _layer_of(ins[w], 0).astype(BF16)
            mine = pltpu.make_async_copy(stage[w], outs[w].at[me], local_sems.at[w])
            mine.start()
            local.append(mine)
            first = [copy(0, w, me, sibling, True)] + [copy(1 + j, w, me, (*chip, c), True) for j, chip in enumerate(chips)]
            for cp in first:
                cp.start()
            sent += first
        for j, chip in enumerate(chips):
            for w in range(n):
                copy(1 + j, w, block_of(*chip, c), (x, y, c), False).wait_recv()
                fwd = copy(4 + j, w, block_of(*chip, c), sibling, False)
                fwd.start()
                sent.append(fwd)
        for w in range(n):
            copy(0, w, block_of(x, y, 1 - c), (x, y, c), False).wait_recv()
            for j, chip in enumerate(chips):
                copy(4 + j, w, block_of(*chip, 1 - c), (x, y, c), False).wait_recv()
        for cp in local:
            cp.wait()
        for cp in sent:
            cp.wait_send()

    return pl.pallas_call(
        body, name="allgather_layer0",
        in_specs=[WHOLE] * n, out_specs=[ANY] * n,
        out_shape=[jax.ShapeDtypeStruct((N_DEV,) + _layer_shape(s), BF16) for s in shards],
        scratch_shapes=[pltpu.VMEM(_layer_shape(s), BF16) for s in shards]
                       + [pltpu.SemaphoreType.DMA((7, n)), pltpu.SemaphoreType.DMA((7, n)), pltpu.SemaphoreType.DMA((n,))],
        compiler_params=_cparams(),
    )(*shards)


def _peer(k):
    x, y, c, _ = _place()
    px = 1 - x if k & 4 else x
    py = 1 - y if k & 2 else y
    pc = 1 - c if k & 1 else c
    return (px, py, pc), 4 * px + 2 * py + pc


def exchange_start(srcs, after, gather, name):
    n = len(srcs)
    land_shapes = [((N_DEV,) + s.shape) if gather else s.shape for s in srcs]

    def body(*refs):
        src_refs, land_refs = refs[:n], refs[n:2 * n]
        send_sems, recv_sems = refs[2 * n + 1:3 * n + 1], refs[3 * n + 1:4 * n + 1]
        token = refs[-1]
        _, _, _, me = _place()
        for k in range(1, N_DEV):
            peer, peer_id = _peer(k)
            for w in range(n):
                pltpu.make_async_remote_copy(
                    src_ref=src_refs[w] if gather else src_refs[w].at[peer_id], dst_ref=land_refs[w].at[me],
                    send_sem=send_sems[w], recv_sem=recv_sems[w], device_id=peer, device_id_type=MESH).start()
        token[...] = jnp.zeros(token.shape, F32)

    hbm = lambda a: pltpu.with_memory_space_constraint(a, pltpu.HBM)
    outs = pl.pallas_call(
        body, name=name,
        out_shape=(pltpu.SemaphoreType.DMA(()),) * (2 * n)
                  + tuple(pltpu.HBM(s.shape, BF16) for s in srcs) + tuple(pltpu.HBM(s, BF16) for s in land_shapes)
                  + (jax.ShapeDtypeStruct((8, 128), F32),),
        in_specs=[HBM_SPEC] * (2 * n) + [ANY],
        out_specs=(SEM_SPEC,) * (2 * n) + (HBM_SPEC,) * (2 * n) + (WHOLE,),
        input_output_aliases={i: 2 * n + i for i in range(2 * n)},
        compiler_params=pltpu.CompilerParams(has_side_effects=pltpu.SideEffectType.DATAFLOW_SIDE_EFFECTING),
    )(*[hbm(s) for s in srcs], *[hbm(lax.empty(s, BF16)) for s in land_shapes], after)
    return list(outs[:n]), list(outs[n:2 * n]), list(outs[2 * n:3 * n]), list(outs[3 * n:4 * n]), outs[-1]


def exchange_wait(started, after, name):
    send_sems, recv_sems, srcs, lands, _ = started
    n = len(srcs)

    def body(*refs):
        land_refs = refs[n:2 * n]
        send_sems, recv_sems = refs[2 * n:3 * n], refs[3 * n:4 * n]
        x, y, c, _ = _place()
        for w in range(n):
            seven = land_refs[w].at[pl.ds(0, N_DEV - 1)]
            cp = pltpu.make_async_remote_copy(src_ref=seven, dst_ref=seven, send_sem=send_sems[w], recv_sem=recv_sems[w],
                                              device_id=(x, y, c), device_id_type=MESH)
            cp.wait_send()
            cp.wait_recv()

    outs = pl.pallas_call(
        body, name=name,
        out_shape=tuple(pltpu.HBM(s.shape, BF16) for s in srcs) + tuple(pltpu.HBM(l.shape, BF16) for l in lands),
        in_specs=[HBM_SPEC] * (2 * n) + [SEM_SPEC] * (2 * n) + [ANY],
        out_specs=(HBM_SPEC,) * (2 * n),
        input_output_aliases={i: i for i in range(2 * n)},
        compiler_params=pltpu.CompilerParams(has_side_effects=pltpu.SideEffectType.DATAFLOW_SIDE_EFFECTING),
    )(*srcs, *lands, *send_sems, *recv_sems, after)
    _, _, _, me = _place()
    filled = []
    for src, land in zip(outs[:n], outs[n:]):
        own = src[None] if src.ndim == 2 else lax.dynamic_slice_in_dim(src, me, 1, axis=0)
        filled.append(lax.dynamic_update_slice_in_dim(land, own, me, axis=0))
    return filled


def exchange_grads(pieces):
    n = len(pieces)

    def body(*refs):
        ins, outs = refs[:n], refs[n:2 * n]
        send_sems, recv_sems, local_sems = refs[2 * n:]
        x, y, c, me = _place()
        local = [pltpu.make_async_copy(ins[w].at[me], outs[w].at[me], local_sems.at[w]) for w in range(n)]
        for cp in local:
            cp.start()
        remote = []
        for k in range(1, N_DEV):
            px = 1 - x if k & 4 else x
            py = 1 - y if k & 2 else y
            pc = 1 - c if k & 1 else c
            peer = 4 * px + 2 * py + pc
            for w in range(n):
                cp = pltpu.make_async_remote_copy(
                    src_ref=ins[w].at[peer], dst_ref=outs[w].at[me], send_sem=send_sems.at[k - 1, w],
                    recv_sem=recv_sems.at[k - 1, w], device_id=(px, py, pc), device_id_type=MESH)
                cp.start()
                remote.append(cp)
        for cp in local:
            cp.wait()
        for cp in remote:
            cp.wait_recv()
        for cp in remote:
            cp.wait_send()

    return pl.pallas_call(
        body, name="exchange_grads",
        in_specs=[ANY] * n, out_specs=[ANY] * n,
        out_shape=[jax.ShapeDtypeStruct(p.shape, BF16) for p in pieces],
        scratch_shapes=[pltpu.SemaphoreType.DMA((7, n)), pltpu.SemaphoreType.DMA((7, n)), pltpu.SemaphoreType.DMA((n,))],
        compiler_params=_cparams(),
    )(*pieces)


def _adamw(w, g, m, v):
    m2 = ADAM_B1 * m + (1.0 - ADAM_B1) * g
    v2 = ADAM_B2 * v + (1.0 - ADAM_B2) * (g * g)
    m_hat = m2 / (1.0 - ADAM_B1 ** ADAM_STEP)
    v_hat = v2 / (1.0 - ADAM_B2 ** ADAM_STEP)
    delta = -ADAM_LR * (m_hat / (jnp.sqrt(v_hat) + ADAM_EPS) + ADAM_WD * w)
    return delta, m2, v2


def adamw_sharded(parts, w, m, v, name):
    L, R, C = w.shape
    fits = [d for d in range(16, min(R, 512) + 1, 16) if R % d == 0]
    br = max(fits) if fits else R
    nblk = R // br

    def body(*refs):
        p_refs = refs[:L]
        w_ref, m_ref, v_ref, g_ref, d_ref, m2_ref, v2_ref = refs[L:]

        def total(p_ref):
            g = p_ref[0].astype(F32)
            for s in range(1, N_DEV):
                g = g + p_ref[s].astype(F32)
            return g

        g = total(p_refs[0])
        for l in range(1, L):
            g = jnp.where(pl.program_id(0) == l, total(p_refs[l]), g)
        g_ref[...] = g
        d_ref[...], m2_ref[...], v2_ref[...] = _adamw(w_ref[...], g, m_ref[...], v_ref[...])

    blk = pl.BlockSpec((None, br, C), lambda l, i: (l, i, 0))

    def part_spec(k):
        return pl.BlockSpec((N_DEV, br, C), lambda l, i: (0, jnp.where(l == k, i, jnp.where(l < k, 0, nblk - 1)), 0))

    return pl.pallas_call(
        body, grid=(L, nblk), name=name,
        in_specs=[part_spec(k) for k in range(L)] + [blk, blk, blk],
        out_specs=[blk] * 4,
        out_shape=[jax.ShapeDtypeStruct((L, R, C), F32)] * 4,
        compiler_params=_cparams(),
    )(*parts, w, m, v)


def allreduce_adamw_small(gpart, w, m, v):
    R = gpart.shape[0]

    def body(g_ref, w_ref, m_ref, v_ref, grad_ref, d_ref, m2_ref, v2_ref, all_ref, send_sems, recv_sems):
        x, y, c, me = _place()
        sibling = (x, y, 1 - c)
        chips = [(1 - x, y), (x, 1 - y), (1 - x, 1 - y)]

        def copy(k, block_id, to, from_input):
            return pltpu.make_async_remote_copy(
                src_ref=g_ref if from_input else all_ref.at[block_id], dst_ref=all_ref.at[block_id],
                send_sem=send_sems.at[k], recv_sem=recv_sems.at[k], device_id=to, device_id_type=MESH)

        def block_of(cx, cy, cc):
            return 4 * cx + 2 * cy + cc

        all_ref[me] = g_ref[...]
        first = [copy(0, me, sibling, True)] + [copy(1 + j, me, (*chip, c), True) for j, chip in enumerate(chips)]
        for cp in first:
            cp.start()
        passed = [copy(4 + j, block_of(*chip, c), sibling, False) for j, chip in enumerate(chips)]
        for j, chip in enumerate(chips):
            copy(1 + j, block_of(*chip, c), (x, y, c), False).wait_recv()
            passed[j].start()
        copy(0, block_of(x, y, 1 - c), (x, y, c), False).wait_recv()
        for j, chip in enumerate(chips):
            copy(4 + j, block_of(*chip, 1 - c), (x, y, c), False).wait_recv()
        for cp in first + passed:
            cp.wait_send()
        g = all_ref[0]
        for s in range(1, N_DEV):
            g = g + all_ref[s]
        grad_ref[...] = g
        d_ref[...], m2_ref[...], v2_ref[...] = _adamw(w_ref[...], g, m_ref[...], v_ref[...])

    return pl.pallas_call(
        body, name="allreduce_adamw_small",
        in_specs=[WHOLE] * 4, out_specs=[WHOLE] * 4,
        out_shape=[jax.ShapeDtypeStruct((R, 128), F32)] * 4,
        scratch_shapes=[pltpu.VMEM((N_DEV, R, 128), F32), pltpu.SemaphoreType.DMA((7,)), pltpu.SemaphoreType.DMA((7,))],
        compiler_params=_cparams(),
    )(gpart, w, m, v)


def _shard_view(name, a):
    if name == "w_in":
        return a.transpose(2, 0, 1)
    return a.swapaxes(1, 2) if name in TRANSPOSED else a


def _shard_unview(name, a):
    if name == "w_in":
        return a.transpose(1, 2, 0)
    return a.swapaxes(1, 2) if name in TRANSPOSED else a


def _pad_head_rows(w, width):
    c = w.shape[1]
    return jnp.pad(w.reshape(HEADS, width, c), ((0, 0), (0, HEAD_PAD - width), (0, 0))).reshape(HEADS * HEAD_PAD, c)


def _unpad_head_rows(w, width):
    c = w.shape[1]
    return w.reshape(HEADS, HEAD_PAD, c)[:, :width].reshape(HEADS * width, c)


O1 = Q_LORA
O2 = O1 + KV_LORA
O3 = O2 + QK_ROPE


def _mixer_weights(gw):
    w_in = gw["w_in"].reshape(IN_WIDTH, D_MODEL)
    zero = lambda n: jnp.zeros((n, D_MODEL), BF16)
    win = jnp.concatenate([w_in[:O2], zero(QK_NOPE), w_in[O2:O3], zero(HEAD_PAD - QK_HEAD), w_in[O3:]], axis=0)
    wq = _pad_head_rows(gw["w_q_up"].reshape(HEADS * QK_HEAD, Q_LORA), QK_HEAD)
    w_kv = gw["w_kv_up"].reshape(HEADS, QK_NOPE + V_HEAD, KV_LORA)
    wk = jnp.pad(w_kv[:, :QK_NOPE], ((0, 0), (0, HEAD_PAD - QK_NOPE), (0, 0))).reshape(HEADS * HEAD_PAD, KV_LORA)
    wv = w_kv[:, QK_NOPE:].reshape(HEADS * V_HEAD, KV_LORA)
    wkv = jnp.concatenate([wk, wv], axis=0)
    out = dict(win=win, wq=wq, wkv=wkv)
    if "w_out" in gw:
        out["wout"] = gw["w_out"].reshape(D_MODEL, D_MODEL)
    return out


def _ffn_weights(gw):
    return dict(wg=gw["w_gate"].reshape(FFN_HIDDEN, D_MODEL), wu=gw["w_up"].reshape(FFN_HIDDEN, D_MODEL),
                wd=gw["w_down"].reshape(FFN_HIDDEN, D_MODEL))


def _layer_small(p, l):
    row = lambda a: a.reshape(1, -1)
    pad_head = lambda g: jnp.pad(g, (0, HEAD_PAD - QK_HEAD)).reshape(1, HEAD_PAD)
    tril = jnp.tril(jnp.ones((CHUNK, CHUNK), F32))
    wsp = p["w_spatial"][l] * tril
    wcat = jnp.concatenate([wsp[h] for h in range(HEADS)], axis=1).astype(BF16)
    wcat_t = jnp.concatenate([wsp[h].T for h in range(HEADS)], axis=1).astype(BF16)
    bz = jnp.repeat(p["b_spatial"][l].T, SGU_HEAD_DIM, axis=1)
    wp = jax.scipy.linalg.block_diag(*[p["w_pool"][l][g] for g in range(HEADS)]).astype(BF16)
    return dict(gmix=row(p["g_mix_norm"][l]), gql=row(p["g_q_lat"][l]), gkvl=row(p["g_kv_lat"][l]),
                gqh=pad_head(p["g_q_head"][l]), gkh=pad_head(p["g_k_head"][l]), gsgu=row(p["g_sgu_v"][l]),
                wcat=wcat, wcat_t=wcat_t, bz=bz, wp=wp, pscale=row(p["pool_scale"][l]),
                goa=row(p["g_out_mla"][l]), gos=row(p["g_out_sgu"][l]), gop=row(p["g_out_pool"][l]),
                gffn=row(p["g_ffn_norm"][l]))


MIXER_SIDE = ("w_in", "w_q_up", "w_kv_up", "w_out")
FFN_SIDE = ("w_gate", "w_up", "w_down")
TRANSPOSED = ("w_in", "w_q_up", "w_kv_up", "w_gate", "w_up")
FIRST_0 = ("w_in", "w_q_up", "w_kv_up")
LATER_0 = ("w_out",) + FFN_SIDE
SHARDED = MIXER_SIDE + FFN_SIDE
SMALL = ("g_mix_norm", "g_q_lat", "g_kv_lat", "g_q_head", "g_k_head", "g_sgu_v", "w_spatial", "b_spatial", "w_pool",
         "pool_scale", "g_out_mla", "g_out_sgu", "g_out_pool", "g_ffn_norm")
WEIGHTS = ("g_mix_norm", "w_in", "g_q_lat", "w_q_up", "g_kv_lat", "w_kv_up", "g_q_head", "g_k_head", "g_sgu_v", "w_spatial",
           "b_spatial", "w_pool", "pool_scale", "g_out_mla", "g_out_sgu", "g_out_pool", "w_out", "g_ffn_norm", "w_gate",
           "w_up", "w_down")
PACK_ROWS = 8 * 128


def _pack_small(parts):
    flat = []
    for name in SMALL:
        a = parts[name].reshape(-1)
        flat.append(jnp.pad(a, (0, -a.shape[0] % PACK_ROWS)))
    return jnp.concatenate(flat).reshape(-1, 128)


def _unpack_small(packed, like):
    out, row = {}, 0
    for name in SMALL:
        n = math.prod(like[name].shape)
        rows = -(-n // PACK_ROWS) * 8
        out[name] = packed[row:row + rows].reshape(-1)[:n].reshape(like[name].shape)
        row += rows
    return out


def _forward_attention(x, rope, W, S):
    z, q, k, v, kt, vt = mixin_fwd(x, rope, S["gmix"], W["win"], S["gql"], W["wq"], S["gkvl"], W["wkv"], S["gqh"], S["gkh"])
    a, lse = attn_fwd(q, k, vt)
    return dict(x=x, z=z, q=q, k=k, kt=kt, v=v, a=a, lse=lse)


def _forward_mixers(A, wout, S):
    A["x1"] = mixers_fwd(A["x"], A["a"], A["z"], S["gsgu"], S["wcat"], S["bz"], S["wp"], S["pscale"], S["goa"], S["gos"],
                         S["gop"], wout)
    return A["x1"]


def _backward_ffn(dx2, W, S, A, l):
    dx1, h2, act, dg, du_ffn, dgffn = ffn_bwd(dx2, A["x1"], A["gs"], A["us"], S["gffn"], W["wg"], W["wu"], W["wd"])
    d_wd = matmul_tn(act, dx2, f"dw_down_{l}")
    d_wg = matmul_tn(dg, h2, f"dw_gate_{l}")
    d_wu = matmul_tn(du_ffn, h2, f"dw_up_{l}")
    big = {n: d.reshape(N_DEV, -1, D_MODEL) for n, d in (("w_gate", d_wg), ("w_up", d_wu), ("w_down", d_wd))}
    return dx1, big, dgffn


def _backward_mixers(dx1, dgffn, rope, W, S, A, l, send_wout=None):
    (da, delta, du, dvs, dm, mix, dgoa, dgos, dgop, dps, dwp, dwsp, db, dgsgu) = mixers_bwd(
        dx1, A["a"], A["z"], S["gsgu"], S["wcat"], S["wcat_t"], S["bz"], S["wp"], S["pscale"], S["goa"], S["gos"],
        S["gop"], W["wout"])
    d_wout = matmul_tn(mix, dx1, f"dw_out_{l}")
    if send_wout is not None:
        delta = delta + send_wout(d_wout.reshape(N_DEV, -1, D_MODEL))[0, 0]
    dpin = pool_bwd(dm)
    dqt, dk, dv = attn_bwd(A["q"], A["k"], A["kt"], A["v"], da, A["lse"], delta)
    (dx, hn, dz, qn, dq_up, kvn, dkv_up, dgmix, dgql, dgkvl, dgqh, dgkh) = mixin_bwd(
        dx1, A["x"], A["z"], rope, dqt, dk, dv, du, dvs, dpin, S["gmix"], W["win"], S["gql"], W["wq"], S["gkvl"],
        W["wkv"], S["gqh"], S["gkh"])
    d_win = matmul_tn(dz, hn, f"dw_in_{l}")
    d_wq = matmul_tn(dq_up, qn, f"dw_q_up_{l}")
    d_wkv = matmul_tn(dkv_up, kvn, f"dw_kv_up_{l}")
    d_win = jnp.concatenate([d_win[:O2], d_win[O2 + QK_NOPE:O2 + QK_HEAD], d_win[O2 + HEAD_PAD:]], axis=0)
    d_wk = d_wkv[:HEADS * HEAD_PAD].reshape(HEADS, HEAD_PAD, KV_LORA)[:, :QK_NOPE]
    d_wv = d_wkv[HEADS * HEAD_PAD:].reshape(HEADS, V_HEAD, KV_LORA)
    d_wkv = jnp.concatenate([d_wk, d_wv], axis=1)
    big = dict(w_in=d_win.reshape(N_DEV, -1, D_MODEL), w_q_up=_unpad_head_rows(d_wq, QK_HEAD).reshape(N_DEV, -1, Q_LORA),
               w_kv_up=d_wkv.reshape(N_DEV, -1, KV_LORA), w_out=d_wout.reshape(N_DEV, -1, D_MODEL))
    tril = jnp.tril(jnp.ones((CHUNK, CHUNK), F32))
    small = dict(g_mix_norm=dgmix[0], g_q_lat=dgql[0], g_kv_lat=dgkvl[0], g_q_head=dgqh[0, :QK_HEAD], g_k_head=dgkh[0, :QK_HEAD],
                 g_sgu_v=dgsgu[0], w_spatial=dwsp.reshape(HEADS, CHUNK, CHUNK) * tril, b_spatial=db[:, :HEADS].T,
                 w_pool=jnp.stack([dwp[64 * g:64 * (g + 1), 64 * g:64 * (g + 1)] for g in range(HEADS)]),
                 pool_scale=dps[0], g_out_mla=dgoa[0], g_out_sgu=dgos[0], g_out_pool=dgop[0], g_ffn_norm=dgffn[0])
    return dx, big, small


def kernel(x, positions, g_mix_norm, w_in, g_q_lat, w_q_up, g_kv_lat, w_kv_up, g_q_head, g_k_head, g_sgu_v, w_spatial, b_spatial, w_pool, pool_scale, g_out_mla, g_out_sgu, g_out_pool, w_out, g_ffn_norm, w_gate, w_up, w_down, loss_target, m_g_mix_norm, m_w_in, m_g_q_lat, m_w_q_up, m_g_kv_lat, m_w_kv_up, m_g_q_head, m_g_k_head, m_g_sgu_v, m_w_spatial, m_b_spatial, m_w_pool, m_pool_scale, m_g_out_mla, m_g_out_sgu, m_g_out_pool, m_w_out, m_g_ffn_norm, m_w_gate, m_w_up, m_w_down, v_g_mix_norm, v_w_in, v_g_q_lat, v_w_q_up, v_g_kv_lat, v_w_kv_up, v_g_q_head, v_g_k_head, v_g_sgu_v, v_w_spatial, v_b_spatial, v_w_pool, v_pool_scale, v_g_out_mla, v_g_out_sgu, v_g_out_pool, v_w_out, v_g_ffn_norm, v_w_gate, v_w_up, v_w_down):
    given = dict(locals())
    w = {n: given[n] for n in WEIGHTS}
    m = {n: given["m_" + n] for n in WEIGHTS}
    v = {n: given["v_" + n] for n in WEIGHTS}
    T = x.shape[1]
    xs = x.reshape(T, D_MODEL)

    half = QK_ROPE // 2
    inv_freq = 1.0 / (ROPE_THETA ** (jnp.arange(half, dtype=F32) / half))
    ang16 = positions.reshape(T).astype(F32)[:, None] * inv_freq
    ang = jnp.concatenate([jnp.zeros((T, QK_NOPE), F32), ang16, ang16, jnp.zeros((T, HEAD_PAD - QK_HEAD), F32)], axis=1)

    wv = {n: _shard_view(n, w[n]) for n in SHARDED}
    wanted = [(SHARDED.index(n), 0) for n in LATER_0] + [(i, 1) for i in range(len(SHARDED))]
    bf = cast_shards([wv[n] for n in SHARDED], wanted)
    first0 = allgather_layer0([wv[n] for n in FIRST_0])
    ag_later0 = exchange_start(bf[:len(LATER_0)], first0[0], True, "ag_start_later0")
    ag_l1 = exchange_start(bf[len(LATER_0):], ag_later0[-1], True, "ag_start_l1")
    rope = rope_tables(ang + (ag_later0[-1][0, 0] + ag_l1[-1][0, 0]))
    Ss = [_layer_small(w, l) for l in range(DEPTH)]

    W0 = dict(zip(FIRST_0, first0))
    A0 = _forward_attention(xs, rope, _mixer_weights(W0), Ss[0])
    W0.update(zip(LATER_0, exchange_wait(ag_later0, A0["a"], "ag_wait_later0")))
    W0 = {**_mixer_weights(W0), **_ffn_weights(W0)}
    x1 = _forward_mixers(A0, W0["wout"], Ss[0])
    h, A0["gs"], A0["us"] = ffn_fwd(x1, Ss[0]["gffn"], W0["wg"], W0["wu"], W0["wd"])
    layer1 = dict(zip(SHARDED, exchange_wait(ag_l1, h, "ag_wait_l1")))
    W1 = {**_mixer_weights(layer1), **_ffn_weights(layer1)}
    A1 = _forward_attention(h, rope, W1, Ss[1])
    x1 = _forward_mixers(A1, W1["wout"], Ss[1])
    dh, loss_part, A1["gs"], A1["us"] = ffn_fwd(x1, Ss[1]["gffn"], W1["wg"], W1["wu"], W1["wd"],
                                                target=loss_target.reshape(T, D_MODEL))
    loss = lax.psum(loss_part[0, 0], ("x", "y", "c"))

    dx1, big_f1, dgffn1 = _backward_ffn(dh, W1, Ss[1], A1, 1)
    dh, big_m1, small1 = _backward_mixers(dx1, dgffn1, rope, W1, Ss[1], A1, 1)
    rs_l1 = exchange_start([{**big_m1, **big_f1}[n] for n in SHARDED], dh, False, "rs_start_l1")
    S0 = dict(Ss[0], gffn=Ss[0]["gffn"] + rs_l1[-1][0, 0])
    dx1, big_f0, dgffn0 = _backward_ffn(dh, W0, S0, A0, 0)
    rs_ffn0 = exchange_start([big_f0[n] for n in FFN_SIDE], dx1, False, "rs_start_ffn0")
    S0 = dict(Ss[0], gsgu=Ss[0]["gsgu"] + rs_ffn0[-1][0, 0])
    rs_out0 = []

    def send_wout(pieces):
        rs_out0.append(exchange_start([pieces], dx1, False, "rs_start_out0"))
        return rs_out0[0][-1]

    dh, big_m0, small0 = _backward_mixers(dx1, dgffn0, rope, W0, S0, A0, 0, send_wout)
    grad_x = dh.reshape(x.shape)
    smalls = [small0, small1]

    parts1 = dict(zip(SHARDED, exchange_wait(rs_l1, dh, "rs_wait_l1")))
    parts0 = dict(zip(FFN_SIDE, exchange_wait(rs_ffn0, dh, "rs_wait_ffn0")))
    parts0["w_out"] = exchange_wait(rs_out0[0], dh, "rs_wait_out0")[0]
    parts0.update(zip(FIRST_0, exchange_grads([big_m0[n] for n in FIRST_0])))
    grad, delta, new_m, new_v = {}, {}, {}, {}
    for n in SHARDED:
        parts, views = [parts0[n], parts1[n]], [_shard_view(n, a[n]) for a in (w, m, v)]
        if n == "w_in":
            parts = [jnp.stack(parts, axis=2).reshape(N_DEV, -1, D_MODEL)]
            views = [a.reshape(1, -1, D_MODEL) for a in views]
        outs = adamw_sharded(parts, *views, f"adamw_{n}")
        if n == "w_in":
            outs = [o.reshape(-1, DEPTH, D_MODEL) for o in outs]
        grad[n], delta[n], new_m[n], new_v[n] = [_shard_unview(n, o) for o in outs]

    small_part = {n: jnp.stack([smalls[l][n] for l in range(DEPTH)]) for n in SMALL}
    outs = allreduce_adamw_small(_pack_small(small_part), _pack_small(w), _pack_small(m), _pack_small(v))
    for d, o in zip((grad, delta, new_m, new_v), outs):
        d.update(_unpack_small(o, w))

    return (loss, grad_x, *[grad[n] for n in WEIGHTS], *[delta[n] for n in WEIGHTS], *[new_m[n] for n in WEIGHTS],
            *[new_v[n] for n in WEIGHTS])
```

```python
import functools
import math

import jax
import jax.numpy as jnp
from jax import lax
from jax.experimental import pallas as pl
from jax.experimental.pallas import tpu as pltpu

F32 = jnp.float32
BF16 = jnp.bfloat16

N_DEV = 8
DEPTH = 2
D_MODEL = 1024
HEADS = 4
HEAD_PAD = 128
QK_NOPE = 64
QK_ROPE = 32
QK_HEAD = QK_NOPE + QK_ROPE
V_HEAD = 128
Q_LORA = 256
KV_LORA = 128
SGU_WIDTH = 256
SGU_HEAD_DIM = 64
CHUNK = 128
POOL_WIDTH = 256
POOL_WINDOWS = (2, 4, 8, 16)
POOL_HALO = 16
MLA_WIDTH = 512
IN_WIDTH = 1184
Z_WIDTH = 1280
FFN_HIDDEN = 2816
FFN_CHUNK = 256
ROPE_THETA = 10000.0
EPS = 1e-6
ATTN_SCALE = 1.0 / math.sqrt(QK_HEAD)
LOG2E = 1.4426950408889634
NEG_BIG = -1e30

ADAM_LR = 0.001
ADAM_B1 = 0.9
ADAM_B2 = 0.999
ADAM_EPS = 1e-08
ADAM_WD = 0.01
ADAM_STEP = 10

VMEM_LIMIT = 56 * 1024 * 1024
ROW_TILE = 512
MIXIN_PART = 256
MIXIN_BWD_PART = 128
ATTN_TILE = 1024
ATTN_SUB = 512
ATTN_FWD_SUB = 512
ATTN_BWD_HEADS = 2
MESH = pl.DeviceIdType.MESH

WHOLE = pl.BlockSpec(memory_space=pltpu.VMEM)
ANY = pl.BlockSpec(memory_space=pl.ANY)
HBM_SPEC = pl.BlockSpec(memory_space=pltpu.HBM)
SEM_SPEC = pl.BlockSpec(memory_space=pltpu.SEMAPHORE)


def _cparams(**kw):
    return pltpu.CompilerParams(vmem_limit_bytes=VMEM_LIMIT, **kw)


def _dot(a, b):
    return jnp.dot(a, b, preferred_element_type=F32)


def _dot_nt(a, b):
    return lax.dot_general(a, b, (((1,), (1,)), ((), ())), preferred_element_type=F32)


def _dot_tn(a, b):
    return lax.dot_general(a, b, (((0,), (0,)), ((), ())), preferred_element_type=F32)


def _rms(x, g, n):
    r = lax.rsqrt(jnp.sum(x * x, axis=-1, keepdims=True) * (1.0 / n) + EPS)
    return x * r * g, r


def _rms_bwd(x, r, g, dy, n):
    gdy = dy * g
    dx = r * gdy - x * (r * r * r) * (jnp.sum(x * gdy, axis=-1, keepdims=True) * (1.0 / n))
    dg = jnp.sum(dy * (x * r), axis=0, keepdims=True)
    return dx, dg


def _sigmoid(x):
    return 1.0 / (1.0 + jnp.exp(-x))


def rope_tables(ang):
    T = ang.shape[0]
    tm = min(ROW_TILE, T)

    def body(ang_ref, c_ref, sa_ref, sb_ref):
        a = ang_ref[...]
        lane = lax.broadcasted_iota(jnp.int32, a.shape, 1)
        s = jnp.sin(a)
        c_ref[...] = jnp.cos(a)
        sa_ref[...] = jnp.where(lane < QK_NOPE + QK_ROPE // 2, -s, 0.0)
        sb_ref[...] = jnp.where(lane >= QK_NOPE + QK_ROPE // 2, s, 0.0)

    row = pl.BlockSpec((tm, HEAD_PAD), lambda i: (i, 0))
    return pl.pallas_call(
        body, grid=(T // tm,), name="rope_tables", in_specs=[row], out_specs=[row] * 3,
        out_shape=[jax.ShapeDtypeStruct((T, HEAD_PAD), F32)] * 3, compiler_params=_cparams(),
    )(ang)


def _rope(x, c, sa, sb):
    half = QK_ROPE // 2
    return x * c + pltpu.roll(x, HEAD_PAD - half, 1) * sa + pltpu.roll(x, half, 1) * sb


def _rope_bwd(dy, c, sa, sb):
    half = QK_ROPE // 2
    return dy * c + pltpu.roll(dy * sa, half, 1) + pltpu.roll(dy * sb, HEAD_PAD - half, 1)


def _skewed(stages, groups):
    for t in range(len(stages) + len(groups) - 1):
        for p, g in enumerate(groups):
            if 0 <= t - p < len(stages):
                stages[t - p](g)


def _head_masks(shape, width):
    lane = lax.broadcasted_iota(jnp.int32, shape, len(shape) - 1)
    return [(lane >= width * h) & (lane < width * (h + 1)) for h in range(HEADS)]


def _qkv_pre(z, gql, wq, gkvl, wkv):
    ql = z[:, 0:Q_LORA]
    kvl = z[:, Q_LORA:Q_LORA + KV_LORA]
    kr = z[:, Q_LORA + KV_LORA:Q_LORA + KV_LORA + HEAD_PAD]
    qn, rq = _rms(ql, gql, Q_LORA)
    kvn, rkv = _rms(kvl, gkvl, KV_LORA)
    qn = qn.astype(BF16)
    kvn = kvn.astype(BF16)
    q_up = _dot_nt(qn, wq)
    kv_up = _dot_nt(kvn, wkv)
    return ql, kvl, kr, qn, rq, kvn, rkv, q_up, kv_up


def mixin_fwd(x, rope, gmix, win, gql, wq, gkvl, wkv, gqh, gkh):
    T = x.shape[0]
    tm = min(2 * ROW_TILE, T)
    part = min(MIXIN_PART, tm)

    def body(x_ref, c_ref, sa_ref, sb_ref, gmix_ref, win_ref, gql_ref, wq_ref, gkvl_ref, wkv_ref, gqh_ref, gkh_ref,
             z_ref, q_ref, k_ref, v_ref, kt_ref, vt_ref):
        def project(g):
            hn = _rms(x_ref[g["rows"], :], gmix_ref[...], D_MODEL)[0].astype(BF16)
            g["z"] = _dot_nt(hn, win_ref[...])
            z_ref[g["rows"], :] = g["z"]

        def latents(g):
            _, _, g["kr"], _, _, _, _, g["q_up"], g["kv_up"] = _qkv_pre(g["z"], gql_ref[...], wq_ref[...], gkvl_ref[...],
                                                                        wkv_ref[...])

        def heads(g):
            rows = g["rows"]
            c, sa, sb = c_ref[rows, :], sa_ref[rows, :], sb_ref[rows, :]
            for h in range(HEADS):
                lo = HEAD_PAD * h
                qh = _rms(g["q_up"][:, lo:lo + HEAD_PAD], gqh_ref[...], QK_HEAD)[0]
                q_ref[h, rows, :] = (_rope(qh, c, sa, sb) * (ATTN_SCALE * LOG2E)).astype(BF16)
                kh = _rope(_rms(g["kv_up"][:, lo:lo + HEAD_PAD] + g["kr"], gkh_ref[...], QK_HEAD)[0], c, sa, sb)
                k_ref[h, rows, :] = kh.astype(BF16)
                kt_ref[h, :, rows] = jnp.transpose(kh).astype(BF16)
                vh = g["kv_up"][:, HEADS * HEAD_PAD + lo:HEADS * HEAD_PAD + lo + HEAD_PAD]
                v_ref[h, rows, :] = vh.astype(BF16)
                vt_ref[h, :, rows] = jnp.transpose(vh).astype(BF16)

        _skewed((project, latents, heads), [dict(rows=pl.ds(part * p, part)) for p in range(tm // part)])

    row = lambda w: pl.BlockSpec((tm, w), lambda i: (i, 0))
    head = pl.BlockSpec((HEADS, tm, HEAD_PAD), lambda i: (0, i, 0))
    head_t = pl.BlockSpec((HEADS, HEAD_PAD, tm), lambda i: (0, 0, i))
    return pl.pallas_call(
        body, grid=(T // tm,), name="mixin_fwd",
        in_specs=[row(D_MODEL)] + [row(HEAD_PAD)] * 3 + [WHOLE] * 8,
        out_specs=[row(Z_WIDTH), head, head, head, head_t, head_t],
        out_shape=[jax.ShapeDtypeStruct((T, Z_WIDTH), F32)] + [jax.ShapeDtypeStruct((HEADS, T, HEAD_PAD), BF16)] * 3
                  + [jax.ShapeDtypeStruct((HEADS, HEAD_PAD, T), BF16)] * 2,
        compiler_params=_cparams(),
    )(x, *rope, gmix, win, gql, wq, gkvl, wkv, gqh, gkh)


def attn_fwd(q, k, vt):
    _, T, _ = q.shape
    tb = min(ATTN_TILE, T)
    sq = min(ATTN_FWD_SUB, tb)
    nb = T // tb

    def body(q_ref, k_ref, vt_ref, o_ref, lse_ref, m_s, l_s, acc_s):
        i, j = pl.program_id(0), pl.program_id(1)

        @pl.when(j == 0)
        def _():
            m_s[...] = jnp.full(m_s.shape, -jnp.inf, F32)
            l_s[...] = jnp.zeros(l_s.shape, F32)
            acc_s[...] = jnp.zeros(acc_s.shape, F32)

        def scores(g):
            st = _dot_nt(k_ref[g["h"]], q_ref[g["h"], g["qa"], :])
            if g["masked"]:
                krow = lax.broadcasted_iota(jnp.int32, st.shape, 0)
                qcol = g["q0"] + lax.broadcasted_iota(jnp.int32, st.shape, 1)
                st = jnp.where(krow <= qcol, st, NEG_BIG)
            g["st"] = st

        def new_max(g):
            h, qa = g["h"], g["qa"]
            m_prev = m_s[h, :, qa]
            g["m"] = jnp.maximum(m_prev, jnp.max(g["st"], axis=0, keepdims=True))
            g["alpha"] = jnp.exp2(m_prev - g["m"])
            m_s[h, :, qa] = g["m"]

        def weights(g):
            pt = jnp.exp2(g["st"] - g["m"])
            g["lsum"] = jnp.sum(pt, axis=0, keepdims=True)
            g["pt"] = pt.astype(BF16)

        def accumulate(g):
            h, qa = g["h"], g["qa"]
            l_s[h, :, qa] = g["alpha"] * l_s[h, :, qa] + g["lsum"]
            acc_s[h, :, qa] = g["alpha"] * acc_s[h, :, qa] + _dot(vt_ref[h], g["pt"])

        def tiles(masked):
            return [dict(h=h, q0=q0, qa=slice(q0, q0 + sq), masked=masked) for h in range(HEADS) for q0 in range(0, tb, sq)]

        @pl.when(j < i)
        def _():
            _skewed((scores, new_max, weights, accumulate), tiles(False))

        @pl.when(j == i)
        def _():
            _skewed((scores, new_max, weights, accumulate), tiles(True))
            for h in range(HEADS):
                l = l_s[h]
                o_ref[:, HEAD_PAD * h:HEAD_PAD * (h + 1)] = jnp.transpose(acc_s[h] / l)
                lse_ref[h] = m_s[h] + jnp.log2(l)

    qspec = pl.BlockSpec((HEADS, tb, HEAD_PAD), lambda i, j: (0, i, 0))
    kspec = pl.BlockSpec((HEADS, tb, HEAD_PAD), lambda i, j: (0, jnp.minimum(i, j), 0))
    vspec = pl.BlockSpec((HEADS, HEAD_PAD, tb), lambda i, j: (0, 0, jnp.minimum(i, j)))
    return pl.pallas_call(
        body, grid=(nb, nb), name="attn_fwd",
        in_specs=[qspec, kspec, vspec],
        out_specs=[pl.BlockSpec((tb, MLA_WIDTH), lambda i, j: (i, 0)), pl.BlockSpec((HEADS, 1, tb), lambda i, j: (0, 0, i))],
        out_shape=[jax.ShapeDtypeStruct((T, MLA_WIDTH), F32), jax.ShapeDtypeStruct((HEADS, 1, T), F32)],
        scratch_shapes=[pltpu.VMEM((HEADS, 1, tb), F32), pltpu.VMEM((HEADS, 1, tb), F32), pltpu.VMEM((HEADS, HEAD_PAD, tb), F32)],
        compiler_params=_cparams(),
    )(q, k, vt)


def _sgu_fwd_chunk(vn_c, wcat, bz, masks):
    vstack = jnp.concatenate([jnp.where(mk, vn_c, 0.0).astype(BF16) for mk in masks], axis=0)
    return _dot(wcat, vstack) + bz


def _pool_counts(i, tm):
    pos1 = (i * tm + 1 + lax.broadcasted_iota(jnp.int32, (tm, POOL_WIDTH), 0)).astype(F32)
    lane = lax.broadcasted_iota(jnp.int32, (tm, POOL_WIDTH), 1)
    win = jnp.where(lane < 64, 2.0, jnp.where(lane < 128, 4.0, jnp.where(lane < 192, 8.0, 16.0)))
    return jnp.minimum(pos1, win), lane


def _by_group(lane, s2, s4, s8, s16):
    return jnp.where(lane < 64, s2, jnp.where(lane < 128, s4, jnp.where(lane < 192, s8, s16)))


def _pool_means(pin, halo, i, tm):
    s1 = jnp.concatenate([halo, pin], axis=0)
    s2 = s1 + pltpu.roll(s1, 1, 0)
    s4 = s2 + pltpu.roll(s2, 2, 0)
    s8 = s4 + pltpu.roll(s4, 4, 0)
    s16 = s8 + pltpu.roll(s8, 8, 0)
    cnt, lane = _pool_counts(i, tm)
    sel = _by_group(lane, s2[POOL_HALO:], s4[POOL_HALO:], s8[POOL_HALO:], s16[POOL_HALO:])
    return sel / cnt - pin


def _mixers_fwd_tile(i, tm, a, u, vs, pin, halo, gsgu, wcat, bz, wp, pscale, goa, gos, gop):
    vn, rv = _rms(vs, gsgu, SGU_WIDTH)
    masks = _head_masks((CHUNK, SGU_WIDTH), SGU_HEAD_DIM)
    zc = jnp.concatenate([_sgu_fwd_chunk(vn[CHUNK * c:CHUNK * (c + 1)], wcat, bz, masks) for c in range(tm // CHUNK)], axis=0)
    gm = u * zc
    halo = jnp.where(i > 0, halo, 0.0)
    m = _pool_means(pin, halo, i, tm).astype(BF16)
    yp_pre = _dot(m, wp)
    yp = yp_pre * pscale
    na, ra = _rms(a, goa, MLA_WIDTH)
    ng, rg = _rms(gm, gos, SGU_WIDTH)
    npo, rp = _rms(yp, gop, POOL_WIDTH)
    mix = jnp.concatenate([na, ng, npo], axis=1).astype(BF16)
    return vn, rv, zc, gm, m, yp_pre, yp, ra, rg, rp, mix


def _z_specs(tm):
    col = lambda c: pl.BlockSpec((tm, 256), lambda i: (i, c))
    halo = pl.BlockSpec((POOL_HALO, 256), lambda i: (jnp.maximum(i * (tm // POOL_HALO) - 1, 0), 4))
    return [col(2), col(3), col(4), halo]


def mixers_fwd(x, a, z, gsgu, wcat, bz, wp, pscale, goa, gos, gop, wout):
    T = x.shape[0]
    tm = min(ROW_TILE, T)

    def body(x_ref, a_ref, u_ref, vs_ref, pin_ref, halo_ref, gsgu_ref, wcat_ref, bz_ref, wp_ref, ps_ref,
             goa_ref, gos_ref, gop_ref, wout_ref, x1_ref):
        i = pl.program_id(0)
        mix = _mixers_fwd_tile(i, tm, a_ref[...], u_ref[...], vs_ref[...], pin_ref[...], halo_ref[...], gsgu_ref[...],
                               wcat_ref[...], bz_ref[...], wp_ref[...], ps_ref[...], goa_ref[...], gos_ref[...],
                               gop_ref[...])[-1]
        x1_ref[...] = x_ref[...] + _dot(mix, wout_ref[...])

    row = lambda w: pl.BlockSpec((tm, w), lambda i: (i, 0))
    return pl.pallas_call(
        body, grid=(T // tm,), name="mixers_fwd",
        in_specs=[row(D_MODEL), row(MLA_WIDTH)] + _z_specs(tm) + [WHOLE] * 9,
        out_specs=row(D_MODEL),
        out_shape=jax.ShapeDtypeStruct((T, D_MODEL), F32),
        compiler_params=_cparams(),
    )(x, a, z, z, z, z, gsgu, wcat, bz, wp, pscale, goa, gos, gop, wout)


def ffn_fwd(x1, gffn, wg, wu, wd, target=None):
    T = x1.shape[0]
    tm = min(ROW_TILE, T)
    with_loss = target is not None

    def body(*refs):
        x1_ref, gffn_ref, wg_ref, wu_ref, wd_ref = refs[:5]
        outs = refs[6:] if with_loss else refs[5:]
        x1v = x1_ref[...]
        h2 = _rms(x1v, gffn_ref[...], D_MODEL)[0].astype(BF16)
        acc = x1v
        for c in range(FFN_HIDDEN // FFN_CHUNK):
            sl = slice(FFN_CHUNK * c, FFN_CHUNK * (c + 1))
            g = _dot_nt(h2, wg_ref[sl, :])
            u = _dot_nt(h2, wu_ref[sl, :])
            outs[-2][:, sl] = g.astype(BF16)
            outs[-1][:, sl] = u.astype(BF16)
            act = (g * _sigmoid(g) * u).astype(BF16)
            acc = acc + _dot(act, wd_ref[sl, :])
        if not with_loss:
            outs[0][...] = acc
            return
        dy_ref, loss_ref = outs[0], outs[1]

        @pl.when(pl.program_id(0) == 0)
        def _():
            loss_ref[...] = jnp.zeros(loss_ref.shape, F32)

        err = acc - refs[5][...]
        dy_ref[...] = err * (1.0 / D_MODEL)
        per_row = jnp.sum(err * err, axis=1, keepdims=True) * (1.0 / D_MODEL)
        loss_ref[...] += 0.5 * jnp.sum(per_row, axis=0, keepdims=True)

    row = lambda w: pl.BlockSpec((tm, w), lambda i: (i, 0))
    hidden = [jax.ShapeDtypeStruct((T, FFN_HIDDEN), BF16)] * 2
    if with_loss:
        return pl.pallas_call(
            body, grid=(T // tm,), name="ffn_fwd_loss",
            in_specs=[row(D_MODEL)] + [WHOLE] * 4 + [row(D_MODEL)],
            out_specs=[row(D_MODEL), pl.BlockSpec((1, 1), lambda i: (0, 0)), row(FFN_HIDDEN), row(FFN_HIDDEN)],
            out_shape=[jax.ShapeDtypeStruct((T, D_MODEL), F32), jax.ShapeDtypeStruct((1, 1), F32)] + hidden,
            compiler_params=_cparams(),
        )(x1, gffn, wg, wu, wd, target)
    return pl.pallas_call(
        body, grid=(T // tm,), name="ffn_fwd",
        in_specs=[row(D_MODEL)] + [WHOLE] * 4,
        out_specs=[row(D_MODEL), row(FFN_HIDDEN), row(FFN_HIDDEN)],
        out_shape=[jax.ShapeDtypeStruct((T, D_MODEL), F32)] + hidden,
        compiler_params=_cparams(),
    )(x1, gffn, wg, wu, wd)


def _acc_spec(shape):
    return pl.BlockSpec(shape, lambda i: (0,) * len(shape))


def ffn_bwd(dx2, x1, gs, us, gffn, wg, wu, wd):
    T = x1.shape[0]
    tm = min(ROW_TILE // 2, T)

    def body(dx2_ref, x1_ref, gs_ref, us_ref, gffn_ref, wg_ref, wu_ref, wd_ref,
             dx1_ref, h2_ref, act_ref, dg_ref, du_ref, dgffn_ref):
        @pl.when(pl.program_id(0) == 0)
        def _():
            dgffn_ref[...] = jnp.zeros(dgffn_ref.shape, F32)

        dx2v = dx2_ref[...]
        dy = dx2v.astype(BF16)
        x1v = x1_ref[...]
        h2, r = _rms(x1v, gffn_ref[...], D_MODEL)
        h2_ref[...] = h2.astype(BF16)
        for c in range(FFN_HIDDEN // FFN_CHUNK):
            sl = slice(FFN_CHUNK * c, FFN_CHUNK * (c + 1))
            g = gs_ref[:, sl].astype(F32)
            u = us_ref[:, sl].astype(F32)
            dact = _dot_nt(dy, wd_ref[sl, :])
            sg = _sigmoid(g)
            silu = g * sg
            act_ref[:, sl] = (silu * u).astype(BF16)
            dg_ref[:, sl] = (dact * u * (sg * (1.0 + g * (1.0 - sg)))).astype(BF16)
            du_ref[:, sl] = (dact * silu).astype(BF16)
        dh2 = _dot(dg_ref[...], wg_ref[...]) + _dot(du_ref[...], wu_ref[...])
        dxn, dgn = _rms_bwd(x1v, r, gffn_ref[...], dh2, D_MODEL)
        dx1_ref[...] = dx2v + dxn
        dgffn_ref[...] += dgn

    row = lambda w: pl.BlockSpec((tm, w), lambda i: (i, 0))
    return pl.pallas_call(
        body, grid=(T // tm,), name="ffn_bwd",
        in_specs=[row(D_MODEL), row(D_MODEL), row(FFN_HIDDEN), row(FFN_HIDDEN)] + [WHOLE] * 4,
        out_specs=[row(D_MODEL), row(D_MODEL), row(FFN_HIDDEN), row(FFN_HIDDEN), row(FFN_HIDDEN), _acc_spec((1, D_MODEL))],
        out_shape=[jax.ShapeDtypeStruct((T, D_MODEL), F32), jax.ShapeDtypeStruct((T, D_MODEL), BF16),
                   jax.ShapeDtypeStruct((T, FFN_HIDDEN), BF16), jax.ShapeDtypeStruct((T, FFN_HIDDEN), BF16),
                   jax.ShapeDtypeStruct((T, FFN_HIDDEN), BF16), jax.ShapeDtypeStruct((1, D_MODEL), F32)],
        compiler_params=_cparams(),
    )(dx2, x1, gs, us, gffn, wg, wu, wd)


TN_K_TILE = 2048
TN_ACC_BYTES = 6 * 1024 * 1024


def matmul_tn(a, b, name):
    T, M = a.shape
    N = b.shape[1]
    tk = min(TN_K_TILE, T)
    tm = M if M <= 1024 else M // 2
    tn = max(d for d in range(128, N + 1, 128) if N % d == 0 and tm * d * 4 <= TN_ACC_BYTES)
    nk = T // tk

    def body(a_ref, b_ref, o_ref, acc):
        k = pl.program_id(2)

        @pl.when(k == 0)
        def _():
            acc[...] = jnp.zeros(acc.shape, F32)

        acc[...] += _dot_tn(a_ref[...].astype(BF16), b_ref[...].astype(BF16))

        @pl.when(k == nk - 1)
        def _():
            o_ref[...] = acc[...].astype(BF16)

    return pl.pallas_call(
        body, grid=(M // tm, N // tn, nk), name=name,
        in_specs=[pl.BlockSpec((tk, tm), lambda i, j, k: (k, i)), pl.BlockSpec((tk, tn), lambda i, j, k: (k, j))],
        out_specs=pl.BlockSpec((tm, tn), lambda i, j, k: (i, j)),
        out_shape=jax.ShapeDtypeStruct((M, N), BF16),
        scratch_shapes=[pltpu.VMEM((tm, tn), F32)],
        compiler_params=_cparams(),
    )(a, b)


def mixers_bwd(dx1, a, z, gsgu, wcat, wcat_t, bz, wp, pscale, goa, gos, gop, wout):
    T = a.shape[0]
    tm = min(ROW_TILE, T)

    def body(dx1_ref, a_ref, u_ref, vs_ref, pin_ref, halo_ref, gsgu_ref, wcat_ref, wcatt_ref, bz_ref, wp_ref,
             ps_ref, goa_ref, gos_ref, gop_ref, wout_ref,
             da_ref, delta_ref, du_ref, dvs_ref, dm_ref, mix_ref,
             dgoa_ref, dgos_ref, dgop_ref, dps_ref, dwp_ref, dwsp_ref, db_ref, dgsgu_ref):
        i = pl.program_id(0)

        @pl.when(i == 0)
        def _():
            for r in (dgoa_ref, dgos_ref, dgop_ref, dps_ref, dwp_ref, dwsp_ref, db_ref, dgsgu_ref):
                r[...] = jnp.zeros(r.shape, F32)

        a_v, u, vs = a_ref[...], u_ref[...], vs_ref[...]
        goa, gos, gop, pscale_v = goa_ref[...], gos_ref[...], gop_ref[...], ps_ref[...]
        vn, rv, zc, gm, m, yp_pre, yp, ra, rg, rp, mix = _mixers_fwd_tile(
            i, tm, a_v, u, vs, pin_ref[...], halo_ref[...], gsgu_ref[...], wcat_ref[...], bz_ref[...], wp_ref[...],
            pscale_v, goa, gos, gop)
        mix_ref[...] = mix
        dmix = _dot_nt(dx1_ref[...].astype(BF16), wout_ref[...])
        da, dgoa = _rms_bwd(a_v, ra, goa, dmix[:, :MLA_WIDTH], MLA_WIDTH)
        dgm, dgos = _rms_bwd(gm, rg, gos, dmix[:, MLA_WIDTH:MLA_WIDTH + SGU_WIDTH], SGU_WIDTH)
        dyp, dgop = _rms_bwd(yp, rp, gop, dmix[:, MLA_WIDTH + SGU_WIDTH:], POOL_WIDTH)
        da_ref[...] = da
        dgoa_ref[...] += dgoa
        dgos_ref[...] += dgos
        dgop_ref[...] += dgop
        prod = da * a_v
        ones = jnp.ones((8, HEAD_PAD), F32)
        for h in range(HEADS):
            lo = HEAD_PAD * h
            sums = lax.dot_general(ones, prod[:, lo:lo + HEAD_PAD], (((1,), (1,)), ((), ())), preferred_element_type=F32,
                                   precision=lax.Precision.HIGHEST)
            delta_ref[h] = sums[0:1, :]
        dps_ref[...] += jnp.sum(dyp * yp_pre, axis=0, keepdims=True)
        dyp_pre = (dyp * pscale_v).astype(BF16)
        dwp_ref[...] += _dot_tn(m, dyp_pre)
        dm_ref[...] = _dot_nt(dyp_pre, wp_ref[...])
        du_ref[...] = dgm * zc
        dzc = dgm * u
        masks = _head_masks((CHUNK, SGU_WIDTH), SGU_HEAD_DIM)
        lane_b = lax.broadcasted_iota(jnp.int32, (CHUNK, HEAD_PAD), 1)
        dvn_parts = []
        dwsp = jnp.zeros(dwsp_ref.shape, F32)
        db = jnp.zeros(db_ref.shape, F32)
        for c in range(tm // CHUNK):
            dz_c = dzc[CHUNK * c:CHUNK * (c + 1)]
            dzstack = jnp.concatenate([jnp.where(mk, dz_c, 0.0).astype(BF16) for mk in masks], axis=0)
            dvn_parts.append(_dot(wcatt_ref[...], dzstack))
            dwsp = dwsp + _dot_nt(dzstack, vn[CHUNK * c:CHUNK * (c + 1)].astype(BF16))
            for h, mk in enumerate(masks):
                col = jnp.sum(jnp.where(mk, dz_c, 0.0), axis=1, keepdims=True)
                db = db + jnp.where(lane_b == h, col, 0.0)
        dwsp_ref[...] += dwsp
        db_ref[...] += db
        dvs, dgsgu = _rms_bwd(vs, rv, gsgu_ref[...], jnp.concatenate(dvn_parts, axis=0), SGU_WIDTH)
        dvs_ref[...] = dvs
        dgsgu_ref[...] += dgsgu

    row = lambda w: pl.BlockSpec((tm, w), lambda i: (i, 0))
    head = pl.BlockSpec((HEADS, 1, tm), lambda i: (0, 0, i))
    acc_shapes = [(1, MLA_WIDTH), (1, SGU_WIDTH), (1, POOL_WIDTH), (1, POOL_WIDTH), (POOL_WIDTH, POOL_WIDTH),
                  (HEADS * CHUNK, CHUNK), (CHUNK, HEAD_PAD), (1, SGU_WIDTH)]
    return pl.pallas_call(
        body, grid=(T // tm,), name="mixers_bwd",
        in_specs=[row(D_MODEL), row(MLA_WIDTH)] + _z_specs(tm) + [WHOLE] * 10,
        out_specs=[row(MLA_WIDTH), head, row(256), row(256), row(256), row(D_MODEL)] + [_acc_spec(s) for s in acc_shapes],
        out_shape=[jax.ShapeDtypeStruct((T, MLA_WIDTH), F32), jax.ShapeDtypeStruct((HEADS, 1, T), F32),
                   jax.ShapeDtypeStruct((T, 256), F32), jax.ShapeDtypeStruct((T, 256), F32),
                   jax.ShapeDtypeStruct((T, 256), F32), jax.ShapeDtypeStruct((T, D_MODEL), BF16)]
                  + [jax.ShapeDtypeStruct(s, F32) for s in acc_shapes],
        compiler_params=_cparams(),
    )(dx1, a, z, z, z, z, gsgu, wcat, wcat_t, bz, wp, pscale, goa, gos, gop, wout)


def pool_bwd(dm):
    T = dm.shape[0]
    tm = min(ROW_TILE, T)
    nt = T // tm

    def body(dm_ref, next_ref, dpin_ref):
        i = pl.program_id(0)
        cnt, lane = _pool_counts(i, tm)
        dmv = dm_ref[...]
        win = _by_group(lane[:POOL_HALO], 2.0, 4.0, 8.0, 16.0)
        nxt = jnp.where(i < nt - 1, next_ref[...] / win, 0.0)
        r1 = jnp.concatenate([dmv / cnt, nxt], axis=0)
        n = tm + POOL_HALO
        r2 = r1 + pltpu.roll(r1, n - 1, 0)
        r4 = r2 + pltpu.roll(r2, n - 2, 0)
        r8 = r4 + pltpu.roll(r4, n - 4, 0)
        r16 = r8 + pltpu.roll(r8, n - 8, 0)
        dpin_ref[...] = _by_group(lane, r2[:tm], r4[:tm], r8[:tm], r16[:tm]) - dmv

    return pl.pallas_call(
        body, grid=(nt,), name="pool_bwd",
        in_specs=[pl.BlockSpec((tm, 256), lambda i: (i, 0)),
                  pl.BlockSpec((POOL_HALO, 256), lambda i: (jnp.minimum((i + 1) * (tm // POOL_HALO), T // POOL_HALO - 1), 0))],
        out_specs=pl.BlockSpec((tm, 256), lambda i: (i, 0)),
        out_shape=jax.ShapeDtypeStruct((T, 256), F32),
        compiler_params=_cparams(),
    )(dm, dm)


def attn_bwd(q, k, kt, v, do, lse, delta):
    _, T, _ = q.shape
    tb = min(ATTN_TILE, T)
    sb = min(ATTN_SUB, tb)
    ns = tb // sb
    nb = T // tb
    hb = ATTN_BWD_HEADS

    def body(q_ref, k_ref, kt_ref, v_ref, do_ref, lse_ref, delta_ref, dqt_ref, dk_ref, dv_ref, dk_s, dv_s):
        i, j = pl.program_id(1), pl.program_id(2)

        @pl.when((i == 0) & (j == 0))
        def _():
            dqt_ref[...] = jnp.zeros(dqt_ref.shape, F32)

        @pl.when(j == i)
        def _():
            dk_s[...] = jnp.zeros(dk_s.shape, F32)
            dv_s[...] = jnp.zeros(dv_s.shape, F32)

        def sub_block(h, a, b, masked):
            qa = slice(sb * a, sb * (a + 1))
            kb = slice(sb * b, sb * (b + 1))
            qv = q_ref[h, qa, :]
            dov = do_ref[qa, HEAD_PAD * h:HEAD_PAD * (h + 1)].astype(BF16)
            st = _dot_nt(k_ref[h, kb, :], qv)
            dpt = _dot_nt(v_ref[h, kb, :], dov)
            pt = jnp.exp2(st - lse_ref[h, :, qa])
            if masked:
                krow = lax.broadcasted_iota(jnp.int32, st.shape, 0)
                qcol = lax.broadcasted_iota(jnp.int32, st.shape, 1)
                pt = jnp.where(krow <= qcol, pt, 0.0)
            dst = (pt * (dpt - delta_ref[h, :, qa])).astype(BF16)
            dv_s[h, kb, :] += _dot(pt.astype(BF16), dov)
            dk_s[h, kb, :] += _dot(dst, qv)
            cols = pl.ds(pl.multiple_of(j * tb + sb * a, sb), sb)
            dqt_ref[h, :, cols] += _dot(kt_ref[h, :, kb], dst)

        @pl.when(j > i)
        def _():
            for a in range(ns):
                for h in range(hb):
                    for b in range(ns):
                        sub_block(h, a, b, False)

        @pl.when(j == i)
        def _():
            for a in range(ns):
                for h in range(hb):
                    for b in range(a + 1):
                        sub_block(h, a, b, a == b)

        @pl.when(j == nb - 1)
        def _():
            dk_ref[...] = dk_s[...] * (1.0 / LOG2E)
            dv_ref[...] = dv_s[...]

    qspec = pl.BlockSpec((hb, tb, HEAD_PAD), lambda g, i, j: (g, jnp.maximum(i, j), 0))
    kspec = pl.BlockSpec((hb, tb, HEAD_PAD), lambda g, i, j: (g, i, 0))
    ktspec = pl.BlockSpec((hb, HEAD_PAD, tb), lambda g, i, j: (g, 0, i))
    rowspec = pl.BlockSpec((hb, 1, tb), lambda g, i, j: (g, 0, jnp.maximum(i, j)))
    return pl.pallas_call(
        body, grid=(HEADS // hb, nb, nb), name="attn_bwd",
        in_specs=[qspec, kspec, ktspec, kspec, pl.BlockSpec((tb, hb * HEAD_PAD), lambda g, i, j: (jnp.maximum(i, j), g)), rowspec, rowspec],
        out_specs=[pl.BlockSpec((hb, HEAD_PAD, T), lambda g, i, j: (g, 0, 0)), kspec, kspec],
        out_shape=[jax.ShapeDtypeStruct((HEADS, HEAD_PAD, T), F32)] + [jax.ShapeDtypeStruct((HEADS, T, HEAD_PAD), F32)] * 2,
        scratch_shapes=[pltpu.VMEM((hb, tb, HEAD_PAD), F32), pltpu.VMEM((hb, tb, HEAD_PAD), F32)],
        compiler_params=_cparams(),
    )(q, k, kt, v, do, lse, delta)


def mixin_bwd(dres, x, z, rope, dqt, dk, dv, du, dvs, dpin, gmix, win, gql, wq, gkvl, wkv, gqh, gkh):
    T = x.shape[0]
    tm = min(ROW_TILE, T)
    part = min(MIXIN_BWD_PART, tm)

    def body(dres_ref, x_ref, z_ref, c_ref, sa_ref, sb_ref, dqt_ref, dk_ref, dv_ref, du_ref, dvs_ref, dpin_ref,
             gmix_ref, win_ref, gql_ref, wq_ref, gkvl_ref, wkv_ref, gqh_ref, gkh_ref,
             dx_ref, hn_ref, dz_ref, qn_ref, dqup_ref, kvn_ref, dkvup_ref,
             dgmix_ref, dgql_ref, dgkvl_ref, dgqh_ref, dgkh_ref):
        @pl.when(pl.program_id(0) == 0)
        def _():
            for r in (dgmix_ref, dgql_ref, dgkvl_ref, dgqh_ref, dgkh_ref):
                r[...] = jnp.zeros(r.shape, F32)

        lane = lax.broadcasted_iota(jnp.int32, (part, HEAD_PAD), 1)
        rope_lanes = (lane >= QK_NOPE) & (lane < QK_HEAD)

        def recompute(g):
            rows = g["rows"]
            g["xv"] = x_ref[rows, :]
            hn, g["rx"] = _rms(g["xv"], gmix_ref[...], D_MODEL)
            hn_ref[rows, :] = hn.astype(BF16)
            g["ql"], g["kvl"], g["kr"], qn, g["rq"], kvn, g["rkv"], g["q_up"], g["kv_up"] = _qkv_pre(
                z_ref[rows, :], gql_ref[...], wq_ref[...], gkvl_ref[...], wkv_ref[...])
            qn_ref[rows, :] = qn
            kvn_ref[rows, :] = kvn

        def heads(g):
            rows = g["rows"]
            c, sa, sb = c_ref[rows, :], sa_ref[rows, :], sb_ref[rows, :]
            dkr = jnp.zeros((part, HEAD_PAD), F32)
            dgqh = jnp.zeros((1, HEAD_PAD), F32)
            dgkh = jnp.zeros((1, HEAD_PAD), F32)
            dq_parts, dk_parts, dv_parts = [], [], []
            for h in range(HEADS):
                lo = HEAD_PAD * h
                qh = g["q_up"][:, lo:lo + HEAD_PAD]
                rqh = lax.rsqrt(jnp.sum(qh * qh, axis=-1, keepdims=True) * (1.0 / QK_HEAD) + EPS)
                dq_h = jnp.transpose(dqt_ref[h, :, rows]) * ATTN_SCALE
                dqh, dg = _rms_bwd(qh, rqh, gqh_ref[...], _rope_bwd(dq_h, c, sa, sb), QK_HEAD)
                dgqh = dgqh + dg
                dq_parts.append(dqh)
                kh = g["kv_up"][:, lo:lo + HEAD_PAD] + g["kr"]
                rkh = lax.rsqrt(jnp.sum(kh * kh, axis=-1, keepdims=True) * (1.0 / QK_HEAD) + EPS)
                dkh, dg = _rms_bwd(kh, rkh, gkh_ref[...], _rope_bwd(dk_ref[h, rows, :], c, sa, sb), QK_HEAD)
                dgkh = dgkh + dg
                dkr = dkr + jnp.where(rope_lanes, dkh, 0.0)
                dk_parts.append(dkh)
                dv_parts.append(dv_ref[h, rows, :])
            dgqh_ref[...] += dgqh
            dgkh_ref[...] += dgkh
            g["dkr"] = dkr
            g["dq_up"] = jnp.concatenate(dq_parts, axis=1).astype(BF16)
            g["dkv_up"] = jnp.concatenate(dk_parts + dv_parts, axis=1).astype(BF16)
            dqup_ref[rows, :] = g["dq_up"]
            dkvup_ref[rows, :] = g["dkv_up"]

        def latents(g):
            rows = g["rows"]
            dql, dg = _rms_bwd(g["ql"], g["rq"], gql_ref[...], _dot(g["dq_up"], wq_ref[...]), Q_LORA)
            dgql_ref[...] += dg
            dkvl, dg = _rms_bwd(g["kvl"], g["rkv"], gkvl_ref[...], _dot(g["dkv_up"], wkv_ref[...]), KV_LORA)
            dgkvl_ref[...] += dg
            g["dz"] = jnp.concatenate([dql, dkvl, g["dkr"], du_ref[rows, :], dvs_ref[rows, :], dpin_ref[rows, :]],
                                      axis=1).astype(BF16)
            dz_ref[rows, :] = g["dz"]

        def inputs(g):
            rows = g["rows"]
            dxn, dg = _rms_bwd(g["xv"], g["rx"], gmix_ref[...], _dot(g["dz"], win_ref[...]), D_MODEL)
            dgmix_ref[...] += dg
            dx_ref[rows, :] = dres_ref[rows, :] + dxn

        _skewed((recompute, heads, latents, inputs), [dict(rows=pl.ds(part * p, part)) for p in range(tm // part)])

    row = lambda w: pl.BlockSpec((tm, w), lambda i: (i, 0))
    head = pl.BlockSpec((HEADS, tm, HEAD_PAD), lambda i: (0, i, 0))
    head_t = pl.BlockSpec((HEADS, HEAD_PAD, tm), lambda i: (0, 0, i))
    acc_shapes = [(1, D_MODEL), (1, Q_LORA), (1, KV_LORA), (1, HEAD_PAD), (1, HEAD_PAD)]
    out_rows = [(D_MODEL, F32), (D_MODEL, BF16), (Z_WIDTH, BF16), (Q_LORA, BF16), (HEADS * HEAD_PAD, BF16),
                (KV_LORA, BF16), (2 * HEADS * HEAD_PAD, BF16)]
    return pl.pallas_call(
        body, grid=(T // tm,), name="mixin_bwd",
        in_specs=[row(D_MODEL), row(D_MODEL), row(Z_WIDTH)] + [row(HEAD_PAD)] * 3 + [head_t, head, head, row(256), row(256), row(256)]
                 + [WHOLE] * 8,
        out_specs=[row(w) for w, _ in out_rows] + [_acc_spec(s) for s in acc_shapes],
        out_shape=[jax.ShapeDtypeStruct((T, w), dt) for w, dt in out_rows] + [jax.ShapeDtypeStruct(s, F32) for s in acc_shapes],
        compiler_params=_cparams(),
    )(dres, x, z, *rope, dqt, dk, dv, du, dvs, dpin, gmix, win, gql, wq, gkvl, wkv, gqh, gkh)


def _place():
    x, y, c = lax.axis_index("x"), lax.axis_index("y"), lax.axis_index("c")
    return x, y, c, 4 * x + 2 * y + c


def _layer_of(ref, l):
    return ref[:, l, :] if ref.shape[1] == DEPTH else ref[l]


def _layer_shape(shard):
    return (shard.shape[0], shard.shape[2]) if shard.shape[1] == DEPTH else shard.shape[1:]


def cast_shards(shards, wanted):
    n = len(shards)

    def body(*refs):
        for o_ref, (w, l) in zip(refs[n:], wanted):
            o_ref[...] = _layer_of(refs[w], l).astype(BF16)

    return pl.pallas_call(
        body, name="cast_shards", in_specs=[WHOLE] * n, out_specs=[WHOLE] * len(wanted),
        out_shape=[jax.ShapeDtypeStruct(_layer_shape(shards[w]), BF16) for w, _ in wanted],
        compiler_params=_cparams(),
    )(*shards)


def allgather_layer0(shards):
    n = len(shards)

    def body(*refs):
        ins, outs = refs[:n], refs[n:2 * n]
        stage = refs[2 * n:3 * n]
        send_sems, recv_sems, local_sems = refs[3 * n:]
        x, y, c, me = _place()
        sibling = (x, y, 1 - c)
        chips = [(1 - x, y), (x, 1 - y), (1 - x, 1 - y)]

        def copy(k, w, block_id, to, from_stage):
            return pltpu.make_async_remote_copy(
                src_ref=stage[w] if from_stage else outs[w].at[block_id], dst_ref=outs[w].at[block_id],
                send_sem=send_sems.at[k, w], recv_sem=recv_sems.at[k, w], device_id=to, device_id_type=MESH)

        def block_of(cx, cy, cc):
            return 4 * cx + 2 * cy + cc

        local, sent = [], []
        for w in range(n):
            stage[w][...] = _layer_of(ins[w], 0).astype(BF16)
            mine = pltpu.make_async_copy(stage[w], outs[w].at[me], local_sems.at[w])
            mine.start()
            local.append(mine)
            first = [copy(0, w, me, sibling, True)] + [copy(1 + j, w, me, (*chip, c), True) for j, chip in enumerate(chips)]
            for cp in first:
                cp.start()
            sent += first
        for j, chip in enumerate(chips):
            for w in range(n):
                copy(1 + j, w, block_of(*chip, c), (x, y, c), False).wait_recv()
                fwd = copy(4 + j, w, block_of(*chip, c), sibling, False)
                fwd.start()
                sent.append(fwd)
        for w in range(n):
            copy(0, w, block_of(x, y, 1 - c), (x, y, c), False).wait_recv()
            for j, chip in enumerate(chips):
                copy(4 + j, w, block_of(*chip, 1 - c), (x, y, c), False).wait_recv()
        for cp in local:
            cp.wait()
        for cp in sent:
            cp.wait_send()

    return pl.pallas_call(
        body, name="allgather_layer0",
        in_specs=[WHOLE] * n, out_specs=[ANY] * n,
        out_shape=[jax.ShapeDtypeStruct((N_DEV,) + _layer_shape(s), BF16) for s in shards],
        scratch_shapes=[pltpu.VMEM(_layer_shape(s), BF16) for s in shards]
                       + [pltpu.SemaphoreType.DMA((7, n)), pltpu.SemaphoreType.DMA((7, n)), pltpu.SemaphoreType.DMA((n,))],
        compiler_params=_cparams(),
    )(*shards)


def _peer(k):
    x, y, c, _ = _place()
    px = 1 - x if k & 4 else x
    py = 1 - y if k & 2 else y
    pc = 1 - c if k & 1 else c
    return (px, py, pc), 4 * px + 2 * py + pc


def exchange_start(srcs, after, gather, name):
    n = len(srcs)
    land_shapes = [((N_DEV,) + s.shape) if gather else s.shape for s in srcs]

    def body(*refs):
        src_refs, land_refs = refs[:n], refs[n:2 * n]
        send_sems, recv_sems = refs[2 * n + 1:3 * n + 1], refs[3 * n + 1:4 * n + 1]
        token = refs[-1]
        _, _, _, me = _place()
        for k in range(1, N_DEV):
            peer, peer_id = _peer(k)
            for w in range(n):
                pltpu.make_async_remote_copy(
                    src_ref=src_refs[w] if gather else src_refs[w].at[peer_id], dst_ref=land_refs[w].at[me],
                    send_sem=send_sems[w], recv_sem=recv_sems[w], device_id=peer, device_id_type=MESH).start()
        token[...] = jnp.zeros(token.shape, F32)

    hbm = lambda a: pltpu.with_memory_space_constraint(a, pltpu.HBM)
    outs = pl.pallas_call(
        body, name=name,
        out_shape=(pltpu.SemaphoreType.DMA(()),) * (2 * n)
                  + tuple(pltpu.HBM(s.shape, BF16) for s in srcs) + tuple(pltpu.HBM(s, BF16) for s in land_shapes)
                  + (jax.ShapeDtypeStruct((8, 128), F32),),
        in_specs=[HBM_SPEC] * (2 * n) + [ANY],
        out_specs=(SEM_SPEC,) * (2 * n) + (HBM_SPEC,) * (2 * n) + (WHOLE,),
        input_output_aliases={i: 2 * n + i for i in range(2 * n)},
        compiler_params=pltpu.CompilerParams(has_side_effects=pltpu.SideEffectType.DATAFLOW_SIDE_EFFECTING),
    )(*[hbm(s) for s in srcs], *[hbm(lax.empty(s, BF16)) for s in land_shapes], after)
    return list(outs[:n]), list(outs[n:2 * n]), list(outs[2 * n:3 * n]), list(outs[3 * n:4 * n]), outs[-1]


def exchange_wait(started, after, name):
    send_sems, recv_sems, srcs, lands, _ = started
    n = len(srcs)

    def body(*refs):
        land_refs = refs[n:2 * n]
        send_sems, recv_sems = refs[2 * n:3 * n], refs[3 * n:4 * n]
        x, y, c, _ = _place()
        for w in range(n):
            seven = land_refs[w].at[pl.ds(0, N_DEV - 1)]
            cp = pltpu.make_async_remote_copy(src_ref=seven, dst_ref=seven, send_sem=send_sems[w], recv_sem=recv_sems[w],
                                              device_id=(x, y, c), device_id_type=MESH)
            cp.wait_send()
            cp.wait_recv()

    outs = pl.pallas_call(
        body, name=name,
        out_shape=tuple(pltpu.HBM(s.shape, BF16) for s in srcs) + tuple(pltpu.HBM(l.shape, BF16) for l in lands),
        in_specs=[HBM_SPEC] * (2 * n) + [SEM_SPEC] * (2 * n) + [ANY],
        out_specs=(HBM_SPEC,) * (2 * n),
        input_output_aliases={i: i for i in range(2 * n)},
        compiler_params=pltpu.CompilerParams(has_side_effects=pltpu.SideEffectType.DATAFLOW_SIDE_EFFECTING),
    )(*srcs, *lands, *send_sems, *recv_sems, after)
    _, _, _, me = _place()
    filled = []
    for src, land in zip(outs[:n], outs[n:]):
        own = src[None] if src.ndim == 2 else lax.dynamic_slice_in_dim(src, me, 1, axis=0)
        filled.append(lax.dynamic_update_slice_in_dim(land, own, me, axis=0))
    return filled


def exchange_grads(pieces):
    n = len(pieces)

    def body(*refs):
        ins, outs = refs[:n], refs[n:2 * n]
        send_sems, recv_sems, local_sems = refs[2 * n:]
        x, y, c, me = _place()
        local = [pltpu.make_async_copy(ins[w].at[me], outs[w].at[me], local_sems.at[w]) for w in range(n)]
        for cp in local:
            cp.start()
        remote = []
        for k in range(1, N_DEV):
            px = 1 - x if k & 4 else x
            py = 1 - y if k & 2 else y
            pc = 1 - c if k & 1 else c
            peer = 4 * px + 2 * py + pc
            for w in range(n):
                cp = pltpu.make_async_remote_copy(
                    src_ref=ins[w].at[peer], dst_ref=outs[w].at[me], send_sem=send_sems.at[k - 1, w],
                    recv_sem=recv_sems.at[k - 1, w], device_id=(px, py, pc), device_id_type=MESH)
                cp.start()
                remote.append(cp)
        for cp in local:
            cp.wait()
        for cp in remote:
            cp.wait_recv()
        for cp in remote:
            cp.wait_send()

    return pl.pallas_call(
        body, name="exchange_grads",
        in_specs=[ANY] * n, out_specs=[ANY] * n,
        out_shape=[jax.ShapeDtypeStruct(p.shape, BF16) for p in pieces],
        scratch_shapes=[pltpu.SemaphoreType.DMA((7, n)), pltpu.SemaphoreType.DMA((7, n)), pltpu.SemaphoreType.DMA((n,))],
        compiler_params=_cparams(),
    )(*pieces)


def _adamw(w, g, m, v):
    m2 = ADAM_B1 * m + (1.0 - ADAM_B1) * g
    v2 = ADAM_B2 * v + (1.0 - ADAM_B2) * (g * g)
    m_hat = m2 / (1.0 - ADAM_B1 ** ADAM_STEP)
    v_hat = v2 / (1.0 - ADAM_B2 ** ADAM_STEP)
    delta = -ADAM_LR * (m_hat / (jnp.sqrt(v_hat) + ADAM_EPS) + ADAM_WD * w)
    return delta, m2, v2


def adamw_sharded(parts, w, m, v, name):
    L, R, C = w.shape
    fits = [d for d in range(16, min(R, 512) + 1, 16) if R % d == 0]
    br = max(fits) if fits else R
    nblk = R // br

    def body(*refs):
        p_refs = refs[:L]
        w_ref, m_ref, v_ref, g_ref, d_ref, m2_ref, v2_ref = refs[L:]

        def total(p_ref):
            g = p_ref[0].astype(F32)
            for s in range(1, N_DEV):
                g = g + p_ref[s].astype(F32)
            return g

        g = total(p_refs[0])
        for l in range(1, L):
            g = jnp.where(pl.program_id(0) == l, total(p_refs[l]), g)
        g_ref[...] = g
        d_ref[...], m2_ref[...], v2_ref[...] = _adamw(w_ref[...], g, m_ref[...], v_ref[...])

    blk = pl.BlockSpec((None, br, C), lambda l, i: (l, i, 0))

    def part_spec(k):
        return pl.BlockSpec((N_DEV, br, C), lambda l, i: (0, jnp.where(l == k, i, jnp.where(l < k, 0, nblk - 1)), 0))

    return pl.pallas_call(
        body, grid=(L, nblk), name=name,
        in_specs=[part_spec(k) for k in range(L)] + [blk, blk, blk],
        out_specs=[blk] * 4,
        out_shape=[jax.ShapeDtypeStruct((L, R, C), F32)] * 4,
        compiler_params=_cparams(),
    )(*parts, w, m, v)


def allreduce_adamw_small(gpart, w, m, v):
    R = gpart.shape[0]

    def body(g_ref, w_ref, m_ref, v_ref, grad_ref, d_ref, m2_ref, v2_ref, all_ref, send_sems, recv_sems):
        x, y, c, me = _place()
        sibling = (x, y, 1 - c)
        chips = [(1 - x, y), (x, 1 - y), (1 - x, 1 - y)]

        def copy(k, block_id, to, from_input):
            return pltpu.make_async_remote_copy(
                src_ref=g_ref if from_input else all_ref.at[block_id], dst_ref=all_ref.at[block_id],
                send_sem=send_sems.at[k], recv_sem=recv_sems.at[k], device_id=to, device_id_type=MESH)

        def block_of(cx, cy, cc):
            return 4 * cx + 2 * cy + cc

        all_ref[me] = g_ref[...]
        first = [copy(0, me, sibling, True)] + [copy(1 + j, me, (*chip, c), True) for j, chip in enumerate(chips)]
        for cp in first:
            cp.start()
        passed = [copy(4 + j, block_of(*chip, c), sibling, False) for j, chip in enumerate(chips)]
        for j, chip in enumerate(chips):
            copy(1 + j, block_of(*chip, c), (x, y, c), False).wait_recv()
            passed[j].start()
        copy(0, block_of(x, y, 1 - c), (x, y, c), False).wait_recv()
        for j, chip in enumerate(chips):
            copy(4 + j, block_of(*chip, 1 - c), (x, y, c), False).wait_recv()
        for cp in first + passed:
            cp.wait_send()
        g = all_ref[0]
        for s in range(1, N_DEV):
            g = g + all_ref[s]
        grad_ref[...] = g
        d_ref[...], m2_ref[...], v2_ref[...] = _adamw(w_ref[...], g, m_ref[...], v_ref[...])

    return pl.pallas_call(
        body, name="allreduce_adamw_small",
        in_specs=[WHOLE] * 4, out_specs=[WHOLE] * 4,
        out_shape=[jax.ShapeDtypeStruct((R, 128), F32)] * 4,
        scratch_shapes=[pltpu.VMEM((N_DEV, R, 128), F32), pltpu.SemaphoreType.DMA((7,)), pltpu.SemaphoreType.DMA((7,))],
        compiler_params=_cparams(),
    )(gpart, w, m, v)


def _shard_view(name, a):
    if name == "w_in":
        return a.transpose(2, 0, 1)
    return a.swapaxes(1, 2) if name in TRANSPOSED else a


def _shard_unview(name, a):
    if name == "w_in":
        return a.transpose(1, 2, 0)
    return a.swapaxes(1, 2) if name in TRANSPOSED else a


def _pad_head_rows(w, width):
    c = w.shape[1]
    return jnp.pad(w.reshape(HEADS, width, c), ((0, 0), (0, HEAD_PAD - width), (0, 0))).reshape(HEADS * HEAD_PAD, c)


def _unpad_head_rows(w, width):
    c = w.shape[1]
    return w.reshape(HEADS, HEAD_PAD, c)[:, :width].reshape(HEADS * width, c)


O1 = Q_LORA
O2 = O1 + KV_LORA
O3 = O2 + QK_ROPE


def _mixer_weights(gw):
    w_in = gw["w_in"].reshape(IN_WIDTH, D_MODEL)
    zero = lambda n: jnp.zeros((n, D_MODEL), BF16)
    win = jnp.concatenate([w_in[:O2], zero(QK_NOPE), w_in[O2:O3], zero(HEAD_PAD - QK_HEAD), w_in[O3:]], axis=0)
    wq = _pad_head_rows(gw["w_q_up"].reshape(HEADS * QK_HEAD, Q_LORA), QK_HEAD)
    w_kv = gw["w_kv_up"].reshape(HEADS, QK_NOPE + V_HEAD, KV_LORA)
    wk = jnp.pad(w_kv[:, :QK_NOPE], ((0, 0), (0, HEAD_PAD - QK_NOPE), (0, 0))).reshape(HEADS * HEAD_PAD, KV_LORA)
    wv = w_kv[:, QK_NOPE:].reshape(HEADS * V_HEAD, KV_LORA)
    wkv = jnp.concatenate([wk, wv], axis=0)
    out = dict(win=win, wq=wq, wkv=wkv)
    if "w_out" in gw:
        out["wout"] = gw["w_out"].reshape(D_MODEL, D_MODEL)
    return out


def _ffn_weights(gw):
    return dict(wg=gw["w_gate"].reshape(FFN_HIDDEN, D_MODEL), wu=gw["w_up"].reshape(FFN_HIDDEN, D_MODEL),
                wd=gw["w_down"].reshape(FFN_HIDDEN, D_MODEL))


def _layer_small(p, l):
    row = lambda a: a.reshape(1, -1)
    pad_head = lambda g: jnp.pad(g, (0, HEAD_PAD - QK_HEAD)).reshape(1, HEAD_PAD)
    tril = jnp.tril(jnp.ones((CHUNK, CHUNK), F32))
    wsp = p["w_spatial"][l] * tril
    wcat = jnp.concatenate([wsp[h] for h in range(HEADS)], axis=1).astype(BF16)
    wcat_t = jnp.concatenate([wsp[h].T for h in range(HEADS)], axis=1).astype(BF16)
    bz = jnp.repeat(p["b_spatial"][l].T, SGU_HEAD_DIM, axis=1)
    wp = jax.scipy.linalg.block_diag(*[p["w_pool"][l][g] for g in range(HEADS)]).astype(BF16)
    return dict(gmix=row(p["g_mix_norm"][l]), gql=row(p["g_q_lat"][l]), gkvl=row(p["g_kv_lat"][l]),
                gqh=pad_head(p["g_q_head"][l]), gkh=pad_head(p["g_k_head"][l]), gsgu=row(p["g_sgu_v"][l]),
                wcat=wcat, wcat_t=wcat_t, bz=bz, wp=wp, pscale=row(p["pool_scale"][l]),
                goa=row(p["g_out_mla"][l]), gos=row(p["g_out_sgu"][l]), gop=row(p["g_out_pool"][l]),
                gffn=row(p["g_ffn_norm"][l]))


MIXER_SIDE = ("w_in", "w_q_up", "w_kv_up", "w_out")
FFN_SIDE = ("w_gate", "w_up", "w_down")
TRANSPOSED = ("w_in", "w_q_up", "w_kv_up", "w_gate", "w_up")
FIRST_0 = ("w_in", "w_q_up", "w_kv_up")
LATER_0 = ("w_out",) + FFN_SIDE
SHARDED = MIXER_SIDE + FFN_SIDE
SMALL = ("g_mix_norm", "g_q_lat", "g_kv_lat", "g_q_head", "g_k_head", "g_sgu_v", "w_spatial", "b_spatial", "w_pool",
         "pool_scale", "g_out_mla", "g_out_sgu", "g_out_pool", "g_ffn_norm")
WEIGHTS = ("g_mix_norm", "w_in", "g_q_lat", "w_q_up", "g_kv_lat", "w_kv_up", "g_q_head", "g_k_head", "g_sgu_v", "w_spatial",
           "b_spatial", "w_pool", "pool_scale", "g_out_mla", "g_out_sgu", "g_out_pool", "w_out", "g_ffn_norm", "w_gate",
           "w_up", "w_down")
PACK_ROWS = 8 * 128


def _pack_small(parts):
    flat = []
    for name in SMALL:
        a = parts[name].reshape(-1)
        flat.append(jnp.pad(a, (0, -a.shape[0] % PACK_ROWS)))
    return jnp.concatenate(flat).reshape(-1, 128)


def _unpack_small(packed, like):
    out, row = {}, 0
    for name in SMALL:
        n = math.prod(like[name].shape)
        rows = -(-n // PACK_ROWS) * 8
        out[name] = packed[row:row + rows].reshape(-1)[:n].reshape(like[name].shape)
        row += rows
    return out


def _forward_attention(x, rope, W, S):
    z, q, k, v, kt, vt = mixin_fwd(x, rope, S["gmix"], W["win"], S["gql"], W["wq"], S["gkvl"], W["wkv"], S["gqh"], S["gkh"])
    a, lse = attn_fwd(q, k, vt)
    return dict(x=x, z=z, q=q, k=k, kt=kt, v=v, a=a, lse=lse)


def _forward_mixers(A, wout, S):
    A["x1"] = mixers_fwd(A["x"], A["a"], A["z"], S["gsgu"], S["wcat"], S["bz"], S["wp"], S["pscale"], S["goa"], S["gos"],
                         S["gop"], wout)
    return A["x1"]


def _backward_ffn(dx2, W, S, A, l):
    dx1, h2, act, dg, du_ffn, dgffn = ffn_bwd(dx2, A["x1"], A["gs"], A["us"], S["gffn"], W["wg"], W["wu"], W["wd"])
    d_wd = matmul_tn(act, dx2, f"dw_down_{l}")
    d_wg = matmul_tn(dg, h2, f"dw_gate_{l}")
    d_wu = matmul_tn(du_ffn, h2, f"dw_up_{l}")
    big = {n: d.reshape(N_DEV, -1, D_MODEL) for n, d in (("w_gate", d_wg), ("w_up", d_wu), ("w_down", d_wd))}
    return dx1, big, dgffn


def _backward_mixers(dx1, dgffn, rope, W, S, A, l, send_wout=None):
    (da, delta, du, dvs, dm, mix, dgoa, dgos, dgop, dps, dwp, dwsp, db, dgsgu) = mixers_bwd(
        dx1, A["a"], A["z"], S["gsgu"], S["wcat"], S["wcat_t"], S["bz"], S["wp"], S["pscale"], S["goa"], S["gos"],
        S["gop"], W["wout"])
    d_wout = matmul_tn(mix, dx1, f"dw_out_{l}")
    if send_wout is not None:
        delta = delta + send_wout(d_wout.reshape(N_DEV, -1, D_MODEL))[0, 0]
    dpin = pool_bwd(dm)
    dqt, dk, dv = attn_bwd(A["q"], A["k"], A["kt"], A["v"], da, A["lse"], delta)
    (dx, hn, dz, qn, dq_up, kvn, dkv_up, dgmix, dgql, dgkvl, dgqh, dgkh) = mixin_bwd(
        dx1, A["x"], A["z"], rope, dqt, dk, dv, du, dvs, dpin, S["gmix"], W["win"], S["gql"], W["wq"], S["gkvl"],
        W["wkv"], S["gqh"], S["gkh"])
    d_win = matmul_tn(dz, hn, f"dw_in_{l}")
    d_wq = matmul_tn(dq_up, qn, f"dw_q_up_{l}")
    d_wkv = matmul_tn(dkv_up, kvn, f"dw_kv_up_{l}")
    d_win = jnp.concatenate([d_win[:O2], d_win[O2 + QK_NOPE:O2 + QK_HEAD], d_win[O2 + HEAD_PAD:]], axis=0)
    d_wk = d_wkv[:HEADS * HEAD_PAD].reshape(HEADS, HEAD_PAD, KV_LORA)[:, :QK_NOPE]
    d_wv = d_wkv[HEADS * HEAD_PAD:].reshape(HEADS, V_HEAD, KV_LORA)
    d_wkv = jnp.concatenate([d_wk, d_wv], axis=1)
    big = dict(w_in=d_win.reshape(N_DEV, -1, D_MODEL), w_q_up=_unpad_head_rows(d_wq, QK_HEAD).reshape(N_DEV, -1, Q_LORA),
               w_kv_up=d_wkv.reshape(N_DEV, -1, KV_LORA), w_out=d_wout.reshape(N_DEV, -1, D_MODEL))
    tril = jnp.tril(jnp.ones((CHUNK, CHUNK), F32))
    small = dict(g_mix_norm=dgmix[0], g_q_lat=dgql[0], g_kv_lat=dgkvl[0], g_q_head=dgqh[0, :QK_HEAD], g_k_head=dgkh[0, :QK_HEAD],
                 g_sgu_v=dgsgu[0], w_spatial=dwsp.reshape(HEADS, CHUNK, CHUNK) * tril, b_spatial=db[:, :HEADS].T,
                 w_pool=jnp.stack([dwp[64 * g:64 * (g + 1), 64 * g:64 * (g + 1)] for g in range(HEADS)]),
                 pool_scale=dps[0], g_out_mla=dgoa[0], g_out_sgu=dgos[0], g_out_pool=dgop[0], g_ffn_norm=dgffn[0])
    return dx, big, small


def kernel(x, positions, g_mix_norm, w_in, g_q_lat, w_q_up, g_kv_lat, w_kv_up, g_q_head, g_k_head, g_sgu_v, w_spatial, b_spatial, w_pool, pool_scale, g_out_mla, g_out_sgu, g_out_pool, w_out, g_ffn_norm, w_gate, w_up, w_down, loss_target, m_g_mix_norm, m_w_in, m_g_q_lat, m_w_q_up, m_g_kv_lat, m_w_kv_up, m_g_q_head, m_g_k_head, m_g_sgu_v, m_w_spatial, m_b_spatial, m_w_pool, m_pool_scale, m_g_out_mla, m_g_out_sgu, m_g_out_pool, m_w_out, m_g_ffn_norm, m_w_gate, m_w_up, m_w_down, v_g_mix_norm, v_w_in, v_g_q_lat, v_w_q_up, v_g_kv_lat, v_w_kv_up, v_g_q_head, v_g_k_head, v_g_sgu_v, v_w_spatial, v_b_spatial, v_w_pool, v_pool_scale, v_g_out_mla, v_g_out_sgu, v_g_out_pool, v_w_out, v_g_ffn_norm, v_w_gate, v_w_up, v_w_down):
    given = dict(locals())
    w = {n: given[n] for n in WEIGHTS}
    m = {n: given["m_" + n] for n in WEIGHTS}
    v = {n: given["v_" + n] for n in WEIGHTS}
    T = x.shape[1]
    xs = x.reshape(T, D_MODEL)

    half = QK_ROPE // 2
    inv_freq = 1.0 / (ROPE_THETA ** (jnp.arange(half, dtype=F32) / half))
    ang16 = positions.reshape(T).astype(F32)[:, None] * inv_freq
    ang = jnp.concatenate([jnp.zeros((T, QK_NOPE), F32), ang16, ang16, jnp.zeros((T, HEAD_PAD - QK_HEAD), F32)], axis=1)

    wv = {n: _shard_view(n, w[n]) for n in SHARDED}
    wanted = [(SHARDED.index(n), 0) for n in LATER_0] + [(i, 1) for i in range(len(SHARDED))]
    bf = cast_shards([wv[n] for n in SHARDED], wanted)
    first0 = allgather_layer0([wv[n] for n in FIRST_0])
    ag_later0 = exchange_start(bf[:len(LATER_0)], first0[0], True, "ag_start_later0")
    ag_l1 = exchange_start(bf[len(LATER_0):], ag_later0[-1], True, "ag_start_l1")
    rope = rope_tables(ang + (ag_later0[-1][0, 0] + ag_l1[-1][0, 0]))
    Ss = [_layer_small(w, l) for l in range(DEPTH)]

    W0 = dict(zip(FIRST_0, first0))
    A0 = _forward_attention(xs, rope, _mixer_weights(W0), Ss[0])
    W0.update(zip(LATER_0, exchange_wait(ag_later0, A0["a"], "ag_wait_later0")))
    W0 = {**_mixer_weights(W0), **_ffn_weights(W0)}
    x1 = _forward_mixers(A0, W0["wout"], Ss[0])
    h, A0["gs"], A0["us"] = ffn_fwd(x1, Ss[0]["gffn"], W0["wg"], W0["wu"], W0["wd"])
    layer1 = dict(zip(SHARDED, exchange_wait(ag_l1, h, "ag_wait_l1")))
    W1 = {**_mixer_weights(layer1), **_ffn_weights(layer1)}
    A1 = _forward_attention(h, rope, W1, Ss[1])
    x1 = _forward_mixers(A1, W1["wout"], Ss[1])
    dh, loss_part, A1["gs"], A1["us"] = ffn_fwd(x1, Ss[1]["gffn"], W1["wg"], W1["wu"], W1["wd"],
                                                target=loss_target.reshape(T, D_MODEL))
    loss = lax.psum(loss_part[0, 0], ("x", "y", "c"))

    dx1, big_f1, dgffn1 = _backward_ffn(dh, W1, Ss[1], A1, 1)
    dh, big_m1, small1 = _backward_mixers(dx1, dgffn1, rope, W1, Ss[1], A1, 1)
    rs_l1 = exchange_start([{**big_m1, **big_f1}[n] for n in SHARDED], dh, False, "rs_start_l1")
    S0 = dict(Ss[0], gffn=Ss[0]["gffn"] + rs_l1[-1][0, 0])
    dx1, big_f0, dgffn0 = _backward_ffn(dh, W0, S0, A0, 0)
    rs_ffn0 = exchange_start([big_f0[n] for n in FFN_SIDE], dx1, False, "rs_start_ffn0")
    S0 = dict(Ss[0], gsgu=Ss[0]["gsgu"] + rs_ffn0[-1][0, 0])
    rs_out0 = []

    def send_wout(pieces):
        rs_out0.append(exchange_start([pieces], dx1, False, "rs_start_out0"))
        return rs_out0[0][-1]

    dh, big_m0, small0 = _backward_mixers(dx1, dgffn0, rope, W0, S0, A0, 0, send_wout)
    grad_x = dh.reshape(x.shape)
    smalls = [small0, small1]

    parts1 = dict(zip(SHARDED, exchange_wait(rs_l1, dh, "rs_wait_l1")))
    parts0 = dict(zip(FFN_SIDE, exchange_wait(rs_ffn0, dh, "rs_wait_ffn0")))
    parts0["w_out"] = exchange_wait(rs_out0[0], dh, "rs_wait_out0")[0]
    parts0.update(zip(FIRST_0, exchange_grads([big_m0[n] for n in FIRST_0])))
    grad, delta, new_m, new_v = {}, {}, {}, {}
    for n in SHARDED:
        parts, views = [parts0[n], parts1[n]], [_shard_view(n, a[n]) for a in (w, m, v)]
        if n == "w_in":
            parts = [jnp.stack(parts, axis=2).reshape(N_DEV, -1, D_MODEL)]
            views = [a.reshape(1, -1, D_MODEL) for a in views]
        outs = adamw_sharded(parts, *views, f"adamw_{n}")
        if n == "w_in":
            outs = [o.reshape(-1, DEPTH, D_MODEL) for o in outs]
        grad[n], delta[n], new_m[n], new_v[n] = [_shard_unview(n, o) for o in outs]

    small_part = {n: jnp.stack([smalls[l][n] for l in range(DEPTH)]) for n in SMALL}
    outs = allreduce_adamw_small(_pack_small(small_part), _pack_small(w), _pack_small(m), _pack_small(v))
    for d, o in zip((grad, delta, new_m, new_v), outs):
        d.update(_unpack_small(o, w))

    return (loss, grad_x, *[grad[n] for n in WEIGHTS], *[delta[n] for n in WEIGHTS], *[new_m[n] for n in WEIGHTS],
            *[new_v[n] for n in WEIGHTS])
```

```python
import functools
import math

import jax
import jax.numpy as jnp
from jax import lax
from jax.experimental import pallas as pl
from jax.experimental.pallas import tpu as pltpu

F32 = jnp.float32
BF16 = jnp.bfloat16

N_DEV = 8
DEPTH = 2
D_MODEL = 1024
HEADS = 4
HEAD_PAD = 128
QK_NOPE = 64
QK_ROPE = 32
QK_HEAD = QK_NOPE + QK_ROPE
V_HEAD = 128
Q_LORA = 256
KV_LORA = 128
SGU_WIDTH = 256
SGU_HEAD_DIM = 64
CHUNK = 128
POOL_WIDTH = 256
POOL_WINDOWS = (2, 4, 8, 16)
POOL_HALO = 16
MLA_WIDTH = 512
IN_WIDTH = 1184
Z_WIDTH = 1280
FFN_HIDDEN = 2816
FFN_CHUNK = 256
ROPE_THETA = 10000.0
EPS = 1e-6
ATTN_SCALE = 1.0 / math.sqrt(QK_HEAD)
LOG2E = 1.4426950408889634
NEG_BIG = -1e30

ADAM_LR = 0.001
ADAM_B1 = 0.9
ADAM_B2 = 0.999
ADAM_EPS = 1e-08
ADAM_WD = 0.01
ADAM_STEP = 10

VMEM_LIMIT = 56 * 1024 * 1024
ROW_TILE = 512
MIXIN_PART = 256
MIXIN_BWD_PART = 128
ATTN_TILE = 1024
ATTN_SUB = 512
ATTN_FWD_SUB = 512
ATTN_BWD_HEADS = 2
MESH = pl.DeviceIdType.MESH

WHOLE = pl.BlockSpec(memory_space=pltpu.VMEM)
ANY = pl.BlockSpec(memory_space=pl.ANY)
HBM_SPEC = pl.BlockSpec(memory_space=pltpu.HBM)
SEM_SPEC = pl.BlockSpec(memory_space=pltpu.SEMAPHORE)


def _cparams(**kw):
    return pltpu.CompilerParams(vmem_limit_bytes=VMEM_LIMIT, **kw)


def _dot(a, b):
    return jnp.dot(a, b, preferred_element_type=F32)


def _dot_nt(a, b):
    return lax.dot_general(a, b, (((1,), (1,)), ((), ())), preferred_element_type=F32)


def _dot_tn(a, b):
    return lax.dot_general(a, b, (((0,), (0,)), ((), ())), preferred_element_type=F32)


def _rms(x, g, n):
    r = lax.rsqrt(jnp.sum(x * x, axis=-1, keepdims=True) * (1.0 / n) + EPS)
    return x * r * g, r


def _rms_bwd(x, r, g, dy, n):
    gdy = dy * g
    dx = r * gdy - x * (r * r * r) * (jnp.sum(x * gdy, axis=-1, keepdims=True) * (1.0 / n))
    dg = jnp.sum(dy * (x * r), axis=0, keepdims=True)
    return dx, dg


def _sigmoid(x):
    return 1.0 / (1.0 + jnp.exp(-x))


def rope_tables(ang):
    T = ang.shape[0]
    tm = min(ROW_TILE, T)

    def body(ang_ref, c_ref, sa_ref, sb_ref):
        a = ang_ref[...]
        lane = lax.broadcasted_iota(jnp.int32, a.shape, 1)
        s = jnp.sin(a)
        c_ref[...] = jnp.cos(a)
        sa_ref[...] = jnp.where(lane < QK_NOPE + QK_ROPE // 2, -s, 0.0)
        sb_ref[...] = jnp.where(lane >= QK_NOPE + QK_ROPE // 2, s, 0.0)

    row = pl.BlockSpec((tm, HEAD_PAD), lambda i: (i, 0))
    return pl.pallas_call(
        body, grid=(T // tm,), name="rope_tables", in_specs=[row], out_specs=[row] * 3,
        out_shape=[jax.ShapeDtypeStruct((T, HEAD_PAD), F32)] * 3, compiler_params=_cparams(),
    )(ang)


def _rope(x, c, sa, sb):
    half = QK_ROPE // 2
    return x * c + pltpu.roll(x, HEAD_PAD - half, 1) * sa + pltpu.roll(x, half, 1) * sb


def _rope_bwd(dy, c, sa, sb):
    half = QK_ROPE // 2
    return dy * c + pltpu.roll(dy * sa, half, 1) + pltpu.roll(dy * sb, HEAD_PAD - half, 1)


def _skewed(stages, groups):
    for t in range(len(stages) + len(groups) - 1):
        for p, g in enumerate(groups):
            if 0 <= t - p < len(stages):
                stages[t - p](g)


def _head_masks(shape, width):
    lane = lax.broadcasted_iota(jnp.int32, shape, len(shape) - 1)
    return [(lane >= width * h) & (lane < width * (h + 1)) for h in range(HEADS)]


def _qkv_pre(z, gql, wq, gkvl, wkv):
    ql = z[:, 0:Q_LORA]
    kvl = z[:, Q_LORA:Q_LORA + KV_LORA]
    kr = z[:, Q_LORA + KV_LORA:Q_LORA + KV_LORA + HEAD_PAD]
    qn, rq = _rms(ql, gql, Q_LORA)
    kvn, rkv = _rms(kvl, gkvl, KV_LORA)
    qn = qn.astype(BF16)
    kvn = kvn.astype(BF16)
    q_up = _dot_nt(qn, wq)
    kv_up = _dot_nt(kvn, wkv)
    return ql, kvl, kr, qn, rq, kvn, rkv, q_up, kv_up


def mixin_fwd(x, rope, gmix, win, gql, wq, gkvl, wkv, gqh, gkh):
    T = x.shape[0]
    tm = min(2 * ROW_TILE, T)
    part = min(MIXIN_PART, tm)

    def body(x_ref, c_ref, sa_ref, sb_ref, gmix_ref, win_ref, gql_ref, wq_ref, gkvl_ref, wkv_ref, gqh_ref, gkh_ref,
             z_ref, q_ref, k_ref, v_ref, kt_ref, vt_ref):
        def project(g):
            hn = _rms(x_ref[g["rows"], :], gmix_ref[...], D_MODEL)[0].astype(BF16)
            g["z"] = _dot_nt(hn, win_ref[...])
            z_ref[g["rows"], :] = g["z"]

        def latents(g):
            _, _, g["kr"], _, _, _, _, g["q_up"], g["kv_up"] = _qkv_pre(g["z"], gql_ref[...], wq_ref[...], gkvl_ref[...],
                                                                        wkv_ref[...])

        def heads(g):
            rows = g["rows"]
            c, sa, sb = c_ref[rows, :], sa_ref[rows, :], sb_ref[rows, :]
            for h in range(HEADS):
                lo = HEAD_PAD * h
                qh = _rms(g["q_up"][:, lo:lo + HEAD_PAD], gqh_ref[...], QK_HEAD)[0]
                q_ref[h, rows, :] = (_rope(qh, c, sa, sb) * (ATTN_SCALE * LOG2E)).astype(BF16)
                kh = _rope(_rms(g["kv_up"][:, lo:lo + HEAD_PAD] + g["kr"], gkh_ref[...], QK_HEAD)[0], c, sa, sb)
                k_ref[h, rows, :] = kh.astype(BF16)
                kt_ref[h, :, rows] = jnp.transpose(kh).astype(BF16)
                vh = g["kv_up"][:, HEADS * HEAD_PAD + lo:HEADS * HEAD_PAD + lo + HEAD_PAD]
                v_ref[h, rows, :] = vh.astype(BF16)
                vt_ref[h, :, rows] = jnp.transpose(vh).astype(BF16)

        _skewed((project, latents, heads), [dict(rows=pl.ds(part * p, part)) for p in range(tm // part)])

    row = lambda w: pl.BlockSpec((tm, w), lambda i: (i, 0))
    head = pl.BlockSpec((HEADS, tm, HEAD_PAD), lambda i: (0, i, 0))
    head_t = pl.BlockSpec((HEADS, HEAD_PAD, tm), lambda i: (0, 0, i))
    return pl.pallas_call(
        body, grid=(T // tm,), name="mixin_fwd",
        in_specs=[row(D_MODEL)] + [row(HEAD_PAD)] * 3 + [WHOLE] * 8,
        out_specs=[row(Z_WIDTH), head, head, head, head_t, head_t],
        out_shape=[jax.ShapeDtypeStruct((T, Z_WIDTH), F32)] + [jax.ShapeDtypeStruct((HEADS, T, HEAD_PAD), BF16)] * 3
                  + [jax.ShapeDtypeStruct((HEADS, HEAD_PAD, T), BF16)] * 2,
        compiler_params=_cparams(),
    )(x, *rope, gmix, win, gql, wq, gkvl, wkv, gqh, gkh)


def attn_fwd(q, k, vt):
    _, T, _ = q.shape
    tb = min(ATTN_TILE, T)
    sq = min(ATTN_FWD_SUB, tb)
    nb = T // tb

    def body(q_ref, k_ref, vt_ref, o_ref, lse_ref, m_s, l_s, acc_s):
        i, j = pl.program_id(0), pl.program_id(1)

        @pl.when(j == 0)
        def _():
            m_s[...] = jnp.full(m_s.shape, -jnp.inf, F32)
            l_s[...] = jnp.zeros(l_s.shape, F32)
            acc_s[...] = jnp.zeros(acc_s.shape, F32)

        def scores(g):
            st = _dot_nt(k_ref[g["h"], g["kb"], :], q_ref[g["h"], g["qa"], :])
            if g["masked"]:
                krow = lax.broadcasted_iota(jnp.int32, st.shape, 0)
                qcol = g["q0"] + lax.broadcasted_iota(jnp.int32, st.shape, 1)
                st = jnp.where(krow <= qcol, st, NEG_BIG)
            g["st"] = st

        def new_max(g):
            h, qa = g["h"], g["qa"]
            m_prev = m_s[h, :, qa]
            g["m"] = jnp.maximum(m_prev, jnp.max(g["st"], axis=0, keepdims=True))
            g["alpha"] = jnp.exp2(m_prev - g["m"])
            m_s[h, :, qa] = g["m"]

        def weights(g):
            pt = jnp.exp2(g["st"] - g["m"])
            g["lsum"] = jnp.sum(pt, axis=0, keepdims=True)
            g["pt"] = pt.astype(BF16)

        def accumulate(g):
            h, qa = g["h"], g["qa"]
            l_s[h, :, qa] = g["alpha"] * l_s[h, :, qa] + g["lsum"]
            acc_s[h, :, qa] = g["alpha"] * acc_s[h, :, qa] + _dot(vt_ref[h, :, g["kb"]], g["pt"])

        def tiles(masked):
            return [dict(h=h, q0=q0, qa=slice(q0, q0 + sq), kb=slice(0, q0 + sq if masked else tb), masked=masked)
                    for h in range(HEADS) for q0 in range(0, tb, sq)]

        @pl.when(j < i)
        def _():
            _skewed((scores, new_max, weights, accumulate), tiles(False))

        @pl.when(j == i)
        def _():
            _skewed((scores, new_max, weights, accumulate), tiles(True))
            for h in range(HEADS):
                l = l_s[h]
                o_ref[:, HEAD_PAD * h:HEAD_PAD * (h + 1)] = jnp.transpose(acc_s[h] / l)
                lse_ref[h] = m_s[h] + jnp.log2(l)

    qspec = pl.BlockSpec((HEADS, tb, HEAD_PAD), lambda i, j: (0, i, 0))
    kspec = pl.BlockSpec((HEADS, tb, HEAD_PAD), lambda i, j: (0, jnp.minimum(i, j), 0))
    vspec = pl.BlockSpec((HEADS, HEAD_PAD, tb), lambda i, j: (0, 0, jnp.minimum(i, j)))
    return pl.pallas_call(
        body, grid=(nb, nb), name="attn_fwd",
        in_specs=[qspec, kspec, vspec],
        out_specs=[pl.BlockSpec((tb, MLA_WIDTH), lambda i, j: (i, 0)), pl.BlockSpec((HEADS, 1, tb), lambda i, j: (0, 0, i))],
        out_shape=[jax.ShapeDtypeStruct((T, MLA_WIDTH), F32), jax.ShapeDtypeStruct((HEADS, 1, T), F32)],
        scratch_shapes=[pltpu.VMEM((HEADS, 1, tb), F32), pltpu.VMEM((HEADS, 1, tb), F32), pltpu.VMEM((HEADS, HEAD_PAD, tb), F32)],
        compiler_params=_cparams(),
    )(q, k, vt)


def _sgu_fwd_chunk(vn_c, wcat, bz, masks):
    vstack = jnp.concatenate([jnp.where(mk, vn_c, 0.0).astype(BF16) for mk in masks], axis=0)
    return _dot(wcat, vstack) + bz


def _pool_counts(i, tm):
    pos1 = (i * tm + 1 + lax.broadcasted_iota(jnp.int32, (tm, POOL_WIDTH), 0)).astype(F32)
    lane = lax.broadcasted_iota(jnp.int32, (tm, POOL_WIDTH), 1)
    win = jnp.where(lane < 64, 2.0, jnp.where(lane < 128, 4.0, jnp.where(lane < 192, 8.0, 16.0)))
    return jnp.minimum(pos1, win), lane


def _by_group(lane, s2, s4, s8, s16):
    return jnp.where(lane < 64, s2, jnp.where(lane < 128, s4, jnp.where(lane < 192, s8, s16)))


def _pool_means(pin, halo, i, tm):
    s1 = jnp.concatenate([halo, pin], axis=0)
    s2 = s1 + pltpu.roll(s1, 1, 0)
    s4 = s2 + pltpu.roll(s2, 2, 0)
    s8 = s4 + pltpu.roll(s4, 4, 0)
    s16 = s8 + pltpu.roll(s8, 8, 0)
    cnt, lane = _pool_counts(i, tm)
    sel = _by_group(lane, s2[POOL_HALO:], s4[POOL_HALO:], s8[POOL_HALO:], s16[POOL_HALO:])
    return sel / cnt - pin


def _mixers_fwd_tile(i, tm, a, u, vs, pin, halo, gsgu, wcat, bz, wp, pscale, goa, gos, gop):
    vn, rv = _rms(vs, gsgu, SGU_WIDTH)
    masks = _head_masks((CHUNK, SGU_WIDTH), SGU_HEAD_DIM)
    zc = jnp.concatenate([_sgu_fwd_chunk(vn[CHUNK * c:CHUNK * (c + 1)], wcat, bz, masks) for c in range(tm // CHUNK)], axis=0)
    gm = u * zc
    halo = jnp.where(i > 0, halo, 0.0)
    m = _pool_means(pin, halo, i, tm).astype(BF16)
    yp_pre = _dot(m, wp)
    yp = yp_pre * pscale
    na, ra = _rms(a, goa, MLA_WIDTH)
    ng, rg = _rms(gm, gos, SGU_WIDTH)
    npo, rp = _rms(yp, gop, POOL_WIDTH)
    mix = jnp.concatenate([na, ng, npo], axis=1).astype(BF16)
    return vn, rv, zc, gm, m, yp_pre, yp, ra, rg, rp, mix


def _z_specs(tm):
    col = lambda c: pl.BlockSpec((tm, 256), lambda i: (i, c))
    halo = pl.BlockSpec((POOL_HALO, 256), lambda i: (jnp.maximum(i * (tm // POOL_HALO) - 1, 0), 4))
    return [col(2), col(3), col(4), halo]


def mixers_fwd(x, a, z, gsgu, wcat, bz, wp, pscale, goa, gos, gop, wout):
    T = x.shape[0]
    tm = min(ROW_TILE, T)

    def body(x_ref, a_ref, u_ref, vs_ref, pin_ref, halo_ref, gsgu_ref, wcat_ref, bz_ref, wp_ref, ps_ref,
             goa_ref, gos_ref, gop_ref, wout_ref, x1_ref):
        i = pl.program_id(0)
        mix = _mixers_fwd_tile(i, tm, a_ref[...], u_ref[...], vs_ref[...], pin_ref[...], halo_ref[...], gsgu_ref[...],
                               wcat_ref[...], bz_ref[...], wp_ref[...], ps_ref[...], goa_ref[...], gos_ref[...],
                               gop_ref[...])[-1]
        x1_ref[...] = x_ref[...] + _dot(mix, wout_ref[...])

    row = lambda w: pl.BlockSpec((tm, w), lambda i: (i, 0))
    return pl.pallas_call(
        body, grid=(T // tm,), name="mixers_fwd",
        in_specs=[row(D_MODEL), row(MLA_WIDTH)] + _z_specs(tm) + [WHOLE] * 9,
        out_specs=row(D_MODEL),
        out_shape=jax.ShapeDtypeStruct((T, D_MODEL), F32),
        compiler_params=_cparams(),
    )(x, a, z, z, z, z, gsgu, wcat, bz, wp, pscale, goa, gos, gop, wout)


def ffn_fwd(x1, gffn, wg, wu, wd, target=None):
    T = x1.shape[0]
    tm = min(ROW_TILE, T)
    with_loss = target is not None

    def body(*refs):
        x1_ref, gffn_ref, wg_ref, wu_ref, wd_ref = refs[:5]
        outs = refs[6:] if with_loss else refs[5:]
        x1v = x1_ref[...]
        h2 = _rms(x1v, gffn_ref[...], D_MODEL)[0].astype(BF16)
        acc = x1v
        for c in range(FFN_HIDDEN // FFN_CHUNK):
            sl = slice(FFN_CHUNK * c, FFN_CHUNK * (c + 1))
            g = _dot_nt(h2, wg_ref[sl, :])
            u = _dot_nt(h2, wu_ref[sl, :])
            outs[-2][:, sl] = g.astype(BF16)
            outs[-1][:, sl] = u.astype(BF16)
            act = (g * _sigmoid(g) * u).astype(BF16)
            acc = acc + _dot(act, wd_ref[sl, :])
        if not with_loss:
            outs[0][...] = acc
            return
        dy_ref, loss_ref = outs[0], outs[1]

        @pl.when(pl.program_id(0) == 0)
        def _():
            loss_ref[...] = jnp.zeros(loss_ref.shape, F32)

        err = acc - refs[5][...]
        dy_ref[...] = err * (1.0 / D_MODEL)
        per_row = jnp.sum(err * err, axis=1, keepdims=True) * (1.0 / D_MODEL)
        loss_ref[...] += 0.5 * jnp.sum(per_row, axis=0, keepdims=True)

    row = lambda w: pl.BlockSpec((tm, w), lambda i: (i, 0))
    hidden = [jax.ShapeDtypeStruct((T, FFN_HIDDEN), BF16)] * 2
    if with_loss:
        return pl.pallas_call(
            body, grid=(T // tm,), name="ffn_fwd_loss",
            in_specs=[row(D_MODEL)] + [WHOLE] * 4 + [row(D_MODEL)],
            out_specs=[row(D_MODEL), pl.BlockSpec((1, 1), lambda i: (0, 0)), row(FFN_HIDDEN), row(FFN_HIDDEN)],
            out_shape=[jax.ShapeDtypeStruct((T, D_MODEL), F32), jax.ShapeDtypeStruct((1, 1), F32)] + hidden,
            compiler_params=_cparams(),
        )(x1, gffn, wg, wu, wd, target)
    return pl.pallas_call(
        body, grid=(T // tm,), name="ffn_fwd",
        in_specs=[row(D_MODEL)] + [WHOLE] * 4,
        out_specs=[row(D_MODEL), row(FFN_HIDDEN), row(FFN_HIDDEN)],
        out_shape=[jax.ShapeDtypeStruct((T, D_MODEL), F32)] + hidden,
        compiler_params=_cparams(),
    )(x1, gffn, wg, wu, wd)


def _acc_spec(shape):
    return pl.BlockSpec(shape, lambda i: (0,) * len(shape))


def ffn_bwd(dx2, x1, gs, us, gffn, wg, wu, wd):
    T = x1.shape[0]
    tm = min(ROW_TILE // 2, T)

    def body(dx2_ref, x1_ref, gs_ref, us_ref, gffn_ref, wg_ref, wu_ref, wd_ref,
             dx1_ref, h2_ref, act_ref, dg_ref, du_ref, dgffn_ref):
        @pl.when(pl.program_id(0) == 0)
        def _():
            dgffn_ref[...] = jnp.zeros(dgffn_ref.shape, F32)

        dx2v = dx2_ref[...]
        dy = dx2v.astype(BF16)
        x1v = x1_ref[...]
        h2, r = _rms(x1v, gffn_ref[...], D_MODEL)
        h2_ref[...] = h2.astype(BF16)
        for c in range(FFN_HIDDEN // FFN_CHUNK):
            sl = slice(FFN_CHUNK * c, FFN_CHUNK * (c + 1))
            g = gs_ref[:, sl].astype(F32)
            u = us_ref[:, sl].astype(F32)
            dact = _dot_nt(dy, wd_ref[sl, :])
            sg = _sigmoid(g)
            silu = g * sg
            act_ref[:, sl] = (silu * u).astype(BF16)
            dg_ref[:, sl] = (dact * u * (sg * (1.0 + g * (1.0 - sg)))).astype(BF16)
            du_ref[:, sl] = (dact * silu).astype(BF16)
        dh2 = _dot(dg_ref[...], wg_ref[...]) + _dot(du_ref[...], wu_ref[...])
        dxn, dgn = _rms_bwd(x1v, r, gffn_ref[...], dh2, D_MODEL)
        dx1_ref[...] = dx2v + dxn
        dgffn_ref[...] += dgn

    row = lambda w: pl.BlockSpec((tm, w), lambda i: (i, 0))
    return pl.pallas_call(
        body, grid=(T // tm,), name="ffn_bwd",
        in_specs=[row(D_MODEL), row(D_MODEL), row(FFN_HIDDEN), row(FFN_HIDDEN)] + [WHOLE] * 4,
        out_specs=[row(D_MODEL), row(D_MODEL), row(FFN_HIDDEN), row(FFN_HIDDEN), row(FFN_HIDDEN), _acc_spec((1, D_MODEL))],
        out_shape=[jax.ShapeDtypeStruct((T, D_MODEL), F32), jax.ShapeDtypeStruct((T, D_MODEL), BF16),
                   jax.ShapeDtypeStruct((T, FFN_HIDDEN), BF16), jax.ShapeDtypeStruct((T, FFN_HIDDEN), BF16),
                   jax.ShapeDtypeStruct((T, FFN_HIDDEN), BF16), jax.ShapeDtypeStruct((1, D_MODEL), F32)],
        compiler_params=_cparams(),
    )(dx2, x1, gs, us, gffn, wg, wu, wd)


TN_K_TILE = 2048
TN_ACC_BYTES = 6 * 1024 * 1024


def matmul_tn(a, b, name):
    T, M = a.shape
    N = b.shape[1]
    tk = min(TN_K_TILE, T)
    tm = M if M <= 1024 else M // 2
    tn = max(d for d in range(128, N + 1, 128) if N % d == 0 and tm * d * 4 <= TN_ACC_BYTES)
    nk = T // tk

    def body(a_ref, b_ref, o_ref, acc):
        k = pl.program_id(2)

        @pl.when(k == 0)
        def _():
            acc[...] = jnp.zeros(acc.shape, F32)

        acc[...] += _dot_tn(a_ref[...].astype(BF16), b_ref[...].astype(BF16))

        @pl.when(k == nk - 1)
        def _():
            o_ref[...] = acc[...].astype(BF16)

    return pl.pallas_call(
        body, grid=(M // tm, N // tn, nk), name=name,
        in_specs=[pl.BlockSpec((tk, tm), lambda i, j, k: (k, i)), pl.BlockSpec((tk, tn), lambda i, j, k: (k, j))],
        out_specs=pl.BlockSpec((tm, tn), lambda i, j, k: (i, j)),
        out_shape=jax.ShapeDtypeStruct((M, N), BF16),
        scratch_shapes=[pltpu.VMEM((tm, tn), F32)],
        compiler_params=_cparams(),
    )(a, b)


def mixers_bwd(dx1, a, z, gsgu, wcat, wcat_t, bz, wp, pscale, goa, gos, gop, wout):
    T = a.shape[0]
    tm = min(ROW_TILE, T)

    def body(dx1_ref, a_ref, u_ref, vs_ref, pin_ref, halo_ref, gsgu_ref, wcat_ref, wcatt_ref, bz_ref, wp_ref,
             ps_ref, goa_ref, gos_ref, gop_ref, wout_ref,
             da_ref, delta_ref, du_ref, dvs_ref, dm_ref, mix_ref,
             dgoa_ref, dgos_ref, dgop_ref, dps_ref, dwp_ref, dwsp_ref, db_ref, dgsgu_ref):
        i = pl.program_id(0)

        @pl.when(i == 0)
        def _():
            for r in (dgoa_ref, dgos_ref, dgop_ref, dps_ref, dwp_ref, dwsp_ref, db_ref, dgsgu_ref):
                r[...] = jnp.zeros(r.shape, F32)

        a_v, u, vs = a_ref[...], u_ref[...], vs_ref[...]
        goa, gos, gop, pscale_v = goa_ref[...], gos_ref[...], gop_ref[...], ps_ref[...]
        vn, rv, zc, gm, m, yp_pre, yp, ra, rg, rp, mix = _mixers_fwd_tile(
            i, tm, a_v, u, vs, pin_ref[...], halo_ref[...], gsgu_ref[...], wcat_ref[...], bz_ref[...], wp_ref[...],
            pscale_v, goa, gos, gop)
        mix_ref[...] = mix
        dmix = _dot_nt(dx1_ref[...].astype(BF16), wout_ref[...])
        da, dgoa = _rms_bwd(a_v, ra, goa, dmix[:, :MLA_WIDTH], MLA_WIDTH)
        dgm, dgos = _rms_bwd(gm, rg, gos, dmix[:, MLA_WIDTH:MLA_WIDTH + SGU_WIDTH], SGU_WIDTH)
        dyp, dgop = _rms_bwd(yp, rp, gop, dmix[:, MLA_WIDTH + SGU_WIDTH:], POOL_WIDTH)
        da_ref[...] = da
        dgoa_ref[...] += dgoa
        dgos_ref[...] += dgos
        dgop_ref[...] += dgop
        prod = da * a_v
        ones = jnp.ones((8, HEAD_PAD), F32)
        for h in range(HEADS):
            lo = HEAD_PAD * h
            sums = lax.dot_general(ones, prod[:, lo:lo + HEAD_PAD], (((1,), (1,)), ((), ())), preferred_element_type=F32,
                                   precision=lax.Precision.HIGHEST)
            delta_ref[h] = sums[0:1, :]
        dps_ref[...] += jnp.sum(dyp * yp_pre, axis=0, keepdims=True)
        dyp_pre = (dyp * pscale_v).astype(BF16)
        dwp_ref[...] += _dot_tn(m, dyp_pre)
        dm_ref[...] = _dot_nt(dyp_pre, wp_ref[...])
        du_ref[...] = dgm * zc
        dzc = dgm * u
        masks = _head_masks((CHUNK, SGU_WIDTH), SGU_HEAD_DIM)
        lane_b = lax.broadcasted_iota(jnp.int32, (CHUNK, HEAD_PAD), 1)
        dvn_parts = []
        dwsp = jnp.zeros(dwsp_ref.shape, F32)
        db = jnp.zeros(db_ref.shape, F32)
        for c in range(tm // CHUNK):
            dz_c = dzc[CHUNK * c:CHUNK * (c + 1)]
            dzstack = jnp.concatenate([jnp.where(mk, dz_c, 0.0).astype(BF16) for mk in masks], axis=0)
            dvn_parts.append(_dot(wcatt_ref[...], dzstack))
            dwsp = dwsp + _dot_nt(dzstack, vn[CHUNK * c:CHUNK * (c + 1)].astype(BF16))
            for h, mk in enumerate(masks):
                col = jnp.sum(jnp.where(mk, dz_c, 0.0), axis=1, keepdims=True)
                db = db + jnp.where(lane_b == h, col, 0.0)
        dwsp_ref[...] += dwsp
        db_ref[...] += db
        dvs, dgsgu = _rms_bwd(vs, rv, gsgu_ref[...], jnp.concatenate(dvn_parts, axis=0), SGU_WIDTH)
        dvs_ref[...] = dvs
        dgsgu_ref[...] += dgsgu

    row = lambda w: pl.BlockSpec((tm, w), lambda i: (i, 0))
    head = pl.BlockSpec((HEADS, 1, tm), lambda i: (0, 0, i))
    acc_shapes = [(1, MLA_WIDTH), (1, SGU_WIDTH), (1, POOL_WIDTH), (1, POOL_WIDTH), (POOL_WIDTH, POOL_WIDTH),
                  (HEADS * CHUNK, CHUNK), (CHUNK, HEAD_PAD), (1, SGU_WIDTH)]
    return pl.pallas_call(
        body, grid=(T // tm,), name="mixers_bwd",
        in_specs=[row(D_MODEL), row(MLA_WIDTH)] + _z_specs(tm) + [WHOLE] * 10,
        out_specs=[row(MLA_WIDTH), head, row(256), row(256), row(256), row(D_MODEL)] + [_acc_spec(s) for s in acc_shapes],
        out_shape=[jax.ShapeDtypeStruct((T, MLA_WIDTH), F32), jax.ShapeDtypeStruct((HEADS, 1, T), F32),
                   jax.ShapeDtypeStruct((T, 256), F32), jax.ShapeDtypeStruct((T, 256), F32),
                   jax.ShapeDtypeStruct((T, 256), F32), jax.ShapeDtypeStruct((T, D_MODEL), BF16)]
                  + [jax.ShapeDtypeStruct(s, F32) for s in acc_shapes],
        compiler_params=_cparams(),
    )(dx1, a, z, z, z, z, gsgu, wcat, wcat_t, bz, wp, pscale, goa, gos, gop, wout)


def pool_bwd(dm):
    T = dm.shape[0]
    tm = min(ROW_TILE, T)
    nt = T // tm

    def body(dm_ref, next_ref, dpin_ref):
        i = pl.program_id(0)
        cnt, lane = _pool_counts(i, tm)
        dmv = dm_ref[...]
        win = _by_group(lane[:POOL_HALO], 2.0, 4.0, 8.0, 16.0)
        nxt = jnp.where(i < nt - 1, next_ref[...] / win, 0.0)
        r1 = jnp.concatenate([dmv / cnt, nxt], axis=0)
        n = tm + POOL_HALO
        r2 = r1 + pltpu.roll(r1, n - 1, 0)
        r4 = r2 + pltpu.roll(r2, n - 2, 0)
        r8 = r4 + pltpu.roll(r4, n - 4, 0)
        r16 = r8 + pltpu.roll(r8, n - 8, 0)
        dpin_ref[...] = _by_group(lane, r2[:tm], r4[:tm], r8[:tm], r16[:tm]) - dmv

    return pl.pallas_call(
        body, grid=(nt,), name="pool_bwd",
        in_specs=[pl.BlockSpec((tm, 256), lambda i: (i, 0)),
                  pl.BlockSpec((POOL_HALO, 256), lambda i: (jnp.minimum((i + 1) * (tm // POOL_HALO), T // POOL_HALO - 1), 0))],
        out_specs=pl.BlockSpec((tm, 256), lambda i: (i, 0)),
        out_shape=jax.ShapeDtypeStruct((T, 256), F32),
        compiler_params=_cparams(),
    )(dm, dm)


def attn_bwd(q, k, kt, v, do, lse, delta):
    _, T, _ = q.shape
    tb = min(ATTN_TILE, T)
    sb = min(ATTN_SUB, tb)
    ns = tb // sb
    nb = T // tb
    hb = ATTN_BWD_HEADS

    def body(q_ref, k_ref, kt_ref, v_ref, do_ref, lse_ref, delta_ref, dqt_ref, dk_ref, dv_ref, dk_s, dv_s):
        i, j = pl.program_id(1), pl.program_id(2)

        @pl.when((i == 0) & (j == 0))
        def _():
            dqt_ref[...] = jnp.zeros(dqt_ref.shape, F32)

        @pl.when(j == i)
        def _():
            dk_s[...] = jnp.zeros(dk_s.shape, F32)
            dv_s[...] = jnp.zeros(dv_s.shape, F32)

        def sub_block(h, a, b, masked):
            qa = slice(sb * a, sb * (a + 1))
            kb = slice(sb * b, sb * (b + 1))
            qv = q_ref[h, qa, :]
            dov = do_ref[qa, HEAD_PAD * h:HEAD_PAD * (h + 1)].astype(BF16)
            st = _dot_nt(k_ref[h, kb, :], qv)
            dpt = _dot_nt(v_ref[h, kb, :], dov)
            pt = jnp.exp2(st - lse_ref[h, :, qa])
            if masked:
                krow = lax.broadcasted_iota(jnp.int32, st.shape, 0)
                qcol = lax.broadcasted_iota(jnp.int32, st.shape, 1)
                pt = jnp.where(krow <= qcol, pt, 0.0)
            dst = (pt * (dpt - delta_ref[h, :, qa])).astype(BF16)
            dv_s[h, kb, :] += _dot(pt.astype(BF16), dov)
            dk_s[h, kb, :] += _dot(dst, qv)
            cols = pl.ds(pl.multiple_of(j * tb + sb * a, sb), sb)
            dqt_ref[h, :, cols] += _dot(kt_ref[h, :, kb], dst)

        @pl.when(j > i)
        def _():
            for a in range(ns):
                for h in range(hb):
                    for b in range(ns):
                        sub_block(h, a, b, False)

        @pl.when(j == i)
        def _():
            for a in range(ns):
                for h in range(hb):
                    for b in range(a + 1):
                        sub_block(h, a, b, a == b)

        @pl.when(j == nb - 1)
        def _():
            dk_ref[...] = dk_s[...] * (1.0 / LOG2E)
            dv_ref[...] = dv_s[...]

    qspec = pl.BlockSpec((hb, tb, HEAD_PAD), lambda g, i, j: (g, jnp.maximum(i, j), 0))
    kspec = pl.BlockSpec((hb, tb, HEAD_PAD), lambda g, i, j: (g, i, 0))
    ktspec = pl.BlockSpec((hb, HEAD_PAD, tb), lambda g, i, j: (g, 0, i))
    rowspec = pl.BlockSpec((hb, 1, tb), lambda g, i, j: (g, 0, jnp.maximum(i, j)))
    return pl.pallas_call(
        body, grid=(HEADS // hb, nb, nb), name="attn_bwd",
        in_specs=[qspec, kspec, ktspec, kspec, pl.BlockSpec((tb, hb * HEAD_PAD), lambda g, i, j: (jnp.maximum(i, j), g)), rowspec, rowspec],
        out_specs=[pl.BlockSpec((hb, HEAD_PAD, T), lambda g, i, j: (g, 0, 0)), kspec, kspec],
        out_shape=[jax.ShapeDtypeStruct((HEADS, HEAD_PAD, T), F32)] + [jax.ShapeDtypeStruct((HEADS, T, HEAD_PAD), F32)] * 2,
        scratch_shapes=[pltpu.VMEM((hb, tb, HEAD_PAD), F32), pltpu.VMEM((hb, tb, HEAD_PAD), F32)],
        compiler_params=_cparams(),
    )(q, k, kt, v, do, lse, delta)


def mixin_bwd(dres, x, z, rope, dqt, dk, dv, du, dvs, dpin, gmix, win, gql, wq, gkvl, wkv, gqh, gkh):
    T = x.shape[0]
    tm = min(ROW_TILE, T)
    part = min(MIXIN_BWD_PART, tm)

    def body(dres_ref, x_ref, z_ref, c_ref, sa_ref, sb_ref, dqt_ref, dk_ref, dv_ref, du_ref, dvs_ref, dpin_ref,
             gmix_ref, win_ref, gql_ref, wq_ref, gkvl_ref, wkv_ref, gqh_ref, gkh_ref,
             dx_ref, hn_ref, dz_ref, qn_ref, dqup_ref, kvn_ref, dkvup_ref,
             dgmix_ref, dgql_ref, dgkvl_ref, dgqh_ref, dgkh_ref):
        @pl.when(pl.program_id(0) == 0)
        def _():
            for r in (dgmix_ref, dgql_ref, dgkvl_ref, dgqh_ref, dgkh_ref):
                r[...] = jnp.zeros(r.shape, F32)

        lane = lax.broadcasted_iota(jnp.int32, (part, HEAD_PAD), 1)
        rope_lanes = (lane >= QK_NOPE) & (lane < QK_HEAD)

        def recompute(g):
            rows = g["rows"]
            g["xv"] = x_ref[rows, :]
            hn, g["rx"] = _rms(g["xv"], gmix_ref[...], D_MODEL)
            hn_ref[rows, :] = hn.astype(BF16)
            g["ql"], g["kvl"], g["kr"], qn, g["rq"], kvn, g["rkv"], g["q_up"], g["kv_up"] = _qkv_pre(
                z_ref[rows, :], gql_ref[...], wq_ref[...], gkvl_ref[...], wkv_ref[...])
            qn_ref[rows, :] = qn
            kvn_ref[rows, :] = kvn

        def heads(g):
            rows = g["rows"]
            c, sa, sb = c_ref[rows, :], sa_ref[rows, :], sb_ref[rows, :]
            dkr = jnp.zeros((part, HEAD_PAD), F32)
            dgqh = jnp.zeros((1, HEAD_PAD), F32)
            dgkh = jnp.zeros((1, HEAD_PAD), F32)
            dq_parts, dk_parts, dv_parts = [], [], []
            for h in range(HEADS):
                lo = HEAD_PAD * h
                qh = g["q_up"][:, lo:lo + HEAD_PAD]
                rqh = lax.rsqrt(jnp.sum(qh * qh, axis=-1, keepdims=True) * (1.0 / QK_HEAD) + EPS)
                dq_h = jnp.transpose(dqt_ref[h, :, rows]) * ATTN_SCALE
                dqh, dg = _rms_bwd(qh, rqh, gqh_ref[...], _rope_bwd(dq_h, c, sa, sb), QK_HEAD)
                dgqh = dgqh + dg
                dq_parts.append(dqh)
                kh = g["kv_up"][:, lo:lo + HEAD_PAD] + g["kr"]
                rkh = lax.rsqrt(jnp.sum(kh * kh, axis=-1, keepdims=True) * (1.0 / QK_HEAD) + EPS)
                dkh, dg = _rms_bwd(kh, rkh, gkh_ref[...], _rope_bwd(dk_ref[h, rows, :], c, sa, sb), QK_HEAD)
                dgkh = dgkh + dg
                dkr = dkr + jnp.where(rope_lanes, dkh, 0.0)
                dk_parts.append(dkh)
                dv_parts.append(dv_ref[h, rows, :])
            dgqh_ref[...] += dgqh
            dgkh_ref[...] += dgkh
            g["dkr"] = dkr
            g["dq_up"] = jnp.concatenate(dq_parts, axis=1).astype(BF16)
            g["dkv_up"] = jnp.concatenate(dk_parts + dv_parts, axis=1).astype(BF16)
            dqup_ref[rows, :] = g["dq_up"]
            dkvup_ref[rows, :] = g["dkv_up"]

        def latents(g):
            rows = g["rows"]
            dql, dg = _rms_bwd(g["ql"], g["rq"], gql_ref[...], _dot(g["dq_up"], wq_ref[...]), Q_LORA)
            dgql_ref[...] += dg
            dkvl, dg = _rms_bwd(g["kvl"], g["rkv"], gkvl_ref[...], _dot(g["dkv_up"], wkv_ref[...]), KV_LORA)
            dgkvl_ref[...] += dg
            g["dz"] = jnp.concatenate([dql, dkvl, g["dkr"], du_ref[rows, :], dvs_ref[rows, :], dpin_ref[rows, :]],
                                      axis=1).astype(BF16)
            dz_ref[rows, :] = g["dz"]

        def inputs(g):
            rows = g["rows"]
            dxn, dg = _rms_bwd(g["xv"], g["rx"], gmix_ref[...], _dot(g["dz"], win_ref[...]), D_MODEL)
            dgmix_ref[...] += dg
            dx_ref[rows, :] = dres_ref[rows, :] + dxn

        _skewed((recompute, heads, latents, inputs), [dict(rows=pl.ds(part * p, part)) for p in range(tm // part)])

    row = lambda w: pl.BlockSpec((tm, w), lambda i: (i, 0))
    head = pl.BlockSpec((HEADS, tm, HEAD_PAD), lambda i: (0, i, 0))
    head_t = pl.BlockSpec((HEADS, HEAD_PAD, tm), lambda i: (0, 0, i))
    acc_shapes = [(1, D_MODEL), (1, Q_LORA), (1, KV_LORA), (1, HEAD_PAD), (1, HEAD_PAD)]
    out_rows = [(D_MODEL, F32), (D_MODEL, BF16), (Z_WIDTH, BF16), (Q_LORA, BF16), (HEADS * HEAD_PAD, BF16),
                (KV_LORA, BF16), (2 * HEADS * HEAD_PAD, BF16)]
    return pl.pallas_call(
        body, grid=(T // tm,), name="mixin_bwd",
        in_specs=[row(D_MODEL), row(D_MODEL), row(Z_WIDTH)] + [row(HEAD_PAD)] * 3 + [head_t, head, head, row(256), row(256), row(256)]
                 + [WHOLE] * 8,
        out_specs=[row(w) for w, _ in out_rows] + [_acc_spec(s) for s in acc_shapes],
        out_shape=[jax.ShapeDtypeStruct((T, w), dt) for w, dt in out_rows] + [jax.ShapeDtypeStruct(s, F32) for s in acc_shapes],
        compiler_params=_cparams(),
    )(dres, x, z, *rope, dqt, dk, dv, du, dvs, dpin, gmix, win, gql, wq, gkvl, wkv, gqh, gkh)


def _place():
    x, y, c = lax.axis_index("x"), lax.axis_index("y"), lax.axis_index("c")
    return x, y, c, 4 * x + 2 * y + c


def _layer_of(ref, l):
    return ref[:, l, :] if ref.shape[1] == DEPTH else ref[l]


def _layer_shape(shard):
    return (shard.shape[0], shard.shape[2]) if shard.shape[1] == DEPTH else shard.shape[1:]


def cast_shards(shards, wanted):
    n = len(shards)

    def body(*refs):
        for o_ref, (w, l) in zip(refs[n:], wanted):
            o_ref[...] = _layer_of(refs[w], l).astype(BF16)

    return pl.pallas_call(
        body, name="cast_shards", in_specs=[WHOLE] * n, out_specs=[WHOLE] * len(wanted),
        out_shape=[jax.ShapeDtypeStruct(_layer_shape(shards[w]), BF16) for w, _ in wanted],
        compiler_params=_cparams(),
    )(*shards)


def allgather_layer0(shards):
    n = len(shards)

    def body(*refs):
        ins, outs = refs[:n], refs[n:2 * n]
        stage = refs[2 * n:3 * n]
        send_sems, recv_sems, local_sems = refs[3 * n:]
        x, y, c, me = _place()
        sibling = (x, y, 1 - c)
        chips = [(1 - x, y), (x, 1 - y), (1 - x, 1 - y)]

        def copy(k, w, block_id, to, from_stage):
            return pltpu.make_async_remote_copy(
                src_ref=stage[w] if from_stage else outs[w].at[block_id], dst_ref=outs[w].at[block_id],
                send_sem=send_sems.at[k, w], recv_sem=recv_sems.at[k, w], device_id=to, device_id_type=MESH)

        def block_of(cx, cy, cc):
            return 4 * cx + 2 * cy + cc

        local, sent = [], []
        for w in range(n):
            stage[w][...] = _layer_of(ins[w], 0).astype(BF16)
            mine = pltpu.make_async_copy(stage[w], outs[w].at[me], local_sems.at[w])
            mine.start()
            local.append(mine)
            first = [copy(0, w, me, sibling, True)] + [copy(1 + j, w, me, (*chip, c), True) for j, chip in enumerate(chips)]
            for cp in first:
                cp.start()
            sent += first
        for j, chip in enumerate(chips):
            for w in range(n):
                copy(1 + j, w, block_of(*chip, c), (x, y, c), False).wait_recv()
                fwd = copy(4 + j, w, block_of(*chip, c), sibling, False)
                fwd.start()
                sent.append(fwd)
        for w in range(n):
            copy(0, w, block_of(x, y, 1 - c), (x, y, c), False).wait_recv()
            for j, chip in enumerate(chips):
                copy(4 + j, w, block_of(*chip, 1 - c), (x, y, c), False).wait_recv()
        for cp in local:
            cp.wait()
        for cp in sent:
            cp.wait_send()

    return pl.pallas_call(
        body, name="allgather_layer0",
        in_specs=[WHOLE] * n, out_specs=[ANY] * n,
        out_shape=[jax.ShapeDtypeStruct((N_DEV,) + _layer_shape(s), BF16) for s in shards],
        scratch_shapes=[pltpu.VMEM(_layer_shape(s), BF16) for s in shards]
                       + [pltpu.SemaphoreType.DMA((7, n)), pltpu.SemaphoreType.DMA((7, n)), pltpu.SemaphoreType.DMA((n,))],
        compiler_params=_cparams(),
    )(*shards)


def _peer(k):
    x, y, c, _ = _place()
    px = 1 - x if k & 4 else x
    py = 1 - y if k & 2 else y
    pc = 1 - c if k & 1 else c
    return (px, py, pc), 4 * px + 2 * py + pc


def exchange_start(srcs, after, gather, name):
    n = len(srcs)
    land_shapes = [((N_DEV,) + s.shape) if gather else s.shape for s in srcs]

    def body(*refs):
        src_refs, land_refs = refs[:n], refs[n:2 * n]
        send_sems, recv_sems = refs[2 * n + 1:3 * n + 1], refs[3 * n + 1:4 * n + 1]
        token = refs[-1]
        _, _, _, me = _place()
        for k in range(1, N_DEV):
            peer, peer_id = _peer(k)
            for w in range(n):
                pltpu.make_async_remote_copy(
                    src_ref=src_refs[w] if gather else src_refs[w].at[peer_id], dst_ref=land_refs[w].at[me],
                    send_sem=send_sems[w], recv_sem=recv_sems[w], device_id=peer, device_id_type=MESH).start()
        token[...] = jnp.zeros(token.shape, F32)

    hbm = lambda a: pltpu.with_memory_space_constraint(a, pltpu.HBM)
    outs = pl.pallas_call(
        body, name=name,
        out_shape=(pltpu.SemaphoreType.DMA(()),) * (2 * n)
                  + tuple(pltpu.HBM(s.shape, BF16) for s in srcs) + tuple(pltpu.HBM(s, BF16) for s in land_shapes)
                  + (jax.ShapeDtypeStruct((8, 128), F32),),
        in_specs=[HBM_SPEC] * (2 * n) + [ANY],
        out_specs=(SEM_SPEC,) * (2 * n) + (HBM_SPEC,) * (2 * n) + (WHOLE,),
        input_output_aliases={i: 2 * n + i for i in range(2 * n)},
        compiler_params=pltpu.CompilerParams(has_side_effects=pltpu.SideEffectType.DATAFLOW_SIDE_EFFECTING),
    )(*[hbm(s) for s in srcs], *[hbm(lax.empty(s, BF16)) for s in land_shapes], after)
    return list(outs[:n]), list(outs[n:2 * n]), list(outs[2 * n:3 * n]), list(outs[3 * n:4 * n]), outs[-1]


def exchange_wait(started, after, name):
    send_sems, recv_sems, srcs, lands, _ = started
    n = len(srcs)

    def body(*refs):
        land_refs = refs[n:2 * n]
        send_sems, recv_sems = refs[2 * n:3 * n], refs[3 * n:4 * n]
        x, y, c, _ = _place()
        for w in range(n):
            seven = land_refs[w].at[pl.ds(0, N_DEV - 1)]
            cp = pltpu.make_async_remote_copy(src_ref=seven, dst_ref=seven, send_sem=send_sems[w], recv_sem=recv_sems[w],
                                              device_id=(x, y, c), device_id_type=MESH)
            cp.wait_send()
            cp.wait_recv()

    outs = pl.pallas_call(
        body, name=name,
        out_shape=tuple(pltpu.HBM(s.shape, BF16) for s in srcs) + tuple(pltpu.HBM(l.shape, BF16) for l in lands),
        in_specs=[HBM_SPEC] * (2 * n) + [SEM_SPEC] * (2 * n) + [ANY],
        out_specs=(HBM_SPEC,) * (2 * n),
        input_output_aliases={i: i for i in range(2 * n)},
        compiler_params=pltpu.CompilerParams(has_side_effects=pltpu.SideEffectType.DATAFLOW_SIDE_EFFECTING),
    )(*srcs, *lands, *send_sems, *recv_sems, after)
    _, _, _, me = _place()
    filled = []
    for src, land in zip(outs[:n], outs[n:]):
        own = src[None] if src.ndim == 2 else lax.dynamic_slice_in_dim(src, me, 1, axis=0)
        filled.append(lax.dynamic_update_slice_in_dim(land, own, me, axis=0))
    return filled


def _adamw(w, g, m, v):
    m2 = ADAM_B1 * m + (1.0 - ADAM_B1) * g
    v2 = ADAM_B2 * v + (1.0 - ADAM_B2) * (g * g)
    m_hat = m2 / (1.0 - ADAM_B1 ** ADAM_STEP)
    v_hat = v2 / (1.0 - ADAM_B2 ** ADAM_STEP)
    delta = -ADAM_LR * (m_hat / (jnp.sqrt(v_hat) + ADAM_EPS) + ADAM_WD * w)
    return delta, m2, v2


def adamw_sharded(parts, w, m, v, name):
    L, R, C = w.shape
    fits = [d for d in range(16, min(R, 512) + 1, 16) if R % d == 0]
    br = max(fits) if fits else R
    nblk = R // br

    def body(*refs):
        p_refs = refs[:L]
        w_ref, m_ref, v_ref, g_ref, d_ref, m2_ref, v2_ref = refs[L:]

        def total(p_ref):
            g = p_ref[0].astype(F32)
            for s in range(1, N_DEV):
                g = g + p_ref[s].astype(F32)
            return g

        g = total(p_refs[0])
        for l in range(1, L):
            g = jnp.where(pl.program_id(0) == l, total(p_refs[l]), g)
        g_ref[...] = g
        d_ref[...], m2_ref[...], v2_ref[...] = _adamw(w_ref[...], g, m_ref[...], v_ref[...])

    blk = pl.BlockSpec((None, br, C), lambda l, i: (l, i, 0))

    def part_spec(k):
        return pl.BlockSpec((N_DEV, br, C), lambda l, i: (0, jnp.where(l == k, i, jnp.where(l < k, 0, nblk - 1)), 0))

    return pl.pallas_call(
        body, grid=(L, nblk), name=name,
        in_specs=[part_spec(k) for k in range(L)] + [blk, blk, blk],
        out_specs=[blk] * 4,
        out_shape=[jax.ShapeDtypeStruct((L, R, C), F32)] * 4,
        compiler_params=_cparams(),
    )(*parts, w, m, v)


def allreduce_adamw_small(gpart, w, m, v):
    R = gpart.shape[0]

    def body(g_ref, w_ref, m_ref, v_ref, grad_ref, d_ref, m2_ref, v2_ref, all_ref, send_sems, recv_sems):
        x, y, c, me = _place()
        sibling = (x, y, 1 - c)
        chips = [(1 - x, y), (x, 1 - y), (1 - x, 1 - y)]

        def copy(k, block_id, to, from_input):
            return pltpu.make_async_remote_copy(
                src_ref=g_ref if from_input else all_ref.at[block_id], dst_ref=all_ref.at[block_id],
                send_sem=send_sems.at[k], recv_sem=recv_sems.at[k], device_id=to, device_id_type=MESH)

        def block_of(cx, cy, cc):
            return 4 * cx + 2 * cy + cc

        all_ref[me] = g_ref[...]
        first = [copy(0, me, sibling, True)] + [copy(1 + j, me, (*chip, c), True) for j, chip in enumerate(chips)]
        for cp in first:
            cp.start()
        passed = [copy(4 + j, block_of(*chip, c), sibling, False) for j, chip in enumerate(chips)]
        for j, chip in enumerate(chips):
            copy(1 + j, block_of(*chip, c), (x, y, c), False).wait_recv()
            passed[j].start()
        copy(0, block_of(x, y, 1 - c), (x, y, c), False).wait_recv()
        for j, chip in enumerate(chips):
            copy(4 + j, block_of(*chip, 1 - c), (x, y, c), False).wait_recv()
        for cp in first + passed:
            cp.wait_send()
        g = all_ref[0]
        for s in range(1, N_DEV):
            g = g + all_ref[s]
        grad_ref[...] = g
        d_ref[...], m2_ref[...], v2_ref[...] = _adamw(w_ref[...], g, m_ref[...], v_ref[...])

    return pl.pallas_call(
        body, name="allreduce_adamw_small",
        in_specs=[WHOLE] * 4, out_specs=[WHOLE] * 4,
        out_shape=[jax.ShapeDtypeStruct((R, 128), F32)] * 4,
        scratch_shapes=[pltpu.VMEM((N_DEV, R, 128), F32), pltpu.SemaphoreType.DMA((7,)), pltpu.SemaphoreType.DMA((7,))],
        compiler_params=_cparams(),
    )(gpart, w, m, v)


def _shard_view(name, a):
    if name == "w_in":
        return a.transpose(2, 0, 1)
    return a.swapaxes(1, 2) if name in TRANSPOSED else a


def _shard_unview(name, a):
    if name == "w_in":
        return a.transpose(1, 2, 0)
    return a.swapaxes(1, 2) if name in TRANSPOSED else a


def _pad_head_rows(w, width):
    c = w.shape[1]
    return jnp.pad(w.reshape(HEADS, width, c), ((0, 0), (0, HEAD_PAD - width), (0, 0))).reshape(HEADS * HEAD_PAD, c)


def _unpad_head_rows(w, width):
    c = w.shape[1]
    return w.reshape(HEADS, HEAD_PAD, c)[:, :width].reshape(HEADS * width, c)


O1 = Q_LORA
O2 = O1 + KV_LORA
O3 = O2 + QK_ROPE


def _mixer_weights(gw):
    w_in = gw["w_in"].reshape(IN_WIDTH, D_MODEL)
    zero = lambda n: jnp.zeros((n, D_MODEL), BF16)
    win = jnp.concatenate([w_in[:O2], zero(QK_NOPE), w_in[O2:O3], zero(HEAD_PAD - QK_HEAD), w_in[O3:]], axis=0)
    wq = _pad_head_rows(gw["w_q_up"].reshape(HEADS * QK_HEAD, Q_LORA), QK_HEAD)
    w_kv = gw["w_kv_up"].reshape(HEADS, QK_NOPE + V_HEAD, KV_LORA)
    wk = jnp.pad(w_kv[:, :QK_NOPE], ((0, 0), (0, HEAD_PAD - QK_NOPE), (0, 0))).reshape(HEADS * HEAD_PAD, KV_LORA)
    wv = w_kv[:, QK_NOPE:].reshape(HEADS * V_HEAD, KV_LORA)
    wkv = jnp.concatenate([wk, wv], axis=0)
    out = dict(win=win, wq=wq, wkv=wkv)
    if "w_out" in gw:
        out["wout"] = gw["w_out"].reshape(D_MODEL, D_MODEL)
    return out


def _ffn_weights(gw):
    return dict(wg=gw["w_gate"].reshape(FFN_HIDDEN, D_MODEL), wu=gw["w_up"].reshape(FFN_HIDDEN, D_MODEL),
                wd=gw["w_down"].reshape(FFN_HIDDEN, D_MODEL))


def _layer_small(p, l):
    row = lambda a: a.reshape(1, -1)
    pad_head = lambda g: jnp.pad(g, (0, HEAD_PAD - QK_HEAD)).reshape(1, HEAD_PAD)
    tril = jnp.tril(jnp.ones((CHUNK, CHUNK), F32))
    wsp = p["w_spatial"][l] * tril
    wcat = jnp.concatenate([wsp[h] for h in range(HEADS)], axis=1).astype(BF16)
    wcat_t = jnp.concatenate([wsp[h].T for h in range(HEADS)], axis=1).astype(BF16)
    bz = jnp.repeat(p["b_spatial"][l].T, SGU_HEAD_DIM, axis=1)
    wp = jax.scipy.linalg.block_diag(*[p["w_pool"][l][g] for g in range(HEADS)]).astype(BF16)
    return dict(gmix=row(p["g_mix_norm"][l]), gql=row(p["g_q_lat"][l]), gkvl=row(p["g_kv_lat"][l]),
                gqh=pad_head(p["g_q_head"][l]), gkh=pad_head(p["g_k_head"][l]), gsgu=row(p["g_sgu_v"][l]),
                wcat=wcat, wcat_t=wcat_t, bz=bz, wp=wp, pscale=row(p["pool_scale"][l]),
                goa=row(p["g_out_mla"][l]), gos=row(p["g_out_sgu"][l]), gop=row(p["g_out_pool"][l]),
                gffn=row(p["g_ffn_norm"][l]))


MIXER_SIDE = ("w_in", "w_q_up", "w_kv_up", "w_out")
FFN_SIDE = ("w_gate", "w_up", "w_down")
TRANSPOSED = ("w_in", "w_q_up", "w_kv_up", "w_gate", "w_up")
FIRST_0 = ("w_in", "w_q_up", "w_kv_up")
LATER_0 = ("w_out",) + FFN_SIDE
SHARDED = MIXER_SIDE + FFN_SIDE
SMALL = ("g_mix_norm", "g_q_lat", "g_kv_lat", "g_q_head", "g_k_head", "g_sgu_v", "w_spatial", "b_spatial", "w_pool",
         "pool_scale", "g_out_mla", "g_out_sgu", "g_out_pool", "g_ffn_norm")
WEIGHTS = ("g_mix_norm", "w_in", "g_q_lat", "w_q_up", "g_kv_lat", "w_kv_up", "g_q_head", "g_k_head", "g_sgu_v", "w_spatial",
           "b_spatial", "w_pool", "pool_scale", "g_out_mla", "g_out_sgu", "g_out_pool", "w_out", "g_ffn_norm", "w_gate",
           "w_up", "w_down")
PACK_ROWS = 8 * 128


def _pack_small(parts):
    flat = []
    for name in SMALL:
        a = parts[name].reshape(-1)
        flat.append(jnp.pad(a, (0, -a.shape[0] % PACK_ROWS)))
    return jnp.concatenate(flat).reshape(-1, 128)


def _unpack_small(packed, like):
    out, row = {}, 0
    for name in SMALL:
        n = math.prod(like[name].shape)
        rows = -(-n // PACK_ROWS) * 8
        out[name] = packed[row:row + rows].reshape(-1)[:n].reshape(like[name].shape)
        row += rows
    return out


def _forward_attention(x, rope, W, S):
    z, q, k, v, kt, vt = mixin_fwd(x, rope, S["gmix"], W["win"], S["gql"], W["wq"], S["gkvl"], W["wkv"], S["gqh"], S["gkh"])
    a, lse = attn_fwd(q, k, vt)
    return dict(x=x, z=z, q=q, k=k, kt=kt, v=v, a=a, lse=lse)


def _forward_mixers(A, wout, S):
    A["x1"] = mixers_fwd(A["x"], A["a"], A["z"], S["gsgu"], S["wcat"], S["bz"], S["wp"], S["pscale"], S["goa"], S["gos"],
                         S["gop"], wout)
    return A["x1"]


def _backward_ffn(dx2, W, S, A, l):
    dx1, h2, act, dg, du_ffn, dgffn = ffn_bwd(dx2, A["x1"], A["gs"], A["us"], S["gffn"], W["wg"], W["wu"], W["wd"])
    d_wd = matmul_tn(act, dx2, f"dw_down_{l}")
    d_wg = matmul_tn(dg, h2, f"dw_gate_{l}")
    d_wu = matmul_tn(du_ffn, h2, f"dw_up_{l}")
    big = {n: d.reshape(N_DEV, -1, D_MODEL) for n, d in (("w_gate", d_wg), ("w_up", d_wu), ("w_down", d_wd))}
    return dx1, big, dgffn


def _backward_mixers(dx1, dgffn, rope, W, S, A, l, send_wout=None):
    (da, delta, du, dvs, dm, mix, dgoa, dgos, dgop, dps, dwp, dwsp, db, dgsgu) = mixers_bwd(
        dx1, A["a"], A["z"], S["gsgu"], S["wcat"], S["wcat_t"], S["bz"], S["wp"], S["pscale"], S["goa"], S["gos"],
        S["gop"], W["wout"])
    d_wout = matmul_tn(mix, dx1, f"dw_out_{l}")
    if send_wout is not None:
        delta = delta + send_wout(d_wout.reshape(N_DEV, -1, D_MODEL))[0, 0]
    dpin = pool_bwd(dm)
    dqt, dk, dv = attn_bwd(A["q"], A["k"], A["kt"], A["v"], da, A["lse"], delta)
    (dx, hn, dz, qn, dq_up, kvn, dkv_up, dgmix, dgql, dgkvl, dgqh, dgkh) = mixin_bwd(
        dx1, A["x"], A["z"], rope, dqt, dk, dv, du, dvs, dpin, S["gmix"], W["win"], S["gql"], W["wq"], S["gkvl"],
        W["wkv"], S["gqh"], S["gkh"])
    d_win = matmul_tn(dz, hn, f"dw_in_{l}")
    d_wq = matmul_tn(dq_up, qn, f"dw_q_up_{l}")
    d_wkv = matmul_tn(dkv_up, kvn, f"dw_kv_up_{l}")
    d_win = jnp.concatenate([d_win[:O2], d_win[O2 + QK_NOPE:O2 + QK_HEAD], d_win[O2 + HEAD_PAD:]], axis=0)
    d_wk = d_wkv[:HEADS * HEAD_PAD].reshape(HEADS, HEAD_PAD, KV_LORA)[:, :QK_NOPE]
    d_wv = d_wkv[HEADS * HEAD_PAD:].reshape(HEADS, V_HEAD, KV_LORA)
    d_wkv = jnp.concatenate([d_wk, d_wv], axis=1)
    big = dict(w_in=d_win.reshape(N_DEV, -1, D_MODEL), w_q_up=_unpad_head_rows(d_wq, QK_HEAD).reshape(N_DEV, -1, Q_LORA),
               w_kv_up=d_wkv.reshape(N_DEV, -1, KV_LORA), w_out=d_wout.reshape(N_DEV, -1, D_MODEL))
    tril = jnp.tril(jnp.ones((CHUNK, CHUNK), F32))
    small = dict(g_mix_norm=dgmix[0], g_q_lat=dgql[0], g_kv_lat=dgkvl[0], g_q_head=dgqh[0, :QK_HEAD], g_k_head=dgkh[0, :QK_HEAD],
                 g_sgu_v=dgsgu[0], w_spatial=dwsp.reshape(HEADS, CHUNK, CHUNK) * tril, b_spatial=db[:, :HEADS].T,
                 w_pool=jnp.stack([dwp[64 * g:64 * (g + 1), 64 * g:64 * (g + 1)] for g in range(HEADS)]),
                 pool_scale=dps[0], g_out_mla=dgoa[0], g_out_sgu=dgos[0], g_out_pool=dgop[0], g_ffn_norm=dgffn[0])
    return dx, big, small


def kernel(x, positions, g_mix_norm, w_in, g_q_lat, w_q_up, g_kv_lat, w_kv_up, g_q_head, g_k_head, g_sgu_v, w_spatial, b_spatial, w_pool, pool_scale, g_out_mla, g_out_sgu, g_out_pool, w_out, g_ffn_norm, w_gate, w_up, w_down, loss_target, m_g_mix_norm, m_w_in, m_g_q_lat, m_w_q_up, m_g_kv_lat, m_w_kv_up, m_g_q_head, m_g_k_head, m_g_sgu_v, m_w_spatial, m_b_spatial, m_w_pool, m_pool_scale, m_g_out_mla, m_g_out_sgu, m_g_out_pool, m_w_out, m_g_ffn_norm, m_w_gate, m_w_up, m_w_down, v_g_mix_norm, v_w_in, v_g_q_lat, v_w_q_up, v_g_kv_lat, v_w_kv_up, v_g_q_head, v_g_k_head, v_g_sgu_v, v_w_spatial, v_b_spatial, v_w_pool, v_pool_scale, v_g_out_mla, v_g_out_sgu, v_g_out_pool, v_w_out, v_g_ffn_norm, v_w_gate, v_w_up, v_w_down):
    given = dict(locals())
    w = {n: given[n] for n in WEIGHTS}
    m = {n: given["m_" + n] for n in WEIGHTS}
    v = {n: given["v_" + n] for n in WEIGHTS}
    T = x.shape[1]
    xs = x.reshape(T, D_MODEL)

    half = QK_ROPE // 2
    inv_freq = 1.0 / (ROPE_THETA ** (jnp.arange(half, dtype=F32) / half))
    ang16 = positions.reshape(T).astype(F32)[:, None] * inv_freq
    ang = jnp.concatenate([jnp.zeros((T, QK_NOPE), F32), ang16, ang16, jnp.zeros((T, HEAD_PAD - QK_HEAD), F32)], axis=1)

    wv = {n: _shard_view(n, w[n]) for n in SHARDED}
    wanted = [(SHARDED.index(n), 0) for n in LATER_0] + [(i, 1) for i in range(len(SHARDED))]
    bf = cast_shards([wv[n] for n in SHARDED], wanted)
    first0 = allgather_layer0([wv[n] for n in FIRST_0])
    ag_later0 = exchange_start(bf[:len(LATER_0)], first0[0], True, "ag_start_later0")
    ag_l1 = exchange_start(bf[len(LATER_0):], ag_later0[-1], True, "ag_start_l1")
    rope = rope_tables(ang + (ag_later0[-1][0, 0] + ag_l1[-1][0, 0]))
    Ss = [_layer_small(w, l) for l in range(DEPTH)]

    W0 = dict(zip(FIRST_0, first0))
    A0 = _forward_attention(xs, rope, _mixer_weights(W0), Ss[0])
    W0.update(zip(LATER_0, exchange_wait(ag_later0, A0["a"], "ag_wait_later0")))
    W0 = {**_mixer_weights(W0), **_ffn_weights(W0)}
    x1 = _forward_mixers(A0, W0["wout"], Ss[0])
    h, A0["gs"], A0["us"] = ffn_fwd(x1, Ss[0]["gffn"], W0["wg"], W0["wu"], W0["wd"])
    layer1 = dict(zip(SHARDED, exchange_wait(ag_l1, h, "ag_wait_l1")))
    W1 = {**_mixer_weights(layer1), **_ffn_weights(layer1)}
    A1 = _forward_attention(h, rope, W1, Ss[1])
    x1 = _forward_mixers(A1, W1["wout"], Ss[1])
    dh, loss_part, A1["gs"], A1["us"] = ffn_fwd(x1, Ss[1]["gffn"], W1["wg"], W1["wu"], W1["wd"],
                                                target=loss_target.reshape(T, D_MODEL))
    loss = lax.psum(loss_part[0, 0], ("x", "y", "c"))

    dx1, big_f1, dgffn1 = _backward_ffn(dh, W1, Ss[1], A1, 1)
    dh, big_m1, small1 = _backward_mixers(dx1, dgffn1, rope, W1, Ss[1], A1, 1)
    rs_l1 = exchange_start([{**big_m1, **big_f1}[n] for n in SHARDED], dh, False, "rs_start_l1")
    S0 = dict(Ss[0], gffn=Ss[0]["gffn"] + rs_l1[-1][0, 0])
    dx1, big_f0, dgffn0 = _backward_ffn(dh, W0, S0, A0, 0)
    rs_ffn0 = exchange_start([big_f0[n] for n in FFN_SIDE], dx1, False, "rs_start_ffn0")
    S0 = dict(Ss[0], gsgu=Ss[0]["gsgu"] + rs_ffn0[-1][0, 0])
    rs_out0 = []

    def send_wout(pieces):
        rs_out0.append(exchange_start([pieces], dx1, False, "rs_start_out0"))
        return rs_out0[0][-1]

    dh, big_m0, small0 = _backward_mixers(dx1, dgffn0, rope, W0, S0, A0, 0, send_wout)
    grad_x = dh.reshape(x.shape)
    smalls = [small0, small1]

    rs_first0 = exchange_start([big_m0[n] for n in FIRST_0], dh, False, "rs_start_first0")
    parts1 = dict(zip(SHARDED, exchange_wait(rs_l1, rs_first0[-1], "rs_wait_l1")))
    parts0 = dict(zip(FFN_SIDE, exchange_wait(rs_ffn0, rs_first0[-1], "rs_wait_ffn0")))
    parts0["w_out"] = exchange_wait(rs_out0[0], rs_first0[-1], "rs_wait_out0")[0]
    grad, delta, new_m, new_v = {}, {}, {}, {}

    small_part = {n: jnp.stack([smalls[l][n] for l in range(DEPTH)]) for n in SMALL}
    outs = allreduce_adamw_small(_pack_small(small_part), _pack_small(w), _pack_small(m), _pack_small(v))
    for d, o in zip((grad, delta, new_m, new_v), outs):
        d.update(_unpack_small(o, w))

    for n in LATER_0 + FIRST_0:
        if n == FIRST_0[0]:
            parts0.update(zip(FIRST_0, exchange_wait(rs_first0, grad["w_down"], "rs_wait_first0")))
        parts, views = [parts0[n], parts1[n]], [_shard_view(n, a[n]) for a in (w, m, v)]
        if n == "w_in":
            parts = [jnp.stack(parts, axis=2).reshape(N_DEV, -1, D_MODEL)]
            views = [a.reshape(1, -1, D_MODEL) for a in views]
        outs = adamw_sharded(parts, *views, f"adamw_{n}")
        if n == "w_in":
            outs = [o.reshape(-1, DEPTH, D_MODEL) for o in outs]
        grad[n], delta[n], new_m[n], new_v[n] = [_shard_unview(n, o) for o in outs]

    return (loss, grad_x, *[grad[n] for n in WEIGHTS], *[delta[n] for n in WEIGHTS], *[new_m[n] for n in WEIGHTS],
            *[new_v[n] for n in WEIGHTS])
```

```python
import functools
import math

import jax
import jax.numpy as jnp
from jax import lax
from jax.experimental import pallas as pl
from jax.experimental.pallas import tpu as pltpu

F32 = jnp.float32
BF16 = jnp.bfloat16

N_DEV = 8
DEPTH = 2
D_MODEL = 1024
HEADS = 4
HEAD_PAD = 128
QK_NOPE = 64
QK_ROPE = 32
QK_HEAD = QK_NOPE + QK_ROPE
V_HEAD = 128
Q_LORA = 256
KV_LORA = 128
SGU_WIDTH = 256
SGU_HEAD_DIM = 64
CHUNK = 128
POOL_WIDTH = 256
POOL_WINDOWS = (2, 4, 8, 16)
POOL_HALO = 16
MLA_WIDTH = 512
IN_WIDTH = 1184
Z_WIDTH = 1280
FFN_HIDDEN = 2816
FFN_CHUNK = 256
ROPE_THETA = 10000.0
EPS = 1e-6
ATTN_SCALE = 1.0 / math.sqrt(QK_HEAD)
LOG2E = 1.4426950408889634
NEG_BIG = -1e30

ADAM_LR = 0.001
ADAM_B1 = 0.9
ADAM_B2 = 0.999
ADAM_EPS = 1e-08
ADAM_WD = 0.01
ADAM_STEP = 10

VMEM_LIMIT = 56 * 1024 * 1024
ROW_TILE = 512
MIXIN_PART = 256
MIXIN_BWD_PART = 128
ATTN_TILE = 1024
ATTN_SUB = 512
ATTN_FWD_SUB = 512
ATTN_BWD_HEADS = 2
MESH = pl.DeviceIdType.MESH

WHOLE = pl.BlockSpec(memory_space=pltpu.VMEM)
ANY = pl.BlockSpec(memory_space=pl.ANY)
HBM_SPEC = pl.BlockSpec(memory_space=pltpu.HBM)
SEM_SPEC = pl.BlockSpec(memory_space=pltpu.SEMAPHORE)


def _cparams(**kw):
    return pltpu.CompilerParams(vmem_limit_bytes=VMEM_LIMIT, **kw)


def _dot(a, b):
    return jnp.dot(a, b, preferred_element_type=F32)


def _dot_nt(a, b):
    return lax.dot_general(a, b, (((1,), (1,)), ((), ())), preferred_element_type=F32)


def _dot_tn(a, b):
    return lax.dot_general(a, b, (((0,), (0,)), ((), ())), preferred_element_type=F32)


def _rms(x, g, n):
    r = lax.rsqrt(jnp.sum(x * x, axis=-1, keepdims=True) * (1.0 / n) + EPS)
    return x * r * g, r


def _rms_bwd(x, r, g, dy, n):
    gdy = dy * g
    dx = r * gdy - x * (r * r * r) * (jnp.sum(x * gdy, axis=-1, keepdims=True) * (1.0 / n))
    dg = jnp.sum(dy * (x * r), axis=0, keepdims=True)
    return dx, dg


def _sigmoid(x):
    return 1.0 / (1.0 + jnp.exp(-x))


def rope_tables(ang):
    T = ang.shape[0]
    tm = min(ROW_TILE, T)

    def body(ang_ref, c_ref, sa_ref, sb_ref):
        a = ang_ref[...]
        lane = lax.broadcasted_iota(jnp.int32, a.shape, 1)
        s = jnp.sin(a)
        c_ref[...] = jnp.cos(a)
        sa_ref[...] = jnp.where(lane < QK_NOPE + QK_ROPE // 2, -s, 0.0)
        sb_ref[...] = jnp.where(lane >= QK_NOPE + QK_ROPE // 2, s, 0.0)

    row = pl.BlockSpec((tm, HEAD_PAD), lambda i: (i, 0))
    return pl.pallas_call(
        body, grid=(T // tm,), name="rope_tables", in_specs=[row], out_specs=[row] * 3,
        out_shape=[jax.ShapeDtypeStruct((T, HEAD_PAD), F32)] * 3, compiler_params=_cparams(),
    )(ang)


def _rope(x, c, sa, sb):
    half = QK_ROPE // 2
    return x * c + pltpu.roll(x, HEAD_PAD - half, 1) * sa + pltpu.roll(x, half, 1) * sb


def _rope_bwd(dy, c, sa, sb):
    half = QK_ROPE // 2
    return dy * c + pltpu.roll(dy * sa, half, 1) + pltpu.roll(dy * sb, HEAD_PAD - half, 1)


def _skewed(stages, groups):
    for t in range(len(stages) + len(groups) - 1):
        for p, g in enumerate(groups):
            if 0 <= t - p < len(stages):
                stages[t - p](g)


def _head_masks(shape, width):
    lane = lax.broadcasted_iota(jnp.int32, shape, len(shape) - 1)
    return [(lane >= width * h) & (lane < width * (h + 1)) for h in range(HEADS)]


def _qkv_pre(z, gql, wq, gkvl, wkv):
    ql = z[:, 0:Q_LORA]
    kvl = z[:, Q_LORA:Q_LORA + KV_LORA]
    kr = z[:, Q_LORA + KV_LORA:Q_LORA + KV_LORA + HEAD_PAD]
    qn, rq = _rms(ql, gql, Q_LORA)
    kvn, rkv = _rms(kvl, gkvl, KV_LORA)
    qn = qn.astype(BF16)
    kvn = kvn.astype(BF16)
    q_up = _dot_nt(qn, wq)
    kv_up = _dot_nt(kvn, wkv)
    return ql, kvl, kr, qn, rq, kvn, rkv, q_up, kv_up


def mixin_fwd(x, rope, gmix, win, gql, wq, gkvl, wkv, gqh, gkh):
    T = x.shape[0]
    tm = min(2 * ROW_TILE, T)
    part = min(MIXIN_PART, tm)

    def body(x_ref, c_ref, sa_ref, sb_ref, gmix_ref, win_ref, gql_ref, wq_ref, gkvl_ref, wkv_ref, gqh_ref, gkh_ref,
             z_ref, q_ref, k_ref, v_ref, kt_ref, vt_ref):
        def project(g):
            hn = _rms(x_ref[g["rows"], :], gmix_ref[...], D_MODEL)[0].astype(BF16)
            g["z"] = _dot_nt(hn, win_ref[...])
            z_ref[g["rows"], :] = g["z"]

        def latents(g):
            _, _, g["kr"], _, _, _, _, g["q_up"], g["kv_up"] = _qkv_pre(g["z"], gql_ref[...], wq_ref[...], gkvl_ref[...],
                                                                        wkv_ref[...])

        def heads(g):
            rows = g["rows"]
            c, sa, sb = c_ref[rows, :], sa_ref[rows, :], sb_ref[rows, :]
            for h in range(HEADS):
                lo = HEAD_PAD * h
                qh = _rms(g["q_up"][:, lo:lo + HEAD_PAD], gqh_ref[...], QK_HEAD)[0]
                q_ref[h, rows, :] = (_rope(qh, c, sa, sb) * (ATTN_SCALE * LOG2E)).astype(BF16)
                kh = _rope(_rms(g["kv_up"][:, lo:lo + HEAD_PAD] + g["kr"], gkh_ref[...], QK_HEAD)[0], c, sa, sb)
                k_ref[h, rows, :] = kh.astype(BF16)
                kt_ref[h, :, rows] = jnp.transpose(kh).astype(BF16)
                vh = g["kv_up"][:, HEADS * HEAD_PAD + lo:HEADS * HEAD_PAD + lo + HEAD_PAD]
                v_ref[h, rows, :] = vh.astype(BF16)
                vt_ref[h, :, rows] = jnp.transpose(vh).astype(BF16)

        _skewed((project, latents, heads), [dict(rows=pl.ds(part * p, part)) for p in range(tm // part)])

    row = lambda w: pl.BlockSpec((tm, w), lambda i: (i, 0))
    head = pl.BlockSpec((HEADS, tm, HEAD_PAD), lambda i: (0, i, 0))
    head_t = pl.BlockSpec((HEADS, HEAD_PAD, tm), lambda i: (0, 0, i))
    return pl.pallas_call(
        body, grid=(T // tm,), name="mixin_fwd",
        in_specs=[row(D_MODEL)] + [row(HEAD_PAD)] * 3 + [WHOLE] * 8,
        out_specs=[row(Z_WIDTH), head, head, head, head_t, head_t],
        out_shape=[jax.ShapeDtypeStruct((T, Z_WIDTH), F32)] + [jax.ShapeDtypeStruct((HEADS, T, HEAD_PAD), BF16)] * 3
                  + [jax.ShapeDtypeStruct((HEADS, HEAD_PAD, T), BF16)] * 2,
        compiler_params=_cparams(),
    )(x, *rope, gmix, win, gql, wq, gkvl, wkv, gqh, gkh)


def attn_fwd(q, k, vt):
    _, T, _ = q.shape
    tb = min(ATTN_TILE, T)
    sq = min(ATTN_FWD_SUB, tb)
    nb = T // tb

    def body(q_ref, k_ref, vt_ref, o_ref, lse_ref, m_s, l_s, acc_s):
        i, j = pl.program_id(0), pl.program_id(1)

        @pl.when(j == 0)
        def _():
            m_s[...] = jnp.full(m_s.shape, -jnp.inf, F32)
            l_s[...] = jnp.zeros(l_s.shape, F32)
            acc_s[...] = jnp.zeros(acc_s.shape, F32)

        def scores(g):
            st = _dot_nt(k_ref[g["h"], g["kb"], :], q_ref[g["h"], g["qa"], :])
            if g["masked"]:
                krow = lax.broadcasted_iota(jnp.int32, st.shape, 0)
                qcol = g["q0"] + lax.broadcasted_iota(jnp.int32, st.shape, 1)
                st = jnp.where(krow <= qcol, st, NEG_BIG)
            g["st"] = st

        def new_max(g):
            h, qa = g["h"], g["qa"]
            m_prev = m_s[h, :, qa]
            g["m"] = jnp.maximum(m_prev, jnp.max(g["st"], axis=0, keepdims=True))
            g["alpha"] = jnp.exp2(m_prev - g["m"])
            m_s[h, :, qa] = g["m"]

        def weights(g):
            pt = jnp.exp2(g["st"] - g["m"])
            g["lsum"] = jnp.sum(pt, axis=0, keepdims=True)
            g["pt"] = pt.astype(BF16)

        def accumulate(g):
            h, qa = g["h"], g["qa"]
            l_s[h, :, qa] = g["alpha"] * l_s[h, :, qa] + g["lsum"]
            acc_s[h, :, qa] = g["alpha"] * acc_s[h, :, qa] + _dot(vt_ref[h, :, g["kb"]], g["pt"])

        def tiles(masked):
            return [dict(h=h, q0=q0, qa=slice(q0, q0 + sq), kb=slice(0, q0 + sq if masked else tb), masked=masked)
                    for h in range(HEADS) for q0 in range(0, tb, sq)]

        @pl.when(j < i)
        def _():
            _skewed((scores, new_max, weights, accumulate), tiles(False))

        @pl.when(j == i)
        def _():
            _skewed((scores, new_max, weights, accumulate), tiles(True))
            for h in range(HEADS):
                l = l_s[h]
                o_ref[:, HEAD_PAD * h:HEAD_PAD * (h + 1)] = jnp.transpose(acc_s[h] / l)
                lse_ref[h] = m_s[h] + jnp.log2(l)

    qspec = pl.BlockSpec((HEADS, tb, HEAD_PAD), lambda i, j: (0, i, 0))
    kspec = pl.BlockSpec((HEADS, tb, HEAD_PAD), lambda i, j: (0, jnp.minimum(i, j), 0))
    vspec = pl.BlockSpec((HEADS, HEAD_PAD, tb), lambda i, j: (0, 0, jnp.minimum(i, j)))
    return pl.pallas_call(
        body, grid=(nb, nb), name="attn_fwd",
        in_specs=[qspec, kspec, vspec],
        out_specs=[pl.BlockSpec((tb, MLA_WIDTH), lambda i, j: (i, 0)), pl.BlockSpec((HEADS, 1, tb), lambda i, j: (0, 0, i))],
        out_shape=[jax.ShapeDtypeStruct((T, MLA_WIDTH), F32), jax.ShapeDtypeStruct((HEADS, 1, T), F32)],
        scratch_shapes=[pltpu.VMEM((HEADS, 1, tb), F32), pltpu.VMEM((HEADS, 1, tb), F32), pltpu.VMEM((HEADS, HEAD_PAD, tb), F32)],
        compiler_params=_cparams(),
    )(q, k, vt)


def _sgu_fwd_chunk(vn_c, wcat, bz, masks):
    vstack = jnp.concatenate([jnp.where(mk, vn_c, 0.0).astype(BF16) for mk in masks], axis=0)
    return _dot(wcat, vstack) + bz


def _pool_counts(i, tm):
    pos1 = (i * tm + 1 + lax.broadcasted_iota(jnp.int32, (tm, POOL_WIDTH), 0)).astype(F32)
    lane = lax.broadcasted_iota(jnp.int32, (tm, POOL_WIDTH), 1)
    win = jnp.where(lane < 64, 2.0, jnp.where(lane < 128, 4.0, jnp.where(lane < 192, 8.0, 16.0)))
    return jnp.minimum(pos1, win), lane


def _by_group(lane, s2, s4, s8, s16):
    return jnp.where(lane < 64, s2, jnp.where(lane < 128, s4, jnp.where(lane < 192, s8, s16)))


def _pool_means(pin, halo, i, tm):
    s1 = jnp.concatenate([halo, pin], axis=0)
    s2 = s1 + pltpu.roll(s1, 1, 0)
    s4 = s2 + pltpu.roll(s2, 2, 0)
    s8 = s4 + pltpu.roll(s4, 4, 0)
    s16 = s8 + pltpu.roll(s8, 8, 0)
    cnt, lane = _pool_counts(i, tm)
    sel = _by_group(lane, s2[POOL_HALO:], s4[POOL_HALO:], s8[POOL_HALO:], s16[POOL_HALO:])
    return sel / cnt - pin


def _mixers_fwd_tile(i, tm, a, u, vs, pin, halo, gsgu, wcat, bz, wp, pscale, goa, gos, gop):
    vn, rv = _rms(vs, gsgu, SGU_WIDTH)
    masks = _head_masks((CHUNK, SGU_WIDTH), SGU_HEAD_DIM)
    zc = jnp.concatenate([_sgu_fwd_chunk(vn[CHUNK * c:CHUNK * (c + 1)], wcat, bz, masks) for c in range(tm // CHUNK)], axis=0)
    gm = u * zc
    halo = jnp.where(i > 0, halo, 0.0)
    m = _pool_means(pin, halo, i, tm).astype(BF16)
    yp_pre = _dot(m, wp)
    yp = yp_pre * pscale
    na, ra = _rms(a, goa, MLA_WIDTH)
    ng, rg = _rms(gm, gos, SGU_WIDTH)
    npo, rp = _rms(yp, gop, POOL_WIDTH)
    mix = jnp.concatenate([na, ng, npo], axis=1).astype(BF16)
    return vn, rv, zc, gm, m, yp_pre, yp, ra, rg, rp, mix


def _z_specs(tm):
    col = lambda c: pl.BlockSpec((tm, 256), lambda i: (i, c))
    halo = pl.BlockSpec((POOL_HALO, 256), lambda i: (jnp.maximum(i * (tm // POOL_HALO) - 1, 0), 4))
    return [col(2), col(3), col(4), halo]


def mixers_fwd(x, a, z, gsgu, wcat, bz, wp, pscale, goa, gos, gop, wout):
    T = x.shape[0]
    tm = min(ROW_TILE, T)

    def body(x_ref, a_ref, u_ref, vs_ref, pin_ref, halo_ref, gsgu_ref, wcat_ref, bz_ref, wp_ref, ps_ref,
             goa_ref, gos_ref, gop_ref, wout_ref, x1_ref):
        i = pl.program_id(0)
        mix = _mixers_fwd_tile(i, tm, a_ref[...], u_ref[...], vs_ref[...], pin_ref[...], halo_ref[...], gsgu_ref[...],
                               wcat_ref[...], bz_ref[...], wp_ref[...], ps_ref[...], goa_ref[...], gos_ref[...],
                               gop_ref[...])[-1]
        x1_ref[...] = x_ref[...] + _dot(mix, wout_ref[...])

    row = lambda w: pl.BlockSpec((tm, w), lambda i: (i, 0))
    return pl.pallas_call(
        body, grid=(T // tm,), name="mixers_fwd",
        in_specs=[row(D_MODEL), row(MLA_WIDTH)] + _z_specs(tm) + [WHOLE] * 9,
        out_specs=row(D_MODEL),
        out_shape=jax.ShapeDtypeStruct((T, D_MODEL), F32),
        compiler_params=_cparams(),
    )(x, a, z, z, z, z, gsgu, wcat, bz, wp, pscale, goa, gos, gop, wout)


def ffn_fwd(x1, gffn, wg, wu, wd, target=None):
    T = x1.shape[0]
    tm = min(ROW_TILE, T)
    with_loss = target is not None

    def body(*refs):
        x1_ref, gffn_ref, wg_ref, wu_ref, wd_ref = refs[:5]
        outs = refs[6:] if with_loss else refs[5:]
        x1v = x1_ref[...]
        h2 = _rms(x1v, gffn_ref[...], D_MODEL)[0].astype(BF16)
        acc = x1v
        for c in range(FFN_HIDDEN // FFN_CHUNK):
            sl = slice(FFN_CHUNK * c, FFN_CHUNK * (c + 1))
            g = _dot_nt(h2, wg_ref[sl, :])
            u = _dot_nt(h2, wu_ref[sl, :])
            outs[-2][:, sl] = g.astype(BF16)
            outs[-1][:, sl] = u.astype(BF16)
            act = (g * _sigmoid(g) * u).astype(BF16)
            acc = acc + _dot(act, wd_ref[sl, :])
        if not with_loss:
            outs[0][...] = acc
            return
        dy_ref, loss_ref = outs[0], outs[1]

        @pl.when(pl.program_id(0) == 0)
        def _():
            loss_ref[...] = jnp.zeros(loss_ref.shape, F32)

        err = acc - refs[5][...]
        dy_ref[...] = err * (1.0 / D_MODEL)
        per_row = jnp.sum(err * err, axis=1, keepdims=True) * (1.0 / D_MODEL)
        loss_ref[...] += 0.5 * jnp.sum(per_row, axis=0, keepdims=True)

    row = lambda w: pl.BlockSpec((tm, w), lambda i: (i, 0))
    hidden = [jax.ShapeDtypeStruct((T, FFN_HIDDEN), BF16)] * 2
    if with_loss:
        return pl.pallas_call(
            body, grid=(T // tm,), name="ffn_fwd_loss",
            in_specs=[row(D_MODEL)] + [WHOLE] * 4 + [row(D_MODEL)],
            out_specs=[row(D_MODEL), pl.BlockSpec((1, 1), lambda i: (0, 0)), row(FFN_HIDDEN), row(FFN_HIDDEN)],
            out_shape=[jax.ShapeDtypeStruct((T, D_MODEL), F32), jax.ShapeDtypeStruct((1, 1), F32)] + hidden,
            compiler_params=_cparams(),
        )(x1, gffn, wg, wu, wd, target)
    return pl.pallas_call(
        body, grid=(T // tm,), name="ffn_fwd",
        in_specs=[row(D_MODEL)] + [WHOLE] * 4,
        out_specs=[row(D_MODEL), row(FFN_HIDDEN), row(FFN_HIDDEN)],
        out_shape=[jax.ShapeDtypeStruct((T, D_MODEL), F32)] + hidden,
        compiler_params=_cparams(),
    )(x1, gffn, wg, wu, wd)


def _acc_spec(shape):
    return pl.BlockSpec(shape, lambda i: (0,) * len(shape))


def ffn_bwd(dx2, x1, gs, us, gffn, wg, wu, wd):
    T = x1.shape[0]
    tm = min(ROW_TILE // 2, T)

    def body(dx2_ref, x1_ref, gs_ref, us_ref, gffn_ref, wg_ref, wu_ref, wd_ref,
             dx1_ref, h2_ref, act_ref, dg_ref, du_ref, dgffn_ref):
        @pl.when(pl.program_id(0) == 0)
        def _():
            dgffn_ref[...] = jnp.zeros(dgffn_ref.shape, F32)

        dx2v = dx2_ref[...]
        dy = dx2v.astype(BF16)
        x1v = x1_ref[...]
        h2, r = _rms(x1v, gffn_ref[...], D_MODEL)
        h2_ref[...] = h2.astype(BF16)
        for c in range(FFN_HIDDEN // FFN_CHUNK):
            sl = slice(FFN_CHUNK * c, FFN_CHUNK * (c + 1))
            g = gs_ref[:, sl].astype(F32)
            u = us_ref[:, sl].astype(F32)
            dact = _dot_nt(dy, wd_ref[sl, :])
            sg = _sigmoid(g)
            silu = g * sg
            act_ref[:, sl] = (silu * u).astype(BF16)
            dg_ref[:, sl] = (dact * u * (sg * (1.0 + g * (1.0 - sg)))).astype(BF16)
            du_ref[:, sl] = (dact * silu).astype(BF16)
        dh2 = _dot(dg_ref[...], wg_ref[...]) + _dot(du_ref[...], wu_ref[...])
        dxn, dgn = _rms_bwd(x1v, r, gffn_ref[...], dh2, D_MODEL)
        dx1_ref[...] = dx2v + dxn
        dgffn_ref[...] += dgn

    row = lambda w: pl.BlockSpec((tm, w), lambda i: (i, 0))
    return pl.pallas_call(
        body, grid=(T // tm,), name="ffn_bwd",
        in_specs=[row(D_MODEL), row(D_MODEL), row(FFN_HIDDEN), row(FFN_HIDDEN)] + [WHOLE] * 4,
        out_specs=[row(D_MODEL), row(D_MODEL), row(FFN_HIDDEN), row(FFN_HIDDEN), row(FFN_HIDDEN), _acc_spec((1, D_MODEL))],
        out_shape=[jax.ShapeDtypeStruct((T, D_MODEL), F32), jax.ShapeDtypeStruct((T, D_MODEL), BF16),
                   jax.ShapeDtypeStruct((T, FFN_HIDDEN), BF16), jax.ShapeDtypeStruct((T, FFN_HIDDEN), BF16),
                   jax.ShapeDtypeStruct((T, FFN_HIDDEN), BF16), jax.ShapeDtypeStruct((1, D_MODEL), F32)],
        compiler_params=_cparams(),
    )(dx2, x1, gs, us, gffn, wg, wu, wd)


TN_K_TILE = 2048
TN_ACC_BYTES = 6 * 1024 * 1024


def matmul_tn(a, b, name):
    T, M = a.shape
    N = b.shape[1]
    tk = min(TN_K_TILE, T)
    tm = M if M <= 1024 else M // 2
    tn = max(d for d in range(128, N + 1, 128) if N % d == 0 and tm * d * 4 <= TN_ACC_BYTES)
    nk = T // tk

    def body(a_ref, b_ref, o_ref, acc):
        k = pl.program_id(2)

        @pl.when(k == 0)
        def _():
            acc[...] = jnp.zeros(acc.shape, F32)

        acc[...] += _dot_tn(a_ref[...].astype(BF16), b_ref[...].astype(BF16))

        @pl.when(k == nk - 1)
        def _():
            o_ref[...] = acc[...].astype(BF16)

    return pl.pallas_call(
        body, grid=(M // tm, N // tn, nk), name=name,
        in_specs=[pl.BlockSpec((tk, tm), lambda i, j, k: (k, i)), pl.BlockSpec((tk, tn), lambda i, j, k: (k, j))],
        out_specs=pl.BlockSpec((tm, tn), lambda i, j, k: (i, j)),
        out_shape=jax.ShapeDtypeStruct((M, N), BF16),
        scratch_shapes=[pltpu.VMEM((tm, tn), F32)],
        compiler_params=_cparams(),
    )(a, b)


def mixers_bwd(dx1, a, z, gsgu, wcat, wcat_t, bz, wp, pscale, goa, gos, gop, wout):
    T = a.shape[0]
    tm = min(ROW_TILE, T)

    def body(dx1_ref, a_ref, u_ref, vs_ref, pin_ref, halo_ref, gsgu_ref, wcat_ref, wcatt_ref, bz_ref, wp_ref,
             ps_ref, goa_ref, gos_ref, gop_ref, wout_ref,
             da_ref, delta_ref, du_ref, dvs_ref, dm_ref, mix_ref,
             dgoa_ref, dgos_ref, dgop_ref, dps_ref, dwp_ref, dwsp_ref, db_ref, dgsgu_ref):
        i = pl.program_id(0)

        @pl.when(i == 0)
        def _():
            for r in (dgoa_ref, dgos_ref, dgop_ref, dps_ref, dwp_ref, dwsp_ref, db_ref, dgsgu_ref):
                r[...] = jnp.zeros(r.shape, F32)

        a_v, u, vs = a_ref[...], u_ref[...], vs_ref[...]
        goa, gos, gop, pscale_v = goa_ref[...], gos_ref[...], gop_ref[...], ps_ref[...]
        vn, rv, zc, gm, m, yp_pre, yp, ra, rg, rp, mix = _mixers_fwd_tile(
            i, tm, a_v, u, vs, pin_ref[...], halo_ref[...], gsgu_ref[...], wcat_ref[...], bz_ref[...], wp_ref[...],
            pscale_v, goa, gos, gop)
        mix_ref[...] = mix
        dmix = _dot_nt(dx1_ref[...].astype(BF16), wout_ref[...])
        da, dgoa = _rms_bwd(a_v, ra, goa, dmix[:, :MLA_WIDTH], MLA_WIDTH)
        dgm, dgos = _rms_bwd(gm, rg, gos, dmix[:, MLA_WIDTH:MLA_WIDTH + SGU_WIDTH], SGU_WIDTH)
        dyp, dgop = _rms_bwd(yp, rp, gop, dmix[:, MLA_WIDTH + SGU_WIDTH:], POOL_WIDTH)
        da_ref[...] = da
        dgoa_ref[...] += dgoa
        dgos_ref[...] += dgos
        dgop_ref[...] += dgop
        prod = da * a_v
        ones = jnp.ones((8, HEAD_PAD), F32)
        for h in range(HEADS):
            lo = HEAD_PAD * h
            sums = lax.dot_general(ones, prod[:, lo:lo + HEAD_PAD], (((1,), (1,)), ((), ())), preferred_element_type=F32,
                                   precision=lax.Precision.HIGHEST)
            delta_ref[h] = sums[0:1, :]
        dps_ref[...] += jnp.sum(dyp * yp_pre, axis=0, keepdims=True)
        dyp_pre = (dyp * pscale_v).astype(BF16)
        dwp_ref[...] += _dot_tn(m, dyp_pre)
        dm_ref[...] = _dot_nt(dyp_pre, wp_ref[...])
        du_ref[...] = dgm * zc
        dzc = dgm * u
        masks = _head_masks((CHUNK, SGU_WIDTH), SGU_HEAD_DIM)
        lane_b = lax.broadcasted_iota(jnp.int32, (CHUNK, HEAD_PAD), 1)
        dvn_parts = []
        dwsp = jnp.zeros(dwsp_ref.shape, F32)
        db = jnp.zeros(db_ref.shape, F32)
        for c in range(tm // CHUNK):
            dz_c = dzc[CHUNK * c:CHUNK * (c + 1)]
            dzstack = jnp.concatenate([jnp.where(mk, dz_c, 0.0).astype(BF16) for mk in masks], axis=0)
            dvn_parts.append(_dot(wcatt_ref[...], dzstack))
            dwsp = dwsp + _dot_nt(dzstack, vn[CHUNK * c:CHUNK * (c + 1)].astype(BF16))
            for h, mk in enumerate(masks):
                col = jnp.sum(jnp.where(mk, dz_c, 0.0), axis=1, keepdims=True)
                db = db + jnp.where(lane_b == h, col, 0.0)
        dwsp_ref[...] += dwsp
        db_ref[...] += db
        dvs, dgsgu = _rms_bwd(vs, rv, gsgu_ref[...], jnp.concatenate(dvn_parts, axis=0), SGU_WIDTH)
        dvs_ref[...] = dvs
        dgsgu_ref[...] += dgsgu

    row = lambda w: pl.BlockSpec((tm, w), lambda i: (i, 0))
    head = pl.BlockSpec((HEADS, 1, tm), lambda i: (0, 0, i))
    acc_shapes = [(1, MLA_WIDTH), (1, SGU_WIDTH), (1, POOL_WIDTH), (1, POOL_WIDTH), (POOL_WIDTH, POOL_WIDTH),
                  (HEADS * CHUNK, CHUNK), (CHUNK, HEAD_PAD), (1, SGU_WIDTH)]
    return pl.pallas_call(
        body, grid=(T // tm,), name="mixers_bwd",
        in_specs=[row(D_MODEL), row(MLA_WIDTH)] + _z_specs(tm) + [WHOLE] * 10,
        out_specs=[row(MLA_WIDTH), head, row(256), row(256), row(256), row(D_MODEL)] + [_acc_spec(s) for s in acc_shapes],
        out_shape=[jax.ShapeDtypeStruct((T, MLA_WIDTH), F32), jax.ShapeDtypeStruct((HEADS, 1, T), F32),
                   jax.ShapeDtypeStruct((T, 256), F32), jax.ShapeDtypeStruct((T, 256), F32),
                   jax.ShapeDtypeStruct((T, 256), F32), jax.ShapeDtypeStruct((T, D_MODEL), BF16)]
                  + [jax.ShapeDtypeStruct(s, F32) for s in acc_shapes],
        compiler_params=_cparams(),
    )(dx1, a, z, z, z, z, gsgu, wcat, wcat_t, bz, wp, pscale, goa, gos, gop, wout)


def pool_bwd(dm):
    T = dm.shape[0]
    tm = min(ROW_TILE, T)
    nt = T // tm

    def body(dm_ref, next_ref, dpin_ref):
        i = pl.program_id(0)
        cnt, lane = _pool_counts(i, tm)
        dmv = dm_ref[...]
        win = _by_group(lane[:POOL_HALO], 2.0, 4.0, 8.0, 16.0)
        nxt = jnp.where(i < nt - 1, next_ref[...] / win, 0.0)
        r1 = jnp.concatenate([dmv / cnt, nxt], axis=0)
        n = tm + POOL_HALO
        r2 = r1 + pltpu.roll(r1, n - 1, 0)
        r4 = r2 + pltpu.roll(r2, n - 2, 0)
        r8 = r4 + pltpu.roll(r4, n - 4, 0)
        r16 = r8 + pltpu.roll(r8, n - 8, 0)
        dpin_ref[...] = _by_group(lane, r2[:tm], r4[:tm], r8[:tm], r16[:tm]) - dmv

    return pl.pallas_call(
        body, grid=(nt,), name="pool_bwd",
        in_specs=[pl.BlockSpec((tm, 256), lambda i: (i, 0)),
                  pl.BlockSpec((POOL_HALO, 256), lambda i: (jnp.minimum((i + 1) * (tm // POOL_HALO), T // POOL_HALO - 1), 0))],
        out_specs=pl.BlockSpec((tm, 256), lambda i: (i, 0)),
        out_shape=jax.ShapeDtypeStruct((T, 256), F32),
        compiler_params=_cparams(),
    )(dm, dm)


def attn_bwd(q, k, kt, v, do, lse, delta):
    _, T, _ = q.shape
    tb = min(ATTN_TILE, T)
    sb = min(ATTN_SUB, tb)
    ns = tb // sb
    nb = T // tb
    hb = ATTN_BWD_HEADS

    def body(q_ref, k_ref, kt_ref, v_ref, do_ref, lse_ref, delta_ref, dqt_ref, dk_ref, dv_ref, dk_s, dv_s):
        i, j = pl.program_id(1), pl.program_id(2)

        @pl.when((i == 0) & (j == 0))
        def _():
            dqt_ref[...] = jnp.zeros(dqt_ref.shape, F32)

        @pl.when(j == i)
        def _():
            dk_s[...] = jnp.zeros(dk_s.shape, F32)
            dv_s[...] = jnp.zeros(dv_s.shape, F32)

        def sub_block(h, a, b, masked):
            qa = slice(sb * a, sb * (a + 1))
            kb = slice(sb * b, sb * (b + 1))
            qv = q_ref[h, qa, :]
            dov = do_ref[qa, HEAD_PAD * h:HEAD_PAD * (h + 1)].astype(BF16)
            st = _dot_nt(k_ref[h, kb, :], qv)
            dpt = _dot_nt(v_ref[h, kb, :], dov)
            pt = jnp.exp2(st - lse_ref[h, :, qa])
            if masked:
                krow = lax.broadcasted_iota(jnp.int32, st.shape, 0)
                qcol = lax.broadcasted_iota(jnp.int32, st.shape, 1)
                pt = jnp.where(krow <= qcol, pt, 0.0)
            dst = (pt * (dpt - delta_ref[h, :, qa])).astype(BF16)
            dv_s[h, kb, :] += _dot(pt.astype(BF16), dov)
            dk_s[h, kb, :] += _dot(dst, qv)
            cols = pl.ds(pl.multiple_of(j * tb + sb * a, sb), sb)
            dqt_ref[h, :, cols] += _dot(kt_ref[h, :, kb], dst)

        @pl.when(j > i)
        def _():
            for a in range(ns):
                for h in range(hb):
                    for b in range(ns):
                        sub_block(h, a, b, False)

        @pl.when(j == i)
        def _():
            for a in range(ns):
                for h in range(hb):
                    for b in range(a + 1):
                        sub_block(h, a, b, a == b)

        @pl.when(j == nb - 1)
        def _():
            dk_ref[...] = dk_s[...] * (1.0 / LOG2E)
            dv_ref[...] = dv_s[...]

    qspec = pl.BlockSpec((hb, tb, HEAD_PAD), lambda g, i, j: (g, jnp.maximum(i, j), 0))
    kspec = pl.BlockSpec((hb, tb, HEAD_PAD), lambda g, i, j: (g, i, 0))
    ktspec = pl.BlockSpec((hb, HEAD_PAD, tb), lambda g, i, j: (g, 0, i))
    rowspec = pl.BlockSpec((hb, 1, tb), lambda g, i, j: (g, 0, jnp.maximum(i, j)))
    return pl.pallas_call(
        body, grid=(HEADS // hb, nb, nb), name="attn_bwd",
        in_specs=[qspec, kspec, ktspec, kspec, pl.BlockSpec((tb, hb * HEAD_PAD), lambda g, i, j: (jnp.maximum(i, j), g)), rowspec, rowspec],
        out_specs=[pl.BlockSpec((hb, HEAD_PAD, T), lambda g, i, j: (g, 0, 0)), kspec, kspec],
        out_shape=[jax.ShapeDtypeStruct((HEADS, HEAD_PAD, T), F32)] + [jax.ShapeDtypeStruct((HEADS, T, HEAD_PAD), F32)] * 2,
        scratch_shapes=[pltpu.VMEM((hb, tb, HEAD_PAD), F32), pltpu.VMEM((hb, tb, HEAD_PAD), F32)],
        compiler_params=_cparams(),
    )(q, k, kt, v, do, lse, delta)


def mixin_bwd(dres, x, z, rope, dqt, dk, dv, du, dvs, dpin, gmix, win, gql, wq, gkvl, wkv, gqh, gkh):
    T = x.shape[0]
    tm = min(ROW_TILE, T)
    part = min(MIXIN_BWD_PART, tm)

    def body(dres_ref, x_ref, z_ref, c_ref, sa_ref, sb_ref, dqt_ref, dk_ref, dv_ref, du_ref, dvs_ref, dpin_ref,
             gmix_ref, win_ref, gql_ref, wq_ref, gkvl_ref, wkv_ref, gqh_ref, gkh_ref,
             dx_ref, hn_ref, dz_ref, qn_ref, dqup_ref, kvn_ref, dkvup_ref,
             dgmix_ref, dgql_ref, dgkvl_ref, dgqh_ref, dgkh_ref):
        @pl.when(pl.program_id(0) == 0)
        def _():
            for r in (dgmix_ref, dgql_ref, dgkvl_ref, dgqh_ref, dgkh_ref):
                r[...] = jnp.zeros(r.shape, F32)

        lane = lax.broadcasted_iota(jnp.int32, (part, HEAD_PAD), 1)
        rope_lanes = (lane >= QK_NOPE) & (lane < QK_HEAD)

        def recompute(g):
            rows = g["rows"]
            g["xv"] = x_ref[rows, :]
            hn, g["rx"] = _rms(g["xv"], gmix_ref[...], D_MODEL)
            hn_ref[rows, :] = hn.astype(BF16)
            g["ql"], g["kvl"], g["kr"], qn, g["rq"], kvn, g["rkv"], g["q_up"], g["kv_up"] = _qkv_pre(
                z_ref[rows, :], gql_ref[...], wq_ref[...], gkvl_ref[...], wkv_ref[...])
            qn_ref[rows, :] = qn
            kvn_ref[rows, :] = kvn

        def heads(g):
            rows = g["rows"]
            c, sa, sb = c_ref[rows, :], sa_ref[rows, :], sb_ref[rows, :]
            dkr = jnp.zeros((part, HEAD_PAD), F32)
            dgqh = jnp.zeros((1, HEAD_PAD), F32)
            dgkh = jnp.zeros((1, HEAD_PAD), F32)
            dq_parts, dk_parts, dv_parts = [], [], []
            for h in range(HEADS):
                lo = HEAD_PAD * h
                qh = g["q_up"][:, lo:lo + HEAD_PAD]
                rqh = lax.rsqrt(jnp.sum(qh * qh, axis=-1, keepdims=True) * (1.0 / QK_HEAD) + EPS)
                dq_h = jnp.transpose(dqt_ref[h, :, rows]) * ATTN_SCALE
                dqh, dg = _rms_bwd(qh, rqh, gqh_ref[...], _rope_bwd(dq_h, c, sa, sb), QK_HEAD)
                dgqh = dgqh + dg
                dq_parts.append(dqh)
                kh = g["kv_up"][:, lo:lo + HEAD_PAD] + g["kr"]
                rkh = lax.rsqrt(jnp.sum(kh * kh, axis=-1, keepdims=True) * (1.0 / QK_HEAD) + EPS)
                dkh, dg = _rms_bwd(kh, rkh, gkh_ref[...], _rope_bwd(dk_ref[h, rows, :], c, sa, sb), QK_HEAD)
                dgkh = dgkh + dg
                dkr = dkr + jnp.where(rope_lanes, dkh, 0.0)
                dk_parts.append(dkh)
                dv_parts.append(dv_ref[h, rows, :])
            dgqh_ref[...] += dgqh
            dgkh_ref[...] += dgkh
            g["dkr"] = dkr
            g["dq_up"] = jnp.concatenate(dq_parts, axis=1).astype(BF16)
            g["dkv_up"] = jnp.concatenate(dk_parts + dv_parts, axis=1).astype(BF16)
            dqup_ref[rows, :] = g["dq_up"]
            dkvup_ref[rows, :] = g["dkv_up"]

        def latents(g):
            rows = g["rows"]
            dql, dg = _rms_bwd(g["ql"], g["rq"], gql_ref[...], _dot(g["dq_up"], wq_ref[...]), Q_LORA)
            dgql_ref[...] += dg
            dkvl, dg = _rms_bwd(g["kvl"], g["rkv"], gkvl_ref[...], _dot(g["dkv_up"], wkv_ref[...]), KV_LORA)
            dgkvl_ref[...] += dg
            g["dz"] = jnp.concatenate([dql, dkvl, g["dkr"], du_ref[rows, :], dvs_ref[rows, :], dpin_ref[rows, :]],
                                      axis=1).astype(BF16)
            dz_ref[rows, :] = g["dz"]

        def inputs(g):
            rows = g["rows"]
            dxn, dg = _rms_bwd(g["xv"], g["rx"], gmix_ref[...], _dot(g["dz"], win_ref[...]), D_MODEL)
            dgmix_ref[...] += dg
            dx_ref[rows, :] = dres_ref[rows, :] + dxn

        _skewed((recompute, heads, latents, inputs), [dict(rows=pl.ds(part * p, part)) for p in range(tm // part)])

    row = lambda w: pl.BlockSpec((tm, w), lambda i: (i, 0))
    head = pl.BlockSpec((HEADS, tm, HEAD_PAD), lambda i: (0, i, 0))
    head_t = pl.BlockSpec((HEADS, HEAD_PAD, tm), lambda i: (0, 0, i))
    acc_shapes = [(1, D_MODEL), (1, Q_LORA), (1, KV_LORA), (1, HEAD_PAD), (1, HEAD_PAD)]
    out_rows = [(D_MODEL, F32), (D_MODEL, BF16), (Z_WIDTH, BF16), (Q_LORA, BF16), (HEADS * HEAD_PAD, BF16),
                (KV_LORA, BF16), (2 * HEADS * HEAD_PAD, BF16)]
    return pl.pallas_call(
        body, grid=(T // tm,), name="mixin_bwd",
        in_specs=[row(D_MODEL), row(D_MODEL), row(Z_WIDTH)] + [row(HEAD_PAD)] * 3 + [head_t, head, head, row(256), row(256), row(256)]
                 + [WHOLE] * 8,
        out_specs=[row(w) for w, _ in out_rows] + [_acc_spec(s) for s in acc_shapes],
        out_shape=[jax.ShapeDtypeStruct((T, w), dt) for w, dt in out_rows] + [jax.ShapeDtypeStruct(s, F32) for s in acc_shapes],
        compiler_params=_cparams(),
    )(dres, x, z, *rope, dqt, dk, dv, du, dvs, dpin, gmix, win, gql, wq, gkvl, wkv, gqh, gkh)


def _place():
    x, y, c = lax.axis_index("x"), lax.axis_index("y"), lax.axis_index("c")
    return x, y, c, 4 * x + 2 * y + c


def _layer_of(ref, l):
    return ref[:, l, :] if ref.shape[1] == DEPTH else ref[l]


def _layer_shape(shard):
    return (shard.shape[0], shard.shape[2]) if shard.shape[1] == DEPTH else shard.shape[1:]


def cast_shards(shards, wanted):
    n = len(shards)

    def body(*refs):
        for o_ref, (w, l) in zip(refs[n:], wanted):
            o_ref[...] = _layer_of(refs[w], l).astype(BF16)

    return pl.pallas_call(
        body, name="cast_shards", in_specs=[WHOLE] * n, out_specs=[WHOLE] * len(wanted),
        out_shape=[jax.ShapeDtypeStruct(_layer_shape(shards[w]), BF16) for w, _ in wanted],
        compiler_params=_cparams(),
    )(*shards)


def _peer(k):
    x, y, c, _ = _place()
    px = 1 - x if k & 4 else x
    py = 1 - y if k & 2 else y
    pc = 1 - c if k & 1 else c
    return (px, py, pc), 4 * px + 2 * py + pc


def exchange_start(srcs, after, gather, name):
    n = len(srcs)
    land_shapes = [((N_DEV,) + s.shape) if gather else s.shape for s in srcs]

    def body(*refs):
        src_refs, land_refs = refs[:n], refs[n:2 * n]
        send_sems, recv_sems = refs[2 * n + 1:3 * n + 1], refs[3 * n + 1:4 * n + 1]
        token = refs[-1]
        _, _, _, me = _place()
        for k in range(1, N_DEV):
            peer, peer_id = _peer(k)
            for w in range(n):
                pltpu.make_async_remote_copy(
                    src_ref=src_refs[w] if gather else src_refs[w].at[peer_id], dst_ref=land_refs[w].at[me],
                    send_sem=send_sems[w], recv_sem=recv_sems[w], device_id=peer, device_id_type=MESH).start()
        token[...] = jnp.zeros(token.shape, F32)

    hbm = lambda a: pltpu.with_memory_space_constraint(a, pltpu.HBM)
    outs = pl.pallas_call(
        body, name=name,
        out_shape=(pltpu.SemaphoreType.DMA(()),) * (2 * n)
                  + tuple(pltpu.HBM(s.shape, BF16) for s in srcs) + tuple(pltpu.HBM(s, BF16) for s in land_shapes)
                  + (jax.ShapeDtypeStruct((8, 128), F32),),
        in_specs=[HBM_SPEC] * (2 * n) + [ANY],
        out_specs=(SEM_SPEC,) * (2 * n) + (HBM_SPEC,) * (2 * n) + (WHOLE,),
        input_output_aliases={i: 2 * n + i for i in range(2 * n)},
        compiler_params=pltpu.CompilerParams(has_side_effects=pltpu.SideEffectType.DATAFLOW_SIDE_EFFECTING),
    )(*[hbm(s) for s in srcs], *[hbm(lax.empty(s, BF16)) for s in land_shapes], after)
    return list(outs[:n]), list(outs[n:2 * n]), list(outs[2 * n:3 * n]), list(outs[3 * n:4 * n]), outs[-1]


def exchange_wait(started, after, name):
    send_sems, recv_sems, srcs, lands, _ = started
    n = len(srcs)

    def body(*refs):
        land_refs = refs[n:2 * n]
        send_sems, recv_sems = refs[2 * n:3 * n], refs[3 * n:4 * n]
        x, y, c, _ = _place()
        for w in range(n):
            seven = land_refs[w].at[pl.ds(0, N_DEV - 1)]
            cp = pltpu.make_async_remote_copy(src_ref=seven, dst_ref=seven, send_sem=send_sems[w], recv_sem=recv_sems[w],
                                              device_id=(x, y, c), device_id_type=MESH)
            cp.wait_send()
            cp.wait_recv()

    outs = pl.pallas_call(
        body, name=name,
        out_shape=tuple(pltpu.HBM(s.shape, BF16) for s in srcs) + tuple(pltpu.HBM(l.shape, BF16) for l in lands),
        in_specs=[HBM_SPEC] * (2 * n) + [SEM_SPEC] * (2 * n) + [ANY],
        out_specs=(HBM_SPEC,) * (2 * n),
        input_output_aliases={i: i for i in range(2 * n)},
        compiler_params=pltpu.CompilerParams(has_side_effects=pltpu.SideEffectType.DATAFLOW_SIDE_EFFECTING),
    )(*srcs, *lands, *send_sems, *recv_sems, after)
    _, _, _, me = _place()
    filled = []
    for src, land in zip(outs[:n], outs[n:]):
        own = src[None] if src.ndim == 2 else lax.dynamic_slice_in_dim(src, me, 1, axis=0)
        filled.append(lax.dynamic_update_slice_in_dim(land, own, me, axis=0))
    return filled


def _adamw(w, g, m, v):
    m2 = ADAM_B1 * m + (1.0 - ADAM_B1) * g
    v2 = ADAM_B2 * v + (1.0 - ADAM_B2) * (g * g)
    m_hat = m2 / (1.0 - ADAM_B1 ** ADAM_STEP)
    v_hat = v2 / (1.0 - ADAM_B2 ** ADAM_STEP)
    delta = -ADAM_LR * (m_hat / (jnp.sqrt(v_hat) + ADAM_EPS) + ADAM_WD * w)
    return delta, m2, v2


def adamw_sharded(parts, w, m, v, name):
    L, R, C = w.shape
    fits = [d for d in range(16, min(R, 512) + 1, 16) if R % d == 0]
    br = max(fits) if fits else R
    nblk = R // br

    def body(*refs):
        p_refs = refs[:L]
        w_ref, m_ref, v_ref, g_ref, d_ref, m2_ref, v2_ref = refs[L:]

        def total(p_ref):
            g = p_ref[0].astype(F32)
            for s in range(1, N_DEV):
                g = g + p_ref[s].astype(F32)
            return g

        g = total(p_refs[0])
        for l in range(1, L):
            g = jnp.where(pl.program_id(0) == l, total(p_refs[l]), g)
        g_ref[...] = g
        d_ref[...], m2_ref[...], v2_ref[...] = _adamw(w_ref[...], g, m_ref[...], v_ref[...])

    blk = pl.BlockSpec((None, br, C), lambda l, i: (l, i, 0))

    def part_spec(k):
        return pl.BlockSpec((N_DEV, br, C), lambda l, i: (0, jnp.where(l == k, i, jnp.where(l < k, 0, nblk - 1)), 0))

    return pl.pallas_call(
        body, grid=(L, nblk), name=name,
        in_specs=[part_spec(k) for k in range(L)] + [blk, blk, blk],
        out_specs=[blk] * 4,
        out_shape=[jax.ShapeDtypeStruct((L, R, C), F32)] * 4,
        compiler_params=_cparams(),
    )(*parts, w, m, v)


def allreduce_adamw_small(gpart, w, m, v):
    R = gpart.shape[0]

    def body(g_ref, w_ref, m_ref, v_ref, grad_ref, d_ref, m2_ref, v2_ref, all_ref, send_sems, recv_sems):
        x, y, c, me = _place()
        sibling = (x, y, 1 - c)
        chips = [(1 - x, y), (x, 1 - y), (1 - x, 1 - y)]

        def copy(k, block_id, to, from_input):
            return pltpu.make_async_remote_copy(
                src_ref=g_ref if from_input else all_ref.at[block_id], dst_ref=all_ref.at[block_id],
                send_sem=send_sems.at[k], recv_sem=recv_sems.at[k], device_id=to, device_id_type=MESH)

        def block_of(cx, cy, cc):
            return 4 * cx + 2 * cy + cc

        all_ref[me] = g_ref[...]
        first = [copy(0, me, sibling, True)] + [copy(1 + j, me, (*chip, c), True) for j, chip in enumerate(chips)]
        for cp in first:
            cp.start()
        passed = [copy(4 + j, block_of(*chip, c), sibling, False) for j, chip in enumerate(chips)]
        for j, chip in enumerate(chips):
            copy(1 + j, block_of(*chip, c), (x, y, c), False).wait_recv()
            passed[j].start()
        copy(0, block_of(x, y, 1 - c), (x, y, c), False).wait_recv()
        for j, chip in enumerate(chips):
            copy(4 + j, block_of(*chip, 1 - c), (x, y, c), False).wait_recv()
        for cp in first + passed:
            cp.wait_send()
        g = all_ref[0]
        for s in range(1, N_DEV):
            g = g + all_ref[s]
        grad_ref[...] = g
        d_ref[...], m2_ref[...], v2_ref[...] = _adamw(w_ref[...], g, m_ref[...], v_ref[...])

    return pl.pallas_call(
        body, name="allreduce_adamw_small",
        in_specs=[WHOLE] * 4, out_specs=[WHOLE] * 4,
        out_shape=[jax.ShapeDtypeStruct((R, 128), F32)] * 4,
        scratch_shapes=[pltpu.VMEM((N_DEV, R, 128), F32), pltpu.SemaphoreType.DMA((7,)), pltpu.SemaphoreType.DMA((7,))],
        compiler_params=_cparams(),
    )(gpart, w, m, v)


def _shard_view(name, a):
    if name == "w_in":
        return a.transpose(2, 0, 1)
    return a.swapaxes(1, 2) if name in TRANSPOSED else a


def _shard_unview(name, a):
    if name == "w_in":
        return a.transpose(1, 2, 0)
    return a.swapaxes(1, 2) if name in TRANSPOSED else a


def _pad_head_rows(w, width):
    c = w.shape[1]
    return jnp.pad(w.reshape(HEADS, width, c), ((0, 0), (0, HEAD_PAD - width), (0, 0))).reshape(HEADS * HEAD_PAD, c)


def _unpad_head_rows(w, width):
    c = w.shape[1]
    return w.reshape(HEADS, HEAD_PAD, c)[:, :width].reshape(HEADS * width, c)


O1 = Q_LORA
O2 = O1 + KV_LORA
O3 = O2 + QK_ROPE


def _mixer_weights(gw):
    w_in = gw["w_in"].reshape(IN_WIDTH, D_MODEL)
    zero = lambda n: jnp.zeros((n, D_MODEL), BF16)
    win = jnp.concatenate([w_in[:O2], zero(QK_NOPE), w_in[O2:O3], zero(HEAD_PAD - QK_HEAD), w_in[O3:]], axis=0)
    wq = _pad_head_rows(gw["w_q_up"].reshape(HEADS * QK_HEAD, Q_LORA), QK_HEAD)
    w_kv = gw["w_kv_up"].reshape(HEADS, QK_NOPE + V_HEAD, KV_LORA)
    wk = jnp.pad(w_kv[:, :QK_NOPE], ((0, 0), (0, HEAD_PAD - QK_NOPE), (0, 0))).reshape(HEADS * HEAD_PAD, KV_LORA)
    wv = w_kv[:, QK_NOPE:].reshape(HEADS * V_HEAD, KV_LORA)
    wkv = jnp.concatenate([wk, wv], axis=0)
    out = dict(win=win, wq=wq, wkv=wkv)
    if "w_out" in gw:
        out["wout"] = gw["w_out"].reshape(D_MODEL, D_MODEL)
    return out


def _ffn_weights(gw):
    return dict(wg=gw["w_gate"].reshape(FFN_HIDDEN, D_MODEL), wu=gw["w_up"].reshape(FFN_HIDDEN, D_MODEL),
                wd=gw["w_down"].reshape(FFN_HIDDEN, D_MODEL))


def _layer_small(p, l):
    row = lambda a: a.reshape(1, -1)
    pad_head = lambda g: jnp.pad(g, (0, HEAD_PAD - QK_HEAD)).reshape(1, HEAD_PAD)
    tril = jnp.tril(jnp.ones((CHUNK, CHUNK), F32))
    wsp = p["w_spatial"][l] * tril
    wcat = jnp.concatenate([wsp[h] for h in range(HEADS)], axis=1).astype(BF16)
    wcat_t = jnp.concatenate([wsp[h].T for h in range(HEADS)], axis=1).astype(BF16)
    bz = jnp.repeat(p["b_spatial"][l].T, SGU_HEAD_DIM, axis=1)
    wp = jax.scipy.linalg.block_diag(*[p["w_pool"][l][g] for g in range(HEADS)]).astype(BF16)
    return dict(gmix=row(p["g_mix_norm"][l]), gql=row(p["g_q_lat"][l]), gkvl=row(p["g_kv_lat"][l]),
                gqh=pad_head(p["g_q_head"][l]), gkh=pad_head(p["g_k_head"][l]), gsgu=row(p["g_sgu_v"][l]),
                wcat=wcat, wcat_t=wcat_t, bz=bz, wp=wp, pscale=row(p["pool_scale"][l]),
                goa=row(p["g_out_mla"][l]), gos=row(p["g_out_sgu"][l]), gop=row(p["g_out_pool"][l]),
                gffn=row(p["g_ffn_norm"][l]))


MIXER_SIDE = ("w_in", "w_q_up", "w_kv_up", "w_out")
FFN_SIDE = ("w_gate", "w_up", "w_down")
TRANSPOSED = ("w_in", "w_q_up", "w_kv_up", "w_gate", "w_up")
FIRST_0 = ("w_in", "w_q_up", "w_kv_up")
LATER_0 = ("w_out",) + FFN_SIDE
SHARDED = MIXER_SIDE + FFN_SIDE
SMALL = ("g_mix_norm", "g_q_lat", "g_kv_lat", "g_q_head", "g_k_head", "g_sgu_v", "w_spatial", "b_spatial", "w_pool",
         "pool_scale", "g_out_mla", "g_out_sgu", "g_out_pool", "g_ffn_norm")
WEIGHTS = ("g_mix_norm", "w_in", "g_q_lat", "w_q_up", "g_kv_lat", "w_kv_up", "g_q_head", "g_k_head", "g_sgu_v", "w_spatial",
           "b_spatial", "w_pool", "pool_scale", "g_out_mla", "g_out_sgu", "g_out_pool", "w_out", "g_ffn_norm", "w_gate",
           "w_up", "w_down")
PACKED = SMALL + ("loss",)
PACK_ROWS = 8 * 128


def _pack_small(parts):
    flat = []
    for name in PACKED:
        a = parts[name].reshape(-1)
        flat.append(jnp.pad(a, (0, -a.shape[0] % PACK_ROWS)))
    return jnp.concatenate(flat).reshape(-1, 128)


def _unpack_small(packed, like):
    out, row = {}, 0
    for name in PACKED:
        n = math.prod(like[name].shape)
        rows = -(-n // PACK_ROWS) * 8
        out[name] = packed[row:row + rows].reshape(-1)[:n].reshape(like[name].shape)
        row += rows
    return out


def _forward_attention(x, rope, W, S):
    z, q, k, v, kt, vt = mixin_fwd(x, rope, S["gmix"], W["win"], S["gql"], W["wq"], S["gkvl"], W["wkv"], S["gqh"], S["gkh"])
    a, lse = attn_fwd(q, k, vt)
    return dict(x=x, z=z, q=q, k=k, kt=kt, v=v, a=a, lse=lse)


def _forward_mixers(A, wout, S):
    A["x1"] = mixers_fwd(A["x"], A["a"], A["z"], S["gsgu"], S["wcat"], S["bz"], S["wp"], S["pscale"], S["goa"], S["gos"],
                         S["gop"], wout)
    return A["x1"]


def _backward_ffn(dx2, W, S, A, l):
    dx1, h2, act, dg, du_ffn, dgffn = ffn_bwd(dx2, A["x1"], A["gs"], A["us"], S["gffn"], W["wg"], W["wu"], W["wd"])
    d_wd = matmul_tn(act, dx2, f"dw_down_{l}")
    d_wg = matmul_tn(dg, h2, f"dw_gate_{l}")
    d_wu = matmul_tn(du_ffn, h2, f"dw_up_{l}")
    big = {n: d.reshape(N_DEV, -1, D_MODEL) for n, d in (("w_gate", d_wg), ("w_up", d_wu), ("w_down", d_wd))}
    return dx1, big, dgffn


def _backward_mixers(dx1, dgffn, rope, W, S, A, l, send_wout=None):
    (da, delta, du, dvs, dm, mix, dgoa, dgos, dgop, dps, dwp, dwsp, db, dgsgu) = mixers_bwd(
        dx1, A["a"], A["z"], S["gsgu"], S["wcat"], S["wcat_t"], S["bz"], S["wp"], S["pscale"], S["goa"], S["gos"],
        S["gop"], W["wout"])
    d_wout = matmul_tn(mix, dx1, f"dw_out_{l}")
    if send_wout is not None:
        delta = delta + send_wout(d_wout.reshape(N_DEV, -1, D_MODEL))[0, 0]
    dpin = pool_bwd(dm)
    dqt, dk, dv = attn_bwd(A["q"], A["k"], A["kt"], A["v"], da, A["lse"], delta)
    (dx, hn, dz, qn, dq_up, kvn, dkv_up, dgmix, dgql, dgkvl, dgqh, dgkh) = mixin_bwd(
        dx1, A["x"], A["z"], rope, dqt, dk, dv, du, dvs, dpin, S["gmix"], W["win"], S["gql"], W["wq"], S["gkvl"],
        W["wkv"], S["gqh"], S["gkh"])
    d_win = matmul_tn(dz, hn, f"dw_in_{l}")
    d_wq = matmul_tn(dq_up, qn, f"dw_q_up_{l}")
    d_wkv = matmul_tn(dkv_up, kvn, f"dw_kv_up_{l}")
    d_win = jnp.concatenate([d_win[:O2], d_win[O2 + QK_NOPE:O2 + QK_HEAD], d_win[O2 + HEAD_PAD:]], axis=0)
    d_wk = d_wkv[:HEADS * HEAD_PAD].reshape(HEADS, HEAD_PAD, KV_LORA)[:, :QK_NOPE]
    d_wv = d_wkv[HEADS * HEAD_PAD:].reshape(HEADS, V_HEAD, KV_LORA)
    d_wkv = jnp.concatenate([d_wk, d_wv], axis=1)
    big = dict(w_in=d_win.reshape(N_DEV, -1, D_MODEL), w_q_up=_unpad_head_rows(d_wq, QK_HEAD).reshape(N_DEV, -1, Q_LORA),
               w_kv_up=d_wkv.reshape(N_DEV, -1, KV_LORA), w_out=d_wout.reshape(N_DEV, -1, D_MODEL))
    tril = jnp.tril(jnp.ones((CHUNK, CHUNK), F32))
    small = dict(g_mix_norm=dgmix[0], g_q_lat=dgql[0], g_kv_lat=dgkvl[0], g_q_head=dgqh[0, :QK_HEAD], g_k_head=dgkh[0, :QK_HEAD],
                 g_sgu_v=dgsgu[0], w_spatial=dwsp.reshape(HEADS, CHUNK, CHUNK) * tril, b_spatial=db[:, :HEADS].T,
                 w_pool=jnp.stack([dwp[64 * g:64 * (g + 1), 64 * g:64 * (g + 1)] for g in range(HEADS)]),
                 pool_scale=dps[0], g_out_mla=dgoa[0], g_out_sgu=dgos[0], g_out_pool=dgop[0], g_ffn_norm=dgffn[0])
    return dx, big, small


def kernel(x, positions, g_mix_norm, w_in, g_q_lat, w_q_up, g_kv_lat, w_kv_up, g_q_head, g_k_head, g_sgu_v, w_spatial, b_spatial, w_pool, pool_scale, g_out_mla, g_out_sgu, g_out_pool, w_out, g_ffn_norm, w_gate, w_up, w_down, loss_target, m_g_mix_norm, m_w_in, m_g_q_lat, m_w_q_up, m_g_kv_lat, m_w_kv_up, m_g_q_head, m_g_k_head, m_g_sgu_v, m_w_spatial, m_b_spatial, m_w_pool, m_pool_scale, m_g_out_mla, m_g_out_sgu, m_g_out_pool, m_w_out, m_g_ffn_norm, m_w_gate, m_w_up, m_w_down, v_g_mix_norm, v_w_in, v_g_q_lat, v_w_q_up, v_g_kv_lat, v_w_kv_up, v_g_q_head, v_g_k_head, v_g_sgu_v, v_w_spatial, v_b_spatial, v_w_pool, v_pool_scale, v_g_out_mla, v_g_out_sgu, v_g_out_pool, v_w_out, v_g_ffn_norm, v_w_gate, v_w_up, v_w_down):
    given = dict(locals())
    w = {n: given[n] for n in WEIGHTS}
    m = {n: given["m_" + n] for n in WEIGHTS}
    v = {n: given["v_" + n] for n in WEIGHTS}
    T = x.shape[1]
    xs = x.reshape(T, D_MODEL)

    half = QK_ROPE // 2
    inv_freq = 1.0 / (ROPE_THETA ** (jnp.arange(half, dtype=F32) / half))
    ang16 = positions.reshape(T).astype(F32)[:, None] * inv_freq
    ang = jnp.concatenate([jnp.zeros((T, QK_NOPE), F32), ang16, ang16, jnp.zeros((T, HEAD_PAD - QK_HEAD), F32)], axis=1)

    wv = {n: _shard_view(n, w[n]) for n in SHARDED}
    wanted = [(SHARDED.index(n), 0) for n in FIRST_0 + LATER_0] + [(i, 1) for i in range(len(SHARDED))]
    bf = cast_shards([wv[n] for n in SHARDED], wanted)
    nf, nl = len(FIRST_0), len(LATER_0)
    ag_first0 = exchange_start(bf[:nf], ang, True, "ag_start_first0")
    rope = rope_tables(ang + ag_first0[-1][0, 0])
    W0 = dict(zip(FIRST_0, exchange_wait(ag_first0, rope[0], "ag_wait_first0")))
    ag_later0 = exchange_start(bf[nf:nf + nl], W0["w_in"], True, "ag_start_later0")
    ag_l1 = exchange_start(bf[nf + nl:], ag_later0[-1], True, "ag_start_l1")
    Ss = [_layer_small(w, l) for l in range(DEPTH)]

    S0 = dict(Ss[0], gmix=Ss[0]["gmix"] + (ag_later0[-1][0, 0] + ag_l1[-1][0, 0]))
    A0 = _forward_attention(xs, rope, _mixer_weights(W0), S0)
    W0.update(zip(LATER_0, exchange_wait(ag_later0, A0["a"], "ag_wait_later0")))
    W0 = {**_mixer_weights(W0), **_ffn_weights(W0)}
    x1 = _forward_mixers(A0, W0["wout"], Ss[0])
    h, A0["gs"], A0["us"] = ffn_fwd(x1, Ss[0]["gffn"], W0["wg"], W0["wu"], W0["wd"])
    layer1 = dict(zip(SHARDED, exchange_wait(ag_l1, h, "ag_wait_l1")))
    W1 = {**_mixer_weights(layer1), **_ffn_weights(layer1)}
    A1 = _forward_attention(h, rope, W1, Ss[1])
    x1 = _forward_mixers(A1, W1["wout"], Ss[1])
    dh, loss_part, A1["gs"], A1["us"] = ffn_fwd(x1, Ss[1]["gffn"], W1["wg"], W1["wu"], W1["wd"],
                                                target=loss_target.reshape(T, D_MODEL))

    dx1, big_f1, dgffn1 = _backward_ffn(dh, W1, Ss[1], A1, 1)
    dh, big_m1, small1 = _backward_mixers(dx1, dgffn1, rope, W1, Ss[1], A1, 1)
    rs_l1 = exchange_start([{**big_m1, **big_f1}[n] for n in SHARDED], dh, False, "rs_start_l1")
    S0 = dict(Ss[0], gffn=Ss[0]["gffn"] + rs_l1[-1][0, 0])
    dx1, big_f0, dgffn0 = _backward_ffn(dh, W0, S0, A0, 0)
    rs_ffn0 = exchange_start([big_f0[n] for n in FFN_SIDE], dx1, False, "rs_start_ffn0")
    S0 = dict(Ss[0], gsgu=Ss[0]["gsgu"] + rs_ffn0[-1][0, 0])
    rs_out0 = []

    def send_wout(pieces):
        rs_out0.append(exchange_start([pieces], dx1, False, "rs_start_out0"))
        return rs_out0[0][-1]

    dh, big_m0, small0 = _backward_mixers(dx1, dgffn0, rope, W0, S0, A0, 0, send_wout)
    grad_x = dh.reshape(x.shape)
    smalls = [small0, small1]

    rs_first0 = exchange_start([big_m0[n] for n in FIRST_0], dh, False, "rs_start_first0")
    parts1 = dict(zip(SHARDED, exchange_wait(rs_l1, rs_first0[-1], "rs_wait_l1")))
    parts0 = dict(zip(FFN_SIDE, exchange_wait(rs_ffn0, rs_first0[-1], "rs_wait_ffn0")))
    parts0["w_out"] = exchange_wait(rs_out0[0], rs_first0[-1], "rs_wait_out0")[0]
    grad, delta, new_m, new_v = {}, {}, {}, {}

    small_part = {n: jnp.stack([smalls[l][n] for l in range(DEPTH)]) for n in SMALL}
    small_part["loss"] = loss_part
    no_loss = {"loss": jnp.zeros((1, 1), F32)}
    outs = allreduce_adamw_small(_pack_small(small_part), *[_pack_small({**a, **no_loss}) for a in (w, m, v)])
    for d, o in zip((grad, delta, new_m, new_v), outs):
        d.update(_unpack_small(o, {**w, **no_loss}))
    loss = grad["loss"][0, 0]

    for n in LATER_0 + FIRST_0:
        if n == FIRST_0[0]:
            parts0.update(zip(FIRST_0, exchange_wait(rs_first0, grad["w_down"], "rs_wait_first0")))
        parts, views = [parts0[n], parts1[n]], [_shard_view(n, a[n]) for a in (w, m, v)]
        if n == "w_in":
            parts = [jnp.stack(parts, axis=2).reshape(N_DEV, -1, D_MODEL)]
            views = [a.reshape(1, -1, D_MODEL) for a in views]
        outs = adamw_sharded(parts, *views, f"adamw_{n}")
        if n == "w_in":
            outs = [o.reshape(-1, DEPTH, D_MODEL) for o in outs]
        grad[n], delta[n], new_m[n], new_v[n] = [_shard_unview(n, o) for o in outs]

    return (loss, grad_x, *[grad[n] for n in WEIGHTS], *[delta[n] for n in WEIGHTS], *[new_m[n] for n in WEIGHTS],
            *[new_v[n] for n in WEIGHTS])
```

```python
import functools
import math

import jax
import jax.numpy as jnp
from jax import lax
from jax.experimental import pallas as pl
from jax.experimental.pallas import tpu as pltpu

F32 = jnp.float32
BF16 = jnp.bfloat16

N_DEV = 8
DEPTH = 2
D_MODEL = 1024
HEADS = 4
HEAD_PAD = 128
QK_NOPE = 64
QK_ROPE = 32
QK_HEAD = QK_NOPE + QK_ROPE
V_HEAD = 128
Q_LORA = 256
KV_LORA = 128
SGU_WIDTH = 256
SGU_HEAD_DIM = 64
CHUNK = 128
POOL_WIDTH = 256
POOL_WINDOWS = (2, 4, 8, 16)
POOL_HALO = 16
MLA_WIDTH = 512
IN_WIDTH = 1184
Z_WIDTH = 1280
FFN_HIDDEN = 2816
FFN_CHUNK = 256
ROPE_THETA = 10000.0
EPS = 1e-6
ATTN_SCALE = 1.0 / math.sqrt(QK_HEAD)
LOG2E = 1.4426950408889634
NEG_BIG = -1e30

ADAM_LR = 0.001
ADAM_B1 = 0.9
ADAM_B2 = 0.999
ADAM_EPS = 1e-08
ADAM_WD = 0.01
ADAM_STEP = 10

VMEM_LIMIT = 56 * 1024 * 1024
ROW_TILE = 512
MIXIN_PART = 256
MIXIN_BWD_PART = 128
ATTN_TILE = 1024
ATTN_SUB = 512
ATTN_FWD_SUB = 512
ATTN_BWD_HEADS = 2
MESH = pl.DeviceIdType.MESH

WHOLE = pl.BlockSpec(memory_space=pltpu.VMEM)
ANY = pl.BlockSpec(memory_space=pl.ANY)
HBM_SPEC = pl.BlockSpec(memory_space=pltpu.HBM)
SEM_SPEC = pl.BlockSpec(memory_space=pltpu.SEMAPHORE)


def _cparams(**kw):
    return pltpu.CompilerParams(vmem_limit_bytes=VMEM_LIMIT, **kw)


def _dot(a, b):
    return jnp.dot(a, b, preferred_element_type=F32)


def _dot_nt(a, b):
    return lax.dot_general(a, b, (((1,), (1,)), ((), ())), preferred_element_type=F32)


def _dot_tn(a, b):
    return lax.dot_general(a, b, (((0,), (0,)), ((), ())), preferred_element_type=F32)


def _rms(x, g, n):
    r = lax.rsqrt(jnp.sum(x * x, axis=-1, keepdims=True) * (1.0 / n) + EPS)
    return x * r * g, r


def _rms_bwd(x, r, g, dy, n):
    gdy = dy * g
    dx = r * gdy - x * (r * r * r) * (jnp.sum(x * gdy, axis=-1, keepdims=True) * (1.0 / n))
    dg = jnp.sum(dy * (x * r), axis=0, keepdims=True)
    return dx, dg


def _sigmoid(x):
    return 1.0 / (1.0 + jnp.exp(-x))


def rope_tables(ang):
    T = ang.shape[0]
    tm = min(ROW_TILE, T)

    def body(ang_ref, c_ref, sa_ref, sb_ref):
        a = ang_ref[...]
        lane = lax.broadcasted_iota(jnp.int32, a.shape, 1)
        s = jnp.sin(a)
        c_ref[...] = jnp.cos(a)
        sa_ref[...] = jnp.where(lane < QK_NOPE + QK_ROPE // 2, -s, 0.0)
        sb_ref[...] = jnp.where(lane >= QK_NOPE + QK_ROPE // 2, s, 0.0)

    row = pl.BlockSpec((tm, HEAD_PAD), lambda i: (i, 0))
    return pl.pallas_call(
        body, grid=(T // tm,), name="rope_tables", in_specs=[row], out_specs=[row] * 3,
        out_shape=[jax.ShapeDtypeStruct((T, HEAD_PAD), F32)] * 3, compiler_params=_cparams(),
    )(ang)


def _rope(x, c, sa, sb):
    half = QK_ROPE // 2
    return x * c + pltpu.roll(x, HEAD_PAD - half, 1) * sa + pltpu.roll(x, half, 1) * sb


def _rope_bwd(dy, c, sa, sb):
    half = QK_ROPE // 2
    return dy * c + pltpu.roll(dy * sa, half, 1) + pltpu.roll(dy * sb, HEAD_PAD - half, 1)


def _skewed(stages, groups):
    for t in range(len(stages) + len(groups) - 1):
        for p, g in enumerate(groups):
            if 0 <= t - p < len(stages):
                stages[t - p](g)


def _head_masks(shape, width):
    lane = lax.broadcasted_iota(jnp.int32, shape, len(shape) - 1)
    return [(lane >= width * h) & (lane < width * (h + 1)) for h in range(HEADS)]


def _qkv_pre(z, gql, wq, gkvl, wkv):
    ql = z[:, 0:Q_LORA]
    kvl = z[:, Q_LORA:Q_LORA + KV_LORA]
    kr = z[:, Q_LORA + KV_LORA:Q_LORA + KV_LORA + HEAD_PAD]
    qn, rq = _rms(ql, gql, Q_LORA)
    kvn, rkv = _rms(kvl, gkvl, KV_LORA)
    qn = qn.astype(BF16)
    kvn = kvn.astype(BF16)
    q_up = _dot_nt(qn, wq)
    kv_up = _dot_nt(kvn, wkv)
    return ql, kvl, kr, qn, rq, kvn, rkv, q_up, kv_up


def mixin_fwd(x, rope, gmix, win, gql, wq, gkvl, wkv, gqh, gkh):
    T = x.shape[0]
    tm = min(2 * ROW_TILE, T)
    part = min(MIXIN_PART, tm)

    def body(x_ref, c_ref, sa_ref, sb_ref, gmix_ref, win_ref, gql_ref, wq_ref, gkvl_ref, wkv_ref, gqh_ref, gkh_ref,
             z_ref, q_ref, k_ref, v_ref, kt_ref, vt_ref):
        def project(g):
            hn = _rms(x_ref[g["rows"], :], gmix_ref[...], D_MODEL)[0].astype(BF16)
            g["z"] = _dot_nt(hn, win_ref[...])
            z_ref[g["rows"], :] = g["z"]

        def latents(g):
            _, _, g["kr"], _, _, _, _, g["q_up"], g["kv_up"] = _qkv_pre(g["z"], gql_ref[...], wq_ref[...], gkvl_ref[...],
                                                                        wkv_ref[...])

        def heads(g):
            rows = g["rows"]
            c, sa, sb = c_ref[rows, :], sa_ref[rows, :], sb_ref[rows, :]
            for h in range(HEADS):
                lo = HEAD_PAD * h
                qh = _rms(g["q_up"][:, lo:lo + HEAD_PAD], gqh_ref[...], QK_HEAD)[0]
                q_ref[h, rows, :] = (_rope(qh, c, sa, sb) * (ATTN_SCALE * LOG2E)).astype(BF16)
                kh = _rope(_rms(g["kv_up"][:, lo:lo + HEAD_PAD] + g["kr"], gkh_ref[...], QK_HEAD)[0], c, sa, sb)
                k_ref[h, rows, :] = kh.astype(BF16)
                kt_ref[h, :, rows] = jnp.transpose(kh).astype(BF16)
                vh = g["kv_up"][:, HEADS * HEAD_PAD + lo:HEADS * HEAD_PAD + lo + HEAD_PAD]
                v_ref[h, rows, :] = vh.astype(BF16)
                vt_ref[h, :, rows] = jnp.transpose(vh).astype(BF16)

        _skewed((project, latents, heads), [dict(rows=pl.ds(part * p, part)) for p in range(tm // part)])

    row = lambda w: pl.BlockSpec((tm, w), lambda i: (i, 0))
    head = pl.BlockSpec((HEADS, tm, HEAD_PAD), lambda i: (0, i, 0))
    head_t = pl.BlockSpec((HEADS, HEAD_PAD, tm), lambda i: (0, 0, i))
    return pl.pallas_call(
        body, grid=(T // tm,), name="mixin_fwd",
        in_specs=[row(D_MODEL)] + [row(HEAD_PAD)] * 3 + [WHOLE] * 8,
        out_specs=[row(Z_WIDTH), head, head, head, head_t, head_t],
        out_shape=[jax.ShapeDtypeStruct((T, Z_WIDTH), F32)] + [jax.ShapeDtypeStruct((HEADS, T, HEAD_PAD), BF16)] * 3
                  + [jax.ShapeDtypeStruct((HEADS, HEAD_PAD, T), BF16)] * 2,
        compiler_params=_cparams(),
    )(x, *rope, gmix, win, gql, wq, gkvl, wkv, gqh, gkh)


def _pair_tables(pairs):
    return (jnp.asarray([p[0] for p in pairs], jnp.int32), jnp.asarray([p[1] for p in pairs], jnp.int32))


def attn_fwd(q, k, vt):
    _, T, _ = q.shape
    tb = min(ATTN_TILE, T)
    sq = min(ATTN_FWD_SUB, tb)
    nb = T // tb

    pairs = [(i, j) for i in range(nb) for j in range(i + 1)]

    def body(i_tab, j_tab, q_ref, k_ref, vt_ref, o_ref, lse_ref, m_s, l_s, acc_s):
        i, j = i_tab[pl.program_id(0)], j_tab[pl.program_id(0)]

        @pl.when(j == 0)
        def _():
            m_s[...] = jnp.full(m_s.shape, -jnp.inf, F32)
            l_s[...] = jnp.zeros(l_s.shape, F32)
            acc_s[...] = jnp.zeros(acc_s.shape, F32)

        def scores(g):
            st = _dot_nt(k_ref[g["h"], g["kb"], :], q_ref[g["h"], g["qa"], :])
            if g["masked"]:
                krow = lax.broadcasted_iota(jnp.int32, st.shape, 0)
                qcol = g["q0"] + lax.broadcasted_iota(jnp.int32, st.shape, 1)
                st = jnp.where(krow <= qcol, st, NEG_BIG)
            g["st"] = st

        def new_max(g):
            h, qa = g["h"], g["qa"]
            m_prev = m_s[h, :, qa]
            g["m"] = jnp.maximum(m_prev, jnp.max(g["st"], axis=0, keepdims=True))
            g["alpha"] = jnp.exp2(m_prev - g["m"])
            m_s[h, :, qa] = g["m"]

        def weights(g):
            pt = jnp.exp2(g["st"] - g["m"])
            g["lsum"] = jnp.sum(pt, axis=0, keepdims=True)
            g["pt"] = pt.astype(BF16)

        def accumulate(g):
            h, qa = g["h"], g["qa"]
            l_s[h, :, qa] = g["alpha"] * l_s[h, :, qa] + g["lsum"]
            acc_s[h, :, qa] = g["alpha"] * acc_s[h, :, qa] + _dot(vt_ref[h, :, g["kb"]], g["pt"])

        def tiles(masked):
            return [dict(h=h, q0=q0, qa=slice(q0, q0 + sq), kb=slice(0, q0 + sq if masked else tb), masked=masked)
                    for h in range(HEADS) for q0 in range(0, tb, sq)]

        @pl.when(j < i)
        def _():
            _skewed((scores, new_max, weights, accumulate), tiles(False))

        @pl.when(j == i)
        def _():
            _skewed((scores, new_max, weights, accumulate), tiles(True))
            for h in range(HEADS):
                l = l_s[h]
                o_ref[:, HEAD_PAD * h:HEAD_PAD * (h + 1)] = jnp.transpose(acc_s[h] / l)
                lse_ref[h] = m_s[h] + jnp.log2(l)

    qspec = pl.BlockSpec((HEADS, tb, HEAD_PAD), lambda p, it, jt: (0, it[p], 0))
    kspec = pl.BlockSpec((HEADS, tb, HEAD_PAD), lambda p, it, jt: (0, jt[p], 0))
    vspec = pl.BlockSpec((HEADS, HEAD_PAD, tb), lambda p, it, jt: (0, 0, jt[p]))
    grid_spec = pltpu.PrefetchScalarGridSpec(
        num_scalar_prefetch=2, grid=(len(pairs),),
        in_specs=[qspec, kspec, vspec],
        out_specs=[pl.BlockSpec((tb, MLA_WIDTH), lambda p, it, jt: (it[p], 0)),
                   pl.BlockSpec((HEADS, 1, tb), lambda p, it, jt: (0, 0, it[p]))],
        scratch_shapes=[pltpu.VMEM((HEADS, 1, tb), F32), pltpu.VMEM((HEADS, 1, tb), F32), pltpu.VMEM((HEADS, HEAD_PAD, tb), F32)])
    return pl.pallas_call(
        body, grid_spec=grid_spec, name="attn_fwd",
        out_shape=[jax.ShapeDtypeStruct((T, MLA_WIDTH), F32), jax.ShapeDtypeStruct((HEADS, 1, T), F32)],
        compiler_params=_cparams(),
    )(*_pair_tables(pairs), q, k, vt)


def _sgu_fwd_chunk(vn_c, wcat, bz, masks):
    vstack = jnp.concatenate([jnp.where(mk, vn_c, 0.0).astype(BF16) for mk in masks], axis=0)
    return _dot(wcat, vstack) + bz


def _pool_counts(i, tm):
    pos1 = (i * tm + 1 + lax.broadcasted_iota(jnp.int32, (tm, POOL_WIDTH), 0)).astype(F32)
    lane = lax.broadcasted_iota(jnp.int32, (tm, POOL_WIDTH), 1)
    win = jnp.where(lane < 64, 2.0, jnp.where(lane < 128, 4.0, jnp.where(lane < 192, 8.0, 16.0)))
    return jnp.minimum(pos1, win), lane


def _by_group(lane, s2, s4, s8, s16):
    return jnp.where(lane < 64, s2, jnp.where(lane < 128, s4, jnp.where(lane < 192, s8, s16)))


def _pool_means(pin, halo, i, tm):
    s1 = jnp.concatenate([halo, pin], axis=0)
    s2 = s1 + pltpu.roll(s1, 1, 0)
    s4 = s2 + pltpu.roll(s2, 2, 0)
    s8 = s4 + pltpu.roll(s4, 4, 0)
    s16 = s8 + pltpu.roll(s8, 8, 0)
    cnt, lane = _pool_counts(i, tm)
    sel = _by_group(lane, s2[POOL_HALO:], s4[POOL_HALO:], s8[POOL_HALO:], s16[POOL_HALO:])
    return sel / cnt - pin


def _mixers_fwd_tile(i, tm, a, u, vs, pin, halo, gsgu, wcat, bz, wp, pscale, goa, gos, gop):
    vn, rv = _rms(vs, gsgu, SGU_WIDTH)
    masks = _head_masks((CHUNK, SGU_WIDTH), SGU_HEAD_DIM)
    zc = jnp.concatenate([_sgu_fwd_chunk(vn[CHUNK * c:CHUNK * (c + 1)], wcat, bz, masks) for c in range(tm // CHUNK)], axis=0)
    gm = u * zc
    halo = jnp.where(i > 0, halo, 0.0)
    m = _pool_means(pin, halo, i, tm).astype(BF16)
    yp_pre = _dot(m, wp)
    yp = yp_pre * pscale
    na, ra = _rms(a, goa, MLA_WIDTH)
    ng, rg = _rms(gm, gos, SGU_WIDTH)
    npo, rp = _rms(yp, gop, POOL_WIDTH)
    mix = jnp.concatenate([na, ng, npo], axis=1).astype(BF16)
    return vn, rv, zc, gm, m, yp_pre, yp, ra, rg, rp, mix


def _z_specs(tm):
    col = lambda c: pl.BlockSpec((tm, 256), lambda i: (i, c))
    halo = pl.BlockSpec((POOL_HALO, 256), lambda i: (jnp.maximum(i * (tm // POOL_HALO) - 1, 0), 4))
    return [col(2), col(3), col(4), halo]


def mixers_fwd(x, a, z, gsgu, wcat, bz, wp, pscale, goa, gos, gop, wout):
    T = x.shape[0]
    tm = min(ROW_TILE, T)

    def body(x_ref, a_ref, u_ref, vs_ref, pin_ref, halo_ref, gsgu_ref, wcat_ref, bz_ref, wp_ref, ps_ref,
             goa_ref, gos_ref, gop_ref, wout_ref, x1_ref):
        i = pl.program_id(0)
        mix = _mixers_fwd_tile(i, tm, a_ref[...], u_ref[...], vs_ref[...], pin_ref[...], halo_ref[...], gsgu_ref[...],
                               wcat_ref[...], bz_ref[...], wp_ref[...], ps_ref[...], goa_ref[...], gos_ref[...],
                               gop_ref[...])[-1]
        x1_ref[...] = x_ref[...] + _dot(mix, wout_ref[...])

    row = lambda w: pl.BlockSpec((tm, w), lambda i: (i, 0))
    return pl.pallas_call(
        body, grid=(T // tm,), name="mixers_fwd",
        in_specs=[row(D_MODEL), row(MLA_WIDTH)] + _z_specs(tm) + [WHOLE] * 9,
        out_specs=row(D_MODEL),
        out_shape=jax.ShapeDtypeStruct((T, D_MODEL), F32),
        compiler_params=_cparams(),
    )(x, a, z, z, z, z, gsgu, wcat, bz, wp, pscale, goa, gos, gop, wout)


def ffn_fwd(x1, gffn, wg, wu, wd, target=None):
    T = x1.shape[0]
    tm = min(ROW_TILE, T)
    with_loss = target is not None

    def body(*refs):
        x1_ref, gffn_ref, wg_ref, wu_ref, wd_ref = refs[:5]
        outs = refs[6:] if with_loss else refs[5:]
        x1v = x1_ref[...]
        h2 = _rms(x1v, gffn_ref[...], D_MODEL)[0].astype(BF16)
        acc = x1v
        for c in range(FFN_HIDDEN // FFN_CHUNK):
            sl = slice(FFN_CHUNK * c, FFN_CHUNK * (c + 1))
            g = _dot_nt(h2, wg_ref[sl, :])
            u = _dot_nt(h2, wu_ref[sl, :])
            outs[-2][:, sl] = g.astype(BF16)
            outs[-1][:, sl] = u.astype(BF16)
            act = (g * _sigmoid(g) * u).astype(BF16)
            acc = acc + _dot(act, wd_ref[sl, :])
        if not with_loss:
            outs[0][...] = acc
            return
        dy_ref, loss_ref = outs[0], outs[1]

        @pl.when(pl.program_id(0) == 0)
        def _():
            loss_ref[...] = jnp.zeros(loss_ref.shape, F32)

        err = acc - refs[5][...]
        dy_ref[...] = err * (1.0 / D_MODEL)
        per_row = jnp.sum(err * err, axis=1, keepdims=True) * (1.0 / D_MODEL)
        loss_ref[...] += 0.5 * jnp.sum(per_row, axis=0, keepdims=True)

    row = lambda w: pl.BlockSpec((tm, w), lambda i: (i, 0))
    hidden = [jax.ShapeDtypeStruct((T, FFN_HIDDEN), BF16)] * 2
    if with_loss:
        return pl.pallas_call(
            body, grid=(T // tm,), name="ffn_fwd_loss",
            in_specs=[row(D_MODEL)] + [WHOLE] * 4 + [row(D_MODEL)],
            out_specs=[row(D_MODEL), pl.BlockSpec((1, 1), lambda i: (0, 0)), row(FFN_HIDDEN), row(FFN_HIDDEN)],
            out_shape=[jax.ShapeDtypeStruct((T, D_MODEL), F32), jax.ShapeDtypeStruct((1, 1), F32)] + hidden,
            compiler_params=_cparams(),
        )(x1, gffn, wg, wu, wd, target)
    return pl.pallas_call(
        body, grid=(T // tm,), name="ffn_fwd",
        in_specs=[row(D_MODEL)] + [WHOLE] * 4,
        out_specs=[row(D_MODEL), row(FFN_HIDDEN), row(FFN_HIDDEN)],
        out_shape=[jax.ShapeDtypeStruct((T, D_MODEL), F32)] + hidden,
        compiler_params=_cparams(),
    )(x1, gffn, wg, wu, wd)


def _acc_spec(shape):
    return pl.BlockSpec(shape, lambda i: (0,) * len(shape))


def ffn_bwd(dx2, x1, gs, us, gffn, wg, wu, wd):
    T = x1.shape[0]
    tm = min(ROW_TILE // 2, T)

    def body(dx2_ref, x1_ref, gs_ref, us_ref, gffn_ref, wg_ref, wu_ref, wd_ref,
             dx1_ref, h2_ref, act_ref, dg_ref, du_ref, dgffn_ref):
        @pl.when(pl.program_id(0) == 0)
        def _():
            dgffn_ref[...] = jnp.zeros(dgffn_ref.shape, F32)

        dx2v = dx2_ref[...]
        dy = dx2v.astype(BF16)
        x1v = x1_ref[...]
        h2, r = _rms(x1v, gffn_ref[...], D_MODEL)
        h2_ref[...] = h2.astype(BF16)
        for c in range(FFN_HIDDEN // FFN_CHUNK):
            sl = slice(FFN_CHUNK * c, FFN_CHUNK * (c + 1))
            g = gs_ref[:, sl].astype(F32)
            u = us_ref[:, sl].astype(F32)
            dact = _dot_nt(dy, wd_ref[sl, :])
            sg = _sigmoid(g)
            silu = g * sg
            act_ref[:, sl] = (silu * u).astype(BF16)
            dg_ref[:, sl] = (dact * u * (sg * (1.0 + g * (1.0 - sg)))).astype(BF16)
            du_ref[:, sl] = (dact * silu).astype(BF16)
        dh2 = _dot(dg_ref[...], wg_ref[...]) + _dot(du_ref[...], wu_ref[...])
        dxn, dgn = _rms_bwd(x1v, r, gffn_ref[...], dh2, D_MODEL)
        dx1_ref[...] = dx2v + dxn
        dgffn_ref[...] += dgn

    row = lambda w: pl.BlockSpec((tm, w), lambda i: (i, 0))
    return pl.pallas_call(
        body, grid=(T // tm,), name="ffn_bwd",
        in_specs=[row(D_MODEL), row(D_MODEL), row(FFN_HIDDEN), row(FFN_HIDDEN)] + [WHOLE] * 4,
        out_specs=[row(D_MODEL), row(D_MODEL), row(FFN_HIDDEN), row(FFN_HIDDEN), row(FFN_HIDDEN), _acc_spec((1, D_MODEL))],
        out_shape=[jax.ShapeDtypeStruct((T, D_MODEL), F32), jax.ShapeDtypeStruct((T, D_MODEL), BF16),
                   jax.ShapeDtypeStruct((T, FFN_HIDDEN), BF16), jax.ShapeDtypeStruct((T, FFN_HIDDEN), BF16),
                   jax.ShapeDtypeStruct((T, FFN_HIDDEN), BF16), jax.ShapeDtypeStruct((1, D_MODEL), F32)],
        compiler_params=_cparams(),
    )(dx2, x1, gs, us, gffn, wg, wu, wd)


TN_K_TILE = 2048
TN_ACC_BYTES = 6 * 1024 * 1024


def matmul_tn(a, b, name):
    T, M = a.shape
    N = b.shape[1]
    tk = min(TN_K_TILE, T)
    tm = M if M <= 1024 else M // 2
    tn = max(d for d in range(128, N + 1, 128) if N % d == 0 and tm * d * 4 <= TN_ACC_BYTES)
    nk = T // tk

    def body(a_ref, b_ref, o_ref, acc):
        k = pl.program_id(2)

        @pl.when(k == 0)
        def _():
            acc[...] = jnp.zeros(acc.shape, F32)

        acc[...] += _dot_tn(a_ref[...].astype(BF16), b_ref[...].astype(BF16))

        @pl.when(k == nk - 1)
        def _():
            o_ref[...] = acc[...].astype(BF16)

    return pl.pallas_call(
        body, grid=(M // tm, N // tn, nk), name=name,
        in_specs=[pl.BlockSpec((tk, tm), lambda i, j, k: (k, i)), pl.BlockSpec((tk, tn), lambda i, j, k: (k, j))],
        out_specs=pl.BlockSpec((tm, tn), lambda i, j, k: (i, j)),
        out_shape=jax.ShapeDtypeStruct((M, N), BF16),
        scratch_shapes=[pltpu.VMEM((tm, tn), F32)],
        compiler_params=_cparams(),
    )(a, b)


def mixers_bwd(dx1, a, z, gsgu, wcat, wcat_t, bz, wp, pscale, goa, gos, gop, wout):
    T = a.shape[0]
    tm = min(ROW_TILE, T)

    def body(dx1_ref, a_ref, u_ref, vs_ref, pin_ref, halo_ref, gsgu_ref, wcat_ref, wcatt_ref, bz_ref, wp_ref,
             ps_ref, goa_ref, gos_ref, gop_ref, wout_ref,
             da_ref, delta_ref, du_ref, dvs_ref, dm_ref, mix_ref,
             dgoa_ref, dgos_ref, dgop_ref, dps_ref, dwp_ref, dwsp_ref, db_ref, dgsgu_ref):
        i = pl.program_id(0)

        @pl.when(i == 0)
        def _():
            for r in (dgoa_ref, dgos_ref, dgop_ref, dps_ref, dwp_ref, dwsp_ref, db_ref, dgsgu_ref):
                r[...] = jnp.zeros(r.shape, F32)

        a_v, u, vs = a_ref[...], u_ref[...], vs_ref[...]
        goa, gos, gop, pscale_v = goa_ref[...], gos_ref[...], gop_ref[...], ps_ref[...]
        vn, rv, zc, gm, m, yp_pre, yp, ra, rg, rp, mix = _mixers_fwd_tile(
            i, tm, a_v, u, vs, pin_ref[...], halo_ref[...], gsgu_ref[...], wcat_ref[...], bz_ref[...], wp_ref[...],
            pscale_v, goa, gos, gop)
        mix_ref[...] = mix
        dmix = _dot_nt(dx1_ref[...].astype(BF16), wout_ref[...])
        da, dgoa = _rms_bwd(a_v, ra, goa, dmix[:, :MLA_WIDTH], MLA_WIDTH)
        dgm, dgos = _rms_bwd(gm, rg, gos, dmix[:, MLA_WIDTH:MLA_WIDTH + SGU_WIDTH], SGU_WIDTH)
        dyp, dgop = _rms_bwd(yp, rp, gop, dmix[:, MLA_WIDTH + SGU_WIDTH:], POOL_WIDTH)
        da_ref[...] = da
        dgoa_ref[...] += dgoa
        dgos_ref[...] += dgos
        dgop_ref[...] += dgop
        prod = da * a_v
        ones = jnp.ones((8, HEAD_PAD), F32)
        for h in range(HEADS):
            lo = HEAD_PAD * h
            sums = lax.dot_general(ones, prod[:, lo:lo + HEAD_PAD], (((1,), (1,)), ((), ())), preferred_element_type=F32,
                                   precision=lax.Precision.HIGHEST)
            delta_ref[h] = sums[0:1, :]
        dps_ref[...] += jnp.sum(dyp * yp_pre, axis=0, keepdims=True)
        dyp_pre = (dyp * pscale_v).astype(BF16)
        dwp_ref[...] += _dot_tn(m, dyp_pre)
        dm_ref[...] = _dot_nt(dyp_pre, wp_ref[...])
        du_ref[...] = dgm * zc
        dzc = dgm * u
        masks = _head_masks((CHUNK, SGU_WIDTH), SGU_HEAD_DIM)
        lane_b = lax.broadcasted_iota(jnp.int32, (CHUNK, HEAD_PAD), 1)
        dvn_parts = []
        dwsp = jnp.zeros(dwsp_ref.shape, F32)
        db = jnp.zeros(db_ref.shape, F32)
        for c in range(tm // CHUNK):
            dz_c = dzc[CHUNK * c:CHUNK * (c + 1)]
            dzstack = jnp.concatenate([jnp.where(mk, dz_c, 0.0).astype(BF16) for mk in masks], axis=0)
            dvn_parts.append(_dot(wcatt_ref[...], dzstack))
            dwsp = dwsp + _dot_nt(dzstack, vn[CHUNK * c:CHUNK * (c + 1)].astype(BF16))
            for h, mk in enumerate(masks):
                col = jnp.sum(jnp.where(mk, dz_c, 0.0), axis=1, keepdims=True)
                db = db + jnp.where(lane_b == h, col, 0.0)
        dwsp_ref[...] += dwsp
        db_ref[...] += db
        dvs, dgsgu = _rms_bwd(vs, rv, gsgu_ref[...], jnp.concatenate(dvn_parts, axis=0), SGU_WIDTH)
        dvs_ref[...] = dvs
        dgsgu_ref[...] += dgsgu

    row = lambda w: pl.BlockSpec((tm, w), lambda i: (i, 0))
    head = pl.BlockSpec((HEADS, 1, tm), lambda i: (0, 0, i))
    acc_shapes = [(1, MLA_WIDTH), (1, SGU_WIDTH), (1, POOL_WIDTH), (1, POOL_WIDTH), (POOL_WIDTH, POOL_WIDTH),
                  (HEADS * CHUNK, CHUNK), (CHUNK, HEAD_PAD), (1, SGU_WIDTH)]
    return pl.pallas_call(
        body, grid=(T // tm,), name="mixers_bwd",
        in_specs=[row(D_MODEL), row(MLA_WIDTH)] + _z_specs(tm) + [WHOLE] * 10,
        out_specs=[row(MLA_WIDTH), head, row(256), row(256), row(256), row(D_MODEL)] + [_acc_spec(s) for s in acc_shapes],
        out_shape=[jax.ShapeDtypeStruct((T, MLA_WIDTH), F32), jax.ShapeDtypeStruct((HEADS, 1, T), F32),
                   jax.ShapeDtypeStruct((T, 256), F32), jax.ShapeDtypeStruct((T, 256), F32),
                   jax.ShapeDtypeStruct((T, 256), F32), jax.ShapeDtypeStruct((T, D_MODEL), BF16)]
                  + [jax.ShapeDtypeStruct(s, F32) for s in acc_shapes],
        compiler_params=_cparams(),
    )(dx1, a, z, z, z, z, gsgu, wcat, wcat_t, bz, wp, pscale, goa, gos, gop, wout)


def pool_bwd(dm):
    T = dm.shape[0]
    tm = min(ROW_TILE, T)
    nt = T // tm

    def body(dm_ref, next_ref, dpin_ref):
        i = pl.program_id(0)
        cnt, lane = _pool_counts(i, tm)
        dmv = dm_ref[...]
        win = _by_group(lane[:POOL_HALO], 2.0, 4.0, 8.0, 16.0)
        nxt = jnp.where(i < nt - 1, next_ref[...] / win, 0.0)
        r1 = jnp.concatenate([dmv / cnt, nxt], axis=0)
        n = tm + POOL_HALO
        r2 = r1 + pltpu.roll(r1, n - 1, 0)
        r4 = r2 + pltpu.roll(r2, n - 2, 0)
        r8 = r4 + pltpu.roll(r4, n - 4, 0)
        r16 = r8 + pltpu.roll(r8, n - 8, 0)
        dpin_ref[...] = _by_group(lane, r2[:tm], r4[:tm], r8[:tm], r16[:tm]) - dmv

    return pl.pallas_call(
        body, grid=(nt,), name="pool_bwd",
        in_specs=[pl.BlockSpec((tm, 256), lambda i: (i, 0)),
                  pl.BlockSpec((POOL_HALO, 256), lambda i: (jnp.minimum((i + 1) * (tm // POOL_HALO), T // POOL_HALO - 1), 0))],
        out_specs=pl.BlockSpec((tm, 256), lambda i: (i, 0)),
        out_shape=jax.ShapeDtypeStruct((T, 256), F32),
        compiler_params=_cparams(),
    )(dm, dm)


def attn_bwd(q, k, kt, v, do, lse, delta):
    _, T, _ = q.shape
    tb = min(ATTN_TILE, T)
    sb = min(ATTN_SUB, tb)
    ns = tb // sb
    nb = T // tb
    hb = ATTN_BWD_HEADS

    pairs = [(i, j) for i in range(nb) for j in range(i, nb)]

    def body(i_tab, j_tab, q_ref, k_ref, kt_ref, v_ref, do_ref, lse_ref, delta_ref, dqt_ref, dk_ref, dv_ref, dk_s, dv_s):
        i, j = i_tab[pl.program_id(1)], j_tab[pl.program_id(1)]

        @pl.when(pl.program_id(1) == 0)
        def _():
            dqt_ref[...] = jnp.zeros(dqt_ref.shape, F32)

        @pl.when(j == i)
        def _():
            dk_s[...] = jnp.zeros(dk_s.shape, F32)
            dv_s[...] = jnp.zeros(dv_s.shape, F32)

        def sub_block(h, a, b, masked):
            qa = slice(sb * a, sb * (a + 1))
            kb = slice(sb * b, sb * (b + 1))
            qv = q_ref[h, qa, :]
            dov = do_ref[qa, HEAD_PAD * h:HEAD_PAD * (h + 1)].astype(BF16)
            st = _dot_nt(k_ref[h, kb, :], qv)
            dpt = _dot_nt(v_ref[h, kb, :], dov)
            pt = jnp.exp2(st - lse_ref[h, :, qa])
            if masked:
                krow = lax.broadcasted_iota(jnp.int32, st.shape, 0)
                qcol = lax.broadcasted_iota(jnp.int32, st.shape, 1)
                pt = jnp.where(krow <= qcol, pt, 0.0)
            dst = (pt * (dpt - delta_ref[h, :, qa])).astype(BF16)
            dv_s[h, kb, :] += _dot(pt.astype(BF16), dov)
            dk_s[h, kb, :] += _dot(dst, qv)
            cols = pl.ds(pl.multiple_of(j * tb + sb * a, sb), sb)
            dqt_ref[h, :, cols] += _dot(kt_ref[h, :, kb], dst)

        @pl.when(j > i)
        def _():
            for a in range(ns):
                for h in range(hb):
                    for b in range(ns):
                        sub_block(h, a, b, False)

        @pl.when(j == i)
        def _():
            for a in range(ns):
                for h in range(hb):
                    for b in range(a + 1):
                        sub_block(h, a, b, a == b)

        @pl.when(j == nb - 1)
        def _():
            dk_ref[...] = dk_s[...] * (1.0 / LOG2E)
            dv_ref[...] = dv_s[...]

    qspec = pl.BlockSpec((hb, tb, HEAD_PAD), lambda g, p, it, jt: (g, jt[p], 0))
    kspec = pl.BlockSpec((hb, tb, HEAD_PAD), lambda g, p, it, jt: (g, it[p], 0))
    ktspec = pl.BlockSpec((hb, HEAD_PAD, tb), lambda g, p, it, jt: (g, 0, it[p]))
    rowspec = pl.BlockSpec((hb, 1, tb), lambda g, p, it, jt: (g, 0, jt[p]))
    grid_spec = pltpu.PrefetchScalarGridSpec(
        num_scalar_prefetch=2, grid=(HEADS // hb, len(pairs)),
        in_specs=[qspec, kspec, ktspec, kspec, pl.BlockSpec((tb, hb * HEAD_PAD), lambda g, p, it, jt: (jt[p], g)), rowspec, rowspec],
        out_specs=[pl.BlockSpec((hb, HEAD_PAD, T), lambda g, p, it, jt: (g, 0, 0)), kspec, kspec],
        scratch_shapes=[pltpu.VMEM((hb, tb, HEAD_PAD), F32), pltpu.VMEM((hb, tb, HEAD_PAD), F32)])
    return pl.pallas_call(
        body, grid_spec=grid_spec, name="attn_bwd",
        out_shape=[jax.ShapeDtypeStruct((HEADS, HEAD_PAD, T), F32)] + [jax.ShapeDtypeStruct((HEADS, T, HEAD_PAD), F32)] * 2,
        compiler_params=_cparams(),
    )(*_pair_tables(pairs), q, k, kt, v, do, lse, delta)


def mixin_bwd(dres, x, z, rope, dqt, dk, dv, du, dvs, dpin, gmix, win, gql, wq, gkvl, wkv, gqh, gkh):
    T = x.shape[0]
    tm = min(ROW_TILE, T)
    part = min(MIXIN_BWD_PART, tm)

    def body(dres_ref, x_ref, z_ref, c_ref, sa_ref, sb_ref, dqt_ref, dk_ref, dv_ref, du_ref, dvs_ref, dpin_ref,
             gmix_ref, win_ref, gql_ref, wq_ref, gkvl_ref, wkv_ref, gqh_ref, gkh_ref,
             dx_ref, hn_ref, dz_ref, qn_ref, dqup_ref, kvn_ref, dkvup_ref,
             dgmix_ref, dgql_ref, dgkvl_ref, dgqh_ref, dgkh_ref):
        @pl.when(pl.program_id(0) == 0)
        def _():
            for r in (dgmix_ref, dgql_ref, dgkvl_ref, dgqh_ref, dgkh_ref):
                r[...] = jnp.zeros(r.shape, F32)

        lane = lax.broadcasted_iota(jnp.int32, (part, HEAD_PAD), 1)
        rope_lanes = (lane >= QK_NOPE) & (lane < QK_HEAD)

        def recompute(g):
            rows = g["rows"]
            g["xv"] = x_ref[rows, :]
            hn, g["rx"] = _rms(g["xv"], gmix_ref[...], D_MODEL)
            hn_ref[rows, :] = hn.astype(BF16)
            g["ql"], g["kvl"], g["kr"], qn, g["rq"], kvn, g["rkv"], g["q_up"], g["kv_up"] = _qkv_pre(
                z_ref[rows, :], gql_ref[...], wq_ref[...], gkvl_ref[...], wkv_ref[...])
            qn_ref[rows, :] = qn
            kvn_ref[rows, :] = kvn

        def heads(g):
            rows = g["rows"]
            c, sa, sb = c_ref[rows, :], sa_ref[rows, :], sb_ref[rows, :]
            dkr = jnp.zeros((part, HEAD_PAD), F32)
            dgqh = jnp.zeros((1, HEAD_PAD), F32)
            dgkh = jnp.zeros((1, HEAD_PAD), F32)
            dq_parts, dk_parts, dv_parts = [], [], []
            for h in range(HEADS):
                lo = HEAD_PAD * h
                qh = g["q_up"][:, lo:lo + HEAD_PAD]
                rqh = lax.rsqrt(jnp.sum(qh * qh, axis=-1, keepdims=True) * (1.0 / QK_HEAD) + EPS)
                dq_h = jnp.transpose(dqt_ref[h, :, rows]) * ATTN_SCALE
                dqh, dg = _rms_bwd(qh, rqh, gqh_ref[...], _rope_bwd(dq_h, c, sa, sb), QK_HEAD)
                dgqh = dgqh + dg
                dq_parts.append(dqh)
                kh = g["kv_up"][:, lo:lo + HEAD_PAD] + g["kr"]
                rkh = lax.rsqrt(jnp.sum(kh * kh, axis=-1, keepdims=True) * (1.0 / QK_HEAD) + EPS)
                dkh, dg = _rms_bwd(kh, rkh, gkh_ref[...], _rope_bwd(dk_ref[h, rows, :], c, sa, sb), QK_HEAD)
                dgkh = dgkh + dg
                dkr = dkr + jnp.where(rope_lanes, dkh, 0.0)
                dk_parts.append(dkh)
                dv_parts.append(dv_ref[h, rows, :])
            dgqh_ref[...] += dgqh
            dgkh_ref[...] += dgkh
            g["dkr"] = dkr
            g["dq_up"] = jnp.concatenate(dq_parts, axis=1).astype(BF16)
            g["dkv_up"] = jnp.concatenate(dk_parts + dv_parts, axis=1).astype(BF16)
            dqup_ref[rows, :] = g["dq_up"]
            dkvup_ref[rows, :] = g["dkv_up"]

        def latents(g):
            rows = g["rows"]
            dql, dg = _rms_bwd(g["ql"], g["rq"], gql_ref[...], _dot(g["dq_up"], wq_ref[...]), Q_LORA)
            dgql_ref[...] += dg
            dkvl, dg = _rms_bwd(g["kvl"], g["rkv"], gkvl_ref[...], _dot(g["dkv_up"], wkv_ref[...]), KV_LORA)
            dgkvl_ref[...] += dg
            g["dz"] = jnp.concatenate([dql, dkvl, g["dkr"], du_ref[rows, :], dvs_ref[rows, :], dpin_ref[rows, :]],
                                      axis=1).astype(BF16)
            dz_ref[rows, :] = g["dz"]

        def inputs(g):
            rows = g["rows"]
            dxn, dg = _rms_bwd(g["xv"], g["rx"], gmix_ref[...], _dot(g["dz"], win_ref[...]), D_MODEL)
            dgmix_ref[...] += dg
            dx_ref[rows, :] = dres_ref[rows, :] + dxn

        _skewed((recompute, heads, latents, inputs), [dict(rows=pl.ds(part * p, part)) for p in range(tm // part)])

    row = lambda w: pl.BlockSpec((tm, w), lambda i: (i, 0))
    head = pl.BlockSpec((HEADS, tm, HEAD_PAD), lambda i: (0, i, 0))
    head_t = pl.BlockSpec((HEADS, HEAD_PAD, tm), lambda i: (0, 0, i))
    acc_shapes = [(1, D_MODEL), (1, Q_LORA), (1, KV_LORA), (1, HEAD_PAD), (1, HEAD_PAD)]
    out_rows = [(D_MODEL, F32), (D_MODEL, BF16), (Z_WIDTH, BF16), (Q_LORA, BF16), (HEADS * HEAD_PAD, BF16),
                (KV_LORA, BF16), (2 * HEADS * HEAD_PAD, BF16)]
    return pl.pallas_call(
        body, grid=(T // tm,), name="mixin_bwd",
        in_specs=[row(D_MODEL), row(D_MODEL), row(Z_WIDTH)] + [row(HEAD_PAD)] * 3 + [head_t, head, head, row(256), row(256), row(256)]
                 + [WHOLE] * 8,
        out_specs=[row(w) for w, _ in out_rows] + [_acc_spec(s) for s in acc_shapes],
        out_shape=[jax.ShapeDtypeStruct((T, w), dt) for w, dt in out_rows] + [jax.ShapeDtypeStruct(s, F32) for s in acc_shapes],
        compiler_params=_cparams(),
    )(dres, x, z, *rope, dqt, dk, dv, du, dvs, dpin, gmix, win, gql, wq, gkvl, wkv, gqh, gkh)


def _place():
    x, y, c = lax.axis_index("x"), lax.axis_index("y"), lax.axis_index("c")
    return x, y, c, 4 * x + 2 * y + c


def _layer_of(ref, l):
    return ref[:, l, :] if ref.shape[1] == DEPTH else ref[l]


def _layer_shape(shard):
    return (shard.shape[0], shard.shape[2]) if shard.shape[1] == DEPTH else shard.shape[1:]


def cast_shards(shards, wanted):
    n = len(shards)

    def body(*refs):
        for o_ref, (w, l) in zip(refs[n:], wanted):
            o_ref[...] = _layer_of(refs[w], l).astype(BF16)

    return pl.pallas_call(
        body, name="cast_shards", in_specs=[WHOLE] * n, out_specs=[WHOLE] * len(wanted),
        out_shape=[jax.ShapeDtypeStruct(_layer_shape(shards[w]), BF16) for w, _ in wanted],
        compiler_params=_cparams(),
    )(*shards)


def _peer(k):
    x, y, c, _ = _place()
    px = 1 - x if k & 4 else x
    py = 1 - y if k & 2 else y
    pc = 1 - c if k & 1 else c
    return (px, py, pc), 4 * px + 2 * py + pc


def exchange_start(srcs, after, gather, name):
    n = len(srcs)
    land_shapes = [((N_DEV,) + s.shape) if gather else s.shape for s in srcs]

    def body(*refs):
        src_refs, land_refs = refs[:n], refs[n:2 * n]
        send_sems, recv_sems = refs[2 * n + 1:3 * n + 1], refs[3 * n + 1:4 * n + 1]
        token = refs[-1]
        _, _, _, me = _place()
        for k in range(1, N_DEV):
            peer, peer_id = _peer(k)
            for w in range(n):
                pltpu.make_async_remote_copy(
                    src_ref=src_refs[w] if gather else src_refs[w].at[peer_id], dst_ref=land_refs[w].at[me],
                    send_sem=send_sems[w], recv_sem=recv_sems[w], device_id=peer, device_id_type=MESH).start()
        token[...] = jnp.zeros(token.shape, F32)

    hbm = lambda a: pltpu.with_memory_space_constraint(a, pltpu.HBM)
    outs = pl.pallas_call(
        body, name=name,
        out_shape=(pltpu.SemaphoreType.DMA(()),) * (2 * n)
                  + tuple(pltpu.HBM(s.shape, BF16) for s in srcs) + tuple(pltpu.HBM(s, BF16) for s in land_shapes)
                  + (jax.ShapeDtypeStruct((8, 128), F32),),
        in_specs=[HBM_SPEC] * (2 * n) + [ANY],
        out_specs=(SEM_SPEC,) * (2 * n) + (HBM_SPEC,) * (2 * n) + (WHOLE,),
        input_output_aliases={i: 2 * n + i for i in range(2 * n)},
        compiler_params=pltpu.CompilerParams(has_side_effects=pltpu.SideEffectType.DATAFLOW_SIDE_EFFECTING),
    )(*[hbm(s) for s in srcs], *[hbm(lax.empty(s, BF16)) for s in land_shapes], after)
    return list(outs[:n]), list(outs[n:2 * n]), list(outs[2 * n:3 * n]), list(outs[3 * n:4 * n]), outs[-1]


def exchange_wait(started, after, name):
    send_sems, recv_sems, srcs, lands, _ = started
    n = len(srcs)

    def body(*refs):
        land_refs = refs[n:2 * n]
        send_sems, recv_sems = refs[2 * n:3 * n], refs[3 * n:4 * n]
        x, y, c, _ = _place()
        for w in range(n):
            seven = land_refs[w].at[pl.ds(0, N_DEV - 1)]
            cp = pltpu.make_async_remote_copy(src_ref=seven, dst_ref=seven, send_sem=send_sems[w], recv_sem=recv_sems[w],
                                              device_id=(x, y, c), device_id_type=MESH)
            cp.wait_send()
            cp.wait_recv()

    outs = pl.pallas_call(
        body, name=name,
        out_shape=tuple(pltpu.HBM(s.shape, BF16) for s in srcs) + tuple(pltpu.HBM(l.shape, BF16) for l in lands),
        in_specs=[HBM_SPEC] * (2 * n) + [SEM_SPEC] * (2 * n) + [ANY],
        out_specs=(HBM_SPEC,) * (2 * n),
        input_output_aliases={i: i for i in range(2 * n)},
        compiler_params=pltpu.CompilerParams(has_side_effects=pltpu.SideEffectType.DATAFLOW_SIDE_EFFECTING),
    )(*srcs, *lands, *send_sems, *recv_sems, after)
    _, _, _, me = _place()
    filled = []
    for src, land in zip(outs[:n], outs[n:]):
        own = src[None] if src.ndim == 2 else lax.dynamic_slice_in_dim(src, me, 1, axis=0)
        filled.append(lax.dynamic_update_slice_in_dim(land, own, me, axis=0))
    return filled


def _adamw(w, g, m, v):
    m2 = ADAM_B1 * m + (1.0 - ADAM_B1) * g
    v2 = ADAM_B2 * v + (1.0 - ADAM_B2) * (g * g)
    m_hat = m2 / (1.0 - ADAM_B1 ** ADAM_STEP)
    v_hat = v2 / (1.0 - ADAM_B2 ** ADAM_STEP)
    delta = -ADAM_LR * (m_hat / (jnp.sqrt(v_hat) + ADAM_EPS) + ADAM_WD * w)
    return delta, m2, v2


def adamw_sharded(parts, w, m, v, name):
    L, R, C = w.shape
    fits = [d for d in range(16, min(R, 512) + 1, 16) if R % d == 0]
    br = max(fits) if fits else R
    nblk = R // br

    def body(*refs):
        p_refs = refs[:L]
        w_ref, m_ref, v_ref, g_ref, d_ref, m2_ref, v2_ref = refs[L:]

        def total(p_ref):
            g = p_ref[0].astype(F32)
            for s in range(1, N_DEV):
                g = g + p_ref[s].astype(F32)
            return g

        g = total(p_refs[0])
        for l in range(1, L):
            g = jnp.where(pl.program_id(0) == l, total(p_refs[l]), g)
        g_ref[...] = g
        d_ref[...], m2_ref[...], v2_ref[...] = _adamw(w_ref[...], g, m_ref[...], v_ref[...])

    blk = pl.BlockSpec((None, br, C), lambda l, i: (l, i, 0))

    def part_spec(k):
        return pl.BlockSpec((N_DEV, br, C), lambda l, i: (0, jnp.where(l == k, i, jnp.where(l < k, 0, nblk - 1)), 0))

    return pl.pallas_call(
        body, grid=(L, nblk), name=name,
        in_specs=[part_spec(k) for k in range(L)] + [blk, blk, blk],
        out_specs=[blk] * 4,
        out_shape=[jax.ShapeDtypeStruct((L, R, C), F32)] * 4,
        compiler_params=_cparams(),
    )(*parts, w, m, v)


def allreduce_adamw_small(gpart, w, m, v):
    R = gpart.shape[0]

    def body(g_ref, w_ref, m_ref, v_ref, grad_ref, d_ref, m2_ref, v2_ref, all_ref, send_sems, recv_sems):
        x, y, c, me = _place()
        sibling = (x, y, 1 - c)
        chips = [(1 - x, y), (x, 1 - y), (1 - x, 1 - y)]

        def copy(k, block_id, to, from_input):
            return pltpu.make_async_remote_copy(
                src_ref=g_ref if from_input else all_ref.at[block_id], dst_ref=all_ref.at[block_id],
                send_sem=send_sems.at[k], recv_sem=recv_sems.at[k], device_id=to, device_id_type=MESH)

        def block_of(cx, cy, cc):
            return 4 * cx + 2 * cy + cc

        all_ref[me] = g_ref[...]
        first = [copy(0, me, sibling, True)] + [copy(1 + j, me, (*chip, c), True) for j, chip in enumerate(chips)]
        for cp in first:
            cp.start()
        passed = [copy(4 + j, block_of(*chip, c), sibling, False) for j, chip in enumerate(chips)]
        for j, chip in enumerate(chips):
            copy(1 + j, block_of(*chip, c), (x, y, c), False).wait_recv()
            passed[j].start()
        copy(0, block_of(x, y, 1 - c), (x, y, c), False).wait_recv()
        for j, chip in enumerate(chips):
            copy(4 + j, block_of(*chip, 1 - c), (x, y, c), False).wait_recv()
        for cp in first + passed:
            cp.wait_send()
        g = all_ref[0]
        for s in range(1, N_DEV):
            g = g + all_ref[s]
        grad_ref[...] = g
        d_ref[...], m2_ref[...], v2_ref[...] = _adamw(w_ref[...], g, m_ref[...], v_ref[...])

    return pl.pallas_call(
        body, name="allreduce_adamw_small",
        in_specs=[WHOLE] * 4, out_specs=[WHOLE] * 4,
        out_shape=[jax.ShapeDtypeStruct((R, 128), F32)] * 4,
        scratch_shapes=[pltpu.VMEM((N_DEV, R, 128), F32), pltpu.SemaphoreType.DMA((7,)), pltpu.SemaphoreType.DMA((7,))],
        compiler_params=_cparams(),
    )(gpart, w, m, v)


def _shard_view(name, a):
    if name == "w_in":
        return a.transpose(2, 0, 1)
    return a.swapaxes(1, 2) if name in TRANSPOSED else a


def _shard_unview(name, a):
    if name == "w_in":
        return a.transpose(1, 2, 0)
    return a.swapaxes(1, 2) if name in TRANSPOSED else a


def _pad_head_rows(w, width):
    c = w.shape[1]
    return jnp.pad(w.reshape(HEADS, width, c), ((0, 0), (0, HEAD_PAD - width), (0, 0))).reshape(HEADS * HEAD_PAD, c)


def _unpad_head_rows(w, width):
    c = w.shape[1]
    return w.reshape(HEADS, HEAD_PAD, c)[:, :width].reshape(HEADS * width, c)


O1 = Q_LORA
O2 = O1 + KV_LORA
O3 = O2 + QK_ROPE


def _mixer_weights(gw):
    w_in = gw["w_in"].reshape(IN_WIDTH, D_MODEL)
    zero = lambda n: jnp.zeros((n, D_MODEL), BF16)
    win = jnp.concatenate([w_in[:O2], zero(QK_NOPE), w_in[O2:O3], zero(HEAD_PAD - QK_HEAD), w_in[O3:]], axis=0)
    wq = _pad_head_rows(gw["w_q_up"].reshape(HEADS * QK_HEAD, Q_LORA), QK_HEAD)
    w_kv = gw["w_kv_up"].reshape(HEADS, QK_NOPE + V_HEAD, KV_LORA)
    wk = jnp.pad(w_kv[:, :QK_NOPE], ((0, 0), (0, HEAD_PAD - QK_NOPE), (0, 0))).reshape(HEADS * HEAD_PAD, KV_LORA)
    wv = w_kv[:, QK_NOPE:].reshape(HEADS * V_HEAD, KV_LORA)
    wkv = jnp.concatenate([wk, wv], axis=0)
    out = dict(win=win, wq=wq, wkv=wkv)
    if "w_out" in gw:
        out["wout"] = gw["w_out"].reshape(D_MODEL, D_MODEL)
    return out


def _ffn_weights(gw):
    return dict(wg=gw["w_gate"].reshape(FFN_HIDDEN, D_MODEL), wu=gw["w_up"].reshape(FFN_HIDDEN, D_MODEL),
                wd=gw["w_down"].reshape(FFN_HIDDEN, D_MODEL))


def _layer_small(p, l):
    row = lambda a: a.reshape(1, -1)
    pad_head = lambda g: jnp.pad(g, (0, HEAD_PAD - QK_HEAD)).reshape(1, HEAD_PAD)
    tril = jnp.tril(jnp.ones((CHUNK, CHUNK), F32))
    wsp = p["w_spatial"][l] * tril
    wcat = jnp.concatenate([wsp[h] for h in range(HEADS)], axis=1).astype(BF16)
    wcat_t = jnp.concatenate([wsp[h].T for h in range(HEADS)], axis=1).astype(BF16)
    bz = jnp.repeat(p["b_spatial"][l].T, SGU_HEAD_DIM, axis=1)
    wp = jax.scipy.linalg.block_diag(*[p["w_pool"][l][g] for g in range(HEADS)]).astype(BF16)
    return dict(gmix=row(p["g_mix_norm"][l]), gql=row(p["g_q_lat"][l]), gkvl=row(p["g_kv_lat"][l]),
                gqh=pad_head(p["g_q_head"][l]), gkh=pad_head(p["g_k_head"][l]), gsgu=row(p["g_sgu_v"][l]),
                wcat=wcat, wcat_t=wcat_t, bz=bz, wp=wp, pscale=row(p["pool_scale"][l]),
                goa=row(p["g_out_mla"][l]), gos=row(p["g_out_sgu"][l]), gop=row(p["g_out_pool"][l]),
                gffn=row(p["g_ffn_norm"][l]))


MIXER_SIDE = ("w_in", "w_q_up", "w_kv_up", "w_out")
FFN_SIDE = ("w_gate", "w_up", "w_down")
TRANSPOSED = ("w_in", "w_q_up", "w_kv_up", "w_gate", "w_up")
FIRST_0 = ("w_in", "w_q_up", "w_kv_up")
LATER_0 = ("w_out",) + FFN_SIDE
SHARDED = MIXER_SIDE + FFN_SIDE
SMALL = ("g_mix_norm", "g_q_lat", "g_kv_lat", "g_q_head", "g_k_head", "g_sgu_v", "w_spatial", "b_spatial", "w_pool",
         "pool_scale", "g_out_mla", "g_out_sgu", "g_out_pool", "g_ffn_norm")
WEIGHTS = ("g_mix_norm", "w_in", "g_q_lat", "w_q_up", "g_kv_lat", "w_kv_up", "g_q_head", "g_k_head", "g_sgu_v", "w_spatial",
           "b_spatial", "w_pool", "pool_scale", "g_out_mla", "g_out_sgu", "g_out_pool", "w_out", "g_ffn_norm", "w_gate",
           "w_up", "w_down")
PACKED = SMALL + ("loss",)
PACK_ROWS = 8 * 128


def _pack_small(parts):
    flat = []
    for name in PACKED:
        a = parts[name].reshape(-1)
        flat.append(jnp.pad(a, (0, -a.shape[0] % PACK_ROWS)))
    return jnp.concatenate(flat).reshape(-1, 128)


def _unpack_small(packed, like):
    out, row = {}, 0
    for name in PACKED:
        n = math.prod(like[name].shape)
        rows = -(-n // PACK_ROWS) * 8
        out[name] = packed[row:row + rows].reshape(-1)[:n].reshape(like[name].shape)
        row += rows
    return out


def _forward_attention(x, rope, W, S):
    z, q, k, v, kt, vt = mixin_fwd(x, rope, S["gmix"], W["win"], S["gql"], W["wq"], S["gkvl"], W["wkv"], S["gqh"], S["gkh"])
    a, lse = attn_fwd(q, k, vt)
    return dict(x=x, z=z, q=q, k=k, kt=kt, v=v, a=a, lse=lse)


def _forward_mixers(A, wout, S):
    A["x1"] = mixers_fwd(A["x"], A["a"], A["z"], S["gsgu"], S["wcat"], S["bz"], S["wp"], S["pscale"], S["goa"], S["gos"],
                         S["gop"], wout)
    return A["x1"]


def _backward_ffn(dx2, W, S, A, l):
    dx1, h2, act, dg, du_ffn, dgffn = ffn_bwd(dx2, A["x1"], A["gs"], A["us"], S["gffn"], W["wg"], W["wu"], W["wd"])
    d_wd = matmul_tn(act, dx2, f"dw_down_{l}")
    d_wg = matmul_tn(dg, h2, f"dw_gate_{l}")
    d_wu = matmul_tn(du_ffn, h2, f"dw_up_{l}")
    big = {n: d.reshape(N_DEV, -1, D_MODEL) for n, d in (("w_gate", d_wg), ("w_up", d_wu), ("w_down", d_wd))}
    return dx1, big, dgffn


def _backward_mixers(dx1, dgffn, rope, W, S, A, l, send_wout=None):
    (da, delta, du, dvs, dm, mix, dgoa, dgos, dgop, dps, dwp, dwsp, db, dgsgu) = mixers_bwd(
        dx1, A["a"], A["z"], S["gsgu"], S["wcat"], S["wcat_t"], S["bz"], S["wp"], S["pscale"], S["goa"], S["gos"],
        S["gop"], W["wout"])
    d_wout = matmul_tn(mix, dx1, f"dw_out_{l}")
    if send_wout is not None:
        delta = delta + send_wout(d_wout.reshape(N_DEV, -1, D_MODEL))[0, 0]
    dpin = pool_bwd(dm)
    dqt, dk, dv = attn_bwd(A["q"], A["k"], A["kt"], A["v"], da, A["lse"], delta)
    (dx, hn, dz, qn, dq_up, kvn, dkv_up, dgmix, dgql, dgkvl, dgqh, dgkh) = mixin_bwd(
        dx1, A["x"], A["z"], rope, dqt, dk, dv, du, dvs, dpin, S["gmix"], W["win"], S["gql"], W["wq"], S["gkvl"],
        W["wkv"], S["gqh"], S["gkh"])
    d_win = matmul_tn(dz, hn, f"dw_in_{l}")
    d_wq = matmul_tn(dq_up, qn, f"dw_q_up_{l}")
    d_wkv = matmul_tn(dkv_up, kvn, f"dw_kv_up_{l}")
    d_win = jnp.concatenate([d_win[:O2], d_win[O2 + QK_NOPE:O2 + QK_HEAD], d_win[O2 + HEAD_PAD:]], axis=0)
    d_wk = d_wkv[:HEADS * HEAD_PAD].reshape(HEADS, HEAD_PAD, KV_LORA)[:, :QK_NOPE]
    d_wv = d_wkv[HEADS * HEAD_PAD:].reshape(HEADS, V_HEAD, KV_LORA)
    d_wkv = jnp.concatenate([d_wk, d_wv], axis=1)
    big = dict(w_in=d_win.reshape(N_DEV, -1, D_MODEL), w_q_up=_unpad_head_rows(d_wq, QK_HEAD).reshape(N_DEV, -1, Q_LORA),
               w_kv_up=d_wkv.reshape(N_DEV, -1, KV_LORA), w_out=d_wout.reshape(N_DEV, -1, D_MODEL))
    tril = jnp.tril(jnp.ones((CHUNK, CHUNK), F32))
    small = dict(g_mix_norm=dgmix[0], g_q_lat=dgql[0], g_kv_lat=dgkvl[0], g_q_head=dgqh[0, :QK_HEAD], g_k_head=dgkh[0, :QK_HEAD],
                 g_sgu_v=dgsgu[0], w_spatial=dwsp.reshape(HEADS, CHUNK, CHUNK) * tril, b_spatial=db[:, :HEADS].T,
                 w_pool=jnp.stack([dwp[64 * g:64 * (g + 1), 64 * g:64 * (g + 1)] for g in range(HEADS)]),
                 pool_scale=dps[0], g_out_mla=dgoa[0], g_out_sgu=dgos[0], g_out_pool=dgop[0], g_ffn_norm=dgffn[0])
    return dx, big, small


def kernel(x, positions, g_mix_norm, w_in, g_q_lat, w_q_up, g_kv_lat, w_kv_up, g_q_head, g_k_head, g_sgu_v, w_spatial, b_spatial, w_pool, pool_scale, g_out_mla, g_out_sgu, g_out_pool, w_out, g_ffn_norm, w_gate, w_up, w_down, loss_target, m_g_mix_norm, m_w_in, m_g_q_lat, m_w_q_up, m_g_kv_lat, m_w_kv_up, m_g_q_head, m_g_k_head, m_g_sgu_v, m_w_spatial, m_b_spatial, m_w_pool, m_pool_scale, m_g_out_mla, m_g_out_sgu, m_g_out_pool, m_w_out, m_g_ffn_norm, m_w_gate, m_w_up, m_w_down, v_g_mix_norm, v_w_in, v_g_q_lat, v_w_q_up, v_g_kv_lat, v_w_kv_up, v_g_q_head, v_g_k_head, v_g_sgu_v, v_w_spatial, v_b_spatial, v_w_pool, v_pool_scale, v_g_out_mla, v_g_out_sgu, v_g_out_pool, v_w_out, v_g_ffn_norm, v_w_gate, v_w_up, v_w_down):
    given = dict(locals())
    w = {n: given[n] for n in WEIGHTS}
    m = {n: given["m_" + n] for n in WEIGHTS}
    v = {n: given["v_" + n] for n in WEIGHTS}
    T = x.shape[1]
    xs = x.reshape(T, D_MODEL)

    half = QK_ROPE // 2
    inv_freq = 1.0 / (ROPE_THETA ** (jnp.arange(half, dtype=F32) / half))
    ang16 = positions.reshape(T).astype(F32)[:, None] * inv_freq
    ang = jnp.concatenate([jnp.zeros((T, QK_NOPE), F32), ang16, ang16, jnp.zeros((T, HEAD_PAD - QK_HEAD), F32)], axis=1)

    wv = {n: _shard_view(n, w[n]) for n in SHARDED}
    wanted = [(SHARDED.index(n), 0) for n in FIRST_0 + LATER_0] + [(i, 1) for i in range(len(SHARDED))]
    bf = cast_shards([wv[n] for n in SHARDED], wanted)
    nf, nl = len(FIRST_0), len(LATER_0)
    ag_first0 = exchange_start(bf[:nf], ang, True, "ag_start_first0")
    rope = rope_tables(ang + ag_first0[-1][0, 0])
    W0 = dict(zip(FIRST_0, exchange_wait(ag_first0, rope[0], "ag_wait_first0")))
    ag_later0 = exchange_start(bf[nf:nf + nl], W0["w_in"], True, "ag_start_later0")
    ag_l1 = exchange_start(bf[nf + nl:], ag_later0[-1], True, "ag_start_l1")
    Ss = [_layer_small(w, l) for l in range(DEPTH)]

    S0 = dict(Ss[0], gmix=Ss[0]["gmix"] + (ag_later0[-1][0, 0] + ag_l1[-1][0, 0]))
    A0 = _forward_attention(xs, rope, _mixer_weights(W0), S0)
    W0.update(zip(LATER_0, exchange_wait(ag_later0, A0["a"], "ag_wait_later0")))
    W0 = {**_mixer_weights(W0), **_ffn_weights(W0)}
    x1 = _forward_mixers(A0, W0["wout"], Ss[0])
    h, A0["gs"], A0["us"] = ffn_fwd(x1, Ss[0]["gffn"], W0["wg"], W0["wu"], W0["wd"])
    layer1 = dict(zip(SHARDED, exchange_wait(ag_l1, h, "ag_wait_l1")))
    W1 = {**_mixer_weights(layer1), **_ffn_weights(layer1)}
    A1 = _forward_attention(h, rope, W1, Ss[1])
    x1 = _forward_mixers(A1, W1["wout"], Ss[1])
    dh, loss_part, A1["gs"], A1["us"] = ffn_fwd(x1, Ss[1]["gffn"], W1["wg"], W1["wu"], W1["wd"],
                                                target=loss_target.reshape(T, D_MODEL))

    dx1, big_f1, dgffn1 = _backward_ffn(dh, W1, Ss[1], A1, 1)
    dh, big_m1, small1 = _backward_mixers(dx1, dgffn1, rope, W1, Ss[1], A1, 1)
    rs_l1 = exchange_start([{**big_m1, **big_f1}[n] for n in SHARDED], dh, False, "rs_start_l1")
    S0 = dict(Ss[0], gffn=Ss[0]["gffn"] + rs_l1[-1][0, 0])
    dx1, big_f0, dgffn0 = _backward_ffn(dh, W0, S0, A0, 0)
    rs_ffn0 = exchange_start([big_f0[n] for n in FFN_SIDE], dx1, False, "rs_start_ffn0")
    S0 = dict(Ss[0], gsgu=Ss[0]["gsgu"] + rs_ffn0[-1][0, 0])
    rs_out0 = []

    def send_wout(pieces):
        rs_out0.append(exchange_start([pieces], dx1, False, "rs_start_out0"))
        return rs_out0[0][-1]

    dh, big_m0, small0 = _backward_mixers(dx1, dgffn0, rope, W0, S0, A0, 0, send_wout)
    grad_x = dh.reshape(x.shape)
    smalls = [small0, small1]

    rs_first0 = exchange_start([big_m0[n] for n in FIRST_0], dh, False, "rs_start_first0")
    parts1 = dict(zip(SHARDED, exchange_wait(rs_l1, rs_first0[-1], "rs_wait_l1")))
    parts0 = dict(zip(FFN_SIDE, exchange_wait(rs_ffn0, rs_first0[-1], "rs_wait_ffn0")))
    parts0["w_out"] = exchange_wait(rs_out0[0], rs_first0[-1], "rs_wait_out0")[0]
    grad, delta, new_m, new_v = {}, {}, {}, {}

    small_part = {n: jnp.stack([smalls[l][n] for l in range(DEPTH)]) for n in SMALL}
    small_part["loss"] = loss_part
    no_loss = {"loss": jnp.zeros((1, 1), F32)}
    outs = allreduce_adamw_small(_pack_small(small_part), *[_pack_small({**a, **no_loss}) for a in (w, m, v)])
    for d, o in zip((grad, delta, new_m, new_v), outs):
        d.update(_unpack_small(o, {**w, **no_loss}))
    loss = grad["loss"][0, 0]

    for n in LATER_0 + FIRST_0:
        if n == FIRST_0[0]:
            parts0.update(zip(FIRST_0, exchange_wait(rs_first0, grad["w_down"], "rs_wait_first0")))
        parts, views = [parts0[n], parts1[n]], [_shard_view(n, a[n]) for a in (w, m, v)]
        if n == "w_in":
            parts = [jnp.stack(parts, axis=2).reshape(N_DEV, -1, D_MODEL)]
            views = [a.reshape(1, -1, D_MODEL) for a in views]
        outs = adamw_sharded(parts, *views, f"adamw_{n}")
        if n == "w_in":
            outs = [o.reshape(-1, DEPTH, D_MODEL) for o in outs]
        grad[n], delta[n], new_m[n], new_v[n] = [_shard_unview(n, o) for o in outs]

    return (loss, grad_x, *[grad[n] for n in WEIGHTS], *[delta[n] for n in WEIGHTS], *[new_m[n] for n in WEIGHTS],
            *[new_v[n] for n in WEIGHTS])
```

```python
import math

import jax
import jax.numpy as jnp
from jax import lax
from jax.experimental import pallas as pl
from jax.experimental.pallas import tpu as pltpu

F32 = jnp.float32
BF16 = jnp.bfloat16

N_DEV = 8
DEPTH = 2
D_MODEL = 1024
HEADS = 4
HEAD_PAD = 128
QK_NOPE = 64
QK_ROPE = 32
QK_HEAD = QK_NOPE + QK_ROPE
V_HEAD = 128
Q_LORA = 256
KV_LORA = 128
SGU_WIDTH = 256
SGU_HEAD_DIM = 64
CHUNK = 128
POOL_WIDTH = 256
POOL_HALO = 16
MLA_WIDTH = 512
IN_WIDTH = 1184
Z_WIDTH = 1280
FFN_HIDDEN = 2816
FFN_CHUNK = 256
ROPE_THETA = 10000.0
EPS = 1e-6
ATTN_SCALE = 1.0 / math.sqrt(QK_HEAD)
LOG2E = 1.4426950408889634
NEG_BIG = -1e30

ADAM_LR = 0.001
ADAM_B1 = 0.9
ADAM_B2 = 0.999
ADAM_EPS = 1e-08
ADAM_WD = 0.01
ADAM_STEP = 10

VMEM_LIMIT = 56 * 1024 * 1024
ROW_TILE = 512
MIXIN_PART = 256
MIXIN_BWD_PART = 128
ATTN_TILE = 1024
ATTN_SUB = 512
ATTN_FWD_SUB = 512
ATTN_BWD_HEADS = 2
MESH = pl.DeviceIdType.MESH

WHOLE = pl.BlockSpec(memory_space=pltpu.VMEM)
ANY = pl.BlockSpec(memory_space=pl.ANY)
HBM_SPEC = pl.BlockSpec(memory_space=pltpu.HBM)
SEM_SPEC = pl.BlockSpec(memory_space=pltpu.SEMAPHORE)


def _cparams(**kw):
    return pltpu.CompilerParams(vmem_limit_bytes=VMEM_LIMIT, **kw)


def _dot(a, b):
    return jnp.dot(a, b, preferred_element_type=F32)


def _dot_nt(a, b):
    return lax.dot_general(a, b, (((1,), (1,)), ((), ())), preferred_element_type=F32)


def _dot_tn(a, b):
    return lax.dot_general(a, b, (((0,), (0,)), ((), ())), preferred_element_type=F32)


def _rms(x, g, n):
    r = lax.rsqrt(jnp.sum(x * x, axis=-1, keepdims=True) * (1.0 / n) + EPS)
    return x * r * g, r


def _rms_bwd(x, r, g, dy, n):
    gdy = dy * g
    dx = r * gdy - x * (r * r * r) * (jnp.sum(x * gdy, axis=-1, keepdims=True) * (1.0 / n))
    dg = jnp.sum(dy * (x * r), axis=0, keepdims=True)
    return dx, dg


def _sigmoid(x):
    return 1.0 / (1.0 + jnp.exp(-x))


def rope_tables(ang):
    T = ang.shape[0]
    tm = min(ROW_TILE, T)

    def body(ang_ref, c_ref, sa_ref, sb_ref):
        a = ang_ref[...]
        lane = lax.broadcasted_iota(jnp.int32, a.shape, 1)
        s = jnp.sin(a)
        c_ref[...] = jnp.cos(a)
        sa_ref[...] = jnp.where(lane < QK_NOPE + QK_ROPE // 2, -s, 0.0)
        sb_ref[...] = jnp.where(lane >= QK_NOPE + QK_ROPE // 2, s, 0.0)

    row = pl.BlockSpec((tm, HEAD_PAD), lambda i: (i, 0))
    return pl.pallas_call(
        body, grid=(T // tm,), name="rope_tables", in_specs=[row], out_specs=[row] * 3,
        out_shape=[jax.ShapeDtypeStruct((T, HEAD_PAD), F32)] * 3, compiler_params=_cparams(),
    )(ang)


def _rope(x, c, sa, sb):
    half = QK_ROPE // 2
    return x * c + pltpu.roll(x, HEAD_PAD - half, 1) * sa + pltpu.roll(x, half, 1) * sb


def _rope_bwd(dy, c, sa, sb):
    half = QK_ROPE // 2
    return dy * c + pltpu.roll(dy * sa, half, 1) + pltpu.roll(dy * sb, HEAD_PAD - half, 1)


def _skewed(stages, groups):
    for t in range(len(stages) + len(groups) - 1):
        for p, g in enumerate(groups):
            if 0 <= t - p < len(stages):
                stages[t - p](g)


def _head_masks(shape, width):
    lane = lax.broadcasted_iota(jnp.int32, shape, len(shape) - 1)
    return [(lane >= width * h) & (lane < width * (h + 1)) for h in range(HEADS)]


def _qkv_pre(z, gql, wq, gkvl, wkv):
    ql = z[:, 0:Q_LORA]
    kvl = z[:, Q_LORA:Q_LORA + KV_LORA]
    kr = z[:, Q_LORA + KV_LORA:Q_LORA + KV_LORA + HEAD_PAD]
    qn, rq = _rms(ql, gql, Q_LORA)
    kvn, rkv = _rms(kvl, gkvl, KV_LORA)
    qn = qn.astype(BF16)
    kvn = kvn.astype(BF16)
    q_up = _dot_nt(qn, wq)
    kv_up = _dot_nt(kvn, wkv)
    return ql, kvl, kr, qn, rq, kvn, rkv, q_up, kv_up


def mixin_fwd(x, rope, gmix, win, gql, wq, gkvl, wkv, gqh, gkh):
    T = x.shape[0]
    tm = min(2 * ROW_TILE, T)
    part = min(MIXIN_PART, tm)

    def body(x_ref, c_ref, sa_ref, sb_ref, gmix_ref, win_ref, gql_ref, wq_ref, gkvl_ref, wkv_ref, gqh_ref, gkh_ref,
             z_ref, q_ref, k_ref, v_ref, kt_ref, vt_ref):
        def project(g):
            hn = _rms(x_ref[g["rows"], :], gmix_ref[...], D_MODEL)[0].astype(BF16)
            g["z"] = _dot_nt(hn, win_ref[...])
            z_ref[g["rows"], :] = g["z"]

        def latents(g):
            _, _, g["kr"], _, _, _, _, g["q_up"], g["kv_up"] = _qkv_pre(g["z"], gql_ref[...], wq_ref[...], gkvl_ref[...],
                                                                        wkv_ref[...])

        def heads(g):
            rows = g["rows"]
            c, sa, sb = c_ref[rows, :], sa_ref[rows, :], sb_ref[rows, :]
            for h in range(HEADS):
                lo = HEAD_PAD * h
                qh = _rms(g["q_up"][:, lo:lo + HEAD_PAD], gqh_ref[...], QK_HEAD)[0]
                q_ref[h, rows, :] = (_rope(qh, c, sa, sb) * (ATTN_SCALE * LOG2E)).astype(BF16)
                kh = _rope(_rms(g["kv_up"][:, lo:lo + HEAD_PAD] + g["kr"], gkh_ref[...], QK_HEAD)[0], c, sa, sb)
                k_ref[h, rows, :] = kh.astype(BF16)
                kt_ref[h, :, rows] = jnp.transpose(kh).astype(BF16)
                vh = g["kv_up"][:, HEADS * HEAD_PAD + lo:HEADS * HEAD_PAD + lo + HEAD_PAD]
                v_ref[h, rows, :] = vh.astype(BF16)
                vt_ref[h, :, rows] = jnp.transpose(vh).astype(BF16)

        _skewed((project, latents, heads), [dict(rows=pl.ds(part * p, part)) for p in range(tm // part)])

    row = lambda w: pl.BlockSpec((tm, w), lambda i: (i, 0))
    head = pl.BlockSpec((HEADS, tm, HEAD_PAD), lambda i: (0, i, 0))
    head_t = pl.BlockSpec((HEADS, HEAD_PAD, tm), lambda i: (0, 0, i))
    return pl.pallas_call(
        body, grid=(T // tm,), name="mixin_fwd",
        in_specs=[row(D_MODEL)] + [row(HEAD_PAD)] * 3 + [WHOLE] * 8,
        out_specs=[row(Z_WIDTH), head, head, head, head_t, head_t],
        out_shape=[jax.ShapeDtypeStruct((T, Z_WIDTH), F32)] + [jax.ShapeDtypeStruct((HEADS, T, HEAD_PAD), BF16)] * 3
                  + [jax.ShapeDtypeStruct((HEADS, HEAD_PAD, T), BF16)] * 2,
        compiler_params=_cparams(),
    )(x, *rope, gmix, win, gql, wq, gkvl, wkv, gqh, gkh)


def _pair_tables(pairs):
    return (jnp.asarray([p[0] for p in pairs], jnp.int32), jnp.asarray([p[1] for p in pairs], jnp.int32))


def attn_fwd(q, k, vt):
    _, T, _ = q.shape
    tb = min(ATTN_TILE, T)
    sq = min(ATTN_FWD_SUB, tb)
    nb = T // tb

    pairs = [(i, j) for i in range(nb) for j in range(i + 1)]

    def body(i_tab, j_tab, q_ref, k_ref, vt_ref, o_ref, lse_ref, m_s, l_s, acc_s):
        i, j = i_tab[pl.program_id(0)], j_tab[pl.program_id(0)]

        @pl.when(j == 0)
        def _():
            m_s[...] = jnp.full(m_s.shape, -jnp.inf, F32)
            l_s[...] = jnp.zeros(l_s.shape, F32)
            acc_s[...] = jnp.zeros(acc_s.shape, F32)

        def scores(g):
            st = _dot_nt(k_ref[g["h"], g["kb"], :], q_ref[g["h"], g["qa"], :])
            if g["masked"]:
                krow = lax.broadcasted_iota(jnp.int32, st.shape, 0)
                qcol = g["q0"] + lax.broadcasted_iota(jnp.int32, st.shape, 1)
                st = jnp.where(krow <= qcol, st, NEG_BIG)
            g["st"] = st

        def new_max(g):
            h, qa = g["h"], g["qa"]
            m_prev = m_s[h, :, qa]
            g["m"] = jnp.maximum(m_prev, jnp.max(g["st"], axis=0, keepdims=True))
            g["alpha"] = jnp.exp2(m_prev - g["m"])
            m_s[h, :, qa] = g["m"]

        def weights(g):
            pt = jnp.exp2(g["st"] - g["m"])
            g["lsum"] = jnp.sum(pt, axis=0, keepdims=True)
            g["pt"] = pt.astype(BF16)

        def accumulate(g):
            h, qa = g["h"], g["qa"]
            l_s[h, :, qa] = g["alpha"] * l_s[h, :, qa] + g["lsum"]
            acc_s[h, :, qa] = g["alpha"] * acc_s[h, :, qa] + _dot(vt_ref[h, :, g["kb"]], g["pt"])

        def tiles(masked):
            return [dict(h=h, q0=q0, qa=slice(q0, q0 + sq), kb=slice(0, q0 + sq if masked else tb), masked=masked)
                    for h in range(HEADS) for q0 in range(0, tb, sq)]

        @pl.when(j < i)
        def _():
            _skewed((scores, new_max, weights, accumulate), tiles(False))

        @pl.when(j == i)
        def _():
            _skewed((scores, new_max, weights, accumulate), tiles(True))
            for h in range(HEADS):
                l = l_s[h]
                o_ref[:, HEAD_PAD * h:HEAD_PAD * (h + 1)] = jnp.transpose(acc_s[h] / l)
                lse_ref[h] = m_s[h] + jnp.log2(l)

    qspec = pl.BlockSpec((HEADS, tb, HEAD_PAD), lambda p, it, jt: (0, it[p], 0))
    kspec = pl.BlockSpec((HEADS, tb, HEAD_PAD), lambda p, it, jt: (0, jt[p], 0))
    vspec = pl.BlockSpec((HEADS, HEAD_PAD, tb), lambda p, it, jt: (0, 0, jt[p]))
    grid_spec = pltpu.PrefetchScalarGridSpec(
        num_scalar_prefetch=2, grid=(len(pairs),),
        in_specs=[qspec, kspec, vspec],
        out_specs=[pl.BlockSpec((tb, MLA_WIDTH), lambda p, it, jt: (it[p], 0)),
                   pl.BlockSpec((HEADS, 1, tb), lambda p, it, jt: (0, 0, it[p]))],
        scratch_shapes=[pltpu.VMEM((HEADS, 1, tb), F32), pltpu.VMEM((HEADS, 1, tb), F32), pltpu.VMEM((HEADS, HEAD_PAD, tb), F32)])
    return pl.pallas_call(
        body, grid_spec=grid_spec, name="attn_fwd",
        out_shape=[jax.ShapeDtypeStruct((T, MLA_WIDTH), F32), jax.ShapeDtypeStruct((HEADS, 1, T), F32)],
        compiler_params=_cparams(),
    )(*_pair_tables(pairs), q, k, vt)


def _sgu_fwd_chunk(vn_c, wcat, bz, masks):
    vstack = jnp.concatenate([jnp.where(mk, vn_c, 0.0).astype(BF16) for mk in masks], axis=0)
    return _dot(wcat, vstack) + bz


def _pool_counts(i, tm):
    pos1 = (i * tm + 1 + lax.broadcasted_iota(jnp.int32, (tm, POOL_WIDTH), 0)).astype(F32)
    lane = lax.broadcasted_iota(jnp.int32, (tm, POOL_WIDTH), 1)
    win = jnp.where(lane < 64, 2.0, jnp.where(lane < 128, 4.0, jnp.where(lane < 192, 8.0, 16.0)))
    return jnp.minimum(pos1, win), lane


def _by_group(lane, s2, s4, s8, s16):
    return jnp.where(lane < 64, s2, jnp.where(lane < 128, s4, jnp.where(lane < 192, s8, s16)))


def _pool_means(pin, halo, i, tm):
    s1 = jnp.concatenate([halo, pin], axis=0)
    s2 = s1 + pltpu.roll(s1, 1, 0)
    s4 = s2 + pltpu.roll(s2, 2, 0)
    s8 = s4 + pltpu.roll(s4, 4, 0)
    s16 = s8 + pltpu.roll(s8, 8, 0)
    cnt, lane = _pool_counts(i, tm)
    sel = _by_group(lane, s2[POOL_HALO:], s4[POOL_HALO:], s8[POOL_HALO:], s16[POOL_HALO:])
    return sel / cnt - pin


def _mixers_fwd_tile(i, tm, a, u, vs, pin, halo, gsgu, wcat, bz, wp, pscale, goa, gos, gop):
    vn, rv = _rms(vs, gsgu, SGU_WIDTH)
    masks = _head_masks((CHUNK, SGU_WIDTH), SGU_HEAD_DIM)
    zc = jnp.concatenate([_sgu_fwd_chunk(vn[CHUNK * c:CHUNK * (c + 1)], wcat, bz, masks) for c in range(tm // CHUNK)], axis=0)
    gm = u * zc
    halo = jnp.where(i > 0, halo, 0.0)
    m = _pool_means(pin, halo, i, tm).astype(BF16)
    yp_pre = _dot(m, wp)
    yp = yp_pre * pscale
    na, ra = _rms(a, goa, MLA_WIDTH)
    ng, rg = _rms(gm, gos, SGU_WIDTH)
    npo, rp = _rms(yp, gop, POOL_WIDTH)
    mix = jnp.concatenate([na, ng, npo], axis=1).astype(BF16)
    return vn, rv, zc, gm, m, yp_pre, yp, ra, rg, rp, mix


def _z_specs(tm):
    col = lambda c: pl.BlockSpec((tm, 256), lambda i: (i, c))
    halo = pl.BlockSpec((POOL_HALO, 256), lambda i: (jnp.maximum(i * (tm // POOL_HALO) - 1, 0), 4))
    return [col(2), col(3), col(4), halo]


def mixers_fwd(x, a, z, gsgu, wcat, bz, wp, pscale, goa, gos, gop, wout):
    T = x.shape[0]
    tm = min(ROW_TILE, T)

    def body(x_ref, a_ref, u_ref, vs_ref, pin_ref, halo_ref, gsgu_ref, wcat_ref, bz_ref, wp_ref, ps_ref,
             goa_ref, gos_ref, gop_ref, wout_ref, x1_ref):
        i = pl.program_id(0)
        mix = _mixers_fwd_tile(i, tm, a_ref[...], u_ref[...], vs_ref[...], pin_ref[...], halo_ref[...], gsgu_ref[...],
                               wcat_ref[...], bz_ref[...], wp_ref[...], ps_ref[...], goa_ref[...], gos_ref[...],
                               gop_ref[...])[-1]
        x1_ref[...] = x_ref[...] + _dot(mix, wout_ref[...])

    row = lambda w: pl.BlockSpec((tm, w), lambda i: (i, 0))
    return pl.pallas_call(
        body, grid=(T // tm,), name="mixers_fwd",
        in_specs=[row(D_MODEL), row(MLA_WIDTH)] + _z_specs(tm) + [WHOLE] * 9,
        out_specs=row(D_MODEL),
        out_shape=jax.ShapeDtypeStruct((T, D_MODEL), F32),
        compiler_params=_cparams(),
    )(x, a, z, z, z, z, gsgu, wcat, bz, wp, pscale, goa, gos, gop, wout)


def ffn_fwd(x1, gffn, wg, wu, wd, target=None):
    T = x1.shape[0]
    tm = min(ROW_TILE, T)
    with_loss = target is not None

    def body(*refs):
        x1_ref, gffn_ref, wg_ref, wu_ref, wd_ref = refs[:5]
        outs = refs[6:] if with_loss else refs[5:]
        x1v = x1_ref[...]
        h2 = _rms(x1v, gffn_ref[...], D_MODEL)[0].astype(BF16)
        acc = x1v
        for c in range(FFN_HIDDEN // FFN_CHUNK):
            sl = slice(FFN_CHUNK * c, FFN_CHUNK * (c + 1))
            g = _dot_nt(h2, wg_ref[sl, :])
            u = _dot_nt(h2, wu_ref[sl, :])
            outs[-2][:, sl] = g.astype(BF16)
            outs[-1][:, sl] = u.astype(BF16)
            act = (g * _sigmoid(g) * u).astype(BF16)
            acc = acc + _dot(act, wd_ref[sl, :])
        if not with_loss:
            outs[0][...] = acc
            return
        dy_ref, loss_ref = outs[0], outs[1]

        @pl.when(pl.program_id(0) == 0)
        def _():
            loss_ref[...] = jnp.zeros(loss_ref.shape, F32)

        err = acc - refs[5][...]
        dy_ref[...] = err * (1.0 / D_MODEL)
        per_row = jnp.sum(err * err, axis=1, keepdims=True) * (1.0 / D_MODEL)
        loss_ref[...] += 0.5 * jnp.sum(per_row, axis=0, keepdims=True)

    row = lambda w: pl.BlockSpec((tm, w), lambda i: (i, 0))
    hidden = [jax.ShapeDtypeStruct((T, FFN_HIDDEN), BF16)] * 2
    if with_loss:
        return pl.pallas_call(
            body, grid=(T // tm,), name="ffn_fwd_loss",
            in_specs=[row(D_MODEL)] + [WHOLE] * 4 + [row(D_MODEL)],
            out_specs=[row(D_MODEL), pl.BlockSpec((1, 1), lambda i: (0, 0)), row(FFN_HIDDEN), row(FFN_HIDDEN)],
            out_shape=[jax.ShapeDtypeStruct((T, D_MODEL), F32), jax.ShapeDtypeStruct((1, 1), F32)] + hidden,
            compiler_params=_cparams(),
        )(x1, gffn, wg, wu, wd, target)
    return pl.pallas_call(
        body, grid=(T // tm,), name="ffn_fwd",
        in_specs=[row(D_MODEL)] + [WHOLE] * 4,
        out_specs=[row(D_MODEL), row(FFN_HIDDEN), row(FFN_HIDDEN)],
        out_shape=[jax.ShapeDtypeStruct((T, D_MODEL), F32)] + hidden,
        compiler_params=_cparams(),
    )(x1, gffn, wg, wu, wd)


def _acc_spec(shape):
    return pl.BlockSpec(shape, lambda i: (0,) * len(shape))


def ffn_bwd(dx2, x1, gs, us, gffn, wg, wu, wd):
    T = x1.shape[0]
    tm = min(ROW_TILE // 2, T)

    def body(dx2_ref, x1_ref, gs_ref, us_ref, gffn_ref, wg_ref, wu_ref, wd_ref,
             dx1_ref, h2_ref, act_ref, dg_ref, du_ref, dgffn_ref):
        @pl.when(pl.program_id(0) == 0)
        def _():
            dgffn_ref[...] = jnp.zeros(dgffn_ref.shape, F32)

        dx2v = dx2_ref[...]
        dy = dx2v.astype(BF16)
        x1v = x1_ref[...]
        h2, r = _rms(x1v, gffn_ref[...], D_MODEL)
        h2_ref[...] = h2.astype(BF16)
        for c in range(FFN_HIDDEN // FFN_CHUNK):
            sl = slice(FFN_CHUNK * c, FFN_CHUNK * (c + 1))
            g = gs_ref[:, sl].astype(F32)
            u = us_ref[:, sl].astype(F32)
            dact = _dot_nt(dy, wd_ref[sl, :])
            sg = _sigmoid(g)
            silu = g * sg
            act_ref[:, sl] = (silu * u).astype(BF16)
            dg_ref[:, sl] = (dact * u * (sg * (1.0 + g * (1.0 - sg)))).astype(BF16)
            du_ref[:, sl] = (dact * silu).astype(BF16)
        dh2 = _dot(dg_ref[...], wg_ref[...]) + _dot(du_ref[...], wu_ref[...])
        dxn, dgn = _rms_bwd(x1v, r, gffn_ref[...], dh2, D_MODEL)
        dx1_ref[...] = dx2v + dxn
        dgffn_ref[...] += dgn

    row = lambda w: pl.BlockSpec((tm, w), lambda i: (i, 0))
    return pl.pallas_call(
        body, grid=(T // tm,), name="ffn_bwd",
        in_specs=[row(D_MODEL), row(D_MODEL), row(FFN_HIDDEN), row(FFN_HIDDEN)] + [WHOLE] * 4,
        out_specs=[row(D_MODEL), row(D_MODEL), row(FFN_HIDDEN), row(FFN_HIDDEN), row(FFN_HIDDEN), _acc_spec((1, D_MODEL))],
        out_shape=[jax.ShapeDtypeStruct((T, D_MODEL), F32), jax.ShapeDtypeStruct((T, D_MODEL), BF16),
                   jax.ShapeDtypeStruct((T, FFN_HIDDEN), BF16), jax.ShapeDtypeStruct((T, FFN_HIDDEN), BF16),
                   jax.ShapeDtypeStruct((T, FFN_HIDDEN), BF16), jax.ShapeDtypeStruct((1, D_MODEL), F32)],
        compiler_params=_cparams(),
    )(dx2, x1, gs, us, gffn, wg, wu, wd)


TN_K_TILE = 2048
TN_ACC_BYTES = 6 * 1024 * 1024


def matmul_tn(a, b, name):
    T, M = a.shape
    N = b.shape[1]
    tk = min(TN_K_TILE, T)
    tm = M if M <= 1024 else M // 2
    tn = max(d for d in range(128, N + 1, 128) if N % d == 0 and tm * d * 4 <= TN_ACC_BYTES)
    nk = T // tk

    def body(a_ref, b_ref, o_ref, acc):
        k = pl.program_id(2)

        @pl.when(k == 0)
        def _():
            acc[...] = jnp.zeros(acc.shape, F32)

        acc[...] += _dot_tn(a_ref[...].astype(BF16), b_ref[...].astype(BF16))

        @pl.when(k == nk - 1)
        def _():
            o_ref[...] = acc[...].astype(BF16)

    return pl.pallas_call(
        body, grid=(M // tm, N // tn, nk), name=name,
        in_specs=[pl.BlockSpec((tk, tm), lambda i, j, k: (k, i)), pl.BlockSpec((tk, tn), lambda i, j, k: (k, j))],
        out_specs=pl.BlockSpec((tm, tn), lambda i, j, k: (i, j)),
        out_shape=jax.ShapeDtypeStruct((M, N), BF16),
        scratch_shapes=[pltpu.VMEM((tm, tn), F32)],
        compiler_params=_cparams(),
    )(a, b)


def mixers_bwd(dx1, a, z, gsgu, wcat, wcat_t, bz, wp, pscale, goa, gos, gop, wout):
    T = a.shape[0]
    tm = min(ROW_TILE, T)

    def body(dx1_ref, a_ref, u_ref, vs_ref, pin_ref, halo_ref, gsgu_ref, wcat_ref, wcatt_ref, bz_ref, wp_ref,
             ps_ref, goa_ref, gos_ref, gop_ref, wout_ref,
             da_ref, delta_ref, du_ref, dvs_ref, dm_ref, mix_ref,
             dgoa_ref, dgos_ref, dgop_ref, dps_ref, dwp_ref, dwsp_ref, db_ref, dgsgu_ref):
        i = pl.program_id(0)

        @pl.when(i == 0)
        def _():
            for r in (dgoa_ref, dgos_ref, dgop_ref, dps_ref, dwp_ref, dwsp_ref, db_ref, dgsgu_ref):
                r[...] = jnp.zeros(r.shape, F32)

        a_v, u, vs = a_ref[...], u_ref[...], vs_ref[...]
        goa, gos, gop, pscale_v = goa_ref[...], gos_ref[...], gop_ref[...], ps_ref[...]
        vn, rv, zc, gm, m, yp_pre, yp, ra, rg, rp, mix = _mixers_fwd_tile(
            i, tm, a_v, u, vs, pin_ref[...], halo_ref[...], gsgu_ref[...], wcat_ref[...], bz_ref[...], wp_ref[...],
            pscale_v, goa, gos, gop)
        mix_ref[...] = mix
        dmix = _dot_nt(dx1_ref[...].astype(BF16), wout_ref[...])
        da, dgoa = _rms_bwd(a_v, ra, goa, dmix[:, :MLA_WIDTH], MLA_WIDTH)
        dgm, dgos = _rms_bwd(gm, rg, gos, dmix[:, MLA_WIDTH:MLA_WIDTH + SGU_WIDTH], SGU_WIDTH)
        dyp, dgop = _rms_bwd(yp, rp, gop, dmix[:, MLA_WIDTH + SGU_WIDTH:], POOL_WIDTH)
        da_ref[...] = da
        dgoa_ref[...] += dgoa
        dgos_ref[...] += dgos
        dgop_ref[...] += dgop
        prod = da * a_v
        ones = jnp.ones((8, HEAD_PAD), F32)
        for h in range(HEADS):
            lo = HEAD_PAD * h
            sums = lax.dot_general(ones, prod[:, lo:lo + HEAD_PAD], (((1,), (1,)), ((), ())), preferred_element_type=F32,
                                   precision=lax.Precision.HIGHEST)
            delta_ref[h] = sums[0:1, :]
        dps_ref[...] += jnp.sum(dyp * yp_pre, axis=0, keepdims=True)
        dyp_pre = (dyp * pscale_v).astype(BF16)
        dwp_ref[...] += _dot_tn(m, dyp_pre)
        dm_ref[...] = _dot_nt(dyp_pre, wp_ref[...])
        du_ref[...] = dgm * zc
        dzc = dgm * u
        masks = _head_masks((CHUNK, SGU_WIDTH), SGU_HEAD_DIM)
        lane_b = lax.broadcasted_iota(jnp.int32, (CHUNK, HEAD_PAD), 1)
        dvn_parts = []
        dwsp = jnp.zeros(dwsp_ref.shape, F32)
        db = jnp.zeros(db_ref.shape, F32)
        for c in range(tm // CHUNK):
            dz_c = dzc[CHUNK * c:CHUNK * (c + 1)]
            dzstack = jnp.concatenate([jnp.where(mk, dz_c, 0.0).astype(BF16) for mk in masks], axis=0)
            dvn_parts.append(_dot(wcatt_ref[...], dzstack))
            dwsp = dwsp + _dot_nt(dzstack, vn[CHUNK * c:CHUNK * (c + 1)].astype(BF16))
            for h, mk in enumerate(masks):
                col = jnp.sum(jnp.where(mk, dz_c, 0.0), axis=1, keepdims=True)
                db = db + jnp.where(lane_b == h, col, 0.0)
        dwsp_ref[...] += dwsp
        db_ref[...] += db
        dvs, dgsgu = _rms_bwd(vs, rv, gsgu_ref[...], jnp.concatenate(dvn_parts, axis=0), SGU_WIDTH)
        dvs_ref[...] = dvs
        dgsgu_ref[...] += dgsgu

    row = lambda w: pl.BlockSpec((tm, w), lambda i: (i, 0))
    head = pl.BlockSpec((HEADS, 1, tm), lambda i: (0, 0, i))
    acc_shapes = [(1, MLA_WIDTH), (1, SGU_WIDTH), (1, POOL_WIDTH), (1, POOL_WIDTH), (POOL_WIDTH, POOL_WIDTH),
                  (HEADS * CHUNK, CHUNK), (CHUNK, HEAD_PAD), (1, SGU_WIDTH)]
    return pl.pallas_call(
        body, grid=(T // tm,), name="mixers_bwd",
        in_specs=[row(D_MODEL), row(MLA_WIDTH)] + _z_specs(tm) + [WHOLE] * 10,
        out_specs=[row(MLA_WIDTH), head, row(256), row(256), row(256), row(D_MODEL)] + [_acc_spec(s) for s in acc_shapes],
        out_shape=[jax.ShapeDtypeStruct((T, MLA_WIDTH), F32), jax.ShapeDtypeStruct((HEADS, 1, T), F32),
                   jax.ShapeDtypeStruct((T, 256), F32), jax.ShapeDtypeStruct((T, 256), F32),
                   jax.ShapeDtypeStruct((T, 256), F32), jax.ShapeDtypeStruct((T, D_MODEL), BF16)]
                  + [jax.ShapeDtypeStruct(s, F32) for s in acc_shapes],
        compiler_params=_cparams(),
    )(dx1, a, z, z, z, z, gsgu, wcat, wcat_t, bz, wp, pscale, goa, gos, gop, wout)


def pool_bwd(dm):
    T = dm.shape[0]
    tm = min(4 * ROW_TILE, T)
    nt = T // tm

    def body(dm_ref, next_ref, dpin_ref):
        i = pl.program_id(0)
        cnt, lane = _pool_counts(i, tm)
        dmv = dm_ref[...]
        win = _by_group(lane[:POOL_HALO], 2.0, 4.0, 8.0, 16.0)
        nxt = jnp.where(i < nt - 1, next_ref[...] / win, 0.0)
        r1 = jnp.concatenate([dmv / cnt, nxt], axis=0)
        n = tm + POOL_HALO
        r2 = r1 + pltpu.roll(r1, n - 1, 0)
        r4 = r2 + pltpu.roll(r2, n - 2, 0)
        r8 = r4 + pltpu.roll(r4, n - 4, 0)
        r16 = r8 + pltpu.roll(r8, n - 8, 0)
        dpin_ref[...] = _by_group(lane, r2[:tm], r4[:tm], r8[:tm], r16[:tm]) - dmv

    return pl.pallas_call(
        body, grid=(nt,), name="pool_bwd",
        in_specs=[pl.BlockSpec((tm, 256), lambda i: (i, 0)),
                  pl.BlockSpec((POOL_HALO, 256), lambda i: (jnp.minimum((i + 1) * (tm // POOL_HALO), T // POOL_HALO - 1), 0))],
        out_specs=pl.BlockSpec((tm, 256), lambda i: (i, 0)),
        out_shape=jax.ShapeDtypeStruct((T, 256), F32),
        compiler_params=_cparams(),
    )(dm, dm)


def attn_bwd(q, k, kt, v, do, lse, delta):
    _, T, _ = q.shape
    tb = min(ATTN_TILE, T)
    sb = min(ATTN_SUB, tb)
    ns = tb // sb
    nb = T // tb
    hb = ATTN_BWD_HEADS

    pairs = [(i, j) for i in range(nb) for j in range(i, nb)]

    def body(i_tab, j_tab, q_ref, k_ref, kt_ref, v_ref, do_ref, lse_ref, delta_ref, dqt_ref, dk_ref, dv_ref, dk_s, dv_s):
        i, j = i_tab[pl.program_id(1)], j_tab[pl.program_id(1)]

        @pl.when(pl.program_id(1) == 0)
        def _():
            dqt_ref[...] = jnp.zeros(dqt_ref.shape, F32)

        @pl.when(j == i)
        def _():
            dk_s[...] = jnp.zeros(dk_s.shape, F32)
            dv_s[...] = jnp.zeros(dv_s.shape, F32)

        def sub_block(h, a, b, masked):
            qa = slice(sb * a, sb * (a + 1))
            kb = slice(sb * b, sb * (b + 1))
            qv = q_ref[h, qa, :]
            dov = do_ref[qa, HEAD_PAD * h:HEAD_PAD * (h + 1)].astype(BF16)
            st = _dot_nt(k_ref[h, kb, :], qv)
            dpt = _dot_nt(v_ref[h, kb, :], dov)
            pt = jnp.exp2(st - lse_ref[h, :, qa])
            if masked:
                krow = lax.broadcasted_iota(jnp.int32, st.shape, 0)
                qcol = lax.broadcasted_iota(jnp.int32, st.shape, 1)
                pt = jnp.where(krow <= qcol, pt, 0.0)
            dst = (pt * (dpt - delta_ref[h, :, qa])).astype(BF16)
            dv_s[h, kb, :] += _dot(pt.astype(BF16), dov)
            dk_s[h, kb, :] += _dot(dst, qv)
            cols = pl.ds(pl.multiple_of(j * tb + sb * a, sb), sb)
            dqt_ref[h, :, cols] += _dot(kt_ref[h, :, kb], dst)

        @pl.when(j > i)
        def _():
            for a in range(ns):
                for h in range(hb):
                    for b in range(ns):
                        sub_block(h, a, b, False)

        @pl.when(j == i)
        def _():
            for a in range(ns):
                for h in range(hb):
                    for b in range(a + 1):
                        sub_block(h, a, b, a == b)

        @pl.when(j == nb - 1)
        def _():
            dk_ref[...] = dk_s[...] * (1.0 / LOG2E)
            dv_ref[...] = dv_s[...]

    qspec = pl.BlockSpec((hb, tb, HEAD_PAD), lambda g, p, it, jt: (g, jt[p], 0))
    kspec = pl.BlockSpec((hb, tb, HEAD_PAD), lambda g, p, it, jt: (g, it[p], 0))
    ktspec = pl.BlockSpec((hb, HEAD_PAD, tb), lambda g, p, it, jt: (g, 0, it[p]))
    rowspec = pl.BlockSpec((hb, 1, tb), lambda g, p, it, jt: (g, 0, jt[p]))
    grid_spec = pltpu.PrefetchScalarGridSpec(
        num_scalar_prefetch=2, grid=(HEADS // hb, len(pairs)),
        in_specs=[qspec, kspec, ktspec, kspec, pl.BlockSpec((tb, hb * HEAD_PAD), lambda g, p, it, jt: (jt[p], g)), rowspec, rowspec],
        out_specs=[pl.BlockSpec((hb, HEAD_PAD, T), lambda g, p, it, jt: (g, 0, 0)), kspec, kspec],
        scratch_shapes=[pltpu.VMEM((hb, tb, HEAD_PAD), F32), pltpu.VMEM((hb, tb, HEAD_PAD), F32)])
    return pl.pallas_call(
        body, grid_spec=grid_spec, name="attn_bwd",
        out_shape=[jax.ShapeDtypeStruct((HEADS, HEAD_PAD, T), F32)] + [jax.ShapeDtypeStruct((HEADS, T, HEAD_PAD), F32)] * 2,
        compiler_params=_cparams(),
    )(*_pair_tables(pairs), q, k, kt, v, do, lse, delta)


def mixin_bwd(dres, x, z, rope, dqt, dk, dv, du, dvs, dpin, gmix, win, gql, wq, gkvl, wkv, gqh, gkh):
    T = x.shape[0]
    tm = min(ROW_TILE, T)
    part = min(MIXIN_BWD_PART, tm)

    def body(dres_ref, x_ref, z_ref, c_ref, sa_ref, sb_ref, dqt_ref, dk_ref, dv_ref, du_ref, dvs_ref, dpin_ref,
             gmix_ref, win_ref, gql_ref, wq_ref, gkvl_ref, wkv_ref, gqh_ref, gkh_ref,
             dx_ref, hn_ref, dz_ref, qn_ref, dqup_ref, kvn_ref, dkvup_ref,
             dgmix_ref, dgql_ref, dgkvl_ref, dgqh_ref, dgkh_ref):
        @pl.when(pl.program_id(0) == 0)
        def _():
            for r in (dgmix_ref, dgql_ref, dgkvl_ref, dgqh_ref, dgkh_ref):
                r[...] = jnp.zeros(r.shape, F32)

        lane = lax.broadcasted_iota(jnp.int32, (part, HEAD_PAD), 1)
        rope_lanes = (lane >= QK_NOPE) & (lane < QK_HEAD)

        def recompute(g):
            rows = g["rows"]
            g["xv"] = x_ref[rows, :]
            hn, g["rx"] = _rms(g["xv"], gmix_ref[...], D_MODEL)
            hn_ref[rows, :] = hn.astype(BF16)
            g["ql"], g["kvl"], g["kr"], qn, g["rq"], kvn, g["rkv"], g["q_up"], g["kv_up"] = _qkv_pre(
                z_ref[rows, :], gql_ref[...], wq_ref[...], gkvl_ref[...], wkv_ref[...])
            qn_ref[rows, :] = qn
            kvn_ref[rows, :] = kvn

        def heads(g):
            rows = g["rows"]
            c, sa, sb = c_ref[rows, :], sa_ref[rows, :], sb_ref[rows, :]
            dkr = jnp.zeros((part, HEAD_PAD), F32)
            dgqh = jnp.zeros((1, HEAD_PAD), F32)
            dgkh = jnp.zeros((1, HEAD_PAD), F32)
            dq_parts, dk_parts, dv_parts = [], [], []
            for h in range(HEADS):
                lo = HEAD_PAD * h
                qh = g["q_up"][:, lo:lo + HEAD_PAD]
                rqh = lax.rsqrt(jnp.sum(qh * qh, axis=-1, keepdims=True) * (1.0 / QK_HEAD) + EPS)
                dq_h = jnp.transpose(dqt_ref[h, :, rows]) * ATTN_SCALE
                dqh, dg = _rms_bwd(qh, rqh, gqh_ref[...], _rope_bwd(dq_h, c, sa, sb), QK_HEAD)
                dgqh = dgqh + dg
                dq_parts.append(dqh)
                kh = g["kv_up"][:, lo:lo + HEAD_PAD] + g["kr"]
                rkh = lax.rsqrt(jnp.sum(kh * kh, axis=-1, keepdims=True) * (1.0 / QK_HEAD) + EPS)
                dkh, dg = _rms_bwd(kh, rkh, gkh_ref[...], _rope_bwd(dk_ref[h, rows, :], c, sa, sb), QK_HEAD)
                dgkh = dgkh + dg
                dkr = dkr + jnp.where(rope_lanes, dkh, 0.0)
                dk_parts.append(dkh)
                dv_parts.append(dv_ref[h, rows, :])
            dgqh_ref[...] += dgqh
            dgkh_ref[...] += dgkh
            g["dkr"] = dkr
            g["dq_up"] = jnp.concatenate(dq_parts, axis=1).astype(BF16)
            g["dkv_up"] = jnp.concatenate(dk_parts + dv_parts, axis=1).astype(BF16)
            dqup_ref[rows, :] = g["dq_up"]
            dkvup_ref[rows, :] = g["dkv_up"]

        def latents(g):
            rows = g["rows"]
            dql, dg = _rms_bwd(g["ql"], g["rq"], gql_ref[...], _dot(g["dq_up"], wq_ref[...]), Q_LORA)
            dgql_ref[...] += dg
            dkvl, dg = _rms_bwd(g["kvl"], g["rkv"], gkvl_ref[...], _dot(g["dkv_up"], wkv_ref[...]), KV_LORA)
            dgkvl_ref[...] += dg
            g["dz"] = jnp.concatenate([dql, dkvl, g["dkr"], du_ref[rows, :], dvs_ref[rows, :], dpin_ref[rows, :]],
                                      axis=1).astype(BF16)
            dz_ref[rows, :] = g["dz"]

        def inputs(g):
            rows = g["rows"]
            dxn, dg = _rms_bwd(g["xv"], g["rx"], gmix_ref[...], _dot(g["dz"], win_ref[...]), D_MODEL)
            dgmix_ref[...] += dg
            dx_ref[rows, :] = dres_ref[rows, :] + dxn

        _skewed((recompute, heads, latents, inputs), [dict(rows=pl.ds(part * p, part)) for p in range(tm // part)])

    row = lambda w: pl.BlockSpec((tm, w), lambda i: (i, 0))
    head = pl.BlockSpec((HEADS, tm, HEAD_PAD), lambda i: (0, i, 0))
    head_t = pl.BlockSpec((HEADS, HEAD_PAD, tm), lambda i: (0, 0, i))
    acc_shapes = [(1, D_MODEL), (1, Q_LORA), (1, KV_LORA), (1, HEAD_PAD), (1, HEAD_PAD)]
    out_rows = [(D_MODEL, F32), (D_MODEL, BF16), (Z_WIDTH, BF16), (Q_LORA, BF16), (HEADS * HEAD_PAD, BF16),
                (KV_LORA, BF16), (2 * HEADS * HEAD_PAD, BF16)]
    return pl.pallas_call(
        body, grid=(T // tm,), name="mixin_bwd",
        in_specs=[row(D_MODEL), row(D_MODEL), row(Z_WIDTH)] + [row(HEAD_PAD)] * 3 + [head_t, head, head, row(256), row(256), row(256)]
                 + [WHOLE] * 8,
        out_specs=[row(w) for w, _ in out_rows] + [_acc_spec(s) for s in acc_shapes],
        out_shape=[jax.ShapeDtypeStruct((T, w), dt) for w, dt in out_rows] + [jax.ShapeDtypeStruct(s, F32) for s in acc_shapes],
        compiler_params=_cparams(),
    )(dres, x, z, *rope, dqt, dk, dv, du, dvs, dpin, gmix, win, gql, wq, gkvl, wkv, gqh, gkh)


def _place():
    x, y, c = lax.axis_index("x"), lax.axis_index("y"), lax.axis_index("c")
    return x, y, c, 4 * x + 2 * y + c


def _layer_of(ref, l):
    return ref[:, l, :] if ref.shape[1] == DEPTH else ref[l]


def _layer_shape(shard):
    return (shard.shape[0], shard.shape[2]) if shard.shape[1] == DEPTH else shard.shape[1:]


def cast_shards(shards, wanted):
    n = len(shards)

    def body(*refs):
        for o_ref, (w, l) in zip(refs[n:], wanted):
            o_ref[...] = _layer_of(refs[w], l).astype(BF16)

    return pl.pallas_call(
        body, name="cast_shards", in_specs=[WHOLE] * n, out_specs=[WHOLE] * len(wanted),
        out_shape=[jax.ShapeDtypeStruct(_layer_shape(shards[w]), BF16) for w, _ in wanted],
        compiler_params=_cparams(),
    )(*shards)


def _peer(k):
    x, y, c, _ = _place()
    px = 1 - x if k & 4 else x
    py = 1 - y if k & 2 else y
    pc = 1 - c if k & 1 else c
    return (px, py, pc), 4 * px + 2 * py + pc


def exchange_start(srcs, after, gather, name):
    n = len(srcs)
    land_shapes = [((N_DEV,) + s.shape) if gather else s.shape for s in srcs]

    def body(*refs):
        src_refs, land_refs = refs[:n], refs[n:2 * n]
        send_sems, recv_sems = refs[2 * n + 1:3 * n + 1], refs[3 * n + 1:4 * n + 1]
        token = refs[-1]
        _, _, _, me = _place()
        for k in range(1, N_DEV):
            peer, peer_id = _peer(k)
            for w in range(n):
                pltpu.make_async_remote_copy(
                    src_ref=src_refs[w] if gather else src_refs[w].at[peer_id], dst_ref=land_refs[w].at[me],
                    send_sem=send_sems[w], recv_sem=recv_sems[w], device_id=peer, device_id_type=MESH).start()
        token[...] = jnp.zeros(token.shape, F32)

    hbm = lambda a: pltpu.with_memory_space_constraint(a, pltpu.HBM)
    outs = pl.pallas_call(
        body, name=name,
        out_shape=(pltpu.SemaphoreType.DMA(()),) * (2 * n)
                  + tuple(pltpu.HBM(s.shape, BF16) for s in srcs) + tuple(pltpu.HBM(s, BF16) for s in land_shapes)
                  + (jax.ShapeDtypeStruct((8, 128), F32),),
        in_specs=[HBM_SPEC] * (2 * n) + [ANY],
        out_specs=(SEM_SPEC,) * (2 * n) + (HBM_SPEC,) * (2 * n) + (WHOLE,),
        input_output_aliases={i: 2 * n + i for i in range(2 * n)},
        compiler_params=pltpu.CompilerParams(has_side_effects=pltpu.SideEffectType.DATAFLOW_SIDE_EFFECTING),
    )(*[hbm(s) for s in srcs], *[hbm(lax.empty(s, BF16)) for s in land_shapes], after)
    return list(outs[:n]), list(outs[n:2 * n]), list(outs[2 * n:3 * n]), list(outs[3 * n:4 * n]), outs[-1]


def exchange_wait(started, after, name):
    send_sems, recv_sems, srcs, lands, _ = started
    n = len(srcs)

    def body(*refs):
        land_refs = refs[n:2 * n]
        send_sems, recv_sems = refs[2 * n:3 * n], refs[3 * n:4 * n]
        x, y, c, _ = _place()
        for w in range(n):
            seven = land_refs[w].at[pl.ds(0, N_DEV - 1)]
            cp = pltpu.make_async_remote_copy(src_ref=seven, dst_ref=seven, send_sem=send_sems[w], recv_sem=recv_sems[w],
                                              device_id=(x, y, c), device_id_type=MESH)
            cp.wait_send()
            cp.wait_recv()

    outs = pl.pallas_call(
        body, name=name,
        out_shape=tuple(pltpu.HBM(s.shape, BF16) for s in srcs) + tuple(pltpu.HBM(l.shape, BF16) for l in lands),
        in_specs=[HBM_SPEC] * (2 * n) + [SEM_SPEC] * (2 * n) + [ANY],
        out_specs=(HBM_SPEC,) * (2 * n),
        input_output_aliases={i: i for i in range(2 * n)},
        compiler_params=pltpu.CompilerParams(has_side_effects=pltpu.SideEffectType.DATAFLOW_SIDE_EFFECTING),
    )(*srcs, *lands, *send_sems, *recv_sems, after)
    _, _, _, me = _place()
    filled = []
    for src, land in zip(outs[:n], outs[n:]):
        own = src[None] if src.ndim == 2 else lax.dynamic_slice_in_dim(src, me, 1, axis=0)
        filled.append(lax.dynamic_update_slice_in_dim(land, own, me, axis=0))
    return filled


def _adamw(w, g, m, v):
    m2 = ADAM_B1 * m + (1.0 - ADAM_B1) * g
    v2 = ADAM_B2 * v + (1.0 - ADAM_B2) * (g * g)
    m_hat = m2 / (1.0 - ADAM_B1 ** ADAM_STEP)
    v_hat = v2 / (1.0 - ADAM_B2 ** ADAM_STEP)
    delta = -ADAM_LR * (m_hat / (jnp.sqrt(v_hat) + ADAM_EPS) + ADAM_WD * w)
    return delta, m2, v2


def adamw_sharded(parts, w, m, v, name):
    L, R, C = w.shape
    fits = [d for d in range(16, min(R, 512) + 1, 16) if R % d == 0]
    br = max(fits) if fits else R
    nblk = R // br

    def body(*refs):
        p_refs = refs[:L]
        w_ref, m_ref, v_ref, g_ref, d_ref, m2_ref, v2_ref = refs[L:]

        def total(p_ref):
            g = p_ref[0].astype(F32)
            for s in range(1, N_DEV):
                g = g + p_ref[s].astype(F32)
            return g

        g = total(p_refs[0])
        for l in range(1, L):
            g = jnp.where(pl.program_id(0) == l, total(p_refs[l]), g)
        g_ref[...] = g
        d_ref[...], m2_ref[...], v2_ref[...] = _adamw(w_ref[...], g, m_ref[...], v_ref[...])

    blk = pl.BlockSpec((None, br, C), lambda l, i: (l, i, 0))

    def part_spec(k):
        return pl.BlockSpec((N_DEV, br, C), lambda l, i: (0, jnp.where(l == k, i, jnp.where(l < k, 0, nblk - 1)), 0))

    return pl.pallas_call(
        body, grid=(L, nblk), name=name,
        in_specs=[part_spec(k) for k in range(L)] + [blk, blk, blk],
        out_specs=[blk] * 4,
        out_shape=[jax.ShapeDtypeStruct((L, R, C), F32)] * 4,
        compiler_params=_cparams(),
    )(*parts, w, m, v)


def allreduce_adamw_small(gpart, w, m, v):
    R = gpart.shape[0]

    def body(g_ref, w_ref, m_ref, v_ref, grad_ref, d_ref, m2_ref, v2_ref, all_ref, send_sems, recv_sems):
        x, y, c, me = _place()
        sibling = (x, y, 1 - c)
        chips = [(1 - x, y), (x, 1 - y), (1 - x, 1 - y)]

        def copy(k, block_id, to, from_input):
            return pltpu.make_async_remote_copy(
                src_ref=g_ref if from_input else all_ref.at[block_id], dst_ref=all_ref.at[block_id],
                send_sem=send_sems.at[k], recv_sem=recv_sems.at[k], device_id=to, device_id_type=MESH)

        def block_of(cx, cy, cc):
            return 4 * cx + 2 * cy + cc

        all_ref[me] = g_ref[...]
        first = [copy(0, me, sibling, True)] + [copy(1 + j, me, (*chip, c), True) for j, chip in enumerate(chips)]
        for cp in first:
            cp.start()
        passed = [copy(4 + j, block_of(*chip, c), sibling, False) for j, chip in enumerate(chips)]
        for j, chip in enumerate(chips):
            copy(1 + j, block_of(*chip, c), (x, y, c), False).wait_recv()
            passed[j].start()
        copy(0, block_of(x, y, 1 - c), (x, y, c), False).wait_recv()
        for j, chip in enumerate(chips):
            copy(4 + j, block_of(*chip, 1 - c), (x, y, c), False).wait_recv()
        for cp in first + passed:
            cp.wait_send()
        g = all_ref[0]
        for s in range(1, N_DEV):
            g = g + all_ref[s]
        grad_ref[...] = g
        d_ref[...], m2_ref[...], v2_ref[...] = _adamw(w_ref[...], g, m_ref[...], v_ref[...])

    return pl.pallas_call(
        body, name="allreduce_adamw_small",
        in_specs=[WHOLE] * 4, out_specs=[WHOLE] * 4,
        out_shape=[jax.ShapeDtypeStruct((R, 128), F32)] * 4,
        scratch_shapes=[pltpu.VMEM((N_DEV, R, 128), F32), pltpu.SemaphoreType.DMA((7,)), pltpu.SemaphoreType.DMA((7,))],
        compiler_params=_cparams(),
    )(gpart, w, m, v)


def _shard_view(name, a):
    if name == "w_in":
        return a.transpose(2, 0, 1)
    return a.swapaxes(1, 2) if name in TRANSPOSED else a


def _shard_unview(name, a):
    if name == "w_in":
        return a.transpose(1, 2, 0)
    return a.swapaxes(1, 2) if name in TRANSPOSED else a


def _pad_head_rows(w, width):
    c = w.shape[1]
    return jnp.pad(w.reshape(HEADS, width, c), ((0, 0), (0, HEAD_PAD - width), (0, 0))).reshape(HEADS * HEAD_PAD, c)


def _unpad_head_rows(w, width):
    c = w.shape[1]
    return w.reshape(HEADS, HEAD_PAD, c)[:, :width].reshape(HEADS * width, c)


O1 = Q_LORA
O2 = O1 + KV_LORA
O3 = O2 + QK_ROPE


def _mixer_weights(gw):
    w_in = gw["w_in"].reshape(IN_WIDTH, D_MODEL)
    zero = lambda n: jnp.zeros((n, D_MODEL), BF16)
    win = jnp.concatenate([w_in[:O2], zero(QK_NOPE), w_in[O2:O3], zero(HEAD_PAD - QK_HEAD), w_in[O3:]], axis=0)
    wq = _pad_head_rows(gw["w_q_up"].reshape(HEADS * QK_HEAD, Q_LORA), QK_HEAD)
    w_kv = gw["w_kv_up"].reshape(HEADS, QK_NOPE + V_HEAD, KV_LORA)
    wk = jnp.pad(w_kv[:, :QK_NOPE], ((0, 0), (0, HEAD_PAD - QK_NOPE), (0, 0))).reshape(HEADS * HEAD_PAD, KV_LORA)
    wv = w_kv[:, QK_NOPE:].reshape(HEADS * V_HEAD, KV_LORA)
    wkv = jnp.concatenate([wk, wv], axis=0)
    out = dict(win=win, wq=wq, wkv=wkv)
    if "w_out" in gw:
        out["wout"] = gw["w_out"].reshape(D_MODEL, D_MODEL)
    return out


def _ffn_weights(gw):
    return dict(wg=gw["w_gate"].reshape(FFN_HIDDEN, D_MODEL), wu=gw["w_up"].reshape(FFN_HIDDEN, D_MODEL),
                wd=gw["w_down"].reshape(FFN_HIDDEN, D_MODEL))


def _layer_small(p, l):
    row = lambda a: a.reshape(1, -1)
    pad_head = lambda g: jnp.pad(g, (0, HEAD_PAD - QK_HEAD)).reshape(1, HEAD_PAD)
    tril = jnp.tril(jnp.ones((CHUNK, CHUNK), F32))
    wsp = p["w_spatial"][l] * tril
    wcat = jnp.concatenate([wsp[h] for h in range(HEADS)], axis=1).astype(BF16)
    wcat_t = jnp.concatenate([wsp[h].T for h in range(HEADS)], axis=1).astype(BF16)
    bz = jnp.repeat(p["b_spatial"][l].T, SGU_HEAD_DIM, axis=1)
    wp = jax.scipy.linalg.block_diag(*[p["w_pool"][l][g] for g in range(HEADS)]).astype(BF16)
    return dict(gmix=row(p["g_mix_norm"][l]), gql=row(p["g_q_lat"][l]), gkvl=row(p["g_kv_lat"][l]),
                gqh=pad_head(p["g_q_head"][l]), gkh=pad_head(p["g_k_head"][l]), gsgu=row(p["g_sgu_v"][l]),
                wcat=wcat, wcat_t=wcat_t, bz=bz, wp=wp, pscale=row(p["pool_scale"][l]),
                goa=row(p["g_out_mla"][l]), gos=row(p["g_out_sgu"][l]), gop=row(p["g_out_pool"][l]),
                gffn=row(p["g_ffn_norm"][l]))


MIXER_SIDE = ("w_in", "w_q_up", "w_kv_up", "w_out")
FFN_SIDE = ("w_gate", "w_up", "w_down")
TRANSPOSED = ("w_in", "w_q_up", "w_kv_up", "w_gate", "w_up")
FIRST_0 = ("w_in", "w_q_up", "w_kv_up")
LATER_0 = ("w_out",) + FFN_SIDE
SHARDED = MIXER_SIDE + FFN_SIDE
SMALL = ("g_mix_norm", "g_q_lat", "g_kv_lat", "g_q_head", "g_k_head", "g_sgu_v", "w_spatial", "b_spatial", "w_pool",
         "pool_scale", "g_out_mla", "g_out_sgu", "g_out_pool", "g_ffn_norm")
WEIGHTS = ("g_mix_norm", "w_in", "g_q_lat", "w_q_up", "g_kv_lat", "w_kv_up", "g_q_head", "g_k_head", "g_sgu_v", "w_spatial",
           "b_spatial", "w_pool", "pool_scale", "g_out_mla", "g_out_sgu", "g_out_pool", "w_out", "g_ffn_norm", "w_gate",
           "w_up", "w_down")
PACKED = SMALL + ("loss",)
PACK_ROWS = 8 * 128


def _pack_small(parts):
    flat = []
    for name in PACKED:
        a = parts[name].reshape(-1)
        flat.append(jnp.pad(a, (0, -a.shape[0] % PACK_ROWS)))
    return jnp.concatenate(flat).reshape(-1, 128)


def _unpack_small(packed, like):
    out, row = {}, 0
    for name in PACKED:
        n = math.prod(like[name].shape)
        rows = -(-n // PACK_ROWS) * 8
        out[name] = packed[row:row + rows].reshape(-1)[:n].reshape(like[name].shape)
        row += rows
    return out


def _forward_attention(x, rope, W, S):
    z, q, k, v, kt, vt = mixin_fwd(x, rope, S["gmix"], W["win"], S["gql"], W["wq"], S["gkvl"], W["wkv"], S["gqh"], S["gkh"])
    a, lse = attn_fwd(q, k, vt)
    return dict(x=x, z=z, q=q, k=k, kt=kt, v=v, a=a, lse=lse)


def _forward_mixers(A, wout, S):
    A["x1"] = mixers_fwd(A["x"], A["a"], A["z"], S["gsgu"], S["wcat"], S["bz"], S["wp"], S["pscale"], S["goa"], S["gos"],
                         S["gop"], wout)
    return A["x1"]


def _backward_ffn(dx2, W, S, A, l):
    dx1, h2, act, dg, du_ffn, dgffn = ffn_bwd(dx2, A["x1"], A["gs"], A["us"], S["gffn"], W["wg"], W["wu"], W["wd"])
    d_wd = matmul_tn(act, dx2, f"dw_down_{l}")
    d_wg = matmul_tn(dg, h2, f"dw_gate_{l}")
    d_wu = matmul_tn(du_ffn, h2, f"dw_up_{l}")
    big = {n: d.reshape(N_DEV, -1, D_MODEL) for n, d in (("w_gate", d_wg), ("w_up", d_wu), ("w_down", d_wd))}
    return dx1, big, dgffn


def _backward_mixers(dx1, dgffn, rope, W, S, A, l, send_wout=None):
    (da, delta, du, dvs, dm, mix, dgoa, dgos, dgop, dps, dwp, dwsp, db, dgsgu) = mixers_bwd(
        dx1, A["a"], A["z"], S["gsgu"], S["wcat"], S["wcat_t"], S["bz"], S["wp"], S["pscale"], S["goa"], S["gos"],
        S["gop"], W["wout"])
    d_wout = matmul_tn(mix, dx1, f"dw_out_{l}")
    if send_wout is not None:
        delta = delta + send_wout(d_wout.reshape(N_DEV, -1, D_MODEL))[0, 0]
    dpin = pool_bwd(dm)
    dqt, dk, dv = attn_bwd(A["q"], A["k"], A["kt"], A["v"], da, A["lse"], delta)
    (dx, hn, dz, qn, dq_up, kvn, dkv_up, dgmix, dgql, dgkvl, dgqh, dgkh) = mixin_bwd(
        dx1, A["x"], A["z"], rope, dqt, dk, dv, du, dvs, dpin, S["gmix"], W["win"], S["gql"], W["wq"], S["gkvl"],
        W["wkv"], S["gqh"], S["gkh"])
    d_win = matmul_tn(dz, hn, f"dw_in_{l}")
    d_wq = matmul_tn(dq_up, qn, f"dw_q_up_{l}")
    d_wkv = matmul_tn(dkv_up, kvn, f"dw_kv_up_{l}")
    d_win = jnp.concatenate([d_win[:O2], d_win[O2 + QK_NOPE:O2 + QK_HEAD], d_win[O2 + HEAD_PAD:]], axis=0)
    d_wk = d_wkv[:HEADS * HEAD_PAD].reshape(HEADS, HEAD_PAD, KV_LORA)[:, :QK_NOPE]
    d_wv = d_wkv[HEADS * HEAD_PAD:].reshape(HEADS, V_HEAD, KV_LORA)
    d_wkv = jnp.concatenate([d_wk, d_wv], axis=1)
    big = dict(w_in=d_win.reshape(N_DEV, -1, D_MODEL), w_q_up=_unpad_head_rows(d_wq, QK_HEAD).reshape(N_DEV, -1, Q_LORA),
               w_kv_up=d_wkv.reshape(N_DEV, -1, KV_LORA), w_out=d_wout.reshape(N_DEV, -1, D_MODEL))
    tril = jnp.tril(jnp.ones((CHUNK, CHUNK), F32))
    small = dict(g_mix_norm=dgmix[0], g_q_lat=dgql[0], g_kv_lat=dgkvl[0], g_q_head=dgqh[0, :QK_HEAD], g_k_head=dgkh[0, :QK_HEAD],
                 g_sgu_v=dgsgu[0], w_spatial=dwsp.reshape(HEADS, CHUNK, CHUNK) * tril, b_spatial=db[:, :HEADS].T,
                 w_pool=jnp.stack([dwp[64 * g:64 * (g + 1), 64 * g:64 * (g + 1)] for g in range(HEADS)]),
                 pool_scale=dps[0], g_out_mla=dgoa[0], g_out_sgu=dgos[0], g_out_pool=dgop[0], g_ffn_norm=dgffn[0])
    return dx, big, small


def kernel(x, positions, g_mix_norm, w_in, g_q_lat, w_q_up, g_kv_lat, w_kv_up, g_q_head, g_k_head, g_sgu_v, w_spatial, b_spatial, w_pool, pool_scale, g_out_mla, g_out_sgu, g_out_pool, w_out, g_ffn_norm, w_gate, w_up, w_down, loss_target, m_g_mix_norm, m_w_in, m_g_q_lat, m_w_q_up, m_g_kv_lat, m_w_kv_up, m_g_q_head, m_g_k_head, m_g_sgu_v, m_w_spatial, m_b_spatial, m_w_pool, m_pool_scale, m_g_out_mla, m_g_out_sgu, m_g_out_pool, m_w_out, m_g_ffn_norm, m_w_gate, m_w_up, m_w_down, v_g_mix_norm, v_w_in, v_g_q_lat, v_w_q_up, v_g_kv_lat, v_w_kv_up, v_g_q_head, v_g_k_head, v_g_sgu_v, v_w_spatial, v_b_spatial, v_w_pool, v_pool_scale, v_g_out_mla, v_g_out_sgu, v_g_out_pool, v_w_out, v_g_ffn_norm, v_w_gate, v_w_up, v_w_down):
    given = dict(locals())
    w = {n: given[n] for n in WEIGHTS}
    m = {n: given["m_" + n] for n in WEIGHTS}
    v = {n: given["v_" + n] for n in WEIGHTS}
    T = x.shape[1]
    xs = x.reshape(T, D_MODEL)

    half = QK_ROPE // 2
    inv_freq = 1.0 / (ROPE_THETA ** (jnp.arange(half, dtype=F32) / half))
    ang16 = positions.reshape(T).astype(F32)[:, None] * inv_freq
    ang = jnp.concatenate([jnp.zeros((T, QK_NOPE), F32), ang16, ang16, jnp.zeros((T, HEAD_PAD - QK_HEAD), F32)], axis=1)

    wv = {n: _shard_view(n, w[n]) for n in SHARDED}
    wanted = [(SHARDED.index(n), 0) for n in FIRST_0 + LATER_0] + [(i, 1) for i in range(len(SHARDED))]
    bf = cast_shards([wv[n] for n in SHARDED], wanted)
    nf, nl = len(FIRST_0), len(LATER_0)
    ag_first0 = exchange_start(bf[:nf], ang, True, "ag_start_first0")
    rope = rope_tables(ang + ag_first0[-1][0, 0])
    W0 = dict(zip(FIRST_0, exchange_wait(ag_first0, rope[0], "ag_wait_first0")))
    ag_later0 = exchange_start(bf[nf:nf + nl], W0["w_in"], True, "ag_start_later0")
    ag_l1 = exchange_start(bf[nf + nl:], ag_later0[-1], True, "ag_start_l1")
    Ss = [_layer_small(w, l) for l in range(DEPTH)]

    S0 = dict(Ss[0], gmix=Ss[0]["gmix"] + (ag_later0[-1][0, 0] + ag_l1[-1][0, 0]))
    A0 = _forward_attention(xs, rope, _mixer_weights(W0), S0)
    W0.update(zip(LATER_0, exchange_wait(ag_later0, A0["a"], "ag_wait_later0")))
    W0 = {**_mixer_weights(W0), **_ffn_weights(W0)}
    x1 = _forward_mixers(A0, W0["wout"], Ss[0])
    h, A0["gs"], A0["us"] = ffn_fwd(x1, Ss[0]["gffn"], W0["wg"], W0["wu"], W0["wd"])
    layer1 = dict(zip(SHARDED, exchange_wait(ag_l1, h, "ag_wait_l1")))
    W1 = {**_mixer_weights(layer1), **_ffn_weights(layer1)}
    A1 = _forward_attention(h, rope, W1, Ss[1])
    x1 = _forward_mixers(A1, W1["wout"], Ss[1])
    dh, loss_part, A1["gs"], A1["us"] = ffn_fwd(x1, Ss[1]["gffn"], W1["wg"], W1["wu"], W1["wd"],
                                                target=loss_target.reshape(T, D_MODEL))

    dx1, big_f1, dgffn1 = _backward_ffn(dh, W1, Ss[1], A1, 1)
    dh, big_m1, small1 = _backward_mixers(dx1, dgffn1, rope, W1, Ss[1], A1, 1)
    rs_l1 = exchange_start([{**big_m1, **big_f1}[n] for n in SHARDED], dh, False, "rs_start_l1")
    S0 = dict(Ss[0], gffn=Ss[0]["gffn"] + rs_l1[-1][0, 0])
    dx1, big_f0, dgffn0 = _backward_ffn(dh, W0, S0, A0, 0)
    rs_ffn0 = exchange_start([big_f0[n] for n in FFN_SIDE], dx1, False, "rs_start_ffn0")
    S0 = dict(Ss[0], gsgu=Ss[0]["gsgu"] + rs_ffn0[-1][0, 0])
    rs_out0 = []

    def send_wout(pieces):
        rs_out0.append(exchange_start([pieces], dx1, False, "rs_start_out0"))
        return rs_out0[0][-1]

    dh, big_m0, small0 = _backward_mixers(dx1, dgffn0, rope, W0, S0, A0, 0, send_wout)
    grad_x = dh.reshape(x.shape)
    smalls = [small0, small1]

    rs_first0 = exchange_start([big_m0[n] for n in FIRST_0], dh, False, "rs_start_first0")
    parts1 = dict(zip(SHARDED, exchange_wait(rs_l1, rs_first0[-1], "rs_wait_l1")))
    parts0 = dict(zip(FFN_SIDE, exchange_wait(rs_ffn0, rs_first0[-1], "rs_wait_ffn0")))
    parts0["w_out"] = exchange_wait(rs_out0[0], rs_first0[-1], "rs_wait_out0")[0]
    grad, delta, new_m, new_v = {}, {}, {}, {}

    small_part = {n: jnp.stack([smalls[l][n] for l in range(DEPTH)]) for n in SMALL}
    small_part["loss"] = loss_part
    no_loss = {"loss": jnp.zeros((1, 1), F32)}
    outs = allreduce_adamw_small(_pack_small(small_part), *[_pack_small({**a, **no_loss}) for a in (w, m, v)])
    for d, o in zip((grad, delta, new_m, new_v), outs):
        d.update(_unpack_small(o, {**w, **no_loss}))
    loss = grad["loss"][0, 0]

    for n in LATER_0 + FIRST_0:
        if n == FIRST_0[0]:
            parts0.update(zip(FIRST_0, exchange_wait(rs_first0, grad["w_down"], "rs_wait_first0")))
        parts, views = [parts0[n], parts1[n]], [_shard_view(n, a[n]) for a in (w, m, v)]
        if n == "w_in":
            parts = [jnp.stack(parts, axis=2).reshape(N_DEV, -1, D_MODEL)]
            views = [a.reshape(1, -1, D_MODEL) for a in views]
        outs = adamw_sharded(parts, *views, f"adamw_{n}")
        if n == "w_in":
            outs = [o.reshape(-1, DEPTH, D_MODEL) for o in outs]
        grad[n], delta[n], new_m[n], new_v[n] = [_shard_unview(n, o) for o in outs]

    return (loss, grad_x, *[grad[n] for n in WEIGHTS], *[delta[n] for n in WEIGHTS], *[new_m[n] for n in WEIGHTS],
            *[new_v[n] for n in WEIGHTS])
```

```python
import math

import jax
import jax.numpy as jnp
from jax import lax
from jax.experimental import pallas as pl
from jax.experimental.pallas import tpu as pltpu

F32 = jnp.float32
BF16 = jnp.bfloat16

N_DEV = 8
DEPTH = 2
D_MODEL = 1024
HEADS = 4
HEAD_PAD = 128
QK_NOPE = 64
QK_ROPE = 32
QK_HEAD = QK_NOPE + QK_ROPE
V_HEAD = 128
Q_LORA = 256
KV_LORA = 128
SGU_WIDTH = 256
SGU_HEAD_DIM = 64
CHUNK = 128
POOL_WIDTH = 256
POOL_HALO = 16
MLA_WIDTH = 512
IN_WIDTH = 1184
Z_WIDTH = 1280
FFN_HIDDEN = 2816
FFN_CHUNK = 256
ROPE_THETA = 10000.0
EPS = 1e-6
ATTN_SCALE = 1.0 / math.sqrt(QK_HEAD)
LOG2E = 1.4426950408889634
NEG_BIG = -1e30

ADAM_LR = 0.001
ADAM_B1 = 0.9
ADAM_B2 = 0.999
ADAM_EPS = 1e-08
ADAM_WD = 0.01
ADAM_STEP = 10

VMEM_LIMIT = 56 * 1024 * 1024
ROW_TILE = 512
MIXIN_PART = 256
MIXIN_BWD_PART = 128
ATTN_TILE = 1024
ATTN_SUB = 512
ATTN_FWD_SUB = 512
ATTN_BWD_HEADS = 2
MESH = pl.DeviceIdType.MESH

WHOLE = pl.BlockSpec(memory_space=pltpu.VMEM)
ANY = pl.BlockSpec(memory_space=pl.ANY)
HBM_SPEC = pl.BlockSpec(memory_space=pltpu.HBM)
SEM_SPEC = pl.BlockSpec(memory_space=pltpu.SEMAPHORE)


def _cparams(**kw):
    return pltpu.CompilerParams(vmem_limit_bytes=VMEM_LIMIT, **kw)


def _dot(a, b):
    return jnp.dot(a, b, preferred_element_type=F32)


def _dot_nt(a, b):
    return lax.dot_general(a, b, (((1,), (1,)), ((), ())), preferred_element_type=F32)


def _dot_tn(a, b):
    return lax.dot_general(a, b, (((0,), (0,)), ((), ())), preferred_element_type=F32)


def _rms(x, g, n):
    r = lax.rsqrt(jnp.sum(x * x, axis=-1, keepdims=True) * (1.0 / n) + EPS)
    return x * r * g, r


def _rms_bwd(x, r, g, dy, n):
    gdy = dy * g
    dx = r * gdy - x * (r * r * r) * (jnp.sum(x * gdy, axis=-1, keepdims=True) * (1.0 / n))
    dg = jnp.sum(dy * (x * r), axis=0, keepdims=True)
    return dx, dg


def _sigmoid(x):
    return 1.0 / (1.0 + jnp.exp(-x))


def rope_tables(ang):
    T = ang.shape[0]
    tm = min(ROW_TILE, T)

    def body(ang_ref, c_ref, sa_ref, sb_ref):
        a = ang_ref[...]
        lane = lax.broadcasted_iota(jnp.int32, a.shape, 1)
        s = jnp.sin(a)
        c_ref[...] = jnp.cos(a)
        sa_ref[...] = jnp.where(lane < QK_NOPE + QK_ROPE // 2, -s, 0.0)
        sb_ref[...] = jnp.where(lane >= QK_NOPE + QK_ROPE // 2, s, 0.0)

    row = pl.BlockSpec((tm, HEAD_PAD), lambda i: (i, 0))
    return pl.pallas_call(
        body, grid=(T // tm,), name="rope_tables", in_specs=[row], out_specs=[row] * 3,
        out_shape=[jax.ShapeDtypeStruct((T, HEAD_PAD), F32)] * 3, compiler_params=_cparams(),
    )(ang)


def _rope(x, c, sa, sb):
    half = QK_ROPE // 2
    return x * c + pltpu.roll(x, HEAD_PAD - half, 1) * sa + pltpu.roll(x, half, 1) * sb


def _rope_bwd(dy, c, sa, sb):
    half = QK_ROPE // 2
    return dy * c + pltpu.roll(dy * sa, half, 1) + pltpu.roll(dy * sb, HEAD_PAD - half, 1)


def _skewed(stages, groups):
    for t in range(len(stages) + len(groups) - 1):
        for p, g in enumerate(groups):
            if 0 <= t - p < len(stages):
                stages[t - p](g)


def _head_masks(shape, width):
    lane = lax.broadcasted_iota(jnp.int32, shape, len(shape) - 1)
    return [(lane >= width * h) & (lane < width * (h + 1)) for h in range(HEADS)]


def _qkv_pre(z, gql, wq, gkvl, wkv):
    ql = z[:, 0:Q_LORA]
    kvl = z[:, Q_LORA:Q_LORA + KV_LORA]
    kr = z[:, Q_LORA + KV_LORA:Q_LORA + KV_LORA + HEAD_PAD]
    qn, rq = _rms(ql, gql, Q_LORA)
    kvn, rkv = _rms(kvl, gkvl, KV_LORA)
    qn = qn.astype(BF16)
    kvn = kvn.astype(BF16)
    q_up = _dot_nt(qn, wq)
    kv_up = _dot_nt(kvn, wkv)
    return ql, kvl, kr, qn, rq, kvn, rkv, q_up, kv_up


def mixin_fwd(x, rope, gmix, win, gql, wq, gkvl, wkv, gqh, gkh):
    T = x.shape[0]
    tm = min(2 * ROW_TILE, T)
    part = min(MIXIN_PART, tm)

    def body(x_ref, c_ref, sa_ref, sb_ref, gmix_ref, win_ref, gql_ref, wq_ref, gkvl_ref, wkv_ref, gqh_ref, gkh_ref,
             z_ref, q_ref, k_ref, v_ref, kt_ref, vt_ref):
        def project(g):
            hn = _rms(x_ref[g["rows"], :], gmix_ref[...], D_MODEL)[0].astype(BF16)
            g["z"] = _dot_nt(hn, win_ref[...])
            z_ref[g["rows"], :] = g["z"]

        def latents(g):
            _, _, g["kr"], _, _, _, _, g["q_up"], g["kv_up"] = _qkv_pre(g["z"], gql_ref[...], wq_ref[...], gkvl_ref[...],
                                                                        wkv_ref[...])

        def heads(g):
            rows = g["rows"]
            c, sa, sb = c_ref[rows, :], sa_ref[rows, :], sb_ref[rows, :]
            for h in range(HEADS):
                lo = HEAD_PAD * h
                qh = _rms(g["q_up"][:, lo:lo + HEAD_PAD], gqh_ref[...], QK_HEAD)[0]
                q_ref[h, rows, :] = (_rope(qh, c, sa, sb) * (ATTN_SCALE * LOG2E)).astype(BF16)
                kh = _rope(_rms(g["kv_up"][:, lo:lo + HEAD_PAD] + g["kr"], gkh_ref[...], QK_HEAD)[0], c, sa, sb)
                k_ref[h, rows, :] = kh.astype(BF16)
                kt_ref[h, :, rows] = jnp.transpose(kh).astype(BF16)
                vh = g["kv_up"][:, HEADS * HEAD_PAD + lo:HEADS * HEAD_PAD + lo + HEAD_PAD]
                v_ref[h, rows, :] = vh.astype(BF16)
                vt_ref[h, :, rows] = jnp.transpose(vh).astype(BF16)

        _skewed((project, latents, heads), [dict(rows=pl.ds(part * p, part)) for p in range(tm // part)])

    row = lambda w: pl.BlockSpec((tm, w), lambda i: (i, 0))
    head = pl.BlockSpec((HEADS, tm, HEAD_PAD), lambda i: (0, i, 0))
    head_t = pl.BlockSpec((HEADS, HEAD_PAD, tm), lambda i: (0, 0, i))
    return pl.pallas_call(
        body, grid=(T // tm,), name="mixin_fwd",
        in_specs=[row(D_MODEL)] + [row(HEAD_PAD)] * 3 + [WHOLE] * 8,
        out_specs=[row(Z_WIDTH), head, head, head, head_t, head_t],
        out_shape=[jax.ShapeDtypeStruct((T, Z_WIDTH), F32)] + [jax.ShapeDtypeStruct((HEADS, T, HEAD_PAD), BF16)] * 3
                  + [jax.ShapeDtypeStruct((HEADS, HEAD_PAD, T), BF16)] * 2,
        compiler_params=_cparams(),
    )(x, *rope, gmix, win, gql, wq, gkvl, wkv, gqh, gkh)


def _pair_tables(pairs):
    return (jnp.asarray([p[0] for p in pairs], jnp.int32), jnp.asarray([p[1] for p in pairs], jnp.int32))


def attn_fwd(q, k, vt):
    _, T, _ = q.shape
    tb = min(ATTN_TILE, T)
    sq = min(ATTN_FWD_SUB, tb)
    nb = T // tb

    pairs = [(i, j) for i in range(nb) for j in range(i + 1)]

    def body(i_tab, j_tab, q_ref, k_ref, vt_ref, o_ref, lse_ref, m_s, l_s, acc_s):
        i, j = i_tab[pl.program_id(0)], j_tab[pl.program_id(0)]

        @pl.when(j == 0)
        def _():
            m_s[...] = jnp.full(m_s.shape, -jnp.inf, F32)
            l_s[...] = jnp.zeros(l_s.shape, F32)
            acc_s[...] = jnp.zeros(acc_s.shape, F32)

        def scores(g):
            st = _dot_nt(k_ref[g["h"], g["kb"], :], q_ref[g["h"], g["qa"], :])
            if g["masked"]:
                krow = lax.broadcasted_iota(jnp.int32, st.shape, 0)
                qcol = g["q0"] + lax.broadcasted_iota(jnp.int32, st.shape, 1)
                st = jnp.where(krow <= qcol, st, NEG_BIG)
            g["st"] = st

        def new_max(g):
            h, qa = g["h"], g["qa"]
            m_prev = m_s[h, :, qa]
            g["m"] = jnp.maximum(m_prev, jnp.max(g["st"], axis=0, keepdims=True))
            g["alpha"] = jnp.exp2(m_prev - g["m"])
            m_s[h, :, qa] = g["m"]

        def weights(g):
            pt = jnp.exp2(g["st"] - g["m"])
            g["lsum"] = jnp.sum(pt, axis=0, keepdims=True)
            g["pt"] = pt.astype(BF16)

        def accumulate(g):
            h, qa = g["h"], g["qa"]
            l_s[h, :, qa] = g["alpha"] * l_s[h, :, qa] + g["lsum"]
            acc_s[h, :, qa] = g["alpha"] * acc_s[h, :, qa] + _dot(vt_ref[h, :, g["kb"]], g["pt"])

        def tiles(masked):
            return [dict(h=h, q0=q0, qa=slice(q0, q0 + sq), kb=slice(0, q0 + sq if masked else tb), masked=masked)
                    for h in range(HEADS) for q0 in range(0, tb, sq)]

        @pl.when(j < i)
        def _():
            _skewed((scores, new_max, weights, accumulate), tiles(False))

        @pl.when(j == i)
        def _():
            _skewed((scores, new_max, weights, accumulate), tiles(True))
            for h in range(HEADS):
                l = l_s[h]
                o_ref[:, HEAD_PAD * h:HEAD_PAD * (h + 1)] = jnp.transpose(acc_s[h] / l)
                lse_ref[h] = m_s[h] + jnp.log2(l)

    qspec = pl.BlockSpec((HEADS, tb, HEAD_PAD), lambda p, it, jt: (0, it[p], 0))
    kspec = pl.BlockSpec((HEADS, tb, HEAD_PAD), lambda p, it, jt: (0, jt[p], 0))
    vspec = pl.BlockSpec((HEADS, HEAD_PAD, tb), lambda p, it, jt: (0, 0, jt[p]))
    grid_spec = pltpu.PrefetchScalarGridSpec(
        num_scalar_prefetch=2, grid=(len(pairs),),
        in_specs=[qspec, kspec, vspec],
        out_specs=[pl.BlockSpec((tb, MLA_WIDTH), lambda p, it, jt: (it[p], 0)),
                   pl.BlockSpec((HEADS, 1, tb), lambda p, it, jt: (0, 0, it[p]))],
        scratch_shapes=[pltpu.VMEM((HEADS, 1, tb), F32), pltpu.VMEM((HEADS, 1, tb), F32), pltpu.VMEM((HEADS, HEAD_PAD, tb), F32)])
    return pl.pallas_call(
        body, grid_spec=grid_spec, name="attn_fwd",
        out_shape=[jax.ShapeDtypeStruct((T, MLA_WIDTH), F32), jax.ShapeDtypeStruct((HEADS, 1, T), F32)],
        compiler_params=_cparams(),
    )(*_pair_tables(pairs), q, k, vt)


def _sgu_fwd_chunk(vn_c, wcat, bz, masks):
    vstack = jnp.concatenate([jnp.where(mk, vn_c, 0.0).astype(BF16) for mk in masks], axis=0)
    return _dot(wcat, vstack) + bz


def _pool_counts(i, tm):
    pos1 = (i * tm + 1 + lax.broadcasted_iota(jnp.int32, (tm, POOL_WIDTH), 0)).astype(F32)
    lane = lax.broadcasted_iota(jnp.int32, (tm, POOL_WIDTH), 1)
    win = jnp.where(lane < 64, 2.0, jnp.where(lane < 128, 4.0, jnp.where(lane < 192, 8.0, 16.0)))
    return jnp.minimum(pos1, win), lane


def _by_group(lane, s2, s4, s8, s16):
    return jnp.where(lane < 64, s2, jnp.where(lane < 128, s4, jnp.where(lane < 192, s8, s16)))


def _pool_means(pin, halo, i, tm):
    s1 = jnp.concatenate([halo, pin], axis=0)
    s2 = s1 + pltpu.roll(s1, 1, 0)
    s4 = s2 + pltpu.roll(s2, 2, 0)
    s8 = s4 + pltpu.roll(s4, 4, 0)
    s16 = s8 + pltpu.roll(s8, 8, 0)
    cnt, lane = _pool_counts(i, tm)
    sel = _by_group(lane, s2[POOL_HALO:], s4[POOL_HALO:], s8[POOL_HALO:], s16[POOL_HALO:])
    return sel / cnt - pin


def _mixers_fwd_tile(i, tm, a, u, vs, pin, halo, gsgu, wcat, bz, wp, pscale, goa, gos, gop):
    vn, rv = _rms(vs, gsgu, SGU_WIDTH)
    masks = _head_masks((CHUNK, SGU_WIDTH), SGU_HEAD_DIM)
    zc = jnp.concatenate([_sgu_fwd_chunk(vn[CHUNK * c:CHUNK * (c + 1)], wcat, bz, masks) for c in range(tm // CHUNK)], axis=0)
    gm = u * zc
    halo = jnp.where(i > 0, halo, 0.0)
    m = _pool_means(pin, halo, i, tm).astype(BF16)
    yp_pre = _dot(m, wp)
    yp = yp_pre * pscale
    na, ra = _rms(a, goa, MLA_WIDTH)
    ng, rg = _rms(gm, gos, SGU_WIDTH)
    npo, rp = _rms(yp, gop, POOL_WIDTH)
    mix = jnp.concatenate([na, ng, npo], axis=1).astype(BF16)
    return vn, rv, zc, gm, m, yp_pre, yp, ra, rg, rp, mix


def _z_specs(tm):
    col = lambda c: pl.BlockSpec((tm, 256), lambda i: (i, c))
    halo = pl.BlockSpec((POOL_HALO, 256), lambda i: (jnp.maximum(i * (tm // POOL_HALO) - 1, 0), 4))
    return [col(2), col(3), col(4), halo]


def mixers_fwd(x, a, z, gsgu, wcat, bz, wp, pscale, goa, gos, gop, wout):
    T = x.shape[0]
    tm = min(ROW_TILE, T)

    def body(x_ref, a_ref, u_ref, vs_ref, pin_ref, halo_ref, gsgu_ref, wcat_ref, bz_ref, wp_ref, ps_ref,
             goa_ref, gos_ref, gop_ref, wout_ref, x1_ref):
        i = pl.program_id(0)
        mix = _mixers_fwd_tile(i, tm, a_ref[...], u_ref[...], vs_ref[...], pin_ref[...], halo_ref[...], gsgu_ref[...],
                               wcat_ref[...], bz_ref[...], wp_ref[...], ps_ref[...], goa_ref[...], gos_ref[...],
                               gop_ref[...])[-1]
        x1_ref[...] = x_ref[...] + _dot(mix, wout_ref[...])

    row = lambda w: pl.BlockSpec((tm, w), lambda i: (i, 0))
    return pl.pallas_call(
        body, grid=(T // tm,), name="mixers_fwd",
        in_specs=[row(D_MODEL), row(MLA_WIDTH)] + _z_specs(tm) + [WHOLE] * 9,
        out_specs=row(D_MODEL),
        out_shape=jax.ShapeDtypeStruct((T, D_MODEL), F32),
        compiler_params=_cparams(),
    )(x, a, z, z, z, z, gsgu, wcat, bz, wp, pscale, goa, gos, gop, wout)


def ffn_fwd(x1, gffn, wg, wu, wd, target=None):
    T = x1.shape[0]
    tm = min(ROW_TILE, T)
    with_loss = target is not None

    def body(*refs):
        x1_ref, gffn_ref, wg_ref, wu_ref, wd_ref = refs[:5]
        outs = refs[6:] if with_loss else refs[5:]
        x1v = x1_ref[...]
        h2 = _rms(x1v, gffn_ref[...], D_MODEL)[0].astype(BF16)
        acc = x1v
        for c in range(FFN_HIDDEN // FFN_CHUNK):
            sl = slice(FFN_CHUNK * c, FFN_CHUNK * (c + 1))
            g = _dot_nt(h2, wg_ref[sl, :])
            u = _dot_nt(h2, wu_ref[sl, :])
            outs[-2][:, sl] = g.astype(BF16)
            outs[-1][:, sl] = u.astype(BF16)
            act = (g * _sigmoid(g) * u).astype(BF16)
            acc = acc + _dot(act, wd_ref[sl, :])
        if not with_loss:
            outs[0][...] = acc
            return
        dy_ref, loss_ref = outs[0], outs[1]

        @pl.when(pl.program_id(0) == 0)
        def _():
            loss_ref[...] = jnp.zeros(loss_ref.shape, F32)

        err = acc - refs[5][...]
        dy_ref[...] = err * (1.0 / D_MODEL)
        per_row = jnp.sum(err * err, axis=1, keepdims=True) * (1.0 / D_MODEL)
        loss_ref[...] += 0.5 * jnp.sum(per_row, axis=0, keepdims=True)

    row = lambda w: pl.BlockSpec((tm, w), lambda i: (i, 0))
    hidden = [jax.ShapeDtypeStruct((T, FFN_HIDDEN), BF16)] * 2
    if with_loss:
        return pl.pallas_call(
            body, grid=(T // tm,), name="ffn_fwd_loss",
            in_specs=[row(D_MODEL)] + [WHOLE] * 4 + [row(D_MODEL)],
            out_specs=[row(D_MODEL), pl.BlockSpec((1, 1), lambda i: (0, 0)), row(FFN_HIDDEN), row(FFN_HIDDEN)],
            out_shape=[jax.ShapeDtypeStruct((T, D_MODEL), F32), jax.ShapeDtypeStruct((1, 1), F32)] + hidden,
            compiler_params=_cparams(),
        )(x1, gffn, wg, wu, wd, target)
    return pl.pallas_call(
        body, grid=(T // tm,), name="ffn_fwd",
        in_specs=[row(D_MODEL)] + [WHOLE] * 4,
        out_specs=[row(D_MODEL), row(FFN_HIDDEN), row(FFN_HIDDEN)],
        out_shape=[jax.ShapeDtypeStruct((T, D_MODEL), F32)] + hidden,
        compiler_params=_cparams(),
    )(x1, gffn, wg, wu, wd)


def _acc_spec(shape):
    return pl.BlockSpec(shape, lambda i: (0,) * len(shape))


def ffn_bwd(dx2, x1, gs, us, gffn, wg, wu, wd):
    T = x1.shape[0]
    tm = min(ROW_TILE // 2, T)

    def body(dx2_ref, x1_ref, gs_ref, us_ref, gffn_ref, wg_ref, wu_ref, wd_ref,
             dx1_ref, h2_ref, act_ref, dg_ref, du_ref, dgffn_ref):
        @pl.when(pl.program_id(0) == 0)
        def _():
            dgffn_ref[...] = jnp.zeros(dgffn_ref.shape, F32)

        dx2v = dx2_ref[...]
        dy = dx2v.astype(BF16)
        x1v = x1_ref[...]
        h2, r = _rms(x1v, gffn_ref[...], D_MODEL)
        h2_ref[...] = h2.astype(BF16)
        for c in range(FFN_HIDDEN // FFN_CHUNK):
            sl = slice(FFN_CHUNK * c, FFN_CHUNK * (c + 1))
            g = gs_ref[:, sl].astype(F32)
            u = us_ref[:, sl].astype(F32)
            dact = _dot_nt(dy, wd_ref[sl, :])
            sg = _sigmoid(g)
            silu = g * sg
            act_ref[:, sl] = (silu * u).astype(BF16)
            dg_ref[:, sl] = (dact * u * (sg * (1.0 + g * (1.0 - sg)))).astype(BF16)
            du_ref[:, sl] = (dact * silu).astype(BF16)
        dh2 = _dot(dg_ref[...], wg_ref[...]) + _dot(du_ref[...], wu_ref[...])
        dxn, dgn = _rms_bwd(x1v, r, gffn_ref[...], dh2, D_MODEL)
        dx1_ref[...] = dx2v + dxn
        dgffn_ref[...] += dgn

    row = lambda w: pl.BlockSpec((tm, w), lambda i: (i, 0))
    return pl.pallas_call(
        body, grid=(T // tm,), name="ffn_bwd",
        in_specs=[row(D_MODEL), row(D_MODEL), row(FFN_HIDDEN), row(FFN_HIDDEN)] + [WHOLE] * 4,
        out_specs=[row(D_MODEL), row(D_MODEL), row(FFN_HIDDEN), row(FFN_HIDDEN), row(FFN_HIDDEN), _acc_spec((1, D_MODEL))],
        out_shape=[jax.ShapeDtypeStruct((T, D_MODEL), F32), jax.ShapeDtypeStruct((T, D_MODEL), BF16),
                   jax.ShapeDtypeStruct((T, FFN_HIDDEN), BF16), jax.ShapeDtypeStruct((T, FFN_HIDDEN), BF16),
                   jax.ShapeDtypeStruct((T, FFN_HIDDEN), BF16), jax.ShapeDtypeStruct((1, D_MODEL), F32)],
        compiler_params=_cparams(),
    )(dx2, x1, gs, us, gffn, wg, wu, wd)


TN_K_TILE = 2048
TN_ACC_BYTES = 6 * 1024 * 1024


def matmul_tn(a, b, name):
    T, M = a.shape
    N = b.shape[1]
    tk = min(TN_K_TILE, T)
    tm = M if M <= 1024 else M // 2
    tn = max(d for d in range(128, N + 1, 128) if N % d == 0 and tm * d * 4 <= TN_ACC_BYTES)
    nk = T // tk

    def body(a_ref, b_ref, o_ref, acc):
        k = pl.program_id(2)

        @pl.when(k == 0)
        def _():
            acc[...] = jnp.zeros(acc.shape, F32)

        acc[...] += _dot_tn(a_ref[...].astype(BF16), b_ref[...].astype(BF16))

        @pl.when(k == nk - 1)
        def _():
            o_ref[...] = acc[...].astype(BF16)

    return pl.pallas_call(
        body, grid=(M // tm, N // tn, nk), name=name,
        in_specs=[pl.BlockSpec((tk, tm), lambda i, j, k: (k, i)), pl.BlockSpec((tk, tn), lambda i, j, k: (k, j))],
        out_specs=pl.BlockSpec((tm, tn), lambda i, j, k: (i, j)),
        out_shape=jax.ShapeDtypeStruct((M, N), BF16),
        scratch_shapes=[pltpu.VMEM((tm, tn), F32)],
        compiler_params=_cparams(),
    )(a, b)


def mixers_bwd(dx1, a, z, gsgu, wcat, wcat_t, bz, wp, pscale, goa, gos, gop, wout):
    T = a.shape[0]
    tm = min(ROW_TILE, T)

    def body(dx1_ref, a_ref, u_ref, vs_ref, pin_ref, halo_ref, gsgu_ref, wcat_ref, wcatt_ref, bz_ref, wp_ref,
             ps_ref, goa_ref, gos_ref, gop_ref, wout_ref,
             da_ref, delta_ref, du_ref, dvs_ref, dm_ref, mix_ref,
             dgoa_ref, dgos_ref, dgop_ref, dps_ref, dwp_ref, dwsp_ref, db_ref, dgsgu_ref):
        i = pl.program_id(0)

        @pl.when(i == 0)
        def _():
            for r in (dgoa_ref, dgos_ref, dgop_ref, dps_ref, dwp_ref, dwsp_ref, db_ref, dgsgu_ref):
                r[...] = jnp.zeros(r.shape, F32)

        a_v, u, vs = a_ref[...], u_ref[...], vs_ref[...]
        goa, gos, gop, pscale_v = goa_ref[...], gos_ref[...], gop_ref[...], ps_ref[...]
        vn, rv, zc, gm, m, yp_pre, yp, ra, rg, rp, mix = _mixers_fwd_tile(
            i, tm, a_v, u, vs, pin_ref[...], halo_ref[...], gsgu_ref[...], wcat_ref[...], bz_ref[...], wp_ref[...],
            pscale_v, goa, gos, gop)
        mix_ref[...] = mix
        dmix = _dot_nt(dx1_ref[...].astype(BF16), wout_ref[...])
        da, dgoa = _rms_bwd(a_v, ra, goa, dmix[:, :MLA_WIDTH], MLA_WIDTH)
        dgm, dgos = _rms_bwd(gm, rg, gos, dmix[:, MLA_WIDTH:MLA_WIDTH + SGU_WIDTH], SGU_WIDTH)
        dyp, dgop = _rms_bwd(yp, rp, gop, dmix[:, MLA_WIDTH + SGU_WIDTH:], POOL_WIDTH)
        da_ref[...] = da
        dgoa_ref[...] += dgoa
        dgos_ref[...] += dgos
        dgop_ref[...] += dgop
        prod = da * a_v
        ones = jnp.ones((8, HEAD_PAD), F32)
        for h in range(HEADS):
            lo = HEAD_PAD * h
            sums = lax.dot_general(ones, prod[:, lo:lo + HEAD_PAD], (((1,), (1,)), ((), ())), preferred_element_type=F32,
                                   precision=lax.Precision.HIGHEST)
            delta_ref[h] = sums[0:1, :]
        dps_ref[...] += jnp.sum(dyp * yp_pre, axis=0, keepdims=True)
        dyp_pre = (dyp * pscale_v).astype(BF16)
        dwp_ref[...] += _dot_tn(m, dyp_pre)
        dm_ref[...] = _dot_nt(dyp_pre, wp_ref[...])
        du_ref[...] = dgm * zc
        dzc = dgm * u
        masks = _head_masks((CHUNK, SGU_WIDTH), SGU_HEAD_DIM)
        lane_b = lax.broadcasted_iota(jnp.int32, (CHUNK, HEAD_PAD), 1)
        dvn_parts = []
        dwsp = jnp.zeros(dwsp_ref.shape, F32)
        db = jnp.zeros(db_ref.shape, F32)
        for c in range(tm // CHUNK):
            dz_c = dzc[CHUNK * c:CHUNK * (c + 1)]
            dzstack = jnp.concatenate([jnp.where(mk, dz_c, 0.0).astype(BF16) for mk in masks], axis=0)
            dvn_parts.append(_dot(wcatt_ref[...], dzstack))
            dwsp = dwsp + _dot_nt(dzstack, vn[CHUNK * c:CHUNK * (c + 1)].astype(BF16))
            for h, mk in enumerate(masks):
                col = jnp.sum(jnp.where(mk, dz_c, 0.0), axis=1, keepdims=True)
                db = db + jnp.where(lane_b == h, col, 0.0)
        dwsp_ref[...] += dwsp
        db_ref[...] += db
        dvs, dgsgu = _rms_bwd(vs, rv, gsgu_ref[...], jnp.concatenate(dvn_parts, axis=0), SGU_WIDTH)
        dvs_ref[...] = dvs
        dgsgu_ref[...] += dgsgu

    row = lambda w: pl.BlockSpec((tm, w), lambda i: (i, 0))
    head = pl.BlockSpec((HEADS, 1, tm), lambda i: (0, 0, i))
    acc_shapes = [(1, MLA_WIDTH), (1, SGU_WIDTH), (1, POOL_WIDTH), (1, POOL_WIDTH), (POOL_WIDTH, POOL_WIDTH),
                  (HEADS * CHUNK, CHUNK), (CHUNK, HEAD_PAD), (1, SGU_WIDTH)]
    return pl.pallas_call(
        body, grid=(T // tm,), name="mixers_bwd",
        in_specs=[row(D_MODEL), row(MLA_WIDTH)] + _z_specs(tm) + [WHOLE] * 10,
        out_specs=[row(MLA_WIDTH), head, row(256), row(256), row(256), row(D_MODEL)] + [_acc_spec(s) for s in acc_shapes],
        out_shape=[jax.ShapeDtypeStruct((T, MLA_WIDTH), F32), jax.ShapeDtypeStruct((HEADS, 1, T), F32),
                   jax.ShapeDtypeStruct((T, 256), F32), jax.ShapeDtypeStruct((T, 256), F32),
                   jax.ShapeDtypeStruct((T, 256), F32), jax.ShapeDtypeStruct((T, D_MODEL), BF16)]
                  + [jax.ShapeDtypeStruct(s, F32) for s in acc_shapes],
        compiler_params=_cparams(),
    )(dx1, a, z, z, z, z, gsgu, wcat, wcat_t, bz, wp, pscale, goa, gos, gop, wout)


def pool_bwd(dm):
    T = dm.shape[0]
    tm = min(4 * ROW_TILE, T)
    nt = T // tm

    def body(dm_ref, next_ref, dpin_ref):
        i = pl.program_id(0)
        cnt, lane = _pool_counts(i, tm)
        dmv = dm_ref[...]
        win = _by_group(lane[:POOL_HALO], 2.0, 4.0, 8.0, 16.0)
        nxt = jnp.where(i < nt - 1, next_ref[...] / win, 0.0)
        r1 = jnp.concatenate([dmv / cnt, nxt], axis=0)
        n = tm + POOL_HALO
        r2 = r1 + pltpu.roll(r1, n - 1, 0)
        r4 = r2 + pltpu.roll(r2, n - 2, 0)
        r8 = r4 + pltpu.roll(r4, n - 4, 0)
        r16 = r8 + pltpu.roll(r8, n - 8, 0)
        dpin_ref[...] = _by_group(lane, r2[:tm], r4[:tm], r8[:tm], r16[:tm]) - dmv

    return pl.pallas_call(
        body, grid=(nt,), name="pool_bwd",
        in_specs=[pl.BlockSpec((tm, 256), lambda i: (i, 0)),
                  pl.BlockSpec((POOL_HALO, 256), lambda i: (jnp.minimum((i + 1) * (tm // POOL_HALO), T // POOL_HALO - 1), 0))],
        out_specs=pl.BlockSpec((tm, 256), lambda i: (i, 0)),
        out_shape=jax.ShapeDtypeStruct((T, 256), F32),
        compiler_params=_cparams(),
    )(dm, dm)


def attn_bwd(q, k, kt, v, do, lse, delta):
    _, T, _ = q.shape
    tb = min(ATTN_TILE, T)
    sb = min(ATTN_SUB, tb)
    ns = tb // sb
    nb = T // tb
    hb = ATTN_BWD_HEADS

    pairs = [(i, j) for i in range(nb) for j in range(i, nb)]

    def body(i_tab, j_tab, q_ref, k_ref, kt_ref, v_ref, do_ref, lse_ref, delta_ref, dqt_ref, dk_ref, dv_ref, dk_s, dv_s):
        i, j = i_tab[pl.program_id(1)], j_tab[pl.program_id(1)]

        @pl.when(pl.program_id(1) == 0)
        def _():
            dqt_ref[...] = jnp.zeros(dqt_ref.shape, F32)

        @pl.when(j == i)
        def _():
            dk_s[...] = jnp.zeros(dk_s.shape, F32)
            dv_s[...] = jnp.zeros(dv_s.shape, F32)

        def sub_block(h, a, b, masked):
            qa = slice(sb * a, sb * (a + 1))
            kb = slice(sb * b, sb * (b + 1))
            qv = q_ref[h, qa, :]
            dov = do_ref[qa, HEAD_PAD * h:HEAD_PAD * (h + 1)].astype(BF16)
            st = _dot_nt(k_ref[h, kb, :], qv)
            dpt = _dot_nt(v_ref[h, kb, :], dov)
            pt = jnp.exp2(st - lse_ref[h, :, qa])
            if masked:
                krow = lax.broadcasted_iota(jnp.int32, st.shape, 0)
                qcol = lax.broadcasted_iota(jnp.int32, st.shape, 1)
                pt = jnp.where(krow <= qcol, pt, 0.0)
            dst = (pt * (dpt - delta_ref[h, :, qa])).astype(BF16)
            dv_s[h, kb, :] += _dot(pt.astype(BF16), dov)
            dk_s[h, kb, :] += _dot(dst, qv)
            cols = pl.ds(pl.multiple_of(j * tb + sb * a, sb), sb)
            dqt_ref[h, :, cols] += _dot(kt_ref[h, :, kb], dst)

        @pl.when(j > i)
        def _():
            for a in range(ns):
                for h in range(hb):
                    for b in range(ns):
                        sub_block(h, a, b, False)

        @pl.when(j == i)
        def _():
            for a in range(ns):
                for h in range(hb):
                    for b in range(a + 1):
                        sub_block(h, a, b, a == b)

        @pl.when(j == nb - 1)
        def _():
            dk_ref[...] = dk_s[...] * (1.0 / LOG2E)
            dv_ref[...] = dv_s[...]

    qspec = pl.BlockSpec((hb, tb, HEAD_PAD), lambda g, p, it, jt: (g, jt[p], 0))
    kspec = pl.BlockSpec((hb, tb, HEAD_PAD), lambda g, p, it, jt: (g, it[p], 0))
    ktspec = pl.BlockSpec((hb, HEAD_PAD, tb), lambda g, p, it, jt: (g, 0, it[p]))
    rowspec = pl.BlockSpec((hb, 1, tb), lambda g, p, it, jt: (g, 0, jt[p]))
    grid_spec = pltpu.PrefetchScalarGridSpec(
        num_scalar_prefetch=2, grid=(HEADS // hb, len(pairs)),
        in_specs=[qspec, kspec, ktspec, kspec, pl.BlockSpec((tb, hb * HEAD_PAD), lambda g, p, it, jt: (jt[p], g)), rowspec, rowspec],
        out_specs=[pl.BlockSpec((hb, HEAD_PAD, T), lambda g, p, it, jt: (g, 0, 0)), kspec, kspec],
        scratch_shapes=[pltpu.VMEM((hb, tb, HEAD_PAD), F32), pltpu.VMEM((hb, tb, HEAD_PAD), F32)])
    return pl.pallas_call(
        body, grid_spec=grid_spec, name="attn_bwd",
        out_shape=[jax.ShapeDtypeStruct((HEADS, HEAD_PAD, T), F32)] + [jax.ShapeDtypeStruct((HEADS, T, HEAD_PAD), F32)] * 2,
        compiler_params=_cparams(),
    )(*_pair_tables(pairs), q, k, kt, v, do, lse, delta)


def mixin_bwd(dres, x, z, rope, dqt, dk, dv, du, dvs, dpin, gmix, win, gql, wq, gkvl, wkv, gqh, gkh):
    T = x.shape[0]
    tm = min(ROW_TILE, T)
    part = min(MIXIN_BWD_PART, tm)

    def body(dres_ref, x_ref, z_ref, c_ref, sa_ref, sb_ref, dqt_ref, dk_ref, dv_ref, du_ref, dvs_ref, dpin_ref,
             gmix_ref, win_ref, gql_ref, wq_ref, gkvl_ref, wkv_ref, gqh_ref, gkh_ref,
             dx_ref, hn_ref, dz_ref, qn_ref, dqup_ref, kvn_ref, dkvup_ref,
             dgmix_ref, dgql_ref, dgkvl_ref, dgqh_ref, dgkh_ref):
        @pl.when(pl.program_id(0) == 0)
        def _():
            for r in (dgmix_ref, dgql_ref, dgkvl_ref, dgqh_ref, dgkh_ref):
                r[...] = jnp.zeros(r.shape, F32)

        lane = lax.broadcasted_iota(jnp.int32, (part, HEAD_PAD), 1)
        rope_lanes = (lane >= QK_NOPE) & (lane < QK_HEAD)

        def recompute(g):
            rows = g["rows"]
            g["xv"] = x_ref[rows, :]
            hn, g["rx"] = _rms(g["xv"], gmix_ref[...], D_MODEL)
            hn_ref[rows, :] = hn.astype(BF16)
            g["ql"], g["kvl"], g["kr"], qn, g["rq"], kvn, g["rkv"], g["q_up"], g["kv_up"] = _qkv_pre(
                z_ref[rows, :], gql_ref[...], wq_ref[...], gkvl_ref[...], wkv_ref[...])
            qn_ref[rows, :] = qn
            kvn_ref[rows, :] = kvn

        def heads(g):
            rows = g["rows"]
            c, sa, sb = c_ref[rows, :], sa_ref[rows, :], sb_ref[rows, :]
            dkr = jnp.zeros((part, HEAD_PAD), F32)
            dgqh = jnp.zeros((1, HEAD_PAD), F32)
            dgkh = jnp.zeros((1, HEAD_PAD), F32)
            dq_parts, dk_parts, dv_parts = [], [], []
            for h in range(HEADS):
                lo = HEAD_PAD * h
                qh = g["q_up"][:, lo:lo + HEAD_PAD]
                rqh = lax.rsqrt(jnp.sum(qh * qh, axis=-1, keepdims=True) * (1.0 / QK_HEAD) + EPS)
                dq_h = jnp.transpose(dqt_ref[h, :, rows]) * ATTN_SCALE
                dqh, dg = _rms_bwd(qh, rqh, gqh_ref[...], _rope_bwd(dq_h, c, sa, sb), QK_HEAD)
                dgqh = dgqh + dg
                dq_parts.append(dqh)
                kh = g["kv_up"][:, lo:lo + HEAD_PAD] + g["kr"]
                rkh = lax.rsqrt(jnp.sum(kh * kh, axis=-1, keepdims=True) * (1.0 / QK_HEAD) + EPS)
                dkh, dg = _rms_bwd(kh, rkh, gkh_ref[...], _rope_bwd(dk_ref[h, rows, :], c, sa, sb), QK_HEAD)
                dgkh = dgkh + dg
                dkr = dkr + jnp.where(rope_lanes, dkh, 0.0)
                dk_parts.append(dkh)
                dv_parts.append(dv_ref[h, rows, :])
            dgqh_ref[...] += dgqh
            dgkh_ref[...] += dgkh
            g["dkr"] = dkr
            g["dq_up"] = jnp.concatenate(dq_parts, axis=1).astype(BF16)
            g["dkv_up"] = jnp.concatenate(dk_parts + dv_parts, axis=1).astype(BF16)
            dqup_ref[rows, :] = g["dq_up"]
            dkvup_ref[rows, :] = g["dkv_up"]

        def latents(g):
            rows = g["rows"]
            dql, dg = _rms_bwd(g["ql"], g["rq"], gql_ref[...], _dot(g["dq_up"], wq_ref[...]), Q_LORA)
            dgql_ref[...] += dg
            dkvl, dg = _rms_bwd(g["kvl"], g["rkv"], gkvl_ref[...], _dot(g["dkv_up"], wkv_ref[...]), KV_LORA)
            dgkvl_ref[...] += dg
            g["dz"] = jnp.concatenate([dql, dkvl, g["dkr"], du_ref[rows, :], dvs_ref[rows, :], dpin_ref[rows, :]],
                                      axis=1).astype(BF16)
            dz_ref[rows, :] = g["dz"]

        def inputs(g):
            rows = g["rows"]
            dxn, dg = _rms_bwd(g["xv"], g["rx"], gmix_ref[...], _dot(g["dz"], win_ref[...]), D_MODEL)
            dgmix_ref[...] += dg
            dx_ref[rows, :] = dres_ref[rows, :] + dxn

        _skewed((recompute, heads, latents, inputs), [dict(rows=pl.ds(part * p, part)) for p in range(tm // part)])

    row = lambda w: pl.BlockSpec((tm, w), lambda i: (i, 0))
    head = pl.BlockSpec((HEADS, tm, HEAD_PAD), lambda i: (0, i, 0))
    head_t = pl.BlockSpec((HEADS, HEAD_PAD, tm), lambda i: (0, 0, i))
    acc_shapes = [(1, D_MODEL), (1, Q_LORA), (1, KV_LORA), (1, HEAD_PAD), (1, HEAD_PAD)]
    out_rows = [(D_MODEL, F32), (D_MODEL, BF16), (Z_WIDTH, BF16), (Q_LORA, BF16), (HEADS * HEAD_PAD, BF16),
                (KV_LORA, BF16), (2 * HEADS * HEAD_PAD, BF16)]
    return pl.pallas_call(
        body, grid=(T // tm,), name="mixin_bwd",
        in_specs=[row(D_MODEL), row(D_MODEL), row(Z_WIDTH)] + [row(HEAD_PAD)] * 3 + [head_t, head, head, row(256), row(256), row(256)]
                 + [WHOLE] * 8,
        out_specs=[row(w) for w, _ in out_rows] + [_acc_spec(s) for s in acc_shapes],
        out_shape=[jax.ShapeDtypeStruct((T, w), dt) for w, dt in out_rows] + [jax.ShapeDtypeStruct(s, F32) for s in acc_shapes],
        compiler_params=_cparams(),
    )(dres, x, z, *rope, dqt, dk, dv, du, dvs, dpin, gmix, win, gql, wq, gkvl, wkv, gqh, gkh)


def _place():
    x, y, c = lax.axis_index("x"), lax.axis_index("y"), lax.axis_index("c")
    return x, y, c, 4 * x + 2 * y + c


def _layer_of(ref, l):
    return ref[:, l, :] if ref.shape[1] == DEPTH else ref[l]


def _layer_shape(shard):
    return (shard.shape[0], shard.shape[2]) if shard.shape[1] == DEPTH else shard.shape[1:]


def cast_shards(shards, wanted):
    n = len(shards)

    def body(*refs):
        for o_ref, (w, l) in zip(refs[n:], wanted):
            o_ref[...] = _layer_of(refs[w], l).astype(BF16)

    return pl.pallas_call(
        body, name="cast_shards", in_specs=[WHOLE] * n, out_specs=[WHOLE] * len(wanted),
        out_shape=[jax.ShapeDtypeStruct(_layer_shape(shards[w]), BF16) for w, _ in wanted],
        compiler_params=_cparams(),
    )(*shards)


def _peer(k):
    x, y, c, _ = _place()
    px = 1 - x if k & 4 else x
    py = 1 - y if k & 2 else y
    pc = 1 - c if k & 1 else c
    return (px, py, pc), 4 * px + 2 * py + pc


def exchange_start(srcs, after, gather, name):
    n = len(srcs)
    land_shapes = [((N_DEV,) + s.shape) if gather else s.shape for s in srcs]

    def body(*refs):
        src_refs, land_refs = refs[:n], refs[n:2 * n]
        send_sems, recv_sems = refs[2 * n + 1:3 * n + 1], refs[3 * n + 1:4 * n + 1]
        token = refs[-1]
        _, _, _, me = _place()
        for k in range(1, N_DEV):
            peer, peer_id = _peer(k)
            for w in range(n):
                pltpu.make_async_remote_copy(
                    src_ref=src_refs[w] if gather else src_refs[w].at[peer_id], dst_ref=land_refs[w].at[me],
                    send_sem=send_sems[w], recv_sem=recv_sems[w], device_id=peer, device_id_type=MESH).start()
        for w in range(n):
            pltpu.make_async_copy(src_refs[w] if gather else src_refs[w].at[me], land_refs[w].at[me], recv_sems[w]).start()
        token[...] = jnp.zeros(token.shape, F32)

    hbm = lambda a: pltpu.with_memory_space_constraint(a, pltpu.HBM)
    outs = pl.pallas_call(
        body, name=name,
        out_shape=(pltpu.SemaphoreType.DMA(()),) * (2 * n)
                  + tuple(pltpu.HBM(s.shape, BF16) for s in srcs) + tuple(pltpu.HBM(s, BF16) for s in land_shapes)
                  + (jax.ShapeDtypeStruct((8, 128), F32),),
        in_specs=[HBM_SPEC] * (2 * n) + [ANY],
        out_specs=(SEM_SPEC,) * (2 * n) + (HBM_SPEC,) * (2 * n) + (WHOLE,),
        input_output_aliases={i: 2 * n + i for i in range(2 * n)},
        compiler_params=pltpu.CompilerParams(has_side_effects=pltpu.SideEffectType.DATAFLOW_SIDE_EFFECTING),
    )(*[hbm(s) for s in srcs], *[hbm(lax.empty(s, BF16)) for s in land_shapes], after)
    return list(outs[:n]), list(outs[n:2 * n]), list(outs[2 * n:3 * n]), list(outs[3 * n:4 * n]), outs[-1]


def exchange_wait(started, after, name):
    send_sems, recv_sems, srcs, lands, _ = started
    n = len(srcs)

    def body(*refs):
        land_refs = refs[n:2 * n]
        send_sems, recv_sems = refs[2 * n:3 * n], refs[3 * n:4 * n]
        x, y, c, _ = _place()
        for w in range(n):
            seven, eight = land_refs[w].at[pl.ds(0, N_DEV - 1)], land_refs[w]
            pltpu.make_async_remote_copy(src_ref=seven, dst_ref=seven, send_sem=send_sems[w], recv_sem=recv_sems[w],
                                         device_id=(x, y, c), device_id_type=MESH).wait_send()
            pltpu.make_async_remote_copy(src_ref=eight, dst_ref=eight, send_sem=send_sems[w], recv_sem=recv_sems[w],
                                         device_id=(x, y, c), device_id_type=MESH).wait_recv()

    outs = pl.pallas_call(
        body, name=name,
        out_shape=tuple(pltpu.HBM(s.shape, BF16) for s in srcs) + tuple(pltpu.HBM(l.shape, BF16) for l in lands),
        in_specs=[HBM_SPEC] * (2 * n) + [SEM_SPEC] * (2 * n) + [ANY],
        out_specs=(HBM_SPEC,) * (2 * n),
        input_output_aliases={i: i for i in range(2 * n)},
        compiler_params=pltpu.CompilerParams(has_side_effects=pltpu.SideEffectType.DATAFLOW_SIDE_EFFECTING),
    )(*srcs, *lands, *send_sems, *recv_sems, after)
    return list(outs[n:])


def _adamw(w, g, m, v):
    m2 = ADAM_B1 * m + (1.0 - ADAM_B1) * g
    v2 = ADAM_B2 * v + (1.0 - ADAM_B2) * (g * g)
    m_hat = m2 / (1.0 - ADAM_B1 ** ADAM_STEP)
    v_hat = v2 / (1.0 - ADAM_B2 ** ADAM_STEP)
    delta = -ADAM_LR * (m_hat / (jnp.sqrt(v_hat) + ADAM_EPS) + ADAM_WD * w)
    return delta, m2, v2


def adamw_sharded(parts, w, m, v, name):
    L, R, C = w.shape
    fits = [d for d in range(16, min(R, 512) + 1, 16) if R % d == 0]
    br = max(fits) if fits else R
    nblk = R // br

    def body(*refs):
        p_refs = refs[:L]
        w_ref, m_ref, v_ref, g_ref, d_ref, m2_ref, v2_ref = refs[L:]

        def total(p_ref):
            g = p_ref[0].astype(F32)
            for s in range(1, N_DEV):
                g = g + p_ref[s].astype(F32)
            return g

        g = total(p_refs[0])
        for l in range(1, L):
            g = jnp.where(pl.program_id(0) == l, total(p_refs[l]), g)
        g_ref[...] = g
        d_ref[...], m2_ref[...], v2_ref[...] = _adamw(w_ref[...], g, m_ref[...], v_ref[...])

    blk = pl.BlockSpec((None, br, C), lambda l, i: (l, i, 0))

    def part_spec(k):
        return pl.BlockSpec((N_DEV, br, C), lambda l, i: (0, jnp.where(l == k, i, jnp.where(l < k, 0, nblk - 1)), 0))

    return pl.pallas_call(
        body, grid=(L, nblk), name=name,
        in_specs=[part_spec(k) for k in range(L)] + [blk, blk, blk],
        out_specs=[blk] * 4,
        out_shape=[jax.ShapeDtypeStruct((L, R, C), F32)] * 4,
        compiler_params=_cparams(),
    )(*parts, w, m, v)


def allreduce_adamw_small(gpart, w, m, v):
    R = gpart.shape[0]

    def body(g_ref, w_ref, m_ref, v_ref, grad_ref, d_ref, m2_ref, v2_ref, all_ref, send_sems, recv_sems):
        x, y, c, me = _place()
        sibling = (x, y, 1 - c)
        chips = [(1 - x, y), (x, 1 - y), (1 - x, 1 - y)]

        def copy(k, block_id, to, from_input):
            return pltpu.make_async_remote_copy(
                src_ref=g_ref if from_input else all_ref.at[block_id], dst_ref=all_ref.at[block_id],
                send_sem=send_sems.at[k], recv_sem=recv_sems.at[k], device_id=to, device_id_type=MESH)

        def block_of(cx, cy, cc):
            return 4 * cx + 2 * cy + cc

        all_ref[me] = g_ref[...]
        first = [copy(0, me, sibling, True)] + [copy(1 + j, me, (*chip, c), True) for j, chip in enumerate(chips)]
        for cp in first:
            cp.start()
        passed = [copy(4 + j, block_of(*chip, c), sibling, False) for j, chip in enumerate(chips)]
        for j, chip in enumerate(chips):
            copy(1 + j, block_of(*chip, c), (x, y, c), False).wait_recv()
            passed[j].start()
        copy(0, block_of(x, y, 1 - c), (x, y, c), False).wait_recv()
        for j, chip in enumerate(chips):
            copy(4 + j, block_of(*chip, 1 - c), (x, y, c), False).wait_recv()
        for cp in first + passed:
            cp.wait_send()
        g = all_ref[0]
        for s in range(1, N_DEV):
            g = g + all_ref[s]
        grad_ref[...] = g
        d_ref[...], m2_ref[...], v2_ref[...] = _adamw(w_ref[...], g, m_ref[...], v_ref[...])

    return pl.pallas_call(
        body, name="allreduce_adamw_small",
        in_specs=[WHOLE] * 4, out_specs=[WHOLE] * 4,
        out_shape=[jax.ShapeDtypeStruct((R, 128), F32)] * 4,
        scratch_shapes=[pltpu.VMEM((N_DEV, R, 128), F32), pltpu.SemaphoreType.DMA((7,)), pltpu.SemaphoreType.DMA((7,))],
        compiler_params=_cparams(),
    )(gpart, w, m, v)


def _shard_view(name, a):
    if name == "w_in":
        return a.transpose(2, 0, 1)
    return a.swapaxes(1, 2) if name in TRANSPOSED else a


def _shard_unview(name, a):
    if name == "w_in":
        return a.transpose(1, 2, 0)
    return a.swapaxes(1, 2) if name in TRANSPOSED else a


def _pad_head_rows(w, width):
    c = w.shape[1]
    return jnp.pad(w.reshape(HEADS, width, c), ((0, 0), (0, HEAD_PAD - width), (0, 0))).reshape(HEADS * HEAD_PAD, c)


def _unpad_head_rows(w, width):
    c = w.shape[1]
    return w.reshape(HEADS, HEAD_PAD, c)[:, :width].reshape(HEADS * width, c)


O1 = Q_LORA
O2 = O1 + KV_LORA
O3 = O2 + QK_ROPE


def _mixer_weights(gw):
    w_in = gw["w_in"].reshape(IN_WIDTH, D_MODEL)
    zero = lambda n: jnp.zeros((n, D_MODEL), BF16)
    win = jnp.concatenate([w_in[:O2], zero(QK_NOPE), w_in[O2:O3], zero(HEAD_PAD - QK_HEAD), w_in[O3:]], axis=0)
    wq = _pad_head_rows(gw["w_q_up"].reshape(HEADS * QK_HEAD, Q_LORA), QK_HEAD)
    w_kv = gw["w_kv_up"].reshape(HEADS, QK_NOPE + V_HEAD, KV_LORA)
    wk = jnp.pad(w_kv[:, :QK_NOPE], ((0, 0), (0, HEAD_PAD - QK_NOPE), (0, 0))).reshape(HEADS * HEAD_PAD, KV_LORA)
    wv = w_kv[:, QK_NOPE:].reshape(HEADS * V_HEAD, KV_LORA)
    wkv = jnp.concatenate([wk, wv], axis=0)
    out = dict(win=win, wq=wq, wkv=wkv)
    if "w_out" in gw:
        out["wout"] = gw["w_out"].reshape(D_MODEL, D_MODEL)
    return out


def _ffn_weights(gw):
    return dict(wg=gw["w_gate"].reshape(FFN_HIDDEN, D_MODEL), wu=gw["w_up"].reshape(FFN_HIDDEN, D_MODEL),
                wd=gw["w_down"].reshape(FFN_HIDDEN, D_MODEL))


def _layer_small(p, l):
    row = lambda a: a.reshape(1, -1)
    pad_head = lambda g: jnp.pad(g, (0, HEAD_PAD - QK_HEAD)).reshape(1, HEAD_PAD)
    tril = jnp.tril(jnp.ones((CHUNK, CHUNK), F32))
    wsp = p["w_spatial"][l] * tril
    wcat = jnp.concatenate([wsp[h] for h in range(HEADS)], axis=1).astype(BF16)
    wcat_t = jnp.concatenate([wsp[h].T for h in range(HEADS)], axis=1).astype(BF16)
    bz = jnp.repeat(p["b_spatial"][l].T, SGU_HEAD_DIM, axis=1)
    wp = jax.scipy.linalg.block_diag(*[p["w_pool"][l][g] for g in range(HEADS)]).astype(BF16)
    return dict(gmix=row(p["g_mix_norm"][l]), gql=row(p["g_q_lat"][l]), gkvl=row(p["g_kv_lat"][l]),
                gqh=pad_head(p["g_q_head"][l]), gkh=pad_head(p["g_k_head"][l]), gsgu=row(p["g_sgu_v"][l]),
                wcat=wcat, wcat_t=wcat_t, bz=bz, wp=wp, pscale=row(p["pool_scale"][l]),
                goa=row(p["g_out_mla"][l]), gos=row(p["g_out_sgu"][l]), gop=row(p["g_out_pool"][l]),
                gffn=row(p["g_ffn_norm"][l]))


MIXER_SIDE = ("w_in", "w_q_up", "w_kv_up", "w_out")
FFN_SIDE = ("w_gate", "w_up", "w_down")
TRANSPOSED = ("w_in", "w_q_up", "w_kv_up", "w_gate", "w_up")
FIRST_0 = ("w_in", "w_q_up", "w_kv_up")
LATER_0 = ("w_out",) + FFN_SIDE
SHARDED = MIXER_SIDE + FFN_SIDE
SMALL = ("g_mix_norm", "g_q_lat", "g_kv_lat", "g_q_head", "g_k_head", "g_sgu_v", "w_spatial", "b_spatial", "w_pool",
         "pool_scale", "g_out_mla", "g_out_sgu", "g_out_pool", "g_ffn_norm")
WEIGHTS = ("g_mix_norm", "w_in", "g_q_lat", "w_q_up", "g_kv_lat", "w_kv_up", "g_q_head", "g_k_head", "g_sgu_v", "w_spatial",
           "b_spatial", "w_pool", "pool_scale", "g_out_mla", "g_out_sgu", "g_out_pool", "w_out", "g_ffn_norm", "w_gate",
           "w_up", "w_down")
PACKED = SMALL + ("loss",)
PACK_ROWS = 8 * 128


def _pack_small(parts):
    flat = []
    for name in PACKED:
        a = parts[name].reshape(-1)
        flat.append(jnp.pad(a, (0, -a.shape[0] % PACK_ROWS)))
    return jnp.concatenate(flat).reshape(-1, 128)


def _unpack_small(packed, like):
    out, row = {}, 0
    for name in PACKED:
        n = math.prod(like[name].shape)
        rows = -(-n // PACK_ROWS) * 8
        out[name] = packed[row:row + rows].reshape(-1)[:n].reshape(like[name].shape)
        row += rows
    return out


def _forward_attention(x, rope, W, S):
    z, q, k, v, kt, vt = mixin_fwd(x, rope, S["gmix"], W["win"], S["gql"], W["wq"], S["gkvl"], W["wkv"], S["gqh"], S["gkh"])
    a, lse = attn_fwd(q, k, vt)
    return dict(x=x, z=z, q=q, k=k, kt=kt, v=v, a=a, lse=lse)


def _forward_mixers(A, wout, S):
    A["x1"] = mixers_fwd(A["x"], A["a"], A["z"], S["gsgu"], S["wcat"], S["bz"], S["wp"], S["pscale"], S["goa"], S["gos"],
                         S["gop"], wout)
    return A["x1"]


def _backward_ffn(dx2, W, S, A, l):
    dx1, h2, act, dg, du_ffn, dgffn = ffn_bwd(dx2, A["x1"], A["gs"], A["us"], S["gffn"], W["wg"], W["wu"], W["wd"])
    d_wd = matmul_tn(act, dx2, f"dw_down_{l}")
    d_wg = matmul_tn(dg, h2, f"dw_gate_{l}")
    d_wu = matmul_tn(du_ffn, h2, f"dw_up_{l}")
    big = {n: d.reshape(N_DEV, -1, D_MODEL) for n, d in (("w_gate", d_wg), ("w_up", d_wu), ("w_down", d_wd))}
    return dx1, big, dgffn


def _backward_mixers(dx1, dgffn, rope, W, S, A, l, send_wout=None):
    (da, delta, du, dvs, dm, mix, dgoa, dgos, dgop, dps, dwp, dwsp, db, dgsgu) = mixers_bwd(
        dx1, A["a"], A["z"], S["gsgu"], S["wcat"], S["wcat_t"], S["bz"], S["wp"], S["pscale"], S["goa"], S["gos"],
        S["gop"], W["wout"])
    d_wout = matmul_tn(mix, dx1, f"dw_out_{l}")
    if send_wout is not None:
        delta = delta + send_wout(d_wout.reshape(N_DEV, -1, D_MODEL))[0, 0]
    dpin = pool_bwd(dm)
    dqt, dk, dv = attn_bwd(A["q"], A["k"], A["kt"], A["v"], da, A["lse"], delta)
    (dx, hn, dz, qn, dq_up, kvn, dkv_up, dgmix, dgql, dgkvl, dgqh, dgkh) = mixin_bwd(
        dx1, A["x"], A["z"], rope, dqt, dk, dv, du, dvs, dpin, S["gmix"], W["win"], S["gql"], W["wq"], S["gkvl"],
        W["wkv"], S["gqh"], S["gkh"])
    d_win = matmul_tn(dz, hn, f"dw_in_{l}")
    d_wq = matmul_tn(dq_up, qn, f"dw_q_up_{l}")
    d_wkv = matmul_tn(dkv_up, kvn, f"dw_kv_up_{l}")
    d_win = jnp.concatenate([d_win[:O2], d_win[O2 + QK_NOPE:O2 + QK_HEAD], d_win[O2 + HEAD_PAD:]], axis=0)
    d_wk = d_wkv[:HEADS * HEAD_PAD].reshape(HEADS, HEAD_PAD, KV_LORA)[:, :QK_NOPE]
    d_wv = d_wkv[HEADS * HEAD_PAD:].reshape(HEADS, V_HEAD, KV_LORA)
    d_wkv = jnp.concatenate([d_wk, d_wv], axis=1)
    big = dict(w_in=d_win.reshape(N_DEV, -1, D_MODEL), w_q_up=_unpad_head_rows(d_wq, QK_HEAD).reshape(N_DEV, -1, Q_LORA),
               w_kv_up=d_wkv.reshape(N_DEV, -1, KV_LORA), w_out=d_wout.reshape(N_DEV, -1, D_MODEL))
    tril = jnp.tril(jnp.ones((CHUNK, CHUNK), F32))
    small = dict(g_mix_norm=dgmix[0], g_q_lat=dgql[0], g_kv_lat=dgkvl[0], g_q_head=dgqh[0, :QK_HEAD], g_k_head=dgkh[0, :QK_HEAD],
                 g_sgu_v=dgsgu[0], w_spatial=dwsp.reshape(HEADS, CHUNK, CHUNK) * tril, b_spatial=db[:, :HEADS].T,
                 w_pool=jnp.stack([dwp[64 * g:64 * (g + 1), 64 * g:64 * (g + 1)] for g in range(HEADS)]),
                 pool_scale=dps[0], g_out_mla=dgoa[0], g_out_sgu=dgos[0], g_out_pool=dgop[0], g_ffn_norm=dgffn[0])
    return dx, big, small


def kernel(x, positions, g_mix_norm, w_in, g_q_lat, w_q_up, g_kv_lat, w_kv_up, g_q_head, g_k_head, g_sgu_v, w_spatial, b_spatial, w_pool, pool_scale, g_out_mla, g_out_sgu, g_out_pool, w_out, g_ffn_norm, w_gate, w_up, w_down, loss_target, m_g_mix_norm, m_w_in, m_g_q_lat, m_w_q_up, m_g_kv_lat, m_w_kv_up, m_g_q_head, m_g_k_head, m_g_sgu_v, m_w_spatial, m_b_spatial, m_w_pool, m_pool_scale, m_g_out_mla, m_g_out_sgu, m_g_out_pool, m_w_out, m_g_ffn_norm, m_w_gate, m_w_up, m_w_down, v_g_mix_norm, v_w_in, v_g_q_lat, v_w_q_up, v_g_kv_lat, v_w_kv_up, v_g_q_head, v_g_k_head, v_g_sgu_v, v_w_spatial, v_b_spatial, v_w_pool, v_pool_scale, v_g_out_mla, v_g_out_sgu, v_g_out_pool, v_w_out, v_g_ffn_norm, v_w_gate, v_w_up, v_w_down):
    given = dict(locals())
    w = {n: given[n] for n in WEIGHTS}
    m = {n: given["m_" + n] for n in WEIGHTS}
    v = {n: given["v_" + n] for n in WEIGHTS}
    T = x.shape[1]
    xs = x.reshape(T, D_MODEL)

    half = QK_ROPE // 2
    inv_freq = 1.0 / (ROPE_THETA ** (jnp.arange(half, dtype=F32) / half))
    ang16 = positions.reshape(T).astype(F32)[:, None] * inv_freq
    ang = jnp.concatenate([jnp.zeros((T, QK_NOPE), F32), ang16, ang16, jnp.zeros((T, HEAD_PAD - QK_HEAD), F32)], axis=1)

    wv = {n: _shard_view(n, w[n]) for n in SHARDED}
    wanted = [(SHARDED.index(n), 0) for n in FIRST_0 + LATER_0] + [(i, 1) for i in range(len(SHARDED))]
    bf = cast_shards([wv[n] for n in SHARDED], wanted)
    nf, nl = len(FIRST_0), len(LATER_0)
    ag_first0 = exchange_start(bf[:nf], ang, True, "ag_start_first0")
    rope = rope_tables(ang + ag_first0[-1][0, 0])
    W0 = dict(zip(FIRST_0, exchange_wait(ag_first0, rope[0], "ag_wait_first0")))
    ag_later0 = exchange_start(bf[nf:nf + nl], W0["w_in"], True, "ag_start_later0")
    ag_l1 = exchange_start(bf[nf + nl:], ag_later0[-1], True, "ag_start_l1")
    Ss = [_layer_small(w, l) for l in range(DEPTH)]

    S0 = dict(Ss[0], gmix=Ss[0]["gmix"] + (ag_later0[-1][0, 0] + ag_l1[-1][0, 0]))
    A0 = _forward_attention(xs, rope, _mixer_weights(W0), S0)
    W0.update(zip(LATER_0, exchange_wait(ag_later0, A0["a"], "ag_wait_later0")))
    W0 = {**_mixer_weights(W0), **_ffn_weights(W0)}
    x1 = _forward_mixers(A0, W0["wout"], Ss[0])
    h, A0["gs"], A0["us"] = ffn_fwd(x1, Ss[0]["gffn"], W0["wg"], W0["wu"], W0["wd"])
    layer1 = dict(zip(SHARDED, exchange_wait(ag_l1, h, "ag_wait_l1")))
    W1 = {**_mixer_weights(layer1), **_ffn_weights(layer1)}
    A1 = _forward_attention(h, rope, W1, Ss[1])
    x1 = _forward_mixers(A1, W1["wout"], Ss[1])
    dh, loss_part, A1["gs"], A1["us"] = ffn_fwd(x1, Ss[1]["gffn"], W1["wg"], W1["wu"], W1["wd"],
                                                target=loss_target.reshape(T, D_MODEL))

    dx1, big_f1, dgffn1 = _backward_ffn(dh, W1, Ss[1], A1, 1)
    dh, big_m1, small1 = _backward_mixers(dx1, dgffn1, rope, W1, Ss[1], A1, 1)
    rs_l1 = exchange_start([{**big_m1, **big_f1}[n] for n in SHARDED], dh, False, "rs_start_l1")
    S0 = dict(Ss[0], gffn=Ss[0]["gffn"] + rs_l1[-1][0, 0])
    dx1, big_f0, dgffn0 = _backward_ffn(dh, W0, S0, A0, 0)
    rs_ffn0 = exchange_start([big_f0[n] for n in FFN_SIDE], dx1, False, "rs_start_ffn0")
    S0 = dict(Ss[0], gsgu=Ss[0]["gsgu"] + rs_ffn0[-1][0, 0])
    rs_out0 = []

    def send_wout(pieces):
        rs_out0.append(exchange_start([pieces], dx1, False, "rs_start_out0"))
        return rs_out0[0][-1]

    dh, big_m0, small0 = _backward_mixers(dx1, dgffn0, rope, W0, S0, A0, 0, send_wout)
    grad_x = dh.reshape(x.shape)
    smalls = [small0, small1]

    rs_first0 = exchange_start([big_m0[n] for n in FIRST_0], dh, False, "rs_start_first0")
    parts1 = dict(zip(SHARDED, exchange_wait(rs_l1, rs_first0[-1], "rs_wait_l1")))
    parts0 = dict(zip(FFN_SIDE, exchange_wait(rs_ffn0, rs_first0[-1], "rs_wait_ffn0")))
    parts0["w_out"] = exchange_wait(rs_out0[0], rs_first0[-1], "rs_wait_out0")[0]
    grad, delta, new_m, new_v = {}, {}, {}, {}

    small_part = {n: jnp.stack([smalls[l][n] for l in range(DEPTH)]) for n in SMALL}
    small_part["loss"] = loss_part
    no_loss = {"loss": jnp.zeros((1, 1), F32)}
    outs = allreduce_adamw_small(_pack_small(small_part), *[_pack_small({**a, **no_loss}) for a in (w, m, v)])
    for d, o in zip((grad, delta, new_m, new_v), outs):
        d.update(_unpack_small(o, {**w, **no_loss}))
    loss = grad["loss"][0, 0]

    for n in LATER_0 + FIRST_0:
        if n == FIRST_0[0]:
            parts0.update(zip(FIRST_0, exchange_wait(rs_first0, grad["w_down"], "rs_wait_first0")))
        parts, views = [parts0[n], parts1[n]], [_shard_view(n, a[n]) for a in (w, m, v)]
        if n == "w_in":
            parts = [jnp.stack(parts, axis=2).reshape(N_DEV, -1, D_MODEL)]
            views = [a.reshape(1, -1, D_MODEL) for a in views]
        outs = adamw_sharded(parts, *views, f"adamw_{n}")
        if n == "w_in":
            outs = [o.reshape(-1, DEPTH, D_MODEL) for o in outs]
        grad[n], delta[n], new_m[n], new_v[n] = [_shard_unview(n, o) for o in outs]

    return (loss, grad_x, *[grad[n] for n in WEIGHTS], *[delta[n] for n in WEIGHTS], *[new_m[n] for n in WEIGHTS],
            *[new_v[n] for n in WEIGHTS])
```

```python
import math
from typing import NamedTuple

import jax
import jax.numpy as jnp
from jax import lax
from jax.experimental import pallas as pl
from jax.experimental.pallas import tpu as pltpu

F32 = jnp.float32
BF16 = jnp.bfloat16

N_DEV = 8
DEPTH = 2
D_MODEL = 1024
HEADS = 4
HEAD_PAD = 128
QK_NOPE = 64
QK_ROPE = 32
QK_HEAD = QK_NOPE + QK_ROPE
V_HEAD = 128
Q_LORA = 256
KV_LORA = 128
SGU_WIDTH = 256
SGU_HEAD_DIM = 64
CHUNK = 128
POOL_WIDTH = 256
POOL_HALO = 16
MLA_WIDTH = 512
IN_WIDTH = 1184
Z_WIDTH = 1280
FFN_HIDDEN = 2816
FFN_CHUNK = 256
ROPE_THETA = 10000.0
EPS = 1e-6
ATTN_SCALE = 1.0 / math.sqrt(QK_HEAD)
LOG2E = 1.4426950408889634
NEG_BIG = -1e30

ADAM_LR = 0.001
ADAM_B1 = 0.9
ADAM_B2 = 0.999
ADAM_EPS = 1e-08
ADAM_WD = 0.01
ADAM_STEP = 10

VMEM_LIMIT = 56 * 1024 * 1024
ROW_TILE = 512
MIXIN_PART = 256
MIXIN_BWD_PART = 128
ATTN_TILE = 1024
ATTN_SUB = 512
ATTN_FWD_SUB = 512
ATTN_BWD_HEADS = 2
MESH = pl.DeviceIdType.MESH

WHOLE = pl.BlockSpec(memory_space=pltpu.VMEM)
ANY = pl.BlockSpec(memory_space=pl.ANY)
HBM_SPEC = pl.BlockSpec(memory_space=pltpu.HBM)
SEM_SPEC = pl.BlockSpec(memory_space=pltpu.SEMAPHORE)


class _Layered(NamedTuple):
    stacked: jax.Array
    l: int


def _operands(params):
    arrays, specs = [], []
    for p in params:
        if isinstance(p, _Layered):
            arrays.append(p.stacked)
            specs.append(pl.BlockSpec((None,) + p.stacked.shape[1:], lambda *g, l=p.l: (l, 0, 0)))
        else:
            arrays.append(p)
            specs.append(WHOLE)
    return arrays, specs


def _cparams(**kw):
    return pltpu.CompilerParams(vmem_limit_bytes=VMEM_LIMIT, **kw)


def _dot(a, b):
    return jnp.dot(a, b, preferred_element_type=F32)


def _dot_nt(a, b):
    return lax.dot_general(a, b, (((1,), (1,)), ((), ())), preferred_element_type=F32)


def _dot_tn(a, b):
    return lax.dot_general(a, b, (((0,), (0,)), ((), ())), preferred_element_type=F32)


def _rms(x, g, n):
    r = lax.rsqrt(jnp.sum(x * x, axis=-1, keepdims=True) * (1.0 / n) + EPS)
    return x * r * g, r


def _rms_bwd(x, r, g, dy, n):
    gdy = dy * g
    dx = r * gdy - x * (r * r * r) * (jnp.sum(x * gdy, axis=-1, keepdims=True) * (1.0 / n))
    dg = jnp.sum(dy * (x * r), axis=0, keepdims=True)
    return dx, dg


def _sigmoid(x):
    return 1.0 / (1.0 + jnp.exp(-x))


def rope_tables(ang):
    T = ang.shape[0]
    tm = min(ROW_TILE, T)

    def body(ang_ref, c_ref, sa_ref, sb_ref):
        a = ang_ref[...]
        lane = lax.broadcasted_iota(jnp.int32, a.shape, 1)
        s = jnp.sin(a)
        c_ref[...] = jnp.cos(a)
        sa_ref[...] = jnp.where(lane < QK_NOPE + QK_ROPE // 2, -s, 0.0)
        sb_ref[...] = jnp.where(lane >= QK_NOPE + QK_ROPE // 2, s, 0.0)

    row = pl.BlockSpec((tm, HEAD_PAD), lambda i: (i, 0))
    return pl.pallas_call(
        body, grid=(T // tm,), name="rope_tables", in_specs=[row], out_specs=[row] * 3,
        out_shape=[jax.ShapeDtypeStruct((T, HEAD_PAD), F32)] * 3, compiler_params=_cparams(),
    )(ang)


def _rope(x, c, sa, sb):
    half = QK_ROPE // 2
    return x * c + pltpu.roll(x, HEAD_PAD - half, 1) * sa + pltpu.roll(x, half, 1) * sb


def _rope_bwd(dy, c, sa, sb):
    half = QK_ROPE // 2
    return dy * c + pltpu.roll(dy * sa, half, 1) + pltpu.roll(dy * sb, HEAD_PAD - half, 1)


def _skewed(stages, groups):
    for t in range(len(stages) + len(groups) - 1):
        for p, g in enumerate(groups):
            if 0 <= t - p < len(stages):
                stages[t - p](g)


def _head_masks(shape, width):
    lane = lax.broadcasted_iota(jnp.int32, shape, len(shape) - 1)
    return [(lane >= width * h) & (lane < width * (h + 1)) for h in range(HEADS)]


def _qkv_pre(z, gql, wq, gkvl, wkv):
    ql = z[:, 0:Q_LORA]
    kvl = z[:, Q_LORA:Q_LORA + KV_LORA]
    kr = z[:, Q_LORA + KV_LORA:Q_LORA + KV_LORA + HEAD_PAD]
    qn, rq = _rms(ql, gql, Q_LORA)
    kvn, rkv = _rms(kvl, gkvl, KV_LORA)
    qn = qn.astype(BF16)
    kvn = kvn.astype(BF16)
    q_up = _dot_nt(qn, wq)
    kv_up = _dot_nt(kvn, wkv)
    return ql, kvl, kr, qn, rq, kvn, rkv, q_up, kv_up


def mixin_fwd(x, rope, gmix, win, gql, wq, gkvl, wkv, gqh, gkh):
    T = x.shape[0]
    tm = min(2 * ROW_TILE, T)
    part = min(MIXIN_PART, tm)

    def body(x_ref, c_ref, sa_ref, sb_ref, gmix_ref, win_ref, gql_ref, wq_ref, gkvl_ref, wkv_ref, gqh_ref, gkh_ref,
             z_ref, q_ref, k_ref, v_ref, kt_ref, vt_ref):
        def project(g):
            hn = _rms(x_ref[g["rows"], :], gmix_ref[...], D_MODEL)[0].astype(BF16)
            g["z"] = _dot_nt(hn, win_ref[...])
            z_ref[g["rows"], :] = g["z"]

        def latents(g):
            _, _, g["kr"], _, _, _, _, g["q_up"], g["kv_up"] = _qkv_pre(g["z"], gql_ref[...], wq_ref[...], gkvl_ref[...],
                                                                        wkv_ref[...])

        def heads(g):
            rows = g["rows"]
            c, sa, sb = c_ref[rows, :], sa_ref[rows, :], sb_ref[rows, :]
            for h in range(HEADS):
                lo = HEAD_PAD * h
                qh = _rms(g["q_up"][:, lo:lo + HEAD_PAD], gqh_ref[...], QK_HEAD)[0]
                q_ref[h, rows, :] = (_rope(qh, c, sa, sb) * (ATTN_SCALE * LOG2E)).astype(BF16)
                kh = _rope(_rms(g["kv_up"][:, lo:lo + HEAD_PAD] + g["kr"], gkh_ref[...], QK_HEAD)[0], c, sa, sb)
                k_ref[h, rows, :] = kh.astype(BF16)
                kt_ref[h, :, rows] = jnp.transpose(kh).astype(BF16)
                vh = g["kv_up"][:, HEADS * HEAD_PAD + lo:HEADS * HEAD_PAD + lo + HEAD_PAD]
                v_ref[h, rows, :] = vh.astype(BF16)
                vt_ref[h, :, rows] = jnp.transpose(vh).astype(BF16)

        _skewed((project, latents, heads), [dict(rows=pl.ds(part * p, part)) for p in range(tm // part)])

    row = lambda w: pl.BlockSpec((tm, w), lambda i: (i, 0))
    head = pl.BlockSpec((HEADS, tm, HEAD_PAD), lambda i: (0, i, 0))
    head_t = pl.BlockSpec((HEADS, HEAD_PAD, tm), lambda i: (0, 0, i))
    params, param_specs = _operands((gmix, win, gql, wq, gkvl, wkv, gqh, gkh))
    return pl.pallas_call(
        body, grid=(T // tm,), name="mixin_fwd",
        in_specs=[row(D_MODEL)] + [row(HEAD_PAD)] * 3 + param_specs,
        out_specs=[row(Z_WIDTH), head, head, head, head_t, head_t],
        out_shape=[jax.ShapeDtypeStruct((T, Z_WIDTH), F32)] + [jax.ShapeDtypeStruct((HEADS, T, HEAD_PAD), BF16)] * 3
                  + [jax.ShapeDtypeStruct((HEADS, HEAD_PAD, T), BF16)] * 2,
        compiler_params=_cparams(),
    )(x, *rope, *params)


def _pair_tables(pairs):
    return (jnp.asarray([p[0] for p in pairs], jnp.int32), jnp.asarray([p[1] for p in pairs], jnp.int32))


def attn_fwd(q, k, vt):
    _, T, _ = q.shape
    tb = min(ATTN_TILE, T)
    sq = min(ATTN_FWD_SUB, tb)
    nb = T // tb

    pairs = [(i, j) for i in range(nb) for j in range(i + 1)]

    def body(i_tab, j_tab, q_ref, k_ref, vt_ref, o_ref, lse_ref, m_s, l_s, acc_s):
        i, j = i_tab[pl.program_id(0)], j_tab[pl.program_id(0)]

        @pl.when(j == 0)
        def _():
            m_s[...] = jnp.full(m_s.shape, -jnp.inf, F32)
            l_s[...] = jnp.zeros(l_s.shape, F32)
            acc_s[...] = jnp.zeros(acc_s.shape, F32)

        def scores(g):
            st = _dot_nt(k_ref[g["h"], g["kb"], :], q_ref[g["h"], g["qa"], :])
            if g["masked"]:
                krow = lax.broadcasted_iota(jnp.int32, st.shape, 0)
                qcol = g["q0"] + lax.broadcasted_iota(jnp.int32, st.shape, 1)
                st = jnp.where(krow <= qcol, st, NEG_BIG)
            g["st"] = st

        def new_max(g):
            h, qa = g["h"], g["qa"]
            m_prev = m_s[h, :, qa]
            g["m"] = jnp.maximum(m_prev, jnp.max(g["st"], axis=0, keepdims=True))
            g["alpha"] = jnp.exp2(m_prev - g["m"])
            m_s[h, :, qa] = g["m"]

        def weights(g):
            pt = jnp.exp2(g["st"] - g["m"])
            g["lsum"] = jnp.sum(pt, axis=0, keepdims=True)
            g["pt"] = pt.astype(BF16)

        def accumulate(g):
            h, qa = g["h"], g["qa"]
            l_s[h, :, qa] = g["alpha"] * l_s[h, :, qa] + g["lsum"]
            acc_s[h, :, qa] = g["alpha"] * acc_s[h, :, qa] + _dot(vt_ref[h, :, g["kb"]], g["pt"])

        def tiles(masked):
            return [dict(h=h, q0=q0, qa=slice(q0, q0 + sq), kb=slice(0, q0 + sq if masked else tb), masked=masked)
                    for h in range(HEADS) for q0 in range(0, tb, sq)]

        @pl.when(j < i)
        def _():
            _skewed((scores, new_max, weights, accumulate), tiles(False))

        @pl.when(j == i)
        def _():
            _skewed((scores, new_max, weights, accumulate), tiles(True))
            for h in range(HEADS):
                l = l_s[h]
                o_ref[:, HEAD_PAD * h:HEAD_PAD * (h + 1)] = jnp.transpose(acc_s[h] / l)
                lse_ref[h] = m_s[h] + jnp.log2(l)

    qspec = pl.BlockSpec((HEADS, tb, HEAD_PAD), lambda p, it, jt: (0, it[p], 0))
    kspec = pl.BlockSpec((HEADS, tb, HEAD_PAD), lambda p, it, jt: (0, jt[p], 0))
    vspec = pl.BlockSpec((HEADS, HEAD_PAD, tb), lambda p, it, jt: (0, 0, jt[p]))
    grid_spec = pltpu.PrefetchScalarGridSpec(
        num_scalar_prefetch=2, grid=(len(pairs),),
        in_specs=[qspec, kspec, vspec],
        out_specs=[pl.BlockSpec((tb, MLA_WIDTH), lambda p, it, jt: (it[p], 0)),
                   pl.BlockSpec((HEADS, 1, tb), lambda p, it, jt: (0, 0, it[p]))],
        scratch_shapes=[pltpu.VMEM((HEADS, 1, tb), F32), pltpu.VMEM((HEADS, 1, tb), F32), pltpu.VMEM((HEADS, HEAD_PAD, tb), F32)])
    return pl.pallas_call(
        body, grid_spec=grid_spec, name="attn_fwd",
        out_shape=[jax.ShapeDtypeStruct((T, MLA_WIDTH), F32), jax.ShapeDtypeStruct((HEADS, 1, T), F32)],
        compiler_params=_cparams(),
    )(*_pair_tables(pairs), q, k, vt)


def _sgu_fwd_chunk(vn_c, wcat, bz, masks):
    vstack = jnp.concatenate([jnp.where(mk, vn_c, 0.0).astype(BF16) for mk in masks], axis=0)
    return _dot(wcat, vstack) + bz


def _pool_counts(i, tm):
    pos1 = (i * tm + 1 + lax.broadcasted_iota(jnp.int32, (tm, POOL_WIDTH), 0)).astype(F32)
    lane = lax.broadcasted_iota(jnp.int32, (tm, POOL_WIDTH), 1)
    win = jnp.where(lane < 64, 2.0, jnp.where(lane < 128, 4.0, jnp.where(lane < 192, 8.0, 16.0)))
    return jnp.minimum(pos1, win), lane


def _by_group(lane, s2, s4, s8, s16):
    return jnp.where(lane < 64, s2, jnp.where(lane < 128, s4, jnp.where(lane < 192, s8, s16)))


def _pool_means(pin, halo, i, tm):
    s1 = jnp.concatenate([halo, pin], axis=0)
    s2 = s1 + pltpu.roll(s1, 1, 0)
    s4 = s2 + pltpu.roll(s2, 2, 0)
    s8 = s4 + pltpu.roll(s4, 4, 0)
    s16 = s8 + pltpu.roll(s8, 8, 0)
    cnt, lane = _pool_counts(i, tm)
    sel = _by_group(lane, s2[POOL_HALO:], s4[POOL_HALO:], s8[POOL_HALO:], s16[POOL_HALO:])
    return sel / cnt - pin


def _mixers_fwd_tile(i, tm, a, u, vs, pin, halo, gsgu, wcat, bz, wp, pscale, goa, gos, gop):
    vn, rv = _rms(vs, gsgu, SGU_WIDTH)
    masks = _head_masks((CHUNK, SGU_WIDTH), SGU_HEAD_DIM)
    zc = jnp.concatenate([_sgu_fwd_chunk(vn[CHUNK * c:CHUNK * (c + 1)], wcat, bz, masks) for c in range(tm // CHUNK)], axis=0)
    gm = u * zc
    halo = jnp.where(i > 0, halo, 0.0)
    m = _pool_means(pin, halo, i, tm).astype(BF16)
    yp_pre = _dot(m, wp)
    yp = yp_pre * pscale
    na, ra = _rms(a, goa, MLA_WIDTH)
    ng, rg = _rms(gm, gos, SGU_WIDTH)
    npo, rp = _rms(yp, gop, POOL_WIDTH)
    mix = jnp.concatenate([na, ng, npo], axis=1).astype(BF16)
    return vn, rv, zc, gm, m, yp_pre, yp, ra, rg, rp, mix


def _z_specs(tm):
    col = lambda c: pl.BlockSpec((tm, 256), lambda i: (i, c))
    halo = pl.BlockSpec((POOL_HALO, 256), lambda i: (jnp.maximum(i * (tm // POOL_HALO) - 1, 0), 4))
    return [col(2), col(3), col(4), halo]


def mixers_fwd(x, a, z, gsgu, wcat, bz, wp, pscale, goa, gos, gop, wout):
    T = x.shape[0]
    tm = min(ROW_TILE, T)

    def body(x_ref, a_ref, u_ref, vs_ref, pin_ref, halo_ref, gsgu_ref, wcat_ref, bz_ref, wp_ref, ps_ref,
             goa_ref, gos_ref, gop_ref, wout_ref, x1_ref):
        i = pl.program_id(0)
        mix = _mixers_fwd_tile(i, tm, a_ref[...], u_ref[...], vs_ref[...], pin_ref[...], halo_ref[...], gsgu_ref[...],
                               wcat_ref[...], bz_ref[...], wp_ref[...], ps_ref[...], goa_ref[...], gos_ref[...],
                               gop_ref[...])[-1]
        x1_ref[...] = x_ref[...] + _dot(mix, wout_ref[...])

    row = lambda w: pl.BlockSpec((tm, w), lambda i: (i, 0))
    params, param_specs = _operands((gsgu, wcat, bz, wp, pscale, goa, gos, gop, wout))
    return pl.pallas_call(
        body, grid=(T // tm,), name="mixers_fwd",
        in_specs=[row(D_MODEL), row(MLA_WIDTH)] + _z_specs(tm) + param_specs,
        out_specs=row(D_MODEL),
        out_shape=jax.ShapeDtypeStruct((T, D_MODEL), F32),
        compiler_params=_cparams(),
    )(x, a, z, z, z, z, *params)


def ffn_fwd(x1, gffn, wg, wu, wd, target=None):
    T = x1.shape[0]
    tm = min(ROW_TILE, T)
    with_loss = target is not None

    def body(*refs):
        x1_ref, gffn_ref, wg_ref, wu_ref, wd_ref = refs[:5]
        outs = refs[6:] if with_loss else refs[5:]
        x1v = x1_ref[...]
        h2 = _rms(x1v, gffn_ref[...], D_MODEL)[0].astype(BF16)
        acc = x1v
        for c in range(FFN_HIDDEN // FFN_CHUNK):
            sl = slice(FFN_CHUNK * c, FFN_CHUNK * (c + 1))
            g = _dot_nt(h2, wg_ref[sl, :])
            u = _dot_nt(h2, wu_ref[sl, :])
            outs[-2][:, sl] = g.astype(BF16)
            outs[-1][:, sl] = u.astype(BF16)
            act = (g * _sigmoid(g) * u).astype(BF16)
            acc = acc + _dot(act, wd_ref[sl, :])
        if not with_loss:
            outs[0][...] = acc
            return
        dy_ref, loss_ref = outs[0], outs[1]

        @pl.when(pl.program_id(0) == 0)
        def _():
            loss_ref[...] = jnp.zeros(loss_ref.shape, F32)

        err = acc - refs[5][...]
        dy_ref[...] = err * (1.0 / D_MODEL)
        per_row = jnp.sum(err * err, axis=1, keepdims=True) * (1.0 / D_MODEL)
        loss_ref[...] += 0.5 * jnp.sum(per_row, axis=0, keepdims=True)

    row = lambda w: pl.BlockSpec((tm, w), lambda i: (i, 0))
    hidden = [jax.ShapeDtypeStruct((T, FFN_HIDDEN), BF16)] * 2
    params, param_specs = _operands((gffn, wg, wu, wd))
    if with_loss:
        return pl.pallas_call(
            body, grid=(T // tm,), name="ffn_fwd_loss",
            in_specs=[row(D_MODEL)] + param_specs + [row(D_MODEL)],
            out_specs=[row(D_MODEL), pl.BlockSpec((1, 1), lambda i: (0, 0)), row(FFN_HIDDEN), row(FFN_HIDDEN)],
            out_shape=[jax.ShapeDtypeStruct((T, D_MODEL), F32), jax.ShapeDtypeStruct((1, 1), F32)] + hidden,
            compiler_params=_cparams(),
        )(x1, *params, target)
    return pl.pallas_call(
        body, grid=(T // tm,), name="ffn_fwd",
        in_specs=[row(D_MODEL)] + param_specs,
        out_specs=[row(D_MODEL), row(FFN_HIDDEN), row(FFN_HIDDEN)],
        out_shape=[jax.ShapeDtypeStruct((T, D_MODEL), F32)] + hidden,
        compiler_params=_cparams(),
    )(x1, *params)


def _acc_spec(shape):
    return pl.BlockSpec(shape, lambda i: (0,) * len(shape))


def ffn_bwd(dx2, x1, gs, us, gffn, wg, wu, wd):
    T = x1.shape[0]
    tm = min(ROW_TILE // 2, T)

    def body(dx2_ref, x1_ref, gs_ref, us_ref, gffn_ref, wg_ref, wu_ref, wd_ref,
             dx1_ref, h2_ref, act_ref, dg_ref, du_ref, dgffn_ref):
        @pl.when(pl.program_id(0) == 0)
        def _():
            dgffn_ref[...] = jnp.zeros(dgffn_ref.shape, F32)

        dx2v = dx2_ref[...]
        dy = dx2v.astype(BF16)
        x1v = x1_ref[...]
        h2, r = _rms(x1v, gffn_ref[...], D_MODEL)
        h2_ref[...] = h2.astype(BF16)
        for c in range(FFN_HIDDEN // FFN_CHUNK):
            sl = slice(FFN_CHUNK * c, FFN_CHUNK * (c + 1))
            g = gs_ref[:, sl].astype(F32)
            u = us_ref[:, sl].astype(F32)
            dact = _dot_nt(dy, wd_ref[sl, :])
            sg = _sigmoid(g)
            silu = g * sg
            act_ref[:, sl] = (silu * u).astype(BF16)
            dg_ref[:, sl] = (dact * u * (sg * (1.0 + g * (1.0 - sg)))).astype(BF16)
            du_ref[:, sl] = (dact * silu).astype(BF16)
        dh2 = _dot(dg_ref[...], wg_ref[...]) + _dot(du_ref[...], wu_ref[...])
        dxn, dgn = _rms_bwd(x1v, r, gffn_ref[...], dh2, D_MODEL)
        dx1_ref[...] = dx2v + dxn
        dgffn_ref[...] += dgn

    row = lambda w: pl.BlockSpec((tm, w), lambda i: (i, 0))
    params, param_specs = _operands((gffn, wg, wu, wd))
    return pl.pallas_call(
        body, grid=(T // tm,), name="ffn_bwd",
        in_specs=[row(D_MODEL), row(D_MODEL), row(FFN_HIDDEN), row(FFN_HIDDEN)] + param_specs,
        out_specs=[row(D_MODEL), row(D_MODEL), row(FFN_HIDDEN), row(FFN_HIDDEN), row(FFN_HIDDEN), _acc_spec((1, D_MODEL))],
        out_shape=[jax.ShapeDtypeStruct((T, D_MODEL), F32), jax.ShapeDtypeStruct((T, D_MODEL), BF16),
                   jax.ShapeDtypeStruct((T, FFN_HIDDEN), BF16), jax.ShapeDtypeStruct((T, FFN_HIDDEN), BF16),
                   jax.ShapeDtypeStruct((T, FFN_HIDDEN), BF16), jax.ShapeDtypeStruct((1, D_MODEL), F32)],
        compiler_params=_cparams(),
    )(dx2, x1, gs, us, *params)


TN_K_TILE = 2048
TN_ACC_BYTES = 6 * 1024 * 1024


def matmul_tn(a, b, name):
    T, M = a.shape
    N = b.shape[1]
    tk = min(TN_K_TILE, T)
    tm = M if M <= 1024 else M // 2
    tn = max(d for d in range(128, N + 1, 128) if N % d == 0 and tm * d * 4 <= TN_ACC_BYTES)
    nk = T // tk

    def body(a_ref, b_ref, o_ref, acc):
        k = pl.program_id(2)

        @pl.when(k == 0)
        def _():
            acc[...] = jnp.zeros(acc.shape, F32)

        acc[...] += _dot_tn(a_ref[...].astype(BF16), b_ref[...].astype(BF16))

        @pl.when(k == nk - 1)
        def _():
            o_ref[...] = acc[...].astype(BF16)

    return pl.pallas_call(
        body, grid=(M // tm, N // tn, nk), name=name,
        in_specs=[pl.BlockSpec((tk, tm), lambda i, j, k: (k, i)), pl.BlockSpec((tk, tn), lambda i, j, k: (k, j))],
        out_specs=pl.BlockSpec((tm, tn), lambda i, j, k: (i, j)),
        out_shape=jax.ShapeDtypeStruct((M, N), BF16),
        scratch_shapes=[pltpu.VMEM((tm, tn), F32)],
        compiler_params=_cparams(),
    )(a, b)


def mixers_bwd(dx1, a, z, gsgu, wcat, wcat_t, bz, wp, pscale, goa, gos, gop, wout):
    T = a.shape[0]
    tm = min(ROW_TILE, T)

    def body(dx1_ref, a_ref, u_ref, vs_ref, pin_ref, halo_ref, gsgu_ref, wcat_ref, wcatt_ref, bz_ref, wp_ref,
             ps_ref, goa_ref, gos_ref, gop_ref, wout_ref,
             da_ref, delta_ref, du_ref, dvs_ref, dm_ref, mix_ref,
             dgoa_ref, dgos_ref, dgop_ref, dps_ref, dwp_ref, dwsp_ref, db_ref, dgsgu_ref):
        i = pl.program_id(0)

        @pl.when(i == 0)
        def _():
            for r in (dgoa_ref, dgos_ref, dgop_ref, dps_ref, dwp_ref, dwsp_ref, db_ref, dgsgu_ref):
                r[...] = jnp.zeros(r.shape, F32)

        a_v, u, vs = a_ref[...], u_ref[...], vs_ref[...]
        goa, gos, gop, pscale_v = goa_ref[...], gos_ref[...], gop_ref[...], ps_ref[...]
        vn, rv, zc, gm, m, yp_pre, yp, ra, rg, rp, mix = _mixers_fwd_tile(
            i, tm, a_v, u, vs, pin_ref[...], halo_ref[...], gsgu_ref[...], wcat_ref[...], bz_ref[...], wp_ref[...],
            pscale_v, goa, gos, gop)
        mix_ref[...] = mix
        dmix = _dot_nt(dx1_ref[...].astype(BF16), wout_ref[...])
        da, dgoa = _rms_bwd(a_v, ra, goa, dmix[:, :MLA_WIDTH], MLA_WIDTH)
        dgm, dgos = _rms_bwd(gm, rg, gos, dmix[:, MLA_WIDTH:MLA_WIDTH + SGU_WIDTH], SGU_WIDTH)
        dyp, dgop = _rms_bwd(yp, rp, gop, dmix[:, MLA_WIDTH + SGU_WIDTH:], POOL_WIDTH)
        da_ref[...] = da
        dgoa_ref[...] += dgoa
        dgos_ref[...] += dgos
        dgop_ref[...] += dgop
        prod = da * a_v
        ones = jnp.ones((8, HEAD_PAD), F32)
        for h in range(HEADS):
            lo = HEAD_PAD * h
            sums = lax.dot_general(ones, prod[:, lo:lo + HEAD_PAD], (((1,), (1,)), ((), ())), preferred_element_type=F32,
                                   precision=lax.Precision.HIGHEST)
            delta_ref[h] = sums[0:1, :]
        dps_ref[...] += jnp.sum(dyp * yp_pre, axis=0, keepdims=True)
        dyp_pre = (dyp * pscale_v).astype(BF16)
        dwp_ref[...] += _dot_tn(m, dyp_pre)
        dm_ref[...] = _dot_nt(dyp_pre, wp_ref[...])
        du_ref[...] = dgm * zc
        dzc = dgm * u
        masks = _head_masks((CHUNK, SGU_WIDTH), SGU_HEAD_DIM)
        lane_b = lax.broadcasted_iota(jnp.int32, (CHUNK, HEAD_PAD), 1)
        dvn_parts = []
        dwsp = jnp.zeros(dwsp_ref.shape, F32)
        db = jnp.zeros(db_ref.shape, F32)
        for c in range(tm // CHUNK):
            dz_c = dzc[CHUNK * c:CHUNK * (c + 1)]
            dzstack = jnp.concatenate([jnp.where(mk, dz_c, 0.0).astype(BF16) for mk in masks], axis=0)
            dvn_parts.append(_dot(wcatt_ref[...], dzstack))
            dwsp = dwsp + _dot_nt(dzstack, vn[CHUNK * c:CHUNK * (c + 1)].astype(BF16))
            for h, mk in enumerate(masks):
                col = jnp.sum(jnp.where(mk, dz_c, 0.0), axis=1, keepdims=True)
                db = db + jnp.where(lane_b == h, col, 0.0)
        dwsp_ref[...] += dwsp
        db_ref[...] += db
        dvs, dgsgu = _rms_bwd(vs, rv, gsgu_ref[...], jnp.concatenate(dvn_parts, axis=0), SGU_WIDTH)
        dvs_ref[...] = dvs
        dgsgu_ref[...] += dgsgu

    row = lambda w: pl.BlockSpec((tm, w), lambda i: (i, 0))
    head = pl.BlockSpec((HEADS, 1, tm), lambda i: (0, 0, i))
    acc_shapes = [(1, MLA_WIDTH), (1, SGU_WIDTH), (1, POOL_WIDTH), (1, POOL_WIDTH), (POOL_WIDTH, POOL_WIDTH),
                  (HEADS * CHUNK, CHUNK), (CHUNK, HEAD_PAD), (1, SGU_WIDTH)]
    params, param_specs = _operands((gsgu, wcat, wcat_t, bz, wp, pscale, goa, gos, gop, wout))
    return pl.pallas_call(
        body, grid=(T // tm,), name="mixers_bwd",
        in_specs=[row(D_MODEL), row(MLA_WIDTH)] + _z_specs(tm) + param_specs,
        out_specs=[row(MLA_WIDTH), head, row(256), row(256), row(256), row(D_MODEL)] + [_acc_spec(s) for s in acc_shapes],
        out_shape=[jax.ShapeDtypeStruct((T, MLA_WIDTH), F32), jax.ShapeDtypeStruct((HEADS, 1, T), F32),
                   jax.ShapeDtypeStruct((T, 256), F32), jax.ShapeDtypeStruct((T, 256), F32),
                   jax.ShapeDtypeStruct((T, 256), F32), jax.ShapeDtypeStruct((T, D_MODEL), BF16)]
                  + [jax.ShapeDtypeStruct(s, F32) for s in acc_shapes],
        compiler_params=_cparams(),
    )(dx1, a, z, z, z, z, *params)


def pool_bwd(dm):
    T = dm.shape[0]
    tm = min(4 * ROW_TILE, T)
    nt = T // tm

    def body(dm_ref, next_ref, dpin_ref):
        i = pl.program_id(0)
        cnt, lane = _pool_counts(i, tm)
        dmv = dm_ref[...]
        win = _by_group(lane[:POOL_HALO], 2.0, 4.0, 8.0, 16.0)
        nxt = jnp.where(i < nt - 1, next_ref[...] / win, 0.0)
        r1 = jnp.concatenate([dmv / cnt, nxt], axis=0)
        n = tm + POOL_HALO
        r2 = r1 + pltpu.roll(r1, n - 1, 0)
        r4 = r2 + pltpu.roll(r2, n - 2, 0)
        r8 = r4 + pltpu.roll(r4, n - 4, 0)
        r16 = r8 + pltpu.roll(r8, n - 8, 0)
        dpin_ref[...] = _by_group(lane, r2[:tm], r4[:tm], r8[:tm], r16[:tm]) - dmv

    return pl.pallas_call(
        body, grid=(nt,), name="pool_bwd",
        in_specs=[pl.BlockSpec((tm, 256), lambda i: (i, 0)),
                  pl.BlockSpec((POOL_HALO, 256), lambda i: (jnp.minimum((i + 1) * (tm // POOL_HALO), T // POOL_HALO - 1), 0))],
        out_specs=pl.BlockSpec((tm, 256), lambda i: (i, 0)),
        out_shape=jax.ShapeDtypeStruct((T, 256), F32),
        compiler_params=_cparams(),
    )(dm, dm)


def attn_bwd(q, k, kt, v, do, lse, delta):
    _, T, _ = q.shape
    tb = min(ATTN_TILE, T)
    sb = min(ATTN_SUB, tb)
    ns = tb // sb
    nb = T // tb
    hb = ATTN_BWD_HEADS

    pairs = [(i, j) for i in range(nb) for j in range(i, nb)]

    def body(i_tab, j_tab, q_ref, k_ref, kt_ref, v_ref, do_ref, lse_ref, delta_ref, dqt_ref, dk_ref, dv_ref, dk_s, dv_s):
        i, j = i_tab[pl.program_id(1)], j_tab[pl.program_id(1)]

        @pl.when(pl.program_id(1) == 0)
        def _():
            dqt_ref[...] = jnp.zeros(dqt_ref.shape, F32)

        @pl.when(j == i)
        def _():
            dk_s[...] = jnp.zeros(dk_s.shape, F32)
            dv_s[...] = jnp.zeros(dv_s.shape, F32)

        def sub_block(h, a, b, masked):
            qa = slice(sb * a, sb * (a + 1))
            kb = slice(sb * b, sb * (b + 1))
            qv = q_ref[h, qa, :]
            dov = do_ref[qa, HEAD_PAD * h:HEAD_PAD * (h + 1)].astype(BF16)
            st = _dot_nt(k_ref[h, kb, :], qv)
            dpt = _dot_nt(v_ref[h, kb, :], dov)
            pt = jnp.exp2(st - lse_ref[h, :, qa])
            if masked:
                krow = lax.broadcasted_iota(jnp.int32, st.shape, 0)
                qcol = lax.broadcasted_iota(jnp.int32, st.shape, 1)
                pt = jnp.where(krow <= qcol, pt, 0.0)
            dst = (pt * (dpt - delta_ref[h, :, qa])).astype(BF16)
            dv_s[h, kb, :] += _dot(pt.astype(BF16), dov)
            dk_s[h, kb, :] += _dot(dst, qv)
            cols = pl.ds(pl.multiple_of(j * tb + sb * a, sb), sb)
            dqt_ref[h, :, cols] += _dot(kt_ref[h, :, kb], dst)

        @pl.when(j > i)
        def _():
            for a in range(ns):
                for h in range(hb):
                    for b in range(ns):
                        sub_block(h, a, b, False)

        @pl.when(j == i)
        def _():
            for a in range(ns):
                for h in range(hb):
                    for b in range(a + 1):
                        sub_block(h, a, b, a == b)

        @pl.when(j == nb - 1)
        def _():
            dk_ref[...] = dk_s[...] * (1.0 / LOG2E)
            dv_ref[...] = dv_s[...]

    qspec = pl.BlockSpec((hb, tb, HEAD_PAD), lambda g, p, it, jt: (g, jt[p], 0))
    kspec = pl.BlockSpec((hb, tb, HEAD_PAD), lambda g, p, it, jt: (g, it[p], 0))
    ktspec = pl.BlockSpec((hb, HEAD_PAD, tb), lambda g, p, it, jt: (g, 0, it[p]))
    rowspec = pl.BlockSpec((hb, 1, tb), lambda g, p, it, jt: (g, 0, jt[p]))
    grid_spec = pltpu.PrefetchScalarGridSpec(
        num_scalar_prefetch=2, grid=(HEADS // hb, len(pairs)),
        in_specs=[qspec, kspec, ktspec, kspec, pl.BlockSpec((tb, hb * HEAD_PAD), lambda g, p, it, jt: (jt[p], g)), rowspec, rowspec],
        out_specs=[pl.BlockSpec((hb, HEAD_PAD, T), lambda g, p, it, jt: (g, 0, 0)), kspec, kspec],
        scratch_shapes=[pltpu.VMEM((hb, tb, HEAD_PAD), F32), pltpu.VMEM((hb, tb, HEAD_PAD), F32)])
    return pl.pallas_call(
        body, grid_spec=grid_spec, name="attn_bwd",
        out_shape=[jax.ShapeDtypeStruct((HEADS, HEAD_PAD, T), F32)] + [jax.ShapeDtypeStruct((HEADS, T, HEAD_PAD), F32)] * 2,
        compiler_params=_cparams(),
    )(*_pair_tables(pairs), q, k, kt, v, do, lse, delta)


def mixin_bwd(dres, x, z, rope, dqt, dk, dv, du, dvs, dpin, gmix, win, gql, wq, gkvl, wkv, gqh, gkh):
    T = x.shape[0]
    tm = min(ROW_TILE, T)
    part = min(MIXIN_BWD_PART, tm)

    def body(dres_ref, x_ref, z_ref, c_ref, sa_ref, sb_ref, dqt_ref, dk_ref, dv_ref, du_ref, dvs_ref, dpin_ref,
             gmix_ref, win_ref, gql_ref, wq_ref, gkvl_ref, wkv_ref, gqh_ref, gkh_ref,
             dx_ref, hn_ref, dz_ref, qn_ref, dqup_ref, kvn_ref, dkvup_ref,
             dgmix_ref, dgql_ref, dgkvl_ref, dgqh_ref, dgkh_ref):
        @pl.when(pl.program_id(0) == 0)
        def _():
            for r in (dgmix_ref, dgql_ref, dgkvl_ref, dgqh_ref, dgkh_ref):
                r[...] = jnp.zeros(r.shape, F32)

        lane = lax.broadcasted_iota(jnp.int32, (part, HEAD_PAD), 1)
        rope_lanes = (lane >= QK_NOPE) & (lane < QK_HEAD)

        def recompute(g):
            rows = g["rows"]
            g["xv"] = x_ref[rows, :]
            hn, g["rx"] = _rms(g["xv"], gmix_ref[...], D_MODEL)
            hn_ref[rows, :] = hn.astype(BF16)
            g["ql"], g["kvl"], g["kr"], qn, g["rq"], kvn, g["rkv"], g["q_up"], g["kv_up"] = _qkv_pre(
                z_ref[rows, :], gql_ref[...], wq_ref[...], gkvl_ref[...], wkv_ref[...])
            qn_ref[rows, :] = qn
            kvn_ref[rows, :] = kvn

        def heads(g):
            rows = g["rows"]
            c, sa, sb = c_ref[rows, :], sa_ref[rows, :], sb_ref[rows, :]
            dkr = jnp.zeros((part, HEAD_PAD), F32)
            dgqh = jnp.zeros((1, HEAD_PAD), F32)
            dgkh = jnp.zeros((1, HEAD_PAD), F32)
            dq_parts, dk_parts, dv_parts = [], [], []
            for h in range(HEADS):
                lo = HEAD_PAD * h
                qh = g["q_up"][:, lo:lo + HEAD_PAD]
                rqh = lax.rsqrt(jnp.sum(qh * qh, axis=-1, keepdims=True) * (1.0 / QK_HEAD) + EPS)
                dq_h = jnp.transpose(dqt_ref[h, :, rows]) * ATTN_SCALE
                dqh, dg = _rms_bwd(qh, rqh, gqh_ref[...], _rope_bwd(dq_h, c, sa, sb), QK_HEAD)
                dgqh = dgqh + dg
                dq_parts.append(dqh)
                kh = g["kv_up"][:, lo:lo + HEAD_PAD] + g["kr"]
                rkh = lax.rsqrt(jnp.sum(kh * kh, axis=-1, keepdims=True) * (1.0 / QK_HEAD) + EPS)
                dkh, dg = _rms_bwd(kh, rkh, gkh_ref[...], _rope_bwd(dk_ref[h, rows, :], c, sa, sb), QK_HEAD)
                dgkh = dgkh + dg
                dkr = dkr + jnp.where(rope_lanes, dkh, 0.0)
                dk_parts.append(dkh)
                dv_parts.append(dv_ref[h, rows, :])
            dgqh_ref[...] += dgqh
            dgkh_ref[...] += dgkh
            g["dkr"] = dkr
            g["dq_up"] = jnp.concatenate(dq_parts, axis=1).astype(BF16)
            g["dkv_up"] = jnp.concatenate(dk_parts + dv_parts, axis=1).astype(BF16)
            dqup_ref[rows, :] = g["dq_up"]
            dkvup_ref[rows, :] = g["dkv_up"]

        def latents(g):
            rows = g["rows"]
            dql, dg = _rms_bwd(g["ql"], g["rq"], gql_ref[...], _dot(g["dq_up"], wq_ref[...]), Q_LORA)
            dgql_ref[...] += dg
            dkvl, dg = _rms_bwd(g["kvl"], g["rkv"], gkvl_ref[...], _dot(g["dkv_up"], wkv_ref[...]), KV_LORA)
            dgkvl_ref[...] += dg
            g["dz"] = jnp.concatenate([dql, dkvl, g["dkr"], du_ref[rows, :], dvs_ref[rows, :], dpin_ref[rows, :]],
                                      axis=1).astype(BF16)
            dz_ref[rows, :] = g["dz"]

        def inputs(g):
            rows = g["rows"]
            dxn, dg = _rms_bwd(g["xv"], g["rx"], gmix_ref[...], _dot(g["dz"], win_ref[...]), D_MODEL)
            dgmix_ref[...] += dg
            dx_ref[rows, :] = dres_ref[rows, :] + dxn

        _skewed((recompute, heads, latents, inputs), [dict(rows=pl.ds(part * p, part)) for p in range(tm // part)])

    row = lambda w: pl.BlockSpec((tm, w), lambda i: (i, 0))
    head = pl.BlockSpec((HEADS, tm, HEAD_PAD), lambda i: (0, i, 0))
    head_t = pl.BlockSpec((HEADS, HEAD_PAD, tm), lambda i: (0, 0, i))
    acc_shapes = [(1, D_MODEL), (1, Q_LORA), (1, KV_LORA), (1, HEAD_PAD), (1, HEAD_PAD)]
    out_rows = [(D_MODEL, F32), (D_MODEL, BF16), (Z_WIDTH, BF16), (Q_LORA, BF16), (HEADS * HEAD_PAD, BF16),
                (KV_LORA, BF16), (2 * HEADS * HEAD_PAD, BF16)]
    params, param_specs = _operands((gmix, win, gql, wq, gkvl, wkv, gqh, gkh))
    return pl.pallas_call(
        body, grid=(T // tm,), name="mixin_bwd",
        in_specs=[row(D_MODEL), row(D_MODEL), row(Z_WIDTH)] + [row(HEAD_PAD)] * 3 + [head_t, head, head, row(256), row(256), row(256)]
                 + param_specs,
        out_specs=[row(w) for w, _ in out_rows] + [_acc_spec(s) for s in acc_shapes],
        out_shape=[jax.ShapeDtypeStruct((T, w), dt) for w, dt in out_rows] + [jax.ShapeDtypeStruct(s, F32) for s in acc_shapes],
        compiler_params=_cparams(),
    )(dres, x, z, *rope, dqt, dk, dv, du, dvs, dpin, *params)


def _place():
    x, y, c = lax.axis_index("x"), lax.axis_index("y"), lax.axis_index("c")
    return x, y, c, 4 * x + 2 * y + c


def _layer_of(ref, l):
    return ref[:, l, :] if ref.shape[1] == DEPTH else ref[l]


def _layer_shape(shard):
    return (shard.shape[0], shard.shape[2]) if shard.shape[1] == DEPTH else shard.shape[1:]


def cast_shards(shards, wanted):
    n = len(shards)

    def body(*refs):
        for o_ref, (w, l) in zip(refs[n:], wanted):
            o_ref[...] = _layer_of(refs[w], l).astype(BF16)

    return pl.pallas_call(
        body, name="cast_shards", in_specs=[WHOLE] * n, out_specs=[WHOLE] * len(wanted),
        out_shape=[jax.ShapeDtypeStruct(_layer_shape(shards[w]), BF16) for w, _ in wanted],
        compiler_params=_cparams(),
    )(*shards)


def _peer(k):
    x, y, c, _ = _place()
    px = 1 - x if k & 4 else x
    py = 1 - y if k & 2 else y
    pc = 1 - c if k & 1 else c
    return (px, py, pc), 4 * px + 2 * py + pc


def exchange_start(srcs, after, gather, name):
    n = len(srcs)
    land_shapes = [((N_DEV,) + s.shape) if gather else s.shape for s in srcs]

    def body(*refs):
        src_refs, land_refs = refs[:n], refs[n:2 * n]
        send_sems, recv_sems = refs[2 * n + 1:3 * n + 1], refs[3 * n + 1:4 * n + 1]
        token = refs[-1]
        _, _, _, me = _place()
        for k in range(1, N_DEV):
            peer, peer_id = _peer(k)
            for w in range(n):
                pltpu.make_async_remote_copy(
                    src_ref=src_refs[w] if gather else src_refs[w].at[peer_id], dst_ref=land_refs[w].at[me],
                    send_sem=send_sems[w], recv_sem=recv_sems[w], device_id=peer, device_id_type=MESH).start()
        for w in range(n):
            pltpu.make_async_copy(src_refs[w] if gather else src_refs[w].at[me], land_refs[w].at[me], recv_sems[w]).start()
        token[...] = jnp.zeros(token.shape, F32)

    hbm = lambda a: pltpu.with_memory_space_constraint(a, pltpu.HBM)
    outs = pl.pallas_call(
        body, name=name,
        out_shape=(pltpu.SemaphoreType.DMA(()),) * (2 * n)
                  + tuple(pltpu.HBM(s.shape, BF16) for s in srcs) + tuple(pltpu.HBM(s, BF16) for s in land_shapes)
                  + (jax.ShapeDtypeStruct((8, 128), F32),),
        in_specs=[HBM_SPEC] * (2 * n) + [ANY],
        out_specs=(SEM_SPEC,) * (2 * n) + (HBM_SPEC,) * (2 * n) + (WHOLE,),
        input_output_aliases={i: 2 * n + i for i in range(2 * n)},
        compiler_params=pltpu.CompilerParams(has_side_effects=pltpu.SideEffectType.DATAFLOW_SIDE_EFFECTING),
    )(*[hbm(s) for s in srcs], *[hbm(lax.empty(s, BF16)) for s in land_shapes], after)
    return list(outs[:n]), list(outs[n:2 * n]), list(outs[2 * n:3 * n]), list(outs[3 * n:4 * n]), outs[-1]


def exchange_wait(started, after, name):
    send_sems, recv_sems, srcs, lands, _ = started
    n = len(srcs)

    def body(*refs):
        land_refs = refs[n:2 * n]
        send_sems, recv_sems = refs[2 * n:3 * n], refs[3 * n:4 * n]
        x, y, c, _ = _place()
        for w in range(n):
            seven, eight = land_refs[w].at[pl.ds(0, N_DEV - 1)], land_refs[w]
            pltpu.make_async_remote_copy(src_ref=seven, dst_ref=seven, send_sem=send_sems[w], recv_sem=recv_sems[w],
                                         device_id=(x, y, c), device_id_type=MESH).wait_send()
            pltpu.make_async_remote_copy(src_ref=eight, dst_ref=eight, send_sem=send_sems[w], recv_sem=recv_sems[w],
                                         device_id=(x, y, c), device_id_type=MESH).wait_recv()

    outs = pl.pallas_call(
        body, name=name,
        out_shape=tuple(pltpu.HBM(s.shape, BF16) for s in srcs) + tuple(pltpu.HBM(l.shape, BF16) for l in lands),
        in_specs=[HBM_SPEC] * (2 * n) + [SEM_SPEC] * (2 * n) + [ANY],
        out_specs=(HBM_SPEC,) * (2 * n),
        input_output_aliases={i: i for i in range(2 * n)},
        compiler_params=pltpu.CompilerParams(has_side_effects=pltpu.SideEffectType.DATAFLOW_SIDE_EFFECTING),
    )(*srcs, *lands, *send_sems, *recv_sems, after)
    return list(outs[n:])


def _adamw(w, g, m, v):
    m2 = ADAM_B1 * m + (1.0 - ADAM_B1) * g
    v2 = ADAM_B2 * v + (1.0 - ADAM_B2) * (g * g)
    m_hat = m2 / (1.0 - ADAM_B1 ** ADAM_STEP)
    v_hat = v2 / (1.0 - ADAM_B2 ** ADAM_STEP)
    delta = -ADAM_LR * (m_hat / (jnp.sqrt(v_hat) + ADAM_EPS) + ADAM_WD * w)
    return delta, m2, v2


def adamw_sharded(parts, w, m, v, name):
    L, R, C = w.shape
    fits = [d for d in range(16, min(R, 512) + 1, 16) if R % d == 0]
    br = max(fits) if fits else R
    nblk = R // br

    def body(*refs):
        p_refs = refs[:L]
        w_ref, m_ref, v_ref, g_ref, d_ref, m2_ref, v2_ref = refs[L:]

        def total(p_ref):
            g = p_ref[0].astype(F32)
            for s in range(1, N_DEV):
                g = g + p_ref[s].astype(F32)
            return g

        g = total(p_refs[0])
        for l in range(1, L):
            g = jnp.where(pl.program_id(0) == l, total(p_refs[l]), g)
        g_ref[...] = g
        d_ref[...], m2_ref[...], v2_ref[...] = _adamw(w_ref[...], g, m_ref[...], v_ref[...])

    blk = pl.BlockSpec((None, br, C), lambda l, i: (l, i, 0))

    def part_spec(k):
        return pl.BlockSpec((N_DEV, br, C), lambda l, i: (0, jnp.where(l == k, i, jnp.where(l < k, 0, nblk - 1)), 0))

    return pl.pallas_call(
        body, grid=(L, nblk), name=name,
        in_specs=[part_spec(k) for k in range(L)] + [blk, blk, blk],
        out_specs=[blk] * 4,
        out_shape=[jax.ShapeDtypeStruct((L, R, C), F32)] * 4,
        compiler_params=_cparams(),
    )(*parts, w, m, v)


def allreduce_adamw_small(gpart, w, m, v):
    R = gpart.shape[0]

    def body(g_ref, w_ref, m_ref, v_ref, grad_ref, d_ref, m2_ref, v2_ref, all_ref, send_sems, recv_sems):
        x, y, c, me = _place()
        sibling = (x, y, 1 - c)
        chips = [(1 - x, y), (x, 1 - y), (1 - x, 1 - y)]

        def copy(k, block_id, to, from_input):
            return pltpu.make_async_remote_copy(
                src_ref=g_ref if from_input else all_ref.at[block_id], dst_ref=all_ref.at[block_id],
                send_sem=send_sems.at[k], recv_sem=recv_sems.at[k], device_id=to, device_id_type=MESH)

        def block_of(cx, cy, cc):
            return 4 * cx + 2 * cy + cc

        all_ref[me] = g_ref[...]
        first = [copy(0, me, sibling, True)] + [copy(1 + j, me, (*chip, c), True) for j, chip in enumerate(chips)]
        for cp in first:
            cp.start()
        passed = [copy(4 + j, block_of(*chip, c), sibling, False) for j, chip in enumerate(chips)]
        for j, chip in enumerate(chips):
            copy(1 + j, block_of(*chip, c), (x, y, c), False).wait_recv()
            passed[j].start()
        copy(0, block_of(x, y, 1 - c), (x, y, c), False).wait_recv()
        for j, chip in enumerate(chips):
            copy(4 + j, block_of(*chip, 1 - c), (x, y, c), False).wait_recv()
        for cp in first + passed:
            cp.wait_send()
        g = all_ref[0]
        for s in range(1, N_DEV):
            g = g + all_ref[s]
        grad_ref[...] = g
        d_ref[...], m2_ref[...], v2_ref[...] = _adamw(w_ref[...], g, m_ref[...], v_ref[...])

    return pl.pallas_call(
        body, name="allreduce_adamw_small",
        in_specs=[WHOLE] * 4, out_specs=[WHOLE] * 4,
        out_shape=[jax.ShapeDtypeStruct((R, 128), F32)] * 4,
        scratch_shapes=[pltpu.VMEM((N_DEV, R, 128), F32), pltpu.SemaphoreType.DMA((7,)), pltpu.SemaphoreType.DMA((7,))],
        compiler_params=_cparams(),
    )(gpart, w, m, v)


def _shard_view(name, a):
    if name == "w_in":
        return a.transpose(2, 0, 1)
    return a.swapaxes(1, 2) if name in TRANSPOSED else a


def _shard_unview(name, a):
    if name == "w_in":
        return a.transpose(1, 2, 0)
    return a.swapaxes(1, 2) if name in TRANSPOSED else a


def _pad_head_rows(w, width):
    c = w.shape[1]
    return jnp.pad(w.reshape(HEADS, width, c), ((0, 0), (0, HEAD_PAD - width), (0, 0))).reshape(HEADS * HEAD_PAD, c)


def _unpad_head_rows(w, width):
    c = w.shape[1]
    return w.reshape(HEADS, HEAD_PAD, c)[:, :width].reshape(HEADS * width, c)


O1 = Q_LORA
O2 = O1 + KV_LORA
O3 = O2 + QK_ROPE


def _mixer_weights(gw):
    w_in = gw["w_in"].reshape(IN_WIDTH, D_MODEL)
    zero = lambda n: jnp.zeros((n, D_MODEL), BF16)
    win = jnp.concatenate([w_in[:O2], zero(QK_NOPE), w_in[O2:O3], zero(HEAD_PAD - QK_HEAD), w_in[O3:]], axis=0)
    wq = _pad_head_rows(gw["w_q_up"].reshape(HEADS * QK_HEAD, Q_LORA), QK_HEAD)
    w_kv = gw["w_kv_up"].reshape(HEADS, QK_NOPE + V_HEAD, KV_LORA)
    wk = jnp.pad(w_kv[:, :QK_NOPE], ((0, 0), (0, HEAD_PAD - QK_NOPE), (0, 0))).reshape(HEADS * HEAD_PAD, KV_LORA)
    wv = w_kv[:, QK_NOPE:].reshape(HEADS * V_HEAD, KV_LORA)
    wkv = jnp.concatenate([wk, wv], axis=0)
    out = dict(win=win, wq=wq, wkv=wkv)
    if "w_out" in gw:
        out["wout"] = gw["w_out"].reshape(D_MODEL, D_MODEL)
    return out


def _ffn_weights(gw):
    return dict(wg=gw["w_gate"].reshape(FFN_HIDDEN, D_MODEL), wu=gw["w_up"].reshape(FFN_HIDDEN, D_MODEL),
                wd=gw["w_down"].reshape(FFN_HIDDEN, D_MODEL))


def _small_weights(p):
    row = lambda a: a.reshape(DEPTH, 1, -1)
    pad_head = lambda g: jnp.pad(g, ((0, 0), (0, HEAD_PAD - QK_HEAD))).reshape(DEPTH, 1, HEAD_PAD)
    wsp = p["w_spatial"] * jnp.tril(jnp.ones((CHUNK, CHUNK), F32))
    wcat = wsp.transpose(0, 2, 1, 3).reshape(DEPTH, CHUNK, HEADS * CHUNK).astype(BF16)
    wcat_t = wsp.transpose(0, 3, 1, 2).reshape(DEPTH, CHUNK, HEADS * CHUNK).astype(BF16)
    bz = jnp.repeat(p["b_spatial"].transpose(0, 2, 1), SGU_HEAD_DIM, axis=2)
    eye = jnp.eye(HEADS, dtype=F32)
    wp = jnp.einsum("lgcd,gh->lgchd", p["w_pool"], eye).reshape(DEPTH, POOL_WIDTH, POOL_WIDTH).astype(BF16)
    stacked = dict(gmix=row(p["g_mix_norm"]), gql=row(p["g_q_lat"]), gkvl=row(p["g_kv_lat"]),
                   gqh=pad_head(p["g_q_head"]), gkh=pad_head(p["g_k_head"]), gsgu=row(p["g_sgu_v"]),
                   wcat=wcat, wcat_t=wcat_t, bz=bz, wp=wp, pscale=row(p["pool_scale"]),
                   goa=row(p["g_out_mla"]), gos=row(p["g_out_sgu"]), gop=row(p["g_out_pool"]), gffn=row(p["g_ffn_norm"]))
    return [{k: _Layered(a, l) for k, a in stacked.items()} for l in range(DEPTH)]


def _behind(layered, token):
    return _Layered(layered.stacked + token[0, 0], layered.l)


MIXER_SIDE = ("w_in", "w_q_up", "w_kv_up", "w_out")
FFN_SIDE = ("w_gate", "w_up", "w_down")
TRANSPOSED = ("w_in", "w_q_up", "w_kv_up", "w_gate", "w_up")
FIRST_0 = ("w_in", "w_q_up", "w_kv_up")
LATER_0 = ("w_out",) + FFN_SIDE
SHARDED = MIXER_SIDE + FFN_SIDE
SMALL = ("g_mix_norm", "g_q_lat", "g_kv_lat", "g_q_head", "g_k_head", "g_sgu_v", "w_spatial", "b_spatial", "w_pool",
         "pool_scale", "g_out_mla", "g_out_sgu", "g_out_pool", "g_ffn_norm")
WEIGHTS = ("g_mix_norm", "w_in", "g_q_lat", "w_q_up", "g_kv_lat", "w_kv_up", "g_q_head", "g_k_head", "g_sgu_v", "w_spatial",
           "b_spatial", "w_pool", "pool_scale", "g_out_mla", "g_out_sgu", "g_out_pool", "w_out", "g_ffn_norm", "w_gate",
           "w_up", "w_down")
PACKED = SMALL + ("loss",)
PACK_ROWS = 8 * 128


def _pack_small(parts):
    flat = []
    for name in PACKED:
        a = parts[name].reshape(-1)
        flat.append(jnp.pad(a, (0, -a.shape[0] % PACK_ROWS)))
    return jnp.concatenate(flat).reshape(-1, 128)


def _unpack_small(packed, like):
    out, row = {}, 0
    for name in PACKED:
        n = math.prod(like[name].shape)
        rows = -(-n // PACK_ROWS) * 8
        out[name] = packed[row:row + rows].reshape(-1)[:n].reshape(like[name].shape)
        row += rows
    return out


def _forward_attention(x, rope, W, S):
    z, q, k, v, kt, vt = mixin_fwd(x, rope, S["gmix"], W["win"], S["gql"], W["wq"], S["gkvl"], W["wkv"], S["gqh"], S["gkh"])
    a, lse = attn_fwd(q, k, vt)
    return dict(x=x, z=z, q=q, k=k, kt=kt, v=v, a=a, lse=lse)


def _forward_mixers(A, wout, S):
    A["x1"] = mixers_fwd(A["x"], A["a"], A["z"], S["gsgu"], S["wcat"], S["bz"], S["wp"], S["pscale"], S["goa"], S["gos"],
                         S["gop"], wout)
    return A["x1"]


def _backward_ffn(dx2, W, S, A, l):
    dx1, h2, act, dg, du_ffn, dgffn = ffn_bwd(dx2, A["x1"], A["gs"], A["us"], S["gffn"], W["wg"], W["wu"], W["wd"])
    d_wd = matmul_tn(act, dx2, f"dw_down_{l}")
    d_wg = matmul_tn(dg, h2, f"dw_gate_{l}")
    d_wu = matmul_tn(du_ffn, h2, f"dw_up_{l}")
    big = {n: d.reshape(N_DEV, -1, D_MODEL) for n, d in (("w_gate", d_wg), ("w_up", d_wu), ("w_down", d_wd))}
    return dx1, big, dgffn


def _backward_mixers(dx1, dgffn, rope, W, S, A, l, send_wout=None):
    (da, delta, du, dvs, dm, mix, dgoa, dgos, dgop, dps, dwp, dwsp, db, dgsgu) = mixers_bwd(
        dx1, A["a"], A["z"], S["gsgu"], S["wcat"], S["wcat_t"], S["bz"], S["wp"], S["pscale"], S["goa"], S["gos"],
        S["gop"], W["wout"])
    d_wout = matmul_tn(mix, dx1, f"dw_out_{l}")
    if send_wout is not None:
        delta = delta + send_wout(d_wout.reshape(N_DEV, -1, D_MODEL))[0, 0]
    dpin = pool_bwd(dm)
    dqt, dk, dv = attn_bwd(A["q"], A["k"], A["kt"], A["v"], da, A["lse"], delta)
    (dx, hn, dz, qn, dq_up, kvn, dkv_up, dgmix, dgql, dgkvl, dgqh, dgkh) = mixin_bwd(
        dx1, A["x"], A["z"], rope, dqt, dk, dv, du, dvs, dpin, S["gmix"], W["win"], S["gql"], W["wq"], S["gkvl"],
        W["wkv"], S["gqh"], S["gkh"])
    d_win = matmul_tn(dz, hn, f"dw_in_{l}")
    d_wq = matmul_tn(dq_up, qn, f"dw_q_up_{l}")
    d_wkv = matmul_tn(dkv_up, kvn, f"dw_kv_up_{l}")
    d_win = jnp.concatenate([d_win[:O2], d_win[O2 + QK_NOPE:O2 + QK_HEAD], d_win[O2 + HEAD_PAD:]], axis=0)
    d_wk = d_wkv[:HEADS * HEAD_PAD].reshape(HEADS, HEAD_PAD, KV_LORA)[:, :QK_NOPE]
    d_wv = d_wkv[HEADS * HEAD_PAD:].reshape(HEADS, V_HEAD, KV_LORA)
    d_wkv = jnp.concatenate([d_wk, d_wv], axis=1)
    big = dict(w_in=d_win.reshape(N_DEV, -1, D_MODEL), w_q_up=_unpad_head_rows(d_wq, QK_HEAD).reshape(N_DEV, -1, Q_LORA),
               w_kv_up=d_wkv.reshape(N_DEV, -1, KV_LORA), w_out=d_wout.reshape(N_DEV, -1, D_MODEL))
    tril = jnp.tril(jnp.ones((CHUNK, CHUNK), F32))
    small = dict(g_mix_norm=dgmix[0], g_q_lat=dgql[0], g_kv_lat=dgkvl[0], g_q_head=dgqh[0, :QK_HEAD], g_k_head=dgkh[0, :QK_HEAD],
                 g_sgu_v=dgsgu[0], w_spatial=dwsp.reshape(HEADS, CHUNK, CHUNK) * tril, b_spatial=db[:, :HEADS].T,
                 w_pool=jnp.stack([dwp[64 * g:64 * (g + 1), 64 * g:64 * (g + 1)] for g in range(HEADS)]),
                 pool_scale=dps[0], g_out_mla=dgoa[0], g_out_sgu=dgos[0], g_out_pool=dgop[0], g_ffn_norm=dgffn[0])
    return dx, big, small


def kernel(x, positions, g_mix_norm, w_in, g_q_lat, w_q_up, g_kv_lat, w_kv_up, g_q_head, g_k_head, g_sgu_v, w_spatial, b_spatial, w_pool, pool_scale, g_out_mla, g_out_sgu, g_out_pool, w_out, g_ffn_norm, w_gate, w_up, w_down, loss_target, m_g_mix_norm, m_w_in, m_g_q_lat, m_w_q_up, m_g_kv_lat, m_w_kv_up, m_g_q_head, m_g_k_head, m_g_sgu_v, m_w_spatial, m_b_spatial, m_w_pool, m_pool_scale, m_g_out_mla, m_g_out_sgu, m_g_out_pool, m_w_out, m_g_ffn_norm, m_w_gate, m_w_up, m_w_down, v_g_mix_norm, v_w_in, v_g_q_lat, v_w_q_up, v_g_kv_lat, v_w_kv_up, v_g_q_head, v_g_k_head, v_g_sgu_v, v_w_spatial, v_b_spatial, v_w_pool, v_pool_scale, v_g_out_mla, v_g_out_sgu, v_g_out_pool, v_w_out, v_g_ffn_norm, v_w_gate, v_w_up, v_w_down):
    given = dict(locals())
    w = {n: given[n] for n in WEIGHTS}
    m = {n: given["m_" + n] for n in WEIGHTS}
    v = {n: given["v_" + n] for n in WEIGHTS}
    T = x.shape[1]
    xs = x.reshape(T, D_MODEL)

    half = QK_ROPE // 2
    inv_freq = 1.0 / (ROPE_THETA ** (jnp.arange(half, dtype=F32) / half))
    ang16 = positions.reshape(T).astype(F32)[:, None] * inv_freq
    ang = jnp.concatenate([jnp.zeros((T, QK_NOPE), F32), ang16, ang16, jnp.zeros((T, HEAD_PAD - QK_HEAD), F32)], axis=1)

    wv = {n: _shard_view(n, w[n]) for n in SHARDED}
    wanted = [(SHARDED.index(n), 0) for n in FIRST_0 + LATER_0] + [(i, 1) for i in range(len(SHARDED))]
    bf = cast_shards([wv[n] for n in SHARDED], wanted)
    nf, nl = len(FIRST_0), len(LATER_0)
    ag_first0 = exchange_start(bf[:nf], ang, True, "ag_start_first0")
    rope = rope_tables(ang + ag_first0[-1][0, 0])
    W0 = dict(zip(FIRST_0, exchange_wait(ag_first0, rope[0], "ag_wait_first0")))
    ag_later0 = exchange_start(bf[nf:nf + nl], W0["w_in"], True, "ag_start_later0")
    ag_l1 = exchange_start(bf[nf + nl:], ag_later0[-1], True, "ag_start_l1")
    Ss = _small_weights(w)

    S0 = dict(Ss[0], gmix=_behind(Ss[0]["gmix"], ag_later0[-1] + ag_l1[-1]))
    A0 = _forward_attention(xs, rope, _mixer_weights(W0), S0)
    W0.update(zip(LATER_0, exchange_wait(ag_later0, A0["a"], "ag_wait_later0")))
    W0 = {**_mixer_weights(W0), **_ffn_weights(W0)}
    x1 = _forward_mixers(A0, W0["wout"], Ss[0])
    h, A0["gs"], A0["us"] = ffn_fwd(x1, Ss[0]["gffn"], W0["wg"], W0["wu"], W0["wd"])
    layer1 = dict(zip(SHARDED, exchange_wait(ag_l1, h, "ag_wait_l1")))
    W1 = {**_mixer_weights(layer1), **_ffn_weights(layer1)}
    A1 = _forward_attention(h, rope, W1, Ss[1])
    x1 = _forward_mixers(A1, W1["wout"], Ss[1])
    dh, loss_part, A1["gs"], A1["us"] = ffn_fwd(x1, Ss[1]["gffn"], W1["wg"], W1["wu"], W1["wd"],
                                                target=loss_target.reshape(T, D_MODEL))

    dx1, big_f1, dgffn1 = _backward_ffn(dh, W1, Ss[1], A1, 1)
    dh, big_m1, small1 = _backward_mixers(dx1, dgffn1, rope, W1, Ss[1], A1, 1)
    rs_l1 = exchange_start([{**big_m1, **big_f1}[n] for n in SHARDED], dh, False, "rs_start_l1")
    S0 = dict(Ss[0], gffn=_behind(Ss[0]["gffn"], rs_l1[-1]))
    dx1, big_f0, dgffn0 = _backward_ffn(dh, W0, S0, A0, 0)
    rs_ffn0 = exchange_start([big_f0[n] for n in FFN_SIDE], dx1, False, "rs_start_ffn0")
    S0 = dict(Ss[0], gsgu=_behind(Ss[0]["gsgu"], rs_ffn0[-1]))
    rs_out0 = []

    def send_wout(pieces):
        rs_out0.append(exchange_start([pieces], dx1, False, "rs_start_out0"))
        return rs_out0[0][-1]

    dh, big_m0, small0 = _backward_mixers(dx1, dgffn0, rope, W0, S0, A0, 0, send_wout)
    grad_x = dh.reshape(x.shape)
    smalls = [small0, small1]

    rs_first0 = exchange_start([big_m0[n] for n in FIRST_0], dh, False, "rs_start_first0")
    parts1 = dict(zip(SHARDED, exchange_wait(rs_l1, rs_first0[-1], "rs_wait_l1")))
    parts0 = dict(zip(FFN_SIDE, exchange_wait(rs_ffn0, rs_first0[-1], "rs_wait_ffn0")))
    parts0["w_out"] = exchange_wait(rs_out0[0], rs_first0[-1], "rs_wait_out0")[0]
    grad, delta, new_m, new_v = {}, {}, {}, {}

    small_part = {n: jnp.stack([smalls[l][n] for l in range(DEPTH)]) for n in SMALL}
    small_part["loss"] = loss_part
    no_loss = {"loss": jnp.zeros((1, 1), F32)}
    outs = allreduce_adamw_small(_pack_small(small_part), *[_pack_small({**a, **no_loss}) for a in (w, m, v)])
    for d, o in zip((grad, delta, new_m, new_v), outs):
        d.update(_unpack_small(o, {**w, **no_loss}))
    loss = grad["loss"][0, 0]

    for n in LATER_0 + FIRST_0:
        if n == FIRST_0[0]:
            parts0.update(zip(FIRST_0, exchange_wait(rs_first0, grad["w_down"], "rs_wait_first0")))
        parts, views = [parts0[n], parts1[n]], [_shard_view(n, a[n]) for a in (w, m, v)]
        if n == "w_in":
            parts = [jnp.stack(parts, axis=2).reshape(N_DEV, -1, D_MODEL)]
            views = [a.reshape(1, -1, D_MODEL) for a in views]
        outs = adamw_sharded(parts, *views, f"adamw_{n}")
        if n == "w_in":
            outs = [o.reshape(-1, DEPTH, D_MODEL) for o in outs]
        grad[n], delta[n], new_m[n], new_v[n] = [_shard_unview(n, o) for o in outs]

    return (loss, grad_x, *[grad[n] for n in WEIGHTS], *[delta[n] for n in WEIGHTS], *[new_m[n] for n in WEIGHTS],
            *[new_v[n] for n in WEIGHTS])
```

```python
import math

import jax
import jax.numpy as jnp
from jax import lax
from jax.experimental import pallas as pl
from jax.experimental.pallas import tpu as pltpu

F32 = jnp.float32
BF16 = jnp.bfloat16

N_DEV = 8
DEPTH = 2
D_MODEL = 1024
HEADS = 4
HEAD_PAD = 128
QK_NOPE = 64
QK_ROPE = 32
QK_HEAD = QK_NOPE + QK_ROPE
V_HEAD = 128
Q_LORA = 256
KV_LORA = 128
SGU_WIDTH = 256
SGU_HEAD_DIM = 64
CHUNK = 128
POOL_WIDTH = 256
POOL_HALO = 16
MLA_WIDTH = 512
IN_WIDTH = 1184
Z_WIDTH = 1280
FFN_HIDDEN = 2816
FFN_CHUNK = 256
ROPE_THETA = 10000.0
EPS = 1e-6
ATTN_SCALE = 1.0 / math.sqrt(QK_HEAD)
LOG2E = 1.4426950408889634
NEG_BIG = -1e30

ADAM_LR = 0.001
ADAM_B1 = 0.9
ADAM_B2 = 0.999
ADAM_EPS = 1e-08
ADAM_WD = 0.01
ADAM_STEP = 10

VMEM_LIMIT = 56 * 1024 * 1024
ROW_TILE = 512
MIXIN_PART = 256
MIXIN_BWD_PART = 128
ATTN_TILE = 1024
ATTN_SUB = 512
ATTN_FWD_SUB = 512
ATTN_BWD_HEADS = 2
MESH = pl.DeviceIdType.MESH

WHOLE = pl.BlockSpec(memory_space=pltpu.VMEM)
ANY = pl.BlockSpec(memory_space=pl.ANY)
HBM_SPEC = pl.BlockSpec(memory_space=pltpu.HBM)
SEM_SPEC = pl.BlockSpec(memory_space=pltpu.SEMAPHORE)


def _cparams(**kw):
    return pltpu.CompilerParams(vmem_limit_bytes=VMEM_LIMIT, **kw)


def _dot(a, b):
    return jnp.dot(a, b, preferred_element_type=F32)


def _dot_nt(a, b):
    return lax.dot_general(a, b, (((1,), (1,)), ((), ())), preferred_element_type=F32)


def _dot_tn(a, b):
    return lax.dot_general(a, b, (((0,), (0,)), ((), ())), preferred_element_type=F32)


def _rms(x, g, n):
    r = lax.rsqrt(jnp.sum(x * x, axis=-1, keepdims=True) * (1.0 / n) + EPS)
    return x * r * g, r


def _rms_bwd(x, r, g, dy, n):
    gdy = dy * g
    dx = r * gdy - x * (r * r * r) * (jnp.sum(x * gdy, axis=-1, keepdims=True) * (1.0 / n))
    dg = jnp.sum(dy * (x * r), axis=0, keepdims=True)
    return dx, dg


def _sigmoid(x):
    return 1.0 / (1.0 + jnp.exp(-x))


def rope_tables(ang):
    T = ang.shape[0]
    tm = min(4 * ROW_TILE, T)

    def body(ang_ref, c_ref, sa_ref, sb_ref):
        a = ang_ref[...]
        lane = lax.broadcasted_iota(jnp.int32, a.shape, 1)
        s = jnp.sin(a)
        c_ref[...] = jnp.cos(a)
        sa_ref[...] = jnp.where(lane < QK_NOPE + QK_ROPE // 2, -s, 0.0)
        sb_ref[...] = jnp.where(lane >= QK_NOPE + QK_ROPE // 2, s, 0.0)

    row = pl.BlockSpec((tm, HEAD_PAD), lambda i: (i, 0))
    return pl.pallas_call(
        body, grid=(T // tm,), name="rope_tables", in_specs=[row], out_specs=[row] * 3,
        out_shape=[jax.ShapeDtypeStruct((T, HEAD_PAD), F32)] * 3, compiler_params=_cparams(),
    )(ang)


def _rope(x, c, sa, sb):
    half = QK_ROPE // 2
    return x * c + pltpu.roll(x, HEAD_PAD - half, 1) * sa + pltpu.roll(x, half, 1) * sb


def _rope_bwd(dy, c, sa, sb):
    half = QK_ROPE // 2
    return dy * c + pltpu.roll(dy * sa, half, 1) + pltpu.roll(dy * sb, HEAD_PAD - half, 1)


def _skewed(stages, groups):
    for t in range(len(stages) + len(groups) - 1):
        for p, g in enumerate(groups):
            if 0 <= t - p < len(stages):
                stages[t - p](g)


def _head_masks(shape, width):
    lane = lax.broadcasted_iota(jnp.int32, shape, len(shape) - 1)
    return [(lane >= width * h) & (lane < width * (h + 1)) for h in range(HEADS)]


def _qkv_pre(z, gql, wq, gkvl, wkv):
    ql = z[:, 0:Q_LORA]
    kvl = z[:, Q_LORA:Q_LORA + KV_LORA]
    kr = z[:, Q_LORA + KV_LORA:Q_LORA + KV_LORA + HEAD_PAD]
    qn, rq = _rms(ql, gql, Q_LORA)
    kvn, rkv = _rms(kvl, gkvl, KV_LORA)
    qn = qn.astype(BF16)
    kvn = kvn.astype(BF16)
    q_up = _dot_nt(qn, wq)
    kv_up = _dot_nt(kvn, wkv)
    return ql, kvl, kr, qn, rq, kvn, rkv, q_up, kv_up


def mixin_fwd(x, rope, gmix, win, gql, wq, gkvl, wkv, gqh, gkh):
    T = x.shape[0]
    tm = min(2 * ROW_TILE, T)
    part = min(MIXIN_PART, tm)

    def body(x_ref, c_ref, sa_ref, sb_ref, gmix_ref, win_ref, gql_ref, wq_ref, gkvl_ref, wkv_ref, gqh_ref, gkh_ref,
             z_ref, q_ref, k_ref, v_ref, kt_ref, vt_ref):
        def project(g):
            hn = _rms(x_ref[g["rows"], :], gmix_ref[...], D_MODEL)[0].astype(BF16)
            g["z"] = _dot_nt(hn, win_ref[...])
            z_ref[g["rows"], :] = g["z"]

        def latents(g):
            _, _, g["kr"], _, _, _, _, g["q_up"], g["kv_up"] = _qkv_pre(g["z"], gql_ref[...], wq_ref[...], gkvl_ref[...],
                                                                        wkv_ref[...])

        def heads(g):
            rows = g["rows"]
            c, sa, sb = c_ref[rows, :], sa_ref[rows, :], sb_ref[rows, :]
            for h in range(HEADS):
                lo = HEAD_PAD * h
                qh = _rms(g["q_up"][:, lo:lo + HEAD_PAD], gqh_ref[...], QK_HEAD)[0]
                q_ref[h, rows, :] = (_rope(qh, c, sa, sb) * (ATTN_SCALE * LOG2E)).astype(BF16)
                kh = _rope(_rms(g["kv_up"][:, lo:lo + HEAD_PAD] + g["kr"], gkh_ref[...], QK_HEAD)[0], c, sa, sb)
                k_ref[h, rows, :] = kh.astype(BF16)
                kt_ref[h, :, rows] = jnp.transpose(kh).astype(BF16)
                vh = g["kv_up"][:, HEADS * HEAD_PAD + lo:HEADS * HEAD_PAD + lo + HEAD_PAD]
                v_ref[h, rows, :] = vh.astype(BF16)
                vt_ref[h, :, rows] = jnp.transpose(vh).astype(BF16)

        _skewed((project, latents, heads), [dict(rows=pl.ds(part * p, part)) for p in range(tm // part)])

    row = lambda w: pl.BlockSpec((tm, w), lambda i: (i, 0))
    head = pl.BlockSpec((HEADS, tm, HEAD_PAD), lambda i: (0, i, 0))
    head_t = pl.BlockSpec((HEADS, HEAD_PAD, tm), lambda i: (0, 0, i))
    return pl.pallas_call(
        body, grid=(T // tm,), name="mixin_fwd",
        in_specs=[row(D_MODEL)] + [row(HEAD_PAD)] * 3 + [WHOLE] * 8,
        out_specs=[row(Z_WIDTH), head, head, head, head_t, head_t],
        out_shape=[jax.ShapeDtypeStruct((T, Z_WIDTH), F32)] + [jax.ShapeDtypeStruct((HEADS, T, HEAD_PAD), BF16)] * 3
                  + [jax.ShapeDtypeStruct((HEADS, HEAD_PAD, T), BF16)] * 2,
        compiler_params=_cparams(),
    )(x, *rope, gmix, win, gql, wq, gkvl, wkv, gqh, gkh)


def _pair_tables(pairs):
    return (jnp.asarray([p[0] for p in pairs], jnp.int32), jnp.asarray([p[1] for p in pairs], jnp.int32))


def attn_fwd(q, k, vt):
    _, T, _ = q.shape
    tb = min(ATTN_TILE, T)
    sq = min(ATTN_FWD_SUB, tb)
    nb = T // tb

    pairs = [(i, j) for i in range(nb) for j in range(i + 1)]

    def body(i_tab, j_tab, q_ref, k_ref, vt_ref, o_ref, lse_ref, m_s, l_s, acc_s):
        i, j = i_tab[pl.program_id(0)], j_tab[pl.program_id(0)]

        @pl.when(j == 0)
        def _():
            m_s[...] = jnp.full(m_s.shape, -jnp.inf, F32)
            l_s[...] = jnp.zeros(l_s.shape, F32)
            acc_s[...] = jnp.zeros(acc_s.shape, F32)

        def scores(g):
            st = _dot_nt(k_ref[g["h"], g["kb"], :], q_ref[g["h"], g["qa"], :])
            if g["masked"]:
                krow = lax.broadcasted_iota(jnp.int32, st.shape, 0)
                qcol = g["q0"] + lax.broadcasted_iota(jnp.int32, st.shape, 1)
                st = jnp.where(krow <= qcol, st, NEG_BIG)
            g["st"] = st

        def new_max(g):
            h, qa = g["h"], g["qa"]
            m_prev = m_s[h, :, qa]
            g["m"] = jnp.maximum(m_prev, jnp.max(g["st"], axis=0, keepdims=True))
            g["alpha"] = jnp.exp2(m_prev - g["m"])
            m_s[h, :, qa] = g["m"]

        def weights(g):
            pt = jnp.exp2(g["st"] - g["m"])
            g["lsum"] = jnp.sum(pt, axis=0, keepdims=True)
            g["pt"] = pt.astype(BF16)

        def accumulate(g):
            h, qa = g["h"], g["qa"]
            l_s[h, :, qa] = g["alpha"] * l_s[h, :, qa] + g["lsum"]
            acc_s[h, :, qa] = g["alpha"] * acc_s[h, :, qa] + _dot(vt_ref[h, :, g["kb"]], g["pt"])

        def tiles(masked):
            return [dict(h=h, q0=q0, qa=slice(q0, q0 + sq), kb=slice(0, q0 + sq if masked else tb), masked=masked)
                    for h in range(HEADS) for q0 in range(0, tb, sq)]

        @pl.when(j < i)
        def _():
            _skewed((scores, new_max, weights, accumulate), tiles(False))

        @pl.when(j == i)
        def _():
            _skewed((scores, new_max, weights, accumulate), tiles(True))
            for h in range(HEADS):
                l = l_s[h]
                o_ref[:, HEAD_PAD * h:HEAD_PAD * (h + 1)] = jnp.transpose(acc_s[h] / l)
                lse_ref[h] = m_s[h] + jnp.log2(l)

    qspec = pl.BlockSpec((HEADS, tb, HEAD_PAD), lambda p, it, jt: (0, it[p], 0))
    kspec = pl.BlockSpec((HEADS, tb, HEAD_PAD), lambda p, it, jt: (0, jt[p], 0))
    vspec = pl.BlockSpec((HEADS, HEAD_PAD, tb), lambda p, it, jt: (0, 0, jt[p]))
    grid_spec = pltpu.PrefetchScalarGridSpec(
        num_scalar_prefetch=2, grid=(len(pairs),),
        in_specs=[qspec, kspec, vspec],
        out_specs=[pl.BlockSpec((tb, MLA_WIDTH), lambda p, it, jt: (it[p], 0)),
                   pl.BlockSpec((HEADS, 1, tb), lambda p, it, jt: (0, 0, it[p]))],
        scratch_shapes=[pltpu.VMEM((HEADS, 1, tb), F32), pltpu.VMEM((HEADS, 1, tb), F32), pltpu.VMEM((HEADS, HEAD_PAD, tb), F32)])
    return pl.pallas_call(
        body, grid_spec=grid_spec, name="attn_fwd",
        out_shape=[jax.ShapeDtypeStruct((T, MLA_WIDTH), F32), jax.ShapeDtypeStruct((HEADS, 1, T), F32)],
        compiler_params=_cparams(),
    )(*_pair_tables(pairs), q, k, vt)


def _sgu_fwd_chunk(vn_c, wcat, bz, masks):
    vstack = jnp.concatenate([jnp.where(mk, vn_c, 0.0).astype(BF16) for mk in masks], axis=0)
    return _dot(wcat, vstack) + bz


def _pool_counts(i, tm):
    pos1 = (i * tm + 1 + lax.broadcasted_iota(jnp.int32, (tm, POOL_WIDTH), 0)).astype(F32)
    lane = lax.broadcasted_iota(jnp.int32, (tm, POOL_WIDTH), 1)
    win = jnp.where(lane < 64, 2.0, jnp.where(lane < 128, 4.0, jnp.where(lane < 192, 8.0, 16.0)))
    return jnp.minimum(pos1, win), lane


def _by_group(lane, s2, s4, s8, s16):
    return jnp.where(lane < 64, s2, jnp.where(lane < 128, s4, jnp.where(lane < 192, s8, s16)))


def _pool_means(pin, halo, i, tm):
    s1 = jnp.concatenate([halo, pin], axis=0)
    s2 = s1 + pltpu.roll(s1, 1, 0)
    s4 = s2 + pltpu.roll(s2, 2, 0)
    s8 = s4 + pltpu.roll(s4, 4, 0)
    s16 = s8 + pltpu.roll(s8, 8, 0)
    cnt, lane = _pool_counts(i, tm)
    sel = _by_group(lane, s2[POOL_HALO:], s4[POOL_HALO:], s8[POOL_HALO:], s16[POOL_HALO:])
    return sel / cnt - pin


def _mixers_fwd_tile(i, tm, a, u, vs, pin, halo, gsgu, wcat, bz, wp, pscale, goa, gos, gop):
    vn, rv = _rms(vs, gsgu, SGU_WIDTH)
    masks = _head_masks((CHUNK, SGU_WIDTH), SGU_HEAD_DIM)
    zc = jnp.concatenate([_sgu_fwd_chunk(vn[CHUNK * c:CHUNK * (c + 1)], wcat, bz, masks) for c in range(tm // CHUNK)], axis=0)
    gm = u * zc
    halo = jnp.where(i > 0, halo, 0.0)
    m = _pool_means(pin, halo, i, tm).astype(BF16)
    yp_pre = _dot(m, wp)
    yp = yp_pre * pscale
    na, ra = _rms(a, goa, MLA_WIDTH)
    ng, rg = _rms(gm, gos, SGU_WIDTH)
    npo, rp = _rms(yp, gop, POOL_WIDTH)
    mix = jnp.concatenate([na, ng, npo], axis=1).astype(BF16)
    return vn, rv, zc, gm, m, yp_pre, yp, ra, rg, rp, mix


def _z_specs(tm):
    col = lambda c: pl.BlockSpec((tm, 256), lambda i: (i, c))
    halo = pl.BlockSpec((POOL_HALO, 256), lambda i: (jnp.maximum(i * (tm // POOL_HALO) - 1, 0), 4))
    return [col(2), col(3), col(4), halo]


def mixers_fwd(x, a, z, gsgu, wcat, bz, wp, pscale, goa, gos, gop, wout):
    T = x.shape[0]
    tm = min(2 * ROW_TILE, T)

    def body(x_ref, a_ref, u_ref, vs_ref, pin_ref, halo_ref, gsgu_ref, wcat_ref, bz_ref, wp_ref, ps_ref,
             goa_ref, gos_ref, gop_ref, wout_ref, x1_ref):
        i = pl.program_id(0)
        mix = _mixers_fwd_tile(i, tm, a_ref[...], u_ref[...], vs_ref[...], pin_ref[...], halo_ref[...], gsgu_ref[...],
                               wcat_ref[...], bz_ref[...], wp_ref[...], ps_ref[...], goa_ref[...], gos_ref[...],
                               gop_ref[...])[-1]
        x1_ref[...] = x_ref[...] + _dot(mix, wout_ref[...])

    row = lambda w: pl.BlockSpec((tm, w), lambda i: (i, 0))
    return pl.pallas_call(
        body, grid=(T // tm,), name="mixers_fwd",
        in_specs=[row(D_MODEL), row(MLA_WIDTH)] + _z_specs(tm) + [WHOLE] * 9,
        out_specs=row(D_MODEL),
        out_shape=jax.ShapeDtypeStruct((T, D_MODEL), F32),
        compiler_params=_cparams(),
    )(x, a, z, z, z, z, gsgu, wcat, bz, wp, pscale, goa, gos, gop, wout)


def ffn_fwd(x1, gffn, wg, wu, wd, target=None):
    T = x1.shape[0]
    tm = min(ROW_TILE, T)
    with_loss = target is not None

    def body(*refs):
        x1_ref, gffn_ref, wg_ref, wu_ref, wd_ref = refs[:5]
        outs = refs[6:] if with_loss else refs[5:]
        x1v = x1_ref[...]
        h2 = _rms(x1v, gffn_ref[...], D_MODEL)[0].astype(BF16)
        acc = x1v
        for c in range(FFN_HIDDEN // FFN_CHUNK):
            sl = slice(FFN_CHUNK * c, FFN_CHUNK * (c + 1))
            g = _dot_nt(h2, wg_ref[sl, :])
            u = _dot_nt(h2, wu_ref[sl, :])
            outs[-2][:, sl] = g.astype(BF16)
            outs[-1][:, sl] = u.astype(BF16)
            act = (g * _sigmoid(g) * u).astype(BF16)
            acc = acc + _dot(act, wd_ref[sl, :])
        if not with_loss:
            outs[0][...] = acc
            return
        dy_ref, loss_ref = outs[0], outs[1]

        @pl.when(pl.program_id(0) == 0)
        def _():
            loss_ref[...] = jnp.zeros(loss_ref.shape, F32)

        err = acc - refs[5][...]
        dy_ref[...] = err * (1.0 / D_MODEL)
        per_row = jnp.sum(err * err, axis=1, keepdims=True) * (1.0 / D_MODEL)
        loss_ref[...] += 0.5 * jnp.sum(per_row, axis=0, keepdims=True)

    row = lambda w: pl.BlockSpec((tm, w), lambda i: (i, 0))
    hidden = [jax.ShapeDtypeStruct((T, FFN_HIDDEN), BF16)] * 2
    if with_loss:
        return pl.pallas_call(
            body, grid=(T // tm,), name="ffn_fwd_loss",
            in_specs=[row(D_MODEL)] + [WHOLE] * 4 + [row(D_MODEL)],
            out_specs=[row(D_MODEL), pl.BlockSpec((1, 1), lambda i: (0, 0)), row(FFN_HIDDEN), row(FFN_HIDDEN)],
            out_shape=[jax.ShapeDtypeStruct((T, D_MODEL), F32), jax.ShapeDtypeStruct((1, 1), F32)] + hidden,
            compiler_params=_cparams(),
        )(x1, gffn, wg, wu, wd, target)
    return pl.pallas_call(
        body, grid=(T // tm,), name="ffn_fwd",
        in_specs=[row(D_MODEL)] + [WHOLE] * 4,
        out_specs=[row(D_MODEL), row(FFN_HIDDEN), row(FFN_HIDDEN)],
        out_shape=[jax.ShapeDtypeStruct((T, D_MODEL), F32)] + hidden,
        compiler_params=_cparams(),
    )(x1, gffn, wg, wu, wd)


def _acc_spec(shape):
    return pl.BlockSpec(shape, lambda i: (0,) * len(shape))


def ffn_bwd(dx2, x1, gs, us, gffn, wg, wu, wd):
    T = x1.shape[0]
    tm = min(ROW_TILE // 2, T)

    def body(dx2_ref, x1_ref, gs_ref, us_ref, gffn_ref, wg_ref, wu_ref, wd_ref,
             dx1_ref, h2_ref, act_ref, dg_ref, du_ref, dgffn_ref):
        @pl.when(pl.program_id(0) == 0)
        def _():
            dgffn_ref[...] = jnp.zeros(dgffn_ref.shape, F32)

        dx2v = dx2_ref[...]
        dy = dx2v.astype(BF16)
        x1v = x1_ref[...]
        h2, r = _rms(x1v, gffn_ref[...], D_MODEL)
        h2_ref[...] = h2.astype(BF16)
        for c in range(FFN_HIDDEN // FFN_CHUNK):
            sl = slice(FFN_CHUNK * c, FFN_CHUNK * (c + 1))
            g = gs_ref[:, sl].astype(F32)
            u = us_ref[:, sl].astype(F32)
            dact = _dot_nt(dy, wd_ref[sl, :])
            sg = _sigmoid(g)
            silu = g * sg
            act_ref[:, sl] = (silu * u).astype(BF16)
            dg_ref[:, sl] = (dact * u * (sg * (1.0 + g * (1.0 - sg)))).astype(BF16)
            du_ref[:, sl] = (dact * silu).astype(BF16)
        dh2 = _dot(dg_ref[...], wg_ref[...]) + _dot(du_ref[...], wu_ref[...])
        dxn, dgn = _rms_bwd(x1v, r, gffn_ref[...], dh2, D_MODEL)
        dx1_ref[...] = dx2v + dxn
        dgffn_ref[...] += dgn

    row = lambda w: pl.BlockSpec((tm, w), lambda i: (i, 0))
    return pl.pallas_call(
        body, grid=(T // tm,), name="ffn_bwd",
        in_specs=[row(D_MODEL), row(D_MODEL), row(FFN_HIDDEN), row(FFN_HIDDEN)] + [WHOLE] * 4,
        out_specs=[row(D_MODEL), row(D_MODEL), row(FFN_HIDDEN), row(FFN_HIDDEN), row(FFN_HIDDEN), _acc_spec((1, D_MODEL))],
        out_shape=[jax.ShapeDtypeStruct((T, D_MODEL), F32), jax.ShapeDtypeStruct((T, D_MODEL), BF16),
                   jax.ShapeDtypeStruct((T, FFN_HIDDEN), BF16), jax.ShapeDtypeStruct((T, FFN_HIDDEN), BF16),
                   jax.ShapeDtypeStruct((T, FFN_HIDDEN), BF16), jax.ShapeDtypeStruct((1, D_MODEL), F32)],
        compiler_params=_cparams(),
    )(dx2, x1, gs, us, gffn, wg, wu, wd)


TN_K_TILE = 2048
TN_ACC_BYTES = 6 * 1024 * 1024


def matmul_tn(a, b, name):
    T, M = a.shape
    N = b.shape[1]
    tk = min(TN_K_TILE, T)
    tm = M if M <= 1024 else M // 2
    tn = max(d for d in range(128, N + 1, 128) if N % d == 0 and tm * d * 4 <= TN_ACC_BYTES)
    nk = T // tk

    def body(a_ref, b_ref, o_ref, acc):
        k = pl.program_id(2)

        @pl.when(k == 0)
        def _():
            acc[...] = jnp.zeros(acc.shape, F32)

        acc[...] += _dot_tn(a_ref[...].astype(BF16), b_ref[...].astype(BF16))

        @pl.when(k == nk - 1)
        def _():
            o_ref[...] = acc[...].astype(BF16)

    return pl.pallas_call(
        body, grid=(M // tm, N // tn, nk), name=name,
        in_specs=[pl.BlockSpec((tk, tm), lambda i, j, k: (k, i)), pl.BlockSpec((tk, tn), lambda i, j, k: (k, j))],
        out_specs=pl.BlockSpec((tm, tn), lambda i, j, k: (i, j)),
        out_shape=jax.ShapeDtypeStruct((M, N), BF16),
        scratch_shapes=[pltpu.VMEM((tm, tn), F32)],
        compiler_params=_cparams(),
    )(a, b)


def mixers_bwd(dx1, a, z, gsgu, wcat, wcat_t, bz, wp, pscale, goa, gos, gop, wout):
    T = a.shape[0]
    tm = min(2 * ROW_TILE, T)

    def body(dx1_ref, a_ref, u_ref, vs_ref, pin_ref, halo_ref, gsgu_ref, wcat_ref, wcatt_ref, bz_ref, wp_ref,
             ps_ref, goa_ref, gos_ref, gop_ref, wout_ref,
             da_ref, delta_ref, du_ref, dvs_ref, dm_ref, mix_ref,
             dgoa_ref, dgos_ref, dgop_ref, dps_ref, dwp_ref, dwsp_ref, db_ref, dgsgu_ref):
        i = pl.program_id(0)

        @pl.when(i == 0)
        def _():
            for r in (dgoa_ref, dgos_ref, dgop_ref, dps_ref, dwp_ref, dwsp_ref, db_ref, dgsgu_ref):
                r[...] = jnp.zeros(r.shape, F32)

        a_v, u, vs = a_ref[...], u_ref[...], vs_ref[...]
        goa, gos, gop, pscale_v = goa_ref[...], gos_ref[...], gop_ref[...], ps_ref[...]
        vn, rv, zc, gm, m, yp_pre, yp, ra, rg, rp, mix = _mixers_fwd_tile(
            i, tm, a_v, u, vs, pin_ref[...], halo_ref[...], gsgu_ref[...], wcat_ref[...], bz_ref[...], wp_ref[...],
            pscale_v, goa, gos, gop)
        mix_ref[...] = mix
        dmix = _dot_nt(dx1_ref[...].astype(BF16), wout_ref[...])
        da, dgoa = _rms_bwd(a_v, ra, goa, dmix[:, :MLA_WIDTH], MLA_WIDTH)
        dgm, dgos = _rms_bwd(gm, rg, gos, dmix[:, MLA_WIDTH:MLA_WIDTH + SGU_WIDTH], SGU_WIDTH)
        dyp, dgop = _rms_bwd(yp, rp, gop, dmix[:, MLA_WIDTH + SGU_WIDTH:], POOL_WIDTH)
        da_ref[...] = da
        dgoa_ref[...] += dgoa
        dgos_ref[...] += dgos
        dgop_ref[...] += dgop
        prod = da * a_v
        ones = jnp.ones((8, HEAD_PAD), F32)
        for h in range(HEADS):
            lo = HEAD_PAD * h
            sums = lax.dot_general(ones, prod[:, lo:lo + HEAD_PAD], (((1,), (1,)), ((), ())), preferred_element_type=F32,
                                   precision=lax.Precision.HIGHEST)
            delta_ref[h] = sums[0:1, :]
        dps_ref[...] += jnp.sum(dyp * yp_pre, axis=0, keepdims=True)
        dyp_pre = (dyp * pscale_v).astype(BF16)
        dwp_ref[...] += _dot_tn(m, dyp_pre)
        dm_ref[...] = _dot_nt(dyp_pre, wp_ref[...])
        du_ref[...] = dgm * zc
        dzc = dgm * u
        masks = _head_masks((CHUNK, SGU_WIDTH), SGU_HEAD_DIM)
        lane_b = lax.broadcasted_iota(jnp.int32, (CHUNK, HEAD_PAD), 1)
        dvn_parts = []
        dwsp = jnp.zeros(dwsp_ref.shape, F32)
        db = jnp.zeros(db_ref.shape, F32)
        for c in range(tm // CHUNK):
            dz_c = dzc[CHUNK * c:CHUNK * (c + 1)]
            dzstack = jnp.concatenate([jnp.where(mk, dz_c, 0.0).astype(BF16) for mk in masks], axis=0)
            dvn_parts.append(_dot(wcatt_ref[...], dzstack))
            dwsp = dwsp + _dot_nt(dzstack, vn[CHUNK * c:CHUNK * (c + 1)].astype(BF16))
            for h, mk in enumerate(masks):
                col = jnp.sum(jnp.where(mk, dz_c, 0.0), axis=1, keepdims=True)
                db = db + jnp.where(lane_b == h, col, 0.0)
        dwsp_ref[...] += dwsp
        db_ref[...] += db
        dvs, dgsgu = _rms_bwd(vs, rv, gsgu_ref[...], jnp.concatenate(dvn_parts, axis=0), SGU_WIDTH)
        dvs_ref[...] = dvs
        dgsgu_ref[...] += dgsgu

    row = lambda w: pl.BlockSpec((tm, w), lambda i: (i, 0))
    head = pl.BlockSpec((HEADS, 1, tm), lambda i: (0, 0, i))
    acc_shapes = [(1, MLA_WIDTH), (1, SGU_WIDTH), (1, POOL_WIDTH), (1, POOL_WIDTH), (POOL_WIDTH, POOL_WIDTH),
                  (HEADS * CHUNK, CHUNK), (CHUNK, HEAD_PAD), (1, SGU_WIDTH)]
    return pl.pallas_call(
        body, grid=(T // tm,), name="mixers_bwd",
        in_specs=[row(D_MODEL), row(MLA_WIDTH)] + _z_specs(tm) + [WHOLE] * 10,
        out_specs=[row(MLA_WIDTH), head, row(256), row(256), row(256), row(D_MODEL)] + [_acc_spec(s) for s in acc_shapes],
        out_shape=[jax.ShapeDtypeStruct((T, MLA_WIDTH), F32), jax.ShapeDtypeStruct((HEADS, 1, T), F32),
                   jax.ShapeDtypeStruct((T, 256), F32), jax.ShapeDtypeStruct((T, 256), F32),
                   jax.ShapeDtypeStruct((T, 256), F32), jax.ShapeDtypeStruct((T, D_MODEL), BF16)]
                  + [jax.ShapeDtypeStruct(s, F32) for s in acc_shapes],
        compiler_params=_cparams(),
    )(dx1, a, z, z, z, z, gsgu, wcat, wcat_t, bz, wp, pscale, goa, gos, gop, wout)


def pool_bwd(dm):
    T = dm.shape[0]
    tm = min(4 * ROW_TILE, T)
    nt = T // tm

    def body(dm_ref, next_ref, dpin_ref):
        i = pl.program_id(0)
        cnt, lane = _pool_counts(i, tm)
        dmv = dm_ref[...]
        win = _by_group(lane[:POOL_HALO], 2.0, 4.0, 8.0, 16.0)
        nxt = jnp.where(i < nt - 1, next_ref[...] / win, 0.0)
        r1 = jnp.concatenate([dmv / cnt, nxt], axis=0)
        n = tm + POOL_HALO
        r2 = r1 + pltpu.roll(r1, n - 1, 0)
        r4 = r2 + pltpu.roll(r2, n - 2, 0)
        r8 = r4 + pltpu.roll(r4, n - 4, 0)
        r16 = r8 + pltpu.roll(r8, n - 8, 0)
        dpin_ref[...] = _by_group(lane, r2[:tm], r4[:tm], r8[:tm], r16[:tm]) - dmv

    return pl.pallas_call(
        body, grid=(nt,), name="pool_bwd",
        in_specs=[pl.BlockSpec((tm, 256), lambda i: (i, 0)),
                  pl.BlockSpec((POOL_HALO, 256), lambda i: (jnp.minimum((i + 1) * (tm // POOL_HALO), T // POOL_HALO - 1), 0))],
        out_specs=pl.BlockSpec((tm, 256), lambda i: (i, 0)),
        out_shape=jax.ShapeDtypeStruct((T, 256), F32),
        compiler_params=_cparams(),
    )(dm, dm)


def attn_bwd(q, k, kt, v, do, lse, delta):
    _, T, _ = q.shape
    tb = min(ATTN_TILE, T)
    sb = min(ATTN_SUB, tb)
    ns = tb // sb
    nb = T // tb
    hb = ATTN_BWD_HEADS

    pairs = [(i, j) for i in range(nb) for j in range(i, nb)]

    def body(i_tab, j_tab, q_ref, k_ref, kt_ref, v_ref, do_ref, lse_ref, delta_ref, dqt_ref, dk_ref, dv_ref, dk_s, dv_s):
        i, j = i_tab[pl.program_id(1)], j_tab[pl.program_id(1)]

        @pl.when(pl.program_id(1) == 0)
        def _():
            dqt_ref[...] = jnp.zeros(dqt_ref.shape, F32)

        @pl.when(j == i)
        def _():
            dk_s[...] = jnp.zeros(dk_s.shape, F32)
            dv_s[...] = jnp.zeros(dv_s.shape, F32)

        def sub_block(h, a, b, masked):
            qa = slice(sb * a, sb * (a + 1))
            kb = slice(sb * b, sb * (b + 1))
            qv = q_ref[h, qa, :]
            dov = do_ref[qa, HEAD_PAD * h:HEAD_PAD * (h + 1)].astype(BF16)
            st = _dot_nt(k_ref[h, kb, :], qv)
            dpt = _dot_nt(v_ref[h, kb, :], dov)
            pt = jnp.exp2(st - lse_ref[h, :, qa])
            if masked:
                krow = lax.broadcasted_iota(jnp.int32, st.shape, 0)
                qcol = lax.broadcasted_iota(jnp.int32, st.shape, 1)
                pt = jnp.where(krow <= qcol, pt, 0.0)
            dst = (pt * (dpt - delta_ref[h, :, qa])).astype(BF16)
            dv_s[h, kb, :] += _dot(pt.astype(BF16), dov)
            dk_s[h, kb, :] += _dot(dst, qv)
            cols = pl.ds(pl.multiple_of(j * tb + sb * a, sb), sb)
            dqt_ref[h, :, cols] += _dot(kt_ref[h, :, kb], dst)

        @pl.when(j > i)
        def _():
            for a in range(ns):
                for h in range(hb):
                    for b in range(ns):
                        sub_block(h, a, b, False)

        @pl.when(j == i)
        def _():
            for a in range(ns):
                for h in range(hb):
                    for b in range(a + 1):
                        sub_block(h, a, b, a == b)

        @pl.when(j == nb - 1)
        def _():
            dk_ref[...] = dk_s[...] * (1.0 / LOG2E)
            dv_ref[...] = dv_s[...]

    qspec = pl.BlockSpec((hb, tb, HEAD_PAD), lambda g, p, it, jt: (g, jt[p], 0))
    kspec = pl.BlockSpec((hb, tb, HEAD_PAD), lambda g, p, it, jt: (g, it[p], 0))
    ktspec = pl.BlockSpec((hb, HEAD_PAD, tb), lambda g, p, it, jt: (g, 0, it[p]))
    rowspec = pl.BlockSpec((hb, 1, tb), lambda g, p, it, jt: (g, 0, jt[p]))
    grid_spec = pltpu.PrefetchScalarGridSpec(
        num_scalar_prefetch=2, grid=(HEADS // hb, len(pairs)),
        in_specs=[qspec, kspec, ktspec, kspec, pl.BlockSpec((tb, hb * HEAD_PAD), lambda g, p, it, jt: (jt[p], g)), rowspec, rowspec],
        out_specs=[pl.BlockSpec((hb, HEAD_PAD, T), lambda g, p, it, jt: (g, 0, 0)), kspec, kspec],
        scratch_shapes=[pltpu.VMEM((hb, tb, HEAD_PAD), F32), pltpu.VMEM((hb, tb, HEAD_PAD), F32)])
    return pl.pallas_call(
        body, grid_spec=grid_spec, name="attn_bwd",
        out_shape=[jax.ShapeDtypeStruct((HEADS, HEAD_PAD, T), F32)] + [jax.ShapeDtypeStruct((HEADS, T, HEAD_PAD), F32)] * 2,
        compiler_params=_cparams(),
    )(*_pair_tables(pairs), q, k, kt, v, do, lse, delta)


def mixin_bwd(dres, x, z, rope, dqt, dk, dv, du, dvs, dpin, gmix, win, gql, wq, gkvl, wkv, gqh, gkh):
    T = x.shape[0]
    tm = min(ROW_TILE, T)
    part = min(MIXIN_BWD_PART, tm)

    def body(dres_ref, x_ref, z_ref, c_ref, sa_ref, sb_ref, dqt_ref, dk_ref, dv_ref, du_ref, dvs_ref, dpin_ref,
             gmix_ref, win_ref, gql_ref, wq_ref, gkvl_ref, wkv_ref, gqh_ref, gkh_ref,
             dx_ref, hn_ref, dz_ref, qn_ref, dqup_ref, kvn_ref, dkvup_ref,
             dgmix_ref, dgql_ref, dgkvl_ref, dgqh_ref, dgkh_ref):
        @pl.when(pl.program_id(0) == 0)
        def _():
            for r in (dgmix_ref, dgql_ref, dgkvl_ref, dgqh_ref, dgkh_ref):
                r[...] = jnp.zeros(r.shape, F32)

        lane = lax.broadcasted_iota(jnp.int32, (part, HEAD_PAD), 1)
        rope_lanes = (lane >= QK_NOPE) & (lane < QK_HEAD)

        def recompute(g):
            rows = g["rows"]
            g["xv"] = x_ref[rows, :]
            hn, g["rx"] = _rms(g["xv"], gmix_ref[...], D_MODEL)
            hn_ref[rows, :] = hn.astype(BF16)
            g["ql"], g["kvl"], g["kr"], qn, g["rq"], kvn, g["rkv"], g["q_up"], g["kv_up"] = _qkv_pre(
                z_ref[rows, :], gql_ref[...], wq_ref[...], gkvl_ref[...], wkv_ref[...])
            qn_ref[rows, :] = qn
            kvn_ref[rows, :] = kvn

        def heads(g):
            rows = g["rows"]
            c, sa, sb = c_ref[rows, :], sa_ref[rows, :], sb_ref[rows, :]
            dkr = jnp.zeros((part, HEAD_PAD), F32)
            dgqh = jnp.zeros((1, HEAD_PAD), F32)
            dgkh = jnp.zeros((1, HEAD_PAD), F32)
            dq_parts, dk_parts, dv_parts = [], [], []
            for h in range(HEADS):
                lo = HEAD_PAD * h
                qh = g["q_up"][:, lo:lo + HEAD_PAD]
                rqh = lax.rsqrt(jnp.sum(qh * qh, axis=-1, keepdims=True) * (1.0 / QK_HEAD) + EPS)
                dq_h = jnp.transpose(dqt_ref[h, :, rows]) * ATTN_SCALE
                dqh, dg = _rms_bwd(qh, rqh, gqh_ref[...], _rope_bwd(dq_h, c, sa, sb), QK_HEAD)
                dgqh = dgqh + dg
                dq_parts.append(dqh)
                kh = g["kv_up"][:, lo:lo + HEAD_PAD] + g["kr"]
                rkh = lax.rsqrt(jnp.sum(kh * kh, axis=-1, keepdims=True) * (1.0 / QK_HEAD) + EPS)
                dkh, dg = _rms_bwd(kh, rkh, gkh_ref[...], _rope_bwd(dk_ref[h, rows, :], c, sa, sb), QK_HEAD)
                dgkh = dgkh + dg
                dkr = dkr + jnp.where(rope_lanes, dkh, 0.0)
                dk_parts.append(dkh)
                dv_parts.append(dv_ref[h, rows, :])
            dgqh_ref[...] += dgqh
            dgkh_ref[...] += dgkh
            g["dkr"] = dkr
            g["dq_up"] = jnp.concatenate(dq_parts, axis=1).astype(BF16)
            g["dkv_up"] = jnp.concatenate(dk_parts + dv_parts, axis=1).astype(BF16)
            dqup_ref[rows, :] = g["dq_up"]
            dkvup_ref[rows, :] = g["dkv_up"]

        def latents(g):
            rows = g["rows"]
            dql, dg = _rms_bwd(g["ql"], g["rq"], gql_ref[...], _dot(g["dq_up"], wq_ref[...]), Q_LORA)
            dgql_ref[...] += dg
            dkvl, dg = _rms_bwd(g["kvl"], g["rkv"], gkvl_ref[...], _dot(g["dkv_up"], wkv_ref[...]), KV_LORA)
            dgkvl_ref[...] += dg
            g["dz"] = jnp.concatenate([dql, dkvl, g["dkr"], du_ref[rows, :], dvs_ref[rows, :], dpin_ref[rows, :]],
                                      axis=1).astype(BF16)
            dz_ref[rows, :] = g["dz"]

        def inputs(g):
            rows = g["rows"]
            dxn, dg = _rms_bwd(g["xv"], g["rx"], gmix_ref[...], _dot(g["dz"], win_ref[...]), D_MODEL)
            dgmix_ref[...] += dg
            dx_ref[rows, :] = dres_ref[rows, :] + dxn

        _skewed((recompute, heads, latents, inputs), [dict(rows=pl.ds(part * p, part)) for p in range(tm // part)])

    row = lambda w: pl.BlockSpec((tm, w), lambda i: (i, 0))
    head = pl.BlockSpec((HEADS, tm, HEAD_PAD), lambda i: (0, i, 0))
    head_t = pl.BlockSpec((HEADS, HEAD_PAD, tm), lambda i: (0, 0, i))
    acc_shapes = [(1, D_MODEL), (1, Q_LORA), (1, KV_LORA), (1, HEAD_PAD), (1, HEAD_PAD)]
    out_rows = [(D_MODEL, F32), (D_MODEL, BF16), (Z_WIDTH, BF16), (Q_LORA, BF16), (HEADS * HEAD_PAD, BF16),
                (KV_LORA, BF16), (2 * HEADS * HEAD_PAD, BF16)]
    return pl.pallas_call(
        body, grid=(T // tm,), name="mixin_bwd",
        in_specs=[row(D_MODEL), row(D_MODEL), row(Z_WIDTH)] + [row(HEAD_PAD)] * 3 + [head_t, head, head, row(256), row(256), row(256)]
                 + [WHOLE] * 8,
        out_specs=[row(w) for w, _ in out_rows] + [_acc_spec(s) for s in acc_shapes],
        out_shape=[jax.ShapeDtypeStruct((T, w), dt) for w, dt in out_rows] + [jax.ShapeDtypeStruct(s, F32) for s in acc_shapes],
        compiler_params=_cparams(),
    )(dres, x, z, *rope, dqt, dk, dv, du, dvs, dpin, gmix, win, gql, wq, gkvl, wkv, gqh, gkh)


def _place():
    x, y, c = lax.axis_index("x"), lax.axis_index("y"), lax.axis_index("c")
    return x, y, c, 4 * x + 2 * y + c


def _layer_of(ref, l):
    return ref[:, l, :] if ref.shape[1] == DEPTH else ref[l]


def _layer_shape(shard):
    return (shard.shape[0], shard.shape[2]) if shard.shape[1] == DEPTH else shard.shape[1:]


def cast_shards(shards, wanted):
    n = len(shards)

    def body(*refs):
        for o_ref, (w, l) in zip(refs[n:], wanted):
            o_ref[...] = _layer_of(refs[w], l).astype(BF16)

    return pl.pallas_call(
        body, name="cast_shards", in_specs=[WHOLE] * n, out_specs=[WHOLE] * len(wanted),
        out_shape=[jax.ShapeDtypeStruct(_layer_shape(shards[w]), BF16) for w, _ in wanted],
        compiler_params=_cparams(),
    )(*shards)


def _peer(k):
    x, y, c, _ = _place()
    px = 1 - x if k & 4 else x
    py = 1 - y if k & 2 else y
    pc = 1 - c if k & 1 else c
    return (px, py, pc), 4 * px + 2 * py + pc


def exchange_start(srcs, after, gather, name):
    n = len(srcs)
    land_shapes = [((N_DEV,) + s.shape) if gather else s.shape for s in srcs]

    def body(*refs):
        src_refs, land_refs = refs[:n], refs[n:2 * n]
        send_sems, recv_sems = refs[2 * n + 1:3 * n + 1], refs[3 * n + 1:4 * n + 1]
        token = refs[-1]
        _, _, _, me = _place()
        for k in range(1, N_DEV):
            peer, peer_id = _peer(k)
            for w in range(n):
                pltpu.make_async_remote_copy(
                    src_ref=src_refs[w] if gather else src_refs[w].at[peer_id], dst_ref=land_refs[w].at[me],
                    send_sem=send_sems[w], recv_sem=recv_sems[w], device_id=peer, device_id_type=MESH).start()
        for w in range(n):
            pltpu.make_async_copy(src_refs[w] if gather else src_refs[w].at[me], land_refs[w].at[me], recv_sems[w]).start()
        token[...] = jnp.zeros(token.shape, F32)

    hbm = lambda a: pltpu.with_memory_space_constraint(a, pltpu.HBM)
    outs = pl.pallas_call(
        body, name=name,
        out_shape=(pltpu.SemaphoreType.DMA(()),) * (2 * n)
                  + tuple(pltpu.HBM(s.shape, BF16) for s in srcs) + tuple(pltpu.HBM(s, BF16) for s in land_shapes)
                  + (jax.ShapeDtypeStruct((8, 128), F32),),
        in_specs=[HBM_SPEC] * (2 * n) + [ANY],
        out_specs=(SEM_SPEC,) * (2 * n) + (HBM_SPEC,) * (2 * n) + (WHOLE,),
        input_output_aliases={i: 2 * n + i for i in range(2 * n)},
        compiler_params=pltpu.CompilerParams(has_side_effects=pltpu.SideEffectType.DATAFLOW_SIDE_EFFECTING),
    )(*[hbm(s) for s in srcs], *[hbm(lax.empty(s, BF16)) for s in land_shapes], after)
    return list(outs[:n]), list(outs[n:2 * n]), list(outs[2 * n:3 * n]), list(outs[3 * n:4 * n]), outs[-1]


def exchange_wait(started, after, name):
    send_sems, recv_sems, srcs, lands, _ = started
    n = len(srcs)

    def body(*refs):
        land_refs = refs[n:2 * n]
        send_sems, recv_sems = refs[2 * n:3 * n], refs[3 * n:4 * n]
        x, y, c, _ = _place()
        for w in range(n):
            seven, eight = land_refs[w].at[pl.ds(0, N_DEV - 1)], land_refs[w]
            pltpu.make_async_remote_copy(src_ref=seven, dst_ref=seven, send_sem=send_sems[w], recv_sem=recv_sems[w],
                                         device_id=(x, y, c), device_id_type=MESH).wait_send()
            pltpu.make_async_remote_copy(src_ref=eight, dst_ref=eight, send_sem=send_sems[w], recv_sem=recv_sems[w],
                                         device_id=(x, y, c), device_id_type=MESH).wait_recv()

    outs = pl.pallas_call(
        body, name=name,
        out_shape=tuple(pltpu.HBM(s.shape, BF16) for s in srcs) + tuple(pltpu.HBM(l.shape, BF16) for l in lands),
        in_specs=[HBM_SPEC] * (2 * n) + [SEM_SPEC] * (2 * n) + [ANY],
        out_specs=(HBM_SPEC,) * (2 * n),
        input_output_aliases={i: i for i in range(2 * n)},
        compiler_params=pltpu.CompilerParams(has_side_effects=pltpu.SideEffectType.DATAFLOW_SIDE_EFFECTING),
    )(*srcs, *lands, *send_sems, *recv_sems, after)
    return list(outs[n:])


def _adamw(w, g, m, v):
    m2 = ADAM_B1 * m + (1.0 - ADAM_B1) * g
    v2 = ADAM_B2 * v + (1.0 - ADAM_B2) * (g * g)
    m_hat = m2 / (1.0 - ADAM_B1 ** ADAM_STEP)
    v_hat = v2 / (1.0 - ADAM_B2 ** ADAM_STEP)
    delta = -ADAM_LR * (m_hat / (jnp.sqrt(v_hat) + ADAM_EPS) + ADAM_WD * w)
    return delta, m2, v2


def adamw_sharded(parts, w, m, v, name):
    L, R, C = w.shape
    fits = [d for d in range(16, min(R, 512) + 1, 16) if R % d == 0]
    br = max(fits) if fits else R
    nblk = R // br

    def body(*refs):
        p_refs = refs[:L]
        w_ref, m_ref, v_ref, g_ref, d_ref, m2_ref, v2_ref = refs[L:]

        def total(p_ref):
            g = p_ref[0].astype(F32)
            for s in range(1, N_DEV):
                g = g + p_ref[s].astype(F32)
            return g

        g = total(p_refs[0])
        for l in range(1, L):
            g = jnp.where(pl.program_id(0) == l, total(p_refs[l]), g)
        g_ref[...] = g
        d_ref[...], m2_ref[...], v2_ref[...] = _adamw(w_ref[...], g, m_ref[...], v_ref[...])

    blk = pl.BlockSpec((None, br, C), lambda l, i: (l, i, 0))

    def part_spec(k):
        return pl.BlockSpec((N_DEV, br, C), lambda l, i: (0, jnp.where(l == k, i, jnp.where(l < k, 0, nblk - 1)), 0))

    return pl.pallas_call(
        body, grid=(L, nblk), name=name,
        in_specs=[part_spec(k) for k in range(L)] + [blk, blk, blk],
        out_specs=[blk] * 4,
        out_shape=[jax.ShapeDtypeStruct((L, R, C), F32)] * 4,
        compiler_params=_cparams(),
    )(*parts, w, m, v)


def allreduce_adamw_small(gpart, w, m, v):
    R = gpart.shape[0]

    def body(g_ref, w_ref, m_ref, v_ref, grad_ref, d_ref, m2_ref, v2_ref, all_ref, send_sems, recv_sems):
        x, y, c, me = _place()
        sibling = (x, y, 1 - c)
        chips = [(1 - x, y), (x, 1 - y), (1 - x, 1 - y)]

        def copy(k, block_id, to, from_input):
            return pltpu.make_async_remote_copy(
                src_ref=g_ref if from_input else all_ref.at[block_id], dst_ref=all_ref.at[block_id],
                send_sem=send_sems.at[k], recv_sem=recv_sems.at[k], device_id=to, device_id_type=MESH)

        def block_of(cx, cy, cc):
            return 4 * cx + 2 * cy + cc

        all_ref[me] = g_ref[...]
        first = [copy(0, me, sibling, True)] + [copy(1 + j, me, (*chip, c), True) for j, chip in enumerate(chips)]
        for cp in first:
            cp.start()
        passed = [copy(4 + j, block_of(*chip, c), sibling, False) for j, chip in enumerate(chips)]
        for j, chip in enumerate(chips):
            copy(1 + j, block_of(*chip, c), (x, y, c), False).wait_recv()
            passed[j].start()
        copy(0, block_of(x, y, 1 - c), (x, y, c), False).wait_recv()
        for j, chip in enumerate(chips):
            copy(4 + j, block_of(*chip, 1 - c), (x, y, c), False).wait_recv()
        for cp in first + passed:
            cp.wait_send()
        g = all_ref[0]
        for s in range(1, N_DEV):
            g = g + all_ref[s]
        grad_ref[...] = g
        d_ref[...], m2_ref[...], v2_ref[...] = _adamw(w_ref[...], g, m_ref[...], v_ref[...])

    return pl.pallas_call(
        body, name="allreduce_adamw_small",
        in_specs=[WHOLE] * 4, out_specs=[WHOLE] * 4,
        out_shape=[jax.ShapeDtypeStruct((R, 128), F32)] * 4,
        scratch_shapes=[pltpu.VMEM((N_DEV, R, 128), F32), pltpu.SemaphoreType.DMA((7,)), pltpu.SemaphoreType.DMA((7,))],
        compiler_params=_cparams(),
    )(gpart, w, m, v)


def _shard_view(name, a):
    if name == "w_in":
        return a.transpose(2, 0, 1)
    return a.swapaxes(1, 2) if name in TRANSPOSED else a


def _shard_unview(name, a):
    if name == "w_in":
        return a.transpose(1, 2, 0)
    return a.swapaxes(1, 2) if name in TRANSPOSED else a


def _pad_head_rows(w, width):
    c = w.shape[1]
    return jnp.pad(w.reshape(HEADS, width, c), ((0, 0), (0, HEAD_PAD - width), (0, 0))).reshape(HEADS * HEAD_PAD, c)


def _unpad_head_rows(w, width):
    c = w.shape[1]
    return w.reshape(HEADS, HEAD_PAD, c)[:, :width].reshape(HEADS * width, c)


O1 = Q_LORA
O2 = O1 + KV_LORA
O3 = O2 + QK_ROPE


def _mixer_weights(gw):
    w_in = gw["w_in"].reshape(IN_WIDTH, D_MODEL)
    zero = lambda n: jnp.zeros((n, D_MODEL), BF16)
    win = jnp.concatenate([w_in[:O2], zero(QK_NOPE), w_in[O2:O3], zero(HEAD_PAD - QK_HEAD), w_in[O3:]], axis=0)
    wq = _pad_head_rows(gw["w_q_up"].reshape(HEADS * QK_HEAD, Q_LORA), QK_HEAD)
    w_kv = gw["w_kv_up"].reshape(HEADS, QK_NOPE + V_HEAD, KV_LORA)
    wk = jnp.pad(w_kv[:, :QK_NOPE], ((0, 0), (0, HEAD_PAD - QK_NOPE), (0, 0))).reshape(HEADS * HEAD_PAD, KV_LORA)
    wv = w_kv[:, QK_NOPE:].reshape(HEADS * V_HEAD, KV_LORA)
    wkv = jnp.concatenate([wk, wv], axis=0)
    out = dict(win=win, wq=wq, wkv=wkv)
    if "w_out" in gw:
        out["wout"] = gw["w_out"].reshape(D_MODEL, D_MODEL)
    return out


def _ffn_weights(gw):
    return dict(wg=gw["w_gate"].reshape(FFN_HIDDEN, D_MODEL), wu=gw["w_up"].reshape(FFN_HIDDEN, D_MODEL),
                wd=gw["w_down"].reshape(FFN_HIDDEN, D_MODEL))


def _layer_small(p, l):
    row = lambda a: a.reshape(1, -1)
    pad_head = lambda g: jnp.pad(g, (0, HEAD_PAD - QK_HEAD)).reshape(1, HEAD_PAD)
    tril = jnp.tril(jnp.ones((CHUNK, CHUNK), F32))
    wsp = p["w_spatial"][l] * tril
    wcat = jnp.concatenate([wsp[h] for h in range(HEADS)], axis=1).astype(BF16)
    wcat_t = jnp.concatenate([wsp[h].T for h in range(HEADS)], axis=1).astype(BF16)
    bz = jnp.repeat(p["b_spatial"][l].T, SGU_HEAD_DIM, axis=1)
    wp = jax.scipy.linalg.block_diag(*[p["w_pool"][l][g] for g in range(HEADS)]).astype(BF16)
    return dict(gmix=row(p["g_mix_norm"][l]), gql=row(p["g_q_lat"][l]), gkvl=row(p["g_kv_lat"][l]),
                gqh=pad_head(p["g_q_head"][l]), gkh=pad_head(p["g_k_head"][l]), gsgu=row(p["g_sgu_v"][l]),
                wcat=wcat, wcat_t=wcat_t, bz=bz, wp=wp, pscale=row(p["pool_scale"][l]),
                goa=row(p["g_out_mla"][l]), gos=row(p["g_out_sgu"][l]), gop=row(p["g_out_pool"][l]),
                gffn=row(p["g_ffn_norm"][l]))


MIXER_SIDE = ("w_in", "w_q_up", "w_kv_up", "w_out")
FFN_SIDE = ("w_gate", "w_up", "w_down")
TRANSPOSED = ("w_in", "w_q_up", "w_kv_up", "w_gate", "w_up")
FIRST_0 = ("w_in", "w_q_up", "w_kv_up")
LATER_0 = ("w_out",) + FFN_SIDE
SHARDED = MIXER_SIDE + FFN_SIDE
SMALL = ("g_mix_norm", "g_q_lat", "g_kv_lat", "g_q_head", "g_k_head", "g_sgu_v", "w_spatial", "b_spatial", "w_pool",
         "pool_scale", "g_out_mla", "g_out_sgu", "g_out_pool", "g_ffn_norm")
WEIGHTS = ("g_mix_norm", "w_in", "g_q_lat", "w_q_up", "g_kv_lat", "w_kv_up", "g_q_head", "g_k_head", "g_sgu_v", "w_spatial",
           "b_spatial", "w_pool", "pool_scale", "g_out_mla", "g_out_sgu", "g_out_pool", "w_out", "g_ffn_norm", "w_gate",
           "w_up", "w_down")
PACKED = SMALL + ("loss",)
PACK_ROWS = 8 * 128


def _pack_small(parts):
    flat = []
    for name in PACKED:
        a = parts[name].reshape(-1)
        flat.append(jnp.pad(a, (0, -a.shape[0] % PACK_ROWS)))
    return jnp.concatenate(flat).reshape(-1, 128)


def _unpack_small(packed, like):
    out, row = {}, 0
    for name in PACKED:
        n = math.prod(like[name].shape)
        rows = -(-n // PACK_ROWS) * 8
        out[name] = packed[row:row + rows].reshape(-1)[:n].reshape(like[name].shape)
        row += rows
    return out


def _forward_attention(x, rope, W, S):
    z, q, k, v, kt, vt = mixin_fwd(x, rope, S["gmix"], W["win"], S["gql"], W["wq"], S["gkvl"], W["wkv"], S["gqh"], S["gkh"])
    a, lse = attn_fwd(q, k, vt)
    return dict(x=x, z=z, q=q, k=k, kt=kt, v=v, a=a, lse=lse)


def _forward_mixers(A, wout, S):
    A["x1"] = mixers_fwd(A["x"], A["a"], A["z"], S["gsgu"], S["wcat"], S["bz"], S["wp"], S["pscale"], S["goa"], S["gos"],
                         S["gop"], wout)
    return A["x1"]


def _backward_ffn(dx2, W, S, A, l):
    dx1, h2, act, dg, du_ffn, dgffn = ffn_bwd(dx2, A["x1"], A["gs"], A["us"], S["gffn"], W["wg"], W["wu"], W["wd"])
    d_wd = matmul_tn(act, dx2, f"dw_down_{l}")
    d_wg = matmul_tn(dg, h2, f"dw_gate_{l}")
    d_wu = matmul_tn(du_ffn, h2, f"dw_up_{l}")
    big = {n: d.reshape(N_DEV, -1, D_MODEL) for n, d in (("w_gate", d_wg), ("w_up", d_wu), ("w_down", d_wd))}
    return dx1, big, dgffn


def _backward_mixers(dx1, dgffn, rope, W, S, A, l, send_wout=None):
    (da, delta, du, dvs, dm, mix, dgoa, dgos, dgop, dps, dwp, dwsp, db, dgsgu) = mixers_bwd(
        dx1, A["a"], A["z"], S["gsgu"], S["wcat"], S["wcat_t"], S["bz"], S["wp"], S["pscale"], S["goa"], S["gos"],
        S["gop"], W["wout"])
    d_wout = matmul_tn(mix, dx1, f"dw_out_{l}")
    if send_wout is not None:
        delta = delta + send_wout(d_wout.reshape(N_DEV, -1, D_MODEL))[0, 0]
    dpin = pool_bwd(dm)
    dqt, dk, dv = attn_bwd(A["q"], A["k"], A["kt"], A["v"], da, A["lse"], delta)
    (dx, hn, dz, qn, dq_up, kvn, dkv_up, dgmix, dgql, dgkvl, dgqh, dgkh) = mixin_bwd(
        dx1, A["x"], A["z"], rope, dqt, dk, dv, du, dvs, dpin, S["gmix"], W["win"], S["gql"], W["wq"], S["gkvl"],
        W["wkv"], S["gqh"], S["gkh"])
    d_win = matmul_tn(dz, hn, f"dw_in_{l}")
    d_wq = matmul_tn(dq_up, qn, f"dw_q_up_{l}")
    d_wkv = matmul_tn(dkv_up, kvn, f"dw_kv_up_{l}")
    d_win = jnp.concatenate([d_win[:O2], d_win[O2 + QK_NOPE:O2 + QK_HEAD], d_win[O2 + HEAD_PAD:]], axis=0)
    d_wk = d_wkv[:HEADS * HEAD_PAD].reshape(HEADS, HEAD_PAD, KV_LORA)[:, :QK_NOPE]
    d_wv = d_wkv[HEADS * HEAD_PAD:].reshape(HEADS, V_HEAD, KV_LORA)
    d_wkv = jnp.concatenate([d_wk, d_wv], axis=1)
    big = dict(w_in=d_win.reshape(N_DEV, -1, D_MODEL), w_q_up=_unpad_head_rows(d_wq, QK_HEAD).reshape(N_DEV, -1, Q_LORA),
               w_kv_up=d_wkv.reshape(N_DEV, -1, KV_LORA), w_out=d_wout.reshape(N_DEV, -1, D_MODEL))
    tril = jnp.tril(jnp.ones((CHUNK, CHUNK), F32))
    small = dict(g_mix_norm=dgmix[0], g_q_lat=dgql[0], g_kv_lat=dgkvl[0], g_q_head=dgqh[0, :QK_HEAD], g_k_head=dgkh[0, :QK_HEAD],
                 g_sgu_v=dgsgu[0], w_spatial=dwsp.reshape(HEADS, CHUNK, CHUNK) * tril, b_spatial=db[:, :HEADS].T,
                 w_pool=jnp.stack([dwp[64 * g:64 * (g + 1), 64 * g:64 * (g + 1)] for g in range(HEADS)]),
                 pool_scale=dps[0], g_out_mla=dgoa[0], g_out_sgu=dgos[0], g_out_pool=dgop[0], g_ffn_norm=dgffn[0])
    return dx, big, small


def kernel(x, positions, g_mix_norm, w_in, g_q_lat, w_q_up, g_kv_lat, w_kv_up, g_q_head, g_k_head, g_sgu_v, w_spatial, b_spatial, w_pool, pool_scale, g_out_mla, g_out_sgu, g_out_pool, w_out, g_ffn_norm, w_gate, w_up, w_down, loss_target, m_g_mix_norm, m_w_in, m_g_q_lat, m_w_q_up, m_g_kv_lat, m_w_kv_up, m_g_q_head, m_g_k_head, m_g_sgu_v, m_w_spatial, m_b_spatial, m_w_pool, m_pool_scale, m_g_out_mla, m_g_out_sgu, m_g_out_pool, m_w_out, m_g_ffn_norm, m_w_gate, m_w_up, m_w_down, v_g_mix_norm, v_w_in, v_g_q_lat, v_w_q_up, v_g_kv_lat, v_w_kv_up, v_g_q_head, v_g_k_head, v_g_sgu_v, v_w_spatial, v_b_spatial, v_w_pool, v_pool_scale, v_g_out_mla, v_g_out_sgu, v_g_out_pool, v_w_out, v_g_ffn_norm, v_w_gate, v_w_up, v_w_down):
    given = dict(locals())
    w = {n: given[n] for n in WEIGHTS}
    m = {n: given["m_" + n] for n in WEIGHTS}
    v = {n: given["v_" + n] for n in WEIGHTS}
    T = x.shape[1]
    xs = x.reshape(T, D_MODEL)

    half = QK_ROPE // 2
    inv_freq = 1.0 / (ROPE_THETA ** (jnp.arange(half, dtype=F32) / half))
    ang16 = positions.reshape(T).astype(F32)[:, None] * inv_freq
    ang = jnp.concatenate([jnp.zeros((T, QK_NOPE), F32), ang16, ang16, jnp.zeros((T, HEAD_PAD - QK_HEAD), F32)], axis=1)

    wv = {n: _shard_view(n, w[n]) for n in SHARDED}
    wanted = [(SHARDED.index(n), 0) for n in FIRST_0 + LATER_0] + [(i, 1) for i in range(len(SHARDED))]
    bf = cast_shards([wv[n] for n in SHARDED], wanted)
    nf, nl = len(FIRST_0), len(LATER_0)
    ag_first0 = exchange_start(bf[:nf], ang, True, "ag_start_first0")
    rope = rope_tables(ang + ag_first0[-1][0, 0])
    W0 = dict(zip(FIRST_0, exchange_wait(ag_first0, rope[0], "ag_wait_first0")))
    ag_later0 = exchange_start(bf[nf:nf + nl], W0["w_in"], True, "ag_start_later0")
    ag_l1 = exchange_start(bf[nf + nl:], ag_later0[-1], True, "ag_start_l1")
    Ss = [_layer_small(w, l) for l in range(DEPTH)]

    S0 = dict(Ss[0], gmix=Ss[0]["gmix"] + (ag_later0[-1][0, 0] + ag_l1[-1][0, 0]))
    A0 = _forward_attention(xs, rope, _mixer_weights(W0), S0)
    W0.update(zip(LATER_0, exchange_wait(ag_later0, A0["a"], "ag_wait_later0")))
    W0 = {**_mixer_weights(W0), **_ffn_weights(W0)}
    x1 = _forward_mixers(A0, W0["wout"], Ss[0])
    h, A0["gs"], A0["us"] = ffn_fwd(x1, Ss[0]["gffn"], W0["wg"], W0["wu"], W0["wd"])
    layer1 = dict(zip(SHARDED, exchange_wait(ag_l1, h, "ag_wait_l1")))
    W1 = {**_mixer_weights(layer1), **_ffn_weights(layer1)}
    A1 = _forward_attention(h, rope, W1, Ss[1])
    x1 = _forward_mixers(A1, W1["wout"], Ss[1])
    dh, loss_part, A1["gs"], A1["us"] = ffn_fwd(x1, Ss[1]["gffn"], W1["wg"], W1["wu"], W1["wd"],
                                                target=loss_target.reshape(T, D_MODEL))

    dx1, big_f1, dgffn1 = _backward_ffn(dh, W1, Ss[1], A1, 1)
    dh, big_m1, small1 = _backward_mixers(dx1, dgffn1, rope, W1, Ss[1], A1, 1)
    rs_l1 = exchange_start([{**big_m1, **big_f1}[n] for n in SHARDED], dh, False, "rs_start_l1")
    S0 = dict(Ss[0], gffn=Ss[0]["gffn"] + rs_l1[-1][0, 0])
    dx1, big_f0, dgffn0 = _backward_ffn(dh, W0, S0, A0, 0)
    rs_ffn0 = exchange_start([big_f0[n] for n in FFN_SIDE], dx1, False, "rs_start_ffn0")
    S0 = dict(Ss[0], gsgu=Ss[0]["gsgu"] + rs_ffn0[-1][0, 0])
    rs_out0 = []

    def send_wout(pieces):
        rs_out0.append(exchange_start([pieces], dx1, False, "rs_start_out0"))
        return rs_out0[0][-1]

    dh, big_m0, small0 = _backward_mixers(dx1, dgffn0, rope, W0, S0, A0, 0, send_wout)
    grad_x = dh.reshape(x.shape)
    smalls = [small0, small1]

    rs_first0 = exchange_start([big_m0[n] for n in FIRST_0], dh, False, "rs_start_first0")
    parts1 = dict(zip(SHARDED, exchange_wait(rs_l1, rs_first0[-1], "rs_wait_l1")))
    parts0 = dict(zip(FFN_SIDE, exchange_wait(rs_ffn0, rs_first0[-1], "rs_wait_ffn0")))
    parts0["w_out"] = exchange_wait(rs_out0[0], rs_first0[-1], "rs_wait_out0")[0]
    grad, delta, new_m, new_v = {}, {}, {}, {}

    small_part = {n: jnp.stack([smalls[l][n] for l in range(DEPTH)]) for n in SMALL}
    small_part["loss"] = loss_part
    no_loss = {"loss": jnp.zeros((1, 1), F32)}
    outs = allreduce_adamw_small(_pack_small(small_part), *[_pack_small({**a, **no_loss}) for a in (w, m, v)])
    for d, o in zip((grad, delta, new_m, new_v), outs):
        d.update(_unpack_small(o, {**w, **no_loss}))
    loss = grad["loss"][0, 0]

    for n in LATER_0 + FIRST_0:
        if n == FIRST_0[0]:
            parts0.update(zip(FIRST_0, exchange_wait(rs_first0, grad["w_down"], "rs_wait_first0")))
        parts, views = [parts0[n], parts1[n]], [_shard_view(n, a[n]) for a in (w, m, v)]
        if n == "w_in":
            parts = [jnp.stack(parts, axis=2).reshape(N_DEV, -1, D_MODEL)]
            views = [a.reshape(1, -1, D_MODEL) for a in views]
        outs = adamw_sharded(parts, *views, f"adamw_{n}")
        if n == "w_in":
            outs = [o.reshape(-1, DEPTH, D_MODEL) for o in outs]
        grad[n], delta[n], new_m[n], new_v[n] = [_shard_unview(n, o) for o in outs]

    return (loss, grad_x, *[grad[n] for n in WEIGHTS], *[delta[n] for n in WEIGHTS], *[new_m[n] for n in WEIGHTS],
            *[new_v[n] for n in WEIGHTS])
```

```python
import math

import jax
import jax.numpy as jnp
from jax import lax
from jax.experimental import pallas as pl
from jax.experimental.pallas import tpu as pltpu

F32 = jnp.float32
BF16 = jnp.bfloat16

N_DEV = 8
DEPTH = 2
D_MODEL = 1024
HEADS = 4
HEAD_PAD = 128
QK_NOPE = 64
QK_ROPE = 32
QK_HEAD = QK_NOPE + QK_ROPE
V_HEAD = 128
Q_LORA = 256
KV_LORA = 128
SGU_WIDTH = 256
SGU_HEAD_DIM = 64
CHUNK = 128
POOL_WIDTH = 256
POOL_HALO = 16
MLA_WIDTH = 512
IN_WIDTH = 1184
Z_WIDTH = 1280
FFN_HIDDEN = 2816
FFN_CHUNK = 256
ROPE_THETA = 10000.0
EPS = 1e-6
ATTN_SCALE = 1.0 / math.sqrt(QK_HEAD)
LOG2E = 1.4426950408889634
NEG_BIG = -1e30

ADAM_LR = 0.001
ADAM_B1 = 0.9
ADAM_B2 = 0.999
ADAM_EPS = 1e-08
ADAM_WD = 0.01
ADAM_STEP = 10

VMEM_LIMIT = 56 * 1024 * 1024
ROW_TILE = 512
MIXIN_PART = 256
MIXIN_BWD_PART = 128
ATTN_TILE = 1024
ATTN_SUB = 512
ATTN_FWD_SUB = 512
ATTN_BWD_HEADS = 2
MESH = pl.DeviceIdType.MESH

WHOLE = pl.BlockSpec(memory_space=pltpu.VMEM)
ANY = pl.BlockSpec(memory_space=pl.ANY)
HBM_SPEC = pl.BlockSpec(memory_space=pltpu.HBM)
SEM_SPEC = pl.BlockSpec(memory_space=pltpu.SEMAPHORE)


def _cparams(**kw):
    return pltpu.CompilerParams(vmem_limit_bytes=VMEM_LIMIT, **kw)


def _dot(a, b):
    return jnp.dot(a, b, preferred_element_type=F32)


def _dot_nt(a, b):
    return lax.dot_general(a, b, (((1,), (1,)), ((), ())), preferred_element_type=F32)


def _dot_tn(a, b):
    return lax.dot_general(a, b, (((0,), (0,)), ((), ())), preferred_element_type=F32)


def _rms(x, g, n):
    r = lax.rsqrt(jnp.sum(x * x, axis=-1, keepdims=True) * (1.0 / n) + EPS)
    return x * r * g, r


def _rms_bwd(x, r, g, dy, n):
    gdy = dy * g
    dx = r * gdy - x * (r * r * r) * (jnp.sum(x * gdy, axis=-1, keepdims=True) * (1.0 / n))
    dg = jnp.sum(dy * (x * r), axis=0, keepdims=True)
    return dx, dg


def _sigmoid(x):
    return 1.0 / (1.0 + jnp.exp(-x))


def rope_tables(ang):
    T = ang.shape[0]
    tm = min(ROW_TILE, T)

    def body(ang_ref, c_ref, sa_ref, sb_ref):
        a = ang_ref[...]
        lane = lax.broadcasted_iota(jnp.int32, a.shape, 1)
        s = jnp.sin(a)
        c_ref[...] = jnp.cos(a)
        sa_ref[...] = jnp.where(lane < QK_NOPE + QK_ROPE // 2, -s, 0.0)
        sb_ref[...] = jnp.where(lane >= QK_NOPE + QK_ROPE // 2, s, 0.0)

    row = pl.BlockSpec((tm, HEAD_PAD), lambda i: (i, 0))
    return pl.pallas_call(
        body, grid=(T // tm,), name="rope_tables", in_specs=[row], out_specs=[row] * 3,
        out_shape=[jax.ShapeDtypeStruct((T, HEAD_PAD), F32)] * 3, compiler_params=_cparams(),
    )(ang)


def _rope(x, c, sa, sb):
    half = QK_ROPE // 2
    return x * c + pltpu.roll(x, HEAD_PAD - half, 1) * sa + pltpu.roll(x, half, 1) * sb


def _rope_bwd(dy, c, sa, sb):
    half = QK_ROPE // 2
    return dy * c + pltpu.roll(dy * sa, half, 1) + pltpu.roll(dy * sb, HEAD_PAD - half, 1)


def _skewed(stages, groups):
    for t in range(len(stages) + len(groups) - 1):
        for p, g in enumerate(groups):
            if 0 <= t - p < len(stages):
                stages[t - p](g)


def _head_masks(shape, width):
    lane = lax.broadcasted_iota(jnp.int32, shape, len(shape) - 1)
    return [(lane >= width * h) & (lane < width * (h + 1)) for h in range(HEADS)]


def _qkv_pre(z, gql, wq, gkvl, wkv):
    ql = z[:, 0:Q_LORA]
    kvl = z[:, Q_LORA:Q_LORA + KV_LORA]
    kr = z[:, Q_LORA + KV_LORA:Q_LORA + KV_LORA + HEAD_PAD]
    qn, rq = _rms(ql, gql, Q_LORA)
    kvn, rkv = _rms(kvl, gkvl, KV_LORA)
    qn = qn.astype(BF16)
    kvn = kvn.astype(BF16)
    q_up = _dot_nt(qn, wq)
    kv_up = _dot_nt(kvn, wkv)
    return ql, kvl, kr, qn, rq, kvn, rkv, q_up, kv_up


def mixin_fwd(x, rope, gmix, win, gql, wq, gkvl, wkv, gqh, gkh):
    T = x.shape[0]
    tm = min(2 * ROW_TILE, T)
    part = min(MIXIN_PART, tm)

    def body(x_ref, c_ref, sa_ref, sb_ref, gmix_ref, win_ref, gql_ref, wq_ref, gkvl_ref, wkv_ref, gqh_ref, gkh_ref,
             z_ref, q_ref, k_ref, v_ref, kt_ref, vt_ref):
        def project(g):
            hn = _rms(x_ref[g["rows"], :], gmix_ref[...], D_MODEL)[0].astype(BF16)
            g["z"] = _dot_nt(hn, win_ref[...])
            z_ref[g["rows"], :] = g["z"]

        def latents(g):
            _, _, g["kr"], _, _, _, _, g["q_up"], g["kv_up"] = _qkv_pre(g["z"], gql_ref[...], wq_ref[...], gkvl_ref[...],
                                                                        wkv_ref[...])

        def heads(g):
            rows = g["rows"]
            c, sa, sb = c_ref[rows, :], sa_ref[rows, :], sb_ref[rows, :]
            for h in range(HEADS):
                lo = HEAD_PAD * h
                qh = _rms(g["q_up"][:, lo:lo + HEAD_PAD], gqh_ref[...], QK_HEAD)[0]
                q_ref[h, rows, :] = (_rope(qh, c, sa, sb) * (ATTN_SCALE * LOG2E)).astype(BF16)
                kh = _rope(_rms(g["kv_up"][:, lo:lo + HEAD_PAD] + g["kr"], gkh_ref[...], QK_HEAD)[0], c, sa, sb)
                k_ref[h, rows, :] = kh.astype(BF16)
                kt_ref[h, :, rows] = jnp.transpose(kh).astype(BF16)
                vh = g["kv_up"][:, HEADS * HEAD_PAD + lo:HEADS * HEAD_PAD + lo + HEAD_PAD]
                v_ref[h, rows, :] = vh.astype(BF16)
                vt_ref[h, :, rows] = jnp.transpose(vh).astype(BF16)

        _skewed((project, latents, heads), [dict(rows=pl.ds(part * p, part)) for p in range(tm // part)])

    row = lambda w: pl.BlockSpec((tm, w), lambda i: (i, 0))
    head = pl.BlockSpec((HEADS, tm, HEAD_PAD), lambda i: (0, i, 0))
    head_t = pl.BlockSpec((HEADS, HEAD_PAD, tm), lambda i: (0, 0, i))
    return pl.pallas_call(
        body, grid=(T // tm,), name="mixin_fwd",
        in_specs=[row(D_MODEL)] + [row(HEAD_PAD)] * 3 + [WHOLE] * 8,
        out_specs=[row(Z_WIDTH), head, head, head, head_t, head_t],
        out_shape=[jax.ShapeDtypeStruct((T, Z_WIDTH), F32)] + [jax.ShapeDtypeStruct((HEADS, T, HEAD_PAD), BF16)] * 3
                  + [jax.ShapeDtypeStruct((HEADS, HEAD_PAD, T), BF16)] * 2,
        compiler_params=_cparams(),
    )(x, *rope, gmix, win, gql, wq, gkvl, wkv, gqh, gkh)


def _pair_tables(pairs):
    return (jnp.asarray([p[0] for p in pairs], jnp.int32), jnp.asarray([p[1] for p in pairs], jnp.int32))


def attn_fwd(q, k, vt):
    _, T, _ = q.shape
    tb = min(ATTN_TILE, T)
    sq = min(ATTN_FWD_SUB, tb)
    nb = T // tb

    pairs = [(i, j) for i in range(nb) for j in range(i + 1)]

    def body(i_tab, j_tab, q_ref, k_ref, vt_ref, o_ref, lse_ref, m_s, l_s, acc_s):
        i, j = i_tab[pl.program_id(0)], j_tab[pl.program_id(0)]

        @pl.when(j == 0)
        def _():
            m_s[...] = jnp.full(m_s.shape, -jnp.inf, F32)
            l_s[...] = jnp.zeros(l_s.shape, F32)
            acc_s[...] = jnp.zeros(acc_s.shape, F32)

        def scores(g):
            st = _dot_nt(k_ref[g["h"], g["kb"], :], q_ref[g["h"], g["qa"], :])
            if g["masked"]:
                krow = lax.broadcasted_iota(jnp.int32, st.shape, 0)
                qcol = g["q0"] + lax.broadcasted_iota(jnp.int32, st.shape, 1)
                st = jnp.where(krow <= qcol, st, NEG_BIG)
            g["st"] = st

        def new_max(g):
            h, qa = g["h"], g["qa"]
            m_prev = m_s[h, :, qa]
            g["m"] = jnp.maximum(m_prev, jnp.max(g["st"], axis=0, keepdims=True))
            g["alpha"] = jnp.exp2(m_prev - g["m"])
            m_s[h, :, qa] = g["m"]

        def weights(g):
            pt = jnp.exp2(g["st"] - g["m"])
            g["lsum"] = jnp.sum(pt, axis=0, keepdims=True)
            g["pt"] = pt.astype(BF16)

        def accumulate(g):
            h, qa = g["h"], g["qa"]
            l_s[h, :, qa] = g["alpha"] * l_s[h, :, qa] + g["lsum"]
            acc_s[h, :, qa] = g["alpha"] * acc_s[h, :, qa] + _dot(vt_ref[h, :, g["kb"]], g["pt"])

        def tiles(masked):
            return [dict(h=h, q0=q0, qa=slice(q0, q0 + sq), kb=slice(0, q0 + sq if masked else tb), masked=masked)
                    for h in range(HEADS) for q0 in range(0, tb, sq)]

        @pl.when(j < i)
        def _():
            _skewed((scores, new_max, weights, accumulate), tiles(False))

        @pl.when(j == i)
        def _():
            _skewed((scores, new_max, weights, accumulate), tiles(True))
            for h in range(HEADS):
                l = l_s[h]
                o_ref[:, HEAD_PAD * h:HEAD_PAD * (h + 1)] = jnp.transpose(acc_s[h] / l)
                lse_ref[h] = m_s[h] + jnp.log2(l)

    qspec = pl.BlockSpec((HEADS, tb, HEAD_PAD), lambda p, it, jt: (0, it[p], 0))
    kspec = pl.BlockSpec((HEADS, tb, HEAD_PAD), lambda p, it, jt: (0, jt[p], 0))
    vspec = pl.BlockSpec((HEADS, HEAD_PAD, tb), lambda p, it, jt: (0, 0, jt[p]))
    grid_spec = pltpu.PrefetchScalarGridSpec(
        num_scalar_prefetch=2, grid=(len(pairs),),
        in_specs=[qspec, kspec, vspec],
        out_specs=[pl.BlockSpec((tb, MLA_WIDTH), lambda p, it, jt: (it[p], 0)),
                   pl.BlockSpec((HEADS, 1, tb), lambda p, it, jt: (0, 0, it[p]))],
        scratch_shapes=[pltpu.VMEM((HEADS, 1, tb), F32), pltpu.VMEM((HEADS, 1, tb), F32), pltpu.VMEM((HEADS, HEAD_PAD, tb), F32)])
    return pl.pallas_call(
        body, grid_spec=grid_spec, name="attn_fwd",
        out_shape=[jax.ShapeDtypeStruct((T, MLA_WIDTH), F32), jax.ShapeDtypeStruct((HEADS, 1, T), F32)],
        compiler_params=_cparams(),
    )(*_pair_tables(pairs), q, k, vt)


def _sgu_fwd_chunk(vn_c, wcat, bz, masks):
    vstack = jnp.concatenate([jnp.where(mk, vn_c, 0.0).astype(BF16) for mk in masks], axis=0)
    return _dot(wcat, vstack) + bz


def _pool_counts(i, tm):
    pos1 = (i * tm + 1 + lax.broadcasted_iota(jnp.int32, (tm, POOL_WIDTH), 0)).astype(F32)
    lane = lax.broadcasted_iota(jnp.int32, (tm, POOL_WIDTH), 1)
    win = jnp.where(lane < 64, 2.0, jnp.where(lane < 128, 4.0, jnp.where(lane < 192, 8.0, 16.0)))
    return jnp.minimum(pos1, win), lane


def _by_group(lane, s2, s4, s8, s16):
    return jnp.where(lane < 64, s2, jnp.where(lane < 128, s4, jnp.where(lane < 192, s8, s16)))


def _pool_means(pin, halo, i, tm):
    s1 = jnp.concatenate([halo, pin], axis=0)
    s2 = s1 + pltpu.roll(s1, 1, 0)
    s4 = s2 + pltpu.roll(s2, 2, 0)
    s8 = s4 + pltpu.roll(s4, 4, 0)
    s16 = s8 + pltpu.roll(s8, 8, 0)
    cnt, lane = _pool_counts(i, tm)
    sel = _by_group(lane, s2[POOL_HALO:], s4[POOL_HALO:], s8[POOL_HALO:], s16[POOL_HALO:])
    return sel / cnt - pin


def _mixers_fwd_tile(i, tm, a, u, vs, pin, halo, gsgu, wcat, bz, wp, pscale, goa, gos, gop):
    vn, rv = _rms(vs, gsgu, SGU_WIDTH)
    masks = _head_masks((CHUNK, SGU_WIDTH), SGU_HEAD_DIM)
    zc = jnp.concatenate([_sgu_fwd_chunk(vn[CHUNK * c:CHUNK * (c + 1)], wcat, bz, masks) for c in range(tm // CHUNK)], axis=0)
    gm = u * zc
    halo = jnp.where(i > 0, halo, 0.0)
    m = _pool_means(pin, halo, i, tm).astype(BF16)
    yp_pre = _dot(m, wp)
    yp = yp_pre * pscale
    na, ra = _rms(a, goa, MLA_WIDTH)
    ng, rg = _rms(gm, gos, SGU_WIDTH)
    npo, rp = _rms(yp, gop, POOL_WIDTH)
    mix = jnp.concatenate([na, ng, npo], axis=1).astype(BF16)
    return vn, rv, zc, gm, m, yp_pre, yp, ra, rg, rp, mix


def _z_specs(tm):
    col = lambda c: pl.BlockSpec((tm, 256), lambda i: (i, c))
    halo = pl.BlockSpec((POOL_HALO, 256), lambda i: (jnp.maximum(i * (tm // POOL_HALO) - 1, 0), 4))
    return [col(2), col(3), col(4), halo]


def mixers_fwd(x, a, z, gsgu, wcat, bz, wp, pscale, goa, gos, gop, wout):
    T = x.shape[0]
    tm = min(ROW_TILE, T)

    def body(x_ref, a_ref, u_ref, vs_ref, pin_ref, halo_ref, gsgu_ref, wcat_ref, bz_ref, wp_ref, ps_ref,
             goa_ref, gos_ref, gop_ref, wout_ref, x1_ref):
        i = pl.program_id(0)
        mix = _mixers_fwd_tile(i, tm, a_ref[...], u_ref[...], vs_ref[...], pin_ref[...], halo_ref[...], gsgu_ref[...],
                               wcat_ref[...], bz_ref[...], wp_ref[...], ps_ref[...], goa_ref[...], gos_ref[...],
                               gop_ref[...])[-1]
        x1_ref[...] = x_ref[...] + _dot(mix, wout_ref[...])

    row = lambda w: pl.BlockSpec((tm, w), lambda i: (i, 0))
    return pl.pallas_call(
        body, grid=(T // tm,), name="mixers_fwd",
        in_specs=[row(D_MODEL), row(MLA_WIDTH)] + _z_specs(tm) + [WHOLE] * 9,
        out_specs=row(D_MODEL),
        out_shape=jax.ShapeDtypeStruct((T, D_MODEL), F32),
        compiler_params=_cparams(),
    )(x, a, z, z, z, z, gsgu, wcat, bz, wp, pscale, goa, gos, gop, wout)


def ffn_fwd(x1, gffn, wg, wu, wd, target=None):
    T = x1.shape[0]
    tm = min(ROW_TILE, T)
    with_loss = target is not None

    def body(*refs):
        x1_ref, gffn_ref, wg_ref, wu_ref, wd_ref = refs[:5]
        outs = refs[6:] if with_loss else refs[5:]
        x1v = x1_ref[...]
        h2 = _rms(x1v, gffn_ref[...], D_MODEL)[0].astype(BF16)
        acc = x1v
        for c in range(FFN_HIDDEN // FFN_CHUNK):
            sl = slice(FFN_CHUNK * c, FFN_CHUNK * (c + 1))
            g = _dot_nt(h2, wg_ref[sl, :])
            u = _dot_nt(h2, wu_ref[sl, :])
            outs[-2][:, sl] = g.astype(BF16)
            outs[-1][:, sl] = u.astype(BF16)
            act = (g * _sigmoid(g) * u).astype(BF16)
            acc = acc + _dot(act, wd_ref[sl, :])
        if not with_loss:
            outs[0][...] = acc
            return
        dy_ref, loss_ref = outs[0], outs[1]

        @pl.when(pl.program_id(0) == 0)
        def _():
            loss_ref[...] = jnp.zeros(loss_ref.shape, F32)

        err = acc - refs[5][...]
        dy_ref[...] = err * (1.0 / D_MODEL)
        per_row = jnp.sum(err * err, axis=1, keepdims=True) * (1.0 / D_MODEL)
        loss_ref[...] += 0.5 * jnp.sum(per_row, axis=0, keepdims=True)

    row = lambda w: pl.BlockSpec((tm, w), lambda i: (i, 0))
    hidden = [jax.ShapeDtypeStruct((T, FFN_HIDDEN), BF16)] * 2
    if with_loss:
        return pl.pallas_call(
            body, grid=(T // tm,), name="ffn_fwd_loss",
            in_specs=[row(D_MODEL)] + [WHOLE] * 4 + [row(D_MODEL)],
            out_specs=[row(D_MODEL), pl.BlockSpec((1, 1), lambda i: (0, 0)), row(FFN_HIDDEN), row(FFN_HIDDEN)],
            out_shape=[jax.ShapeDtypeStruct((T, D_MODEL), F32), jax.ShapeDtypeStruct((1, 1), F32)] + hidden,
            compiler_params=_cparams(),
        )(x1, gffn, wg, wu, wd, target)
    return pl.pallas_call(
        body, grid=(T // tm,), name="ffn_fwd",
        in_specs=[row(D_MODEL)] + [WHOLE] * 4,
        out_specs=[row(D_MODEL), row(FFN_HIDDEN), row(FFN_HIDDEN)],
        out_shape=[jax.ShapeDtypeStruct((T, D_MODEL), F32)] + hidden,
        compiler_params=_cparams(),
    )(x1, gffn, wg, wu, wd)


def _acc_spec(shape):
    return pl.BlockSpec(shape, lambda i: (0,) * len(shape))


def ffn_bwd(dx2, x1, gs, us, gffn, wg, wu, wd):
    T = x1.shape[0]
    tm = min(ROW_TILE // 2, T)

    def body(dx2_ref, x1_ref, gs_ref, us_ref, gffn_ref, wg_ref, wu_ref, wd_ref,
             dx1_ref, h2_ref, act_ref, dg_ref, du_ref, dgffn_ref):
        @pl.when(pl.program_id(0) == 0)
        def _():
            dgffn_ref[...] = jnp.zeros(dgffn_ref.shape, F32)

        dx2v = dx2_ref[...]
        dy = dx2v.astype(BF16)
        x1v = x1_ref[...]
        h2, r = _rms(x1v, gffn_ref[...], D_MODEL)
        h2_ref[...] = h2.astype(BF16)
        for c in range(FFN_HIDDEN // FFN_CHUNK):
            sl = slice(FFN_CHUNK * c, FFN_CHUNK * (c + 1))
            g = gs_ref[:, sl].astype(F32)
            u = us_ref[:, sl].astype(F32)
            dact = _dot_nt(dy, wd_ref[sl, :])
            sg = _sigmoid(g)
            silu = g * sg
            act_ref[:, sl] = (silu * u).astype(BF16)
            dg_ref[:, sl] = (dact * u * (sg * (1.0 + g * (1.0 - sg)))).astype(BF16)
            du_ref[:, sl] = (dact * silu).astype(BF16)
        dh2 = _dot(dg_ref[...], wg_ref[...]) + _dot(du_ref[...], wu_ref[...])
        dxn, dgn = _rms_bwd(x1v, r, gffn_ref[...], dh2, D_MODEL)
        dx1_ref[...] = dx2v + dxn
        dgffn_ref[...] += dgn

    row = lambda w: pl.BlockSpec((tm, w), lambda i: (i, 0))
    return pl.pallas_call(
        body, grid=(T // tm,), name="ffn_bwd",
        in_specs=[row(D_MODEL), row(D_MODEL), row(FFN_HIDDEN), row(FFN_HIDDEN)] + [WHOLE] * 4,
        out_specs=[row(D_MODEL), row(D_MODEL), row(FFN_HIDDEN), row(FFN_HIDDEN), row(FFN_HIDDEN), _acc_spec((1, D_MODEL))],
        out_shape=[jax.ShapeDtypeStruct((T, D_MODEL), F32), jax.ShapeDtypeStruct((T, D_MODEL), BF16),
                   jax.ShapeDtypeStruct((T, FFN_HIDDEN), BF16), jax.ShapeDtypeStruct((T, FFN_HIDDEN), BF16),
                   jax.ShapeDtypeStruct((T, FFN_HIDDEN), BF16), jax.ShapeDtypeStruct((1, D_MODEL), F32)],
        compiler_params=_cparams(),
    )(dx2, x1, gs, us, gffn, wg, wu, wd)


TN_K_TILE = 2048
TN_ACC_BYTES = 6 * 1024 * 1024


def matmul_tn(a, b, name):
    T, M = a.shape
    N = b.shape[1]
    tk = min(TN_K_TILE, T)
    tm = M if M <= 1024 else M // 2
    tn = max(d for d in range(128, N + 1, 128) if N % d == 0 and tm * d * 4 <= TN_ACC_BYTES)
    nk = T // tk

    def body(a_ref, b_ref, o_ref, acc):
        k = pl.program_id(2)

        @pl.when(k == 0)
        def _():
            acc[...] = jnp.zeros(acc.shape, F32)

        acc[...] += _dot_tn(a_ref[...].astype(BF16), b_ref[...].astype(BF16))

        @pl.when(k == nk - 1)
        def _():
            o_ref[...] = acc[...].astype(BF16)

    return pl.pallas_call(
        body, grid=(M // tm, N // tn, nk), name=name,
        in_specs=[pl.BlockSpec((tk, tm), lambda i, j, k: (k, i)), pl.BlockSpec((tk, tn), lambda i, j, k: (k, j))],
        out_specs=pl.BlockSpec((tm, tn), lambda i, j, k: (i, j)),
        out_shape=jax.ShapeDtypeStruct((M, N), BF16),
        scratch_shapes=[pltpu.VMEM((tm, tn), F32)],
        compiler_params=_cparams(),
    )(a, b)


def mixers_bwd(dx1, a, z, gsgu, wcat, wcat_t, bz, wp, pscale, goa, gos, gop, wout):
    T = a.shape[0]
    tm = min(ROW_TILE, T)

    def body(dx1_ref, a_ref, u_ref, vs_ref, pin_ref, halo_ref, gsgu_ref, wcat_ref, wcatt_ref, bz_ref, wp_ref,
             ps_ref, goa_ref, gos_ref, gop_ref, wout_ref,
             da_ref, delta_ref, du_ref, dvs_ref, dm_ref, mix_ref,
             dgoa_ref, dgos_ref, dgop_ref, dps_ref, dwp_ref, dwsp_ref, db_ref, dgsgu_ref):
        i = pl.program_id(0)

        @pl.when(i == 0)
        def _():
            for r in (dgoa_ref, dgos_ref, dgop_ref, dps_ref, dwp_ref, dwsp_ref, db_ref, dgsgu_ref):
                r[...] = jnp.zeros(r.shape, F32)

        a_v, u, vs = a_ref[...], u_ref[...], vs_ref[...]
        goa, gos, gop, pscale_v = goa_ref[...], gos_ref[...], gop_ref[...], ps_ref[...]
        vn, rv, zc, gm, m, yp_pre, yp, ra, rg, rp, mix = _mixers_fwd_tile(
            i, tm, a_v, u, vs, pin_ref[...], halo_ref[...], gsgu_ref[...], wcat_ref[...], bz_ref[...], wp_ref[...],
            pscale_v, goa, gos, gop)
        mix_ref[...] = mix
        dmix = _dot_nt(dx1_ref[...].astype(BF16), wout_ref[...])
        da, dgoa = _rms_bwd(a_v, ra, goa, dmix[:, :MLA_WIDTH], MLA_WIDTH)
        dgm, dgos = _rms_bwd(gm, rg, gos, dmix[:, MLA_WIDTH:MLA_WIDTH + SGU_WIDTH], SGU_WIDTH)
        dyp, dgop = _rms_bwd(yp, rp, gop, dmix[:, MLA_WIDTH + SGU_WIDTH:], POOL_WIDTH)
        da_ref[...] = da
        dgoa_ref[...] += dgoa
        dgos_ref[...] += dgos
        dgop_ref[...] += dgop
        prod = da * a_v
        ones = jnp.ones((8, HEAD_PAD), F32)
        for h in range(HEADS):
            lo = HEAD_PAD * h
            sums = lax.dot_general(ones, prod[:, lo:lo + HEAD_PAD], (((1,), (1,)), ((), ())), preferred_element_type=F32,
                                   precision=lax.Precision.HIGHEST)
            delta_ref[h] = sums[0:1, :]
        dps_ref[...] += jnp.sum(dyp * yp_pre, axis=0, keepdims=True)
        dyp_pre = (dyp * pscale_v).astype(BF16)
        dwp_ref[...] += _dot_tn(m, dyp_pre)
        dm_ref[...] = _dot_nt(dyp_pre, wp_ref[...])
        du_ref[...] = dgm * zc
        dzc = dgm * u
        masks = _head_masks((CHUNK, SGU_WIDTH), SGU_HEAD_DIM)
        lane_b = lax.broadcasted_iota(jnp.int32, (CHUNK, HEAD_PAD), 1)
        dvn_parts = []
        dwsp = jnp.zeros(dwsp_ref.shape, F32)
        db = jnp.zeros(db_ref.shape, F32)
        for c in range(tm // CHUNK):
            dz_c = dzc[CHUNK * c:CHUNK * (c + 1)]
            dzstack = jnp.concatenate([jnp.where(mk, dz_c, 0.0).astype(BF16) for mk in masks], axis=0)
            dvn_parts.append(_dot(wcatt_ref[...], dzstack))
            dwsp = dwsp + _dot_nt(dzstack, vn[CHUNK * c:CHUNK * (c + 1)].astype(BF16))
            for h, mk in enumerate(masks):
                col = jnp.sum(jnp.where(mk, dz_c, 0.0), axis=1, keepdims=True)
                db = db + jnp.where(lane_b == h, col, 0.0)
        dwsp_ref[...] += dwsp
        db_ref[...] += db
        dvs, dgsgu = _rms_bwd(vs, rv, gsgu_ref[...], jnp.concatenate(dvn_parts, axis=0), SGU_WIDTH)
        dvs_ref[...] = dvs
        dgsgu_ref[...] += dgsgu

    row = lambda w: pl.BlockSpec((tm, w), lambda i: (i, 0))
    head = pl.BlockSpec((HEADS, 1, tm), lambda i: (0, 0, i))
    acc_shapes = [(1, MLA_WIDTH), (1, SGU_WIDTH), (1, POOL_WIDTH), (1, POOL_WIDTH), (POOL_WIDTH, POOL_WIDTH),
                  (HEADS * CHUNK, CHUNK), (CHUNK, HEAD_PAD), (1, SGU_WIDTH)]
    return pl.pallas_call(
        body, grid=(T // tm,), name="mixers_bwd",
        in_specs=[row(D_MODEL), row(MLA_WIDTH)] + _z_specs(tm) + [WHOLE] * 10,
        out_specs=[row(MLA_WIDTH), head, row(256), row(256), row(256), row(D_MODEL)] + [_acc_spec(s) for s in acc_shapes],
        out_shape=[jax.ShapeDtypeStruct((T, MLA_WIDTH), F32), jax.ShapeDtypeStruct((HEADS, 1, T), F32),
                   jax.ShapeDtypeStruct((T, 256), F32), jax.ShapeDtypeStruct((T, 256), F32),
                   jax.ShapeDtypeStruct((T, 256), F32), jax.ShapeDtypeStruct((T, D_MODEL), BF16)]
                  + [jax.ShapeDtypeStruct(s, F32) for s in acc_shapes],
        compiler_params=_cparams(),
    )(dx1, a, z, z, z, z, gsgu, wcat, wcat_t, bz, wp, pscale, goa, gos, gop, wout)


def pool_bwd(dm):
    T = dm.shape[0]
    tm = min(4 * ROW_TILE, T)
    nt = T // tm

    def body(dm_ref, next_ref, dpin_ref):
        i = pl.program_id(0)
        cnt, lane = _pool_counts(i, tm)
        dmv = dm_ref[...]
        win = _by_group(lane[:POOL_HALO], 2.0, 4.0, 8.0, 16.0)
        nxt = jnp.where(i < nt - 1, next_ref[...] / win, 0.0)
        r1 = jnp.concatenate([dmv / cnt, nxt], axis=0)
        n = tm + POOL_HALO
        r2 = r1 + pltpu.roll(r1, n - 1, 0)
        r4 = r2 + pltpu.roll(r2, n - 2, 0)
        r8 = r4 + pltpu.roll(r4, n - 4, 0)
        r16 = r8 + pltpu.roll(r8, n - 8, 0)
        dpin_ref[...] = _by_group(lane, r2[:tm], r4[:tm], r8[:tm], r16[:tm]) - dmv

    return pl.pallas_call(
        body, grid=(nt,), name="pool_bwd",
        in_specs=[pl.BlockSpec((tm, 256), lambda i: (i, 0)),
                  pl.BlockSpec((POOL_HALO, 256), lambda i: (jnp.minimum((i + 1) * (tm // POOL_HALO), T // POOL_HALO - 1), 0))],
        out_specs=pl.BlockSpec((tm, 256), lambda i: (i, 0)),
        out_shape=jax.ShapeDtypeStruct((T, 256), F32),
        compiler_params=_cparams(),
    )(dm, dm)


def attn_bwd(q, k, kt, v, do, lse, delta):
    _, T, _ = q.shape
    tb = min(ATTN_TILE, T)
    sb = min(ATTN_SUB, tb)
    ns = tb // sb
    nb = T // tb
    hb = ATTN_BWD_HEADS

    pairs = [(i, j) for i in range(nb) for j in range(i, nb)]

    def body(i_tab, j_tab, q_ref, k_ref, kt_ref, v_ref, do_ref, lse_ref, delta_ref, dqt_ref, dk_ref, dv_ref, dk_s, dv_s):
        i, j = i_tab[pl.program_id(1)], j_tab[pl.program_id(1)]

        @pl.when(pl.program_id(1) == 0)
        def _():
            dqt_ref[...] = jnp.zeros(dqt_ref.shape, F32)

        @pl.when(j == i)
        def _():
            dk_s[...] = jnp.zeros(dk_s.shape, F32)
            dv_s[...] = jnp.zeros(dv_s.shape, F32)

        def sub_block(h, a, b, masked):
            qa = slice(sb * a, sb * (a + 1))
            kb = slice(sb * b, sb * (b + 1))
            qv = q_ref[h, qa, :]
            dov = do_ref[qa, HEAD_PAD * h:HEAD_PAD * (h + 1)].astype(BF16)
            st = _dot_nt(k_ref[h, kb, :], qv)
            dpt = _dot_nt(v_ref[h, kb, :], dov)
            pt = jnp.exp2(st - lse_ref[h, :, qa])
            if masked:
                krow = lax.broadcasted_iota(jnp.int32, st.shape, 0)
                qcol = lax.broadcasted_iota(jnp.int32, st.shape, 1)
                pt = jnp.where(krow <= qcol, pt, 0.0)
            dst = (pt * (dpt - delta_ref[h, :, qa])).astype(BF16)
            dv_s[h, kb, :] += _dot(pt.astype(BF16), dov)
            dk_s[h, kb, :] += _dot(dst, qv)
            cols = pl.ds(pl.multiple_of(j * tb + sb * a, sb), sb)
            dqt_ref[h, :, cols] += _dot(kt_ref[h, :, kb], dst)

        @pl.when(j > i)
        def _():
            for a in range(ns):
                for h in range(hb):
                    for b in range(ns):
                        sub_block(h, a, b, False)

        @pl.when(j == i)
        def _():
            for a in range(ns):
                for h in range(hb):
                    for b in range(a + 1):
                        sub_block(h, a, b, a == b)

        @pl.when(j == nb - 1)
        def _():
            dk_ref[...] = dk_s[...] * (1.0 / LOG2E)
            dv_ref[...] = dv_s[...]

    qspec = pl.BlockSpec((hb, tb, HEAD_PAD), lambda g, p, it, jt: (g, jt[p], 0))
    kspec = pl.BlockSpec((hb, tb, HEAD_PAD), lambda g, p, it, jt: (g, it[p], 0))
    ktspec = pl.BlockSpec((hb, HEAD_PAD, tb), lambda g, p, it, jt: (g, 0, it[p]))
    rowspec = pl.BlockSpec((hb, 1, tb), lambda g, p, it, jt: (g, 0, jt[p]))
    grid_spec = pltpu.PrefetchScalarGridSpec(
        num_scalar_prefetch=2, grid=(HEADS // hb, len(pairs)),
        in_specs=[qspec, kspec, ktspec, kspec, pl.BlockSpec((tb, hb * HEAD_PAD), lambda g, p, it, jt: (jt[p], g)), rowspec, rowspec],
        out_specs=[pl.BlockSpec((hb, HEAD_PAD, T), lambda g, p, it, jt: (g, 0, 0)), kspec, kspec],
        scratch_shapes=[pltpu.VMEM((hb, tb, HEAD_PAD), F32), pltpu.VMEM((hb, tb, HEAD_PAD), F32)])
    return pl.pallas_call(
        body, grid_spec=grid_spec, name="attn_bwd",
        out_shape=[jax.ShapeDtypeStruct((HEADS, HEAD_PAD, T), F32)] + [jax.ShapeDtypeStruct((HEADS, T, HEAD_PAD), F32)] * 2,
        compiler_params=_cparams(),
    )(*_pair_tables(pairs), q, k, kt, v, do, lse, delta)


def mixin_bwd(dres, x, z, rope, dqt, dk, dv, du, dvs, dpin, gmix, win, gql, wq, gkvl, wkv, gqh, gkh):
    T = x.shape[0]
    tm = min(ROW_TILE, T)
    part = min(MIXIN_BWD_PART, tm)

    def body(dres_ref, x_ref, z_ref, c_ref, sa_ref, sb_ref, dqt_ref, dk_ref, dv_ref, du_ref, dvs_ref, dpin_ref,
             gmix_ref, win_ref, gql_ref, wq_ref, gkvl_ref, wkv_ref, gqh_ref, gkh_ref,
             dx_ref, hn_ref, dz_ref,
             dgmix_ref, dgql_ref, dgkvl_ref, dgqh_ref, dgkh_ref, dwq_ref, dwkv_ref):
        @pl.when(pl.program_id(0) == 0)
        def _():
            for r in (dgmix_ref, dgql_ref, dgkvl_ref, dgqh_ref, dgkh_ref, dwq_ref, dwkv_ref):
                r[...] = jnp.zeros(r.shape, F32)

        lane = lax.broadcasted_iota(jnp.int32, (part, HEAD_PAD), 1)
        rope_lanes = (lane >= QK_NOPE) & (lane < QK_HEAD)

        def recompute(g):
            rows = g["rows"]
            g["xv"] = x_ref[rows, :]
            hn, g["rx"] = _rms(g["xv"], gmix_ref[...], D_MODEL)
            hn_ref[rows, :] = hn.astype(BF16)
            g["ql"], g["kvl"], g["kr"], g["qn"], g["rq"], g["kvn"], g["rkv"], g["q_up"], g["kv_up"] = _qkv_pre(
                z_ref[rows, :], gql_ref[...], wq_ref[...], gkvl_ref[...], wkv_ref[...])

        def heads(g):
            rows = g["rows"]
            c, sa, sb = c_ref[rows, :], sa_ref[rows, :], sb_ref[rows, :]
            dkr = jnp.zeros((part, HEAD_PAD), F32)
            dgqh = jnp.zeros((1, HEAD_PAD), F32)
            dgkh = jnp.zeros((1, HEAD_PAD), F32)
            dq_parts, dk_parts, dv_parts = [], [], []
            for h in range(HEADS):
                lo = HEAD_PAD * h
                qh = g["q_up"][:, lo:lo + HEAD_PAD]
                rqh = lax.rsqrt(jnp.sum(qh * qh, axis=-1, keepdims=True) * (1.0 / QK_HEAD) + EPS)
                dq_h = jnp.transpose(dqt_ref[h, :, rows]) * ATTN_SCALE
                dqh, dg = _rms_bwd(qh, rqh, gqh_ref[...], _rope_bwd(dq_h, c, sa, sb), QK_HEAD)
                dgqh = dgqh + dg
                dq_parts.append(dqh)
                kh = g["kv_up"][:, lo:lo + HEAD_PAD] + g["kr"]
                rkh = lax.rsqrt(jnp.sum(kh * kh, axis=-1, keepdims=True) * (1.0 / QK_HEAD) + EPS)
                dkh, dg = _rms_bwd(kh, rkh, gkh_ref[...], _rope_bwd(dk_ref[h, rows, :], c, sa, sb), QK_HEAD)
                dgkh = dgkh + dg
                dkr = dkr + jnp.where(rope_lanes, dkh, 0.0)
                dk_parts.append(dkh)
                dv_parts.append(dv_ref[h, rows, :])
            dgqh_ref[...] += dgqh
            dgkh_ref[...] += dgkh
            g["dkr"] = dkr
            g["dq_up"] = jnp.concatenate(dq_parts, axis=1).astype(BF16)
            g["dkv_up"] = jnp.concatenate(dk_parts + dv_parts, axis=1).astype(BF16)

        def latents(g):
            rows = g["rows"]
            dwq_ref[...] += _dot_tn(g["dq_up"], g["qn"])
            dwkv_ref[...] += _dot_tn(g["dkv_up"], g["kvn"])
            dql, dg = _rms_bwd(g["ql"], g["rq"], gql_ref[...], _dot(g["dq_up"], wq_ref[...]), Q_LORA)
            dgql_ref[...] += dg
            dkvl, dg = _rms_bwd(g["kvl"], g["rkv"], gkvl_ref[...], _dot(g["dkv_up"], wkv_ref[...]), KV_LORA)
            dgkvl_ref[...] += dg
            g["dz"] = jnp.concatenate([dql, dkvl, g["dkr"], du_ref[rows, :], dvs_ref[rows, :], dpin_ref[rows, :]],
                                      axis=1).astype(BF16)
            dz_ref[rows, :] = g["dz"]

        def inputs(g):
            rows = g["rows"]
            dxn, dg = _rms_bwd(g["xv"], g["rx"], gmix_ref[...], _dot(g["dz"], win_ref[...]), D_MODEL)
            dgmix_ref[...] += dg
            dx_ref[rows, :] = dres_ref[rows, :] + dxn

        _skewed((recompute, heads, latents, inputs), [dict(rows=pl.ds(part * p, part)) for p in range(tm // part)])

    row = lambda w: pl.BlockSpec((tm, w), lambda i: (i, 0))
    head = pl.BlockSpec((HEADS, tm, HEAD_PAD), lambda i: (0, i, 0))
    head_t = pl.BlockSpec((HEADS, HEAD_PAD, tm), lambda i: (0, 0, i))
    acc_shapes = [(1, D_MODEL), (1, Q_LORA), (1, KV_LORA), (1, HEAD_PAD), (1, HEAD_PAD),
                  (HEADS * HEAD_PAD, Q_LORA), (2 * HEADS * HEAD_PAD, KV_LORA)]
    out_rows = [(D_MODEL, F32), (D_MODEL, BF16), (Z_WIDTH, BF16)]
    return pl.pallas_call(
        body, grid=(T // tm,), name="mixin_bwd",
        in_specs=[row(D_MODEL), row(D_MODEL), row(Z_WIDTH)] + [row(HEAD_PAD)] * 3 + [head_t, head, head, row(256), row(256), row(256)]
                 + [WHOLE] * 8,
        out_specs=[row(w) for w, _ in out_rows] + [_acc_spec(s) for s in acc_shapes],
        out_shape=[jax.ShapeDtypeStruct((T, w), dt) for w, dt in out_rows] + [jax.ShapeDtypeStruct(s, F32) for s in acc_shapes],
        compiler_params=_cparams(),
    )(dres, x, z, *rope, dqt, dk, dv, du, dvs, dpin, gmix, win, gql, wq, gkvl, wkv, gqh, gkh)


def _place():
    x, y, c = lax.axis_index("x"), lax.axis_index("y"), lax.axis_index("c")
    return x, y, c, 4 * x + 2 * y + c


def _layer_of(ref, l):
    return ref[:, l, :] if ref.shape[1] == DEPTH else ref[l]


def _layer_shape(shard):
    return (shard.shape[0], shard.shape[2]) if shard.shape[1] == DEPTH else shard.shape[1:]


def cast_shards(shards, wanted):
    n = len(shards)

    def body(*refs):
        for o_ref, (w, l) in zip(refs[n:], wanted):
            o_ref[...] = _layer_of(refs[w], l).astype(BF16)

    return pl.pallas_call(
        body, name="cast_shards", in_specs=[WHOLE] * n, out_specs=[WHOLE] * len(wanted),
        out_shape=[jax.ShapeDtypeStruct(_layer_shape(shards[w]), BF16) for w, _ in wanted],
        compiler_params=_cparams(),
    )(*shards)


def _peer(k):
    x, y, c, _ = _place()
    px = 1 - x if k & 4 else x
    py = 1 - y if k & 2 else y
    pc = 1 - c if k & 1 else c
    return (px, py, pc), 4 * px + 2 * py + pc


def exchange_start(srcs, after, gather, name):
    n = len(srcs)
    land_shapes = [((N_DEV,) + s.shape) if gather else s.shape for s in srcs]

    def body(*refs):
        src_refs, land_refs = refs[:n], refs[n:2 * n]
        send_sems, recv_sems = refs[2 * n + 1:3 * n + 1], refs[3 * n + 1:4 * n + 1]
        token = refs[-1]
        _, _, _, me = _place()
        for k in range(1, N_DEV):
            peer, peer_id = _peer(k)
            for w in range(n):
                pltpu.make_async_remote_copy(
                    src_ref=src_refs[w] if gather else src_refs[w].at[peer_id], dst_ref=land_refs[w].at[me],
                    send_sem=send_sems[w], recv_sem=recv_sems[w], device_id=peer, device_id_type=MESH).start()
        for w in range(n):
            pltpu.make_async_copy(src_refs[w] if gather else src_refs[w].at[me], land_refs[w].at[me], recv_sems[w]).start()
        token[...] = jnp.zeros(token.shape, F32)

    hbm = lambda a: pltpu.with_memory_space_constraint(a, pltpu.HBM)
    outs = pl.pallas_call(
        body, name=name,
        out_shape=(pltpu.SemaphoreType.DMA(()),) * (2 * n)
                  + tuple(pltpu.HBM(s.shape, BF16) for s in srcs) + tuple(pltpu.HBM(s, BF16) for s in land_shapes)
                  + (jax.ShapeDtypeStruct((8, 128), F32),),
        in_specs=[HBM_SPEC] * (2 * n) + [ANY],
        out_specs=(SEM_SPEC,) * (2 * n) + (HBM_SPEC,) * (2 * n) + (WHOLE,),
        input_output_aliases={i: 2 * n + i for i in range(2 * n)},
        compiler_params=pltpu.CompilerParams(has_side_effects=pltpu.SideEffectType.DATAFLOW_SIDE_EFFECTING),
    )(*[hbm(s) for s in srcs], *[hbm(lax.empty(s, BF16)) for s in land_shapes], after)
    return list(outs[:n]), list(outs[n:2 * n]), list(outs[2 * n:3 * n]), list(outs[3 * n:4 * n]), outs[-1]


def exchange_wait(started, after, name):
    send_sems, recv_sems, srcs, lands, _ = started
    n = len(srcs)

    def body(*refs):
        land_refs = refs[n:2 * n]
        send_sems, recv_sems = refs[2 * n:3 * n], refs[3 * n:4 * n]
        x, y, c, _ = _place()
        for w in range(n):
            seven, eight = land_refs[w].at[pl.ds(0, N_DEV - 1)], land_refs[w]
            pltpu.make_async_remote_copy(src_ref=seven, dst_ref=seven, send_sem=send_sems[w], recv_sem=recv_sems[w],
                                         device_id=(x, y, c), device_id_type=MESH).wait_send()
            pltpu.make_async_remote_copy(src_ref=eight, dst_ref=eight, send_sem=send_sems[w], recv_sem=recv_sems[w],
                                         device_id=(x, y, c), device_id_type=MESH).wait_recv()

    outs = pl.pallas_call(
        body, name=name,
        out_shape=tuple(pltpu.HBM(s.shape, BF16) for s in srcs) + tuple(pltpu.HBM(l.shape, BF16) for l in lands),
        in_specs=[HBM_SPEC] * (2 * n) + [SEM_SPEC] * (2 * n) + [ANY],
        out_specs=(HBM_SPEC,) * (2 * n),
        input_output_aliases={i: i for i in range(2 * n)},
        compiler_params=pltpu.CompilerParams(has_side_effects=pltpu.SideEffectType.DATAFLOW_SIDE_EFFECTING),
    )(*srcs, *lands, *send_sems, *recv_sems, after)
    return list(outs[n:])


def _adamw(w, g, m, v):
    m2 = ADAM_B1 * m + (1.0 - ADAM_B1) * g
    v2 = ADAM_B2 * v + (1.0 - ADAM_B2) * (g * g)
    m_hat = m2 / (1.0 - ADAM_B1 ** ADAM_STEP)
    v_hat = v2 / (1.0 - ADAM_B2 ** ADAM_STEP)
    delta = -ADAM_LR * (m_hat / (jnp.sqrt(v_hat) + ADAM_EPS) + ADAM_WD * w)
    return delta, m2, v2


def adamw_sharded(parts, w, m, v, name):
    L, R, C = w.shape
    fits = [d for d in range(16, min(R, 512) + 1, 16) if R % d == 0]
    br = max(fits) if fits else R
    nblk = R // br

    def body(*refs):
        p_refs = refs[:L]
        w_ref, m_ref, v_ref, g_ref, d_ref, m2_ref, v2_ref = refs[L:]

        def total(p_ref):
            g = p_ref[0].astype(F32)
            for s in range(1, N_DEV):
                g = g + p_ref[s].astype(F32)
            return g

        g = total(p_refs[0])
        for l in range(1, L):
            g = jnp.where(pl.program_id(0) == l, total(p_refs[l]), g)
        g_ref[...] = g
        d_ref[...], m2_ref[...], v2_ref[...] = _adamw(w_ref[...], g, m_ref[...], v_ref[...])

    blk = pl.BlockSpec((None, br, C), lambda l, i: (l, i, 0))

    def part_spec(k):
        return pl.BlockSpec((N_DEV, br, C), lambda l, i: (0, jnp.where(l == k, i, jnp.where(l < k, 0, nblk - 1)), 0))

    return pl.pallas_call(
        body, grid=(L, nblk), name=name,
        in_specs=[part_spec(k) for k in range(L)] + [blk, blk, blk],
        out_specs=[blk] * 4,
        out_shape=[jax.ShapeDtypeStruct((L, R, C), F32)] * 4,
        compiler_params=_cparams(),
    )(*parts, w, m, v)


def allreduce_adamw_small(gpart, w, m, v):
    R = gpart.shape[0]

    def body(g_ref, w_ref, m_ref, v_ref, grad_ref, d_ref, m2_ref, v2_ref, all_ref, send_sems, recv_sems):
        x, y, c, me = _place()
        sibling = (x, y, 1 - c)
        chips = [(1 - x, y), (x, 1 - y), (1 - x, 1 - y)]

        def copy(k, block_id, to, from_input):
            return pltpu.make_async_remote_copy(
                src_ref=g_ref if from_input else all_ref.at[block_id], dst_ref=all_ref.at[block_id],
                send_sem=send_sems.at[k], recv_sem=recv_sems.at[k], device_id=to, device_id_type=MESH)

        def block_of(cx, cy, cc):
            return 4 * cx + 2 * cy + cc

        all_ref[me] = g_ref[...]
        first = [copy(0, me, sibling, True)] + [copy(1 + j, me, (*chip, c), True) for j, chip in enumerate(chips)]
        for cp in first:
            cp.start()
        passed = [copy(4 + j, block_of(*chip, c), sibling, False) for j, chip in enumerate(chips)]
        for j, chip in enumerate(chips):
            copy(1 + j, block_of(*chip, c), (x, y, c), False).wait_recv()
            passed[j].start()
        copy(0, block_of(x, y, 1 - c), (x, y, c), False).wait_recv()
        for j, chip in enumerate(chips):
            copy(4 + j, block_of(*chip, 1 - c), (x, y, c), False).wait_recv()
        for cp in first + passed:
            cp.wait_send()
        g = all_ref[0]
        for s in range(1, N_DEV):
            g = g + all_ref[s]
        grad_ref[...] = g
        d_ref[...], m2_ref[...], v2_ref[...] = _adamw(w_ref[...], g, m_ref[...], v_ref[...])

    return pl.pallas_call(
        body, name="allreduce_adamw_small",
        in_specs=[WHOLE] * 4, out_specs=[WHOLE] * 4,
        out_shape=[jax.ShapeDtypeStruct((R, 128), F32)] * 4,
        scratch_shapes=[pltpu.VMEM((N_DEV, R, 128), F32), pltpu.SemaphoreType.DMA((7,)), pltpu.SemaphoreType.DMA((7,))],
        compiler_params=_cparams(),
    )(gpart, w, m, v)


def _shard_view(name, a):
    if name == "w_in":
        return a.transpose(2, 0, 1)
    return a.swapaxes(1, 2) if name in TRANSPOSED else a


def _shard_unview(name, a):
    if name == "w_in":
        return a.transpose(1, 2, 0)
    return a.swapaxes(1, 2) if name in TRANSPOSED else a


def _pad_head_rows(w, width):
    c = w.shape[1]
    return jnp.pad(w.reshape(HEADS, width, c), ((0, 0), (0, HEAD_PAD - width), (0, 0))).reshape(HEADS * HEAD_PAD, c)


def _unpad_head_rows(w, width):
    c = w.shape[1]
    return w.reshape(HEADS, HEAD_PAD, c)[:, :width].reshape(HEADS * width, c)


O1 = Q_LORA
O2 = O1 + KV_LORA
O3 = O2 + QK_ROPE


def _mixer_weights(gw):
    w_in = gw["w_in"].reshape(IN_WIDTH, D_MODEL)
    zero = lambda n: jnp.zeros((n, D_MODEL), BF16)
    win = jnp.concatenate([w_in[:O2], zero(QK_NOPE), w_in[O2:O3], zero(HEAD_PAD - QK_HEAD), w_in[O3:]], axis=0)
    wq = _pad_head_rows(gw["w_q_up"].reshape(HEADS * QK_HEAD, Q_LORA), QK_HEAD)
    w_kv = gw["w_kv_up"].reshape(HEADS, QK_NOPE + V_HEAD, KV_LORA)
    wk = jnp.pad(w_kv[:, :QK_NOPE], ((0, 0), (0, HEAD_PAD - QK_NOPE), (0, 0))).reshape(HEADS * HEAD_PAD, KV_LORA)
    wv = w_kv[:, QK_NOPE:].reshape(HEADS * V_HEAD, KV_LORA)
    wkv = jnp.concatenate([wk, wv], axis=0)
    out = dict(win=win, wq=wq, wkv=wkv)
    if "w_out" in gw:
        out["wout"] = gw["w_out"].reshape(D_MODEL, D_MODEL)
    return out


def _ffn_weights(gw):
    return dict(wg=gw["w_gate"].reshape(FFN_HIDDEN, D_MODEL), wu=gw["w_up"].reshape(FFN_HIDDEN, D_MODEL),
                wd=gw["w_down"].reshape(FFN_HIDDEN, D_MODEL))


def _layer_small(p, l):
    row = lambda a: a.reshape(1, -1)
    pad_head = lambda g: jnp.pad(g, (0, HEAD_PAD - QK_HEAD)).reshape(1, HEAD_PAD)
    tril = jnp.tril(jnp.ones((CHUNK, CHUNK), F32))
    wsp = p["w_spatial"][l] * tril
    wcat = jnp.concatenate([wsp[h] for h in range(HEADS)], axis=1).astype(BF16)
    wcat_t = jnp.concatenate([wsp[h].T for h in range(HEADS)], axis=1).astype(BF16)
    bz = jnp.repeat(p["b_spatial"][l].T, SGU_HEAD_DIM, axis=1)
    wp = jax.scipy.linalg.block_diag(*[p["w_pool"][l][g] for g in range(HEADS)]).astype(BF16)
    return dict(gmix=row(p["g_mix_norm"][l]), gql=row(p["g_q_lat"][l]), gkvl=row(p["g_kv_lat"][l]),
                gqh=pad_head(p["g_q_head"][l]), gkh=pad_head(p["g_k_head"][l]), gsgu=row(p["g_sgu_v"][l]),
                wcat=wcat, wcat_t=wcat_t, bz=bz, wp=wp, pscale=row(p["pool_scale"][l]),
                goa=row(p["g_out_mla"][l]), gos=row(p["g_out_sgu"][l]), gop=row(p["g_out_pool"][l]),
                gffn=row(p["g_ffn_norm"][l]))


MIXER_SIDE = ("w_in", "w_q_up", "w_kv_up", "w_out")
FFN_SIDE = ("w_gate", "w_up", "w_down")
TRANSPOSED = ("w_in", "w_q_up", "w_kv_up", "w_gate", "w_up")
FIRST_0 = ("w_in", "w_q_up", "w_kv_up")
LATER_0 = ("w_out",) + FFN_SIDE
SHARDED = MIXER_SIDE + FFN_SIDE
SMALL = ("g_mix_norm", "g_q_lat", "g_kv_lat", "g_q_head", "g_k_head", "g_sgu_v", "w_spatial", "b_spatial", "w_pool",
         "pool_scale", "g_out_mla", "g_out_sgu", "g_out_pool", "g_ffn_norm")
WEIGHTS = ("g_mix_norm", "w_in", "g_q_lat", "w_q_up", "g_kv_lat", "w_kv_up", "g_q_head", "g_k_head", "g_sgu_v", "w_spatial",
           "b_spatial", "w_pool", "pool_scale", "g_out_mla", "g_out_sgu", "g_out_pool", "w_out", "g_ffn_norm", "w_gate",
           "w_up", "w_down")
PACKED = SMALL + ("loss",)
PACK_ROWS = 8 * 128


def _pack_small(parts):
    flat = []
    for name in PACKED:
        a = parts[name].reshape(-1)
        flat.append(jnp.pad(a, (0, -a.shape[0] % PACK_ROWS)))
    return jnp.concatenate(flat).reshape(-1, 128)


def _unpack_small(packed, like):
    out, row = {}, 0
    for name in PACKED:
        n = math.prod(like[name].shape)
        rows = -(-n // PACK_ROWS) * 8
        out[name] = packed[row:row + rows].reshape(-1)[:n].reshape(like[name].shape)
        row += rows
    return out


def _forward_attention(x, rope, W, S):
    z, q, k, v, kt, vt = mixin_fwd(x, rope, S["gmix"], W["win"], S["gql"], W["wq"], S["gkvl"], W["wkv"], S["gqh"], S["gkh"])
    a, lse = attn_fwd(q, k, vt)
    return dict(x=x, z=z, q=q, k=k, kt=kt, v=v, a=a, lse=lse)


def _forward_mixers(A, wout, S):
    A["x1"] = mixers_fwd(A["x"], A["a"], A["z"], S["gsgu"], S["wcat"], S["bz"], S["wp"], S["pscale"], S["goa"], S["gos"],
                         S["gop"], wout)
    return A["x1"]


def _backward_ffn(dx2, W, S, A, l):
    dx1, h2, act, dg, du_ffn, dgffn = ffn_bwd(dx2, A["x1"], A["gs"], A["us"], S["gffn"], W["wg"], W["wu"], W["wd"])
    d_wd = matmul_tn(act, dx2, f"dw_down_{l}")
    d_wg = matmul_tn(dg, h2, f"dw_gate_{l}")
    d_wu = matmul_tn(du_ffn, h2, f"dw_up_{l}")
    big = {n: d.reshape(N_DEV, -1, D_MODEL) for n, d in (("w_gate", d_wg), ("w_up", d_wu), ("w_down", d_wd))}
    return dx1, big, dgffn


def _backward_mixers(dx1, dgffn, rope, W, S, A, l, send_wout=None):
    (da, delta, du, dvs, dm, mix, dgoa, dgos, dgop, dps, dwp, dwsp, db, dgsgu) = mixers_bwd(
        dx1, A["a"], A["z"], S["gsgu"], S["wcat"], S["wcat_t"], S["bz"], S["wp"], S["pscale"], S["goa"], S["gos"],
        S["gop"], W["wout"])
    d_wout = matmul_tn(mix, dx1, f"dw_out_{l}")
    if send_wout is not None:
        delta = delta + send_wout(d_wout.reshape(N_DEV, -1, D_MODEL))[0, 0]
    dpin = pool_bwd(dm)
    dqt, dk, dv = attn_bwd(A["q"], A["k"], A["kt"], A["v"], da, A["lse"], delta)
    (dx, hn, dz, dgmix, dgql, dgkvl, dgqh, dgkh, d_wq, d_wkv) = mixin_bwd(
        dx1, A["x"], A["z"], rope, dqt, dk, dv, du, dvs, dpin, S["gmix"], W["win"], S["gql"], W["wq"], S["gkvl"],
        W["wkv"], S["gqh"], S["gkh"])
    d_win = matmul_tn(dz, hn, f"dw_in_{l}")
    d_wq, d_wkv = d_wq.astype(BF16), d_wkv.astype(BF16)
    d_win = jnp.concatenate([d_win[:O2], d_win[O2 + QK_NOPE:O2 + QK_HEAD], d_win[O2 + HEAD_PAD:]], axis=0)
    d_wk = d_wkv[:HEADS * HEAD_PAD].reshape(HEADS, HEAD_PAD, KV_LORA)[:, :QK_NOPE]
    d_wv = d_wkv[HEADS * HEAD_PAD:].reshape(HEADS, V_HEAD, KV_LORA)
    d_wkv = jnp.concatenate([d_wk, d_wv], axis=1)
    big = dict(w_in=d_win.reshape(N_DEV, -1, D_MODEL), w_q_up=_unpad_head_rows(d_wq, QK_HEAD).reshape(N_DEV, -1, Q_LORA),
               w_kv_up=d_wkv.reshape(N_DEV, -1, KV_LORA), w_out=d_wout.reshape(N_DEV, -1, D_MODEL))
    tril = jnp.tril(jnp.ones((CHUNK, CHUNK), F32))
    small = dict(g_mix_norm=dgmix[0], g_q_lat=dgql[0], g_kv_lat=dgkvl[0], g_q_head=dgqh[0, :QK_HEAD], g_k_head=dgkh[0, :QK_HEAD],
                 g_sgu_v=dgsgu[0], w_spatial=dwsp.reshape(HEADS, CHUNK, CHUNK) * tril, b_spatial=db[:, :HEADS].T,
                 w_pool=jnp.stack([dwp[64 * g:64 * (g + 1), 64 * g:64 * (g + 1)] for g in range(HEADS)]),
                 pool_scale=dps[0], g_out_mla=dgoa[0], g_out_sgu=dgos[0], g_out_pool=dgop[0], g_ffn_norm=dgffn[0])
    return dx, big, small


def kernel(x, positions, g_mix_norm, w_in, g_q_lat, w_q_up, g_kv_lat, w_kv_up, g_q_head, g_k_head, g_sgu_v, w_spatial, b_spatial, w_pool, pool_scale, g_out_mla, g_out_sgu, g_out_pool, w_out, g_ffn_norm, w_gate, w_up, w_down, loss_target, m_g_mix_norm, m_w_in, m_g_q_lat, m_w_q_up, m_g_kv_lat, m_w_kv_up, m_g_q_head, m_g_k_head, m_g_sgu_v, m_w_spatial, m_b_spatial, m_w_pool, m_pool_scale, m_g_out_mla, m_g_out_sgu, m_g_out_pool, m_w_out, m_g_ffn_norm, m_w_gate, m_w_up, m_w_down, v_g_mix_norm, v_w_in, v_g_q_lat, v_w_q_up, v_g_kv_lat, v_w_kv_up, v_g_q_head, v_g_k_head, v_g_sgu_v, v_w_spatial, v_b_spatial, v_w_pool, v_pool_scale, v_g_out_mla, v_g_out_sgu, v_g_out_pool, v_w_out, v_g_ffn_norm, v_w_gate, v_w_up, v_w_down):
    given = dict(locals())
    w = {n: given[n] for n in WEIGHTS}
    m = {n: given["m_" + n] for n in WEIGHTS}
    v = {n: given["v_" + n] for n in WEIGHTS}
    T = x.shape[1]
    xs = x.reshape(T, D_MODEL)

    half = QK_ROPE // 2
    inv_freq = 1.0 / (ROPE_THETA ** (jnp.arange(half, dtype=F32) / half))
    ang16 = positions.reshape(T).astype(F32)[:, None] * inv_freq
    ang = jnp.concatenate([jnp.zeros((T, QK_NOPE), F32), ang16, ang16, jnp.zeros((T, HEAD_PAD - QK_HEAD), F32)], axis=1)

    wv = {n: _shard_view(n, w[n]) for n in SHARDED}
    wanted = [(SHARDED.index(n), 0) for n in FIRST_0 + LATER_0] + [(i, 1) for i in range(len(SHARDED))]
    bf = cast_shards([wv[n] for n in SHARDED], wanted)
    nf, nl = len(FIRST_0), len(LATER_0)
    ag_first0 = exchange_start(bf[:nf], ang, True, "ag_start_first0")
    rope = rope_tables(ang + ag_first0[-1][0, 0])
    W0 = dict(zip(FIRST_0, exchange_wait(ag_first0, rope[0], "ag_wait_first0")))
    ag_later0 = exchange_start(bf[nf:nf + nl], W0["w_in"], True, "ag_start_later0")
    ag_l1 = exchange_start(bf[nf + nl:], ag_later0[-1], True, "ag_start_l1")
    Ss = [_layer_small(w, l) for l in range(DEPTH)]

    S0 = dict(Ss[0], gmix=Ss[0]["gmix"] + (ag_later0[-1][0, 0] + ag_l1[-1][0, 0]))
    A0 = _forward_attention(xs, rope, _mixer_weights(W0), S0)
    W0.update(zip(LATER_0, exchange_wait(ag_later0, A0["a"], "ag_wait_later0")))
    W0 = {**_mixer_weights(W0), **_ffn_weights(W0)}
    x1 = _forward_mixers(A0, W0["wout"], Ss[0])
    h, A0["gs"], A0["us"] = ffn_fwd(x1, Ss[0]["gffn"], W0["wg"], W0["wu"], W0["wd"])
    layer1 = dict(zip(SHARDED, exchange_wait(ag_l1, h, "ag_wait_l1")))
    W1 = {**_mixer_weights(layer1), **_ffn_weights(layer1)}
    A1 = _forward_attention(h, rope, W1, Ss[1])
    x1 = _forward_mixers(A1, W1["wout"], Ss[1])
    dh, loss_part, A1["gs"], A1["us"] = ffn_fwd(x1, Ss[1]["gffn"], W1["wg"], W1["wu"], W1["wd"],
                                                target=loss_target.reshape(T, D_MODEL))

    dx1, big_f1, dgffn1 = _backward_ffn(dh, W1, Ss[1], A1, 1)
    dh, big_m1, small1 = _backward_mixers(dx1, dgffn1, rope, W1, Ss[1], A1, 1)
    rs_l1 = exchange_start([{**big_m1, **big_f1}[n] for n in SHARDED], dh, False, "rs_start_l1")
    S0 = dict(Ss[0], gffn=Ss[0]["gffn"] + rs_l1[-1][0, 0])
    dx1, big_f0, dgffn0 = _backward_ffn(dh, W0, S0, A0, 0)
    rs_ffn0 = exchange_start([big_f0[n] for n in FFN_SIDE], dx1, False, "rs_start_ffn0")
    S0 = dict(Ss[0], gsgu=Ss[0]["gsgu"] + rs_ffn0[-1][0, 0])
    rs_out0 = []

    def send_wout(pieces):
        rs_out0.append(exchange_start([pieces], dx1, False, "rs_start_out0"))
        return rs_out0[0][-1]

    dh, big_m0, small0 = _backward_mixers(dx1, dgffn0, rope, W0, S0, A0, 0, send_wout)
    grad_x = dh.reshape(x.shape)
    smalls = [small0, small1]

    rs_first0 = exchange_start([big_m0[n] for n in FIRST_0], dh, False, "rs_start_first0")
    parts1 = dict(zip(SHARDED, exchange_wait(rs_l1, rs_first0[-1], "rs_wait_l1")))
    parts0 = dict(zip(FFN_SIDE, exchange_wait(rs_ffn0, rs_first0[-1], "rs_wait_ffn0")))
    parts0["w_out"] = exchange_wait(rs_out0[0], rs_first0[-1], "rs_wait_out0")[0]
    grad, delta, new_m, new_v = {}, {}, {}, {}

    small_part = {n: jnp.stack([smalls[l][n] for l in range(DEPTH)]) for n in SMALL}
    small_part["loss"] = loss_part
    no_loss = {"loss": jnp.zeros((1, 1), F32)}
    outs = allreduce_adamw_small(_pack_small(small_part), *[_pack_small({**a, **no_loss}) for a in (w, m, v)])
    for d, o in zip((grad, delta, new_m, new_v), outs):
        d.update(_unpack_small(o, {**w, **no_loss}))
    loss = grad["loss"][0, 0]

    for n in LATER_0 + FIRST_0:
        if n == FIRST_0[0]:
            parts0.update(zip(FIRST_0, exchange_wait(rs_first0, grad["w_down"], "rs_wait_first0")))
        parts, views = [parts0[n], parts1[n]], [_shard_view(n, a[n]) for a in (w, m, v)]
        if n == "w_in":
            parts = [jnp.stack(parts, axis=2).reshape(N_DEV, -1, D_MODEL)]
            views = [a.reshape(1, -1, D_MODEL) for a in views]
        outs = adamw_sharded(parts, *views, f"adamw_{n}")
        if n == "w_in":
            outs = [o.reshape(-1, DEPTH, D_MODEL) for o in outs]
        grad[n], delta[n], new_m[n], new_v[n] = [_shard_unview(n, o) for o in outs]

    return (loss, grad_x, *[grad[n] for n in WEIGHTS], *[delta[n] for n in WEIGHTS], *[new_m[n] for n in WEIGHTS],
            *[new_v[n] for n in WEIGHTS])
```

```python
import math

import jax
import jax.numpy as jnp
from jax import lax
from jax.experimental import pallas as pl
from jax.experimental.pallas import tpu as pltpu

F32 = jnp.float32
BF16 = jnp.bfloat16

N_DEV = 8
DEPTH = 2
D_MODEL = 1024
HEADS = 4
HEAD_PAD = 128
QK_NOPE = 64
QK_ROPE = 32
QK_HEAD = QK_NOPE + QK_ROPE
V_HEAD = 128
Q_LORA = 256
KV_LORA = 128
SGU_WIDTH = 256
SGU_HEAD_DIM = 64
CHUNK = 128
POOL_WIDTH = 256
POOL_HALO = 16
MLA_WIDTH = 512
IN_WIDTH = 1184
Z_WIDTH = 1280
FFN_HIDDEN = 2816
FFN_CHUNK = 256
ROPE_THETA = 10000.0
EPS = 1e-6
ATTN_SCALE = 1.0 / math.sqrt(QK_HEAD)
LOG2E = 1.4426950408889634
NEG_BIG = -1e30

ADAM_LR = 0.001
ADAM_B1 = 0.9
ADAM_B2 = 0.999
ADAM_EPS = 1e-08
ADAM_WD = 0.01
ADAM_STEP = 10

VMEM_LIMIT = 56 * 1024 * 1024
ROW_TILE = 512
MIXIN_PART = 256
MIXIN_BWD_PART = 128
ATTN_TILE = 1024
ATTN_SUB = 512
ATTN_FWD_SUB = 512
ATTN_BWD_HEADS = 2
MESH = pl.DeviceIdType.MESH

WHOLE = pl.BlockSpec(memory_space=pltpu.VMEM)
ANY = pl.BlockSpec(memory_space=pl.ANY)
HBM_SPEC = pl.BlockSpec(memory_space=pltpu.HBM)
SEM_SPEC = pl.BlockSpec(memory_space=pltpu.SEMAPHORE)


def _cparams(**kw):
    return pltpu.CompilerParams(vmem_limit_bytes=VMEM_LIMIT, **kw)


def _dot(a, b):
    return jnp.dot(a, b, preferred_element_type=F32)


def _dot_nt(a, b):
    return lax.dot_general(a, b, (((1,), (1,)), ((), ())), preferred_element_type=F32)


def _dot_tn(a, b):
    return lax.dot_general(a, b, (((0,), (0,)), ((), ())), preferred_element_type=F32)


def _rms(x, g, n):
    r = lax.rsqrt(jnp.sum(x * x, axis=-1, keepdims=True) * (1.0 / n) + EPS)
    return x * r * g, r


def _rms_bwd(x, r, g, dy, n):
    gdy = dy * g
    dx = r * gdy - x * (r * r * r) * (jnp.sum(x * gdy, axis=-1, keepdims=True) * (1.0 / n))
    dg = jnp.sum(dy * (x * r), axis=0, keepdims=True)
    return dx, dg


def _sigmoid(x):
    return 1.0 / (1.0 + jnp.exp(-x))


def rope_tables(ang):
    T = ang.shape[0]
    tm = min(ROW_TILE, T)

    def body(ang_ref, c_ref, sa_ref, sb_ref):
        a = ang_ref[...]
        lane = lax.broadcasted_iota(jnp.int32, a.shape, 1)
        s = jnp.sin(a)
        c_ref[...] = jnp.cos(a)
        sa_ref[...] = jnp.where(lane < QK_NOPE + QK_ROPE // 2, -s, 0.0)
        sb_ref[...] = jnp.where(lane >= QK_NOPE + QK_ROPE // 2, s, 0.0)

    row = pl.BlockSpec((tm, HEAD_PAD), lambda i: (i, 0))
    return pl.pallas_call(
        body, grid=(T // tm,), name="rope_tables", in_specs=[row], out_specs=[row] * 3,
        out_shape=[jax.ShapeDtypeStruct((T, HEAD_PAD), F32)] * 3, compiler_params=_cparams(),
    )(ang)


def _rope(x, c, sa, sb):
    half = QK_ROPE // 2
    return x * c + pltpu.roll(x, HEAD_PAD - half, 1) * sa + pltpu.roll(x, half, 1) * sb


def _rope_bwd(dy, c, sa, sb):
    half = QK_ROPE // 2
    return dy * c + pltpu.roll(dy * sa, half, 1) + pltpu.roll(dy * sb, HEAD_PAD - half, 1)


def _skewed(stages, groups):
    for t in range(len(stages) + len(groups) - 1):
        for p, g in enumerate(groups):
            if 0 <= t - p < len(stages):
                stages[t - p](g)


def _head_masks(shape, width):
    lane = lax.broadcasted_iota(jnp.int32, shape, len(shape) - 1)
    return [(lane >= width * h) & (lane < width * (h + 1)) for h in range(HEADS)]


def _qkv_pre(z, gql, wq, gkvl, wkv):
    ql = z[:, 0:Q_LORA]
    kvl = z[:, Q_LORA:Q_LORA + KV_LORA]
    kr = z[:, Q_LORA + KV_LORA:Q_LORA + KV_LORA + HEAD_PAD]
    qn, rq = _rms(ql, gql, Q_LORA)
    kvn, rkv = _rms(kvl, gkvl, KV_LORA)
    qn = qn.astype(BF16)
    kvn = kvn.astype(BF16)
    q_up = _dot_nt(qn, wq)
    kv_up = _dot_nt(kvn, wkv)
    return ql, kvl, kr, qn, rq, kvn, rkv, q_up, kv_up


def mixin_fwd(x, rope, gmix, win, gql, wq, gkvl, wkv, gqh, gkh):
    T = x.shape[0]
    tm = min(2 * ROW_TILE, T)
    part = min(MIXIN_PART, tm)

    def body(x_ref, c_ref, sa_ref, sb_ref, gmix_ref, win_ref, gql_ref, wq_ref, gkvl_ref, wkv_ref, gqh_ref, gkh_ref,
             z_ref, q_ref, k_ref, v_ref, kt_ref, vt_ref):
        def project(g):
            hn = _rms(x_ref[g["rows"], :], gmix_ref[...], D_MODEL)[0].astype(BF16)
            g["z"] = _dot_nt(hn, win_ref[...])
            z_ref[g["rows"], :] = g["z"]

        def latents(g):
            _, _, g["kr"], _, _, _, _, g["q_up"], g["kv_up"] = _qkv_pre(g["z"], gql_ref[...], wq_ref[...], gkvl_ref[...],
                                                                        wkv_ref[...])

        def heads(g):
            rows = g["rows"]
            c, sa, sb = c_ref[rows, :], sa_ref[rows, :], sb_ref[rows, :]
            for h in range(HEADS):
                lo = HEAD_PAD * h
                qh = _rms(g["q_up"][:, lo:lo + HEAD_PAD], gqh_ref[...], QK_HEAD)[0]
                q_ref[h, rows, :] = (_rope(qh, c, sa, sb) * (ATTN_SCALE * LOG2E)).astype(BF16)
                kh = _rope(_rms(g["kv_up"][:, lo:lo + HEAD_PAD] + g["kr"], gkh_ref[...], QK_HEAD)[0], c, sa, sb)
                k_ref[h, rows, :] = kh.astype(BF16)
                kt_ref[h, :, rows] = jnp.transpose(kh).astype(BF16)
                vh = g["kv_up"][:, HEADS * HEAD_PAD + lo:HEADS * HEAD_PAD + lo + HEAD_PAD]
                v_ref[h, rows, :] = vh.astype(BF16)
                vt_ref[h, :, rows] = jnp.transpose(vh).astype(BF16)

        _skewed((project, latents, heads), [dict(rows=pl.ds(part * p, part)) for p in range(tm // part)])

    row = lambda w: pl.BlockSpec((tm, w), lambda i: (i, 0))
    head = pl.BlockSpec((HEADS, tm, HEAD_PAD), lambda i: (0, i, 0))
    head_t = pl.BlockSpec((HEADS, HEAD_PAD, tm), lambda i: (0, 0, i))
    return pl.pallas_call(
        body, grid=(T // tm,), name="mixin_fwd",
        in_specs=[row(D_MODEL)] + [row(HEAD_PAD)] * 3 + [WHOLE] * 8,
        out_specs=[row(Z_WIDTH), head, head, head, head_t, head_t],
        out_shape=[jax.ShapeDtypeStruct((T, Z_WIDTH), F32)] + [jax.ShapeDtypeStruct((HEADS, T, HEAD_PAD), BF16)] * 3
                  + [jax.ShapeDtypeStruct((HEADS, HEAD_PAD, T), BF16)] * 2,
        compiler_params=_cparams(),
    )(x, *rope, gmix, win, gql, wq, gkvl, wkv, gqh, gkh)


def _pair_tables(pairs):
    return (jnp.asarray([p[0] for p in pairs], jnp.int32), jnp.asarray([p[1] for p in pairs], jnp.int32))


def attn_fwd(q, k, vt):
    _, T, _ = q.shape
    tb = min(ATTN_TILE, T)
    sq = min(ATTN_FWD_SUB, tb)
    nb = T // tb

    pairs = [(i, j) for i in range(nb) for j in range(i + 1)]

    def body(i_tab, j_tab, q_ref, k_ref, vt_ref, o_ref, lse_ref, m_s, l_s, acc_s):
        i, j = i_tab[pl.program_id(0)], j_tab[pl.program_id(0)]

        @pl.when(j == 0)
        def _():
            m_s[...] = jnp.full(m_s.shape, -jnp.inf, F32)
            l_s[...] = jnp.zeros(l_s.shape, F32)
            acc_s[...] = jnp.zeros(acc_s.shape, F32)

        def scores(g):
            st = _dot_nt(k_ref[g["h"], g["kb"], :], q_ref[g["h"], g["qa"], :])
            if g["masked"]:
                krow = lax.broadcasted_iota(jnp.int32, st.shape, 0)
                qcol = g["q0"] + lax.broadcasted_iota(jnp.int32, st.shape, 1)
                st = jnp.where(krow <= qcol, st, NEG_BIG)
            g["st"] = st

        def new_max(g):
            h, qa = g["h"], g["qa"]
            m_prev = m_s[h, :, qa]
            g["m"] = jnp.maximum(m_prev, jnp.max(g["st"], axis=0, keepdims=True))
            g["alpha"] = jnp.exp2(m_prev - g["m"])
            m_s[h, :, qa] = g["m"]

        def weights(g):
            pt = jnp.exp2(g["st"] - g["m"])
            g["lsum"] = jnp.sum(pt, axis=0, keepdims=True)
            g["pt"] = pt.astype(BF16)

        def accumulate(g):
            h, qa = g["h"], g["qa"]
            l_s[h, :, qa] = g["alpha"] * l_s[h, :, qa] + g["lsum"]
            acc_s[h, :, qa] = g["alpha"] * acc_s[h, :, qa] + _dot(vt_ref[h, :, g["kb"]], g["pt"])

        def tiles(masked):
            return [dict(h=h, q0=q0, qa=slice(q0, q0 + sq), kb=slice(0, q0 + sq if masked else tb), masked=masked)
                    for h in range(HEADS) for q0 in range(0, tb, sq)]

        @pl.when(j < i)
        def _():
            _skewed((scores, new_max, weights, accumulate), tiles(False))

        @pl.when(j == i)
        def _():
            _skewed((scores, new_max, weights, accumulate), tiles(True))
            for h in range(HEADS):
                l = l_s[h]
                o_ref[:, HEAD_PAD * h:HEAD_PAD * (h + 1)] = jnp.transpose(acc_s[h] / l)
                lse_ref[h] = m_s[h] + jnp.log2(l)

    qspec = pl.BlockSpec((HEADS, tb, HEAD_PAD), lambda p, it, jt: (0, it[p], 0))
    kspec = pl.BlockSpec((HEADS, tb, HEAD_PAD), lambda p, it, jt: (0, jt[p], 0))
    vspec = pl.BlockSpec((HEADS, HEAD_PAD, tb), lambda p, it, jt: (0, 0, jt[p]))
    grid_spec = pltpu.PrefetchScalarGridSpec(
        num_scalar_prefetch=2, grid=(len(pairs),),
        in_specs=[qspec, kspec, vspec],
        out_specs=[pl.BlockSpec((tb, MLA_WIDTH), lambda p, it, jt: (it[p], 0)),
                   pl.BlockSpec((HEADS, 1, tb), lambda p, it, jt: (0, 0, it[p]))],
        scratch_shapes=[pltpu.VMEM((HEADS, 1, tb), F32), pltpu.VMEM((HEADS, 1, tb), F32), pltpu.VMEM((HEADS, HEAD_PAD, tb), F32)])
    return pl.pallas_call(
        body, grid_spec=grid_spec, name="attn_fwd",
        out_shape=[jax.ShapeDtypeStruct((T, MLA_WIDTH), F32), jax.ShapeDtypeStruct((HEADS, 1, T), F32)],
        compiler_params=_cparams(),
    )(*_pair_tables(pairs), q, k, vt)


def _sgu_fwd_chunk(vn_c, wcat, bz, masks):
    vstack = jnp.concatenate([jnp.where(mk, vn_c, 0.0).astype(BF16) for mk in masks], axis=0)
    return _dot(wcat, vstack) + bz


def _pool_counts(i, tm):
    pos1 = (i * tm + 1 + lax.broadcasted_iota(jnp.int32, (tm, POOL_WIDTH), 0)).astype(F32)
    lane = lax.broadcasted_iota(jnp.int32, (tm, POOL_WIDTH), 1)
    win = jnp.where(lane < 64, 2.0, jnp.where(lane < 128, 4.0, jnp.where(lane < 192, 8.0, 16.0)))
    return jnp.minimum(pos1, win), lane


def _by_group(lane, s2, s4, s8, s16):
    return jnp.where(lane < 64, s2, jnp.where(lane < 128, s4, jnp.where(lane < 192, s8, s16)))


def _pool_means(pin, halo, i, tm):
    s1 = jnp.concatenate([halo, pin], axis=0)
    s2 = s1 + pltpu.roll(s1, 1, 0)
    s4 = s2 + pltpu.roll(s2, 2, 0)
    s8 = s4 + pltpu.roll(s4, 4, 0)
    s16 = s8 + pltpu.roll(s8, 8, 0)
    cnt, lane = _pool_counts(i, tm)
    sel = _by_group(lane, s2[POOL_HALO:], s4[POOL_HALO:], s8[POOL_HALO:], s16[POOL_HALO:])
    return sel / cnt - pin


def _mixers_fwd_tile(i, tm, a, u, vs, pin, halo, gsgu, wcat, bz, wp, pscale, goa, gos, gop):
    vn, rv = _rms(vs, gsgu, SGU_WIDTH)
    masks = _head_masks((CHUNK, SGU_WIDTH), SGU_HEAD_DIM)
    zc = jnp.concatenate([_sgu_fwd_chunk(vn[CHUNK * c:CHUNK * (c + 1)], wcat, bz, masks) for c in range(tm // CHUNK)], axis=0)
    gm = u * zc
    halo = jnp.where(i > 0, halo, 0.0)
    m = _pool_means(pin, halo, i, tm).astype(BF16)
    yp_pre = _dot(m, wp)
    yp = yp_pre * pscale
    na, ra = _rms(a, goa, MLA_WIDTH)
    ng, rg = _rms(gm, gos, SGU_WIDTH)
    npo, rp = _rms(yp, gop, POOL_WIDTH)
    mix = jnp.concatenate([na, ng, npo], axis=1).astype(BF16)
    return vn, rv, zc, gm, m, yp_pre, yp, ra, rg, rp, mix


def _z_specs(tm):
    col = lambda c: pl.BlockSpec((tm, 256), lambda i: (i, c))
    halo = pl.BlockSpec((POOL_HALO, 256), lambda i: (jnp.maximum(i * (tm // POOL_HALO) - 1, 0), 4))
    return [col(2), col(3), col(4), halo]


def mixers_fwd(x, a, z, gsgu, wcat, bz, wp, pscale, goa, gos, gop, wout):
    T = x.shape[0]
    tm = min(ROW_TILE, T)

    def body(x_ref, a_ref, u_ref, vs_ref, pin_ref, halo_ref, gsgu_ref, wcat_ref, bz_ref, wp_ref, ps_ref,
             goa_ref, gos_ref, gop_ref, wout_ref, x1_ref):
        i = pl.program_id(0)
        mix = _mixers_fwd_tile(i, tm, a_ref[...], u_ref[...], vs_ref[...], pin_ref[...], halo_ref[...], gsgu_ref[...],
                               wcat_ref[...], bz_ref[...], wp_ref[...], ps_ref[...], goa_ref[...], gos_ref[...],
                               gop_ref[...])[-1]
        x1_ref[...] = x_ref[...] + _dot(mix, wout_ref[...])

    row = lambda w: pl.BlockSpec((tm, w), lambda i: (i, 0))
    return pl.pallas_call(
        body, grid=(T // tm,), name="mixers_fwd",
        in_specs=[row(D_MODEL), row(MLA_WIDTH)] + _z_specs(tm) + [WHOLE] * 9,
        out_specs=row(D_MODEL),
        out_shape=jax.ShapeDtypeStruct((T, D_MODEL), F32),
        compiler_params=_cparams(),
    )(x, a, z, z, z, z, gsgu, wcat, bz, wp, pscale, goa, gos, gop, wout)


def ffn_fwd(x1, gffn, wg, wu, wd, target=None):
    T = x1.shape[0]
    tm = min(ROW_TILE, T)
    with_loss = target is not None

    def body(*refs):
        x1_ref, gffn_ref, wg_ref, wu_ref, wd_ref = refs[:5]
        outs = refs[6:] if with_loss else refs[5:]
        x1v = x1_ref[...]
        h2 = _rms(x1v, gffn_ref[...], D_MODEL)[0].astype(BF16)
        acc = x1v
        for c in range(FFN_HIDDEN // FFN_CHUNK):
            sl = slice(FFN_CHUNK * c, FFN_CHUNK * (c + 1))
            g = _dot_nt(h2, wg_ref[sl, :])
            u = _dot_nt(h2, wu_ref[sl, :])
            outs[-2][:, sl] = g.astype(BF16)
            outs[-1][:, sl] = u.astype(BF16)
            act = (g * _sigmoid(g) * u).astype(BF16)
            acc = acc + _dot(act, wd_ref[sl, :])
        if not with_loss:
            outs[0][...] = acc
            return
        dy_ref, loss_ref = outs[0], outs[1]

        @pl.when(pl.program_id(0) == 0)
        def _():
            loss_ref[...] = jnp.zeros(loss_ref.shape, F32)

        err = acc - refs[5][...]
        dy_ref[...] = err * (1.0 / D_MODEL)
        per_row = jnp.sum(err * err, axis=1, keepdims=True) * (1.0 / D_MODEL)
        loss_ref[...] += 0.5 * jnp.sum(per_row, axis=0, keepdims=True)

    row = lambda w: pl.BlockSpec((tm, w), lambda i: (i, 0))
    hidden = [jax.ShapeDtypeStruct((T, FFN_HIDDEN), BF16)] * 2
    if with_loss:
        return pl.pallas_call(
            body, grid=(T // tm,), name="ffn_fwd_loss",
            in_specs=[row(D_MODEL)] + [WHOLE] * 4 + [row(D_MODEL)],
            out_specs=[row(D_MODEL), pl.BlockSpec((1, 1), lambda i: (0, 0)), row(FFN_HIDDEN), row(FFN_HIDDEN)],
            out_shape=[jax.ShapeDtypeStruct((T, D_MODEL), F32), jax.ShapeDtypeStruct((1, 1), F32)] + hidden,
            compiler_params=_cparams(),
        )(x1, gffn, wg, wu, wd, target)
    return pl.pallas_call(
        body, grid=(T // tm,), name="ffn_fwd",
        in_specs=[row(D_MODEL)] + [WHOLE] * 4,
        out_specs=[row(D_MODEL), row(FFN_HIDDEN), row(FFN_HIDDEN)],
        out_shape=[jax.ShapeDtypeStruct((T, D_MODEL), F32)] + hidden,
        compiler_params=_cparams(),
    )(x1, gffn, wg, wu, wd)


def _acc_spec(shape):
    return pl.BlockSpec(shape, lambda i: (0,) * len(shape))


def ffn_bwd(dx2, x1, gs, us, gffn, wg, wu, wd):
    T = x1.shape[0]
    tm = min(ROW_TILE // 2, T)

    def body(dx2_ref, x1_ref, gs_ref, us_ref, gffn_ref, wg_ref, wu_ref, wd_ref,
             dx1_ref, h2_ref, act_ref, dg_ref, du_ref, dgffn_ref):
        @pl.when(pl.program_id(0) == 0)
        def _():
            dgffn_ref[...] = jnp.zeros(dgffn_ref.shape, F32)

        dx2v = dx2_ref[...]
        dy = dx2v.astype(BF16)
        x1v = x1_ref[...]
        h2, r = _rms(x1v, gffn_ref[...], D_MODEL)
        h2_ref[...] = h2.astype(BF16)
        for c in range(FFN_HIDDEN // FFN_CHUNK):
            sl = slice(FFN_CHUNK * c, FFN_CHUNK * (c + 1))
            g = gs_ref[:, sl].astype(F32)
            u = us_ref[:, sl].astype(F32)
            dact = _dot_nt(dy, wd_ref[sl, :])
            sg = _sigmoid(g)
            silu = g * sg
            act_ref[:, sl] = (silu * u).astype(BF16)
            dg_ref[:, sl] = (dact * u * (sg * (1.0 + g * (1.0 - sg)))).astype(BF16)
            du_ref[:, sl] = (dact * silu).astype(BF16)
        dh2 = _dot(dg_ref[...], wg_ref[...]) + _dot(du_ref[...], wu_ref[...])
        dxn, dgn = _rms_bwd(x1v, r, gffn_ref[...], dh2, D_MODEL)
        dx1_ref[...] = dx2v + dxn
        dgffn_ref[...] += dgn

    row = lambda w: pl.BlockSpec((tm, w), lambda i: (i, 0))
    return pl.pallas_call(
        body, grid=(T // tm,), name="ffn_bwd",
        in_specs=[row(D_MODEL), row(D_MODEL), row(FFN_HIDDEN), row(FFN_HIDDEN)] + [WHOLE] * 4,
        out_specs=[row(D_MODEL), row(D_MODEL), row(FFN_HIDDEN), row(FFN_HIDDEN), row(FFN_HIDDEN), _acc_spec((1, D_MODEL))],
        out_shape=[jax.ShapeDtypeStruct((T, D_MODEL), F32), jax.ShapeDtypeStruct((T, D_MODEL), BF16),
                   jax.ShapeDtypeStruct((T, FFN_HIDDEN), BF16), jax.ShapeDtypeStruct((T, FFN_HIDDEN), BF16),
                   jax.ShapeDtypeStruct((T, FFN_HIDDEN), BF16), jax.ShapeDtypeStruct((1, D_MODEL), F32)],
        compiler_params=_cparams(),
    )(dx2, x1, gs, us, gffn, wg, wu, wd)


TN_K_TILE = 2048
TN_ACC_BYTES = 6 * 1024 * 1024


def matmul_tn(a, b, name):
    T, M = a.shape
    N = b.shape[1]
    tk = min(TN_K_TILE, T)
    tm = M if M <= 1024 else M // 2
    tn = max(d for d in range(128, N + 1, 128) if N % d == 0 and tm * d * 4 <= TN_ACC_BYTES)
    nk = T // tk

    def body(a_ref, b_ref, o_ref, acc):
        k = pl.program_id(2)

        @pl.when(k == 0)
        def _():
            acc[...] = jnp.zeros(acc.shape, F32)

        acc[...] += _dot_tn(a_ref[...].astype(BF16), b_ref[...].astype(BF16))

        @pl.when(k == nk - 1)
        def _():
            o_ref[...] = acc[...].astype(BF16)

    return pl.pallas_call(
        body, grid=(M // tm, N // tn, nk), name=name,
        in_specs=[pl.BlockSpec((tk, tm), lambda i, j, k: (k, i)), pl.BlockSpec((tk, tn), lambda i, j, k: (k, j))],
        out_specs=pl.BlockSpec((tm, tn), lambda i, j, k: (i, j)),
        out_shape=jax.ShapeDtypeStruct((M, N), BF16),
        scratch_shapes=[pltpu.VMEM((tm, tn), F32)],
        compiler_params=_cparams(),
    )(a, b)


def mixers_bwd(dx1, a, z, gsgu, wcat, wcat_t, bz, wp, pscale, goa, gos, gop, wout):
    T = a.shape[0]
    tm = min(ROW_TILE, T)

    def body(dx1_ref, a_ref, u_ref, vs_ref, pin_ref, halo_ref, gsgu_ref, wcat_ref, wcatt_ref, bz_ref, wp_ref,
             ps_ref, goa_ref, gos_ref, gop_ref, wout_ref,
             da_ref, delta_ref, du_ref, dvs_ref, dm_ref, mix_ref,
             dgoa_ref, dgos_ref, dgop_ref, dps_ref, dwp_ref, dwsp_ref, db_ref, dgsgu_ref):
        i = pl.program_id(0)

        @pl.when(i == 0)
        def _():
            for r in (dgoa_ref, dgos_ref, dgop_ref, dps_ref, dwp_ref, dwsp_ref, db_ref, dgsgu_ref):
                r[...] = jnp.zeros(r.shape, F32)

        a_v, u, vs = a_ref[...], u_ref[...], vs_ref[...]
        goa, gos, gop, pscale_v = goa_ref[...], gos_ref[...], gop_ref[...], ps_ref[...]
        vn, rv, zc, gm, m, yp_pre, yp, ra, rg, rp, mix = _mixers_fwd_tile(
            i, tm, a_v, u, vs, pin_ref[...], halo_ref[...], gsgu_ref[...], wcat_ref[...], bz_ref[...], wp_ref[...],
            pscale_v, goa, gos, gop)
        mix_ref[...] = mix
        dmix = _dot_nt(dx1_ref[...].astype(BF16), wout_ref[...])
        da, dgoa = _rms_bwd(a_v, ra, goa, dmix[:, :MLA_WIDTH], MLA_WIDTH)
        dgm, dgos = _rms_bwd(gm, rg, gos, dmix[:, MLA_WIDTH:MLA_WIDTH + SGU_WIDTH], SGU_WIDTH)
        dyp, dgop = _rms_bwd(yp, rp, gop, dmix[:, MLA_WIDTH + SGU_WIDTH:], POOL_WIDTH)
        da_ref[...] = da
        dgoa_ref[...] += dgoa
        dgos_ref[...] += dgos
        dgop_ref[...] += dgop
        prod = da * a_v
        ones = jnp.ones((8, HEAD_PAD), F32)
        for h in range(HEADS):
            lo = HEAD_PAD * h
            sums = lax.dot_general(ones, prod[:, lo:lo + HEAD_PAD], (((1,), (1,)), ((), ())), preferred_element_type=F32,
                                   precision=lax.Precision.HIGHEST)
            delta_ref[h] = sums[0:1, :]
        dps_ref[...] += jnp.sum(dyp * yp_pre, axis=0, keepdims=True)
        dyp_pre = (dyp * pscale_v).astype(BF16)
        dwp_ref[...] += _dot_tn(m, dyp_pre)
        dm_ref[...] = _dot_nt(dyp_pre, wp_ref[...])
        du_ref[...] = (dgm * zc).astype(BF16)
        dzc = dgm * u
        masks = _head_masks((CHUNK, SGU_WIDTH), SGU_HEAD_DIM)
        lane_b = lax.broadcasted_iota(jnp.int32, (CHUNK, HEAD_PAD), 1)
        dvn_parts = []
        dwsp = jnp.zeros(dwsp_ref.shape, F32)
        db = jnp.zeros(db_ref.shape, F32)
        for c in range(tm // CHUNK):
            dz_c = dzc[CHUNK * c:CHUNK * (c + 1)]
            dzstack = jnp.concatenate([jnp.where(mk, dz_c, 0.0).astype(BF16) for mk in masks], axis=0)
            dvn_parts.append(_dot(wcatt_ref[...], dzstack))
            dwsp = dwsp + _dot_nt(dzstack, vn[CHUNK * c:CHUNK * (c + 1)].astype(BF16))
            for h, mk in enumerate(masks):
                col = jnp.sum(jnp.where(mk, dz_c, 0.0), axis=1, keepdims=True)
                db = db + jnp.where(lane_b == h, col, 0.0)
        dwsp_ref[...] += dwsp
        db_ref[...] += db
        dvs, dgsgu = _rms_bwd(vs, rv, gsgu_ref[...], jnp.concatenate(dvn_parts, axis=0), SGU_WIDTH)
        dvs_ref[...] = dvs.astype(BF16)
        dgsgu_ref[...] += dgsgu

    row = lambda w: pl.BlockSpec((tm, w), lambda i: (i, 0))
    head = pl.BlockSpec((HEADS, 1, tm), lambda i: (0, 0, i))
    acc_shapes = [(1, MLA_WIDTH), (1, SGU_WIDTH), (1, POOL_WIDTH), (1, POOL_WIDTH), (POOL_WIDTH, POOL_WIDTH),
                  (HEADS * CHUNK, CHUNK), (CHUNK, HEAD_PAD), (1, SGU_WIDTH)]
    return pl.pallas_call(
        body, grid=(T // tm,), name="mixers_bwd",
        in_specs=[row(D_MODEL), row(MLA_WIDTH)] + _z_specs(tm) + [WHOLE] * 10,
        out_specs=[row(MLA_WIDTH), head, row(256), row(256), row(256), row(D_MODEL)] + [_acc_spec(s) for s in acc_shapes],
        out_shape=[jax.ShapeDtypeStruct((T, MLA_WIDTH), F32), jax.ShapeDtypeStruct((HEADS, 1, T), F32),
                   jax.ShapeDtypeStruct((T, 256), BF16), jax.ShapeDtypeStruct((T, 256), BF16),
                   jax.ShapeDtypeStruct((T, 256), F32), jax.ShapeDtypeStruct((T, D_MODEL), BF16)]
                  + [jax.ShapeDtypeStruct(s, F32) for s in acc_shapes],
        compiler_params=_cparams(),
    )(dx1, a, z, z, z, z, gsgu, wcat, wcat_t, bz, wp, pscale, goa, gos, gop, wout)


def pool_bwd(dm):
    T = dm.shape[0]
    tm = min(4 * ROW_TILE, T)
    nt = T // tm

    def body(dm_ref, next_ref, dpin_ref):
        i = pl.program_id(0)
        cnt, lane = _pool_counts(i, tm)
        dmv = dm_ref[...]
        win = _by_group(lane[:POOL_HALO], 2.0, 4.0, 8.0, 16.0)
        nxt = jnp.where(i < nt - 1, next_ref[...] / win, 0.0)
        r1 = jnp.concatenate([dmv / cnt, nxt], axis=0)
        n = tm + POOL_HALO
        r2 = r1 + pltpu.roll(r1, n - 1, 0)
        r4 = r2 + pltpu.roll(r2, n - 2, 0)
        r8 = r4 + pltpu.roll(r4, n - 4, 0)
        r16 = r8 + pltpu.roll(r8, n - 8, 0)
        dpin_ref[...] = (_by_group(lane, r2[:tm], r4[:tm], r8[:tm], r16[:tm]) - dmv).astype(BF16)

    return pl.pallas_call(
        body, grid=(nt,), name="pool_bwd",
        in_specs=[pl.BlockSpec((tm, 256), lambda i: (i, 0)),
                  pl.BlockSpec((POOL_HALO, 256), lambda i: (jnp.minimum((i + 1) * (tm // POOL_HALO), T // POOL_HALO - 1), 0))],
        out_specs=pl.BlockSpec((tm, 256), lambda i: (i, 0)),
        out_shape=jax.ShapeDtypeStruct((T, 256), BF16),
        compiler_params=_cparams(),
    )(dm, dm)


def attn_bwd(q, k, kt, v, do, lse, delta):
    _, T, _ = q.shape
    tb = min(ATTN_TILE, T)
    sb = min(ATTN_SUB, tb)
    ns = tb // sb
    nb = T // tb
    hb = ATTN_BWD_HEADS

    pairs = [(i, j) for i in range(nb) for j in range(i, nb)]

    def body(i_tab, j_tab, q_ref, k_ref, kt_ref, v_ref, do_ref, lse_ref, delta_ref, dqt_ref, dk_ref, dv_ref, dk_s, dv_s):
        i, j = i_tab[pl.program_id(1)], j_tab[pl.program_id(1)]

        @pl.when(pl.program_id(1) == 0)
        def _():
            dqt_ref[...] = jnp.zeros(dqt_ref.shape, F32)

        @pl.when(j == i)
        def _():
            dk_s[...] = jnp.zeros(dk_s.shape, F32)
            dv_s[...] = jnp.zeros(dv_s.shape, F32)

        def sub_block(h, a, b, masked):
            qa = slice(sb * a, sb * (a + 1))
            kb = slice(sb * b, sb * (b + 1))
            qv = q_ref[h, qa, :]
            dov = do_ref[qa, HEAD_PAD * h:HEAD_PAD * (h + 1)].astype(BF16)
            st = _dot_nt(k_ref[h, kb, :], qv)
            dpt = _dot_nt(v_ref[h, kb, :], dov)
            pt = jnp.exp2(st - lse_ref[h, :, qa])
            if masked:
                krow = lax.broadcasted_iota(jnp.int32, st.shape, 0)
                qcol = lax.broadcasted_iota(jnp.int32, st.shape, 1)
                pt = jnp.where(krow <= qcol, pt, 0.0)
            dst = (pt * (dpt - delta_ref[h, :, qa])).astype(BF16)
            dv_s[h, kb, :] += _dot(pt.astype(BF16), dov)
            dk_s[h, kb, :] += _dot(dst, qv)
            cols = pl.ds(pl.multiple_of(j * tb + sb * a, sb), sb)
            dqt_ref[h, :, cols] += _dot(kt_ref[h, :, kb], dst)

        @pl.when(j > i)
        def _():
            for a in range(ns):
                for h in range(hb):
                    for b in range(ns):
                        sub_block(h, a, b, False)

        @pl.when(j == i)
        def _():
            for a in range(ns):
                for h in range(hb):
                    for b in range(a + 1):
                        sub_block(h, a, b, a == b)

        @pl.when(j == nb - 1)
        def _():
            dk_ref[...] = dk_s[...] * (1.0 / LOG2E)
            dv_ref[...] = dv_s[...]

    qspec = pl.BlockSpec((hb, tb, HEAD_PAD), lambda g, p, it, jt: (g, jt[p], 0))
    kspec = pl.BlockSpec((hb, tb, HEAD_PAD), lambda g, p, it, jt: (g, it[p], 0))
    ktspec = pl.BlockSpec((hb, HEAD_PAD, tb), lambda g, p, it, jt: (g, 0, it[p]))
    rowspec = pl.BlockSpec((hb, 1, tb), lambda g, p, it, jt: (g, 0, jt[p]))
    grid_spec = pltpu.PrefetchScalarGridSpec(
        num_scalar_prefetch=2, grid=(HEADS // hb, len(pairs)),
        in_specs=[qspec, kspec, ktspec, kspec, pl.BlockSpec((tb, hb * HEAD_PAD), lambda g, p, it, jt: (jt[p], g)), rowspec, rowspec],
        out_specs=[pl.BlockSpec((hb, HEAD_PAD, T), lambda g, p, it, jt: (g, 0, 0)), kspec, kspec],
        scratch_shapes=[pltpu.VMEM((hb, tb, HEAD_PAD), F32), pltpu.VMEM((hb, tb, HEAD_PAD), F32)])
    return pl.pallas_call(
        body, grid_spec=grid_spec, name="attn_bwd",
        out_shape=[jax.ShapeDtypeStruct((HEADS, HEAD_PAD, T), F32)] + [jax.ShapeDtypeStruct((HEADS, T, HEAD_PAD), F32)] * 2,
        compiler_params=_cparams(),
    )(*_pair_tables(pairs), q, k, kt, v, do, lse, delta)


def mixin_bwd(dres, x, z, rope, dqt, dk, dv, du, dvs, dpin, gmix, win, gql, wq, gkvl, wkv, gqh, gkh):
    T = x.shape[0]
    tm = min(ROW_TILE, T)
    part = min(MIXIN_BWD_PART, tm)

    def body(dres_ref, x_ref, z_ref, c_ref, sa_ref, sb_ref, dqt_ref, dk_ref, dv_ref, du_ref, dvs_ref, dpin_ref,
             gmix_ref, win_ref, gql_ref, wq_ref, gkvl_ref, wkv_ref, gqh_ref, gkh_ref,
             dx_ref, hn_ref, dz_ref, qn_ref, dqup_ref, kvn_ref, dkvup_ref,
             dgmix_ref, dgql_ref, dgkvl_ref, dgqh_ref, dgkh_ref):
        @pl.when(pl.program_id(0) == 0)
        def _():
            for r in (dgmix_ref, dgql_ref, dgkvl_ref, dgqh_ref, dgkh_ref):
                r[...] = jnp.zeros(r.shape, F32)

        lane = lax.broadcasted_iota(jnp.int32, (part, HEAD_PAD), 1)
        rope_lanes = (lane >= QK_NOPE) & (lane < QK_HEAD)

        def recompute(g):
            rows = g["rows"]
            g["xv"] = x_ref[rows, :]
            hn, g["rx"] = _rms(g["xv"], gmix_ref[...], D_MODEL)
            hn_ref[rows, :] = hn.astype(BF16)
            g["ql"], g["kvl"], g["kr"], qn, g["rq"], kvn, g["rkv"], g["q_up"], g["kv_up"] = _qkv_pre(
                z_ref[rows, :], gql_ref[...], wq_ref[...], gkvl_ref[...], wkv_ref[...])
            qn_ref[rows, :] = qn
            kvn_ref[rows, :] = kvn

        def heads(g):
            rows = g["rows"]
            c, sa, sb = c_ref[rows, :], sa_ref[rows, :], sb_ref[rows, :]
            dkr = jnp.zeros((part, HEAD_PAD), F32)
            dgqh = jnp.zeros((1, HEAD_PAD), F32)
            dgkh = jnp.zeros((1, HEAD_PAD), F32)
            dq_parts, dk_parts, dv_parts = [], [], []
            for h in range(HEADS):
                lo = HEAD_PAD * h
                qh = g["q_up"][:, lo:lo + HEAD_PAD]
                rqh = lax.rsqrt(jnp.sum(qh * qh, axis=-1, keepdims=True) * (1.0 / QK_HEAD) + EPS)
                dq_h = jnp.transpose(dqt_ref[h, :, rows]) * ATTN_SCALE
                dqh, dg = _rms_bwd(qh, rqh, gqh_ref[...], _rope_bwd(dq_h, c, sa, sb), QK_HEAD)
                dgqh = dgqh + dg
                dq_parts.append(dqh)
                kh = g["kv_up"][:, lo:lo + HEAD_PAD] + g["kr"]
                rkh = lax.rsqrt(jnp.sum(kh * kh, axis=-1, keepdims=True) * (1.0 / QK_HEAD) + EPS)
                dkh, dg = _rms_bwd(kh, rkh, gkh_ref[...], _rope_bwd(dk_ref[h, rows, :], c, sa, sb), QK_HEAD)
                dgkh = dgkh + dg
                dkr = dkr + jnp.where(rope_lanes, dkh, 0.0)
                dk_parts.append(dkh)
                dv_parts.append(dv_ref[h, rows, :])
            dgqh_ref[...] += dgqh
            dgkh_ref[...] += dgkh
            g["dkr"] = dkr
            g["dq_up"] = jnp.concatenate(dq_parts, axis=1).astype(BF16)
            g["dkv_up"] = jnp.concatenate(dk_parts + dv_parts, axis=1).astype(BF16)
            dqup_ref[rows, :] = g["dq_up"]
            dkvup_ref[rows, :] = g["dkv_up"]

        def latents(g):
            rows = g["rows"]
            dql, dg = _rms_bwd(g["ql"], g["rq"], gql_ref[...], _dot(g["dq_up"], wq_ref[...]), Q_LORA)
            dgql_ref[...] += dg
            dkvl, dg = _rms_bwd(g["kvl"], g["rkv"], gkvl_ref[...], _dot(g["dkv_up"], wkv_ref[...]), KV_LORA)
            dgkvl_ref[...] += dg
            g["dz"] = jnp.concatenate([dql.astype(BF16), dkvl.astype(BF16), g["dkr"].astype(BF16), du_ref[rows, :],
                                       dvs_ref[rows, :], dpin_ref[rows, :]], axis=1)
            dz_ref[rows, :] = g["dz"]

        def inputs(g):
            rows = g["rows"]
            dxn, dg = _rms_bwd(g["xv"], g["rx"], gmix_ref[...], _dot(g["dz"], win_ref[...]), D_MODEL)
            dgmix_ref[...] += dg
            dx_ref[rows, :] = dres_ref[rows, :] + dxn

        _skewed((recompute, heads, latents, inputs), [dict(rows=pl.ds(part * p, part)) for p in range(tm // part)])

    row = lambda w: pl.BlockSpec((tm, w), lambda i: (i, 0))
    head = pl.BlockSpec((HEADS, tm, HEAD_PAD), lambda i: (0, i, 0))
    head_t = pl.BlockSpec((HEADS, HEAD_PAD, tm), lambda i: (0, 0, i))
    acc_shapes = [(1, D_MODEL), (1, Q_LORA), (1, KV_LORA), (1, HEAD_PAD), (1, HEAD_PAD)]
    out_rows = [(D_MODEL, F32), (D_MODEL, BF16), (Z_WIDTH, BF16), (Q_LORA, BF16), (HEADS * HEAD_PAD, BF16),
                (KV_LORA, BF16), (2 * HEADS * HEAD_PAD, BF16)]
    return pl.pallas_call(
        body, grid=(T // tm,), name="mixin_bwd",
        in_specs=[row(D_MODEL), row(D_MODEL), row(Z_WIDTH)] + [row(HEAD_PAD)] * 3 + [head_t, head, head, row(256), row(256), row(256)]
                 + [WHOLE] * 8,
        out_specs=[row(w) for w, _ in out_rows] + [_acc_spec(s) for s in acc_shapes],
        out_shape=[jax.ShapeDtypeStruct((T, w), dt) for w, dt in out_rows] + [jax.ShapeDtypeStruct(s, F32) for s in acc_shapes],
        compiler_params=_cparams(),
    )(dres, x, z, *rope, dqt, dk, dv, du, dvs, dpin, gmix, win, gql, wq, gkvl, wkv, gqh, gkh)


def _place():
    x, y, c = lax.axis_index("x"), lax.axis_index("y"), lax.axis_index("c")
    return x, y, c, 4 * x + 2 * y + c


def _layer_of(ref, l):
    return ref[:, l, :] if ref.shape[1] == DEPTH else ref[l]


def _layer_shape(shard):
    return (shard.shape[0], shard.shape[2]) if shard.shape[1] == DEPTH else shard.shape[1:]


def cast_shards(shards, wanted):
    n = len(shards)

    def body(*refs):
        for o_ref, (w, l) in zip(refs[n:], wanted):
            o_ref[...] = _layer_of(refs[w], l).astype(BF16)

    return pl.pallas_call(
        body, name="cast_shards", in_specs=[WHOLE] * n, out_specs=[WHOLE] * len(wanted),
        out_shape=[jax.ShapeDtypeStruct(_layer_shape(shards[w]), BF16) for w, _ in wanted],
        compiler_params=_cparams(),
    )(*shards)


def _peer(k):
    x, y, c, _ = _place()
    px = 1 - x if k & 4 else x
    py = 1 - y if k & 2 else y
    pc = 1 - c if k & 1 else c
    return (px, py, pc), 4 * px + 2 * py + pc


def exchange_start(srcs, after, gather, name):
    n = len(srcs)
    land_shapes = [((N_DEV,) + s.shape) if gather else s.shape for s in srcs]

    def body(*refs):
        src_refs, land_refs = refs[:n], refs[n:2 * n]
        send_sems, recv_sems = refs[2 * n + 1:3 * n + 1], refs[3 * n + 1:4 * n + 1]
        token = refs[-1]
        _, _, _, me = _place()
        for k in range(1, N_DEV):
            peer, peer_id = _peer(k)
            for w in range(n):
                pltpu.make_async_remote_copy(
                    src_ref=src_refs[w] if gather else src_refs[w].at[peer_id], dst_ref=land_refs[w].at[me],
                    send_sem=send_sems[w], recv_sem=recv_sems[w], device_id=peer, device_id_type=MESH).start()
        for w in range(n):
            pltpu.make_async_copy(src_refs[w] if gather else src_refs[w].at[me], land_refs[w].at[me], recv_sems[w]).start()
        token[...] = jnp.zeros(token.shape, F32)

    hbm = lambda a: pltpu.with_memory_space_constraint(a, pltpu.HBM)
    outs = pl.pallas_call(
        body, name=name,
        out_shape=(pltpu.SemaphoreType.DMA(()),) * (2 * n)
                  + tuple(pltpu.HBM(s.shape, BF16) for s in srcs) + tuple(pltpu.HBM(s, BF16) for s in land_shapes)
                  + (jax.ShapeDtypeStruct((8, 128), F32),),
        in_specs=[HBM_SPEC] * (2 * n) + [ANY],
        out_specs=(SEM_SPEC,) * (2 * n) + (HBM_SPEC,) * (2 * n) + (WHOLE,),
        input_output_aliases={i: 2 * n + i for i in range(2 * n)},
        compiler_params=pltpu.CompilerParams(has_side_effects=pltpu.SideEffectType.DATAFLOW_SIDE_EFFECTING),
    )(*[hbm(s) for s in srcs], *[hbm(lax.empty(s, BF16)) for s in land_shapes], after)
    return list(outs[:n]), list(outs[n:2 * n]), list(outs[2 * n:3 * n]), list(outs[3 * n:4 * n]), outs[-1]


def exchange_wait(started, after, name):
    send_sems, recv_sems, srcs, lands, _ = started
    n = len(srcs)

    def body(*refs):
        land_refs = refs[n:2 * n]
        send_sems, recv_sems = refs[2 * n:3 * n], refs[3 * n:4 * n]
        x, y, c, _ = _place()
        for w in range(n):
            seven, eight = land_refs[w].at[pl.ds(0, N_DEV - 1)], land_refs[w]
            pltpu.make_async_remote_copy(src_ref=seven, dst_ref=seven, send_sem=send_sems[w], recv_sem=recv_sems[w],
                                         device_id=(x, y, c), device_id_type=MESH).wait_send()
            pltpu.make_async_remote_copy(src_ref=eight, dst_ref=eight, send_sem=send_sems[w], recv_sem=recv_sems[w],
                                         device_id=(x, y, c), device_id_type=MESH).wait_recv()

    outs = pl.pallas_call(
        body, name=name,
        out_shape=tuple(pltpu.HBM(s.shape, BF16) for s in srcs) + tuple(pltpu.HBM(l.shape, BF16) for l in lands),
        in_specs=[HBM_SPEC] * (2 * n) + [SEM_SPEC] * (2 * n) + [ANY],
        out_specs=(HBM_SPEC,) * (2 * n),
        input_output_aliases={i: i for i in range(2 * n)},
        compiler_params=pltpu.CompilerParams(has_side_effects=pltpu.SideEffectType.DATAFLOW_SIDE_EFFECTING),
    )(*srcs, *lands, *send_sems, *recv_sems, after)
    return list(outs[n:])


def _adamw(w, g, m, v):
    m2 = ADAM_B1 * m + (1.0 - ADAM_B1) * g
    v2 = ADAM_B2 * v + (1.0 - ADAM_B2) * (g * g)
    m_hat = m2 / (1.0 - ADAM_B1 ** ADAM_STEP)
    v_hat = v2 / (1.0 - ADAM_B2 ** ADAM_STEP)
    delta = -ADAM_LR * (m_hat / (jnp.sqrt(v_hat) + ADAM_EPS) + ADAM_WD * w)
    return delta, m2, v2


def adamw_sharded(parts, w, m, v, name):
    L, R, C = w.shape
    fits = [d for d in range(16, min(R, 512) + 1, 16) if R % d == 0]
    br = max(fits) if fits else R
    nblk = R // br

    def body(*refs):
        p_refs = refs[:L]
        w_ref, m_ref, v_ref, g_ref, d_ref, m2_ref, v2_ref = refs[L:]

        def total(p_ref):
            g = p_ref[0].astype(F32)
            for s in range(1, N_DEV):
                g = g + p_ref[s].astype(F32)
            return g

        g = total(p_refs[0])
        for l in range(1, L):
            g = jnp.where(pl.program_id(0) == l, total(p_refs[l]), g)
        g_ref[...] = g
        d_ref[...], m2_ref[...], v2_ref[...] = _adamw(w_ref[...], g, m_ref[...], v_ref[...])

    blk = pl.BlockSpec((None, br, C), lambda l, i: (l, i, 0))

    def part_spec(k):
        return pl.BlockSpec((N_DEV, br, C), lambda l, i: (0, jnp.where(l == k, i, jnp.where(l < k, 0, nblk - 1)), 0))

    return pl.pallas_call(
        body, grid=(L, nblk), name=name,
        in_specs=[part_spec(k) for k in range(L)] + [blk, blk, blk],
        out_specs=[blk] * 4,
        out_shape=[jax.ShapeDtypeStruct((L, R, C), F32)] * 4,
        compiler_params=_cparams(),
    )(*parts, w, m, v)


def allreduce_adamw_small(gpart, w, m, v):
    R = gpart.shape[0]

    def body(g_ref, w_ref, m_ref, v_ref, grad_ref, d_ref, m2_ref, v2_ref, all_ref, send_sems, recv_sems):
        x, y, c, me = _place()
        sibling = (x, y, 1 - c)
        chips = [(1 - x, y), (x, 1 - y), (1 - x, 1 - y)]

        def copy(k, block_id, to, from_input):
            return pltpu.make_async_remote_copy(
                src_ref=g_ref if from_input else all_ref.at[block_id], dst_ref=all_ref.at[block_id],
                send_sem=send_sems.at[k], recv_sem=recv_sems.at[k], device_id=to, device_id_type=MESH)

        def block_of(cx, cy, cc):
            return 4 * cx + 2 * cy + cc

        all_ref[me] = g_ref[...]
        first = [copy(0, me, sibling, True)] + [copy(1 + j, me, (*chip, c), True) for j, chip in enumerate(chips)]
        for cp in first:
            cp.start()
        passed = [copy(4 + j, block_of(*chip, c), sibling, False) for j, chip in enumerate(chips)]
        for j, chip in enumerate(chips):
            copy(1 + j, block_of(*chip, c), (x, y, c), False).wait_recv()
            passed[j].start()
        copy(0, block_of(x, y, 1 - c), (x, y, c), False).wait_recv()
        for j, chip in enumerate(chips):
            copy(4 + j, block_of(*chip, 1 - c), (x, y, c), False).wait_recv()
        for cp in first + passed:
            cp.wait_send()
        g = all_ref[0]
        for s in range(1, N_DEV):
            g = g + all_ref[s]
        grad_ref[...] = g
        d_ref[...], m2_ref[...], v2_ref[...] = _adamw(w_ref[...], g, m_ref[...], v_ref[...])

    return pl.pallas_call(
        body, name="allreduce_adamw_small",
        in_specs=[WHOLE] * 4, out_specs=[WHOLE] * 4,
        out_shape=[jax.ShapeDtypeStruct((R, 128), F32)] * 4,
        scratch_shapes=[pltpu.VMEM((N_DEV, R, 128), F32), pltpu.SemaphoreType.DMA((7,)), pltpu.SemaphoreType.DMA((7,))],
        compiler_params=_cparams(),
    )(gpart, w, m, v)


def _shard_view(name, a):
    if name == "w_in":
        return a.transpose(2, 0, 1)
    return a.swapaxes(1, 2) if name in TRANSPOSED else a


def _shard_unview(name, a):
    if name == "w_in":
        return a.transpose(1, 2, 0)
    return a.swapaxes(1, 2) if name in TRANSPOSED else a


def _pad_head_rows(w, width):
    c = w.shape[1]
    return jnp.pad(w.reshape(HEADS, width, c), ((0, 0), (0, HEAD_PAD - width), (0, 0))).reshape(HEADS * HEAD_PAD, c)


def _unpad_head_rows(w, width):
    c = w.shape[1]
    return w.reshape(HEADS, HEAD_PAD, c)[:, :width].reshape(HEADS * width, c)


O1 = Q_LORA
O2 = O1 + KV_LORA
O3 = O2 + QK_ROPE


def _mixer_weights(gw):
    w_in = gw["w_in"].reshape(IN_WIDTH, D_MODEL)
    zero = lambda n: jnp.zeros((n, D_MODEL), BF16)
    win = jnp.concatenate([w_in[:O2], zero(QK_NOPE), w_in[O2:O3], zero(HEAD_PAD - QK_HEAD), w_in[O3:]], axis=0)
    wq = _pad_head_rows(gw["w_q_up"].reshape(HEADS * QK_HEAD, Q_LORA), QK_HEAD)
    w_kv = gw["w_kv_up"].reshape(HEADS, QK_NOPE + V_HEAD, KV_LORA)
    wk = jnp.pad(w_kv[:, :QK_NOPE], ((0, 0), (0, HEAD_PAD - QK_NOPE), (0, 0))).reshape(HEADS * HEAD_PAD, KV_LORA)
    wv = w_kv[:, QK_NOPE:].reshape(HEADS * V_HEAD, KV_LORA)
    wkv = jnp.concatenate([wk, wv], axis=0)
    out = dict(win=win, wq=wq, wkv=wkv)
    if "w_out" in gw:
        out["wout"] = gw["w_out"].reshape(D_MODEL, D_MODEL)
    return out


def _ffn_weights(gw):
    return dict(wg=gw["w_gate"].reshape(FFN_HIDDEN, D_MODEL), wu=gw["w_up"].reshape(FFN_HIDDEN, D_MODEL),
                wd=gw["w_down"].reshape(FFN_HIDDEN, D_MODEL))


def _layer_small(p, l):
    row = lambda a: a.reshape(1, -1)
    pad_head = lambda g: jnp.pad(g, (0, HEAD_PAD - QK_HEAD)).reshape(1, HEAD_PAD)
    tril = jnp.tril(jnp.ones((CHUNK, CHUNK), F32))
    wsp = p["w_spatial"][l] * tril
    wcat = jnp.concatenate([wsp[h] for h in range(HEADS)], axis=1).astype(BF16)
    wcat_t = jnp.concatenate([wsp[h].T for h in range(HEADS)], axis=1).astype(BF16)
    bz = jnp.repeat(p["b_spatial"][l].T, SGU_HEAD_DIM, axis=1)
    wp = jax.scipy.linalg.block_diag(*[p["w_pool"][l][g] for g in range(HEADS)]).astype(BF16)
    return dict(gmix=row(p["g_mix_norm"][l]), gql=row(p["g_q_lat"][l]), gkvl=row(p["g_kv_lat"][l]),
                gqh=pad_head(p["g_q_head"][l]), gkh=pad_head(p["g_k_head"][l]), gsgu=row(p["g_sgu_v"][l]),
                wcat=wcat, wcat_t=wcat_t, bz=bz, wp=wp, pscale=row(p["pool_scale"][l]),
                goa=row(p["g_out_mla"][l]), gos=row(p["g_out_sgu"][l]), gop=row(p["g_out_pool"][l]),
                gffn=row(p["g_ffn_norm"][l]))


MIXER_SIDE = ("w_in", "w_q_up", "w_kv_up", "w_out")
FFN_SIDE = ("w_gate", "w_up", "w_down")
TRANSPOSED = ("w_in", "w_q_up", "w_kv_up", "w_gate", "w_up")
FIRST_0 = ("w_in", "w_q_up", "w_kv_up")
LATER_0 = ("w_out",) + FFN_SIDE
SHARDED = MIXER_SIDE + FFN_SIDE
SMALL = ("g_mix_norm", "g_q_lat", "g_kv_lat", "g_q_head", "g_k_head", "g_sgu_v", "w_spatial", "b_spatial", "w_pool",
         "pool_scale", "g_out_mla", "g_out_sgu", "g_out_pool", "g_ffn_norm")
WEIGHTS = ("g_mix_norm", "w_in", "g_q_lat", "w_q_up", "g_kv_lat", "w_kv_up", "g_q_head", "g_k_head", "g_sgu_v", "w_spatial",
           "b_spatial", "w_pool", "pool_scale", "g_out_mla", "g_out_sgu", "g_out_pool", "w_out", "g_ffn_norm", "w_gate",
           "w_up", "w_down")
PACKED = SMALL + ("loss",)
PACK_ROWS = 8 * 128


def _pack_small(parts):
    flat = []
    for name in PACKED:
        a = parts[name].reshape(-1)
        flat.append(jnp.pad(a, (0, -a.shape[0] % PACK_ROWS)))
    return jnp.concatenate(flat).reshape(-1, 128)


def _unpack_small(packed, like):
    out, row = {}, 0
    for name in PACKED:
        n = math.prod(like[name].shape)
        rows = -(-n // PACK_ROWS) * 8
        out[name] = packed[row:row + rows].reshape(-1)[:n].reshape(like[name].shape)
        row += rows
    return out


def _forward_attention(x, rope, W, S):
    z, q, k, v, kt, vt = mixin_fwd(x, rope, S["gmix"], W["win"], S["gql"], W["wq"], S["gkvl"], W["wkv"], S["gqh"], S["gkh"])
    a, lse = attn_fwd(q, k, vt)
    return dict(x=x, z=z, q=q, k=k, kt=kt, v=v, a=a, lse=lse)


def _forward_mixers(A, wout, S):
    A["x1"] = mixers_fwd(A["x"], A["a"], A["z"], S["gsgu"], S["wcat"], S["bz"], S["wp"], S["pscale"], S["goa"], S["gos"],
                         S["gop"], wout)
    return A["x1"]


def _backward_ffn(dx2, W, S, A, l):
    dx1, h2, act, dg, du_ffn, dgffn = ffn_bwd(dx2, A["x1"], A["gs"], A["us"], S["gffn"], W["wg"], W["wu"], W["wd"])
    d_wd = matmul_tn(act, dx2, f"dw_down_{l}")
    d_wg = matmul_tn(dg, h2, f"dw_gate_{l}")
    d_wu = matmul_tn(du_ffn, h2, f"dw_up_{l}")
    big = {n: d.reshape(N_DEV, -1, D_MODEL) for n, d in (("w_gate", d_wg), ("w_up", d_wu), ("w_down", d_wd))}
    return dx1, big, dgffn


def _backward_mixers(dx1, dgffn, rope, W, S, A, l, send_wout=None):
    (da, delta, du, dvs, dm, mix, dgoa, dgos, dgop, dps, dwp, dwsp, db, dgsgu) = mixers_bwd(
        dx1, A["a"], A["z"], S["gsgu"], S["wcat"], S["wcat_t"], S["bz"], S["wp"], S["pscale"], S["goa"], S["gos"],
        S["gop"], W["wout"])
    d_wout = matmul_tn(mix, dx1, f"dw_out_{l}")
    if send_wout is not None:
        delta = delta + send_wout(d_wout.reshape(N_DEV, -1, D_MODEL))[0, 0]
    dpin = pool_bwd(dm)
    dqt, dk, dv = attn_bwd(A["q"], A["k"], A["kt"], A["v"], da, A["lse"], delta)
    (dx, hn, dz, qn, dq_up, kvn, dkv_up, dgmix, dgql, dgkvl, dgqh, dgkh) = mixin_bwd(
        dx1, A["x"], A["z"], rope, dqt, dk, dv, du, dvs, dpin, S["gmix"], W["win"], S["gql"], W["wq"], S["gkvl"],
        W["wkv"], S["gqh"], S["gkh"])
    d_win = matmul_tn(dz, hn, f"dw_in_{l}")
    d_wq = matmul_tn(dq_up, qn, f"dw_q_up_{l}")
    d_wkv = matmul_tn(dkv_up, kvn, f"dw_kv_up_{l}")
    d_win = jnp.concatenate([d_win[:O2], d_win[O2 + QK_NOPE:O2 + QK_HEAD], d_win[O2 + HEAD_PAD:]], axis=0)
    d_wk = d_wkv[:HEADS * HEAD_PAD].reshape(HEADS, HEAD_PAD, KV_LORA)[:, :QK_NOPE]
    d_wv = d_wkv[HEADS * HEAD_PAD:].reshape(HEADS, V_HEAD, KV_LORA)
    d_wkv = jnp.concatenate([d_wk, d_wv], axis=1)
    big = dict(w_in=d_win.reshape(N_DEV, -1, D_MODEL), w_q_up=_unpad_head_rows(d_wq, QK_HEAD).reshape(N_DEV, -1, Q_LORA),
               w_kv_up=d_wkv.reshape(N_DEV, -1, KV_LORA), w_out=d_wout.reshape(N_DEV, -1, D_MODEL))
    tril = jnp.tril(jnp.ones((CHUNK, CHUNK), F32))
    small = dict(g_mix_norm=dgmix[0], g_q_lat=dgql[0], g_kv_lat=dgkvl[0], g_q_head=dgqh[0, :QK_HEAD], g_k_head=dgkh[0, :QK_HEAD],
                 g_sgu_v=dgsgu[0], w_spatial=dwsp.reshape(HEADS, CHUNK, CHUNK) * tril, b_spatial=db[:, :HEADS].T,
                 w_pool=jnp.stack([dwp[64 * g:64 * (g + 1), 64 * g:64 * (g + 1)] for g in range(HEADS)]),
                 pool_scale=dps[0], g_out_mla=dgoa[0], g_out_sgu=dgos[0], g_out_pool=dgop[0], g_ffn_norm=dgffn[0])
    return dx, big, small


def kernel(x, positions, g_mix_norm, w_in, g_q_lat, w_q_up, g_kv_lat, w_kv_up, g_q_head, g_k_head, g_sgu_v, w_spatial, b_spatial, w_pool, pool_scale, g_out_mla, g_out_sgu, g_out_pool, w_out, g_ffn_norm, w_gate, w_up, w_down, loss_target, m_g_mix_norm, m_w_in, m_g_q_lat, m_w_q_up, m_g_kv_lat, m_w_kv_up, m_g_q_head, m_g_k_head, m_g_sgu_v, m_w_spatial, m_b_spatial, m_w_pool, m_pool_scale, m_g_out_mla, m_g_out_sgu, m_g_out_pool, m_w_out, m_g_ffn_norm, m_w_gate, m_w_up, m_w_down, v_g_mix_norm, v_w_in, v_g_q_lat, v_w_q_up, v_g_kv_lat, v_w_kv_up, v_g_q_head, v_g_k_head, v_g_sgu_v, v_w_spatial, v_b_spatial, v_w_pool, v_pool_scale, v_g_out_mla, v_g_out_sgu, v_g_out_pool, v_w_out, v_g_ffn_norm, v_w_gate, v_w_up, v_w_down):
    given = dict(locals())
    w = {n: given[n] for n in WEIGHTS}
    m = {n: given["m_" + n] for n in WEIGHTS}
    v = {n: given["v_" + n] for n in WEIGHTS}
    T = x.shape[1]
    xs = x.reshape(T, D_MODEL)

    half = QK_ROPE // 2
    inv_freq = 1.0 / (ROPE_THETA ** (jnp.arange(half, dtype=F32) / half))
    ang16 = positions.reshape(T).astype(F32)[:, None] * inv_freq
    ang = jnp.concatenate([jnp.zeros((T, QK_NOPE), F32), ang16, ang16, jnp.zeros((T, HEAD_PAD - QK_HEAD), F32)], axis=1)

    wv = {n: _shard_view(n, w[n]) for n in SHARDED}
    wanted = [(SHARDED.index(n), 0) for n in FIRST_0 + LATER_0] + [(i, 1) for i in range(len(SHARDED))]
    bf = cast_shards([wv[n] for n in SHARDED], wanted)
    nf, nl = len(FIRST_0), len(LATER_0)
    ag_first0 = exchange_start(bf[:nf], ang, True, "ag_start_first0")
    rope = rope_tables(ang + ag_first0[-1][0, 0])
    W0 = dict(zip(FIRST_0, exchange_wait(ag_first0, rope[0], "ag_wait_first0")))
    ag_later0 = exchange_start(bf[nf:nf + nl], W0["w_in"], True, "ag_start_later0")
    ag_l1 = exchange_start(bf[nf + nl:], ag_later0[-1], True, "ag_start_l1")
    Ss = [_layer_small(w, l) for l in range(DEPTH)]

    S0 = dict(Ss[0], gmix=Ss[0]["gmix"] + (ag_later0[-1][0, 0] + ag_l1[-1][0, 0]))
    A0 = _forward_attention(xs, rope, _mixer_weights(W0), S0)
    W0.update(zip(LATER_0, exchange_wait(ag_later0, A0["a"], "ag_wait_later0")))
    W0 = {**_mixer_weights(W0), **_ffn_weights(W0)}
    x1 = _forward_mixers(A0, W0["wout"], Ss[0])
    h, A0["gs"], A0["us"] = ffn_fwd(x1, Ss[0]["gffn"], W0["wg"], W0["wu"], W0["wd"])
    layer1 = dict(zip(SHARDED, exchange_wait(ag_l1, h, "ag_wait_l1")))
    W1 = {**_mixer_weights(layer1), **_ffn_weights(layer1)}
    A1 = _forward_attention(h, rope, W1, Ss[1])
    x1 = _forward_mixers(A1, W1["wout"], Ss[1])
    dh, loss_part, A1["gs"], A1["us"] = ffn_fwd(x1, Ss[1]["gffn"], W1["wg"], W1["wu"], W1["wd"],
                                                target=loss_target.reshape(T, D_MODEL))

    dx1, big_f1, dgffn1 = _backward_ffn(dh, W1, Ss[1], A1, 1)
    dh, big_m1, small1 = _backward_mixers(dx1, dgffn1, rope, W1, Ss[1], A1, 1)
    rs_l1 = exchange_start([{**big_m1, **big_f1}[n] for n in SHARDED], dh, False, "rs_start_l1")
    S0 = dict(Ss[0], gffn=Ss[0]["gffn"] + rs_l1[-1][0, 0])
    dx1, big_f0, dgffn0 = _backward_ffn(dh, W0, S0, A0, 0)
    rs_ffn0 = exchange_start([big_f0[n] for n in FFN_SIDE], dx1, False, "rs_start_ffn0")
    S0 = dict(Ss[0], gsgu=Ss[0]["gsgu"] + rs_ffn0[-1][0, 0])
    rs_out0 = []

    def send_wout(pieces):
        rs_out0.append(exchange_start([pieces], dx1, False, "rs_start_out0"))
        return rs_out0[0][-1]

    dh, big_m0, small0 = _backward_mixers(dx1, dgffn0, rope, W0, S0, A0, 0, send_wout)
    grad_x = dh.reshape(x.shape)
    smalls = [small0, small1]

    rs_first0 = exchange_start([big_m0[n] for n in FIRST_0], dh, False, "rs_start_first0")
    parts1 = dict(zip(SHARDED, exchange_wait(rs_l1, rs_first0[-1], "rs_wait_l1")))
    parts0 = dict(zip(FFN_SIDE, exchange_wait(rs_ffn0, rs_first0[-1], "rs_wait_ffn0")))
    parts0["w_out"] = exchange_wait(rs_out0[0], rs_first0[-1], "rs_wait_out0")[0]
    grad, delta, new_m, new_v = {}, {}, {}, {}

    small_part = {n: jnp.stack([smalls[l][n] for l in range(DEPTH)]) for n in SMALL}
    small_part["loss"] = loss_part
    no_loss = {"loss": jnp.zeros((1, 1), F32)}
    outs = allreduce_adamw_small(_pack_small(small_part), *[_pack_small({**a, **no_loss}) for a in (w, m, v)])
    for d, o in zip((grad, delta, new_m, new_v), outs):
        d.update(_unpack_small(o, {**w, **no_loss}))
    loss = grad["loss"][0, 0]

    for n in LATER_0 + FIRST_0:
        if n == FIRST_0[0]:
            parts0.update(zip(FIRST_0, exchange_wait(rs_first0, grad["w_down"], "rs_wait_first0")))
        parts, views = [parts0[n], parts1[n]], [_shard_view(n, a[n]) for a in (w, m, v)]
        if n == "w_in":
            parts = [jnp.stack(parts, axis=2).reshape(N_DEV, -1, D_MODEL)]
            views = [a.reshape(1, -1, D_MODEL) for a in views]
        outs = adamw_sharded(parts, *views, f"adamw_{n}")
        if n == "w_in":
            outs = [o.reshape(-1, DEPTH, D_MODEL) for o in outs]
        grad[n], delta[n], new_m[n], new_v[n] = [_shard_unview(n, o) for o in outs]

    return (loss, grad_x, *[grad[n] for n in WEIGHTS], *[delta[n] for n in WEIGHTS], *[new_m[n] for n in WEIGHTS],
            *[new_v[n] for n in WEIGHTS])
```

```python
import math

import jax
import jax.numpy as jnp
from jax import lax
from jax.experimental import pallas as pl
from jax.experimental.pallas import tpu as pltpu

F32 = jnp.float32
BF16 = jnp.bfloat16

N_DEV = 8
DEPTH = 2
D_MODEL = 1024
HEADS = 4
HEAD_PAD = 128
QK_NOPE = 64
QK_ROPE = 32
QK_HEAD = QK_NOPE + QK_ROPE
V_HEAD = 128
Q_LORA = 256
KV_LORA = 128
SGU_WIDTH = 256
SGU_HEAD_DIM = 64
CHUNK = 128
POOL_WIDTH = 256
POOL_HALO = 16
MLA_WIDTH = 512
IN_WIDTH = 1184
Z_WIDTH = 1280
FFN_HIDDEN = 2816
FFN_CHUNK = 256
ROPE_THETA = 10000.0
EPS = 1e-6
ATTN_SCALE = 1.0 / math.sqrt(QK_HEAD)
LOG2E = 1.4426950408889634
NEG_BIG = -1e30

ADAM_LR = 0.001
ADAM_B1 = 0.9
ADAM_B2 = 0.999
ADAM_EPS = 1e-08
ADAM_WD = 0.01
ADAM_STEP = 10

VMEM_LIMIT = 56 * 1024 * 1024
ROW_TILE = 512
MIXIN_PART = 256
MIXIN_BWD_PART = 128
ATTN_TILE = 1024
ATTN_SUB = 512
ATTN_FWD_SUB = 512
ATTN_BWD_HEADS = 2
MESH = pl.DeviceIdType.MESH

WHOLE = pl.BlockSpec(memory_space=pltpu.VMEM)
ANY = pl.BlockSpec(memory_space=pl.ANY)
HBM_SPEC = pl.BlockSpec(memory_space=pltpu.HBM)
SEM_SPEC = pl.BlockSpec(memory_space=pltpu.SEMAPHORE)


def _cparams(**kw):
    return pltpu.CompilerParams(vmem_limit_bytes=VMEM_LIMIT, **kw)


def _dot(a, b):
    return jnp.dot(a, b, preferred_element_type=F32)


def _dot_nt(a, b):
    return lax.dot_general(a, b, (((1,), (1,)), ((), ())), preferred_element_type=F32)


def _dot_tn(a, b):
    return lax.dot_general(a, b, (((0,), (0,)), ((), ())), preferred_element_type=F32)


def _rms(x, g, n):
    r = lax.rsqrt(jnp.sum(x * x, axis=-1, keepdims=True) * (1.0 / n) + EPS)
    return x * r * g, r


def _rms_bwd(x, r, g, dy, n):
    gdy = dy * g
    dx = r * gdy - x * (r * r * r) * (jnp.sum(x * gdy, axis=-1, keepdims=True) * (1.0 / n))
    dg = jnp.sum(dy * (x * r), axis=0, keepdims=True)
    return dx, dg


def _sigmoid(x):
    return 1.0 / (1.0 + jnp.exp(-x))


def rope_tables(ang):
    T = ang.shape[0]
    tm = min(ROW_TILE, T)

    def body(ang_ref, c_ref, sa_ref, sb_ref):
        a = ang_ref[...]
        lane = lax.broadcasted_iota(jnp.int32, a.shape, 1)
        s = jnp.sin(a)
        c_ref[...] = jnp.cos(a)
        sa_ref[...] = jnp.where(lane < QK_NOPE + QK_ROPE // 2, -s, 0.0)
        sb_ref[...] = jnp.where(lane >= QK_NOPE + QK_ROPE // 2, s, 0.0)

    row = pl.BlockSpec((tm, HEAD_PAD), lambda i: (i, 0))
    return pl.pallas_call(
        body, grid=(T // tm,), name="rope_tables", in_specs=[row], out_specs=[row] * 3,
        out_shape=[jax.ShapeDtypeStruct((T, HEAD_PAD), F32)] * 3, compiler_params=_cparams(),
    )(ang)


def _rope(x, c, sa, sb):
    half = QK_ROPE // 2
    return x * c + pltpu.roll(x, HEAD_PAD - half, 1) * sa + pltpu.roll(x, half, 1) * sb


def _rope_bwd(dy, c, sa, sb):
    half = QK_ROPE // 2
    return dy * c + pltpu.roll(dy * sa, half, 1) + pltpu.roll(dy * sb, HEAD_PAD - half, 1)


def _skewed(stages, groups):
    for t in range(len(stages) + len(groups) - 1):
        for p, g in enumerate(groups):
            if 0 <= t - p < len(stages):
                stages[t - p](g)


def _head_masks(shape, width):
    lane = lax.broadcasted_iota(jnp.int32, shape, len(shape) - 1)
    return [(lane >= width * h) & (lane < width * (h + 1)) for h in range(HEADS)]


def _qkv_pre(z, gql, wq, gkvl, wkv):
    ql = z[:, 0:Q_LORA]
    kvl = z[:, Q_LORA:Q_LORA + KV_LORA]
    kr = z[:, Q_LORA + KV_LORA:Q_LORA + KV_LORA + HEAD_PAD]
    qn, rq = _rms(ql, gql, Q_LORA)
    kvn, rkv = _rms(kvl, gkvl, KV_LORA)
    qn = qn.astype(BF16)
    kvn = kvn.astype(BF16)
    q_up = _dot_nt(qn, wq)
    kv_up = _dot_nt(kvn, wkv)
    return ql, kvl, kr, qn, rq, kvn, rkv, q_up, kv_up


def mixin_fwd(x, rope, gmix, win, gql, wq, gkvl, wkv, gqh, gkh):
    T = x.shape[0]
    tm = min(2 * ROW_TILE, T)
    part = min(MIXIN_PART, tm)

    def body(x_ref, c_ref, sa_ref, sb_ref, gmix_ref, win_ref, gql_ref, wq_ref, gkvl_ref, wkv_ref, gqh_ref, gkh_ref,
             z_ref, q_ref, k_ref, v_ref, kt_ref, vt_ref):
        def project(g):
            hn = _rms(x_ref[g["rows"], :], gmix_ref[...], D_MODEL)[0].astype(BF16)
            g["z"] = _dot_nt(hn, win_ref[...])
            z_ref[g["rows"], :] = g["z"]

        def latents(g):
            _, _, g["kr"], _, _, _, _, g["q_up"], g["kv_up"] = _qkv_pre(g["z"], gql_ref[...], wq_ref[...], gkvl_ref[...],
                                                                        wkv_ref[...])

        def heads(g):
            rows = g["rows"]
            c, sa, sb = c_ref[rows, :], sa_ref[rows, :], sb_ref[rows, :]
            for h in range(HEADS):
                lo = HEAD_PAD * h
                qh = _rms(g["q_up"][:, lo:lo + HEAD_PAD], gqh_ref[...], QK_HEAD)[0]
                q_ref[h, rows, :] = (_rope(qh, c, sa, sb) * (ATTN_SCALE * LOG2E)).astype(BF16)
                kh = _rope(_rms(g["kv_up"][:, lo:lo + HEAD_PAD] + g["kr"], gkh_ref[...], QK_HEAD)[0], c, sa, sb)
                k_ref[h, rows, :] = kh.astype(BF16)
                kt_ref[h, :, rows] = jnp.transpose(kh).astype(BF16)
                vh = g["kv_up"][:, HEADS * HEAD_PAD + lo:HEADS * HEAD_PAD + lo + HEAD_PAD]
                v_ref[h, rows, :] = vh.astype(BF16)
                vt_ref[h, :, rows] = jnp.transpose(vh).astype(BF16)

        _skewed((project, latents, heads), [dict(rows=pl.ds(part * p, part)) for p in range(tm // part)])

    row = lambda w: pl.BlockSpec((tm, w), lambda i: (i, 0))
    head = pl.BlockSpec((HEADS, tm, HEAD_PAD), lambda i: (0, i, 0))
    head_t = pl.BlockSpec((HEADS, HEAD_PAD, tm), lambda i: (0, 0, i))
    return pl.pallas_call(
        body, grid=(T // tm,), name="mixin_fwd",
        in_specs=[row(D_MODEL)] + [row(HEAD_PAD)] * 3 + [WHOLE] * 8,
        out_specs=[row(Z_WIDTH), head, head, head, head_t, head_t],
        out_shape=[jax.ShapeDtypeStruct((T, Z_WIDTH), F32)] + [jax.ShapeDtypeStruct((HEADS, T, HEAD_PAD), BF16)] * 3
                  + [jax.ShapeDtypeStruct((HEADS, HEAD_PAD, T), BF16)] * 2,
        compiler_params=_cparams(),
    )(x, *rope, gmix, win, gql, wq, gkvl, wkv, gqh, gkh)


def _pair_tables(pairs):
    return (jnp.asarray([p[0] for p in pairs], jnp.int32), jnp.asarray([p[1] for p in pairs], jnp.int32))


def attn_fwd(q, k, vt):
    _, T, _ = q.shape
    tb = min(ATTN_TILE, T)
    sq = min(ATTN_FWD_SUB, tb)
    nb = T // tb

    pairs = [(i, j) for i in range(nb) for j in range(i + 1)]

    def body(i_tab, j_tab, q_ref, k_ref, vt_ref, o_ref, lse_ref, m_s, l_s, acc_s):
        i, j = i_tab[pl.program_id(0)], j_tab[pl.program_id(0)]

        @pl.when(j == 0)
        def _():
            m_s[...] = jnp.full(m_s.shape, -jnp.inf, F32)
            l_s[...] = jnp.zeros(l_s.shape, F32)
            acc_s[...] = jnp.zeros(acc_s.shape, F32)

        def scores(g):
            st = _dot_nt(k_ref[g["h"], g["kb"], :], q_ref[g["h"], g["qa"], :])
            if g["masked"]:
                krow = lax.broadcasted_iota(jnp.int32, st.shape, 0)
                qcol = g["q0"] + lax.broadcasted_iota(jnp.int32, st.shape, 1)
                st = jnp.where(krow <= qcol, st, NEG_BIG)
            g["st"] = st

        def new_max(g):
            h, qa = g["h"], g["qa"]
            m_prev = m_s[h, :, qa]
            g["m"] = jnp.maximum(m_prev, jnp.max(g["st"], axis=0, keepdims=True))
            g["alpha"] = jnp.exp2(m_prev - g["m"])
            m_s[h, :, qa] = g["m"]

        def weights(g):
            pt = jnp.exp2(g["st"] - g["m"])
            g["lsum"] = jnp.sum(pt, axis=0, keepdims=True)
            g["pt"] = pt.astype(BF16)

        def accumulate(g):
            h, qa = g["h"], g["qa"]
            l_s[h, :, qa] = g["alpha"] * l_s[h, :, qa] + g["lsum"]
            acc_s[h, :, qa] = g["alpha"] * acc_s[h, :, qa] + _dot(vt_ref[h, :, g["kb"]], g["pt"])

        def tiles(masked):
            return [dict(h=h, q0=q0, qa=slice(q0, q0 + sq), kb=slice(0, q0 + sq if masked else tb), masked=masked)
                    for h in range(HEADS) for q0 in range(0, tb, sq)]

        @pl.when(j < i)
        def _():
            _skewed((scores, new_max, weights, accumulate), tiles(False))

        @pl.when(j == i)
        def _():
            _skewed((scores, new_max, weights, accumulate), tiles(True))
            for h in range(HEADS):
                l = l_s[h]
                o_ref[:, HEAD_PAD * h:HEAD_PAD * (h + 1)] = jnp.transpose(acc_s[h] / l)
                lse_ref[h] = m_s[h] + jnp.log2(l)

    qspec = pl.BlockSpec((HEADS, tb, HEAD_PAD), lambda p, it, jt: (0, it[p], 0))
    kspec = pl.BlockSpec((HEADS, tb, HEAD_PAD), lambda p, it, jt: (0, jt[p], 0))
    vspec = pl.BlockSpec((HEADS, HEAD_PAD, tb), lambda p, it, jt: (0, 0, jt[p]))
    grid_spec = pltpu.PrefetchScalarGridSpec(
        num_scalar_prefetch=2, grid=(len(pairs),),
        in_specs=[qspec, kspec, vspec],
        out_specs=[pl.BlockSpec((tb, MLA_WIDTH), lambda p, it, jt: (it[p], 0)),
                   pl.BlockSpec((HEADS, 1, tb), lambda p, it, jt: (0, 0, it[p]))],
        scratch_shapes=[pltpu.VMEM((HEADS, 1, tb), F32), pltpu.VMEM((HEADS, 1, tb), F32), pltpu.VMEM((HEADS, HEAD_PAD, tb), F32)])
    return pl.pallas_call(
        body, grid_spec=grid_spec, name="attn_fwd",
        out_shape=[jax.ShapeDtypeStruct((T, MLA_WIDTH), F32), jax.ShapeDtypeStruct((HEADS, 1, T), F32)],
        compiler_params=_cparams(),
    )(*_pair_tables(pairs), q, k, vt)


def _sgu_fwd_chunk(vn_c, wcat, bz, masks):
    vstack = jnp.concatenate([jnp.where(mk, vn_c, 0.0).astype(BF16) for mk in masks], axis=0)
    return _dot(wcat, vstack) + bz


def _pool_counts(i, tm):
    pos1 = (i * tm + 1 + lax.broadcasted_iota(jnp.int32, (tm, POOL_WIDTH), 0)).astype(F32)
    lane = lax.broadcasted_iota(jnp.int32, (tm, POOL_WIDTH), 1)
    win = jnp.where(lane < 64, 2.0, jnp.where(lane < 128, 4.0, jnp.where(lane < 192, 8.0, 16.0)))
    return jnp.minimum(pos1, win), lane


def _by_group(lane, s2, s4, s8, s16):
    return jnp.where(lane < 64, s2, jnp.where(lane < 128, s4, jnp.where(lane < 192, s8, s16)))


def _pool_means(pin, halo, i, tm):
    s1 = jnp.concatenate([halo, pin], axis=0)
    s2 = s1 + pltpu.roll(s1, 1, 0)
    s4 = s2 + pltpu.roll(s2, 2, 0)
    s8 = s4 + pltpu.roll(s4, 4, 0)
    s16 = s8 + pltpu.roll(s8, 8, 0)
    cnt, lane = _pool_counts(i, tm)
    sel = _by_group(lane, s2[POOL_HALO:], s4[POOL_HALO:], s8[POOL_HALO:], s16[POOL_HALO:])
    return sel / cnt - pin


def _mixers_fwd_tile(i, tm, a, u, vs, pin, halo, gsgu, wcat, bz, wp, pscale, goa, gos, gop):
    vn, rv = _rms(vs, gsgu, SGU_WIDTH)
    masks = _head_masks((CHUNK, SGU_WIDTH), SGU_HEAD_DIM)
    zc = jnp.concatenate([_sgu_fwd_chunk(vn[CHUNK * c:CHUNK * (c + 1)], wcat, bz, masks) for c in range(tm // CHUNK)], axis=0)
    gm = u * zc
    halo = jnp.where(i > 0, halo, 0.0)
    m = _pool_means(pin, halo, i, tm).astype(BF16)
    yp_pre = _dot(m, wp)
    yp = yp_pre * pscale
    na, ra = _rms(a, goa, MLA_WIDTH)
    ng, rg = _rms(gm, gos, SGU_WIDTH)
    npo, rp = _rms(yp, gop, POOL_WIDTH)
    mix = jnp.concatenate([na, ng, npo], axis=1).astype(BF16)
    return vn, rv, zc, gm, m, yp_pre, yp, ra, rg, rp, mix


def _z_specs(tm):
    col = lambda c: pl.BlockSpec((tm, 256), lambda i: (i, c))
    halo = pl.BlockSpec((POOL_HALO, 256), lambda i: (jnp.maximum(i * (tm // POOL_HALO) - 1, 0), 4))
    return [col(2), col(3), col(4), halo]


def mixers_fwd(x, a, z, gsgu, wcat, bz, wp, pscale, goa, gos, gop, wout):
    T = x.shape[0]
    tm = min(ROW_TILE, T)

    def body(x_ref, a_ref, u_ref, vs_ref, pin_ref, halo_ref, gsgu_ref, wcat_ref, bz_ref, wp_ref, ps_ref,
             goa_ref, gos_ref, gop_ref, wout_ref, x1_ref):
        i = pl.program_id(0)
        mix = _mixers_fwd_tile(i, tm, a_ref[...], u_ref[...], vs_ref[...], pin_ref[...], halo_ref[...], gsgu_ref[...],
                               wcat_ref[...], bz_ref[...], wp_ref[...], ps_ref[...], goa_ref[...], gos_ref[...],
                               gop_ref[...])[-1]
        x1_ref[...] = x_ref[...] + _dot(mix, wout_ref[...])

    row = lambda w: pl.BlockSpec((tm, w), lambda i: (i, 0))
    return pl.pallas_call(
        body, grid=(T // tm,), name="mixers_fwd",
        in_specs=[row(D_MODEL), row(MLA_WIDTH)] + _z_specs(tm) + [WHOLE] * 9,
        out_specs=row(D_MODEL),
        out_shape=jax.ShapeDtypeStruct((T, D_MODEL), F32),
        compiler_params=_cparams(),
    )(x, a, z, z, z, z, gsgu, wcat, bz, wp, pscale, goa, gos, gop, wout)


def ffn_fwd(x1, gffn, wg, wu, wd, target=None):
    T = x1.shape[0]
    tm = min(ROW_TILE, T)
    with_loss = target is not None

    def body(*refs):
        x1_ref, gffn_ref, wg_ref, wu_ref, wd_ref = refs[:5]
        outs = refs[6:] if with_loss else refs[5:]
        x1v = x1_ref[...]
        h2 = _rms(x1v, gffn_ref[...], D_MODEL)[0].astype(BF16)
        acc = x1v
        for c in range(FFN_HIDDEN // FFN_CHUNK):
            sl = slice(FFN_CHUNK * c, FFN_CHUNK * (c + 1))
            g = _dot_nt(h2, wg_ref[sl, :])
            u = _dot_nt(h2, wu_ref[sl, :])
            outs[-2][:, sl] = g.astype(BF16)
            outs[-1][:, sl] = u.astype(BF16)
            act = (g * _sigmoid(g) * u).astype(BF16)
            acc = acc + _dot(act, wd_ref[sl, :])
        if not with_loss:
            outs[0][...] = acc
            return
        dy_ref, loss_ref = outs[0], outs[1]

        @pl.when(pl.program_id(0) == 0)
        def _():
            loss_ref[...] = jnp.zeros(loss_ref.shape, F32)

        err = acc - refs[5][...]
        dy_ref[...] = err * (1.0 / D_MODEL)
        per_row = jnp.sum(err * err, axis=1, keepdims=True) * (1.0 / D_MODEL)
        loss_ref[...] += 0.5 * jnp.sum(per_row, axis=0, keepdims=True)

    row = lambda w: pl.BlockSpec((tm, w), lambda i: (i, 0))
    hidden = [jax.ShapeDtypeStruct((T, FFN_HIDDEN), BF16)] * 2
    if with_loss:
        return pl.pallas_call(
            body, grid=(T // tm,), name="ffn_fwd_loss",
            in_specs=[row(D_MODEL)] + [WHOLE] * 4 + [row(D_MODEL)],
            out_specs=[row(D_MODEL), pl.BlockSpec((1, 1), lambda i: (0, 0)), row(FFN_HIDDEN), row(FFN_HIDDEN)],
            out_shape=[jax.ShapeDtypeStruct((T, D_MODEL), F32), jax.ShapeDtypeStruct((1, 1), F32)] + hidden,
            compiler_params=_cparams(),
        )(x1, gffn, wg, wu, wd, target)
    return pl.pallas_call(
        body, grid=(T // tm,), name="ffn_fwd",
        in_specs=[row(D_MODEL)] + [WHOLE] * 4,
        out_specs=[row(D_MODEL), row(FFN_HIDDEN), row(FFN_HIDDEN)],
        out_shape=[jax.ShapeDtypeStruct((T, D_MODEL), F32)] + hidden,
        compiler_params=_cparams(),
    )(x1, gffn, wg, wu, wd)


def _acc_spec(shape):
    return pl.BlockSpec(shape, lambda i: (0,) * len(shape))


def ffn_bwd(dx2, x1, gs, us, gffn, wg, wu, wd):
    T = x1.shape[0]
    tm = min(ROW_TILE // 2, T)

    def body(dx2_ref, x1_ref, gs_ref, us_ref, gffn_ref, wg_ref, wu_ref, wd_ref,
             dx1_ref, h2_ref, act_ref, dg_ref, du_ref, dgffn_ref):
        @pl.when(pl.program_id(0) == 0)
        def _():
            dgffn_ref[...] = jnp.zeros(dgffn_ref.shape, F32)

        dx2v = dx2_ref[...]
        dy = dx2v.astype(BF16)
        x1v = x1_ref[...]
        h2, r = _rms(x1v, gffn_ref[...], D_MODEL)
        h2_ref[...] = h2.astype(BF16)
        for c in range(FFN_HIDDEN // FFN_CHUNK):
            sl = slice(FFN_CHUNK * c, FFN_CHUNK * (c + 1))
            g = gs_ref[:, sl].astype(F32)
            u = us_ref[:, sl].astype(F32)
            dact = _dot_nt(dy, wd_ref[sl, :])
            sg = _sigmoid(g)
            silu = g * sg
            act_ref[:, sl] = (silu * u).astype(BF16)
            dg_ref[:, sl] = (dact * u * (sg * (1.0 + g * (1.0 - sg)))).astype(BF16)
            du_ref[:, sl] = (dact * silu).astype(BF16)
        dh2 = _dot(dg_ref[...], wg_ref[...]) + _dot(du_ref[...], wu_ref[...])
        dxn, dgn = _rms_bwd(x1v, r, gffn_ref[...], dh2, D_MODEL)
        dx1_ref[...] = dx2v + dxn
        dgffn_ref[...] += dgn

    row = lambda w: pl.BlockSpec((tm, w), lambda i: (i, 0))
    return pl.pallas_call(
        body, grid=(T // tm,), name="ffn_bwd",
        in_specs=[row(D_MODEL), row(D_MODEL), row(FFN_HIDDEN), row(FFN_HIDDEN)] + [WHOLE] * 4,
        out_specs=[row(D_MODEL), row(D_MODEL), row(FFN_HIDDEN), row(FFN_HIDDEN), row(FFN_HIDDEN), _acc_spec((1, D_MODEL))],
        out_shape=[jax.ShapeDtypeStruct((T, D_MODEL), F32), jax.ShapeDtypeStruct((T, D_MODEL), BF16),
                   jax.ShapeDtypeStruct((T, FFN_HIDDEN), BF16), jax.ShapeDtypeStruct((T, FFN_HIDDEN), BF16),
                   jax.ShapeDtypeStruct((T, FFN_HIDDEN), BF16), jax.ShapeDtypeStruct((1, D_MODEL), F32)],
        compiler_params=_cparams(),
    )(dx2, x1, gs, us, gffn, wg, wu, wd)


TN_K_TILE = 2048
TN_ACC_BYTES = 6 * 1024 * 1024


def matmul_tn(a, b, name):
    T, M = a.shape
    N = b.shape[1]
    tk = min(TN_K_TILE, T)
    tm = M if M <= 1024 else M // 2
    tn = max(d for d in range(128, N + 1, 128) if N % d == 0 and tm * d * 4 <= TN_ACC_BYTES)
    nk = T // tk

    def body(a_ref, b_ref, o_ref, acc):
        k = pl.program_id(2)

        @pl.when(k == 0)
        def _():
            acc[...] = jnp.zeros(acc.shape, F32)

        acc[...] += _dot_tn(a_ref[...].astype(BF16), b_ref[...].astype(BF16))

        @pl.when(k == nk - 1)
        def _():
            o_ref[...] = acc[...].astype(BF16)

    return pl.pallas_call(
        body, grid=(M // tm, N // tn, nk), name=name,
        in_specs=[pl.BlockSpec((tk, tm), lambda i, j, k: (k, i)), pl.BlockSpec((tk, tn), lambda i, j, k: (k, j))],
        out_specs=pl.BlockSpec((tm, tn), lambda i, j, k: (i, j)),
        out_shape=jax.ShapeDtypeStruct((M, N), BF16),
        scratch_shapes=[pltpu.VMEM((tm, tn), F32)],
        compiler_params=_cparams(),
    )(a, b)


def mixers_bwd(dx1, a, z, gsgu, wcat, wcat_t, bz, wp, pscale, goa, gos, gop, wout):
    T = a.shape[0]
    tm = min(ROW_TILE, T)

    def body(dx1_ref, a_ref, u_ref, vs_ref, pin_ref, halo_ref, gsgu_ref, wcat_ref, wcatt_ref, bz_ref, wp_ref,
             ps_ref, goa_ref, gos_ref, gop_ref, wout_ref,
             da_ref, delta_ref, du_ref, dvs_ref, dm_ref, mix_ref,
             dgoa_ref, dgos_ref, dgop_ref, dps_ref, dwp_ref, dwsp_ref, db_ref, dgsgu_ref):
        i = pl.program_id(0)

        @pl.when(i == 0)
        def _():
            for r in (dgoa_ref, dgos_ref, dgop_ref, dps_ref, dwp_ref, dwsp_ref, db_ref, dgsgu_ref):
                r[...] = jnp.zeros(r.shape, F32)

        a_v, u, vs = a_ref[...], u_ref[...], vs_ref[...]
        goa, gos, gop, pscale_v = goa_ref[...], gos_ref[...], gop_ref[...], ps_ref[...]
        vn, rv, zc, gm, m, yp_pre, yp, ra, rg, rp, mix = _mixers_fwd_tile(
            i, tm, a_v, u, vs, pin_ref[...], halo_ref[...], gsgu_ref[...], wcat_ref[...], bz_ref[...], wp_ref[...],
            pscale_v, goa, gos, gop)
        mix_ref[...] = mix
        dmix = _dot_nt(dx1_ref[...].astype(BF16), wout_ref[...])
        da, dgoa = _rms_bwd(a_v, ra, goa, dmix[:, :MLA_WIDTH], MLA_WIDTH)
        dgm, dgos = _rms_bwd(gm, rg, gos, dmix[:, MLA_WIDTH:MLA_WIDTH + SGU_WIDTH], SGU_WIDTH)
        dyp, dgop = _rms_bwd(yp, rp, gop, dmix[:, MLA_WIDTH + SGU_WIDTH:], POOL_WIDTH)
        da_ref[...] = da
        dgoa_ref[...] += dgoa
        dgos_ref[...] += dgos
        dgop_ref[...] += dgop
        prod = da * a_v
        ones = jnp.ones((8, HEAD_PAD), F32)
        for h in range(HEADS):
            lo = HEAD_PAD * h
            sums = lax.dot_general(ones, prod[:, lo:lo + HEAD_PAD], (((1,), (1,)), ((), ())), preferred_element_type=F32,
                                   precision=lax.Precision.HIGHEST)
            delta_ref[h] = sums[0:1, :]
        dps_ref[...] += jnp.sum(dyp * yp_pre, axis=0, keepdims=True)
        dyp_pre = (dyp * pscale_v).astype(BF16)
        dwp_ref[...] += _dot_tn(m, dyp_pre)
        dm_ref[...] = _dot_nt(dyp_pre, wp_ref[...])
        du_ref[...] = dgm * zc
        dzc = dgm * u
        masks = _head_masks((CHUNK, SGU_WIDTH), SGU_HEAD_DIM)
        lane_b = lax.broadcasted_iota(jnp.int32, (CHUNK, HEAD_PAD), 1)
        dvn_parts = []
        dwsp = jnp.zeros(dwsp_ref.shape, F32)
        db = jnp.zeros(db_ref.shape, F32)
        for c in range(tm // CHUNK):
            dz_c = dzc[CHUNK * c:CHUNK * (c + 1)]
            dzstack = jnp.concatenate([jnp.where(mk, dz_c, 0.0).astype(BF16) for mk in masks], axis=0)
            dvn_parts.append(_dot(wcatt_ref[...], dzstack))
            dwsp = dwsp + _dot_nt(dzstack, vn[CHUNK * c:CHUNK * (c + 1)].astype(BF16))
            for h, mk in enumerate(masks):
                col = jnp.sum(jnp.where(mk, dz_c, 0.0), axis=1, keepdims=True)
                db = db + jnp.where(lane_b == h, col, 0.0)
        dwsp_ref[...] += dwsp
        db_ref[...] += db
        dvs, dgsgu = _rms_bwd(vs, rv, gsgu_ref[...], jnp.concatenate(dvn_parts, axis=0), SGU_WIDTH)
        dvs_ref[...] = dvs
        dgsgu_ref[...] += dgsgu

    row = lambda w: pl.BlockSpec((tm, w), lambda i: (i, 0))
    head = pl.BlockSpec((HEADS, 1, tm), lambda i: (0, 0, i))
    acc_shapes = [(1, MLA_WIDTH), (1, SGU_WIDTH), (1, POOL_WIDTH), (1, POOL_WIDTH), (POOL_WIDTH, POOL_WIDTH),
                  (HEADS * CHUNK, CHUNK), (CHUNK, HEAD_PAD), (1, SGU_WIDTH)]
    return pl.pallas_call(
        body, grid=(T // tm,), name="mixers_bwd",
        in_specs=[row(D_MODEL), row(MLA_WIDTH)] + _z_specs(tm) + [WHOLE] * 10,
        out_specs=[row(MLA_WIDTH), head, row(256), row(256), row(256), row(D_MODEL)] + [_acc_spec(s) for s in acc_shapes],
        out_shape=[jax.ShapeDtypeStruct((T, MLA_WIDTH), F32), jax.ShapeDtypeStruct((HEADS, 1, T), F32),
                   jax.ShapeDtypeStruct((T, 256), F32), jax.ShapeDtypeStruct((T, 256), F32),
                   jax.ShapeDtypeStruct((T, 256), F32), jax.ShapeDtypeStruct((T, D_MODEL), BF16)]
                  + [jax.ShapeDtypeStruct(s, F32) for s in acc_shapes],
        compiler_params=_cparams(),
    )(dx1, a, z, z, z, z, gsgu, wcat, wcat_t, bz, wp, pscale, goa, gos, gop, wout)


def pool_bwd(dm):
    T = dm.shape[0]
    tm = min(4 * ROW_TILE, T)
    nt = T // tm

    def body(dm_ref, next_ref, dpin_ref):
        i = pl.program_id(0)
        cnt, lane = _pool_counts(i, tm)
        dmv = dm_ref[...]
        win = _by_group(lane[:POOL_HALO], 2.0, 4.0, 8.0, 16.0)
        nxt = jnp.where(i < nt - 1, next_ref[...] / win, 0.0)
        r1 = jnp.concatenate([dmv / cnt, nxt], axis=0)
        n = tm + POOL_HALO
        r2 = r1 + pltpu.roll(r1, n - 1, 0)
        r4 = r2 + pltpu.roll(r2, n - 2, 0)
        r8 = r4 + pltpu.roll(r4, n - 4, 0)
        r16 = r8 + pltpu.roll(r8, n - 8, 0)
        dpin_ref[...] = _by_group(lane, r2[:tm], r4[:tm], r8[:tm], r16[:tm]) - dmv

    return pl.pallas_call(
        body, grid=(nt,), name="pool_bwd",
        in_specs=[pl.BlockSpec((tm, 256), lambda i: (i, 0)),
                  pl.BlockSpec((POOL_HALO, 256), lambda i: (jnp.minimum((i + 1) * (tm // POOL_HALO), T // POOL_HALO - 1), 0))],
        out_specs=pl.BlockSpec((tm, 256), lambda i: (i, 0)),
        out_shape=jax.ShapeDtypeStruct((T, 256), F32),
        compiler_params=_cparams(),
    )(dm, dm)


def attn_bwd(q, k, kt, v, do, lse, delta):
    _, T, _ = q.shape
    tb = min(ATTN_TILE, T)
    sb = min(ATTN_SUB, tb)
    ns = tb // sb
    nb = T // tb
    hb = ATTN_BWD_HEADS

    pairs = [(i, j) for i in range(nb) for j in range(i, nb)]

    def body(i_tab, j_tab, q_ref, k_ref, kt_ref, v_ref, do_ref, lse_ref, delta_ref, dqt_ref, dk_ref, dv_ref, dk_s, dv_s):
        i, j = i_tab[pl.program_id(1)], j_tab[pl.program_id(1)]

        @pl.when(pl.program_id(1) == 0)
        def _():
            dqt_ref[...] = jnp.zeros(dqt_ref.shape, F32)

        @pl.when(j == i)
        def _():
            dk_s[...] = jnp.zeros(dk_s.shape, F32)
            dv_s[...] = jnp.zeros(dv_s.shape, F32)

        def sub_block(h, a, b, masked):
            qa = slice(sb * a, sb * (a + 1))
            kb = slice(sb * b, sb * (b + 1))
            qv = q_ref[h, qa, :]
            dov = do_ref[qa, HEAD_PAD * h:HEAD_PAD * (h + 1)].astype(BF16)
            st = _dot_nt(k_ref[h, kb, :], qv)
            dpt = _dot_nt(v_ref[h, kb, :], dov)
            pt = jnp.exp2(st - lse_ref[h, :, qa])
            if masked:
                krow = lax.broadcasted_iota(jnp.int32, st.shape, 0)
                qcol = lax.broadcasted_iota(jnp.int32, st.shape, 1)
                pt = jnp.where(krow <= qcol, pt, 0.0)
            dst = (pt * (dpt - delta_ref[h, :, qa])).astype(BF16)
            dv_s[h, kb, :] += _dot(pt.astype(BF16), dov)
            dk_s[h, kb, :] += _dot(dst, qv)
            cols = pl.ds(pl.multiple_of(j * tb + sb * a, sb), sb)
            dqt_ref[h, :, cols] += _dot(kt_ref[h, :, kb], dst)

        @pl.when(j > i)
        def _():
            for a in range(ns):
                for h in range(hb):
                    for b in range(ns):
                        sub_block(h, a, b, False)

        @pl.when(j == i)
        def _():
            for a in range(ns):
                for h in range(hb):
                    for b in range(a + 1):
                        sub_block(h, a, b, a == b)

        @pl.when(j == nb - 1)
        def _():
            dk_ref[...] = dk_s[...] * (1.0 / LOG2E)
            dv_ref[...] = dv_s[...]

    qspec = pl.BlockSpec((hb, tb, HEAD_PAD), lambda g, p, it, jt: (g, jt[p], 0))
    kspec = pl.BlockSpec((hb, tb, HEAD_PAD), lambda g, p, it, jt: (g, it[p], 0))
    ktspec = pl.BlockSpec((hb, HEAD_PAD, tb), lambda g, p, it, jt: (g, 0, it[p]))
    rowspec = pl.BlockSpec((hb, 1, tb), lambda g, p, it, jt: (g, 0, jt[p]))
    grid_spec = pltpu.PrefetchScalarGridSpec(
        num_scalar_prefetch=2, grid=(HEADS // hb, len(pairs)),
        in_specs=[qspec, kspec, ktspec, kspec, pl.BlockSpec((tb, hb * HEAD_PAD), lambda g, p, it, jt: (jt[p], g)), rowspec, rowspec],
        out_specs=[pl.BlockSpec((hb, HEAD_PAD, T), lambda g, p, it, jt: (g, 0, 0)), kspec, kspec],
        scratch_shapes=[pltpu.VMEM((hb, tb, HEAD_PAD), F32), pltpu.VMEM((hb, tb, HEAD_PAD), F32)])
    return pl.pallas_call(
        body, grid_spec=grid_spec, name="attn_bwd",
        out_shape=[jax.ShapeDtypeStruct((HEADS, HEAD_PAD, T), F32)] + [jax.ShapeDtypeStruct((HEADS, T, HEAD_PAD), F32)] * 2,
        compiler_params=_cparams(),
    )(*_pair_tables(pairs), q, k, kt, v, do, lse, delta)


def mixin_bwd(dres, x, z, rope, dqt, dk, dv, du, dvs, dpin, gmix, win, gql, wq, gkvl, wkv, gqh, gkh):
    T = x.shape[0]
    tm = min(ROW_TILE, T)
    part = min(MIXIN_BWD_PART, tm)

    def body(dres_ref, x_ref, z_ref, c_ref, sa_ref, sb_ref, dqt_ref, dk_ref, dv_ref, du_ref, dvs_ref, dpin_ref,
             gmix_ref, win_ref, gql_ref, wq_ref, gkvl_ref, wkv_ref, gqh_ref, gkh_ref,
             dx_ref, hn_ref, dz_ref, qn_ref, dqup_ref, kvn_ref, dkvup_ref,
             dgmix_ref, dgql_ref, dgkvl_ref, dgqh_ref, dgkh_ref):
        @pl.when(pl.program_id(0) == 0)
        def _():
            for r in (dgmix_ref, dgql_ref, dgkvl_ref, dgqh_ref, dgkh_ref):
                r[...] = jnp.zeros(r.shape, F32)

        lane = lax.broadcasted_iota(jnp.int32, (part, HEAD_PAD), 1)
        rope_lanes = (lane >= QK_NOPE) & (lane < QK_HEAD)

        def recompute(g):
            rows = g["rows"]
            g["xv"] = x_ref[rows, :]
            hn, g["rx"] = _rms(g["xv"], gmix_ref[...], D_MODEL)
            hn_ref[rows, :] = hn.astype(BF16)
            g["ql"], g["kvl"], g["kr"], qn, g["rq"], kvn, g["rkv"], g["q_up"], g["kv_up"] = _qkv_pre(
                z_ref[rows, :], gql_ref[...], wq_ref[...], gkvl_ref[...], wkv_ref[...])
            qn_ref[rows, :] = qn
            kvn_ref[rows, :] = kvn

        def heads(g):
            rows = g["rows"]
            c, sa, sb = c_ref[rows, :], sa_ref[rows, :], sb_ref[rows, :]
            dkr = jnp.zeros((part, HEAD_PAD), F32)
            dgqh = jnp.zeros((1, HEAD_PAD), F32)
            dgkh = jnp.zeros((1, HEAD_PAD), F32)
            dq_parts, dk_parts, dv_parts = [], [], []
            for h in range(HEADS):
                lo = HEAD_PAD * h
                qh = g["q_up"][:, lo:lo + HEAD_PAD]
                rqh = lax.rsqrt(jnp.sum(qh * qh, axis=-1, keepdims=True) * (1.0 / QK_HEAD) + EPS)
                dq_h = jnp.transpose(dqt_ref[h, :, rows]) * ATTN_SCALE
                dqh, dg = _rms_bwd(qh, rqh, gqh_ref[...], _rope_bwd(dq_h, c, sa, sb), QK_HEAD)
                dgqh = dgqh + dg
                dq_parts.append(dqh)
                kh = g["kv_up"][:, lo:lo + HEAD_PAD] + g["kr"]
                rkh = lax.rsqrt(jnp.sum(kh * kh, axis=-1, keepdims=True) * (1.0 / QK_HEAD) + EPS)
                dkh, dg = _rms_bwd(kh, rkh, gkh_ref[...], _rope_bwd(dk_ref[h, rows, :], c, sa, sb), QK_HEAD)
                dgkh = dgkh + dg
                dkr = dkr + jnp.where(rope_lanes, dkh, 0.0)
                dk_parts.append(dkh)
                dv_parts.append(dv_ref[h, rows, :])
            dgqh_ref[...] += dgqh
            dgkh_ref[...] += dgkh
            g["dkr"] = dkr
            g["dq_up"] = jnp.concatenate(dq_parts, axis=1).astype(BF16)
            g["dkv_up"] = jnp.concatenate(dk_parts + dv_parts, axis=1).astype(BF16)
            dqup_ref[rows, :] = g["dq_up"]
            dkvup_ref[rows, :] = g["dkv_up"]

        def latents(g):
            rows = g["rows"]
            dql, dg = _rms_bwd(g["ql"], g["rq"], gql_ref[...], _dot(g["dq_up"], wq_ref[...]), Q_LORA)
            dgql_ref[...] += dg
            dkvl, dg = _rms_bwd(g["kvl"], g["rkv"], gkvl_ref[...], _dot(g["dkv_up"], wkv_ref[...]), KV_LORA)
            dgkvl_ref[...] += dg
            g["dz"] = jnp.concatenate([dql, dkvl, g["dkr"], du_ref[rows, :], dvs_ref[rows, :], dpin_ref[rows, :]],
                                      axis=1).astype(BF16)
            dz_ref[rows, :] = g["dz"]

        def inputs(g):
            rows = g["rows"]
            dxn, dg = _rms_bwd(g["xv"], g["rx"], gmix_ref[...], _dot(g["dz"], win_ref[...]), D_MODEL)
            dgmix_ref[...] += dg
            dx_ref[rows, :] = dres_ref[rows, :] + dxn

        _skewed((recompute, heads, latents, inputs), [dict(rows=pl.ds(part * p, part)) for p in range(tm // part)])

    row = lambda w: pl.BlockSpec((tm, w), lambda i: (i, 0))
    head = pl.BlockSpec((HEADS, tm, HEAD_PAD), lambda i: (0, i, 0))
    head_t = pl.BlockSpec((HEADS, HEAD_PAD, tm), lambda i: (0, 0, i))
    acc_shapes = [(1, D_MODEL), (1, Q_LORA), (1, KV_LORA), (1, HEAD_PAD), (1, HEAD_PAD)]
    out_rows = [(D_MODEL, F32), (D_MODEL, BF16), (Z_WIDTH, BF16), (Q_LORA, BF16), (HEADS * HEAD_PAD, BF16),
                (KV_LORA, BF16), (2 * HEADS * HEAD_PAD, BF16)]
    return pl.pallas_call(
        body, grid=(T // tm,), name="mixin_bwd",
        in_specs=[row(D_MODEL), row(D_MODEL), row(Z_WIDTH)] + [row(HEAD_PAD)] * 3 + [head_t, head, head, row(256), row(256), row(256)]
                 + [WHOLE] * 8,
        out_specs=[row(w) for w, _ in out_rows] + [_acc_spec(s) for s in acc_shapes],
        out_shape=[jax.ShapeDtypeStruct((T, w), dt) for w, dt in out_rows] + [jax.ShapeDtypeStruct(s, F32) for s in acc_shapes],
        compiler_params=_cparams(),
    )(dres, x, z, *rope, dqt, dk, dv, du, dvs, dpin, gmix, win, gql, wq, gkvl, wkv, gqh, gkh)


def _place():
    x, y, c = lax.axis_index("x"), lax.axis_index("y"), lax.axis_index("c")
    return x, y, c, 4 * x + 2 * y + c


def _layer_of(ref, l):
    return ref[:, l, :] if ref.shape[1] == DEPTH else ref[l]


def _layer_shape(shard):
    return (shard.shape[0], shard.shape[2]) if shard.shape[1] == DEPTH else shard.shape[1:]


def cast_shards(shards, wanted):
    n = len(shards)

    def body(*refs):
        for o_ref, (w, l) in zip(refs[n:], wanted):
            o_ref[...] = _layer_of(refs[w], l).astype(BF16)

    return pl.pallas_call(
        body, name="cast_shards", in_specs=[WHOLE] * n, out_specs=[WHOLE] * len(wanted),
        out_shape=[jax.ShapeDtypeStruct(_layer_shape(shards[w]), BF16) for w, _ in wanted],
        compiler_params=_cparams(),
    )(*shards)


def _peer(k):
    x, y, c, _ = _place()
    px = 1 - x if k & 4 else x
    py = 1 - y if k & 2 else y
    pc = 1 - c if k & 1 else c
    return (px, py, pc), 4 * px + 2 * py + pc


def exchange_start(srcs, after, gather, name):
    n = len(srcs)
    land_shapes = [((N_DEV,) + s.shape) if gather else s.shape for s in srcs]

    def body(*refs):
        src_refs, land_refs = refs[:n], refs[n:2 * n]
        send_sems, recv_sems = refs[2 * n + 1:3 * n + 1], refs[3 * n + 1:4 * n + 1]
        token = refs[-1]
        _, _, _, me = _place()
        for k in range(1, N_DEV):
            peer, peer_id = _peer(k)
            for w in range(n):
                pltpu.make_async_remote_copy(
                    src_ref=src_refs[w] if gather else src_refs[w].at[peer_id], dst_ref=land_refs[w].at[me],
                    send_sem=send_sems[w], recv_sem=recv_sems[w], device_id=peer, device_id_type=MESH).start()
        for w in range(n):
            pltpu.make_async_copy(src_refs[w] if gather else src_refs[w].at[me], land_refs[w].at[me], recv_sems[w]).start()
        token[...] = jnp.zeros(token.shape, F32)

    hbm = lambda a: pltpu.with_memory_space_constraint(a, pltpu.HBM)
    outs = pl.pallas_call(
        body, name=name,
        out_shape=(pltpu.SemaphoreType.DMA(()),) * (2 * n)
                  + tuple(pltpu.HBM(s.shape, s.dtype) for s in srcs)
                  + tuple(pltpu.HBM(l, s.dtype) for l, s in zip(land_shapes, srcs))
                  + (jax.ShapeDtypeStruct((8, 128), F32),),
        in_specs=[HBM_SPEC] * (2 * n) + [ANY],
        out_specs=(SEM_SPEC,) * (2 * n) + (HBM_SPEC,) * (2 * n) + (WHOLE,),
        input_output_aliases={i: 2 * n + i for i in range(2 * n)},
        compiler_params=pltpu.CompilerParams(has_side_effects=pltpu.SideEffectType.DATAFLOW_SIDE_EFFECTING),
    )(*[hbm(s) for s in srcs], *[hbm(lax.empty(l, s.dtype)) for l, s in zip(land_shapes, srcs)], after)
    return list(outs[:n]), list(outs[n:2 * n]), list(outs[2 * n:3 * n]), list(outs[3 * n:4 * n]), outs[-1]


def exchange_wait(started, after, name):
    send_sems, recv_sems, srcs, lands, _ = started
    n = len(srcs)

    def body(*refs):
        land_refs = refs[n:2 * n]
        send_sems, recv_sems = refs[2 * n:3 * n], refs[3 * n:4 * n]
        x, y, c, _ = _place()
        for w in range(n):
            seven, eight = land_refs[w].at[pl.ds(0, N_DEV - 1)], land_refs[w]
            pltpu.make_async_remote_copy(src_ref=seven, dst_ref=seven, send_sem=send_sems[w], recv_sem=recv_sems[w],
                                         device_id=(x, y, c), device_id_type=MESH).wait_send()
            pltpu.make_async_remote_copy(src_ref=eight, dst_ref=eight, send_sem=send_sems[w], recv_sem=recv_sems[w],
                                         device_id=(x, y, c), device_id_type=MESH).wait_recv()

    outs = pl.pallas_call(
        body, name=name,
        out_shape=tuple(pltpu.HBM(a.shape, a.dtype) for a in (*srcs, *lands)),
        in_specs=[HBM_SPEC] * (2 * n) + [SEM_SPEC] * (2 * n) + [ANY],
        out_specs=(HBM_SPEC,) * (2 * n),
        input_output_aliases={i: i for i in range(2 * n)},
        compiler_params=pltpu.CompilerParams(has_side_effects=pltpu.SideEffectType.DATAFLOW_SIDE_EFFECTING),
    )(*srcs, *lands, *send_sems, *recv_sems, after)
    return list(outs[n:])


def _adamw(w, g, m, v):
    m2 = ADAM_B1 * m + (1.0 - ADAM_B1) * g
    v2 = ADAM_B2 * v + (1.0 - ADAM_B2) * (g * g)
    m_hat = m2 / (1.0 - ADAM_B1 ** ADAM_STEP)
    v_hat = v2 / (1.0 - ADAM_B2 ** ADAM_STEP)
    delta = -ADAM_LR * (m_hat / (jnp.sqrt(v_hat) + ADAM_EPS) + ADAM_WD * w)
    return delta, m2, v2


def adamw_sharded(parts, w, m, v, name):
    L, R, C = w.shape
    fits = [d for d in range(16, min(R, 512) + 1, 16) if R % d == 0]
    br = max(fits) if fits else R
    nblk = R // br

    def body(*refs):
        p_refs = refs[:L]
        w_ref, m_ref, v_ref, g_ref, d_ref, m2_ref, v2_ref = refs[L:]

        def total(p_ref):
            g = p_ref[0].astype(F32)
            for s in range(1, N_DEV):
                g = g + p_ref[s].astype(F32)
            return g

        g = total(p_refs[0])
        for l in range(1, L):
            g = jnp.where(pl.program_id(0) == l, total(p_refs[l]), g)
        g_ref[...] = g
        d_ref[...], m2_ref[...], v2_ref[...] = _adamw(w_ref[...], g, m_ref[...], v_ref[...])

    blk = pl.BlockSpec((None, br, C), lambda l, i: (l, i, 0))

    def part_spec(k):
        return pl.BlockSpec((N_DEV, br, C), lambda l, i: (0, jnp.where(l == k, i, jnp.where(l < k, 0, nblk - 1)), 0))

    return pl.pallas_call(
        body, grid=(L, nblk), name=name,
        in_specs=[part_spec(k) for k in range(L)] + [blk, blk, blk],
        out_specs=[blk] * 4,
        out_shape=[jax.ShapeDtypeStruct((L, R, C), F32)] * 4,
        compiler_params=_cparams(),
    )(*parts, w, m, v)


def adamw_small(gparts, w, m, v):
    R = gparts.shape[1]

    def body(g_ref, w_ref, m_ref, v_ref, grad_ref, d_ref, m2_ref, v2_ref):
        g = g_ref[0]
        for s in range(1, N_DEV):
            g = g + g_ref[s]
        grad_ref[...] = g
        d_ref[...], m2_ref[...], v2_ref[...] = _adamw(w_ref[...], g, m_ref[...], v_ref[...])

    return pl.pallas_call(
        body, name="adamw_small",
        in_specs=[WHOLE] * 4, out_specs=[WHOLE] * 4,
        out_shape=[jax.ShapeDtypeStruct((R, 128), F32)] * 4,
        compiler_params=_cparams(),
    )(gparts, w, m, v)


def _shard_view(name, a):
    if name == "w_in":
        return a.transpose(2, 0, 1)
    return a.swapaxes(1, 2) if name in TRANSPOSED else a


def _shard_unview(name, a):
    if name == "w_in":
        return a.transpose(1, 2, 0)
    return a.swapaxes(1, 2) if name in TRANSPOSED else a


def _pad_head_rows(w, width):
    c = w.shape[1]
    return jnp.pad(w.reshape(HEADS, width, c), ((0, 0), (0, HEAD_PAD - width), (0, 0))).reshape(HEADS * HEAD_PAD, c)


def _unpad_head_rows(w, width):
    c = w.shape[1]
    return w.reshape(HEADS, HEAD_PAD, c)[:, :width].reshape(HEADS * width, c)


O1 = Q_LORA
O2 = O1 + KV_LORA
O3 = O2 + QK_ROPE


def _mixer_weights(gw):
    w_in = gw["w_in"].reshape(IN_WIDTH, D_MODEL)
    zero = lambda n: jnp.zeros((n, D_MODEL), BF16)
    win = jnp.concatenate([w_in[:O2], zero(QK_NOPE), w_in[O2:O3], zero(HEAD_PAD - QK_HEAD), w_in[O3:]], axis=0)
    wq = _pad_head_rows(gw["w_q_up"].reshape(HEADS * QK_HEAD, Q_LORA), QK_HEAD)
    w_kv = gw["w_kv_up"].reshape(HEADS, QK_NOPE + V_HEAD, KV_LORA)
    wk = jnp.pad(w_kv[:, :QK_NOPE], ((0, 0), (0, HEAD_PAD - QK_NOPE), (0, 0))).reshape(HEADS * HEAD_PAD, KV_LORA)
    wv = w_kv[:, QK_NOPE:].reshape(HEADS * V_HEAD, KV_LORA)
    wkv = jnp.concatenate([wk, wv], axis=0)
    out = dict(win=win, wq=wq, wkv=wkv)
    if "w_out" in gw:
        out["wout"] = gw["w_out"].reshape(D_MODEL, D_MODEL)
    return out


def _ffn_weights(gw):
    return dict(wg=gw["w_gate"].reshape(FFN_HIDDEN, D_MODEL), wu=gw["w_up"].reshape(FFN_HIDDEN, D_MODEL),
                wd=gw["w_down"].reshape(FFN_HIDDEN, D_MODEL))


def _layer_small(p, l):
    row = lambda a: a.reshape(1, -1)
    pad_head = lambda g: jnp.pad(g, (0, HEAD_PAD - QK_HEAD)).reshape(1, HEAD_PAD)
    tril = jnp.tril(jnp.ones((CHUNK, CHUNK), F32))
    wsp = p["w_spatial"][l] * tril
    wcat = jnp.concatenate([wsp[h] for h in range(HEADS)], axis=1).astype(BF16)
    wcat_t = jnp.concatenate([wsp[h].T for h in range(HEADS)], axis=1).astype(BF16)
    bz = jnp.repeat(p["b_spatial"][l].T, SGU_HEAD_DIM, axis=1)
    wp = jax.scipy.linalg.block_diag(*[p["w_pool"][l][g] for g in range(HEADS)]).astype(BF16)
    return dict(gmix=row(p["g_mix_norm"][l]), gql=row(p["g_q_lat"][l]), gkvl=row(p["g_kv_lat"][l]),
                gqh=pad_head(p["g_q_head"][l]), gkh=pad_head(p["g_k_head"][l]), gsgu=row(p["g_sgu_v"][l]),
                wcat=wcat, wcat_t=wcat_t, bz=bz, wp=wp, pscale=row(p["pool_scale"][l]),
                goa=row(p["g_out_mla"][l]), gos=row(p["g_out_sgu"][l]), gop=row(p["g_out_pool"][l]),
                gffn=row(p["g_ffn_norm"][l]))


MIXER_SIDE = ("w_in", "w_q_up", "w_kv_up", "w_out")
FFN_SIDE = ("w_gate", "w_up", "w_down")
TRANSPOSED = ("w_in", "w_q_up", "w_kv_up", "w_gate", "w_up")
FIRST_0 = ("w_in", "w_q_up", "w_kv_up")
LATER_0 = ("w_out",) + FFN_SIDE
SHARDED = MIXER_SIDE + FFN_SIDE
SMALL = ("g_mix_norm", "g_q_lat", "g_kv_lat", "g_q_head", "g_k_head", "g_sgu_v", "w_spatial", "b_spatial", "w_pool",
         "pool_scale", "g_out_mla", "g_out_sgu", "g_out_pool", "g_ffn_norm")
WEIGHTS = ("g_mix_norm", "w_in", "g_q_lat", "w_q_up", "g_kv_lat", "w_kv_up", "g_q_head", "g_k_head", "g_sgu_v", "w_spatial",
           "b_spatial", "w_pool", "pool_scale", "g_out_mla", "g_out_sgu", "g_out_pool", "w_out", "g_ffn_norm", "w_gate",
           "w_up", "w_down")
PACKED = SMALL + ("loss",)
PACK_ROWS = 8 * 128


def _pack_small(parts):
    flat = []
    for name in PACKED:
        a = parts[name].reshape(-1)
        flat.append(jnp.pad(a, (0, -a.shape[0] % PACK_ROWS)))
    return jnp.concatenate(flat).reshape(-1, 128)


def _unpack_small(packed, like):
    out, row = {}, 0
    for name in PACKED:
        n = math.prod(like[name].shape)
        rows = -(-n // PACK_ROWS) * 8
        out[name] = packed[row:row + rows].reshape(-1)[:n].reshape(like[name].shape)
        row += rows
    return out


def _forward_attention(x, rope, W, S):
    z, q, k, v, kt, vt = mixin_fwd(x, rope, S["gmix"], W["win"], S["gql"], W["wq"], S["gkvl"], W["wkv"], S["gqh"], S["gkh"])
    a, lse = attn_fwd(q, k, vt)
    return dict(x=x, z=z, q=q, k=k, kt=kt, v=v, a=a, lse=lse)


def _forward_mixers(A, wout, S):
    A["x1"] = mixers_fwd(A["x"], A["a"], A["z"], S["gsgu"], S["wcat"], S["bz"], S["wp"], S["pscale"], S["goa"], S["gos"],
                         S["gop"], wout)
    return A["x1"]


def _backward_ffn(dx2, W, S, A, l):
    dx1, h2, act, dg, du_ffn, dgffn = ffn_bwd(dx2, A["x1"], A["gs"], A["us"], S["gffn"], W["wg"], W["wu"], W["wd"])
    d_wd = matmul_tn(act, dx2, f"dw_down_{l}")
    d_wg = matmul_tn(dg, h2, f"dw_gate_{l}")
    d_wu = matmul_tn(du_ffn, h2, f"dw_up_{l}")
    big = {n: d.reshape(N_DEV, -1, D_MODEL) for n, d in (("w_gate", d_wg), ("w_up", d_wu), ("w_down", d_wd))}
    return dx1, big, dgffn


def _backward_mixers(dx1, dgffn, rope, W, S, A, l, send_wout=None):
    (da, delta, du, dvs, dm, mix, dgoa, dgos, dgop, dps, dwp, dwsp, db, dgsgu) = mixers_bwd(
        dx1, A["a"], A["z"], S["gsgu"], S["wcat"], S["wcat_t"], S["bz"], S["wp"], S["pscale"], S["goa"], S["gos"],
        S["gop"], W["wout"])
    d_wout = matmul_tn(mix, dx1, f"dw_out_{l}")
    if send_wout is not None:
        delta = delta + send_wout(d_wout.reshape(N_DEV, -1, D_MODEL))[0, 0]
    dpin = pool_bwd(dm)
    dqt, dk, dv = attn_bwd(A["q"], A["k"], A["kt"], A["v"], da, A["lse"], delta)
    (dx, hn, dz, qn, dq_up, kvn, dkv_up, dgmix, dgql, dgkvl, dgqh, dgkh) = mixin_bwd(
        dx1, A["x"], A["z"], rope, dqt, dk, dv, du, dvs, dpin, S["gmix"], W["win"], S["gql"], W["wq"], S["gkvl"],
        W["wkv"], S["gqh"], S["gkh"])
    d_win = matmul_tn(dz, hn, f"dw_in_{l}")
    d_wq = matmul_tn(dq_up, qn, f"dw_q_up_{l}")
    d_wkv = matmul_tn(dkv_up, kvn, f"dw_kv_up_{l}")
    d_win = jnp.concatenate([d_win[:O2], d_win[O2 + QK_NOPE:O2 + QK_HEAD], d_win[O2 + HEAD_PAD:]], axis=0)
    d_wk = d_wkv[:HEADS * HEAD_PAD].reshape(HEADS, HEAD_PAD, KV_LORA)[:, :QK_NOPE]
    d_wv = d_wkv[HEADS * HEAD_PAD:].reshape(HEADS, V_HEAD, KV_LORA)
    d_wkv = jnp.concatenate([d_wk, d_wv], axis=1)
    big = dict(w_in=d_win.reshape(N_DEV, -1, D_MODEL), w_q_up=_unpad_head_rows(d_wq, QK_HEAD).reshape(N_DEV, -1, Q_LORA),
               w_kv_up=d_wkv.reshape(N_DEV, -1, KV_LORA), w_out=d_wout.reshape(N_DEV, -1, D_MODEL))
    tril = jnp.tril(jnp.ones((CHUNK, CHUNK), F32))
    small = dict(g_mix_norm=dgmix[0], g_q_lat=dgql[0], g_kv_lat=dgkvl[0], g_q_head=dgqh[0, :QK_HEAD], g_k_head=dgkh[0, :QK_HEAD],
                 g_sgu_v=dgsgu[0], w_spatial=dwsp.reshape(HEADS, CHUNK, CHUNK) * tril, b_spatial=db[:, :HEADS].T,
                 w_pool=jnp.stack([dwp[64 * g:64 * (g + 1), 64 * g:64 * (g + 1)] for g in range(HEADS)]),
                 pool_scale=dps[0], g_out_mla=dgoa[0], g_out_sgu=dgos[0], g_out_pool=dgop[0], g_ffn_norm=dgffn[0])
    return dx, big, small


def kernel(x, positions, g_mix_norm, w_in, g_q_lat, w_q_up, g_kv_lat, w_kv_up, g_q_head, g_k_head, g_sgu_v, w_spatial, b_spatial, w_pool, pool_scale, g_out_mla, g_out_sgu, g_out_pool, w_out, g_ffn_norm, w_gate, w_up, w_down, loss_target, m_g_mix_norm, m_w_in, m_g_q_lat, m_w_q_up, m_g_kv_lat, m_w_kv_up, m_g_q_head, m_g_k_head, m_g_sgu_v, m_w_spatial, m_b_spatial, m_w_pool, m_pool_scale, m_g_out_mla, m_g_out_sgu, m_g_out_pool, m_w_out, m_g_ffn_norm, m_w_gate, m_w_up, m_w_down, v_g_mix_norm, v_w_in, v_g_q_lat, v_w_q_up, v_g_kv_lat, v_w_kv_up, v_g_q_head, v_g_k_head, v_g_sgu_v, v_w_spatial, v_b_spatial, v_w_pool, v_pool_scale, v_g_out_mla, v_g_out_sgu, v_g_out_pool, v_w_out, v_g_ffn_norm, v_w_gate, v_w_up, v_w_down):
    given = dict(locals())
    w = {n: given[n] for n in WEIGHTS}
    m = {n: given["m_" + n] for n in WEIGHTS}
    v = {n: given["v_" + n] for n in WEIGHTS}
    T = x.shape[1]
    xs = x.reshape(T, D_MODEL)

    half = QK_ROPE // 2
    inv_freq = 1.0 / (ROPE_THETA ** (jnp.arange(half, dtype=F32) / half))
    ang16 = positions.reshape(T).astype(F32)[:, None] * inv_freq
    ang = jnp.concatenate([jnp.zeros((T, QK_NOPE), F32), ang16, ang16, jnp.zeros((T, HEAD_PAD - QK_HEAD), F32)], axis=1)

    wv = {n: _shard_view(n, w[n]) for n in SHARDED}
    wanted = [(SHARDED.index(n), 0) for n in FIRST_0 + LATER_0] + [(i, 1) for i in range(len(SHARDED))]
    bf = cast_shards([wv[n] for n in SHARDED], wanted)
    nf, nl = len(FIRST_0), len(LATER_0)
    ag_first0 = exchange_start(bf[:nf], ang, True, "ag_start_first0")
    rope = rope_tables(ang + ag_first0[-1][0, 0])
    W0 = dict(zip(FIRST_0, exchange_wait(ag_first0, rope[0], "ag_wait_first0")))
    ag_later0 = exchange_start(bf[nf:nf + nl], W0["w_in"], True, "ag_start_later0")
    ag_l1 = exchange_start(bf[nf + nl:], ag_later0[-1], True, "ag_start_l1")
    Ss = [_layer_small(w, l) for l in range(DEPTH)]

    S0 = dict(Ss[0], gmix=Ss[0]["gmix"] + (ag_later0[-1][0, 0] + ag_l1[-1][0, 0]))
    A0 = _forward_attention(xs, rope, _mixer_weights(W0), S0)
    W0.update(zip(LATER_0, exchange_wait(ag_later0, A0["a"], "ag_wait_later0")))
    W0 = {**_mixer_weights(W0), **_ffn_weights(W0)}
    x1 = _forward_mixers(A0, W0["wout"], Ss[0])
    h, A0["gs"], A0["us"] = ffn_fwd(x1, Ss[0]["gffn"], W0["wg"], W0["wu"], W0["wd"])
    layer1 = dict(zip(SHARDED, exchange_wait(ag_l1, h, "ag_wait_l1")))
    W1 = {**_mixer_weights(layer1), **_ffn_weights(layer1)}
    A1 = _forward_attention(h, rope, W1, Ss[1])
    x1 = _forward_mixers(A1, W1["wout"], Ss[1])
    dh, loss_part, A1["gs"], A1["us"] = ffn_fwd(x1, Ss[1]["gffn"], W1["wg"], W1["wu"], W1["wd"],
                                                target=loss_target.reshape(T, D_MODEL))

    dx1, big_f1, dgffn1 = _backward_ffn(dh, W1, Ss[1], A1, 1)
    dh, big_m1, small1 = _backward_mixers(dx1, dgffn1, rope, W1, Ss[1], A1, 1)
    rs_l1 = exchange_start([{**big_m1, **big_f1}[n] for n in SHARDED], dh, False, "rs_start_l1")
    S0 = dict(Ss[0], gffn=Ss[0]["gffn"] + rs_l1[-1][0, 0])
    dx1, big_f0, dgffn0 = _backward_ffn(dh, W0, S0, A0, 0)
    rs_ffn0 = exchange_start([big_f0[n] for n in FFN_SIDE], dx1, False, "rs_start_ffn0")
    S0 = dict(Ss[0], gsgu=Ss[0]["gsgu"] + rs_ffn0[-1][0, 0])
    rs_out0 = []

    def send_wout(pieces):
        rs_out0.append(exchange_start([pieces], dx1, False, "rs_start_out0"))
        return rs_out0[0][-1]

    dh, big_m0, small0 = _backward_mixers(dx1, dgffn0, rope, W0, S0, A0, 0, send_wout)
    grad_x = dh.reshape(x.shape)
    smalls = [small0, small1]

    rs_first0 = exchange_start([big_m0[n] for n in FIRST_0], dh, False, "rs_start_first0")
    small_part = {n: jnp.stack([smalls[l][n] for l in range(DEPTH)]) for n in SMALL}
    small_part["loss"] = loss_part
    ag_small = exchange_start([_pack_small(small_part)], rs_first0[-1], True, "ag_start_small")
    parts1 = dict(zip(SHARDED, exchange_wait(rs_l1, ag_small[-1], "rs_wait_l1")))
    parts0 = dict(zip(FFN_SIDE, exchange_wait(rs_ffn0, ag_small[-1], "rs_wait_ffn0")))
    parts0["w_out"] = exchange_wait(rs_out0[0], ag_small[-1], "rs_wait_out0")[0]
    grad, delta, new_m, new_v = {}, {}, {}, {}

    for n in LATER_0 + FIRST_0:
        if n == FIRST_0[0]:
            no_loss = {"loss": jnp.zeros((1, 1), F32)}
            gparts = exchange_wait(ag_small, grad["w_down"], "ag_wait_small")[0]
            outs = adamw_small(gparts, *[_pack_small({**a, **no_loss}) for a in (w, m, v)])
            for d, o in zip((grad, delta, new_m, new_v), outs):
                d.update(_unpack_small(o, {**w, **no_loss}))
            parts0.update(zip(FIRST_0, exchange_wait(rs_first0, outs[0], "rs_wait_first0")))
        parts, views = [parts0[n], parts1[n]], [_shard_view(n, a[n]) for a in (w, m, v)]
        if n == "w_in":
            parts = [jnp.stack(parts, axis=2).reshape(N_DEV, -1, D_MODEL)]
            views = [a.reshape(1, -1, D_MODEL) for a in views]
        outs = adamw_sharded(parts, *views, f"adamw_{n}")
        if n == "w_in":
            outs = [o.reshape(-1, DEPTH, D_MODEL) for o in outs]
        grad[n], delta[n], new_m[n], new_v[n] = [_shard_unview(n, o) for o in outs]

    return (grad["loss"][0, 0], grad_x, *[grad[n] for n in WEIGHTS], *[delta[n] for n in WEIGHTS], *[new_m[n] for n in WEIGHTS],
            *[new_v[n] for n in WEIGHTS])
```

```python
import math

import jax
import jax.numpy as jnp
from jax import lax
from jax.experimental import pallas as pl
from jax.experimental.pallas import tpu as pltpu

F32 = jnp.float32
BF16 = jnp.bfloat16

N_DEV = 8
DEPTH = 2
D_MODEL = 1024
HEADS = 4
HEAD_PAD = 128
QK_NOPE = 64
QK_ROPE = 32
QK_HEAD = QK_NOPE + QK_ROPE
V_HEAD = 128
Q_LORA = 256
KV_LORA = 128
SGU_WIDTH = 256
SGU_HEAD_DIM = 64
CHUNK = 128
POOL_WIDTH = 256
POOL_HALO = 16
MLA_WIDTH = 512
IN_WIDTH = 1184
Z_WIDTH = 1280
FFN_HIDDEN = 2816
FFN_CHUNK = 256
ROPE_THETA = 10000.0
EPS = 1e-6
ATTN_SCALE = 1.0 / math.sqrt(QK_HEAD)
LOG2E = 1.4426950408889634
NEG_BIG = -1e30

ADAM_LR = 0.001
ADAM_B1 = 0.9
ADAM_B2 = 0.999
ADAM_EPS = 1e-08
ADAM_WD = 0.01
ADAM_STEP = 10

VMEM_LIMIT = 56 * 1024 * 1024
ROW_TILE = 512
MIXIN_PART = 256
MIXIN_BWD_PART = 128
ATTN_TILE = 1024
ATTN_SUB = 512
ATTN_FWD_SUB = 512
ATTN_BWD_HEADS = 2
MESH = pl.DeviceIdType.MESH

WHOLE = pl.BlockSpec(memory_space=pltpu.VMEM)
ANY = pl.BlockSpec(memory_space=pl.ANY)
HBM_SPEC = pl.BlockSpec(memory_space=pltpu.HBM)
SEM_SPEC = pl.BlockSpec(memory_space=pltpu.SEMAPHORE)


def _cparams(**kw):
    return pltpu.CompilerParams(vmem_limit_bytes=VMEM_LIMIT, **kw)


def _dot(a, b):
    return jnp.dot(a, b, preferred_element_type=F32)


def _dot_nt(a, b):
    return lax.dot_general(a, b, (((1,), (1,)), ((), ())), preferred_element_type=F32)


def _dot_tn(a, b):
    return lax.dot_general(a, b, (((0,), (0,)), ((), ())), preferred_element_type=F32)


def _rms(x, g, n):
    r = lax.rsqrt(jnp.sum(x * x, axis=-1, keepdims=True) * (1.0 / n) + EPS)
    return x * r * g, r


def _rms_bwd(x, r, g, dy, n):
    gdy = dy * g
    dx = r * gdy - x * (r * r * r) * (jnp.sum(x * gdy, axis=-1, keepdims=True) * (1.0 / n))
    dg = jnp.sum(dy * (x * r), axis=0, keepdims=True)
    return dx, dg


def _sigmoid(x):
    return 1.0 / (1.0 + jnp.exp(-x))


def rope_tables(ang):
    T = ang.shape[0]
    tm = min(ROW_TILE, T)

    def body(ang_ref, c_ref, sa_ref, sb_ref):
        a = ang_ref[...]
        lane = lax.broadcasted_iota(jnp.int32, a.shape, 1)
        s = jnp.sin(a)
        c_ref[...] = jnp.cos(a)
        sa_ref[...] = jnp.where(lane < QK_NOPE + QK_ROPE // 2, -s, 0.0)
        sb_ref[...] = jnp.where(lane >= QK_NOPE + QK_ROPE // 2, s, 0.0)

    row = pl.BlockSpec((tm, HEAD_PAD), lambda i: (i, 0))
    return pl.pallas_call(
        body, grid=(T // tm,), name="rope_tables", in_specs=[row], out_specs=[row] * 3,
        out_shape=[jax.ShapeDtypeStruct((T, HEAD_PAD), F32)] * 3, compiler_params=_cparams(),
    )(ang)


def _rope(x, c, sa, sb):
    half = QK_ROPE // 2
    return x * c + pltpu.roll(x, HEAD_PAD - half, 1) * sa + pltpu.roll(x, half, 1) * sb


def _rope_bwd(dy, c, sa, sb):
    half = QK_ROPE // 2
    return dy * c + pltpu.roll(dy * sa, half, 1) + pltpu.roll(dy * sb, HEAD_PAD - half, 1)


def _skewed(stages, groups):
    for t in range(len(stages) + len(groups) - 1):
        for p, g in enumerate(groups):
            if 0 <= t - p < len(stages):
                stages[t - p](g)


def _head_masks(shape, width):
    lane = lax.broadcasted_iota(jnp.int32, shape, len(shape) - 1)
    return [(lane >= width * h) & (lane < width * (h + 1)) for h in range(HEADS)]


def _qkv_pre(z, gql, wq, gkvl, wkv):
    ql = z[:, 0:Q_LORA]
    kvl = z[:, Q_LORA:Q_LORA + KV_LORA]
    kr = z[:, Q_LORA + KV_LORA:Q_LORA + KV_LORA + HEAD_PAD]
    qn, rq = _rms(ql, gql, Q_LORA)
    kvn, rkv = _rms(kvl, gkvl, KV_LORA)
    qn = qn.astype(BF16)
    kvn = kvn.astype(BF16)
    q_up = _dot_nt(qn, wq)
    kv_up = _dot_nt(kvn, wkv)
    return ql, kvl, kr, qn, rq, kvn, rkv, q_up, kv_up


def mixin_fwd(x, rope, gmix, win, gql, wq, gkvl, wkv, gqh, gkh):
    T = x.shape[0]
    tm = min(2 * ROW_TILE, T)
    part = min(MIXIN_PART, tm)

    def body(x_ref, c_ref, sa_ref, sb_ref, gmix_ref, win_ref, gql_ref, wq_ref, gkvl_ref, wkv_ref, gqh_ref, gkh_ref,
             z_ref, q_ref, k_ref, v_ref, kt_ref, vt_ref):
        def project(g):
            hn = _rms(x_ref[g["rows"], :], gmix_ref[...], D_MODEL)[0].astype(BF16)
            g["z"] = _dot_nt(hn, win_ref[...])
            z_ref[g["rows"], :] = g["z"]

        def latents(g):
            _, _, g["kr"], _, _, _, _, g["q_up"], g["kv_up"] = _qkv_pre(g["z"], gql_ref[...], wq_ref[...], gkvl_ref[...],
                                                                        wkv_ref[...])

        def heads(g):
            rows = g["rows"]
            c, sa, sb = c_ref[rows, :], sa_ref[rows, :], sb_ref[rows, :]
            for h in range(HEADS):
                lo = HEAD_PAD * h
                qh = _rms(g["q_up"][:, lo:lo + HEAD_PAD], gqh_ref[...], QK_HEAD)[0]
                q_ref[h, rows, :] = (_rope(qh, c, sa, sb) * (ATTN_SCALE * LOG2E)).astype(BF16)
                kh = _rope(_rms(g["kv_up"][:, lo:lo + HEAD_PAD] + g["kr"], gkh_ref[...], QK_HEAD)[0], c, sa, sb)
                k_ref[h, rows, :] = kh.astype(BF16)
                kt_ref[h, :, rows] = jnp.transpose(kh).astype(BF16)
                vh = g["kv_up"][:, HEADS * HEAD_PAD + lo:HEADS * HEAD_PAD + lo + HEAD_PAD]
                v_ref[h, rows, :] = vh.astype(BF16)
                vt_ref[h, :, rows] = jnp.transpose(vh).astype(BF16)

        _skewed((project, latents, heads), [dict(rows=pl.ds(part * p, part)) for p in range(tm // part)])

    row = lambda w: pl.BlockSpec((tm, w), lambda i: (i, 0))
    head = pl.BlockSpec((HEADS, tm, HEAD_PAD), lambda i: (0, i, 0))
    head_t = pl.BlockSpec((HEADS, HEAD_PAD, tm), lambda i: (0, 0, i))
    return pl.pallas_call(
        body, grid=(T // tm,), name="mixin_fwd",
        in_specs=[row(D_MODEL)] + [row(HEAD_PAD)] * 3 + [WHOLE] * 8,
        out_specs=[row(Z_WIDTH), head, head, head, head_t, head_t],
        out_shape=[jax.ShapeDtypeStruct((T, Z_WIDTH), F32)] + [jax.ShapeDtypeStruct((HEADS, T, HEAD_PAD), BF16)] * 3
                  + [jax.ShapeDtypeStruct((HEADS, HEAD_PAD, T), BF16)] * 2,
        compiler_params=_cparams(),
    )(x, *rope, gmix, win, gql, wq, gkvl, wkv, gqh, gkh)


def _pair_tables(pairs):
    return (jnp.asarray([p[0] for p in pairs], jnp.int32), jnp.asarray([p[1] for p in pairs], jnp.int32))


def attn_fwd(q, k, vt):
    _, T, _ = q.shape
    tb = min(ATTN_TILE, T)
    sq = min(ATTN_FWD_SUB, tb)
    nb = T // tb

    pairs = [(i, j) for i in range(nb) for j in range(i + 1)]

    def body(i_tab, j_tab, q_ref, k_ref, vt_ref, o_ref, lse_ref, m_s, l_s, acc_s):
        i, j = i_tab[pl.program_id(0)], j_tab[pl.program_id(0)]

        @pl.when(j == 0)
        def _():
            m_s[...] = jnp.full(m_s.shape, -jnp.inf, F32)
            l_s[...] = jnp.zeros(l_s.shape, F32)
            acc_s[...] = jnp.zeros(acc_s.shape, F32)

        def scores(g):
            st = _dot_nt(k_ref[g["h"], g["kb"], :], q_ref[g["h"], g["qa"], :])
            if g["masked"]:
                krow = lax.broadcasted_iota(jnp.int32, st.shape, 0)
                qcol = g["q0"] + lax.broadcasted_iota(jnp.int32, st.shape, 1)
                st = jnp.where(krow <= qcol, st, NEG_BIG)
            g["st"] = st

        def new_max(g):
            h, qa = g["h"], g["qa"]
            m_prev = m_s[h, :, qa]
            g["m"] = jnp.maximum(m_prev, jnp.max(g["st"], axis=0, keepdims=True))
            g["alpha"] = jnp.exp2(m_prev - g["m"])
            m_s[h, :, qa] = g["m"]

        def weights(g):
            pt = jnp.exp2(g["st"] - g["m"])
            g["lsum"] = jnp.sum(pt, axis=0, keepdims=True)
            g["pt"] = pt.astype(BF16)

        def accumulate(g):
            h, qa = g["h"], g["qa"]
            l_s[h, :, qa] = g["alpha"] * l_s[h, :, qa] + g["lsum"]
            acc_s[h, :, qa] = g["alpha"] * acc_s[h, :, qa] + _dot(vt_ref[h, :, g["kb"]], g["pt"])

        def tiles(masked):
            return [dict(h=h, q0=q0, qa=slice(q0, q0 + sq), kb=slice(0, q0 + sq if masked else tb), masked=masked)
                    for h in range(HEADS) for q0 in range(0, tb, sq)]

        @pl.when(j < i)
        def _():
            _skewed((scores, new_max, weights, accumulate), tiles(False))

        @pl.when(j == i)
        def _():
            _skewed((scores, new_max, weights, accumulate), tiles(True))
            for h in range(HEADS):
                l = l_s[h]
                o_ref[:, HEAD_PAD * h:HEAD_PAD * (h + 1)] = jnp.transpose(acc_s[h] / l)
                lse_ref[h] = m_s[h] + jnp.log2(l)

    qspec = pl.BlockSpec((HEADS, tb, HEAD_PAD), lambda p, it, jt: (0, it[p], 0))
    kspec = pl.BlockSpec((HEADS, tb, HEAD_PAD), lambda p, it, jt: (0, jt[p], 0))
    vspec = pl.BlockSpec((HEADS, HEAD_PAD, tb), lambda p, it, jt: (0, 0, jt[p]))
    grid_spec = pltpu.PrefetchScalarGridSpec(
        num_scalar_prefetch=2, grid=(len(pairs),),
        in_specs=[qspec, kspec, vspec],
        out_specs=[pl.BlockSpec((tb, MLA_WIDTH), lambda p, it, jt: (it[p], 0)),
                   pl.BlockSpec((HEADS, 1, tb), lambda p, it, jt: (0, 0, it[p]))],
        scratch_shapes=[pltpu.VMEM((HEADS, 1, tb), F32), pltpu.VMEM((HEADS, 1, tb), F32), pltpu.VMEM((HEADS, HEAD_PAD, tb), F32)])
    return pl.pallas_call(
        body, grid_spec=grid_spec, name="attn_fwd",
        out_shape=[jax.ShapeDtypeStruct((T, MLA_WIDTH), F32), jax.ShapeDtypeStruct((HEADS, 1, T), F32)],
        compiler_params=_cparams(),
    )(*_pair_tables(pairs), q, k, vt)


def _sgu_fwd_chunk(vn_c, wcat, bz, masks):
    vstack = jnp.concatenate([jnp.where(mk, vn_c, 0.0).astype(BF16) for mk in masks], axis=0)
    return _dot(wcat, vstack) + bz


def _pool_counts(i, tm):
    pos1 = (i * tm + 1 + lax.broadcasted_iota(jnp.int32, (tm, POOL_WIDTH), 0)).astype(F32)
    lane = lax.broadcasted_iota(jnp.int32, (tm, POOL_WIDTH), 1)
    win = jnp.where(lane < 64, 2.0, jnp.where(lane < 128, 4.0, jnp.where(lane < 192, 8.0, 16.0)))
    return jnp.minimum(pos1, win), lane


def _by_group(lane, s2, s4, s8, s16):
    return jnp.where(lane < 64, s2, jnp.where(lane < 128, s4, jnp.where(lane < 192, s8, s16)))


def _pool_means(pin, halo, i, tm):
    s1 = jnp.concatenate([halo, pin], axis=0)
    s2 = s1 + pltpu.roll(s1, 1, 0)
    s4 = s2 + pltpu.roll(s2, 2, 0)
    s8 = s4 + pltpu.roll(s4, 4, 0)
    s16 = s8 + pltpu.roll(s8, 8, 0)
    cnt, lane = _pool_counts(i, tm)
    sel = _by_group(lane, s2[POOL_HALO:], s4[POOL_HALO:], s8[POOL_HALO:], s16[POOL_HALO:])
    return sel / cnt - pin


def _mixers_fwd_tile(i, tm, a, u, vs, pin, halo, gsgu, wcat, bz, wp, pscale, goa, gos, gop):
    vn, rv = _rms(vs, gsgu, SGU_WIDTH)
    masks = _head_masks((CHUNK, SGU_WIDTH), SGU_HEAD_DIM)
    zc = jnp.concatenate([_sgu_fwd_chunk(vn[CHUNK * c:CHUNK * (c + 1)], wcat, bz, masks) for c in range(tm // CHUNK)], axis=0)
    gm = u * zc
    halo = jnp.where(i > 0, halo, 0.0)
    m = _pool_means(pin, halo, i, tm).astype(BF16)
    yp_pre = _dot(m, wp)
    yp = yp_pre * pscale
    na, ra = _rms(a, goa, MLA_WIDTH)
    ng, rg = _rms(gm, gos, SGU_WIDTH)
    npo, rp = _rms(yp, gop, POOL_WIDTH)
    mix = jnp.concatenate([na, ng, npo], axis=1).astype(BF16)
    return vn, rv, zc, gm, m, yp_pre, yp, ra, rg, rp, mix


def _z_specs(tm):
    col = lambda c: pl.BlockSpec((tm, 256), lambda i: (i, c))
    halo = pl.BlockSpec((POOL_HALO, 256), lambda i: (jnp.maximum(i * (tm // POOL_HALO) - 1, 0), 4))
    return [col(2), col(3), col(4), halo]


def mixers_fwd(x, a, z, gsgu, wcat, bz, wp, pscale, goa, gos, gop, wout):
    T = x.shape[0]
    tm = min(ROW_TILE, T)

    def body(x_ref, a_ref, u_ref, vs_ref, pin_ref, halo_ref, gsgu_ref, wcat_ref, bz_ref, wp_ref, ps_ref,
             goa_ref, gos_ref, gop_ref, wout_ref, x1_ref):
        i = pl.program_id(0)
        mix = _mixers_fwd_tile(i, tm, a_ref[...], u_ref[...], vs_ref[...], pin_ref[...], halo_ref[...], gsgu_ref[...],
                               wcat_ref[...], bz_ref[...], wp_ref[...], ps_ref[...], goa_ref[...], gos_ref[...],
                               gop_ref[...])[-1]
        x1_ref[...] = x_ref[...] + _dot(mix, wout_ref[...])

    row = lambda w: pl.BlockSpec((tm, w), lambda i: (i, 0))
    return pl.pallas_call(
        body, grid=(T // tm,), name="mixers_fwd",
        in_specs=[row(D_MODEL), row(MLA_WIDTH)] + _z_specs(tm) + [WHOLE] * 9,
        out_specs=row(D_MODEL),
        out_shape=jax.ShapeDtypeStruct((T, D_MODEL), F32),
        compiler_params=_cparams(),
    )(x, a, z, z, z, z, gsgu, wcat, bz, wp, pscale, goa, gos, gop, wout)


def ffn_fwd(x1, gffn, wg, wu, wd, target=None):
    T = x1.shape[0]
    tm = min(ROW_TILE, T)
    with_loss = target is not None

    def body(*refs):
        x1_ref, gffn_ref, wg_ref, wu_ref, wd_ref = refs[:5]
        outs = refs[6:] if with_loss else refs[5:]
        x1v = x1_ref[...]
        h2 = _rms(x1v, gffn_ref[...], D_MODEL)[0].astype(BF16)
        acc = x1v
        for c in range(FFN_HIDDEN // FFN_CHUNK):
            sl = slice(FFN_CHUNK * c, FFN_CHUNK * (c + 1))
            g = _dot_nt(h2, wg_ref[sl, :])
            u = _dot_nt(h2, wu_ref[sl, :])
            outs[-2][:, sl] = g.astype(BF16)
            outs[-1][:, sl] = u.astype(BF16)
            act = (g * _sigmoid(g) * u).astype(BF16)
            acc = acc + _dot(act, wd_ref[sl, :])
        if not with_loss:
            outs[0][...] = acc
            return
        dy_ref, loss_ref = outs[0], outs[1]

        @pl.when(pl.program_id(0) == 0)
        def _():
            loss_ref[...] = jnp.zeros(loss_ref.shape, F32)

        err = acc - refs[5][...]
        dy_ref[...] = err * (1.0 / D_MODEL)
        per_row = jnp.sum(err * err, axis=1, keepdims=True) * (1.0 / D_MODEL)
        loss_ref[...] += 0.5 * jnp.sum(per_row, axis=0, keepdims=True)

    row = lambda w: pl.BlockSpec((tm, w), lambda i: (i, 0))
    hidden = [jax.ShapeDtypeStruct((T, FFN_HIDDEN), BF16)] * 2
    if with_loss:
        return pl.pallas_call(
            body, grid=(T // tm,), name="ffn_fwd_loss",
            in_specs=[row(D_MODEL)] + [WHOLE] * 4 + [row(D_MODEL)],
            out_specs=[row(D_MODEL), pl.BlockSpec((1, 1), lambda i: (0, 0)), row(FFN_HIDDEN), row(FFN_HIDDEN)],
            out_shape=[jax.ShapeDtypeStruct((T, D_MODEL), F32), jax.ShapeDtypeStruct((1, 1), F32)] + hidden,
            compiler_params=_cparams(),
        )(x1, gffn, wg, wu, wd, target)
    return pl.pallas_call(
        body, grid=(T // tm,), name="ffn_fwd",
        in_specs=[row(D_MODEL)] + [WHOLE] * 4,
        out_specs=[row(D_MODEL), row(FFN_HIDDEN), row(FFN_HIDDEN)],
        out_shape=[jax.ShapeDtypeStruct((T, D_MODEL), F32)] + hidden,
        compiler_params=_cparams(),
    )(x1, gffn, wg, wu, wd)


def _acc_spec(shape):
    return pl.BlockSpec(shape, lambda i: (0,) * len(shape))


def ffn_bwd(dx2, x1, gs, us, gffn, wg, wu, wd):
    T = x1.shape[0]
    tm = min(ROW_TILE // 2, T)

    def body(dx2_ref, x1_ref, gs_ref, us_ref, gffn_ref, wg_ref, wu_ref, wd_ref,
             dx1_ref, h2_ref, act_ref, dg_ref, du_ref, dgffn_ref):
        @pl.when(pl.program_id(0) == 0)
        def _():
            dgffn_ref[...] = jnp.zeros(dgffn_ref.shape, F32)

        dx2v = dx2_ref[...]
        dy = dx2v.astype(BF16)
        x1v = x1_ref[...]
        h2, r = _rms(x1v, gffn_ref[...], D_MODEL)
        h2_ref[...] = h2.astype(BF16)
        for c in range(FFN_HIDDEN // FFN_CHUNK):
            sl = slice(FFN_CHUNK * c, FFN_CHUNK * (c + 1))
            g = gs_ref[:, sl].astype(F32)
            u = us_ref[:, sl].astype(F32)
            dact = _dot_nt(dy, wd_ref[sl, :])
            sg = _sigmoid(g)
            silu = g * sg
            act_ref[:, sl] = (silu * u).astype(BF16)
            dg_ref[:, sl] = (dact * u * (sg * (1.0 + g * (1.0 - sg)))).astype(BF16)
            du_ref[:, sl] = (dact * silu).astype(BF16)
        dh2 = _dot(dg_ref[...], wg_ref[...]) + _dot(du_ref[...], wu_ref[...])
        dxn, dgn = _rms_bwd(x1v, r, gffn_ref[...], dh2, D_MODEL)
        dx1_ref[...] = dx2v + dxn
        dgffn_ref[...] += dgn

    row = lambda w: pl.BlockSpec((tm, w), lambda i: (i, 0))
    return pl.pallas_call(
        body, grid=(T // tm,), name="ffn_bwd",
        in_specs=[row(D_MODEL), row(D_MODEL), row(FFN_HIDDEN), row(FFN_HIDDEN)] + [WHOLE] * 4,
        out_specs=[row(D_MODEL), row(D_MODEL), row(FFN_HIDDEN), row(FFN_HIDDEN), row(FFN_HIDDEN), _acc_spec((1, D_MODEL))],
        out_shape=[jax.ShapeDtypeStruct((T, D_MODEL), F32), jax.ShapeDtypeStruct((T, D_MODEL), BF16),
                   jax.ShapeDtypeStruct((T, FFN_HIDDEN), BF16), jax.ShapeDtypeStruct((T, FFN_HIDDEN), BF16),
                   jax.ShapeDtypeStruct((T, FFN_HIDDEN), BF16), jax.ShapeDtypeStruct((1, D_MODEL), F32)],
        compiler_params=_cparams(),
    )(dx2, x1, gs, us, gffn, wg, wu, wd)


TN_K_TILE = 2048
TN_ACC_BYTES = 6 * 1024 * 1024


def matmul_tn(a, b, name):
    T, M = a.shape
    N = b.shape[1]
    tk = min(TN_K_TILE, T)
    tm = M if M <= 1024 else M // 2
    tn = max(d for d in range(128, N + 1, 128) if N % d == 0 and tm * d * 4 <= TN_ACC_BYTES)
    nk = T // tk

    def body(a_ref, b_ref, o_ref, acc):
        k = pl.program_id(2)

        @pl.when(k == 0)
        def _():
            acc[...] = jnp.zeros(acc.shape, F32)

        acc[...] += _dot_tn(a_ref[...].astype(BF16), b_ref[...].astype(BF16))

        @pl.when(k == nk - 1)
        def _():
            o_ref[...] = acc[...].astype(BF16)

    return pl.pallas_call(
        body, grid=(M // tm, N // tn, nk), name=name,
        in_specs=[pl.BlockSpec((tk, tm), lambda i, j, k: (k, i)), pl.BlockSpec((tk, tn), lambda i, j, k: (k, j))],
        out_specs=pl.BlockSpec((tm, tn), lambda i, j, k: (i, j)),
        out_shape=jax.ShapeDtypeStruct((M, N), BF16),
        scratch_shapes=[pltpu.VMEM((tm, tn), F32)],
        compiler_params=_cparams(),
    )(a, b)


def mixers_bwd(dx1, a, z, gsgu, wcat, wcat_t, bz, wp, pscale, goa, gos, gop, wout):
    T = a.shape[0]
    tm = min(ROW_TILE, T)

    def body(dx1_ref, a_ref, u_ref, vs_ref, pin_ref, halo_ref, gsgu_ref, wcat_ref, wcatt_ref, bz_ref, wp_ref,
             ps_ref, goa_ref, gos_ref, gop_ref, wout_ref,
             da_ref, delta_ref, du_ref, dvs_ref, dm_ref, mix_ref,
             dgoa_ref, dgos_ref, dgop_ref, dps_ref, dwp_ref, dwsp_ref, db_ref, dgsgu_ref):
        i = pl.program_id(0)

        @pl.when(i == 0)
        def _():
            for r in (dgoa_ref, dgos_ref, dgop_ref, dps_ref, dwp_ref, dwsp_ref, db_ref, dgsgu_ref):
                r[...] = jnp.zeros(r.shape, F32)

        a_v, u, vs = a_ref[...], u_ref[...], vs_ref[...]
        goa, gos, gop, pscale_v = goa_ref[...], gos_ref[...], gop_ref[...], ps_ref[...]
        vn, rv, zc, gm, m, yp_pre, yp, ra, rg, rp, mix = _mixers_fwd_tile(
            i, tm, a_v, u, vs, pin_ref[...], halo_ref[...], gsgu_ref[...], wcat_ref[...], bz_ref[...], wp_ref[...],
            pscale_v, goa, gos, gop)
        mix_ref[...] = mix
        dmix = _dot_nt(dx1_ref[...].astype(BF16), wout_ref[...])
        da, dgoa = _rms_bwd(a_v, ra, goa, dmix[:, :MLA_WIDTH], MLA_WIDTH)
        dgm, dgos = _rms_bwd(gm, rg, gos, dmix[:, MLA_WIDTH:MLA_WIDTH + SGU_WIDTH], SGU_WIDTH)
        dyp, dgop = _rms_bwd(yp, rp, gop, dmix[:, MLA_WIDTH + SGU_WIDTH:], POOL_WIDTH)
        da_ref[...] = da
        dgoa_ref[...] += dgoa
        dgos_ref[...] += dgos
        dgop_ref[...] += dgop
        prod = da * a_v
        ones = jnp.ones((8, HEAD_PAD), F32)
        for h in range(HEADS):
            lo = HEAD_PAD * h
            sums = lax.dot_general(ones, prod[:, lo:lo + HEAD_PAD], (((1,), (1,)), ((), ())), preferred_element_type=F32,
                                   precision=lax.Precision.HIGHEST)
            delta_ref[h] = sums[0:1, :]
        dps_ref[...] += jnp.sum(dyp * yp_pre, axis=0, keepdims=True)
        dyp_pre = (dyp * pscale_v).astype(BF16)
        dwp_ref[...] += _dot_tn(m, dyp_pre)
        dm_ref[...] = _dot_nt(dyp_pre, wp_ref[...])
        du_ref[...] = dgm * zc
        dzc = dgm * u
        masks = _head_masks((CHUNK, SGU_WIDTH), SGU_HEAD_DIM)
        lane_b = lax.broadcasted_iota(jnp.int32, (CHUNK, HEAD_PAD), 1)
        dvn_parts = []
        dwsp = jnp.zeros(dwsp_ref.shape, F32)
        db = jnp.zeros(db_ref.shape, F32)
        for c in range(tm // CHUNK):
            dz_c = dzc[CHUNK * c:CHUNK * (c + 1)]
            dzstack = jnp.concatenate([jnp.where(mk, dz_c, 0.0).astype(BF16) for mk in masks], axis=0)
            dvn_parts.append(_dot(wcatt_ref[...], dzstack))
            dwsp = dwsp + _dot_nt(dzstack, vn[CHUNK * c:CHUNK * (c + 1)].astype(BF16))
            for h, mk in enumerate(masks):
                col = jnp.sum(jnp.where(mk, dz_c, 0.0), axis=1, keepdims=True)
                db = db + jnp.where(lane_b == h, col, 0.0)
        dwsp_ref[...] += dwsp
        db_ref[...] += db
        dvs, dgsgu = _rms_bwd(vs, rv, gsgu_ref[...], jnp.concatenate(dvn_parts, axis=0), SGU_WIDTH)
        dvs_ref[...] = dvs
        dgsgu_ref[...] += dgsgu

    row = lambda w: pl.BlockSpec((tm, w), lambda i: (i, 0))
    head = pl.BlockSpec((HEADS, 1, tm), lambda i: (0, 0, i))
    acc_shapes = [(1, MLA_WIDTH), (1, SGU_WIDTH), (1, POOL_WIDTH), (1, POOL_WIDTH), (POOL_WIDTH, POOL_WIDTH),
                  (HEADS * CHUNK, CHUNK), (CHUNK, HEAD_PAD), (1, SGU_WIDTH)]
    return pl.pallas_call(
        body, grid=(T // tm,), name="mixers_bwd",
        in_specs=[row(D_MODEL), row(MLA_WIDTH)] + _z_specs(tm) + [WHOLE] * 10,
        out_specs=[row(MLA_WIDTH), head, row(256), row(256), row(256), row(D_MODEL)] + [_acc_spec(s) for s in acc_shapes],
        out_shape=[jax.ShapeDtypeStruct((T, MLA_WIDTH), F32), jax.ShapeDtypeStruct((HEADS, 1, T), F32),
                   jax.ShapeDtypeStruct((T, 256), F32), jax.ShapeDtypeStruct((T, 256), F32),
                   jax.ShapeDtypeStruct((T, 256), F32), jax.ShapeDtypeStruct((T, D_MODEL), BF16)]
                  + [jax.ShapeDtypeStruct(s, F32) for s in acc_shapes],
        compiler_params=_cparams(),
    )(dx1, a, z, z, z, z, gsgu, wcat, wcat_t, bz, wp, pscale, goa, gos, gop, wout)


def pool_bwd(dm):
    T = dm.shape[0]
    tm = min(4 * ROW_TILE, T)
    nt = T // tm

    def body(dm_ref, next_ref, dpin_ref):
        i = pl.program_id(0)
        cnt, lane = _pool_counts(i, tm)
        dmv = dm_ref[...]
        win = _by_group(lane[:POOL_HALO], 2.0, 4.0, 8.0, 16.0)
        nxt = jnp.where(i < nt - 1, next_ref[...] / win, 0.0)
        r1 = jnp.concatenate([dmv / cnt, nxt], axis=0)
        n = tm + POOL_HALO
        r2 = r1 + pltpu.roll(r1, n - 1, 0)
        r4 = r2 + pltpu.roll(r2, n - 2, 0)
        r8 = r4 + pltpu.roll(r4, n - 4, 0)
        r16 = r8 + pltpu.roll(r8, n - 8, 0)
        dpin_ref[...] = _by_group(lane, r2[:tm], r4[:tm], r8[:tm], r16[:tm]) - dmv

    return pl.pallas_call(
        body, grid=(nt,), name="pool_bwd",
        in_specs=[pl.BlockSpec((tm, 256), lambda i: (i, 0)),
                  pl.BlockSpec((POOL_HALO, 256), lambda i: (jnp.minimum((i + 1) * (tm // POOL_HALO), T // POOL_HALO - 1), 0))],
        out_specs=pl.BlockSpec((tm, 256), lambda i: (i, 0)),
        out_shape=jax.ShapeDtypeStruct((T, 256), F32),
        compiler_params=_cparams(),
    )(dm, dm)


def attn_bwd(q, k, kt, v, do, lse, delta):
    _, T, _ = q.shape
    tb = min(ATTN_TILE, T)
    sb = min(ATTN_SUB, tb)
    ns = tb // sb
    nb = T // tb
    hb = ATTN_BWD_HEADS

    pairs = [(i, j) for i in range(nb) for j in range(i, nb)]

    def body(i_tab, j_tab, q_ref, k_ref, kt_ref, v_ref, do_ref, lse_ref, delta_ref, dqt_ref, dk_ref, dv_ref, dk_s, dv_s):
        i, j = i_tab[pl.program_id(1)], j_tab[pl.program_id(1)]

        @pl.when(pl.program_id(1) == 0)
        def _():
            dqt_ref[...] = jnp.zeros(dqt_ref.shape, F32)

        @pl.when(j == i)
        def _():
            dk_s[...] = jnp.zeros(dk_s.shape, F32)
            dv_s[...] = jnp.zeros(dv_s.shape, F32)

        def sub_block(h, a, b, masked):
            qa = slice(sb * a, sb * (a + 1))
            kb = slice(sb * b, sb * (b + 1))
            qv = q_ref[h, qa, :]
            dov = do_ref[qa, HEAD_PAD * h:HEAD_PAD * (h + 1)].astype(BF16)
            st = _dot_nt(k_ref[h, kb, :], qv)
            dpt = _dot_nt(v_ref[h, kb, :], dov)
            pt = jnp.exp2(st - lse_ref[h, :, qa])
            if masked:
                krow = lax.broadcasted_iota(jnp.int32, st.shape, 0)
                qcol = lax.broadcasted_iota(jnp.int32, st.shape, 1)
                pt = jnp.where(krow <= qcol, pt, 0.0)
            dst = (pt * (dpt - delta_ref[h, :, qa])).astype(BF16)
            dv_s[h, kb, :] += _dot(pt.astype(BF16), dov)
            dk_s[h, kb, :] += _dot(dst, qv)
            cols = pl.ds(pl.multiple_of(j * tb + sb * a, sb), sb)
            dqt_ref[h, :, cols] += _dot(kt_ref[h, :, kb], dst)

        @pl.when(j > i)
        def _():
            for a in range(ns):
                for h in range(hb):
                    for b in range(ns):
                        sub_block(h, a, b, False)

        @pl.when(j == i)
        def _():
            for a in range(ns):
                for h in range(hb):
                    for b in range(a + 1):
                        sub_block(h, a, b, a == b)

        @pl.when(j == nb - 1)
        def _():
            dk_ref[...] = dk_s[...] * (1.0 / LOG2E)
            dv_ref[...] = dv_s[...]

    qspec = pl.BlockSpec((hb, tb, HEAD_PAD), lambda g, p, it, jt: (g, jt[p], 0))
    kspec = pl.BlockSpec((hb, tb, HEAD_PAD), lambda g, p, it, jt: (g, it[p], 0))
    ktspec = pl.BlockSpec((hb, HEAD_PAD, tb), lambda g, p, it, jt: (g, 0, it[p]))
    rowspec = pl.BlockSpec((hb, 1, tb), lambda g, p, it, jt: (g, 0, jt[p]))
    grid_spec = pltpu.PrefetchScalarGridSpec(
        num_scalar_prefetch=2, grid=(HEADS // hb, len(pairs)),
        in_specs=[qspec, kspec, ktspec, kspec, pl.BlockSpec((tb, hb * HEAD_PAD), lambda g, p, it, jt: (jt[p], g)), rowspec, rowspec],
        out_specs=[pl.BlockSpec((hb, HEAD_PAD, T), lambda g, p, it, jt: (g, 0, 0)), kspec, kspec],
        scratch_shapes=[pltpu.VMEM((hb, tb, HEAD_PAD), F32), pltpu.VMEM((hb, tb, HEAD_PAD), F32)])
    return pl.pallas_call(
        body, grid_spec=grid_spec, name="attn_bwd",
        out_shape=[jax.ShapeDtypeStruct((HEADS, HEAD_PAD, T), F32)] + [jax.ShapeDtypeStruct((HEADS, T, HEAD_PAD), F32)] * 2,
        compiler_params=_cparams(),
    )(*_pair_tables(pairs), q, k, kt, v, do, lse, delta)


def mixin_bwd(dres, x, z, rope, dqt, dk, dv, du, dvs, dpin, gmix, win, gql, wq, gkvl, wkv, gqh, gkh):
    T = x.shape[0]
    tm = min(ROW_TILE, T)
    part = min(MIXIN_BWD_PART, tm)

    def body(dres_ref, x_ref, z_ref, c_ref, sa_ref, sb_ref, dqt_ref, dk_ref, dv_ref, du_ref, dvs_ref, dpin_ref,
             gmix_ref, win_ref, gql_ref, wq_ref, gkvl_ref, wkv_ref, gqh_ref, gkh_ref,
             dx_ref, hn_ref, dz_ref, qn_ref, dqup_ref, kvn_ref, dkvup_ref,
             dgmix_ref, dgql_ref, dgkvl_ref, dgqh_ref, dgkh_ref):
        @pl.when(pl.program_id(0) == 0)
        def _():
            for r in (dgmix_ref, dgql_ref, dgkvl_ref, dgqh_ref, dgkh_ref):
                r[...] = jnp.zeros(r.shape, F32)

        lane = lax.broadcasted_iota(jnp.int32, (part, HEAD_PAD), 1)
        rope_lanes = (lane >= QK_NOPE) & (lane < QK_HEAD)

        def recompute(g):
            rows = g["rows"]
            g["xv"] = x_ref[rows, :]
            hn, g["rx"] = _rms(g["xv"], gmix_ref[...], D_MODEL)
            hn_ref[rows, :] = hn.astype(BF16)
            g["ql"], g["kvl"], g["kr"], qn, g["rq"], kvn, g["rkv"], g["q_up"], g["kv_up"] = _qkv_pre(
                z_ref[rows, :], gql_ref[...], wq_ref[...], gkvl_ref[...], wkv_ref[...])
            qn_ref[rows, :] = qn
            kvn_ref[rows, :] = kvn

        def heads(g):
            rows = g["rows"]
            c, sa, sb = c_ref[rows, :], sa_ref[rows, :], sb_ref[rows, :]
            dkr = jnp.zeros((part, HEAD_PAD), F32)
            dgqh = jnp.zeros((1, HEAD_PAD), F32)
            dgkh = jnp.zeros((1, HEAD_PAD), F32)
            dq_parts, dk_parts, dv_parts = [], [], []
            for h in range(HEADS):
                lo = HEAD_PAD * h
                qh = g["q_up"][:, lo:lo + HEAD_PAD]
                rqh = lax.rsqrt(jnp.sum(qh * qh, axis=-1, keepdims=True) * (1.0 / QK_HEAD) + EPS)
                dq_h = jnp.transpose(dqt_ref[h, :, rows]) * ATTN_SCALE
                dqh, dg = _rms_bwd(qh, rqh, gqh_ref[...], _rope_bwd(dq_h, c, sa, sb), QK_HEAD)
                dgqh = dgqh + dg
                dq_parts.append(dqh)
                kh = g["kv_up"][:, lo:lo + HEAD_PAD] + g["kr"]
                rkh = lax.rsqrt(jnp.sum(kh * kh, axis=-1, keepdims=True) * (1.0 / QK_HEAD) + EPS)
                dkh, dg = _rms_bwd(kh, rkh, gkh_ref[...], _rope_bwd(dk_ref[h, rows, :], c, sa, sb), QK_HEAD)
                dgkh = dgkh + dg
                dkr = dkr + jnp.where(rope_lanes, dkh, 0.0)
                dk_parts.append(dkh)
                dv_parts.append(dv_ref[h, rows, :])
            dgqh_ref[...] += dgqh
            dgkh_ref[...] += dgkh
            g["dkr"] = dkr
            g["dq_up"] = jnp.concatenate(dq_parts, axis=1).astype(BF16)
            g["dkv_up"] = jnp.concatenate(dk_parts + dv_parts, axis=1).astype(BF16)
            dqup_ref[rows, :] = g["dq_up"]
            dkvup_ref[rows, :] = g["dkv_up"]

        def latents(g):
            rows = g["rows"]
            dql, dg = _rms_bwd(g["ql"], g["rq"], gql_ref[...], _dot(g["dq_up"], wq_ref[...]), Q_LORA)
            dgql_ref[...] += dg
            dkvl, dg = _rms_bwd(g["kvl"], g["rkv"], gkvl_ref[...], _dot(g["dkv_up"], wkv_ref[...]), KV_LORA)
            dgkvl_ref[...] += dg
            g["dz"] = jnp.concatenate([dql, dkvl, g["dkr"], du_ref[rows, :], dvs_ref[rows, :], dpin_ref[rows, :]],
                                      axis=1).astype(BF16)
            dz_ref[rows, :] = g["dz"]

        def inputs(g):
            rows = g["rows"]
            dxn, dg = _rms_bwd(g["xv"], g["rx"], gmix_ref[...], _dot(g["dz"], win_ref[...]), D_MODEL)
            dgmix_ref[...] += dg
            dx_ref[rows, :] = dres_ref[rows, :] + dxn

        _skewed((recompute, heads, latents, inputs), [dict(rows=pl.ds(part * p, part)) for p in range(tm // part)])

    row = lambda w: pl.BlockSpec((tm, w), lambda i: (i, 0))
    head = pl.BlockSpec((HEADS, tm, HEAD_PAD), lambda i: (0, i, 0))
    head_t = pl.BlockSpec((HEADS, HEAD_PAD, tm), lambda i: (0, 0, i))
    acc_shapes = [(1, D_MODEL), (1, Q_LORA), (1, KV_LORA), (1, HEAD_PAD), (1, HEAD_PAD)]
    out_rows = [(D_MODEL, F32), (D_MODEL, BF16), (Z_WIDTH, BF16), (Q_LORA, BF16), (HEADS * HEAD_PAD, BF16),
                (KV_LORA, BF16), (2 * HEADS * HEAD_PAD, BF16)]
    return pl.pallas_call(
        body, grid=(T // tm,), name="mixin_bwd",
        in_specs=[row(D_MODEL), row(D_MODEL), row(Z_WIDTH)] + [row(HEAD_PAD)] * 3 + [head_t, head, head, row(256), row(256), row(256)]
                 + [WHOLE] * 8,
        out_specs=[row(w) for w, _ in out_rows] + [_acc_spec(s) for s in acc_shapes],
        out_shape=[jax.ShapeDtypeStruct((T, w), dt) for w, dt in out_rows] + [jax.ShapeDtypeStruct(s, F32) for s in acc_shapes],
        compiler_params=_cparams(),
    )(dres, x, z, *rope, dqt, dk, dv, du, dvs, dpin, gmix, win, gql, wq, gkvl, wkv, gqh, gkh)


def _place():
    x, y, c = lax.axis_index("x"), lax.axis_index("y"), lax.axis_index("c")
    return x, y, c, 4 * x + 2 * y + c


def _layer_of(ref, l):
    return ref[:, l, :] if ref.shape[1] == DEPTH else ref[l]


def _layer_shape(shard):
    return (shard.shape[0], shard.shape[2]) if shard.shape[1] == DEPTH else shard.shape[1:]


def cast_shards(shards, wanted):
    n = len(shards)

    def body(*refs):
        for o_ref, (w, l) in zip(refs[n:], wanted):
            o_ref[...] = _layer_of(refs[w], l).astype(BF16)

    return pl.pallas_call(
        body, name="cast_shards", in_specs=[WHOLE] * n, out_specs=[WHOLE] * len(wanted),
        out_shape=[jax.ShapeDtypeStruct(_layer_shape(shards[w]), BF16) for w, _ in wanted],
        compiler_params=_cparams(),
    )(*shards)


def _peer(k):
    x, y, c, _ = _place()
    px = 1 - x if k & 4 else x
    py = 1 - y if k & 2 else y
    pc = 1 - c if k & 1 else c
    return (px, py, pc), 4 * px + 2 * py + pc


def exchange_start(srcs, after, gather, name):
    n = len(srcs)
    land_shapes = [((N_DEV,) + s.shape) if gather else s.shape for s in srcs]

    def body(*refs):
        src_refs, land_refs = refs[:n], refs[n:2 * n]
        send_sems, recv_sems = refs[2 * n + 1:3 * n + 1], refs[3 * n + 1:4 * n + 1]
        token = refs[-1]
        _, _, _, me = _place()
        for k in range(1, N_DEV):
            peer, peer_id = _peer(k)
            for w in range(n):
                pltpu.make_async_remote_copy(
                    src_ref=src_refs[w] if gather else src_refs[w].at[peer_id], dst_ref=land_refs[w].at[me],
                    send_sem=send_sems[w], recv_sem=recv_sems[w], device_id=peer, device_id_type=MESH).start()
        for w in range(n):
            pltpu.make_async_copy(src_refs[w] if gather else src_refs[w].at[me], land_refs[w].at[me], recv_sems[w]).start()
        token[...] = jnp.zeros(token.shape, F32)

    hbm = lambda a: pltpu.with_memory_space_constraint(a, pltpu.HBM)
    outs = pl.pallas_call(
        body, name=name,
        out_shape=(pltpu.SemaphoreType.DMA(()),) * (2 * n)
                  + tuple(pltpu.HBM(s.shape, s.dtype) for s in srcs)
                  + tuple(pltpu.HBM(l, s.dtype) for l, s in zip(land_shapes, srcs))
                  + (jax.ShapeDtypeStruct((8, 128), F32),),
        in_specs=[HBM_SPEC] * (2 * n) + [ANY],
        out_specs=(SEM_SPEC,) * (2 * n) + (HBM_SPEC,) * (2 * n) + (WHOLE,),
        input_output_aliases={i: 2 * n + i for i in range(2 * n)},
        compiler_params=pltpu.CompilerParams(has_side_effects=pltpu.SideEffectType.DATAFLOW_SIDE_EFFECTING),
    )(*[hbm(s) for s in srcs], *[hbm(lax.empty(l, s.dtype)) for l, s in zip(land_shapes, srcs)], after)
    return list(outs[:n]), list(outs[n:2 * n]), list(outs[2 * n:3 * n]), list(outs[3 * n:4 * n]), outs[-1]


def exchange_wait(started, after, name):
    send_sems, recv_sems, srcs, lands, _ = started
    n = len(srcs)

    def body(*refs):
        land_refs = refs[n:2 * n]
        send_sems, recv_sems = refs[2 * n:3 * n], refs[3 * n:4 * n]
        x, y, c, _ = _place()
        for w in range(n):
            seven, eight = land_refs[w].at[pl.ds(0, N_DEV - 1)], land_refs[w]
            pltpu.make_async_remote_copy(src_ref=seven, dst_ref=seven, send_sem=send_sems[w], recv_sem=recv_sems[w],
                                         device_id=(x, y, c), device_id_type=MESH).wait_send()
            pltpu.make_async_remote_copy(src_ref=eight, dst_ref=eight, send_sem=send_sems[w], recv_sem=recv_sems[w],
                                         device_id=(x, y, c), device_id_type=MESH).wait_recv()

    outs = pl.pallas_call(
        body, name=name,
        out_shape=tuple(pltpu.HBM(a.shape, a.dtype) for a in (*srcs, *lands)),
        in_specs=[HBM_SPEC] * (2 * n) + [SEM_SPEC] * (2 * n) + [ANY],
        out_specs=(HBM_SPEC,) * (2 * n),
        input_output_aliases={i: i for i in range(2 * n)},
        compiler_params=pltpu.CompilerParams(has_side_effects=pltpu.SideEffectType.DATAFLOW_SIDE_EFFECTING),
    )(*srcs, *lands, *send_sems, *recv_sems, after)
    return list(outs[n:])


def _adamw(w, g, m, v):
    m2 = ADAM_B1 * m + (1.0 - ADAM_B1) * g
    v2 = ADAM_B2 * v + (1.0 - ADAM_B2) * (g * g)
    m_hat = m2 / (1.0 - ADAM_B1 ** ADAM_STEP)
    v_hat = v2 / (1.0 - ADAM_B2 ** ADAM_STEP)
    delta = -ADAM_LR * (m_hat / (jnp.sqrt(v_hat) + ADAM_EPS) + ADAM_WD * w)
    return delta, m2, v2


def adamw_sharded(parts, w, m, v, name):
    L, R, C = w.shape
    fits = [d for d in range(16, min(R, 512) + 1, 16) if R % d == 0]
    br = max(fits) if fits else R
    nblk = R // br

    def body(*refs):
        p_refs = refs[:L]
        w_ref, m_ref, v_ref, g_ref, d_ref, m2_ref, v2_ref = refs[L:]

        def total(p_ref):
            g = p_ref[0].astype(F32)
            for s in range(1, N_DEV):
                g = g + p_ref[s].astype(F32)
            return g

        g = total(p_refs[0])
        for l in range(1, L):
            g = jnp.where(pl.program_id(0) == l, total(p_refs[l]), g)
        g_ref[...] = g
        d_ref[...], m2_ref[...], v2_ref[...] = _adamw(w_ref[...], g, m_ref[...], v_ref[...])

    blk = pl.BlockSpec((None, br, C), lambda l, i: (l, i, 0))

    def part_spec(k):
        return pl.BlockSpec((N_DEV, br, C), lambda l, i: (0, jnp.where(l == k, i, jnp.where(l < k, 0, nblk - 1)), 0))

    return pl.pallas_call(
        body, grid=(L, nblk), name=name,
        in_specs=[part_spec(k) for k in range(L)] + [blk, blk, blk],
        out_specs=[blk] * 4,
        out_shape=[jax.ShapeDtypeStruct((L, R, C), F32)] * 4,
        compiler_params=_cparams(),
    )(*parts, w, m, v)


def adamw_small(gparts, w, m, v):
    R = gparts.shape[1]

    def body(g_ref, w_ref, m_ref, v_ref, grad_ref, d_ref, m2_ref, v2_ref):
        g = g_ref[0]
        for s in range(1, N_DEV):
            g = g + g_ref[s]
        grad_ref[...] = g
        d_ref[...], m2_ref[...], v2_ref[...] = _adamw(w_ref[...], g, m_ref[...], v_ref[...])

    return pl.pallas_call(
        body, name="adamw_small",
        in_specs=[WHOLE] * 4, out_specs=[WHOLE] * 4,
        out_shape=[jax.ShapeDtypeStruct((R, 128), F32)] * 4,
        compiler_params=_cparams(),
    )(gparts, w, m, v)


def _shard_view(name, a):
    if name == "w_in":
        return a.transpose(2, 0, 1)
    return a.swapaxes(1, 2) if name in TRANSPOSED else a


def _shard_unview(name, a):
    if name == "w_in":
        return a.transpose(1, 2, 0)
    return a.swapaxes(1, 2) if name in TRANSPOSED else a


def _pad_head_rows(w, width):
    c = w.shape[1]
    return jnp.pad(w.reshape(HEADS, width, c), ((0, 0), (0, HEAD_PAD - width), (0, 0))).reshape(HEADS * HEAD_PAD, c)


def _unpad_head_rows(w, width):
    c = w.shape[1]
    return w.reshape(HEADS, HEAD_PAD, c)[:, :width].reshape(HEADS * width, c)


O1 = Q_LORA
O2 = O1 + KV_LORA
O3 = O2 + QK_ROPE


def _mixer_weights(gw):
    w_in = gw["w_in"].reshape(IN_WIDTH, D_MODEL)
    zero = lambda n: jnp.zeros((n, D_MODEL), BF16)
    win = jnp.concatenate([w_in[:O2], zero(QK_NOPE), w_in[O2:O3], zero(HEAD_PAD - QK_HEAD), w_in[O3:]], axis=0)
    wq = _pad_head_rows(gw["w_q_up"].reshape(HEADS * QK_HEAD, Q_LORA), QK_HEAD)
    w_kv = gw["w_kv_up"].reshape(HEADS, QK_NOPE + V_HEAD, KV_LORA)
    wk = jnp.pad(w_kv[:, :QK_NOPE], ((0, 0), (0, HEAD_PAD - QK_NOPE), (0, 0))).reshape(HEADS * HEAD_PAD, KV_LORA)
    wv = w_kv[:, QK_NOPE:].reshape(HEADS * V_HEAD, KV_LORA)
    wkv = jnp.concatenate([wk, wv], axis=0)
    out = dict(win=win, wq=wq, wkv=wkv)
    if "w_out" in gw:
        out["wout"] = gw["w_out"].reshape(D_MODEL, D_MODEL)
    return out


def _ffn_weights(gw):
    return dict(wg=gw["w_gate"].reshape(FFN_HIDDEN, D_MODEL), wu=gw["w_up"].reshape(FFN_HIDDEN, D_MODEL),
                wd=gw["w_down"].reshape(FFN_HIDDEN, D_MODEL))


def _layer_small(p, l):
    row = lambda a: a.reshape(1, -1)
    pad_head = lambda g: jnp.pad(g, (0, HEAD_PAD - QK_HEAD)).reshape(1, HEAD_PAD)
    tril = jnp.tril(jnp.ones((CHUNK, CHUNK), F32))
    wsp = p["w_spatial"][l] * tril
    wcat = jnp.concatenate([wsp[h] for h in range(HEADS)], axis=1).astype(BF16)
    wcat_t = jnp.concatenate([wsp[h].T for h in range(HEADS)], axis=1).astype(BF16)
    bz = jnp.repeat(p["b_spatial"][l].T, SGU_HEAD_DIM, axis=1)
    wp = jax.scipy.linalg.block_diag(*[p["w_pool"][l][g] for g in range(HEADS)]).astype(BF16)
    return dict(gmix=row(p["g_mix_norm"][l]), gql=row(p["g_q_lat"][l]), gkvl=row(p["g_kv_lat"][l]),
                gqh=pad_head(p["g_q_head"][l]), gkh=pad_head(p["g_k_head"][l]), gsgu=row(p["g_sgu_v"][l]),
                wcat=wcat, wcat_t=wcat_t, bz=bz, wp=wp, pscale=row(p["pool_scale"][l]),
                goa=row(p["g_out_mla"][l]), gos=row(p["g_out_sgu"][l]), gop=row(p["g_out_pool"][l]),
                gffn=row(p["g_ffn_norm"][l]))


MIXER_SIDE = ("w_in", "w_q_up", "w_kv_up", "w_out")
FFN_SIDE = ("w_gate", "w_up", "w_down")
TRANSPOSED = ("w_in", "w_q_up", "w_kv_up", "w_gate", "w_up")
FIRST_0 = ("w_in", "w_q_up", "w_kv_up")
LATER_0 = ("w_out",) + FFN_SIDE
SHARDED = MIXER_SIDE + FFN_SIDE
SMALL = ("g_mix_norm", "g_q_lat", "g_kv_lat", "g_q_head", "g_k_head", "g_sgu_v", "w_spatial", "b_spatial", "w_pool",
         "pool_scale", "g_out_mla", "g_out_sgu", "g_out_pool", "g_ffn_norm")
WEIGHTS = ("g_mix_norm", "w_in", "g_q_lat", "w_q_up", "g_kv_lat", "w_kv_up", "g_q_head", "g_k_head", "g_sgu_v", "w_spatial",
           "b_spatial", "w_pool", "pool_scale", "g_out_mla", "g_out_sgu", "g_out_pool", "w_out", "g_ffn_norm", "w_gate",
           "w_up", "w_down")
PACKED = SMALL + ("loss",)
PACK_ROWS = 8 * 128


def _pack_small(parts):
    flat = []
    for name in PACKED:
        a = parts[name].reshape(-1)
        flat.append(jnp.pad(a, (0, -a.shape[0] % PACK_ROWS)))
    return jnp.concatenate(flat).reshape(-1, 128)


def _unpack_small(packed, like):
    out, row = {}, 0
    for name in PACKED:
        n = math.prod(like[name].shape)
        rows = -(-n // PACK_ROWS) * 8
        out[name] = packed[row:row + rows].reshape(-1)[:n].reshape(like[name].shape)
        row += rows
    return out


def _forward_attention(x, rope, W, S):
    z, q, k, v, kt, vt = mixin_fwd(x, rope, S["gmix"], W["win"], S["gql"], W["wq"], S["gkvl"], W["wkv"], S["gqh"], S["gkh"])
    a, lse = attn_fwd(q, k, vt)
    return dict(x=x, z=z, q=q, k=k, kt=kt, v=v, a=a, lse=lse)


def _forward_mixers(A, wout, S):
    A["x1"] = mixers_fwd(A["x"], A["a"], A["z"], S["gsgu"], S["wcat"], S["bz"], S["wp"], S["pscale"], S["goa"], S["gos"],
                         S["gop"], wout)
    return A["x1"]


def _backward_ffn(dx2, W, S, A, l):
    dx1, h2, act, dg, du_ffn, dgffn = ffn_bwd(dx2, A["x1"], A["gs"], A["us"], S["gffn"], W["wg"], W["wu"], W["wd"])
    d_wd = matmul_tn(act, dx2, f"dw_down_{l}")
    d_wg = matmul_tn(dg, h2, f"dw_gate_{l}")
    d_wu = matmul_tn(du_ffn, h2, f"dw_up_{l}")
    big = {n: d.reshape(N_DEV, -1, D_MODEL) for n, d in (("w_gate", d_wg), ("w_up", d_wu), ("w_down", d_wd))}
    return dx1, big, dgffn


def _backward_mixers(dx1, dgffn, rope, W, S, A, l, send_wout=None):
    (da, delta, du, dvs, dm, mix, dgoa, dgos, dgop, dps, dwp, dwsp, db, dgsgu) = mixers_bwd(
        dx1, A["a"], A["z"], S["gsgu"], S["wcat"], S["wcat_t"], S["bz"], S["wp"], S["pscale"], S["goa"], S["gos"],
        S["gop"], W["wout"])
    d_wout = matmul_tn(mix, dx1, f"dw_out_{l}")
    if send_wout is not None:
        delta = delta + send_wout(d_wout.reshape(N_DEV, -1, D_MODEL))[0, 0]
    dpin = pool_bwd(dm)
    dqt, dk, dv = attn_bwd(A["q"], A["k"], A["kt"], A["v"], da, A["lse"], delta)
    (dx, hn, dz, qn, dq_up, kvn, dkv_up, dgmix, dgql, dgkvl, dgqh, dgkh) = mixin_bwd(
        dx1, A["x"], A["z"], rope, dqt, dk, dv, du, dvs, dpin, S["gmix"], W["win"], S["gql"], W["wq"], S["gkvl"],
        W["wkv"], S["gqh"], S["gkh"])
    d_win = matmul_tn(dz, hn, f"dw_in_{l}")
    d_wq = matmul_tn(dq_up, qn, f"dw_q_up_{l}")
    d_wkv = matmul_tn(dkv_up, kvn, f"dw_kv_up_{l}")
    d_win = jnp.concatenate([d_win[:O2], d_win[O2 + QK_NOPE:O2 + QK_HEAD], d_win[O2 + HEAD_PAD:]], axis=0)
    d_wk = d_wkv[:HEADS * HEAD_PAD].reshape(HEADS, HEAD_PAD, KV_LORA)[:, :QK_NOPE]
    d_wv = d_wkv[HEADS * HEAD_PAD:].reshape(HEADS, V_HEAD, KV_LORA)
    d_wkv = jnp.concatenate([d_wk, d_wv], axis=1)
    big = dict(w_in=d_win.reshape(N_DEV, -1, D_MODEL), w_q_up=_unpad_head_rows(d_wq, QK_HEAD).reshape(N_DEV, -1, Q_LORA),
               w_kv_up=d_wkv.reshape(N_DEV, -1, KV_LORA), w_out=d_wout.reshape(N_DEV, -1, D_MODEL))
    tril = jnp.tril(jnp.ones((CHUNK, CHUNK), F32))
    small = dict(g_mix_norm=dgmix[0], g_q_lat=dgql[0], g_kv_lat=dgkvl[0], g_q_head=dgqh[0, :QK_HEAD], g_k_head=dgkh[0, :QK_HEAD],
                 g_sgu_v=dgsgu[0], w_spatial=dwsp.reshape(HEADS, CHUNK, CHUNK) * tril, b_spatial=db[:, :HEADS].T,
                 w_pool=jnp.stack([dwp[64 * g:64 * (g + 1), 64 * g:64 * (g + 1)] for g in range(HEADS)]),
                 pool_scale=dps[0], g_out_mla=dgoa[0], g_out_sgu=dgos[0], g_out_pool=dgop[0], g_ffn_norm=dgffn[0])
    return dx, big, small


def kernel(x, positions, g_mix_norm, w_in, g_q_lat, w_q_up, g_kv_lat, w_kv_up, g_q_head, g_k_head, g_sgu_v, w_spatial, b_spatial, w_pool, pool_scale, g_out_mla, g_out_sgu, g_out_pool, w_out, g_ffn_norm, w_gate, w_up, w_down, loss_target, m_g_mix_norm, m_w_in, m_g_q_lat, m_w_q_up, m_g_kv_lat, m_w_kv_up, m_g_q_head, m_g_k_head, m_g_sgu_v, m_w_spatial, m_b_spatial, m_w_pool, m_pool_scale, m_g_out_mla, m_g_out_sgu, m_g_out_pool, m_w_out, m_g_ffn_norm, m_w_gate, m_w_up, m_w_down, v_g_mix_norm, v_w_in, v_g_q_lat, v_w_q_up, v_g_kv_lat, v_w_kv_up, v_g_q_head, v_g_k_head, v_g_sgu_v, v_w_spatial, v_b_spatial, v_w_pool, v_pool_scale, v_g_out_mla, v_g_out_sgu, v_g_out_pool, v_w_out, v_g_ffn_norm, v_w_gate, v_w_up, v_w_down):
    given = dict(locals())
    w = {n: given[n] for n in WEIGHTS}
    m = {n: given["m_" + n] for n in WEIGHTS}
    v = {n: given["v_" + n] for n in WEIGHTS}
    T = x.shape[1]
    xs = x.reshape(T, D_MODEL)

    half = QK_ROPE // 2
    inv_freq = 1.0 / (ROPE_THETA ** (jnp.arange(half, dtype=F32) / half))
    ang16 = positions.reshape(T).astype(F32)[:, None] * inv_freq
    ang = jnp.concatenate([jnp.zeros((T, QK_NOPE), F32), ang16, ang16, jnp.zeros((T, HEAD_PAD - QK_HEAD), F32)], axis=1)

    wv = {n: _shard_view(n, w[n]) for n in SHARDED}
    wanted = [(SHARDED.index(n), 0) for n in FIRST_0 + LATER_0] + [(i, 1) for i in range(len(SHARDED))]
    bf = cast_shards([wv[n] for n in SHARDED], wanted)
    nf, nl = len(FIRST_0), len(LATER_0)
    ag_first0 = exchange_start(bf[:nf], ang, True, "ag_start_first0")
    rope = rope_tables(ang + ag_first0[-1][0, 0])
    W0 = dict(zip(FIRST_0, exchange_wait(ag_first0, rope[0], "ag_wait_first0")))
    ag_later0 = exchange_start(bf[nf:nf + nl], W0["w_in"], True, "ag_start_later0")
    ag_l1 = exchange_start(bf[nf + nl:], ag_later0[-1], True, "ag_start_l1")
    Ss = [_layer_small(w, l) for l in range(DEPTH)]

    S0 = dict(Ss[0], gmix=Ss[0]["gmix"] + (ag_later0[-1][0, 0] + ag_l1[-1][0, 0]))
    A0 = _forward_attention(xs, rope, _mixer_weights(W0), S0)
    W0.update(zip(LATER_0, exchange_wait(ag_later0, A0["a"], "ag_wait_later0")))
    W0 = {**_mixer_weights(W0), **_ffn_weights(W0)}
    x1 = _forward_mixers(A0, W0["wout"], Ss[0])
    h, A0["gs"], A0["us"] = ffn_fwd(x1, Ss[0]["gffn"], W0["wg"], W0["wu"], W0["wd"])
    layer1 = dict(zip(SHARDED, exchange_wait(ag_l1, h, "ag_wait_l1")))
    W1 = {**_mixer_weights(layer1), **_ffn_weights(layer1)}
    A1 = _forward_attention(h, rope, W1, Ss[1])
    x1 = _forward_mixers(A1, W1["wout"], Ss[1])
    dh, loss_part, A1["gs"], A1["us"] = ffn_fwd(x1, Ss[1]["gffn"], W1["wg"], W1["wu"], W1["wd"],
                                                target=loss_target.reshape(T, D_MODEL))

    dx1, big_f1, dgffn1 = _backward_ffn(dh, W1, Ss[1], A1, 1)
    dh, big_m1, small1 = _backward_mixers(dx1, dgffn1, rope, W1, Ss[1], A1, 1)
    rs_l1 = exchange_start([{**big_m1, **big_f1}[n] for n in SHARDED], dh, False, "rs_start_l1")
    S0 = dict(Ss[0], gffn=Ss[0]["gffn"] + rs_l1[-1][0, 0])
    dx1, big_f0, dgffn0 = _backward_ffn(dh, W0, S0, A0, 0)
    rs_ffn0 = exchange_start([big_f0[n] for n in FFN_SIDE], dx1, False, "rs_start_ffn0")
    S0 = dict(Ss[0], gsgu=Ss[0]["gsgu"] + rs_ffn0[-1][0, 0])
    rs_out0 = []

    def send_wout(pieces):
        rs_out0.append(exchange_start([pieces], dx1, False, "rs_start_out0"))
        return rs_out0[0][-1]

    dh, big_m0, small0 = _backward_mixers(dx1, dgffn0, rope, W0, S0, A0, 0, send_wout)
    grad_x = dh.reshape(x.shape)
    smalls = [small0, small1]

    rs_first0 = exchange_start([big_m0[n] for n in FIRST_0], dh, False, "rs_start_first0")
    small_part = {n: jnp.stack([smalls[l][n] for l in range(DEPTH)]) for n in SMALL}
    small_part["loss"] = loss_part
    ag_small = exchange_start([_pack_small(small_part)], rs_first0[-1], True, "ag_start_small")
    parts1 = dict(zip(SHARDED, exchange_wait(rs_l1, ag_small[-1], "rs_wait_l1")))
    parts0 = dict(zip(FFN_SIDE, exchange_wait(rs_ffn0, ag_small[-1], "rs_wait_ffn0")))
    parts0["w_out"] = exchange_wait(rs_out0[0], ag_small[-1], "rs_wait_out0")[0]
    grad, delta, new_m, new_v = {}, {}, {}, {}

    for n in LATER_0 + FIRST_0:
        if n == FIRST_0[0]:
            parts0.update(zip(FIRST_0, exchange_wait(rs_first0, grad["w_down"], "rs_wait_first0")))
        parts, views = [parts0[n], parts1[n]], [_shard_view(n, a[n]) for a in (w, m, v)]
        if n == "w_in":
            parts = [jnp.stack(parts, axis=2).reshape(N_DEV, -1, D_MODEL)]
            views = [a.reshape(1, -1, D_MODEL) for a in views]
        outs = adamw_sharded(parts, *views, f"adamw_{n}")
        if n == "w_in":
            outs = [o.reshape(-1, DEPTH, D_MODEL) for o in outs]
        grad[n], delta[n], new_m[n], new_v[n] = [_shard_unview(n, o) for o in outs]

    no_loss = {"loss": jnp.zeros((1, 1), F32)}
    gparts = exchange_wait(ag_small, outs[0], "ag_wait_small")[0]
    for d, o in zip((grad, delta, new_m, new_v), adamw_small(gparts, *[_pack_small({**a, **no_loss}) for a in (w, m, v)])):
        d.update(_unpack_small(o, {**w, **no_loss}))
    return (grad["loss"][0, 0], grad_x, *[grad[n] for n in WEIGHTS], *[delta[n] for n in WEIGHTS], *[new_m[n] for n in WEIGHTS],
            *[new_v[n] for n in WEIGHTS])
```

```python
import math

import jax
import jax.numpy as jnp
from jax import lax
from jax.experimental import pallas as pl
from jax.experimental.pallas import tpu as pltpu

F32 = jnp.float32
BF16 = jnp.bfloat16

N_DEV = 8
DEPTH = 2
D_MODEL = 1024
HEADS = 4
HEAD_PAD = 128
QK_NOPE = 64
QK_ROPE = 32
QK_HEAD = QK_NOPE + QK_ROPE
V_HEAD = 128
Q_LORA = 256
KV_LORA = 128
SGU_WIDTH = 256
SGU_HEAD_DIM = 64
CHUNK = 128
POOL_WIDTH = 256
POOL_HALO = 16
MLA_WIDTH = 512
IN_WIDTH = 1184
Z_WIDTH = 1280
FFN_HIDDEN = 2816
FFN_CHUNK = 256
ROPE_THETA = 10000.0
EPS = 1e-6
ATTN_SCALE = 1.0 / math.sqrt(QK_HEAD)
LOG2E = 1.4426950408889634
NEG_BIG = -1e30

ADAM_LR = 0.001
ADAM_B1 = 0.9
ADAM_B2 = 0.999
ADAM_EPS = 1e-08
ADAM_WD = 0.01
ADAM_STEP = 10

VMEM_LIMIT = 56 * 1024 * 1024
ROW_TILE = 512
MIXIN_PART = 256
MIXIN_BWD_PART = 128
ATTN_TILE = 1024
ATTN_SUB = 512
ATTN_FWD_SUB = 512
ATTN_BWD_HEADS = 2
MESH = pl.DeviceIdType.MESH

WHOLE = pl.BlockSpec(memory_space=pltpu.VMEM)
ANY = pl.BlockSpec(memory_space=pl.ANY)
HBM_SPEC = pl.BlockSpec(memory_space=pltpu.HBM)
SEM_SPEC = pl.BlockSpec(memory_space=pltpu.SEMAPHORE)


def _cparams(**kw):
    return pltpu.CompilerParams(vmem_limit_bytes=VMEM_LIMIT, **kw)


def _dot(a, b):
    return jnp.dot(a, b, preferred_element_type=F32)


def _dot_nt(a, b):
    return lax.dot_general(a, b, (((1,), (1,)), ((), ())), preferred_element_type=F32)


def _dot_tn(a, b):
    return lax.dot_general(a, b, (((0,), (0,)), ((), ())), preferred_element_type=F32)


def _rms(x, g, n):
    r = lax.rsqrt(jnp.sum(x * x, axis=-1, keepdims=True) * (1.0 / n) + EPS)
    return x * r * g, r


def _rms_bwd(x, r, g, dy, n):
    gdy = dy * g
    dx = r * gdy - x * (r * r * r) * (jnp.sum(x * gdy, axis=-1, keepdims=True) * (1.0 / n))
    dg = jnp.sum(dy * (x * r), axis=0, keepdims=True)
    return dx, dg


def _sigmoid(x):
    return 1.0 / (1.0 + jnp.exp(-x))


def rope_tables(ang):
    T = ang.shape[0]
    tm = min(ROW_TILE, T)

    def body(ang_ref, c_ref, sa_ref, sb_ref):
        a = ang_ref[...]
        lane = lax.broadcasted_iota(jnp.int32, a.shape, 1)
        s = jnp.sin(a)
        c_ref[...] = jnp.cos(a)
        sa_ref[...] = jnp.where(lane < QK_NOPE + QK_ROPE // 2, -s, 0.0)
        sb_ref[...] = jnp.where(lane >= QK_NOPE + QK_ROPE // 2, s, 0.0)

    row = pl.BlockSpec((tm, HEAD_PAD), lambda i: (i, 0))
    return pl.pallas_call(
        body, grid=(T // tm,), name="rope_tables", in_specs=[row], out_specs=[row] * 3,
        out_shape=[jax.ShapeDtypeStruct((T, HEAD_PAD), F32)] * 3, compiler_params=_cparams(),
    )(ang)


def _rope(x, c, sa, sb):
    half = QK_ROPE // 2
    return x * c + pltpu.roll(x, HEAD_PAD - half, 1) * sa + pltpu.roll(x, half, 1) * sb


def _rope_bwd(dy, c, sa, sb):
    half = QK_ROPE // 2
    return dy * c + pltpu.roll(dy * sa, half, 1) + pltpu.roll(dy * sb, HEAD_PAD - half, 1)


def _skewed(stages, groups):
    for t in range(len(stages) + len(groups) - 1):
        for p, g in enumerate(groups):
            if 0 <= t - p < len(stages):
                stages[t - p](g)


def _head_masks(shape, width):
    lane = lax.broadcasted_iota(jnp.int32, shape, len(shape) - 1)
    return [(lane >= width * h) & (lane < width * (h + 1)) for h in range(HEADS)]


def _qkv_pre(z, gql, wq, gkvl, wkv):
    ql = z[:, 0:Q_LORA]
    kvl = z[:, Q_LORA:Q_LORA + KV_LORA]
    kr = z[:, Q_LORA + KV_LORA:Q_LORA + KV_LORA + HEAD_PAD]
    qn, rq = _rms(ql, gql, Q_LORA)
    kvn, rkv = _rms(kvl, gkvl, KV_LORA)
    qn = qn.astype(BF16)
    kvn = kvn.astype(BF16)
    q_up = _dot_nt(qn, wq)
    kv_up = _dot_nt(kvn, wkv)
    return ql, kvl, kr, qn, rq, kvn, rkv, q_up, kv_up


def mixin_fwd(x, rope, gmix, win, gql, wq, gkvl, wkv, gqh, gkh):
    T = x.shape[0]
    tm = min(2 * ROW_TILE, T)
    part = min(MIXIN_PART, tm)

    def body(x_ref, c_ref, sa_ref, sb_ref, gmix_ref, win_ref, gql_ref, wq_ref, gkvl_ref, wkv_ref, gqh_ref, gkh_ref,
             z_ref, q_ref, k_ref, v_ref, kt_ref, vt_ref):
        def project(g):
            hn = _rms(x_ref[g["rows"], :], gmix_ref[...], D_MODEL)[0].astype(BF16)
            g["z"] = _dot_nt(hn, win_ref[...])
            z_ref[g["rows"], :] = g["z"]

        def latents(g):
            _, _, g["kr"], _, _, _, _, g["q_up"], g["kv_up"] = _qkv_pre(g["z"], gql_ref[...], wq_ref[...], gkvl_ref[...],
                                                                        wkv_ref[...])

        def heads(g):
            rows = g["rows"]
            c, sa, sb = c_ref[rows, :], sa_ref[rows, :], sb_ref[rows, :]
            for h in range(HEADS):
                lo = HEAD_PAD * h
                qh = _rms(g["q_up"][:, lo:lo + HEAD_PAD], gqh_ref[...], QK_HEAD)[0]
                q_ref[h, rows, :] = (_rope(qh, c, sa, sb) * (ATTN_SCALE * LOG2E)).astype(BF16)
                kh = _rope(_rms(g["kv_up"][:, lo:lo + HEAD_PAD] + g["kr"], gkh_ref[...], QK_HEAD)[0], c, sa, sb)
                k_ref[h, rows, :] = kh.astype(BF16)
                kt_ref[h, :, rows] = jnp.transpose(kh).astype(BF16)
                vh = g["kv_up"][:, HEADS * HEAD_PAD + lo:HEADS * HEAD_PAD + lo + HEAD_PAD]
                v_ref[h, rows, :] = vh.astype(BF16)
                vt_ref[h, :, rows] = jnp.transpose(vh).astype(BF16)

        _skewed((project, latents, heads), [dict(rows=pl.ds(part * p, part)) for p in range(tm // part)])

    row = lambda w: pl.BlockSpec((tm, w), lambda i: (i, 0))
    head = pl.BlockSpec((HEADS, tm, HEAD_PAD), lambda i: (0, i, 0))
    head_t = pl.BlockSpec((HEADS, HEAD_PAD, tm), lambda i: (0, 0, i))
    return pl.pallas_call(
        body, grid=(T // tm,), name="mixin_fwd",
        in_specs=[row(D_MODEL)] + [row(HEAD_PAD)] * 3 + [WHOLE] * 8,
        out_specs=[row(Z_WIDTH), head, head, head, head_t, head_t],
        out_shape=[jax.ShapeDtypeStruct((T, Z_WIDTH), F32)] + [jax.ShapeDtypeStruct((HEADS, T, HEAD_PAD), BF16)] * 3
                  + [jax.ShapeDtypeStruct((HEADS, HEAD_PAD, T), BF16)] * 2,
        compiler_params=_cparams(),
    )(x, *rope, gmix, win, gql, wq, gkvl, wkv, gqh, gkh)


def _pair_tables(pairs):
    return (jnp.asarray([p[0] for p in pairs], jnp.int32), jnp.asarray([p[1] for p in pairs], jnp.int32))


def attn_fwd(q, k, vt):
    _, T, _ = q.shape
    tb = min(ATTN_TILE, T)
    sq = min(ATTN_FWD_SUB, tb)
    nb = T // tb

    pairs = [(i, j) for i in range(nb) for j in range(i + 1)]

    def body(i_tab, j_tab, q_ref, k_ref, vt_ref, o_ref, lse_ref, m_s, l_s, acc_s):
        i, j = i_tab[pl.program_id(0)], j_tab[pl.program_id(0)]

        @pl.when(j == 0)
        def _():
            m_s[...] = jnp.full(m_s.shape, -jnp.inf, F32)
            l_s[...] = jnp.zeros(l_s.shape, F32)
            acc_s[...] = jnp.zeros(acc_s.shape, F32)

        def scores(g):
            st = _dot_nt(k_ref[g["h"], g["kb"], :], q_ref[g["h"], g["qa"], :])
            if g["masked"]:
                krow = lax.broadcasted_iota(jnp.int32, st.shape, 0)
                qcol = g["q0"] + lax.broadcasted_iota(jnp.int32, st.shape, 1)
                st = jnp.where(krow <= qcol, st, NEG_BIG)
            g["st"] = st

        def new_max(g):
            h, qa = g["h"], g["qa"]
            m_prev = m_s[h, :, qa]
            g["m"] = jnp.maximum(m_prev, jnp.max(g["st"], axis=0, keepdims=True))
            g["alpha"] = jnp.exp2(m_prev - g["m"])
            m_s[h, :, qa] = g["m"]

        def weights(g):
            pt = jnp.exp2(g["st"] - g["m"])
            g["lsum"] = jnp.sum(pt, axis=0, keepdims=True)
            g["pt"] = pt.astype(BF16)

        def accumulate(g):
            h, qa = g["h"], g["qa"]
            l_s[h, :, qa] = g["alpha"] * l_s[h, :, qa] + g["lsum"]
            acc_s[h, :, qa] = g["alpha"] * acc_s[h, :, qa] + _dot(vt_ref[h, :, g["kb"]], g["pt"])

        def tiles(masked):
            return [dict(h=h, q0=q0, qa=slice(q0, q0 + sq), kb=slice(0, q0 + sq if masked else tb), masked=masked)
                    for h in range(HEADS) for q0 in range(0, tb, sq)]

        @pl.when(j < i)
        def _():
            _skewed((scores, new_max, weights, accumulate), tiles(False))

        @pl.when(j == i)
        def _():
            _skewed((scores, new_max, weights, accumulate), tiles(True))
            for h in range(HEADS):
                l = l_s[h]
                o_ref[:, HEAD_PAD * h:HEAD_PAD * (h + 1)] = jnp.transpose(acc_s[h] / l)
                lse_ref[h] = m_s[h] + jnp.log2(l)

    qspec = pl.BlockSpec((HEADS, tb, HEAD_PAD), lambda p, it, jt: (0, it[p], 0))
    kspec = pl.BlockSpec((HEADS, tb, HEAD_PAD), lambda p, it, jt: (0, jt[p], 0))
    vspec = pl.BlockSpec((HEADS, HEAD_PAD, tb), lambda p, it, jt: (0, 0, jt[p]))
    grid_spec = pltpu.PrefetchScalarGridSpec(
        num_scalar_prefetch=2, grid=(len(pairs),),
        in_specs=[qspec, kspec, vspec],
        out_specs=[pl.BlockSpec((tb, MLA_WIDTH), lambda p, it, jt: (it[p], 0)),
                   pl.BlockSpec((HEADS, 1, tb), lambda p, it, jt: (0, 0, it[p]))],
        scratch_shapes=[pltpu.VMEM((HEADS, 1, tb), F32), pltpu.VMEM((HEADS, 1, tb), F32), pltpu.VMEM((HEADS, HEAD_PAD, tb), F32)])
    return pl.pallas_call(
        body, grid_spec=grid_spec, name="attn_fwd",
        out_shape=[jax.ShapeDtypeStruct((T, MLA_WIDTH), F32), jax.ShapeDtypeStruct((HEADS, 1, T), F32)],
        compiler_params=_cparams(),
    )(*_pair_tables(pairs), q, k, vt)


def _sgu_fwd_chunk(vn_c, wcat, bz, masks):
    vstack = jnp.concatenate([jnp.where(mk, vn_c, 0.0).astype(BF16) for mk in masks], axis=0)
    return _dot(wcat, vstack) + bz


def _pool_counts(i, tm):
    pos1 = (i * tm + 1 + lax.broadcasted_iota(jnp.int32, (tm, POOL_WIDTH), 0)).astype(F32)
    lane = lax.broadcasted_iota(jnp.int32, (tm, POOL_WIDTH), 1)
    win = jnp.where(lane < 64, 2.0, jnp.where(lane < 128, 4.0, jnp.where(lane < 192, 8.0, 16.0)))
    return jnp.minimum(pos1, win), lane


def _by_group(lane, s2, s4, s8, s16):
    return jnp.where(lane < 64, s2, jnp.where(lane < 128, s4, jnp.where(lane < 192, s8, s16)))


def _pool_means(pin, halo, i, tm):
    s1 = jnp.concatenate([halo, pin], axis=0)
    s2 = s1 + pltpu.roll(s1, 1, 0)
    s4 = s2 + pltpu.roll(s2, 2, 0)
    s8 = s4 + pltpu.roll(s4, 4, 0)
    s16 = s8 + pltpu.roll(s8, 8, 0)
    cnt, lane = _pool_counts(i, tm)
    sel = _by_group(lane, s2[POOL_HALO:], s4[POOL_HALO:], s8[POOL_HALO:], s16[POOL_HALO:])
    return sel / cnt - pin


def _mixers_fwd_tile(i, tm, a, u, vs, pin, halo, gsgu, wcat, bz, wp, pscale, goa, gos, gop):
    vn, rv = _rms(vs, gsgu, SGU_WIDTH)
    masks = _head_masks((CHUNK, SGU_WIDTH), SGU_HEAD_DIM)
    zc = jnp.concatenate([_sgu_fwd_chunk(vn[CHUNK * c:CHUNK * (c + 1)], wcat, bz, masks) for c in range(tm // CHUNK)], axis=0)
    gm = u * zc
    halo = jnp.where(i > 0, halo, 0.0)
    m = _pool_means(pin, halo, i, tm).astype(BF16)
    yp_pre = _dot(m, wp)
    yp = yp_pre * pscale
    na, ra = _rms(a, goa, MLA_WIDTH)
    ng, rg = _rms(gm, gos, SGU_WIDTH)
    npo, rp = _rms(yp, gop, POOL_WIDTH)
    mix = jnp.concatenate([na, ng, npo], axis=1).astype(BF16)
    return vn, rv, zc, gm, m, yp_pre, yp, ra, rg, rp, mix


def _z_specs(tm):
    col = lambda c: pl.BlockSpec((tm, 256), lambda i: (i, c))
    halo = pl.BlockSpec((POOL_HALO, 256), lambda i: (jnp.maximum(i * (tm // POOL_HALO) - 1, 0), 4))
    return [col(2), col(3), col(4), halo]


def mixers_fwd(x, a, z, gsgu, wcat, bz, wp, pscale, goa, gos, gop, wout):
    T = x.shape[0]
    tm = min(ROW_TILE, T)

    def body(x_ref, a_ref, u_ref, vs_ref, pin_ref, halo_ref, gsgu_ref, wcat_ref, bz_ref, wp_ref, ps_ref,
             goa_ref, gos_ref, gop_ref, wout_ref, x1_ref):
        i = pl.program_id(0)
        mix = _mixers_fwd_tile(i, tm, a_ref[...], u_ref[...], vs_ref[...], pin_ref[...], halo_ref[...], gsgu_ref[...],
                               wcat_ref[...], bz_ref[...], wp_ref[...], ps_ref[...], goa_ref[...], gos_ref[...],
                               gop_ref[...])[-1]
        x1_ref[...] = x_ref[...] + _dot(mix, wout_ref[...])

    row = lambda w: pl.BlockSpec((tm, w), lambda i: (i, 0))
    return pl.pallas_call(
        body, grid=(T // tm,), name="mixers_fwd",
        in_specs=[row(D_MODEL), row(MLA_WIDTH)] + _z_specs(tm) + [WHOLE] * 9,
        out_specs=row(D_MODEL),
        out_shape=jax.ShapeDtypeStruct((T, D_MODEL), F32),
        compiler_params=_cparams(),
    )(x, a, z, z, z, z, gsgu, wcat, bz, wp, pscale, goa, gos, gop, wout)


def ffn_fwd(x1, gffn, wg, wu, wd, target=None):
    T = x1.shape[0]
    tm = min(ROW_TILE, T)
    with_loss = target is not None

    def body(*refs):
        x1_ref, gffn_ref, wg_ref, wu_ref, wd_ref = refs[:5]
        outs = refs[6:] if with_loss else refs[5:]
        x1v = x1_ref[...]
        h2 = _rms(x1v, gffn_ref[...], D_MODEL)[0].astype(BF16)
        acc = x1v
        for c in range(FFN_HIDDEN // FFN_CHUNK):
            sl = slice(FFN_CHUNK * c, FFN_CHUNK * (c + 1))
            g = _dot_nt(h2, wg_ref[sl, :])
            u = _dot_nt(h2, wu_ref[sl, :])
            outs[-2][:, sl] = g.astype(BF16)
            outs[-1][:, sl] = u.astype(BF16)
            act = (g * _sigmoid(g) * u).astype(BF16)
            acc = acc + _dot(act, wd_ref[sl, :])
        if not with_loss:
            outs[0][...] = acc
            return
        dy_ref, loss_ref = outs[0], outs[1]

        @pl.when(pl.program_id(0) == 0)
        def _():
            loss_ref[...] = jnp.zeros(loss_ref.shape, F32)

        err = acc - refs[5][...]
        dy_ref[...] = err * (1.0 / D_MODEL)
        per_row = jnp.sum(err * err, axis=1, keepdims=True) * (1.0 / D_MODEL)
        loss_ref[...] += 0.5 * jnp.sum(per_row, axis=0, keepdims=True)

    row = lambda w: pl.BlockSpec((tm, w), lambda i: (i, 0))
    hidden = [jax.ShapeDtypeStruct((T, FFN_HIDDEN), BF16)] * 2
    if with_loss:
        return pl.pallas_call(
            body, grid=(T // tm,), name="ffn_fwd_loss",
            in_specs=[row(D_MODEL)] + [WHOLE] * 4 + [row(D_MODEL)],
            out_specs=[row(D_MODEL), pl.BlockSpec((1, 1), lambda i: (0, 0)), row(FFN_HIDDEN), row(FFN_HIDDEN)],
            out_shape=[jax.ShapeDtypeStruct((T, D_MODEL), F32), jax.ShapeDtypeStruct((1, 1), F32)] + hidden,
            compiler_params=_cparams(),
        )(x1, gffn, wg, wu, wd, target)
    return pl.pallas_call(
        body, grid=(T // tm,), name="ffn_fwd",
        in_specs=[row(D_MODEL)] + [WHOLE] * 4,
        out_specs=[row(D_MODEL), row(FFN_HIDDEN), row(FFN_HIDDEN)],
        out_shape=[jax.ShapeDtypeStruct((T, D_MODEL), F32)] + hidden,
        compiler_params=_cparams(),
    )(x1, gffn, wg, wu, wd)


def _acc_spec(shape):
    return pl.BlockSpec(shape, lambda i: (0,) * len(shape))


def ffn_bwd(dx2, x1, gs, us, gffn, wg, wu, wd):
    T = x1.shape[0]
    tm = min(ROW_TILE // 2, T)

    def body(dx2_ref, x1_ref, gs_ref, us_ref, gffn_ref, wg_ref, wu_ref, wd_ref,
             dx1_ref, h2_ref, act_ref, dg_ref, du_ref, dgffn_ref):
        @pl.when(pl.program_id(0) == 0)
        def _():
            dgffn_ref[...] = jnp.zeros(dgffn_ref.shape, F32)

        dx2v = dx2_ref[...]
        dy = dx2v.astype(BF16)
        x1v = x1_ref[...]
        h2, r = _rms(x1v, gffn_ref[...], D_MODEL)
        h2_ref[...] = h2.astype(BF16)
        for c in range(FFN_HIDDEN // FFN_CHUNK):
            sl = slice(FFN_CHUNK * c, FFN_CHUNK * (c + 1))
            g = gs_ref[:, sl].astype(F32)
            u = us_ref[:, sl].astype(F32)
            dact = _dot_nt(dy, wd_ref[sl, :])
            sg = _sigmoid(g)
            silu = g * sg
            act_ref[:, sl] = (silu * u).astype(BF16)
            dg_ref[:, sl] = (dact * u * (sg * (1.0 + g * (1.0 - sg)))).astype(BF16)
            du_ref[:, sl] = (dact * silu).astype(BF16)
        dh2 = _dot(dg_ref[...], wg_ref[...]) + _dot(du_ref[...], wu_ref[...])
        dxn, dgn = _rms_bwd(x1v, r, gffn_ref[...], dh2, D_MODEL)
        dx1_ref[...] = dx2v + dxn
        dgffn_ref[...] += dgn

    row = lambda w: pl.BlockSpec((tm, w), lambda i: (i, 0))
    return pl.pallas_call(
        body, grid=(T // tm,), name="ffn_bwd",
        in_specs=[row(D_MODEL), row(D_MODEL), row(FFN_HIDDEN), row(FFN_HIDDEN)] + [WHOLE] * 4,
        out_specs=[row(D_MODEL), row(D_MODEL), row(FFN_HIDDEN), row(FFN_HIDDEN), row(FFN_HIDDEN), _acc_spec((1, D_MODEL))],
        out_shape=[jax.ShapeDtypeStruct((T, D_MODEL), F32), jax.ShapeDtypeStruct((T, D_MODEL), BF16),
                   jax.ShapeDtypeStruct((T, FFN_HIDDEN), BF16), jax.ShapeDtypeStruct((T, FFN_HIDDEN), BF16),
                   jax.ShapeDtypeStruct((T, FFN_HIDDEN), BF16), jax.ShapeDtypeStruct((1, D_MODEL), F32)],
        compiler_params=_cparams(),
    )(dx2, x1, gs, us, gffn, wg, wu, wd)


TN_K_TILE = 2048
TN_ACC_BYTES = 6 * 1024 * 1024


def matmul_tn(a, b, name):
    T, M = a.shape
    N = b.shape[1]
    tk = min(TN_K_TILE, T)
    tm = M if M <= 1024 else M // 2
    tn = max(d for d in range(128, N + 1, 128) if N % d == 0 and tm * d * 4 <= TN_ACC_BYTES)
    nk = T // tk

    def body(a_ref, b_ref, o_ref, acc):
        k = pl.program_id(2)

        @pl.when(k == 0)
        def _():
            acc[...] = jnp.zeros(acc.shape, F32)

        acc[...] += _dot_tn(a_ref[...].astype(BF16), b_ref[...].astype(BF16))

        @pl.when(k == nk - 1)
        def _():
            o_ref[...] = acc[...].astype(BF16)

    return pl.pallas_call(
        body, grid=(M // tm, N // tn, nk), name=name,
        in_specs=[pl.BlockSpec((tk, tm), lambda i, j, k: (k, i)), pl.BlockSpec((tk, tn), lambda i, j, k: (k, j))],
        out_specs=pl.BlockSpec((tm, tn), lambda i, j, k: (i, j)),
        out_shape=jax.ShapeDtypeStruct((M, N), BF16),
        scratch_shapes=[pltpu.VMEM((tm, tn), F32)],
        compiler_params=_cparams(),
    )(a, b)


def mixers_bwd(dx1, a, z, gsgu, wcat, wcat_t, bz, wp, pscale, goa, gos, gop, wout):
    T = a.shape[0]
    tm = min(ROW_TILE, T)

    def body(dx1_ref, a_ref, u_ref, vs_ref, pin_ref, halo_ref, gsgu_ref, wcat_ref, wcatt_ref, bz_ref, wp_ref,
             ps_ref, goa_ref, gos_ref, gop_ref, wout_ref,
             da_ref, delta_ref, du_ref, dvs_ref, dm_ref, mix_ref,
             dgoa_ref, dgos_ref, dgop_ref, dps_ref, dwp_ref, dwsp_ref, db_ref, dgsgu_ref):
        i = pl.program_id(0)

        @pl.when(i == 0)
        def _():
            for r in (dgoa_ref, dgos_ref, dgop_ref, dps_ref, dwp_ref, dwsp_ref, db_ref, dgsgu_ref):
                r[...] = jnp.zeros(r.shape, F32)

        a_v, u, vs = a_ref[...], u_ref[...], vs_ref[...]
        goa, gos, gop, pscale_v = goa_ref[...], gos_ref[...], gop_ref[...], ps_ref[...]
        vn, rv, zc, gm, m, yp_pre, yp, ra, rg, rp, mix = _mixers_fwd_tile(
            i, tm, a_v, u, vs, pin_ref[...], halo_ref[...], gsgu_ref[...], wcat_ref[...], bz_ref[...], wp_ref[...],
            pscale_v, goa, gos, gop)
        mix_ref[...] = mix
        dmix = _dot_nt(dx1_ref[...].astype(BF16), wout_ref[...])
        da, dgoa = _rms_bwd(a_v, ra, goa, dmix[:, :MLA_WIDTH], MLA_WIDTH)
        dgm, dgos = _rms_bwd(gm, rg, gos, dmix[:, MLA_WIDTH:MLA_WIDTH + SGU_WIDTH], SGU_WIDTH)
        dyp, dgop = _rms_bwd(yp, rp, gop, dmix[:, MLA_WIDTH + SGU_WIDTH:], POOL_WIDTH)
        da_ref[...] = da
        dgoa_ref[...] += dgoa
        dgos_ref[...] += dgos
        dgop_ref[...] += dgop
        prod = da * a_v
        ones = jnp.ones((8, HEAD_PAD), F32)
        for h in range(HEADS):
            lo = HEAD_PAD * h
            sums = lax.dot_general(ones, prod[:, lo:lo + HEAD_PAD], (((1,), (1,)), ((), ())), preferred_element_type=F32,
                                   precision=lax.Precision.HIGHEST)
            delta_ref[h] = sums[0:1, :]
        dps_ref[...] += jnp.sum(dyp * yp_pre, axis=0, keepdims=True)
        dyp_pre = (dyp * pscale_v).astype(BF16)
        dwp_ref[...] += _dot_tn(m, dyp_pre)
        dm_ref[...] = _dot_nt(dyp_pre, wp_ref[...])
        du_ref[...] = dgm * zc
        dzc = dgm * u
        masks = _head_masks((CHUNK, SGU_WIDTH), SGU_HEAD_DIM)
        lane_b = lax.broadcasted_iota(jnp.int32, (CHUNK, HEAD_PAD), 1)
        dvn_parts = []
        dwsp = jnp.zeros(dwsp_ref.shape, F32)
        db = jnp.zeros(db_ref.shape, F32)
        for c in range(tm // CHUNK):
            dz_c = dzc[CHUNK * c:CHUNK * (c + 1)]
            dzstack = jnp.concatenate([jnp.where(mk, dz_c, 0.0).astype(BF16) for mk in masks], axis=0)
            dvn_parts.append(_dot(wcatt_ref[...], dzstack))
            dwsp = dwsp + _dot_nt(dzstack, vn[CHUNK * c:CHUNK * (c + 1)].astype(BF16))
            for h, mk in enumerate(masks):
                col = jnp.sum(jnp.where(mk, dz_c, 0.0), axis=1, keepdims=True)
                db = db + jnp.where(lane_b == h, col, 0.0)
        dwsp_ref[...] += dwsp
        db_ref[...] += db
        dvs, dgsgu = _rms_bwd(vs, rv, gsgu_ref[...], jnp.concatenate(dvn_parts, axis=0), SGU_WIDTH)
        dvs_ref[...] = dvs
        dgsgu_ref[...] += dgsgu

    row = lambda w: pl.BlockSpec((tm, w), lambda i: (i, 0))
    head = pl.BlockSpec((HEADS, 1, tm), lambda i: (0, 0, i))
    acc_shapes = [(1, MLA_WIDTH), (1, SGU_WIDTH), (1, POOL_WIDTH), (1, POOL_WIDTH), (POOL_WIDTH, POOL_WIDTH),
                  (HEADS * CHUNK, CHUNK), (CHUNK, HEAD_PAD), (1, SGU_WIDTH)]
    return pl.pallas_call(
        body, grid=(T // tm,), name="mixers_bwd",
        in_specs=[row(D_MODEL), row(MLA_WIDTH)] + _z_specs(tm) + [WHOLE] * 10,
        out_specs=[row(MLA_WIDTH), head, row(256), row(256), row(256), row(D_MODEL)] + [_acc_spec(s) for s in acc_shapes],
        out_shape=[jax.ShapeDtypeStruct((T, MLA_WIDTH), F32), jax.ShapeDtypeStruct((HEADS, 1, T), F32),
                   jax.ShapeDtypeStruct((T, 256), F32), jax.ShapeDtypeStruct((T, 256), F32),
                   jax.ShapeDtypeStruct((T, 256), F32), jax.ShapeDtypeStruct((T, D_MODEL), BF16)]
                  + [jax.ShapeDtypeStruct(s, F32) for s in acc_shapes],
        compiler_params=_cparams(),
    )(dx1, a, z, z, z, z, gsgu, wcat, wcat_t, bz, wp, pscale, goa, gos, gop, wout)


def pool_bwd(dm):
    T = dm.shape[0]
    tm = min(4 * ROW_TILE, T)
    nt = T // tm

    def body(dm_ref, next_ref, dpin_ref):
        i = pl.program_id(0)
        cnt, lane = _pool_counts(i, tm)
        dmv = dm_ref[...]
        win = _by_group(lane[:POOL_HALO], 2.0, 4.0, 8.0, 16.0)
        nxt = jnp.where(i < nt - 1, next_ref[...] / win, 0.0)
        r1 = jnp.concatenate([dmv / cnt, nxt], axis=0)
        n = tm + POOL_HALO
        r2 = r1 + pltpu.roll(r1, n - 1, 0)
        r4 = r2 + pltpu.roll(r2, n - 2, 0)
        r8 = r4 + pltpu.roll(r4, n - 4, 0)
        r16 = r8 + pltpu.roll(r8, n - 8, 0)
        dpin_ref[...] = _by_group(lane, r2[:tm], r4[:tm], r8[:tm], r16[:tm]) - dmv

    return pl.pallas_call(
        body, grid=(nt,), name="pool_bwd",
        in_specs=[pl.BlockSpec((tm, 256), lambda i: (i, 0)),
                  pl.BlockSpec((POOL_HALO, 256), lambda i: (jnp.minimum((i + 1) * (tm // POOL_HALO), T // POOL_HALO - 1), 0))],
        out_specs=pl.BlockSpec((tm, 256), lambda i: (i, 0)),
        out_shape=jax.ShapeDtypeStruct((T, 256), F32),
        compiler_params=_cparams(),
    )(dm, dm)


def attn_bwd(q, k, kt, v, do, lse, delta):
    _, T, _ = q.shape
    tb = min(ATTN_TILE, T)
    sb = min(ATTN_SUB, tb)
    ns = tb // sb
    nb = T // tb
    hb = ATTN_BWD_HEADS

    pairs = [(i, j) for i in range(nb) for j in range(i, nb)]

    def body(i_tab, j_tab, q_ref, k_ref, kt_ref, v_ref, do_ref, lse_ref, delta_ref, dqt_ref, dk_ref, dv_ref, dk_s, dv_s):
        i, j = i_tab[pl.program_id(1)], j_tab[pl.program_id(1)]

        @pl.when(pl.program_id(1) == 0)
        def _():
            dqt_ref[...] = jnp.zeros(dqt_ref.shape, F32)

        @pl.when(j == i)
        def _():
            dk_s[...] = jnp.zeros(dk_s.shape, F32)
            dv_s[...] = jnp.zeros(dv_s.shape, F32)

        def sub_block(h, a, b, masked):
            qa = slice(sb * a, sb * (a + 1))
            kb = slice(sb * b, sb * (b + 1))
            qv = q_ref[h, qa, :]
            dov = do_ref[qa, HEAD_PAD * h:HEAD_PAD * (h + 1)].astype(BF16)
            st = _dot_nt(k_ref[h, kb, :], qv)
            dpt = _dot_nt(v_ref[h, kb, :], dov)
            pt = jnp.exp2(st - lse_ref[h, :, qa])
            if masked:
                krow = lax.broadcasted_iota(jnp.int32, st.shape, 0)
                qcol = lax.broadcasted_iota(jnp.int32, st.shape, 1)
                pt = jnp.where(krow <= qcol, pt, 0.0)
            dst = (pt * (dpt - delta_ref[h, :, qa])).astype(BF16)
            dv_s[h, kb, :] += _dot(pt.astype(BF16), dov)
            dk_s[h, kb, :] += _dot(dst, qv)
            cols = pl.ds(pl.multiple_of(j * tb + sb * a, sb), sb)
            dqt_ref[h, :, cols] += _dot(kt_ref[h, :, kb], dst)

        @pl.when(j > i)
        def _():
            for a in range(ns):
                for h in range(hb):
                    for b in range(ns):
                        sub_block(h, a, b, False)

        @pl.when(j == i)
        def _():
            for a in range(ns):
                for h in range(hb):
                    for b in range(a + 1):
                        sub_block(h, a, b, a == b)

        @pl.when(j == nb - 1)
        def _():
            dk_ref[...] = dk_s[...] * (1.0 / LOG2E)
            dv_ref[...] = dv_s[...]

    qspec = pl.BlockSpec((hb, tb, HEAD_PAD), lambda g, p, it, jt: (g, jt[p], 0))
    kspec = pl.BlockSpec((hb, tb, HEAD_PAD), lambda g, p, it, jt: (g, it[p], 0))
    ktspec = pl.BlockSpec((hb, HEAD_PAD, tb), lambda g, p, it, jt: (g, 0, it[p]))
    rowspec = pl.BlockSpec((hb, 1, tb), lambda g, p, it, jt: (g, 0, jt[p]))
    grid_spec = pltpu.PrefetchScalarGridSpec(
        num_scalar_prefetch=2, grid=(HEADS // hb, len(pairs)),
        in_specs=[qspec, kspec, ktspec, kspec, pl.BlockSpec((tb, hb * HEAD_PAD), lambda g, p, it, jt: (jt[p], g)), rowspec, rowspec],
        out_specs=[pl.BlockSpec((hb, HEAD_PAD, T), lambda g, p, it, jt: (g, 0, 0)), kspec, kspec],
        scratch_shapes=[pltpu.VMEM((hb, tb, HEAD_PAD), F32), pltpu.VMEM((hb, tb, HEAD_PAD), F32)])
    return pl.pallas_call(
        body, grid_spec=grid_spec, name="attn_bwd",
        out_shape=[jax.ShapeDtypeStruct((HEADS, HEAD_PAD, T), F32)] + [jax.ShapeDtypeStruct((HEADS, T, HEAD_PAD), F32)] * 2,
        compiler_params=_cparams(),
    )(*_pair_tables(pairs), q, k, kt, v, do, lse, delta)


def mixin_bwd(dres, x, z, rope, dqt, dk, dv, du, dvs, dpin, gmix, win, gql, wq, gkvl, wkv, gqh, gkh):
    T = x.shape[0]
    tm = min(ROW_TILE, T)
    part = min(MIXIN_BWD_PART, tm)

    def body(dres_ref, x_ref, z_ref, c_ref, sa_ref, sb_ref, dqt_ref, dk_ref, dv_ref, du_ref, dvs_ref, dpin_ref,
             gmix_ref, win_ref, gql_ref, wq_ref, gkvl_ref, wkv_ref, gqh_ref, gkh_ref,
             dx_ref, hn_ref, dz_ref, qn_ref, dqup_ref, kvn_ref, dkvup_ref,
             dgmix_ref, dgql_ref, dgkvl_ref, dgqh_ref, dgkh_ref):
        @pl.when(pl.program_id(0) == 0)
        def _():
            for r in (dgmix_ref, dgql_ref, dgkvl_ref, dgqh_ref, dgkh_ref):
                r[...] = jnp.zeros(r.shape, F32)

        lane = lax.broadcasted_iota(jnp.int32, (part, HEAD_PAD), 1)
        rope_lanes = (lane >= QK_NOPE) & (lane < QK_HEAD)

        def recompute(g):
            rows = g["rows"]
            g["xv"] = x_ref[rows, :]
            hn, g["rx"] = _rms(g["xv"], gmix_ref[...], D_MODEL)
            hn_ref[rows, :] = hn.astype(BF16)
            g["ql"], g["kvl"], g["kr"], qn, g["rq"], kvn, g["rkv"], g["q_up"], g["kv_up"] = _qkv_pre(
                z_ref[rows, :], gql_ref[...], wq_ref[...], gkvl_ref[...], wkv_ref[...])
            qn_ref[rows, :] = qn
            kvn_ref[rows, :] = kvn

        def heads(g):
            rows = g["rows"]
            c, sa, sb = c_ref[rows, :], sa_ref[rows, :], sb_ref[rows, :]
            dkr = jnp.zeros((part, HEAD_PAD), F32)
            dgqh = jnp.zeros((1, HEAD_PAD), F32)
            dgkh = jnp.zeros((1, HEAD_PAD), F32)
            dq_parts, dk_parts, dv_parts = [], [], []
            for h in range(HEADS):
                lo = HEAD_PAD * h
                qh = g["q_up"][:, lo:lo + HEAD_PAD]
                rqh = lax.rsqrt(jnp.sum(qh * qh, axis=-1, keepdims=True) * (1.0 / QK_HEAD) + EPS)
                dq_h = jnp.transpose(dqt_ref[h, :, rows]) * ATTN_SCALE
                dqh, dg = _rms_bwd(qh, rqh, gqh_ref[...], _rope_bwd(dq_h, c, sa, sb), QK_HEAD)
                dgqh = dgqh + dg
                dq_parts.append(dqh)
                kh = g["kv_up"][:, lo:lo + HEAD_PAD] + g["kr"]
                rkh = lax.rsqrt(jnp.sum(kh * kh, axis=-1, keepdims=True) * (1.0 / QK_HEAD) + EPS)
                dkh, dg = _rms_bwd(kh, rkh, gkh_ref[...], _rope_bwd(dk_ref[h, rows, :], c, sa, sb), QK_HEAD)
                dgkh = dgkh + dg
                dkr = dkr + jnp.where(rope_lanes, dkh, 0.0)
                dk_parts.append(dkh)
                dv_parts.append(dv_ref[h, rows, :])
            dgqh_ref[...] += dgqh
            dgkh_ref[...] += dgkh
            g["dkr"] = dkr
            g["dq_up"] = jnp.concatenate(dq_parts, axis=1).astype(BF16)
            g["dkv_up"] = jnp.concatenate(dk_parts + dv_parts, axis=1).astype(BF16)
            dqup_ref[rows, :] = g["dq_up"]
            dkvup_ref[rows, :] = g["dkv_up"]

        def latents(g):
            rows = g["rows"]
            dql, dg = _rms_bwd(g["ql"], g["rq"], gql_ref[...], _dot(g["dq_up"], wq_ref[...]), Q_LORA)
            dgql_ref[...] += dg
            dkvl, dg = _rms_bwd(g["kvl"], g["rkv"], gkvl_ref[...], _dot(g["dkv_up"], wkv_ref[...]), KV_LORA)
            dgkvl_ref[...] += dg
            g["dz"] = jnp.concatenate([dql, dkvl, g["dkr"], du_ref[rows, :], dvs_ref[rows, :], dpin_ref[rows, :]],
                                      axis=1).astype(BF16)
            dz_ref[rows, :] = g["dz"]

        def inputs(g):
            rows = g["rows"]
            dxn, dg = _rms_bwd(g["xv"], g["rx"], gmix_ref[...], _dot(g["dz"], win_ref[...]), D_MODEL)
            dgmix_ref[...] += dg
            dx_ref[rows, :] = dres_ref[rows, :] + dxn

        _skewed((recompute, heads, latents, inputs), [dict(rows=pl.ds(part * p, part)) for p in range(tm // part)])

    row = lambda w: pl.BlockSpec((tm, w), lambda i: (i, 0))
    head = pl.BlockSpec((HEADS, tm, HEAD_PAD), lambda i: (0, i, 0))
    head_t = pl.BlockSpec((HEADS, HEAD_PAD, tm), lambda i: (0, 0, i))
    acc_shapes = [(1, D_MODEL), (1, Q_LORA), (1, KV_LORA), (1, HEAD_PAD), (1, HEAD_PAD)]
    out_rows = [(D_MODEL, F32), (D_MODEL, BF16), (Z_WIDTH, BF16), (Q_LORA, BF16), (HEADS * HEAD_PAD, BF16),
                (KV_LORA, BF16), (2 * HEADS * HEAD_PAD, BF16)]
    return pl.pallas_call(
        body, grid=(T // tm,), name="mixin_bwd",
        in_specs=[row(D_MODEL), row(D_MODEL), row(Z_WIDTH)] + [row(HEAD_PAD)] * 3 + [head_t, head, head, row(256), row(256), row(256)]
                 + [WHOLE] * 8,
        out_specs=[row(w) for w, _ in out_rows] + [_acc_spec(s) for s in acc_shapes],
        out_shape=[jax.ShapeDtypeStruct((T, w), dt) for w, dt in out_rows] + [jax.ShapeDtypeStruct(s, F32) for s in acc_shapes],
        compiler_params=_cparams(),
    )(dres, x, z, *rope, dqt, dk, dv, du, dvs, dpin, gmix, win, gql, wq, gkvl, wkv, gqh, gkh)


def _place():
    x, y, c = lax.axis_index("x"), lax.axis_index("y"), lax.axis_index("c")
    return x, y, c, 4 * x + 2 * y + c


def _layer_of(ref, l):
    return ref[:, l, :] if ref.shape[1] == DEPTH else ref[l]


def _layer_shape(shard):
    return (shard.shape[0], shard.shape[2]) if shard.shape[1] == DEPTH else shard.shape[1:]


def cast_shards(shards, wanted):
    n = len(shards)

    def body(*refs):
        for o_ref, (w, l) in zip(refs[n:], wanted):
            o_ref[...] = _layer_of(refs[w], l).astype(BF16)

    return pl.pallas_call(
        body, name="cast_shards", in_specs=[WHOLE] * n, out_specs=[WHOLE] * len(wanted),
        out_shape=[jax.ShapeDtypeStruct(_layer_shape(shards[w]), BF16) for w, _ in wanted],
        compiler_params=_cparams(),
    )(*shards)


def _peer(k):
    x, y, c, _ = _place()
    px = 1 - x if k & 4 else x
    py = 1 - y if k & 2 else y
    pc = 1 - c if k & 1 else c
    return (px, py, pc), 4 * px + 2 * py + pc


def exchange_start(srcs, after, gather, name):
    n = len(srcs)
    land_shapes = [((N_DEV,) + s.shape) if gather else s.shape for s in srcs]

    def body(*refs):
        src_refs, land_refs = refs[:n], refs[n:2 * n]
        send_sems, recv_sems = refs[2 * n + 1:3 * n + 1], refs[3 * n + 1:4 * n + 1]
        token = refs[-1]
        _, _, _, me = _place()
        for k in range(1, N_DEV):
            peer, peer_id = _peer(k)
            for w in range(n):
                pltpu.make_async_remote_copy(
                    src_ref=src_refs[w] if gather else src_refs[w].at[peer_id], dst_ref=land_refs[w].at[me],
                    send_sem=send_sems[w], recv_sem=recv_sems[w], device_id=peer, device_id_type=MESH).start()
        for w in range(n):
            pltpu.make_async_copy(src_refs[w] if gather else src_refs[w].at[me], land_refs[w].at[me], recv_sems[w]).start()
        token[...] = jnp.zeros(token.shape, F32)

    hbm = lambda a: pltpu.with_memory_space_constraint(a, pltpu.HBM)
    outs = pl.pallas_call(
        body, name=name,
        out_shape=(pltpu.SemaphoreType.DMA(()),) * (2 * n)
                  + tuple(pltpu.HBM(s.shape, s.dtype) for s in srcs)
                  + tuple(pltpu.HBM(l, s.dtype) for l, s in zip(land_shapes, srcs))
                  + (jax.ShapeDtypeStruct((8, 128), F32),),
        in_specs=[HBM_SPEC] * (2 * n) + [ANY],
        out_specs=(SEM_SPEC,) * (2 * n) + (HBM_SPEC,) * (2 * n) + (WHOLE,),
        input_output_aliases={i: 2 * n + i for i in range(2 * n)},
        compiler_params=pltpu.CompilerParams(has_side_effects=pltpu.SideEffectType.DATAFLOW_SIDE_EFFECTING),
    )(*[hbm(s) for s in srcs], *[hbm(lax.empty(l, s.dtype)) for l, s in zip(land_shapes, srcs)], after)
    return list(outs[:n]), list(outs[n:2 * n]), list(outs[2 * n:3 * n]), list(outs[3 * n:4 * n]), outs[-1]


def exchange_wait(started, after, name):
    send_sems, recv_sems, srcs, lands, _ = started
    n = len(srcs)

    def body(*refs):
        land_refs = refs[n:2 * n]
        send_sems, recv_sems = refs[2 * n:3 * n], refs[3 * n:4 * n]
        x, y, c, _ = _place()
        for w in range(n):
            seven, eight = land_refs[w].at[pl.ds(0, N_DEV - 1)], land_refs[w]
            pltpu.make_async_remote_copy(src_ref=seven, dst_ref=seven, send_sem=send_sems[w], recv_sem=recv_sems[w],
                                         device_id=(x, y, c), device_id_type=MESH).wait_send()
            pltpu.make_async_remote_copy(src_ref=eight, dst_ref=eight, send_sem=send_sems[w], recv_sem=recv_sems[w],
                                         device_id=(x, y, c), device_id_type=MESH).wait_recv()

    outs = pl.pallas_call(
        body, name=name,
        out_shape=tuple(pltpu.HBM(a.shape, a.dtype) for a in (*srcs, *lands)),
        in_specs=[HBM_SPEC] * (2 * n) + [SEM_SPEC] * (2 * n) + [ANY],
        out_specs=(HBM_SPEC,) * (2 * n),
        input_output_aliases={i: i for i in range(2 * n)},
        compiler_params=pltpu.CompilerParams(has_side_effects=pltpu.SideEffectType.DATAFLOW_SIDE_EFFECTING),
    )(*srcs, *lands, *send_sems, *recv_sems, after)
    return list(outs[n:])


def _adamw(w, g, m, v):
    m2 = ADAM_B1 * m + (1.0 - ADAM_B1) * g
    v2 = ADAM_B2 * v + (1.0 - ADAM_B2) * (g * g)
    m_hat = m2 / (1.0 - ADAM_B1 ** ADAM_STEP)
    v_hat = v2 / (1.0 - ADAM_B2 ** ADAM_STEP)
    delta = -ADAM_LR * (m_hat / (jnp.sqrt(v_hat) + ADAM_EPS) + ADAM_WD * w)
    return delta, m2, v2


def adamw_sharded(parts, w, m, v, name):
    L, R, C = w.shape
    fits = [d for d in range(16, min(R, 512) + 1, 16) if R % d == 0]
    br = max(fits) if fits else R
    nblk = R // br

    def body(*refs):
        p_refs = refs[:L]
        w_ref, m_ref, v_ref, g_ref, d_ref, m2_ref, v2_ref = refs[L:]

        def total(p_ref):
            g = p_ref[0].astype(F32)
            for s in range(1, N_DEV):
                g = g + p_ref[s].astype(F32)
            return g

        g = total(p_refs[0])
        for l in range(1, L):
            g = jnp.where(pl.program_id(0) == l, total(p_refs[l]), g)
        g_ref[...] = g
        d_ref[...], m2_ref[...], v2_ref[...] = _adamw(w_ref[...], g, m_ref[...], v_ref[...])

    blk = pl.BlockSpec((None, br, C), lambda l, i: (l, i, 0))

    def part_spec(k):
        return pl.BlockSpec((N_DEV, br, C), lambda l, i: (0, jnp.where(l == k, i, jnp.where(l < k, 0, nblk - 1)), 0))

    return pl.pallas_call(
        body, grid=(L, nblk), name=name,
        in_specs=[part_spec(k) for k in range(L)] + [blk, blk, blk],
        out_specs=[blk] * 4,
        out_shape=[jax.ShapeDtypeStruct((L, R, C), F32)] * 4,
        compiler_params=_cparams(),
    )(*parts, w, m, v)


def adamw_small(gparts, w, m, v):
    R = gparts.shape[1]

    def body(g_ref, w_ref, m_ref, v_ref, grad_ref, d_ref, m2_ref, v2_ref):
        g = g_ref[0]
        for s in range(1, N_DEV):
            g = g + g_ref[s]
        grad_ref[...] = g
        d_ref[...], m2_ref[...], v2_ref[...] = _adamw(w_ref[...], g, m_ref[...], v_ref[...])

    return pl.pallas_call(
        body, name="adamw_small",
        in_specs=[WHOLE] * 4, out_specs=[WHOLE] * 4,
        out_shape=[jax.ShapeDtypeStruct((R, 128), F32)] * 4,
        compiler_params=_cparams(),
    )(gparts, w, m, v)


def _shard_view(name, a):
    if name == "w_in":
        return a.transpose(2, 0, 1)
    return a.swapaxes(1, 2) if name in TRANSPOSED else a


def _shard_unview(name, a):
    if name == "w_in":
        return a.transpose(1, 2, 0)
    return a.swapaxes(1, 2) if name in TRANSPOSED else a


def _pad_head_rows(w, width):
    c = w.shape[1]
    return jnp.pad(w.reshape(HEADS, width, c), ((0, 0), (0, HEAD_PAD - width), (0, 0))).reshape(HEADS * HEAD_PAD, c)


def _unpad_head_rows(w, width):
    c = w.shape[1]
    return w.reshape(HEADS, HEAD_PAD, c)[:, :width].reshape(HEADS * width, c)


O1 = Q_LORA
O2 = O1 + KV_LORA
O3 = O2 + QK_ROPE


def _mixer_weights(gw):
    w_in = gw["w_in"].reshape(IN_WIDTH, D_MODEL)
    zero = lambda n: jnp.zeros((n, D_MODEL), BF16)
    win = jnp.concatenate([w_in[:O2], zero(QK_NOPE), w_in[O2:O3], zero(HEAD_PAD - QK_HEAD), w_in[O3:]], axis=0)
    wq = _pad_head_rows(gw["w_q_up"].reshape(HEADS * QK_HEAD, Q_LORA), QK_HEAD)
    w_kv = gw["w_kv_up"].reshape(HEADS, QK_NOPE + V_HEAD, KV_LORA)
    wk = jnp.pad(w_kv[:, :QK_NOPE], ((0, 0), (0, HEAD_PAD - QK_NOPE), (0, 0))).reshape(HEADS * HEAD_PAD, KV_LORA)
    wv = w_kv[:, QK_NOPE:].reshape(HEADS * V_HEAD, KV_LORA)
    wkv = jnp.concatenate([wk, wv], axis=0)
    out = dict(win=win, wq=wq, wkv=wkv)
    if "w_out" in gw:
        out["wout"] = gw["w_out"].reshape(D_MODEL, D_MODEL)
    return out


def _ffn_weights(gw):
    return dict(wg=gw["w_gate"].reshape(FFN_HIDDEN, D_MODEL), wu=gw["w_up"].reshape(FFN_HIDDEN, D_MODEL),
                wd=gw["w_down"].reshape(FFN_HIDDEN, D_MODEL))


def _layer_small(p, l):
    row = lambda a: a.reshape(1, -1)
    pad_head = lambda g: jnp.pad(g, (0, HEAD_PAD - QK_HEAD)).reshape(1, HEAD_PAD)
    tril = jnp.tril(jnp.ones((CHUNK, CHUNK), F32))
    wsp = p["w_spatial"][l] * tril
    wcat = jnp.concatenate([wsp[h] for h in range(HEADS)], axis=1).astype(BF16)
    wcat_t = jnp.concatenate([wsp[h].T for h in range(HEADS)], axis=1).astype(BF16)
    bz = jnp.repeat(p["b_spatial"][l].T, SGU_HEAD_DIM, axis=1)
    wp = jax.scipy.linalg.block_diag(*[p["w_pool"][l][g] for g in range(HEADS)]).astype(BF16)
    return dict(gmix=row(p["g_mix_norm"][l]), gql=row(p["g_q_lat"][l]), gkvl=row(p["g_kv_lat"][l]),
                gqh=pad_head(p["g_q_head"][l]), gkh=pad_head(p["g_k_head"][l]), gsgu=row(p["g_sgu_v"][l]),
                wcat=wcat, wcat_t=wcat_t, bz=bz, wp=wp, pscale=row(p["pool_scale"][l]),
                goa=row(p["g_out_mla"][l]), gos=row(p["g_out_sgu"][l]), gop=row(p["g_out_pool"][l]),
                gffn=row(p["g_ffn_norm"][l]))


MIXER_SIDE = ("w_in", "w_q_up", "w_kv_up", "w_out")
FFN_SIDE = ("w_gate", "w_up", "w_down")
TRANSPOSED = ("w_in", "w_q_up", "w_kv_up", "w_gate", "w_up")
FIRST_0 = ("w_in", "w_q_up", "w_kv_up")
LATER_0 = ("w_out",) + FFN_SIDE
SHARDED = MIXER_SIDE + FFN_SIDE
SMALL = ("g_mix_norm", "g_q_lat", "g_kv_lat", "g_q_head", "g_k_head", "g_sgu_v", "w_spatial", "b_spatial", "w_pool",
         "pool_scale", "g_out_mla", "g_out_sgu", "g_out_pool", "g_ffn_norm")
WEIGHTS = ("g_mix_norm", "w_in", "g_q_lat", "w_q_up", "g_kv_lat", "w_kv_up", "g_q_head", "g_k_head", "g_sgu_v", "w_spatial",
           "b_spatial", "w_pool", "pool_scale", "g_out_mla", "g_out_sgu", "g_out_pool", "w_out", "g_ffn_norm", "w_gate",
           "w_up", "w_down")
PACKED = SMALL + ("loss",)
PACK_ROWS = 8 * 128


def _pack_small(parts):
    flat = []
    for name in PACKED:
        a = parts[name].reshape(-1)
        flat.append(jnp.pad(a, (0, -a.shape[0] % PACK_ROWS)))
    return jnp.concatenate(flat).reshape(-1, 128)


def _unpack_small(packed, like):
    out, row = {}, 0
    for name in PACKED:
        n = math.prod(like[name].shape)
        rows = -(-n // PACK_ROWS) * 8
        out[name] = packed[row:row + rows].reshape(-1)[:n].reshape(like[name].shape)
        row += rows
    return out


def _forward_attention(x, rope, W, S):
    z, q, k, v, kt, vt = mixin_fwd(x, rope, S["gmix"], W["win"], S["gql"], W["wq"], S["gkvl"], W["wkv"], S["gqh"], S["gkh"])
    a, lse = attn_fwd(q, k, vt)
    return dict(x=x, z=z, q=q, k=k, kt=kt, v=v, a=a, lse=lse)


def _forward_mixers(A, wout, S):
    A["x1"] = mixers_fwd(A["x"], A["a"], A["z"], S["gsgu"], S["wcat"], S["bz"], S["wp"], S["pscale"], S["goa"], S["gos"],
                         S["gop"], wout)
    return A["x1"]


def _backward_ffn(dx2, W, S, A, l):
    dx1, h2, act, dg, du_ffn, dgffn = ffn_bwd(dx2, A["x1"], A["gs"], A["us"], S["gffn"], W["wg"], W["wu"], W["wd"])
    d_wd = matmul_tn(act, dx2, f"dw_down_{l}")
    d_wg = matmul_tn(dg, h2, f"dw_gate_{l}")
    d_wu = matmul_tn(du_ffn, h2, f"dw_up_{l}")
    big = {n: d.reshape(N_DEV, -1, D_MODEL) for n, d in (("w_gate", d_wg), ("w_up", d_wu), ("w_down", d_wd))}
    return dx1, big, dgffn


def _backward_mixers(dx1, dgffn, rope, W, S, A, l, send_wout=None):
    (da, delta, du, dvs, dm, mix, dgoa, dgos, dgop, dps, dwp, dwsp, db, dgsgu) = mixers_bwd(
        dx1, A["a"], A["z"], S["gsgu"], S["wcat"], S["wcat_t"], S["bz"], S["wp"], S["pscale"], S["goa"], S["gos"],
        S["gop"], W["wout"])
    d_wout = matmul_tn(mix, dx1, f"dw_out_{l}")
    if send_wout is not None:
        delta = delta + send_wout(d_wout.reshape(N_DEV, -1, D_MODEL))[0, 0]
    dpin = pool_bwd(dm)
    dqt, dk, dv = attn_bwd(A["q"], A["k"], A["kt"], A["v"], da, A["lse"], delta)
    (dx, hn, dz, qn, dq_up, kvn, dkv_up, dgmix, dgql, dgkvl, dgqh, dgkh) = mixin_bwd(
        dx1, A["x"], A["z"], rope, dqt, dk, dv, du, dvs, dpin, S["gmix"], W["win"], S["gql"], W["wq"], S["gkvl"],
        W["wkv"], S["gqh"], S["gkh"])
    d_win = matmul_tn(dz, hn, f"dw_in_{l}")
    d_wq = matmul_tn(dq_up, qn, f"dw_q_up_{l}")
    d_wkv = matmul_tn(dkv_up, kvn, f"dw_kv_up_{l}")
    d_win = jnp.concatenate([d_win[:O2], d_win[O2 + QK_NOPE:O2 + QK_HEAD], d_win[O2 + HEAD_PAD:]], axis=0)
    d_wk = d_wkv[:HEADS * HEAD_PAD].reshape(HEADS, HEAD_PAD, KV_LORA)[:, :QK_NOPE]
    d_wv = d_wkv[HEADS * HEAD_PAD:].reshape(HEADS, V_HEAD, KV_LORA)
    d_wkv = jnp.concatenate([d_wk, d_wv], axis=1)
    big = dict(w_in=d_win.reshape(N_DEV, -1, D_MODEL), w_q_up=_unpad_head_rows(d_wq, QK_HEAD).reshape(N_DEV, -1, Q_LORA),
               w_kv_up=d_wkv.reshape(N_DEV, -1, KV_LORA), w_out=d_wout.reshape(N_DEV, -1, D_MODEL))
    tril = jnp.tril(jnp.ones((CHUNK, CHUNK), F32))
    small = dict(g_mix_norm=dgmix[0], g_q_lat=dgql[0], g_kv_lat=dgkvl[0], g_q_head=dgqh[0, :QK_HEAD], g_k_head=dgkh[0, :QK_HEAD],
                 g_sgu_v=dgsgu[0], w_spatial=dwsp.reshape(HEADS, CHUNK, CHUNK) * tril, b_spatial=db[:, :HEADS].T,
                 w_pool=jnp.stack([dwp[64 * g:64 * (g + 1), 64 * g:64 * (g + 1)] for g in range(HEADS)]),
                 pool_scale=dps[0], g_out_mla=dgoa[0], g_out_sgu=dgos[0], g_out_pool=dgop[0], g_ffn_norm=dgffn[0])
    return dx, big, small


def kernel(x, positions, g_mix_norm, w_in, g_q_lat, w_q_up, g_kv_lat, w_kv_up, g_q_head, g_k_head, g_sgu_v, w_spatial, b_spatial, w_pool, pool_scale, g_out_mla, g_out_sgu, g_out_pool, w_out, g_ffn_norm, w_gate, w_up, w_down, loss_target, m_g_mix_norm, m_w_in, m_g_q_lat, m_w_q_up, m_g_kv_lat, m_w_kv_up, m_g_q_head, m_g_k_head, m_g_sgu_v, m_w_spatial, m_b_spatial, m_w_pool, m_pool_scale, m_g_out_mla, m_g_out_sgu, m_g_out_pool, m_w_out, m_g_ffn_norm, m_w_gate, m_w_up, m_w_down, v_g_mix_norm, v_w_in, v_g_q_lat, v_w_q_up, v_g_kv_lat, v_w_kv_up, v_g_q_head, v_g_k_head, v_g_sgu_v, v_w_spatial, v_b_spatial, v_w_pool, v_pool_scale, v_g_out_mla, v_g_out_sgu, v_g_out_pool, v_w_out, v_g_ffn_norm, v_w_gate, v_w_up, v_w_down):
    given = dict(locals())
    w = {n: given[n] for n in WEIGHTS}
    m = {n: given["m_" + n] for n in WEIGHTS}
    v = {n: given["v_" + n] for n in WEIGHTS}
    T = x.shape[1]
    xs = x.reshape(T, D_MODEL)

    half = QK_ROPE // 2
    inv_freq = 1.0 / (ROPE_THETA ** (jnp.arange(half, dtype=F32) / half))
    ang16 = positions.reshape(T).astype(F32)[:, None] * inv_freq
    ang = jnp.concatenate([jnp.zeros((T, QK_NOPE), F32), ang16, ang16, jnp.zeros((T, HEAD_PAD - QK_HEAD), F32)], axis=1)

    wv = {n: _shard_view(n, w[n]) for n in SHARDED}
    wanted = [(SHARDED.index(n), 0) for n in FIRST_0 + LATER_0] + [(i, 1) for i in range(len(SHARDED))]
    bf = cast_shards([wv[n] for n in SHARDED], wanted)
    nf, nl = len(FIRST_0), len(LATER_0)
    ag_first0 = exchange_start(bf[:nf], ang, True, "ag_start_first0")
    rope = rope_tables(ang + ag_first0[-1][0, 0])
    W0 = dict(zip(FIRST_0, exchange_wait(ag_first0, rope[0], "ag_wait_first0")))
    ag_rest = exchange_start(bf[nf:], W0["w_in"], True, "ag_start_rest")
    ag_later0, ag_l1 = [tuple(part[a:b] for part in ag_rest[:4]) + (ag_rest[-1],) for a, b in ((0, nl), (nl, None))]
    Ss = [_layer_small(w, l) for l in range(DEPTH)]

    S0 = dict(Ss[0], gmix=Ss[0]["gmix"] + ag_rest[-1][0, 0])
    A0 = _forward_attention(xs, rope, _mixer_weights(W0), S0)
    W0.update(zip(LATER_0, exchange_wait(ag_later0, A0["a"], "ag_wait_later0")))
    W0 = {**_mixer_weights(W0), **_ffn_weights(W0)}
    x1 = _forward_mixers(A0, W0["wout"], Ss[0])
    h, A0["gs"], A0["us"] = ffn_fwd(x1, Ss[0]["gffn"], W0["wg"], W0["wu"], W0["wd"])
    layer1 = dict(zip(SHARDED, exchange_wait(ag_l1, h, "ag_wait_l1")))
    W1 = {**_mixer_weights(layer1), **_ffn_weights(layer1)}
    A1 = _forward_attention(h, rope, W1, Ss[1])
    x1 = _forward_mixers(A1, W1["wout"], Ss[1])
    dh, loss_part, A1["gs"], A1["us"] = ffn_fwd(x1, Ss[1]["gffn"], W1["wg"], W1["wu"], W1["wd"],
                                                target=loss_target.reshape(T, D_MODEL))

    dx1, big_f1, dgffn1 = _backward_ffn(dh, W1, Ss[1], A1, 1)
    dh, big_m1, small1 = _backward_mixers(dx1, dgffn1, rope, W1, Ss[1], A1, 1)
    rs_l1 = exchange_start([{**big_m1, **big_f1}[n] for n in SHARDED], dh, False, "rs_start_l1")
    S0 = dict(Ss[0], gffn=Ss[0]["gffn"] + rs_l1[-1][0, 0])
    dx1, big_f0, dgffn0 = _backward_ffn(dh, W0, S0, A0, 0)
    rs_ffn0 = exchange_start([big_f0[n] for n in FFN_SIDE], dx1, False, "rs_start_ffn0")
    S0 = dict(Ss[0], gsgu=Ss[0]["gsgu"] + rs_ffn0[-1][0, 0])
    rs_out0 = []

    def send_wout(pieces):
        rs_out0.append(exchange_start([pieces], dx1, False, "rs_start_out0"))
        return rs_out0[0][-1]

    dh, big_m0, small0 = _backward_mixers(dx1, dgffn0, rope, W0, S0, A0, 0, send_wout)
    grad_x = dh.reshape(x.shape)
    smalls = [small0, small1]

    rs_first0 = exchange_start([big_m0[n] for n in FIRST_0], dh, False, "rs_start_first0")
    small_part = {n: jnp.stack([smalls[l][n] for l in range(DEPTH)]) for n in SMALL}
    small_part["loss"] = loss_part
    ag_small = exchange_start([_pack_small(small_part)], rs_first0[-1], True, "ag_start_small")
    parts1 = dict(zip(SHARDED, exchange_wait(rs_l1, ag_small[-1], "rs_wait_l1")))
    parts0 = dict(zip(FFN_SIDE, exchange_wait(rs_ffn0, ag_small[-1], "rs_wait_ffn0")))
    parts0["w_out"] = exchange_wait(rs_out0[0], ag_small[-1], "rs_wait_out0")[0]
    grad, delta, new_m, new_v = {}, {}, {}, {}

    for n in LATER_0 + FIRST_0:
        if n == FIRST_0[0]:
            parts0.update(zip(FIRST_0, exchange_wait(rs_first0, grad["w_down"], "rs_wait_first0")))
        parts, views = [parts0[n], parts1[n]], [_shard_view(n, a[n]) for a in (w, m, v)]
        if n == "w_in":
            parts = [jnp.stack(parts, axis=2).reshape(N_DEV, -1, D_MODEL)]
            views = [a.reshape(1, -1, D_MODEL) for a in views]
        outs = adamw_sharded(parts, *views, f"adamw_{n}")
        if n == "w_in":
            outs = [o.reshape(-1, DEPTH, D_MODEL) for o in outs]
        grad[n], delta[n], new_m[n], new_v[n] = [_shard_unview(n, o) for o in outs]

    no_loss = {"loss": jnp.zeros((1, 1), F32)}
    gparts = exchange_wait(ag_small, outs[0], "ag_wait_small")[0]
    for d, o in zip((grad, delta, new_m, new_v), adamw_small(gparts, *[_pack_small({**a, **no_loss}) for a in (w, m, v)])):
        d.update(_unpack_small(o, {**w, **no_loss}))
    return (grad["loss"][0, 0], grad_x, *[grad[n] for n in WEIGHTS], *[delta[n] for n in WEIGHTS], *[new_m[n] for n in WEIGHTS],
            *[new_v[n] for n in WEIGHTS])
```

```python
import math

import jax
import jax.numpy as jnp
from jax import lax
from jax.experimental import pallas as pl
from jax.experimental.pallas import tpu as pltpu

F32 = jnp.float32
BF16 = jnp.bfloat16

N_DEV = 8
DEPTH = 2
D_MODEL = 1024
HEADS = 4
HEAD_PAD = 128
QK_NOPE = 64
QK_ROPE = 32
QK_HEAD = QK_NOPE + QK_ROPE
V_HEAD = 128
Q_LORA = 256
KV_LORA = 128
SGU_WIDTH = 256
SGU_HEAD_DIM = 64
CHUNK = 128
POOL_WIDTH = 256
POOL_HALO = 16
MLA_WIDTH = 512
IN_WIDTH = 1184
Z_WIDTH = 1280
FFN_HIDDEN = 2816
FFN_CHUNK = 256
ROPE_THETA = 10000.0
EPS = 1e-6
ATTN_SCALE = 1.0 / math.sqrt(QK_HEAD)
LOG2E = 1.4426950408889634
NEG_BIG = -1e30

ADAM_LR = 0.001
ADAM_B1 = 0.9
ADAM_B2 = 0.999
ADAM_EPS = 1e-08
ADAM_WD = 0.01
ADAM_STEP = 10

VMEM_LIMIT = 56 * 1024 * 1024
ROW_TILE = 512
MIXIN_PART = 256
MIXIN_BWD_PART = 128
ATTN_TILE = 1024
ATTN_SUB = 512
ATTN_FWD_SUB = 512
ATTN_BWD_HEADS = 2
MESH = pl.DeviceIdType.MESH

WHOLE = pl.BlockSpec(memory_space=pltpu.VMEM)
ANY = pl.BlockSpec(memory_space=pl.ANY)
HBM_SPEC = pl.BlockSpec(memory_space=pltpu.HBM)
SEM_SPEC = pl.BlockSpec(memory_space=pltpu.SEMAPHORE)


def _cparams(**kw):
    return pltpu.CompilerParams(vmem_limit_bytes=VMEM_LIMIT, **kw)


def _dot(a, b):
    return jnp.dot(a, b, preferred_element_type=F32)


def _dot_nt(a, b):
    return lax.dot_general(a, b, (((1,), (1,)), ((), ())), preferred_element_type=F32)


def _dot_tn(a, b):
    return lax.dot_general(a, b, (((0,), (0,)), ((), ())), preferred_element_type=F32)


def _rms(x, g, n):
    r = lax.rsqrt(jnp.sum(x * x, axis=-1, keepdims=True) * (1.0 / n) + EPS)
    return x * r * g, r


def _rms_bwd(x, r, g, dy, n):
    gdy = dy * g
    dx = r * gdy - x * (r * r * r) * (jnp.sum(x * gdy, axis=-1, keepdims=True) * (1.0 / n))
    dg = jnp.sum(dy * (x * r), axis=0, keepdims=True)
    return dx, dg


def _sigmoid(x):
    return 1.0 / (1.0 + jnp.exp(-x))


def rope_tables(ang):
    T = ang.shape[0]
    tm = min(ROW_TILE, T)

    def body(ang_ref, c_ref, sa_ref, sb_ref):
        a = ang_ref[...]
        lane = lax.broadcasted_iota(jnp.int32, a.shape, 1)
        s = jnp.sin(a)
        c_ref[...] = jnp.cos(a)
        sa_ref[...] = jnp.where(lane < QK_NOPE + QK_ROPE // 2, -s, 0.0)
        sb_ref[...] = jnp.where(lane >= QK_NOPE + QK_ROPE // 2, s, 0.0)

    row = pl.BlockSpec((tm, HEAD_PAD), lambda i: (i, 0))
    return pl.pallas_call(
        body, grid=(T // tm,), name="rope_tables", in_specs=[row], out_specs=[row] * 3,
        out_shape=[jax.ShapeDtypeStruct((T, HEAD_PAD), F32)] * 3, compiler_params=_cparams(),
    )(ang)


def _rope(x, c, sa, sb):
    half = QK_ROPE // 2
    return x * c + pltpu.roll(x, HEAD_PAD - half, 1) * sa + pltpu.roll(x, half, 1) * sb


def _rope_bwd(dy, c, sa, sb):
    half = QK_ROPE // 2
    return dy * c + pltpu.roll(dy * sa, half, 1) + pltpu.roll(dy * sb, HEAD_PAD - half, 1)


def _skewed(stages, groups):
    for t in range(len(stages) + len(groups) - 1):
        for p, g in enumerate(groups):
            if 0 <= t - p < len(stages):
                stages[t - p](g)


def _head_masks(shape, width):
    lane = lax.broadcasted_iota(jnp.int32, shape, len(shape) - 1)
    return [(lane >= width * h) & (lane < width * (h + 1)) for h in range(HEADS)]


def _qkv_pre(z, gql, wq, gkvl, wkv):
    ql = z[:, 0:Q_LORA]
    kvl = z[:, Q_LORA:Q_LORA + KV_LORA]
    kr = z[:, Q_LORA + KV_LORA:Q_LORA + KV_LORA + HEAD_PAD]
    qn, rq = _rms(ql, gql, Q_LORA)
    kvn, rkv = _rms(kvl, gkvl, KV_LORA)
    qn = qn.astype(BF16)
    kvn = kvn.astype(BF16)
    q_up = _dot_nt(qn, wq)
    kv_up = _dot_nt(kvn, wkv)
    return ql, kvl, kr, qn, rq, kvn, rkv, q_up, kv_up


def mixin_fwd(x, rope, gmix, win, gql, wq, gkvl, wkv, gqh, gkh):
    T = x.shape[0]
    tm = min(2 * ROW_TILE, T)
    part = min(MIXIN_PART, tm)

    def body(x_ref, c_ref, sa_ref, sb_ref, gmix_ref, win_ref, gql_ref, wq_ref, gkvl_ref, wkv_ref, gqh_ref, gkh_ref,
             z_ref, q_ref, k_ref, v_ref, kt_ref, vt_ref):
        def project(g):
            hn = _rms(x_ref[g["rows"], :], gmix_ref[...], D_MODEL)[0].astype(BF16)
            g["z"] = _dot_nt(hn, win_ref[...])
            z_ref[g["rows"], :] = g["z"]

        def latents(g):
            _, _, g["kr"], _, _, _, _, g["q_up"], g["kv_up"] = _qkv_pre(g["z"], gql_ref[...], wq_ref[...], gkvl_ref[...],
                                                                        wkv_ref[...])

        def heads(g):
            rows = g["rows"]
            c, sa, sb = c_ref[rows, :], sa_ref[rows, :], sb_ref[rows, :]
            for h in range(HEADS):
                lo = HEAD_PAD * h
                qh = _rms(g["q_up"][:, lo:lo + HEAD_PAD], gqh_ref[...], QK_HEAD)[0]
                q_ref[h, rows, :] = (_rope(qh, c, sa, sb) * (ATTN_SCALE * LOG2E)).astype(BF16)
                kh = _rope(_rms(g["kv_up"][:, lo:lo + HEAD_PAD] + g["kr"], gkh_ref[...], QK_HEAD)[0], c, sa, sb)
                k_ref[h, rows, :] = kh.astype(BF16)
                kt_ref[h, :, rows] = jnp.transpose(kh).astype(BF16)
                vh = g["kv_up"][:, HEADS * HEAD_PAD + lo:HEADS * HEAD_PAD + lo + HEAD_PAD]
                v_ref[h, rows, :] = vh.astype(BF16)
                vt_ref[h, :, rows] = jnp.transpose(vh).astype(BF16)

        _skewed((project, latents, heads), [dict(rows=pl.ds(part * p, part)) for p in range(tm // part)])

    row = lambda w: pl.BlockSpec((tm, w), lambda i: (i, 0))
    head = pl.BlockSpec((HEADS, tm, HEAD_PAD), lambda i: (0, i, 0))
    head_t = pl.BlockSpec((HEADS, HEAD_PAD, tm), lambda i: (0, 0, i))
    return pl.pallas_call(
        body, grid=(T // tm,), name="mixin_fwd",
        in_specs=[row(D_MODEL)] + [row(HEAD_PAD)] * 3 + [WHOLE] * 8,
        out_specs=[row(Z_WIDTH), head, head, head, head_t, head_t],
        out_shape=[jax.ShapeDtypeStruct((T, Z_WIDTH), F32)] + [jax.ShapeDtypeStruct((HEADS, T, HEAD_PAD), BF16)] * 3
                  + [jax.ShapeDtypeStruct((HEADS, HEAD_PAD, T), BF16)] * 2,
        compiler_params=_cparams(),
    )(x, *rope, gmix, win, gql, wq, gkvl, wkv, gqh, gkh)


def _pair_tables(pairs):
    return (jnp.asarray([p[0] for p in pairs], jnp.int32), jnp.asarray([p[1] for p in pairs], jnp.int32))


def attn_fwd(q, k, vt):
    _, T, _ = q.shape
    tb = min(ATTN_TILE, T)
    sq = min(ATTN_FWD_SUB, tb)
    nb = T // tb

    pairs = [(i, j) for i in range(nb) for j in range(i + 1)]

    def body(i_tab, j_tab, q_ref, k_ref, vt_ref, o_ref, lse_ref, m_s, l_s, acc_s):
        i, j = i_tab[pl.program_id(0)], j_tab[pl.program_id(0)]

        @pl.when(j == 0)
        def _():
            m_s[...] = jnp.full(m_s.shape, -jnp.inf, F32)
            l_s[...] = jnp.zeros(l_s.shape, F32)
            acc_s[...] = jnp.zeros(acc_s.shape, F32)

        def scores(g):
            st = _dot_nt(k_ref[g["h"], g["kb"], :], q_ref[g["h"], g["qa"], :])
            if g["masked"]:
                krow = lax.broadcasted_iota(jnp.int32, st.shape, 0)
                qcol = g["q0"] + lax.broadcasted_iota(jnp.int32, st.shape, 1)
                st = jnp.where(krow <= qcol, st, NEG_BIG)
            g["st"] = st

        def new_max(g):
            h, qa = g["h"], g["qa"]
            m_prev = m_s[h, :, qa]
            g["m"] = jnp.maximum(m_prev, jnp.max(g["st"], axis=0, keepdims=True))
            g["alpha"] = jnp.exp2(m_prev - g["m"])
            m_s[h, :, qa] = g["m"]

        def weights(g):
            pt = jnp.exp2(g["st"] - g["m"])
            g["lsum"] = jnp.sum(pt, axis=0, keepdims=True)
            g["pt"] = pt.astype(BF16)

        def accumulate(g):
            h, qa = g["h"], g["qa"]
            l_s[h, :, qa] = g["alpha"] * l_s[h, :, qa] + g["lsum"]
            acc_s[h, :, qa] = g["alpha"] * acc_s[h, :, qa] + _dot(vt_ref[h, :, g["kb"]], g["pt"])

        def tiles(masked):
            return [dict(h=h, q0=q0, qa=slice(q0, q0 + sq), kb=slice(0, q0 + sq if masked else tb), masked=masked)
                    for h in range(HEADS) for q0 in range(0, tb, sq)]

        @pl.when(j < i)
        def _():
            _skewed((scores, new_max, weights, accumulate), tiles(False))

        @pl.when(j == i)
        def _():
            _skewed((scores, new_max, weights, accumulate), tiles(True))
            for h in range(HEADS):
                l = l_s[h]
                o_ref[:, HEAD_PAD * h:HEAD_PAD * (h + 1)] = jnp.transpose(acc_s[h] / l)
                lse_ref[h] = m_s[h] + jnp.log2(l)

    qspec = pl.BlockSpec((HEADS, tb, HEAD_PAD), lambda p, it, jt: (0, it[p], 0))
    kspec = pl.BlockSpec((HEADS, tb, HEAD_PAD), lambda p, it, jt: (0, jt[p], 0))
    vspec = pl.BlockSpec((HEADS, HEAD_PAD, tb), lambda p, it, jt: (0, 0, jt[p]))
    grid_spec = pltpu.PrefetchScalarGridSpec(
        num_scalar_prefetch=2, grid=(len(pairs),),
        in_specs=[qspec, kspec, vspec],
        out_specs=[pl.BlockSpec((tb, MLA_WIDTH), lambda p, it, jt: (it[p], 0)),
                   pl.BlockSpec((HEADS, 1, tb), lambda p, it, jt: (0, 0, it[p]))],
        scratch_shapes=[pltpu.VMEM((HEADS, 1, tb), F32), pltpu.VMEM((HEADS, 1, tb), F32), pltpu.VMEM((HEADS, HEAD_PAD, tb), F32)])
    return pl.pallas_call(
        body, grid_spec=grid_spec, name="attn_fwd",
        out_shape=[jax.ShapeDtypeStruct((T, MLA_WIDTH), F32), jax.ShapeDtypeStruct((HEADS, 1, T), F32)],
        compiler_params=_cparams(),
    )(*_pair_tables(pairs), q, k, vt)


def _sgu_fwd_chunk(vn_c, wcat, bz, masks):
    vstack = jnp.concatenate([jnp.where(mk, vn_c, 0.0).astype(BF16) for mk in masks], axis=0)
    return _dot(wcat, vstack) + bz


def _pool_counts(i, tm):
    pos1 = (i * tm + 1 + lax.broadcasted_iota(jnp.int32, (tm, POOL_WIDTH), 0)).astype(F32)
    lane = lax.broadcasted_iota(jnp.int32, (tm, POOL_WIDTH), 1)
    win = jnp.where(lane < 64, 2.0, jnp.where(lane < 128, 4.0, jnp.where(lane < 192, 8.0, 16.0)))
    return jnp.minimum(pos1, win), lane


def _by_group(lane, s2, s4, s8, s16):
    return jnp.where(lane < 64, s2, jnp.where(lane < 128, s4, jnp.where(lane < 192, s8, s16)))


def _pool_means(pin, halo, i, tm):
    s1 = jnp.concatenate([halo, pin], axis=0)
    s2 = s1 + pltpu.roll(s1, 1, 0)
    s4 = s2 + pltpu.roll(s2, 2, 0)
    s8 = s4 + pltpu.roll(s4, 4, 0)
    s16 = s8 + pltpu.roll(s8, 8, 0)
    cnt, lane = _pool_counts(i, tm)
    sel = _by_group(lane, s2[POOL_HALO:], s4[POOL_HALO:], s8[POOL_HALO:], s16[POOL_HALO:])
    return sel / cnt - pin


def _mixers_fwd_tile(i, tm, a, u, vs, pin, halo, gsgu, wcat, bz, wp, pscale, goa, gos, gop):
    vn, rv = _rms(vs, gsgu, SGU_WIDTH)
    masks = _head_masks((CHUNK, SGU_WIDTH), SGU_HEAD_DIM)
    zc = jnp.concatenate([_sgu_fwd_chunk(vn[CHUNK * c:CHUNK * (c + 1)], wcat, bz, masks) for c in range(tm // CHUNK)], axis=0)
    gm = u * zc
    halo = jnp.where(i > 0, halo, 0.0)
    m = _pool_means(pin, halo, i, tm).astype(BF16)
    yp_pre = _dot(m, wp)
    yp = yp_pre * pscale
    na, ra = _rms(a, goa, MLA_WIDTH)
    ng, rg = _rms(gm, gos, SGU_WIDTH)
    npo, rp = _rms(yp, gop, POOL_WIDTH)
    mix = jnp.concatenate([na, ng, npo], axis=1).astype(BF16)
    return vn, rv, zc, gm, m, yp_pre, yp, ra, rg, rp, mix


def _z_specs(tm):
    col = lambda c: pl.BlockSpec((tm, 256), lambda i: (i, c))
    halo = pl.BlockSpec((POOL_HALO, 256), lambda i: (jnp.maximum(i * (tm // POOL_HALO) - 1, 0), 4))
    return [col(2), col(3), col(4), halo]


def mixers_fwd(x, a, z, gsgu, wcat, bz, wp, pscale, goa, gos, gop, wout):
    T = x.shape[0]
    tm = min(ROW_TILE, T)

    def body(x_ref, a_ref, u_ref, vs_ref, pin_ref, halo_ref, gsgu_ref, wcat_ref, bz_ref, wp_ref, ps_ref,
             goa_ref, gos_ref, gop_ref, wout_ref, x1_ref):
        i = pl.program_id(0)
        mix = _mixers_fwd_tile(i, tm, a_ref[...], u_ref[...], vs_ref[...], pin_ref[...], halo_ref[...], gsgu_ref[...],
                               wcat_ref[...], bz_ref[...], wp_ref[...], ps_ref[...], goa_ref[...], gos_ref[...],
                               gop_ref[...])[-1]
        x1_ref[...] = x_ref[...] + _dot(mix, wout_ref[...])

    row = lambda w: pl.BlockSpec((tm, w), lambda i: (i, 0))
    return pl.pallas_call(
        body, grid=(T // tm,), name="mixers_fwd",
        in_specs=[row(D_MODEL), row(MLA_WIDTH)] + _z_specs(tm) + [WHOLE] * 9,
        out_specs=row(D_MODEL),
        out_shape=jax.ShapeDtypeStruct((T, D_MODEL), F32),
        compiler_params=_cparams(),
    )(x, a, z, z, z, z, gsgu, wcat, bz, wp, pscale, goa, gos, gop, wout)


def ffn_fwd(x1, gffn, wg, wu, wd, target=None):
    T = x1.shape[0]
    tm = min(ROW_TILE, T)
    with_loss = target is not None

    def body(*refs):
        x1_ref, gffn_ref, wg_ref, wu_ref, wd_ref = refs[:5]
        outs = refs[6:] if with_loss else refs[5:]
        x1v = x1_ref[...]
        h2 = _rms(x1v, gffn_ref[...], D_MODEL)[0].astype(BF16)
        acc = x1v
        for c in range(FFN_HIDDEN // FFN_CHUNK):
            sl = slice(FFN_CHUNK * c, FFN_CHUNK * (c + 1))
            g = _dot_nt(h2, wg_ref[sl, :])
            u = _dot_nt(h2, wu_ref[sl, :])
            outs[-2][:, sl] = g.astype(BF16)
            outs[-1][:, sl] = u.astype(BF16)
            act = (g * _sigmoid(g) * u).astype(BF16)
            acc = acc + _dot(act, wd_ref[sl, :])
        if not with_loss:
            outs[0][...] = acc
            return
        dy_ref, loss_ref = outs[0], outs[1]

        @pl.when(pl.program_id(0) == 0)
        def _():
            loss_ref[...] = jnp.zeros(loss_ref.shape, F32)

        err = acc - refs[5][...]
        dy_ref[...] = err * (1.0 / D_MODEL)
        per_row = jnp.sum(err * err, axis=1, keepdims=True) * (1.0 / D_MODEL)
        loss_ref[...] += 0.5 * jnp.sum(per_row, axis=0, keepdims=True)

    row = lambda w: pl.BlockSpec((tm, w), lambda i: (i, 0))
    hidden = [jax.ShapeDtypeStruct((T, FFN_HIDDEN), BF16)] * 2
    if with_loss:
        return pl.pallas_call(
            body, grid=(T // tm,), name="ffn_fwd_loss",
            in_specs=[row(D_MODEL)] + [WHOLE] * 4 + [row(D_MODEL)],
            out_specs=[row(D_MODEL), pl.BlockSpec((1, 1), lambda i: (0, 0)), row(FFN_HIDDEN), row(FFN_HIDDEN)],
            out_shape=[jax.ShapeDtypeStruct((T, D_MODEL), F32), jax.ShapeDtypeStruct((1, 1), F32)] + hidden,
            compiler_params=_cparams(),
        )(x1, gffn, wg, wu, wd, target)
    return pl.pallas_call(
        body, grid=(T // tm,), name="ffn_fwd",
        in_specs=[row(D_MODEL)] + [WHOLE] * 4,
        out_specs=[row(D_MODEL), row(FFN_HIDDEN), row(FFN_HIDDEN)],
        out_shape=[jax.ShapeDtypeStruct((T, D_MODEL), F32)] + hidden,
        compiler_params=_cparams(),
    )(x1, gffn, wg, wu, wd)


def _acc_spec(shape):
    return pl.BlockSpec(shape, lambda i: (0,) * len(shape))


def ffn_bwd(dx2, x1, gs, us, gffn, wg, wu, wd):
    T = x1.shape[0]
    tm = min(ROW_TILE // 2, T)

    def body(dx2_ref, x1_ref, gs_ref, us_ref, gffn_ref, wg_ref, wu_ref, wd_ref,
             dx1_ref, h2_ref, act_ref, dg_ref, du_ref, dgffn_ref):
        @pl.when(pl.program_id(0) == 0)
        def _():
            dgffn_ref[...] = jnp.zeros(dgffn_ref.shape, F32)

        dx2v = dx2_ref[...]
        dy = dx2v.astype(BF16)
        x1v = x1_ref[...]
        h2, r = _rms(x1v, gffn_ref[...], D_MODEL)
        h2_ref[...] = h2.astype(BF16)
        for c in range(FFN_HIDDEN // FFN_CHUNK):
            sl = slice(FFN_CHUNK * c, FFN_CHUNK * (c + 1))
            g = gs_ref[:, sl].astype(F32)
            u = us_ref[:, sl].astype(F32)
            dact = _dot_nt(dy, wd_ref[sl, :])
            sg = _sigmoid(g)
            silu = g * sg
            act_ref[:, sl] = (silu * u).astype(BF16)
            dg_ref[:, sl] = (dact * u * (sg * (1.0 + g * (1.0 - sg)))).astype(BF16)
            du_ref[:, sl] = (dact * silu).astype(BF16)
        dh2 = _dot(dg_ref[...], wg_ref[...]) + _dot(du_ref[...], wu_ref[...])
        dxn, dgn = _rms_bwd(x1v, r, gffn_ref[...], dh2, D_MODEL)
        dx1_ref[...] = dx2v + dxn
        dgffn_ref[...] += dgn

    row = lambda w: pl.BlockSpec((tm, w), lambda i: (i, 0))
    return pl.pallas_call(
        body, grid=(T // tm,), name="ffn_bwd",
        in_specs=[row(D_MODEL), row(D_MODEL), row(FFN_HIDDEN), row(FFN_HIDDEN)] + [WHOLE] * 4,
        out_specs=[row(D_MODEL), row(D_MODEL), row(FFN_HIDDEN), row(FFN_HIDDEN), row(FFN_HIDDEN), _acc_spec((1, D_MODEL))],
        out_shape=[jax.ShapeDtypeStruct((T, D_MODEL), F32), jax.ShapeDtypeStruct((T, D_MODEL), BF16),
                   jax.ShapeDtypeStruct((T, FFN_HIDDEN), BF16), jax.ShapeDtypeStruct((T, FFN_HIDDEN), BF16),
                   jax.ShapeDtypeStruct((T, FFN_HIDDEN), BF16), jax.ShapeDtypeStruct((1, D_MODEL), F32)],
        compiler_params=_cparams(),
    )(dx2, x1, gs, us, gffn, wg, wu, wd)


TN_K_TILE = 2048
TN_ACC_BYTES = 6 * 1024 * 1024


def matmul_tn(a, b, name):
    T, M = a.shape
    N = b.shape[1]
    tk = min(TN_K_TILE, T)
    tm = M if M <= 1024 else M // 2
    tn = max(d for d in range(128, N + 1, 128) if N % d == 0 and tm * d * 4 <= TN_ACC_BYTES)
    nk = T // tk

    def body(a_ref, b_ref, o_ref, acc):
        k = pl.program_id(2)

        @pl.when(k == 0)
        def _():
            acc[...] = jnp.zeros(acc.shape, F32)

        acc[...] += _dot_tn(a_ref[...].astype(BF16), b_ref[...].astype(BF16))

        @pl.when(k == nk - 1)
        def _():
            o_ref[...] = acc[...].astype(BF16)

    return pl.pallas_call(
        body, grid=(M // tm, N // tn, nk), name=name,
        in_specs=[pl.BlockSpec((tk, tm), lambda i, j, k: (k, i)), pl.BlockSpec((tk, tn), lambda i, j, k: (k, j))],
        out_specs=pl.BlockSpec((tm, tn), lambda i, j, k: (i, j)),
        out_shape=jax.ShapeDtypeStruct((M, N), BF16),
        scratch_shapes=[pltpu.VMEM((tm, tn), F32)],
        compiler_params=_cparams(),
    )(a, b)


def mixers_bwd(dx1, a, z, gsgu, wcat, wcat_t, bz, wp, pscale, goa, gos, gop, wout):
    T = a.shape[0]
    tm = min(ROW_TILE, T)

    def body(dx1_ref, a_ref, u_ref, vs_ref, pin_ref, halo_ref, gsgu_ref, wcat_ref, wcatt_ref, bz_ref, wp_ref,
             ps_ref, goa_ref, gos_ref, gop_ref, wout_ref,
             da_ref, delta_ref, du_ref, dvs_ref, dm_ref, mix_ref,
             dgoa_ref, dgos_ref, dgop_ref, dps_ref, dwp_ref, dwsp_ref, db_ref, dgsgu_ref):
        i = pl.program_id(0)

        @pl.when(i == 0)
        def _():
            for r in (dgoa_ref, dgos_ref, dgop_ref, dps_ref, dwp_ref, dwsp_ref, db_ref, dgsgu_ref):
                r[...] = jnp.zeros(r.shape, F32)

        a_v, u, vs = a_ref[...], u_ref[...], vs_ref[...]
        goa, gos, gop, pscale_v = goa_ref[...], gos_ref[...], gop_ref[...], ps_ref[...]
        vn, rv, zc, gm, m, yp_pre, yp, ra, rg, rp, mix = _mixers_fwd_tile(
            i, tm, a_v, u, vs, pin_ref[...], halo_ref[...], gsgu_ref[...], wcat_ref[...], bz_ref[...], wp_ref[...],
            pscale_v, goa, gos, gop)
        mix_ref[...] = mix
        dmix = _dot_nt(dx1_ref[...].astype(BF16), wout_ref[...])
        da, dgoa = _rms_bwd(a_v, ra, goa, dmix[:, :MLA_WIDTH], MLA_WIDTH)
        dgm, dgos = _rms_bwd(gm, rg, gos, dmix[:, MLA_WIDTH:MLA_WIDTH + SGU_WIDTH], SGU_WIDTH)
        dyp, dgop = _rms_bwd(yp, rp, gop, dmix[:, MLA_WIDTH + SGU_WIDTH:], POOL_WIDTH)
        da_ref[...] = da
        dgoa_ref[...] += dgoa
        dgos_ref[...] += dgos
        dgop_ref[...] += dgop
        prod = da * a_v
        ones = jnp.ones((8, HEAD_PAD), F32)
        for h in range(HEADS):
            lo = HEAD_PAD * h
            sums = lax.dot_general(ones, prod[:, lo:lo + HEAD_PAD], (((1,), (1,)), ((), ())), preferred_element_type=F32,
                                   precision=lax.Precision.HIGHEST)
            delta_ref[h] = sums[0:1, :]
        dps_ref[...] += jnp.sum(dyp * yp_pre, axis=0, keepdims=True)
        dyp_pre = (dyp * pscale_v).astype(BF16)
        dwp_ref[...] += _dot_tn(m, dyp_pre)
        dm_ref[...] = _dot_nt(dyp_pre, wp_ref[...])
        du_ref[...] = dgm * zc
        dzc = dgm * u
        masks = _head_masks((CHUNK, SGU_WIDTH), SGU_HEAD_DIM)
        lane_b = lax.broadcasted_iota(jnp.int32, (CHUNK, HEAD_PAD), 1)
        dvn_parts = []
        dwsp = jnp.zeros(dwsp_ref.shape, F32)
        db = jnp.zeros(db_ref.shape, F32)
        for c in range(tm // CHUNK):
            dz_c = dzc[CHUNK * c:CHUNK * (c + 1)]
            dzstack = jnp.concatenate([jnp.where(mk, dz_c, 0.0).astype(BF16) for mk in masks], axis=0)
            dvn_parts.append(_dot(wcatt_ref[...], dzstack))
            dwsp = dwsp + _dot_nt(dzstack, vn[CHUNK * c:CHUNK * (c + 1)].astype(BF16))
            for h, mk in enumerate(masks):
                col = jnp.sum(jnp.where(mk, dz_c, 0.0), axis=1, keepdims=True)
                db = db + jnp.where(lane_b == h, col, 0.0)
        dwsp_ref[...] += dwsp
        db_ref[...] += db
        dvs, dgsgu = _rms_bwd(vs, rv, gsgu_ref[...], jnp.concatenate(dvn_parts, axis=0), SGU_WIDTH)
        dvs_ref[...] = dvs
        dgsgu_ref[...] += dgsgu

    row = lambda w: pl.BlockSpec((tm, w), lambda i: (i, 0))
    head = pl.BlockSpec((HEADS, 1, tm), lambda i: (0, 0, i))
    acc_shapes = [(1, MLA_WIDTH), (1, SGU_WIDTH), (1, POOL_WIDTH), (1, POOL_WIDTH), (POOL_WIDTH, POOL_WIDTH),
                  (HEADS * CHUNK, CHUNK), (CHUNK, HEAD_PAD), (1, SGU_WIDTH)]
    return pl.pallas_call(
        body, grid=(T // tm,), name="mixers_bwd",
        in_specs=[row(D_MODEL), row(MLA_WIDTH)] + _z_specs(tm) + [WHOLE] * 10,
        out_specs=[row(MLA_WIDTH), head, row(256), row(256), row(256), row(D_MODEL)] + [_acc_spec(s) for s in acc_shapes],
        out_shape=[jax.ShapeDtypeStruct((T, MLA_WIDTH), F32), jax.ShapeDtypeStruct((HEADS, 1, T), F32),
                   jax.ShapeDtypeStruct((T, 256), F32), jax.ShapeDtypeStruct((T, 256), F32),
                   jax.ShapeDtypeStruct((T, 256), F32), jax.ShapeDtypeStruct((T, D_MODEL), BF16)]
                  + [jax.ShapeDtypeStruct(s, F32) for s in acc_shapes],
        compiler_params=_cparams(),
    )(dx1, a, z, z, z, z, gsgu, wcat, wcat_t, bz, wp, pscale, goa, gos, gop, wout)


def pool_bwd(dm):
    T = dm.shape[0]
    tm = min(4 * ROW_TILE, T)
    nt = T // tm

    def body(dm_ref, next_ref, dpin_ref):
        i = pl.program_id(0)
        cnt, lane = _pool_counts(i, tm)
        dmv = dm_ref[...]
        win = _by_group(lane[:POOL_HALO], 2.0, 4.0, 8.0, 16.0)
        nxt = jnp.where(i < nt - 1, next_ref[...] / win, 0.0)
        r1 = jnp.concatenate([dmv / cnt, nxt], axis=0)
        n = tm + POOL_HALO
        r2 = r1 + pltpu.roll(r1, n - 1, 0)
        r4 = r2 + pltpu.roll(r2, n - 2, 0)
        r8 = r4 + pltpu.roll(r4, n - 4, 0)
        r16 = r8 + pltpu.roll(r8, n - 8, 0)
        dpin_ref[...] = _by_group(lane, r2[:tm], r4[:tm], r8[:tm], r16[:tm]) - dmv

    return pl.pallas_call(
        body, grid=(nt,), name="pool_bwd",
        in_specs=[pl.BlockSpec((tm, 256), lambda i: (i, 0)),
                  pl.BlockSpec((POOL_HALO, 256), lambda i: (jnp.minimum((i + 1) * (tm // POOL_HALO), T // POOL_HALO - 1), 0))],
        out_specs=pl.BlockSpec((tm, 256), lambda i: (i, 0)),
        out_shape=jax.ShapeDtypeStruct((T, 256), F32),
        compiler_params=_cparams(),
    )(dm, dm)


def attn_bwd(q, k, kt, v, do, lse, delta):
    _, T, _ = q.shape
    tb = min(ATTN_TILE, T)
    sb = min(ATTN_SUB, tb)
    ns = tb // sb
    nb = T // tb
    hb = ATTN_BWD_HEADS

    pairs = [(i, j) for i in range(nb) for j in range(i, nb)]

    def body(i_tab, j_tab, q_ref, k_ref, kt_ref, v_ref, do_ref, lse_ref, delta_ref, dqt_ref, dk_ref, dv_ref, dk_s, dv_s):
        i, j = i_tab[pl.program_id(1)], j_tab[pl.program_id(1)]

        @pl.when(pl.program_id(1) == 0)
        def _():
            dqt_ref[...] = jnp.zeros(dqt_ref.shape, F32)

        @pl.when(j == i)
        def _():
            dk_s[...] = jnp.zeros(dk_s.shape, F32)
            dv_s[...] = jnp.zeros(dv_s.shape, F32)

        def sub_block(h, a, b, masked):
            qa = slice(sb * a, sb * (a + 1))
            kb = slice(sb * b, sb * (b + 1))
            qv = q_ref[h, qa, :]
            dov = do_ref[qa, HEAD_PAD * h:HEAD_PAD * (h + 1)].astype(BF16)
            st = _dot_nt(k_ref[h, kb, :], qv)
            dpt = _dot_nt(v_ref[h, kb, :], dov)
            pt = jnp.exp2(st - lse_ref[h, :, qa])
            if masked:
                krow = lax.broadcasted_iota(jnp.int32, st.shape, 0)
                qcol = lax.broadcasted_iota(jnp.int32, st.shape, 1)
                pt = jnp.where(krow <= qcol, pt, 0.0)
            dst = (pt * (dpt - delta_ref[h, :, qa])).astype(BF16)
            dv_s[h, kb, :] += _dot(pt.astype(BF16), dov)
            dk_s[h, kb, :] += _dot(dst, qv)
            cols = pl.ds(pl.multiple_of(j * tb + sb * a, sb), sb)
            dqt_ref[h, :, cols] += _dot(kt_ref[h, :, kb], dst)

        @pl.when(j > i)
        def _():
            for a in range(ns):
                for h in range(hb):
                    for b in range(ns):
                        sub_block(h, a, b, False)

        @pl.when(j == i)
        def _():
            for a in range(ns):
                for h in range(hb):
                    for b in range(a + 1):
                        sub_block(h, a, b, a == b)

        @pl.when(j == nb - 1)
        def _():
            dk_ref[...] = dk_s[...] * (1.0 / LOG2E)
            dv_ref[...] = dv_s[...]

    qspec = pl.BlockSpec((hb, tb, HEAD_PAD), lambda g, p, it, jt: (g, jt[p], 0))
    kspec = pl.BlockSpec((hb, tb, HEAD_PAD), lambda g, p, it, jt: (g, it[p], 0))
    ktspec = pl.BlockSpec((hb, HEAD_PAD, tb), lambda g, p, it, jt: (g, 0, it[p]))
    rowspec = pl.BlockSpec((hb, 1, tb), lambda g, p, it, jt: (g, 0, jt[p]))
    grid_spec = pltpu.PrefetchScalarGridSpec(
        num_scalar_prefetch=2, grid=(HEADS // hb, len(pairs)),
        in_specs=[qspec, kspec, ktspec, kspec, pl.BlockSpec((tb, hb * HEAD_PAD), lambda g, p, it, jt: (jt[p], g)), rowspec, rowspec],
        out_specs=[pl.BlockSpec((hb, HEAD_PAD, T), lambda g, p, it, jt: (g, 0, 0)), kspec, kspec],
        scratch_shapes=[pltpu.VMEM((hb, tb, HEAD_PAD), F32), pltpu.VMEM((hb, tb, HEAD_PAD), F32)])
    return pl.pallas_call(
        body, grid_spec=grid_spec, name="attn_bwd",
        out_shape=[jax.ShapeDtypeStruct((HEADS, HEAD_PAD, T), F32)] + [jax.ShapeDtypeStruct((HEADS, T, HEAD_PAD), F32)] * 2,
        compiler_params=_cparams(),
    )(*_pair_tables(pairs), q, k, kt, v, do, lse, delta)


def mixin_bwd(dres, x, z, rope, dqt, dk, dv, du, dvs, dpin, gmix, win, gql, wq, gkvl, wkv, gqh, gkh):
    T = x.shape[0]
    tm = min(ROW_TILE, T)
    part = min(MIXIN_BWD_PART, tm)

    def body(dres_ref, x_ref, z_ref, c_ref, sa_ref, sb_ref, dqt_ref, dk_ref, dv_ref, du_ref, dvs_ref, dpin_ref,
             gmix_ref, win_ref, gql_ref, wq_ref, gkvl_ref, wkv_ref, gqh_ref, gkh_ref,
             dx_ref, hn_ref, dz_ref, qn_ref, dqup_ref, kvn_ref, dkvup_ref,
             dgmix_ref, dgql_ref, dgkvl_ref, dgqh_ref, dgkh_ref):
        @pl.when(pl.program_id(0) == 0)
        def _():
            for r in (dgmix_ref, dgql_ref, dgkvl_ref, dgqh_ref, dgkh_ref):
                r[...] = jnp.zeros(r.shape, F32)

        lane = lax.broadcasted_iota(jnp.int32, (part, HEAD_PAD), 1)
        rope_lanes = (lane >= QK_NOPE) & (lane < QK_HEAD)

        def recompute(g):
            rows = g["rows"]
            g["xv"] = x_ref[rows, :]
            hn, g["rx"] = _rms(g["xv"], gmix_ref[...], D_MODEL)
            hn_ref[rows, :] = hn.astype(BF16)
            g["ql"], g["kvl"], g["kr"], qn, g["rq"], kvn, g["rkv"], g["q_up"], g["kv_up"] = _qkv_pre(
                z_ref[rows, :], gql_ref[...], wq_ref[...], gkvl_ref[...], wkv_ref[...])
            qn_ref[rows, :] = qn
            kvn_ref[rows, :] = kvn

        def heads(g):
            rows = g["rows"]
            c, sa, sb = c_ref[rows, :], sa_ref[rows, :], sb_ref[rows, :]
            dkr = jnp.zeros((part, HEAD_PAD), F32)
            dgqh = jnp.zeros((1, HEAD_PAD), F32)
            dgkh = jnp.zeros((1, HEAD_PAD), F32)
            dq_parts, dk_parts, dv_parts = [], [], []
            for h in range(HEADS):
                lo = HEAD_PAD * h
                qh = g["q_up"][:, lo:lo + HEAD_PAD]
                rqh = lax.rsqrt(jnp.sum(qh * qh, axis=-1, keepdims=True) * (1.0 / QK_HEAD) + EPS)
                dq_h = jnp.transpose(dqt_ref[h, :, rows]) * ATTN_SCALE
                dqh, dg = _rms_bwd(qh, rqh, gqh_ref[...], _rope_bwd(dq_h, c, sa, sb), QK_HEAD)
                dgqh = dgqh + dg
                dq_parts.append(dqh)
                kh = g["kv_up"][:, lo:lo + HEAD_PAD] + g["kr"]
                rkh = lax.rsqrt(jnp.sum(kh * kh, axis=-1, keepdims=True) * (1.0 / QK_HEAD) + EPS)
                dkh, dg = _rms_bwd(kh, rkh, gkh_ref[...], _rope_bwd(dk_ref[h, rows, :], c, sa, sb), QK_HEAD)
                dgkh = dgkh + dg
                dkr = dkr + jnp.where(rope_lanes, dkh, 0.0)
                dk_parts.append(dkh)
                dv_parts.append(dv_ref[h, rows, :])
            dgqh_ref[...] += dgqh
            dgkh_ref[...] += dgkh
            g["dkr"] = dkr
            g["dq_up"] = jnp.concatenate(dq_parts, axis=1).astype(BF16)
            g["dkv_up"] = jnp.concatenate(dk_parts + dv_parts, axis=1).astype(BF16)
            dqup_ref[rows, :] = g["dq_up"]
            dkvup_ref[rows, :] = g["dkv_up"]

        def latents(g):
            rows = g["rows"]
            dql, dg = _rms_bwd(g["ql"], g["rq"], gql_ref[...], _dot(g["dq_up"], wq_ref[...]), Q_LORA)
            dgql_ref[...] += dg
            dkvl, dg = _rms_bwd(g["kvl"], g["rkv"], gkvl_ref[...], _dot(g["dkv_up"], wkv_ref[...]), KV_LORA)
            dgkvl_ref[...] += dg
            g["dz"] = jnp.concatenate([dql, dkvl, g["dkr"], du_ref[rows, :], dvs_ref[rows, :], dpin_ref[rows, :]],
                                      axis=1).astype(BF16)
            dz_ref[rows, :] = g["dz"]

        def inputs(g):
            rows = g["rows"]
            dxn, dg = _rms_bwd(g["xv"], g["rx"], gmix_ref[...], _dot(g["dz"], win_ref[...]), D_MODEL)
            dgmix_ref[...] += dg
            dx_ref[rows, :] = dres_ref[rows, :] + dxn

        _skewed((recompute, heads, latents, inputs), [dict(rows=pl.ds(part * p, part)) for p in range(tm // part)])

    row = lambda w: pl.BlockSpec((tm, w), lambda i: (i, 0))
    head = pl.BlockSpec((HEADS, tm, HEAD_PAD), lambda i: (0, i, 0))
    head_t = pl.BlockSpec((HEADS, HEAD_PAD, tm), lambda i: (0, 0, i))
    acc_shapes = [(1, D_MODEL), (1, Q_LORA), (1, KV_LORA), (1, HEAD_PAD), (1, HEAD_PAD)]
    out_rows = [(D_MODEL, F32), (D_MODEL, BF16), (Z_WIDTH, BF16), (Q_LORA, BF16), (HEADS * HEAD_PAD, BF16),
                (KV_LORA, BF16), (2 * HEADS * HEAD_PAD, BF16)]
    return pl.pallas_call(
        body, grid=(T // tm,), name="mixin_bwd",
        in_specs=[row(D_MODEL), row(D_MODEL), row(Z_WIDTH)] + [row(HEAD_PAD)] * 3 + [head_t, head, head, row(256), row(256), row(256)]
                 + [WHOLE] * 8,
        out_specs=[row(w) for w, _ in out_rows] + [_acc_spec(s) for s in acc_shapes],
        out_shape=[jax.ShapeDtypeStruct((T, w), dt) for w, dt in out_rows] + [jax.ShapeDtypeStruct(s, F32) for s in acc_shapes],
        compiler_params=_cparams(),
    )(dres, x, z, *rope, dqt, dk, dv, du, dvs, dpin, gmix, win, gql, wq, gkvl, wkv, gqh, gkh)


def _place():
    x, y, c = lax.axis_index("x"), lax.axis_index("y"), lax.axis_index("c")
    return x, y, c, 4 * x + 2 * y + c


def _layer_of(ref, l):
    return ref[:, l, :] if ref.shape[1] == DEPTH else ref[l]


def _layer_shape(shard):
    return (shard.shape[0], shard.shape[2]) if shard.shape[1] == DEPTH else shard.shape[1:]


def cast_shards(shards, wanted):
    n = len(shards)

    def body(*refs):
        for o_ref, (w, l) in zip(refs[n:], wanted):
            o_ref[...] = _layer_of(refs[w], l).astype(BF16)

    return pl.pallas_call(
        body, name="cast_shards", in_specs=[WHOLE] * n, out_specs=[WHOLE] * len(wanted),
        out_shape=[jax.ShapeDtypeStruct(_layer_shape(shards[w]), BF16) for w, _ in wanted],
        compiler_params=_cparams(),
    )(*shards)


def _peer(k):
    x, y, c, _ = _place()
    px = 1 - x if k & 4 else x
    py = 1 - y if k & 2 else y
    pc = 1 - c if k & 1 else c
    return (px, py, pc), 4 * px + 2 * py + pc


def exchange_start(srcs, after, gather, name):
    n = len(srcs)
    land_shapes = [((N_DEV,) + s.shape) if gather else s.shape for s in srcs]

    def body(*refs):
        src_refs, land_refs = refs[:n], refs[n:2 * n]
        send_sems, recv_sems = refs[2 * n + 1:3 * n + 1], refs[3 * n + 1:4 * n + 1]
        token = refs[-1]
        _, _, _, me = _place()
        for w in range(n):
            for k in range(1, N_DEV):
                peer, peer_id = _peer(k)
                pltpu.make_async_remote_copy(
                    src_ref=src_refs[w] if gather else src_refs[w].at[peer_id], dst_ref=land_refs[w].at[me],
                    send_sem=send_sems[w], recv_sem=recv_sems[w], device_id=peer, device_id_type=MESH).start()
        for w in range(n):
            pltpu.make_async_copy(src_refs[w] if gather else src_refs[w].at[me], land_refs[w].at[me], recv_sems[w]).start()
        token[...] = jnp.zeros(token.shape, F32)

    hbm = lambda a: pltpu.with_memory_space_constraint(a, pltpu.HBM)
    outs = pl.pallas_call(
        body, name=name,
        out_shape=(pltpu.SemaphoreType.DMA(()),) * (2 * n)
                  + tuple(pltpu.HBM(s.shape, s.dtype) for s in srcs)
                  + tuple(pltpu.HBM(l, s.dtype) for l, s in zip(land_shapes, srcs))
                  + (jax.ShapeDtypeStruct((8, 128), F32),),
        in_specs=[HBM_SPEC] * (2 * n) + [ANY],
        out_specs=(SEM_SPEC,) * (2 * n) + (HBM_SPEC,) * (2 * n) + (WHOLE,),
        input_output_aliases={i: 2 * n + i for i in range(2 * n)},
        compiler_params=pltpu.CompilerParams(has_side_effects=pltpu.SideEffectType.DATAFLOW_SIDE_EFFECTING),
    )(*[hbm(s) for s in srcs], *[hbm(lax.empty(l, s.dtype)) for l, s in zip(land_shapes, srcs)], after)
    return list(outs[:n]), list(outs[n:2 * n]), list(outs[2 * n:3 * n]), list(outs[3 * n:4 * n]), outs[-1]


def exchange_wait(started, after, name):
    send_sems, recv_sems, srcs, lands, _ = started
    n = len(srcs)

    def body(*refs):
        land_refs = refs[n:2 * n]
        send_sems, recv_sems = refs[2 * n:3 * n], refs[3 * n:4 * n]
        x, y, c, _ = _place()
        for w in range(n):
            seven, eight = land_refs[w].at[pl.ds(0, N_DEV - 1)], land_refs[w]
            pltpu.make_async_remote_copy(src_ref=seven, dst_ref=seven, send_sem=send_sems[w], recv_sem=recv_sems[w],
                                         device_id=(x, y, c), device_id_type=MESH).wait_send()
            pltpu.make_async_remote_copy(src_ref=eight, dst_ref=eight, send_sem=send_sems[w], recv_sem=recv_sems[w],
                                         device_id=(x, y, c), device_id_type=MESH).wait_recv()

    outs = pl.pallas_call(
        body, name=name,
        out_shape=tuple(pltpu.HBM(a.shape, a.dtype) for a in (*srcs, *lands)),
        in_specs=[HBM_SPEC] * (2 * n) + [SEM_SPEC] * (2 * n) + [ANY],
        out_specs=(HBM_SPEC,) * (2 * n),
        input_output_aliases={i: i for i in range(2 * n)},
        compiler_params=pltpu.CompilerParams(has_side_effects=pltpu.SideEffectType.DATAFLOW_SIDE_EFFECTING),
    )(*srcs, *lands, *send_sems, *recv_sems, after)
    return list(outs[n:])


def _adamw(w, g, m, v):
    m2 = ADAM_B1 * m + (1.0 - ADAM_B1) * g
    v2 = ADAM_B2 * v + (1.0 - ADAM_B2) * (g * g)
    m_hat = m2 / (1.0 - ADAM_B1 ** ADAM_STEP)
    v_hat = v2 / (1.0 - ADAM_B2 ** ADAM_STEP)
    delta = -ADAM_LR * (m_hat / (jnp.sqrt(v_hat) + ADAM_EPS) + ADAM_WD * w)
    return delta, m2, v2


def adamw_sharded(parts, w, m, v, name):
    L, R, C = w.shape
    fits = [d for d in range(16, min(R, 512) + 1, 16) if R % d == 0]
    br = max(fits) if fits else R
    nblk = R // br

    def body(*refs):
        p_refs = refs[:L]
        w_ref, m_ref, v_ref, g_ref, d_ref, m2_ref, v2_ref = refs[L:]

        def total(p_ref):
            g = p_ref[0].astype(F32)
            for s in range(1, N_DEV):
                g = g + p_ref[s].astype(F32)
            return g

        g = total(p_refs[0])
        for l in range(1, L):
            g = jnp.where(pl.program_id(0) == l, total(p_refs[l]), g)
        g_ref[...] = g
        d_ref[...], m2_ref[...], v2_ref[...] = _adamw(w_ref[...], g, m_ref[...], v_ref[...])

    blk = pl.BlockSpec((None, br, C), lambda l, i: (l, i, 0))

    def part_spec(k):
        return pl.BlockSpec((N_DEV, br, C), lambda l, i: (0, jnp.where(l == k, i, jnp.where(l < k, 0, nblk - 1)), 0))

    return pl.pallas_call(
        body, grid=(L, nblk), name=name,
        in_specs=[part_spec(k) for k in range(L)] + [blk, blk, blk],
        out_specs=[blk] * 4,
        out_shape=[jax.ShapeDtypeStruct((L, R, C), F32)] * 4,
        compiler_params=_cparams(),
    )(*parts, w, m, v)


def adamw_small(gparts, w, m, v):
    R = gparts.shape[1]

    def body(g_ref, w_ref, m_ref, v_ref, grad_ref, d_ref, m2_ref, v2_ref):
        g = g_ref[0]
        for s in range(1, N_DEV):
            g = g + g_ref[s]
        grad_ref[...] = g
        d_ref[...], m2_ref[...], v2_ref[...] = _adamw(w_ref[...], g, m_ref[...], v_ref[...])

    return pl.pallas_call(
        body, name="adamw_small",
        in_specs=[WHOLE] * 4, out_specs=[WHOLE] * 4,
        out_shape=[jax.ShapeDtypeStruct((R, 128), F32)] * 4,
        compiler_params=_cparams(),
    )(gparts, w, m, v)


def _shard_view(name, a):
    if name == "w_in":
        return a.transpose(2, 0, 1)
    return a.swapaxes(1, 2) if name in TRANSPOSED else a


def _shard_unview(name, a):
    if name == "w_in":
        return a.transpose(1, 2, 0)
    return a.swapaxes(1, 2) if name in TRANSPOSED else a


def _pad_head_rows(w, width):
    c = w.shape[1]
    return jnp.pad(w.reshape(HEADS, width, c), ((0, 0), (0, HEAD_PAD - width), (0, 0))).reshape(HEADS * HEAD_PAD, c)


def _unpad_head_rows(w, width):
    c = w.shape[1]
    return w.reshape(HEADS, HEAD_PAD, c)[:, :width].reshape(HEADS * width, c)


O1 = Q_LORA
O2 = O1 + KV_LORA
O3 = O2 + QK_ROPE


def _mixer_weights(gw):
    w_in = gw["w_in"].reshape(IN_WIDTH, D_MODEL)
    zero = lambda n: jnp.zeros((n, D_MODEL), BF16)
    win = jnp.concatenate([w_in[:O2], zero(QK_NOPE), w_in[O2:O3], zero(HEAD_PAD - QK_HEAD), w_in[O3:]], axis=0)
    wq = _pad_head_rows(gw["w_q_up"].reshape(HEADS * QK_HEAD, Q_LORA), QK_HEAD)
    w_kv = gw["w_kv_up"].reshape(HEADS, QK_NOPE + V_HEAD, KV_LORA)
    wk = jnp.pad(w_kv[:, :QK_NOPE], ((0, 0), (0, HEAD_PAD - QK_NOPE), (0, 0))).reshape(HEADS * HEAD_PAD, KV_LORA)
    wv = w_kv[:, QK_NOPE:].reshape(HEADS * V_HEAD, KV_LORA)
    wkv = jnp.concatenate([wk, wv], axis=0)
    out = dict(win=win, wq=wq, wkv=wkv)
    if "w_out" in gw:
        out["wout"] = gw["w_out"].reshape(D_MODEL, D_MODEL)
    return out


def _ffn_weights(gw):
    return dict(wg=gw["w_gate"].reshape(FFN_HIDDEN, D_MODEL), wu=gw["w_up"].reshape(FFN_HIDDEN, D_MODEL),
                wd=gw["w_down"].reshape(FFN_HIDDEN, D_MODEL))


def _layer_small(p, l):
    row = lambda a: a.reshape(1, -1)
    pad_head = lambda g: jnp.pad(g, (0, HEAD_PAD - QK_HEAD)).reshape(1, HEAD_PAD)
    tril = jnp.tril(jnp.ones((CHUNK, CHUNK), F32))
    wsp = p["w_spatial"][l] * tril
    wcat = jnp.concatenate([wsp[h] for h in range(HEADS)], axis=1).astype(BF16)
    wcat_t = jnp.concatenate([wsp[h].T for h in range(HEADS)], axis=1).astype(BF16)
    bz = jnp.repeat(p["b_spatial"][l].T, SGU_HEAD_DIM, axis=1)
    wp = jax.scipy.linalg.block_diag(*[p["w_pool"][l][g] for g in range(HEADS)]).astype(BF16)
    return dict(gmix=row(p["g_mix_norm"][l]), gql=row(p["g_q_lat"][l]), gkvl=row(p["g_kv_lat"][l]),
                gqh=pad_head(p["g_q_head"][l]), gkh=pad_head(p["g_k_head"][l]), gsgu=row(p["g_sgu_v"][l]),
                wcat=wcat, wcat_t=wcat_t, bz=bz, wp=wp, pscale=row(p["pool_scale"][l]),
                goa=row(p["g_out_mla"][l]), gos=row(p["g_out_sgu"][l]), gop=row(p["g_out_pool"][l]),
                gffn=row(p["g_ffn_norm"][l]))


MIXER_SIDE = ("w_in", "w_q_up", "w_kv_up", "w_out")
FFN_SIDE = ("w_gate", "w_up", "w_down")
TRANSPOSED = ("w_in", "w_q_up", "w_kv_up", "w_gate", "w_up")
FIRST_0 = ("w_in", "w_q_up", "w_kv_up")
LATER_0 = ("w_out",) + FFN_SIDE
SHARDED = MIXER_SIDE + FFN_SIDE
SMALL = ("g_mix_norm", "g_q_lat", "g_kv_lat", "g_q_head", "g_k_head", "g_sgu_v", "w_spatial", "b_spatial", "w_pool",
         "pool_scale", "g_out_mla", "g_out_sgu", "g_out_pool", "g_ffn_norm")
WEIGHTS = ("g_mix_norm", "w_in", "g_q_lat", "w_q_up", "g_kv_lat", "w_kv_up", "g_q_head", "g_k_head", "g_sgu_v", "w_spatial",
           "b_spatial", "w_pool", "pool_scale", "g_out_mla", "g_out_sgu", "g_out_pool", "w_out", "g_ffn_norm", "w_gate",
           "w_up", "w_down")
PACKED = SMALL + ("loss",)
PACK_ROWS = 8 * 128


def _pack_small(parts):
    flat = []
    for name in PACKED:
        a = parts[name].reshape(-1)
        flat.append(jnp.pad(a, (0, -a.shape[0] % PACK_ROWS)))
    return jnp.concatenate(flat).reshape(-1, 128)


def _unpack_small(packed, like):
    out, row = {}, 0
    for name in PACKED:
        n = math.prod(like[name].shape)
        rows = -(-n // PACK_ROWS) * 8
        out[name] = packed[row:row + rows].reshape(-1)[:n].reshape(like[name].shape)
        row += rows
    return out


def _forward_attention(x, rope, W, S):
    z, q, k, v, kt, vt = mixin_fwd(x, rope, S["gmix"], W["win"], S["gql"], W["wq"], S["gkvl"], W["wkv"], S["gqh"], S["gkh"])
    a, lse = attn_fwd(q, k, vt)
    return dict(x=x, z=z, q=q, k=k, kt=kt, v=v, a=a, lse=lse)


def _forward_mixers(A, wout, S):
    A["x1"] = mixers_fwd(A["x"], A["a"], A["z"], S["gsgu"], S["wcat"], S["bz"], S["wp"], S["pscale"], S["goa"], S["gos"],
                         S["gop"], wout)
    return A["x1"]


def _backward_ffn(dx2, W, S, A, l):
    dx1, h2, act, dg, du_ffn, dgffn = ffn_bwd(dx2, A["x1"], A["gs"], A["us"], S["gffn"], W["wg"], W["wu"], W["wd"])
    d_wd = matmul_tn(act, dx2, f"dw_down_{l}")
    d_wg = matmul_tn(dg, h2, f"dw_gate_{l}")
    d_wu = matmul_tn(du_ffn, h2, f"dw_up_{l}")
    big = {n: d.reshape(N_DEV, -1, D_MODEL) for n, d in (("w_gate", d_wg), ("w_up", d_wu), ("w_down", d_wd))}
    return dx1, big, dgffn


def _backward_mixers(dx1, dgffn, rope, W, S, A, l, send_wout=None):
    (da, delta, du, dvs, dm, mix, dgoa, dgos, dgop, dps, dwp, dwsp, db, dgsgu) = mixers_bwd(
        dx1, A["a"], A["z"], S["gsgu"], S["wcat"], S["wcat_t"], S["bz"], S["wp"], S["pscale"], S["goa"], S["gos"],
        S["gop"], W["wout"])
    d_wout = matmul_tn(mix, dx1, f"dw_out_{l}")
    if send_wout is not None:
        delta = delta + send_wout(d_wout.reshape(N_DEV, -1, D_MODEL))[0, 0]
    dpin = pool_bwd(dm)
    dqt, dk, dv = attn_bwd(A["q"], A["k"], A["kt"], A["v"], da, A["lse"], delta)
    (dx, hn, dz, qn, dq_up, kvn, dkv_up, dgmix, dgql, dgkvl, dgqh, dgkh) = mixin_bwd(
        dx1, A["x"], A["z"], rope, dqt, dk, dv, du, dvs, dpin, S["gmix"], W["win"], S["gql"], W["wq"], S["gkvl"],
        W["wkv"], S["gqh"], S["gkh"])
    d_win = matmul_tn(dz, hn, f"dw_in_{l}")
    d_wq = matmul_tn(dq_up, qn, f"dw_q_up_{l}")
    d_wkv = matmul_tn(dkv_up, kvn, f"dw_kv_up_{l}")
    d_win = jnp.concatenate([d_win[:O2], d_win[O2 + QK_NOPE:O2 + QK_HEAD], d_win[O2 + HEAD_PAD:]], axis=0)
    d_wk = d_wkv[:HEADS * HEAD_PAD].reshape(HEADS, HEAD_PAD, KV_LORA)[:, :QK_NOPE]
    d_wv = d_wkv[HEADS * HEAD_PAD:].reshape(HEADS, V_HEAD, KV_LORA)
    d_wkv = jnp.concatenate([d_wk, d_wv], axis=1)
    big = dict(w_in=d_win.reshape(N_DEV, -1, D_MODEL), w_q_up=_unpad_head_rows(d_wq, QK_HEAD).reshape(N_DEV, -1, Q_LORA),
               w_kv_up=d_wkv.reshape(N_DEV, -1, KV_LORA), w_out=d_wout.reshape(N_DEV, -1, D_MODEL))
    tril = jnp.tril(jnp.ones((CHUNK, CHUNK), F32))
    small = dict(g_mix_norm=dgmix[0], g_q_lat=dgql[0], g_kv_lat=dgkvl[0], g_q_head=dgqh[0, :QK_HEAD], g_k_head=dgkh[0, :QK_HEAD],
                 g_sgu_v=dgsgu[0], w_spatial=dwsp.reshape(HEADS, CHUNK, CHUNK) * tril, b_spatial=db[:, :HEADS].T,
                 w_pool=jnp.stack([dwp[64 * g:64 * (g + 1), 64 * g:64 * (g + 1)] for g in range(HEADS)]),
                 pool_scale=dps[0], g_out_mla=dgoa[0], g_out_sgu=dgos[0], g_out_pool=dgop[0], g_ffn_norm=dgffn[0])
    return dx, big, small


def kernel(x, positions, g_mix_norm, w_in, g_q_lat, w_q_up, g_kv_lat, w_kv_up, g_q_head, g_k_head, g_sgu_v, w_spatial, b_spatial, w_pool, pool_scale, g_out_mla, g_out_sgu, g_out_pool, w_out, g_ffn_norm, w_gate, w_up, w_down, loss_target, m_g_mix_norm, m_w_in, m_g_q_lat, m_w_q_up, m_g_kv_lat, m_w_kv_up, m_g_q_head, m_g_k_head, m_g_sgu_v, m_w_spatial, m_b_spatial, m_w_pool, m_pool_scale, m_g_out_mla, m_g_out_sgu, m_g_out_pool, m_w_out, m_g_ffn_norm, m_w_gate, m_w_up, m_w_down, v_g_mix_norm, v_w_in, v_g_q_lat, v_w_q_up, v_g_kv_lat, v_w_kv_up, v_g_q_head, v_g_k_head, v_g_sgu_v, v_w_spatial, v_b_spatial, v_w_pool, v_pool_scale, v_g_out_mla, v_g_out_sgu, v_g_out_pool, v_w_out, v_g_ffn_norm, v_w_gate, v_w_up, v_w_down):
    given = dict(locals())
    w = {n: given[n] for n in WEIGHTS}
    m = {n: given["m_" + n] for n in WEIGHTS}
    v = {n: given["v_" + n] for n in WEIGHTS}
    T = x.shape[1]
    xs = x.reshape(T, D_MODEL)

    half = QK_ROPE // 2
    inv_freq = 1.0 / (ROPE_THETA ** (jnp.arange(half, dtype=F32) / half))
    ang16 = positions.reshape(T).astype(F32)[:, None] * inv_freq
    ang = jnp.concatenate([jnp.zeros((T, QK_NOPE), F32), ang16, ang16, jnp.zeros((T, HEAD_PAD - QK_HEAD), F32)], axis=1)

    wv = {n: _shard_view(n, w[n]) for n in SHARDED}
    wanted = [(SHARDED.index(n), 0) for n in FIRST_0 + LATER_0] + [(i, 1) for i in range(len(SHARDED))]
    bf = cast_shards([wv[n] for n in SHARDED], wanted)
    nf, nl = len(FIRST_0), len(LATER_0)
    ag_first0 = exchange_start(bf[:nf], ang, True, "ag_start_first0")
    rope = rope_tables(ang + ag_first0[-1][0, 0])
    W0 = dict(zip(FIRST_0, exchange_wait(ag_first0, rope[0], "ag_wait_first0")))
    ag_rest = exchange_start(bf[nf:], W0["w_in"], True, "ag_start_rest")
    ag_later0, ag_l1 = [tuple(part[a:b] for part in ag_rest[:4]) + (ag_rest[-1],) for a, b in ((0, nl), (nl, None))]
    Ss = [_layer_small(w, l) for l in range(DEPTH)]

    S0 = dict(Ss[0], gmix=Ss[0]["gmix"] + ag_rest[-1][0, 0])
    A0 = _forward_attention(xs, rope, _mixer_weights(W0), S0)
    W0.update(zip(LATER_0, exchange_wait(ag_later0, A0["a"], "ag_wait_later0")))
    W0 = {**_mixer_weights(W0), **_ffn_weights(W0)}
    x1 = _forward_mixers(A0, W0["wout"], Ss[0])
    h, A0["gs"], A0["us"] = ffn_fwd(x1, Ss[0]["gffn"], W0["wg"], W0["wu"], W0["wd"])
    layer1 = dict(zip(SHARDED, exchange_wait(ag_l1, h, "ag_wait_l1")))
    W1 = {**_mixer_weights(layer1), **_ffn_weights(layer1)}
    A1 = _forward_attention(h, rope, W1, Ss[1])
    x1 = _forward_mixers(A1, W1["wout"], Ss[1])
    dh, loss_part, A1["gs"], A1["us"] = ffn_fwd(x1, Ss[1]["gffn"], W1["wg"], W1["wu"], W1["wd"],
                                                target=loss_target.reshape(T, D_MODEL))

    dx1, big_f1, dgffn1 = _backward_ffn(dh, W1, Ss[1], A1, 1)
    dh, big_m1, small1 = _backward_mixers(dx1, dgffn1, rope, W1, Ss[1], A1, 1)
    rs_l1 = exchange_start([{**big_m1, **big_f1}[n] for n in SHARDED], dh, False, "rs_start_l1")
    S0 = dict(Ss[0], gffn=Ss[0]["gffn"] + rs_l1[-1][0, 0])
    dx1, big_f0, dgffn0 = _backward_ffn(dh, W0, S0, A0, 0)
    rs_ffn0 = exchange_start([big_f0[n] for n in FFN_SIDE], dx1, False, "rs_start_ffn0")
    S0 = dict(Ss[0], gsgu=Ss[0]["gsgu"] + rs_ffn0[-1][0, 0])
    rs_out0 = []

    def send_wout(pieces):
        rs_out0.append(exchange_start([pieces], dx1, False, "rs_start_out0"))
        return rs_out0[0][-1]

    dh, big_m0, small0 = _backward_mixers(dx1, dgffn0, rope, W0, S0, A0, 0, send_wout)
    grad_x = dh.reshape(x.shape)
    smalls = [small0, small1]

    rs_first0 = exchange_start([big_m0[n] for n in FIRST_0], dh, False, "rs_start_first0")
    small_part = {n: jnp.stack([smalls[l][n] for l in range(DEPTH)]) for n in SMALL}
    small_part["loss"] = loss_part
    ag_small = exchange_start([_pack_small(small_part)], rs_first0[-1], True, "ag_start_small")
    parts1 = dict(zip(SHARDED, exchange_wait(rs_l1, ag_small[-1], "rs_wait_l1")))
    parts0 = dict(zip(FFN_SIDE, exchange_wait(rs_ffn0, ag_small[-1], "rs_wait_ffn0")))
    parts0["w_out"] = exchange_wait(rs_out0[0], ag_small[-1], "rs_wait_out0")[0]
    grad, delta, new_m, new_v = {}, {}, {}, {}

    for n in LATER_0 + FIRST_0:
        if n == FIRST_0[0]:
            parts0.update(zip(FIRST_0, exchange_wait(rs_first0, grad["w_down"], "rs_wait_first0")))
        parts, views = [parts0[n], parts1[n]], [_shard_view(n, a[n]) for a in (w, m, v)]
        if n == "w_in":
            parts = [jnp.stack(parts, axis=2).reshape(N_DEV, -1, D_MODEL)]
            views = [a.reshape(1, -1, D_MODEL) for a in views]
        outs = adamw_sharded(parts, *views, f"adamw_{n}")
        if n == "w_in":
            outs = [o.reshape(-1, DEPTH, D_MODEL) for o in outs]
        grad[n], delta[n], new_m[n], new_v[n] = [_shard_unview(n, o) for o in outs]

    no_loss = {"loss": jnp.zeros((1, 1), F32)}
    gparts = exchange_wait(ag_small, outs[0], "ag_wait_small")[0]
    for d, o in zip((grad, delta, new_m, new_v), adamw_small(gparts, *[_pack_small({**a, **no_loss}) for a in (w, m, v)])):
        d.update(_unpack_small(o, {**w, **no_loss}))
    return (grad["loss"][0, 0], grad_x, *[grad[n] for n in WEIGHTS], *[delta[n] for n in WEIGHTS], *[new_m[n] for n in WEIGHTS],
            *[new_v[n] for n in WEIGHTS])
```
